```python
import math
import jax, jax.numpy as jnp
from jax import lax
import numpy as np

D_MODEL = 1024
BATCH = 8
SEQ = 2048
DEPTH = 1
DEC_BATCH = 128
DEC_SEQ = 8
PAST_LEN = 16384
PAGE_SIZE = 128

D_CONV = D_MODEL // 2
CONV_WIDTH = 3
GLA_HEADS = 4
GLA_DK = D_MODEL // 2 // GLA_HEADS
GLA_DV = D_MODEL // GLA_HEADS
GLA_RANK = 16
GLA_GATE_NORM = 16.0
GLA_CHUNK = 32
N_GROUPS = 4
EXPERTS_PER_GROUP = 8
N_EXPERTS = N_GROUPS * EXPERTS_PER_GROUP
TOP_K = 2
D_EXPERT = 256
EPS = 1e-6

SPLIT_SIZES = (D_CONV, D_CONV, D_CONV,
               GLA_HEADS * GLA_DK, GLA_HEADS * GLA_DK,
               GLA_HEADS * GLA_DV, GLA_HEADS * GLA_DV,
               GLA_RANK, D_MODEL, D_MODEL)
D_IN = sum(SPLIT_SIZES)

kernel_name = "hybrid_conv_gla_hmoe_step"


def rmsnorm(x, g):
    xf = x.astype(jnp.float32)
    y = xf * lax.rsqrt(jnp.mean(xf * xf, axis=-1, keepdims=True) + EPS)
    return (y * g.astype(jnp.float32)).astype(x.dtype)


def short_conv(u, conv_state, w_conv):
    full = jnp.concatenate([conv_state.astype(u.dtype), u], axis=1)
    rhs = w_conv.astype(u.dtype)[:, None, :]
    z = lax.conv_general_dilated(full, rhs, window_strides=(1,), padding='VALID',
                                 dimension_numbers=('NWC', 'WIO', 'NWC'),
                                 feature_group_count=u.shape[-1])
    return z, full[:, -(CONV_WIDTH - 1):]


def gla(q, k, v, log_a, s0):
    B, L = q.shape[:2]
    c = L if L <= GLA_CHUNK else math.gcd(L, GLA_CHUNK)
    n = L // c
    f32 = jnp.float32
    blk = lambda t: t.astype(f32).reshape(B, n, c, *t.shape[2:])
    qc = blk(q) * (GLA_DK ** -0.5)
    kc, vc, ac = blk(k), blk(v), blk(log_a)
    b = jnp.cumsum(ac, axis=2)
    b_last = b[:, :, -1]
    q_in = qc * jnp.exp(b)
    k_in = kc * jnp.exp(-b)
    k_st = kc * jnp.exp(b_last[:, :, None] - b)
    scores = jnp.einsum('bnthd,bnshd->bnhts', q_in, k_in)
    mask = jnp.tril(jnp.ones((c, c), dtype=bool))
    scores = jnp.where(mask, scores, 0.0)
    o_intra = jnp.einsum('bnhts,bnshv->bnthv', scores, vc)

    def step(S, inp):
        q_i, k_i, v_i, d_i = inp
        o = jnp.einsum('bthd,bhdv->bthv', q_i, S)
        S = S * jnp.exp(d_i)[..., None] + jnp.einsum('bthd,bthv->bhdv', k_i, v_i)
        return S, o

    xs = tuple(jnp.moveaxis(t, 1, 0) for t in (q_in, k_st, vc, b_last))
    S_final, o_inter = lax.scan(step, s0.astype(f32), xs)
    o = o_intra + jnp.moveaxis(o_inter, 0, 1)
    return o.reshape(B, L, GLA_HEADS, GLA_DV), S_final


def mixer(xn, conv_state, gla_state, w_in, w_conv, w_alpha_up, b_alpha, gla_norm,
          w_conv_out, w_gla_out, w_o):
    B, L = xn.shape[:2]
    splits = np.cumsum(SPLIT_SIZES)[:-1].tolist()
    cb, cc, ch, q, k, v, g, a_lr, ga, gb = jnp.split(xn @ w_in, splits, axis=-1)
    z, conv_new = short_conv(cc * ch, conv_state, w_conv)
    y_a = (cb * z) @ w_conv_out
    hd = lambda t, d: t.reshape(B, L, GLA_HEADS, d)
    log_a = jax.nn.log_sigmoid((a_lr @ w_alpha_up + b_alpha).astype(jnp.float32)) / GLA_GATE_NORM
    o, S = gla(hd(q, GLA_DK), hd(k, GLA_DK), hd(v, GLA_DV), hd(log_a, GLA_DK), gla_state)
    o = rmsnorm(o, gla_norm) * jax.nn.silu(hd(g, GLA_DV).astype(jnp.float32))
    y_b = o.astype(xn.dtype).reshape(B, L, GLA_HEADS * GLA_DV) @ w_gla_out
    m = jax.nn.sigmoid(ga) * y_a + jax.nn.sigmoid(gb) * y_b
    return m @ w_o, conv_new, S.astype(gla_state.dtype)


def hmoe(xn, w_group_router, b_group_router, w_expert_router, b_expert_router, w_gate, w_up, w_down):
    B, L, D = xn.shape
    xt = xn.reshape(B * L, D)
    T = xt.shape[0]
    g_prob = jax.nn.softmax((xt @ w_group_router).astype(jnp.float32) + b_group_router.astype(jnp.float32), axis=-1)
    g_top, g_idx = lax.top_k(g_prob, 1)
    e_logits = ((xt @ w_expert_router).astype(jnp.float32)
                + b_expert_router.astype(jnp.float32)).reshape(T, N_GROUPS, EXPERTS_PER_GROUP)
    e_sel = jnp.take_along_axis(e_logits, g_idx[:, :, None], axis=1)[:, 0]
    e_top, e_idx = lax.top_k(jax.nn.softmax(e_sel, axis=-1), TOP_K)
    weights = g_top * (e_top / jnp.sum(e_top, axis=-1, keepdims=True))
    expert_id = g_idx * EXPERTS_PER_GROUP + e_idx
    combine = jnp.einsum('tk,tke->te', weights,
                         jax.nn.one_hot(expert_id, N_EXPERTS, dtype=jnp.float32)).astype(xn.dtype)
    y = jnp.zeros((T, D), dtype=xn.dtype)
    for gi in range(N_GROUPS):
        sl = slice(gi * EXPERTS_PER_GROUP, (gi + 1) * EXPERTS_PER_GROUP)
        hg = jax.nn.silu(jnp.einsum('td,edh->teh', xt, w_gate[sl])) * jnp.einsum('td,edh->teh', xt, w_up[sl])
        y = y + jnp.einsum('teh,ehd->td', hg * combine[:, sl, None], w_down[sl])
    return y.reshape(B, L, D)


def trunk(x, conv_states, gla_states, norm_mix, w_in, w_conv, w_alpha_up, b_alpha, gla_norm,
          w_conv_out, w_gla_out, w_o, norm_ffn, w_group_router, b_group_router,
          w_expert_router, b_expert_router, w_gate, w_up, w_down, norm_final):
    h = x
    new_conv, new_gla = [], []
    for l in range(DEPTH):
        m, c_new, s_new = mixer(rmsnorm(h, norm_mix[l]), conv_states[l], gla_states[l], w_in[l], w_conv[l],
                                w_alpha_up[l], b_alpha[l], gla_norm[l], w_conv_out[l], w_gla_out[l], w_o[l])
        h = h + m
        h = h + hmoe(rmsnorm(h, norm_ffn[l]), w_group_router[l], b_group_router[l], w_expert_router[l],
                     b_expert_router[l], w_gate[l], w_up[l], w_down[l])
        new_conv.append(c_new)
        new_gla.append(s_new)
    return rmsnorm(h, norm_final), jnp.stack(new_conv), jnp.stack(new_gla)


def setup_inputs(seed: int = 0) -> dict:
    key = jax.random.key(seed)
    ks = jax.random.split(key, 24)
    nrm = lambda k, shape, s: jax.random.normal(k, shape, jnp.float32) * s
    return {
        "x_prompt": nrm(ks[0], (BATCH, SEQ, D_MODEL), 1.0),
        "x_sample": nrm(ks[1], (DEC_BATCH, DEC_SEQ, D_MODEL), 1.0),
        "state_conv": nrm(ks[2], (DEPTH, DEC_BATCH, CONV_WIDTH - 1, D_CONV), 1.0),
        "state_gla": nrm(ks[3], (DEPTH, DEC_BATCH, GLA_HEADS, GLA_DK, GLA_DV), 1.0),
        "norm_mix": 1.0 + nrm(ks[4], (DEPTH, D_MODEL), 0.02),
        "w_in": nrm(ks[5], (DEPTH, D_MODEL, D_IN), D_MODEL ** -0.5),
        "w_conv": nrm(ks[6], (DEPTH, CONV_WIDTH, D_CONV), CONV_WIDTH ** -0.5),
        "w_alpha_up": nrm(ks[7], (DEPTH, GLA_RANK, GLA_HEADS * GLA_DK), GLA_RANK ** -0.5),
        "b_alpha": nrm(ks[8], (DEPTH, GLA_HEADS * GLA_DK), 0.1),
        "gla_norm": 1.0 + nrm(ks[9], (DEPTH, GLA_DV), 0.02),
        "w_conv_out": nrm(ks[10], (DEPTH, D_CONV, D_MODEL), D_CONV ** -0.5),
        "w_gla_out": nrm(ks[11], (DEPTH, GLA_HEADS * GLA_DV, D_MODEL), (GLA_HEADS * GLA_DV) ** -0.5),
        "w_o": nrm(ks[12], (DEPTH, D_MODEL, D_MODEL), D_MODEL ** -0.5),
        "norm_ffn": 1.0 + nrm(ks[13], (DEPTH, D_MODEL), 0.02),
        "w_group_router": nrm(ks[14], (DEPTH, D_MODEL, N_GROUPS), D_MODEL ** -0.5),
        "b_group_router": nrm(ks[15], (DEPTH, N_GROUPS), 0.01),
        "w_expert_router": nrm(ks[16], (DEPTH, D_MODEL, N_EXPERTS), D_MODEL ** -0.5),
        "b_expert_router": nrm(ks[17], (DEPTH, N_EXPERTS), 0.01),
        "w_gate": nrm(ks[18], (DEPTH, N_EXPERTS, D_MODEL, D_EXPERT), D_MODEL ** -0.5),
        "w_up": nrm(ks[19], (DEPTH, N_EXPERTS, D_MODEL, D_EXPERT), D_MODEL ** -0.5),
        "w_down": nrm(ks[20], (DEPTH, N_EXPERTS, D_EXPERT, D_MODEL), D_EXPERT ** -0.5),
        "norm_final": 1.0 + nrm(ks[21], (D_MODEL,), 0.02),
    }


def reference(x_prompt, x_sample, state_conv, state_gla, norm_mix, w_in, w_conv, w_alpha_up, b_alpha,
              gla_norm, w_conv_out, w_gla_out, w_o, norm_ffn, w_group_router, b_group_router,
              w_expert_router, b_expert_router, w_gate, w_up, w_down, norm_final):
    weights = (norm_mix, w_in, w_conv, w_alpha_up, b_alpha, gla_norm, w_conv_out, w_gla_out, w_o,
               norm_ffn, w_group_router, b_group_router, w_expert_router, b_expert_router,
               w_gate, w_up, w_down, norm_final)
    b_p = x_prompt.shape[0]
    conv0 = jnp.zeros((DEPTH, b_p, CONV_WIDTH - 1, D_CONV), dtype=x_prompt.dtype)
    gla0 = jnp.zeros((DEPTH, b_p, GLA_HEADS, GLA_DK, GLA_DV), dtype=x_prompt.dtype)
    y_prompt, conv_prompt_new, gla_prompt_new = trunk(x_prompt, conv0, gla0, *weights)
    y_sample, conv_sample_new, gla_sample_new = trunk(x_sample, state_conv, state_gla, *weights)
    return (y_prompt, y_sample, conv_prompt_new, gla_prompt_new, conv_sample_new, gla_sample_new)
```

```python
import functools

import jax
import jax.numpy as jnp
from jax import lax
from jax.experimental import pallas as pl
from jax.experimental.pallas import tpu as pltpu

F32 = jnp.float32
BF16 = jnp.bfloat16

D_MODEL = 1024
D_CONV = 512
CONV_WIDTH = 3
GLA_HEADS = 4
GLA_DK = 128
GLA_DV = 256
GLA_RANK = 16
GLA_GATE_NORM = 16.0
GLA_CHUNK = 32
N_GROUPS = 4
EXPERTS_PER_GROUP = 8
N_EXPERTS = 32
D_EXPERT = 256
EPS = 1e-6

LANES = 128
OFF_CB, OFF_CC, OFF_CH = 0, 512, 1024
OFF_Q, OFF_K, OFF_V, OFF_G = 1536, 2048, 2560, 3584
OFF_GA, OFF_GB, OFF_AL = 4608, 5632, 6656
D_IN_PACKED = OFF_AL + LANES

PROMPT_TILE = 256
SAMPLE_SEQS = 8
ROW_TILE = 256
GATHER_ROWS = 256
COMBINE_TILE = 512
RANK_TILE = 512
VMEM_LIMIT = 56 * 1024 * 1024


def _rmsnorm(x, g):
    ms = jnp.mean(x * x, axis=-1, keepdims=True)
    return (x * lax.rsqrt(ms + EPS)) * g


def _split_bf16(x):
    hi = x.astype(BF16)
    lo = (x - hi.astype(F32)).astype(BF16)
    return hi, lo


def _dot(a, b):
    return jnp.dot(a, b, preferred_element_type=F32)


def _chunk_tri(n, chunk):
    r = lax.broadcasted_iota(jnp.int32, (n, n), 0)
    c = lax.broadcasted_iota(jnp.int32, (n, n), 1)
    shift = chunk.bit_length() - 1
    assert chunk == 1 << shift
    same = lax.shift_right_arithmetic(r, shift) == lax.shift_right_arithmetic(c, shift)
    return jnp.where(same & (c <= r), 1.0, 0.0).astype(BF16)


def _gla_block(qb, kb, ksb, vb, s_prev, dcol, causal):
    sc = lax.dot_general(qb, kb, (((1,), (1,)), ((), ())), preferred_element_type=F32)
    sc = jnp.where(causal, sc, 0.0).astype(BF16)
    o = _dot(sc, vb) + _dot(qb, s_prev.astype(BF16))
    upd = lax.dot_general(ksb, vb, (((0,), (0,)), ((), ())), preferred_element_type=F32)
    return o, s_prev * dcol + upd


def _route(xn2, wrh_ref, wrl_ref, br_ref):
    rows = xn2.shape[0]
    xh, xl = _split_bf16(xn2)
    lg = _dot(xh, wrh_ref[...]) + _dot(xl, wrh_ref[...]) + _dot(xh, wrl_ref[...]) + br_ref[...]
    lane_i = lax.broadcasted_iota(jnp.int32, (rows, LANES), 1)
    lane = lane_i.astype(F32)
    neg = -jnp.inf
    is_g = lane_i < N_GROUPS
    gm = jnp.max(jnp.where(is_g, lg, neg), axis=1, keepdims=True)
    gs = jnp.sum(jnp.where(is_g, jnp.exp(lg - gm), 0.0), axis=1, keepdims=True)
    g_top = 1.0 / gs
    g_idx = jnp.min(jnp.where(is_g & (lg == gm), lane, float(LANES)), axis=1, keepdims=True)
    e_group = lax.shift_right_arithmetic(lane_i - N_GROUPS, 3).astype(F32)
    sel = (lane_i >= N_GROUPS) & (lane_i < N_GROUPS + N_EXPERTS) & (e_group == g_idx)
    el = jnp.where(sel, lg, neg)
    e1 = jnp.max(el, axis=1, keepdims=True)
    i1 = jnp.min(jnp.where(sel & (el == e1), lane, float(LANES)), axis=1, keepdims=True)
    sel2 = sel & (lane != i1)
    el2 = jnp.where(sel2, lg, neg)
    e2 = jnp.max(el2, axis=1, keepdims=True)
    i2 = jnp.min(jnp.where(sel2 & (el2 == e2), lane, float(LANES)), axis=1, keepdims=True)
    d = jnp.exp(e2 - e1)
    w1 = g_top / (1.0 + d)
    w2 = (g_top * d) / (1.0 + d)
    id1 = i1 - float(N_GROUPS)
    id2 = i2 - float(N_GROUPS)
    return jnp.where(lane_i == 0, w1, jnp.where(lane_i == 1, w2, jnp.where(lane_i == 2, id1,
                     jnp.where(lane_i == 3, id2, 0.0))))


def _mixer_kernel(x_ref, conv_in_ref, gla_in_ref,
                  nmix_ref, w_ref, wconv_ref, walpha_ref, balpha_ref, gnorm_ref,
                  wco_ref, wgo_ref, wo_ref, nffn_ref, wrh_ref, wrl_ref, br_ref,
                  h_ref, xn2_ref, route_ref, conv_out_ref, gla_out_ref,
                  s_ref, ubuf_ref, obuf_ref, *, prompt):
    if prompt:
        rows, chunk, nseq = PROMPT_TILE, GLA_CHUNK, 1
        x = x_ref[0]

        @pl.when(pl.program_id(1) == 0)
        def _():
            s_ref[...] = jnp.zeros_like(s_ref)
            ubuf_ref[...] = jnp.zeros_like(ubuf_ref)
    else:
        nseq, chunk = SAMPLE_SEQS, x_ref.shape[1]
        rows = nseq * chunk
        x = x_ref[...].reshape(rows, D_MODEL)
    nchunks = rows // chunk

    xn = _rmsnorm(x, nmix_ref[...]).astype(BF16)

    def proj(off, n):
        return _dot(xn, w_ref[:, off:off + n])

    u = proj(OFF_CC, D_CONV) * proj(OFF_CH, D_CONV)
    wc = wconv_ref[...]
    if prompt:
        ubuf_ref[0, 8:8 + rows, :] = u
        z = (wc[0:1] * ubuf_ref[0, 6:6 + rows, :] + wc[1:2] * ubuf_ref[0, 7:7 + rows, :]
             + wc[2:3] * u)
        tail = u[rows - 2:rows, :]
        ubuf_ref[0, 6:8, :] = tail
        conv_out_ref[0] = tail
    else:
        u3 = u.reshape(nseq, chunk, D_CONV)
        ubuf_ref[:, 6:8, :] = conv_in_ref[...]
        ubuf_ref[:, 8:8 + chunk, :] = u3
        z3 = (wc[0:1] * ubuf_ref[:, 6:6 + chunk, :] + wc[1:2] * ubuf_ref[:, 7:7 + chunk, :]
              + wc[2:3] * u3)
        z = z3.reshape(rows, D_CONV)
        conv_out_ref[...] = ubuf_ref[:, 6 + chunk:8 + chunk, :]
    ya = _dot((proj(OFF_CB, D_CONV) * z).astype(BF16), wco_ref[...])

    alr = proj(OFF_AL, LANES).astype(BF16)
    la = jax.nn.log_sigmoid(_dot(alr, walpha_ref[...]) + balpha_ref[...]) * (1.0 / GLA_GATE_NORM)
    tri = _chunk_tri(rows, chunk)
    la_hi, la_lo = _split_bf16(la)
    b = _dot(tri, la_hi) + _dot(tri, la_lo)
    b3 = b.reshape(nchunks, chunk, D_CONV)
    bl3 = b3[:, chunk - 1:chunk, :]
    q = proj(OFF_Q, D_CONV)
    k = proj(OFF_K, D_CONV)
    qin = (q * (GLA_DK ** -0.5)) * jnp.exp(b)
    kin = k * jnp.exp(-b)
    kst = (k.reshape(nchunks, chunk, D_CONV) * jnp.exp(bl3 - b3)).reshape(rows, D_CONV)
    v = proj(OFF_V, GLA_HEADS * GLA_DV)
    if rows % LANES:
        b_pad = jnp.concatenate([b, jnp.zeros((LANES - rows % LANES, D_CONV), F32)], axis=0)
    else:
        b_pad = b
    b_t = b_pad.T
    ri = lax.broadcasted_iota(jnp.int32, (chunk, chunk), 0)
    ci = lax.broadcasted_iota(jnp.int32, (chunk, chunk), 1)
    causal = ci <= ri
    for n in range(nchunks):
        r0 = n * chunk
        for hd in range(GLA_HEADS):
            c0, v0 = hd * GLA_DK, hd * GLA_DV
            qb = qin[r0:r0 + chunk, c0:c0 + GLA_DK].astype(BF16)
            kb = kin[r0:r0 + chunk, c0:c0 + GLA_DK].astype(BF16)
            ksb = kst[r0:r0 + chunk, c0:c0 + GLA_DK].astype(BF16)
            vb = v[r0:r0 + chunk, v0:v0 + GLA_DV].astype(BF16)
            dcol = jnp.exp(b_t[c0:c0 + GLA_DK, r0 + chunk - 1:r0 + chunk])
            s_prev = s_ref[hd] if prompt else gla_in_ref[n, hd]
            o, s_new = _gla_block(qb, kb, ksb, vb, s_prev, dcol, causal)
            obuf_ref[r0:r0 + chunk, v0:v0 + GLA_DV] = o
            if prompt:
                s_ref[hd] = s_new
            else:
                gla_out_ref[n, hd] = s_new
    if prompt:
        gla_out_ref[0] = s_ref[...]

    g = proj(OFF_G, GLA_HEADS * GLA_DV)
    gated = []
    for hd in range(GLA_HEADS):
        v0 = hd * GLA_DV
        oh = _rmsnorm(obuf_ref[:, v0:v0 + GLA_DV], gnorm_ref[...])
        gh = g[:, v0:v0 + GLA_DV]
        gated.append(oh * (gh * jax.nn.sigmoid(gh)))
    yb = _dot(jnp.concatenate(gated, axis=1).astype(BF16), wgo_ref[...])

    m = (jax.nn.sigmoid(proj(OFF_GA, D_MODEL)) * ya + jax.nn.sigmoid(proj(OFF_GB, D_MODEL)) * yb)
    hh = x + _dot(m.astype(BF16), wo_ref[...])

    xn2 = _rmsnorm(hh, nffn_ref[...])
    route = _route(xn2, wrh_ref, wrl_ref, br_ref)
    if prompt:
        h_ref[0] = hh
        xn2_ref[0] = xn2
        route_ref[0] = route
    else:
        h_ref[...] = hh.reshape(nseq, chunk, D_MODEL)
        xn2_ref[...] = xn2.reshape(nseq, chunk, D_MODEL)
        route_ref[...] = route.reshape(nseq, chunk, LANES)


def _const_spec(shape):
    nd = len(shape)
    return pl.BlockSpec(shape, lambda *_: (0,) * nd, pipeline_mode=pl.Buffered(1))


def _mixer_call(x, conv_state, gla_state, weights, *, prompt):
    nb, seq, _ = x.shape
    if prompt:
        grid = (nb, seq // PROMPT_TILE)
        tok = lambda last: pl.BlockSpec((1, PROMPT_TILE, last), lambda b, i: (b, i, 0))
        conv_spec = pl.BlockSpec((1, CONV_WIDTH - 1, D_CONV), lambda b, i: (b, 0, 0))
        gla_spec = pl.BlockSpec((1, GLA_HEADS, GLA_DK, GLA_DV), lambda b, i: (b, 0, 0, 0))
        gla_in_spec = pl.BlockSpec((1, 1, 8, LANES), lambda b, i: (b, 0, 0, 0))
        scratch = [pltpu.VMEM((GLA_HEADS, GLA_DK, GLA_DV), F32),
                   pltpu.VMEM((1, PROMPT_TILE + 8, D_CONV), F32),
                   pltpu.VMEM((PROMPT_TILE, GLA_HEADS * GLA_DV), F32)]
    else:
        grid = (nb // SAMPLE_SEQS, 1)
        tok = lambda last: pl.BlockSpec((SAMPLE_SEQS, seq, last), lambda b, i: (b, 0, 0))
        conv_spec = pl.BlockSpec((SAMPLE_SEQS, CONV_WIDTH - 1, D_CONV), lambda b, i: (b, 0, 0))
        gla_spec = pl.BlockSpec((SAMPLE_SEQS, GLA_HEADS, GLA_DK, GLA_DV), lambda b, i: (b, 0, 0, 0))
        gla_in_spec = gla_spec
        scratch = [pltpu.VMEM((1, 8, LANES), F32),
                   pltpu.VMEM((SAMPLE_SEQS, 8 + seq, D_CONV), F32),
                   pltpu.VMEM((SAMPLE_SEQS * seq, GLA_HEADS * GLA_DV), F32)]
    out_shape = (jax.ShapeDtypeStruct((nb, seq, D_MODEL), F32),
                 jax.ShapeDtypeStruct((nb, seq, D_MODEL), F32),
                 jax.ShapeDtypeStruct((nb, seq, LANES), F32),
                 jax.ShapeDtypeStruct((nb, CONV_WIDTH - 1, D_CONV), F32),
                 jax.ShapeDtypeStruct((nb, GLA_HEADS, GLA_DK, GLA_DV), F32))
    return pl.pallas_call(
        functools.partial(_mixer_kernel, prompt=prompt),
        grid=grid,
        in_specs=[tok(D_MODEL), conv_spec, gla_in_spec] + [_const_spec(w.shape) for w in weights],
        out_specs=(tok(D_MODEL), tok(D_MODEL), tok(LANES), conv_spec, gla_spec),
        out_shape=out_shape,
        scratch_shapes=scratch,
        compiler_params=pltpu.CompilerParams(
            dimension_semantics=("arbitrary", "arbitrary"), vmem_limit_bytes=VMEM_LIMIT),
        name="mixer_prompt" if prompt else "mixer_sample",
    )(x, conv_state, gla_state, *weights)


def _rank_kernel(eid_ref, rank_ref, cnt_ref, carry_ref):
    @pl.when(pl.program_id(0) == 0)
    def _():
        carry_ref[...] = jnp.zeros_like(carry_ref)

    n = eid_ref.shape[1]
    e_iota = lax.broadcasted_iota(jnp.int32, (N_EXPERTS, n), 0)
    oh0 = jnp.where(e_iota == eid_ref[0:1, :], 1.0, 0.0)
    oh1 = jnp.where(e_iota == eid_ref[1:2, :], 1.0, 0.0)
    cnt = oh0 + oh1
    r = lax.broadcasted_iota(jnp.int32, (n, n), 0)
    c = lax.broadcasted_iota(jnp.int32, (n, n), 1)
    before = jnp.where(r < c, 1.0, 0.0).astype(BF16)
    base = carry_ref[...] + _dot(cnt.astype(BF16), before)
    rank0 = jnp.sum(oh0 * base, axis=0, keepdims=True)
    rank1 = jnp.sum(oh1 * base, axis=0, keepdims=True)
    rank_ref[...] = jnp.concatenate([rank0, rank1], axis=0).astype(jnp.int32)
    carry_ref[...] = carry_ref[...] + jnp.sum(cnt, axis=1, keepdims=True)
    cnt_ref[...] = jnp.broadcast_to(carry_ref[...], cnt_ref.shape)


def _rank_call(eid):
    ntok = eid.shape[1]
    return pl.pallas_call(
        _rank_kernel,
        grid=(ntok // RANK_TILE,),
        in_specs=[pl.BlockSpec((2, RANK_TILE), lambda i: (0, i))],
        out_specs=(pl.BlockSpec((2, RANK_TILE), lambda i: (0, i)),
                   pl.BlockSpec((N_EXPERTS, LANES), lambda i: (0, 0))),
        out_shape=(jax.ShapeDtypeStruct((2, ntok), jnp.int32),
                   jax.ShapeDtypeStruct((N_EXPERTS, LANES), F32)),
        scratch_shapes=[pltpu.VMEM((N_EXPERTS, 1), F32)],
        compiler_params=pltpu.CompilerParams(dimension_semantics=("arbitrary",)),
        name="expert_rank",
    )(eid)


def _gather_kernel(idx_ref, src_ref, out_ref, sem):
    base = pl.program_id(0) * GATHER_ROWS

    def row_copy(r):
        return pltpu.make_async_copy(src_ref.at[pl.ds(idx_ref[base + r], 1)],
                                     out_ref.at[pl.ds(r, 1)], sem)

    def start(r, carry):
        row_copy(r).start()
        return carry

    def wait(r, carry):
        row_copy(r).wait()
        return carry

    lax.fori_loop(0, GATHER_ROWS, start, 0, unroll=8)
    lax.fori_loop(0, GATHER_ROWS, wait, 0, unroll=8)


def _gather_rows(src, idx):
    nrows, width = idx.shape[0], src.shape[1]
    return pl.pallas_call(
        _gather_kernel,
        grid_spec=pltpu.PrefetchScalarGridSpec(
            num_scalar_prefetch=1,
            grid=(nrows // GATHER_ROWS,),
            in_specs=[pl.BlockSpec(memory_space=pl.ANY)],
            out_specs=pl.BlockSpec((GATHER_ROWS, width), lambda i, idx_ref: (i, 0)),
            scratch_shapes=[pltpu.SemaphoreType.DMA(())]),
        out_shape=jax.ShapeDtypeStruct((nrows, width), src.dtype),
        compiler_params=pltpu.CompilerParams(dimension_semantics=("arbitrary",)),
        name="row_gather",
    )(idx, src)


def _expert_kernel(te_ref, nv_ref, xs_ref, wgu_ref, wdn_ref, out_ref):
    i = pl.program_id(0)

    @pl.when(i < nv_ref[0])
    def _():
        xb = xs_ref[...].astype(BF16)
        gu = _dot(xb, wgu_ref[0])
        gt, up = gu[:, :D_EXPERT], gu[:, D_EXPERT:]
        act = (gt * jax.nn.sigmoid(gt)) * up
        out_ref[...] = _dot(act.astype(BF16), wdn_ref[0])

    @pl.when(i >= nv_ref[0])
    def _():
        out_ref[...] = jnp.zeros_like(out_ref)


def _expert_call(tile_expert, n_valid, xs, wgu, wdn):
    nrows = xs.shape[0]
    return pl.pallas_call(
        _expert_kernel,
        grid_spec=pltpu.PrefetchScalarGridSpec(
            num_scalar_prefetch=2,
            grid=(nrows // ROW_TILE,),
            in_specs=[pl.BlockSpec((ROW_TILE, D_MODEL), lambda i, te, nv: (i, 0)),
                      pl.BlockSpec((1, D_MODEL, 2 * D_EXPERT), lambda i, te, nv: (te[i], 0, 0)),
                      pl.BlockSpec((1, D_EXPERT, D_MODEL), lambda i, te, nv: (te[i], 0, 0))],
            out_specs=pl.BlockSpec((ROW_TILE, D_MODEL), lambda i, te, nv: (i, 0))),
        out_shape=jax.ShapeDtypeStruct((nrows, D_MODEL), F32),
        compiler_params=pltpu.CompilerParams(
            dimension_semantics=("arbitrary",), vmem_limit_bytes=VMEM_LIMIT),
        name="expert_mlp",
    )(tile_expert, n_valid, xs, wgu, wdn)


def _combine_kernel(h_ref, y2_ref, route_ref, nfin_ref, out_ref):
    route = route_ref[...]
    w1, w2 = route[:, 0:1], route[:, 1:2]
    y = h_ref[...] + (w1 * y2_ref[0] + w2 * y2_ref[1])
    out_ref[...] = _rmsnorm(y, nfin_ref[...])


def _combine_call(h, y2, route, nfin):
    ntok = h.shape[0]
    return pl.pallas_call(
        _combine_kernel,
        grid=(ntok // COMBINE_TILE,),
        in_specs=[pl.BlockSpec((COMBINE_TILE, D_MODEL), lambda i: (i, 0)),
                  pl.BlockSpec((2, COMBINE_TILE, D_MODEL), lambda i: (0, i, 0)),
                  pl.BlockSpec((COMBINE_TILE, LANES), lambda i: (i, 0)),
                  pl.BlockSpec((1, D_MODEL), lambda i: (0, 0))],
        out_specs=pl.BlockSpec((COMBINE_TILE, D_MODEL), lambda i: (i, 0)),
        out_shape=jax.ShapeDtypeStruct((ntok, D_MODEL), F32),
        compiler_params=pltpu.CompilerParams(
            dimension_semantics=("arbitrary",), vmem_limit_bytes=VMEM_LIMIT),
        name="moe_combine",
    )(h, y2, route, nfin)


def kernel(x_prompt, x_sample, state_conv, state_gla, norm_mix, w_in, w_conv, w_alpha_up, b_alpha,
           gla_norm, w_conv_out, w_gla_out, w_o, norm_ffn, w_group_router, b_group_router,
           w_expert_router, b_expert_router, w_gate, w_up, w_down, norm_final):
    nbp, seq_p, _ = x_prompt.shape
    nbs, seq_s, _ = x_sample.shape
    assert norm_mix.shape[0] == 1, "single layer"
    assert seq_p % PROMPT_TILE == 0 and nbs % SAMPLE_SEQS == 0 and seq_s == 8

    wi = w_in[0]
    al0 = OFF_GA + 0
    w_pack = jnp.concatenate(
        [wi[:, :al0], wi[:, al0 + GLA_RANK:], wi[:, al0:al0 + GLA_RANK],
         jnp.zeros((D_MODEL, LANES - GLA_RANK), F32)], axis=1).astype(BF16)
    walpha = jnp.concatenate(
        [w_alpha_up[0], jnp.zeros((LANES - GLA_RANK, D_CONV), F32)], axis=0).astype(BF16)
    wr = jnp.concatenate(
        [w_group_router[0], w_expert_router[0],
         jnp.zeros((D_MODEL, LANES - N_GROUPS - N_EXPERTS), F32)], axis=1)
    wr_hi = wr.astype(BF16)
    wr_lo = (wr - wr_hi.astype(F32)).astype(BF16)
    br = jnp.concatenate([b_group_router[0], b_expert_router[0],
                          jnp.zeros((LANES - N_GROUPS - N_EXPERTS,), F32)])[None, :]
    weights = (norm_mix, w_pack, w_conv[0], walpha, b_alpha, gla_norm,
               w_conv_out[0].astype(BF16), w_gla_out[0].astype(BF16), w_o[0].astype(BF16),
               norm_ffn, wr_hi, wr_lo, br)
    wgu = jnp.concatenate([w_gate[0], w_up[0]], axis=-1).astype(BF16)
    wdn = w_down[0].astype(BF16)

    conv0 = jnp.zeros((nbp, CONV_WIDTH - 1, D_CONV), F32)
    gla0 = jnp.zeros((nbp, 1, 8, LANES), F32)
    h_p, xn_p, rt_p, conv_p, gla_p = _mixer_call(x_prompt, conv0, gla0, weights, prompt=True)
    h_s, xn_s, rt_s, conv_s, gla_s = _mixer_call(x_sample, state_conv[0], state_gla[0], weights,
                                                 prompt=False)
    ntp, nts = nbp * seq_p, nbs * seq_s
    ntok = ntp + nts
    h = jnp.concatenate([h_p.reshape(ntp, D_MODEL), h_s.reshape(nts, D_MODEL)], axis=0)
    xn2 = jnp.concatenate([xn_p.reshape(ntp, D_MODEL), xn_s.reshape(nts, D_MODEL)], axis=0)
    route = jnp.concatenate([rt_p.reshape(ntp, LANES), rt_s.reshape(nts, LANES)], axis=0)

    eid = route[:, 2:4].astype(jnp.int32).T
    rank, cnt = _rank_call(eid)
    counts = cnt[:, 0].astype(jnp.int32)
    padded = ((counts + ROW_TILE - 1) // ROW_TILE) * ROW_TILE
    ends = jnp.cumsum(padded)
    starts = ends - padded
    pos = starts[eid] + rank
    n_tiles = (2 * ntok + N_EXPERTS * (ROW_TILE - 1)) // ROW_TILE
    n_rows = n_tiles * ROW_TILE
    tok_ids = jnp.tile(jnp.arange(ntok, dtype=jnp.int32), 2)
    src_tok = jnp.zeros((n_rows,), jnp.int32).at[pos.reshape(-1)].set(tok_ids)
    tile_start = jnp.arange(n_tiles, dtype=jnp.int32) * ROW_TILE
    tile_expert = jnp.minimum(
        jnp.sum((tile_start[:, None] >= ends[None, :]).astype(jnp.int32), axis=1), N_EXPERTS - 1)
    n_valid = (ends[-1] // ROW_TILE).astype(jnp.int32)[None]
    last_valid_expert = tile_expert[jnp.maximum(n_valid[0] - 1, 0)]
    tile_expert = jnp.where(tile_start < ends[-1], tile_expert, last_valid_expert)

    xs = _gather_rows(xn2, src_tok)
    ys = _expert_call(tile_expert, n_valid, xs, wgu, wdn)
    y2 = _gather_rows(ys, pos.reshape(-1)).reshape(2, ntok, D_MODEL)
    y = _combine_call(h, y2, route, norm_final[None, :])

    y_prompt = y[:ntp].reshape(nbp, seq_p, D_MODEL)
    y_sample = y[ntp:].reshape(nbs, seq_s, D_MODEL)
    return (y_prompt, y_sample, conv_p[None], gla_p[None], conv_s[None], gla_s[None])
```

```python
import functools

import jax
import jax.numpy as jnp
from jax import lax
from jax.experimental import pallas as pl
from jax.experimental.pallas import tpu as pltpu
from jax.experimental.pallas import tpu_sc as plsc

F32 = jnp.float32
BF16 = jnp.bfloat16

D_MODEL = 1024
D_CONV = 512
CONV_WIDTH = 3
GLA_HEADS = 4
GLA_DK = 128
GLA_DV = 256
GLA_RANK = 16
GLA_GATE_NORM = 16.0
GLA_CHUNK = 32
N_GROUPS = 4
EXPERTS_PER_GROUP = 8
N_EXPERTS = 32
D_EXPERT = 256
EPS = 1e-6

LANES = 128
OFF_CB, OFF_CC, OFF_CH = 0, 512, 1024
OFF_Q, OFF_K, OFF_V, OFF_G = 1536, 2048, 2560, 3584
OFF_GA, OFF_GB, OFF_AL = 4608, 5632, 6656
D_IN_PACKED = OFF_AL + LANES

PROMPT_TILE = 256
SAMPLE_SEQS = 8
ROW_TILE = 256
SC_WINDOW = 128
COMBINE_TILE = 512
RANK_TILE = 512
VMEM_LIMIT = 56 * 1024 * 1024


def _rmsnorm(x, g):
    ms = jnp.mean(x * x, axis=-1, keepdims=True)
    return (x * lax.rsqrt(ms + EPS)) * g


def _split_bf16(x):
    hi = x.astype(BF16)
    lo = (x - hi.astype(F32)).astype(BF16)
    return hi, lo


def _pack_bf16_pairs(x):
    n = x.shape[1] // 2
    lo = lax.bitcast_convert_type(x[:, :n].astype(BF16).astype(F32), jnp.uint32)
    hi = lax.bitcast_convert_type(x[:, n:].astype(BF16).astype(F32), jnp.uint32)
    return lax.shift_right_logical(lo, jnp.uint32(16)) | hi


def _unpack_bf16_pairs(w):
    lo = lax.bitcast_convert_type(lax.shift_left(w, jnp.uint32(16)), F32)
    hi = lax.bitcast_convert_type(w & jnp.uint32(0xFFFF0000), F32)
    return lo, hi


def _dot(a, b):
    return jnp.dot(a, b, preferred_element_type=F32)


def _chunk_tri(n, chunk):
    r = lax.broadcasted_iota(jnp.int32, (n, n), 0)
    c = lax.broadcasted_iota(jnp.int32, (n, n), 1)
    shift = chunk.bit_length() - 1
    assert chunk == 1 << shift
    same = lax.shift_right_arithmetic(r, shift) == lax.shift_right_arithmetic(c, shift)
    return jnp.where(same & (c <= r), 1.0, 0.0).astype(BF16)


def _gla_block(qb, kb, ksb, vb, s_prev, dcol, causal):
    sc = lax.dot_general(qb, kb, (((1,), (1,)), ((), ())), preferred_element_type=F32)
    sc = jnp.where(causal, sc, 0.0).astype(BF16)
    o = _dot(sc, vb) + _dot(qb, s_prev.astype(BF16))
    upd = lax.dot_general(ksb, vb, (((0,), (0,)), ((), ())), preferred_element_type=F32)
    return o, s_prev * dcol + upd


def _route(xn2, wrh_ref, wrl_ref, br_ref):
    rows = xn2.shape[0]
    xh, xl = _split_bf16(xn2)
    lg = _dot(xh, wrh_ref[...]) + _dot(xl, wrh_ref[...]) + _dot(xh, wrl_ref[...]) + br_ref[...]
    lane_i = lax.broadcasted_iota(jnp.int32, (rows, LANES), 1)
    lane = lane_i.astype(F32)
    neg = -jnp.inf
    is_g = lane_i < N_GROUPS
    gm = jnp.max(jnp.where(is_g, lg, neg), axis=1, keepdims=True)
    gs = jnp.sum(jnp.where(is_g, jnp.exp(lg - gm), 0.0), axis=1, keepdims=True)
    g_top = 1.0 / gs
    g_idx = jnp.min(jnp.where(is_g & (lg == gm), lane, float(LANES)), axis=1, keepdims=True)
    e_group = lax.shift_right_arithmetic(lane_i - N_GROUPS, 3).astype(F32)
    sel = (lane_i >= N_GROUPS) & (lane_i < N_GROUPS + N_EXPERTS) & (e_group == g_idx)
    el = jnp.where(sel, lg, neg)
    e1 = jnp.max(el, axis=1, keepdims=True)
    i1 = jnp.min(jnp.where(sel & (el == e1), lane, float(LANES)), axis=1, keepdims=True)
    sel2 = sel & (lane != i1)
    el2 = jnp.where(sel2, lg, neg)
    e2 = jnp.max(el2, axis=1, keepdims=True)
    i2 = jnp.min(jnp.where(sel2 & (el2 == e2), lane, float(LANES)), axis=1, keepdims=True)
    d = jnp.exp(e2 - e1)
    w1 = g_top / (1.0 + d)
    w2 = (g_top * d) / (1.0 + d)
    id1 = i1 - float(N_GROUPS)
    id2 = i2 - float(N_GROUPS)
    return jnp.where(lane_i == 0, w1, jnp.where(lane_i == 1, w2, jnp.where(lane_i == 2, id1,
                     jnp.where(lane_i == 3, id2, 0.0))))


def _mixer_kernel(x_ref, conv_in_ref, gla_in_ref,
                  nmix_ref, w_ref, wconv_ref, walpha_ref, balpha_ref, gnorm_ref,
                  wco_ref, wgo_ref, wo_ref, nffn_ref, wrh_ref, wrl_ref, br_ref,
                  h_ref, xn2_ref, route_ref, conv_out_ref, gla_out_ref,
                  s_ref, ubuf_ref, obuf_ref, *, prompt):
    if prompt:
        rows, chunk, nseq = PROMPT_TILE, GLA_CHUNK, 1
        x = x_ref[0]

        @pl.when(pl.program_id(1) == 0)
        def _():
            s_ref[...] = jnp.zeros_like(s_ref)
            ubuf_ref[...] = jnp.zeros_like(ubuf_ref)
    else:
        nseq, chunk = SAMPLE_SEQS, x_ref.shape[1]
        rows = nseq * chunk
        x = x_ref[...].reshape(rows, D_MODEL)
    nchunks = rows // chunk

    xn = _rmsnorm(x, nmix_ref[...]).astype(BF16)

    def proj(off, n):
        return _dot(xn, w_ref[:, off:off + n])

    u = proj(OFF_CC, D_CONV) * proj(OFF_CH, D_CONV)
    wc = wconv_ref[...]
    if prompt:
        ubuf_ref[0, 8:8 + rows, :] = u
        z = (wc[0:1] * ubuf_ref[0, 6:6 + rows, :] + wc[1:2] * ubuf_ref[0, 7:7 + rows, :]
             + wc[2:3] * u)
        tail = u[rows - 2:rows, :]
        ubuf_ref[0, 6:8, :] = tail
        conv_out_ref[0] = tail
    else:
        u3 = u.reshape(nseq, chunk, D_CONV)
        ubuf_ref[:, 6:8, :] = conv_in_ref[...]
        ubuf_ref[:, 8:8 + chunk, :] = u3
        z3 = (wc[0:1] * ubuf_ref[:, 6:6 + chunk, :] + wc[1:2] * ubuf_ref[:, 7:7 + chunk, :]
              + wc[2:3] * u3)
        z = z3.reshape(rows, D_CONV)
        conv_out_ref[...] = ubuf_ref[:, 6 + chunk:8 + chunk, :]
    ya = _dot((proj(OFF_CB, D_CONV) * z).astype(BF16), wco_ref[...])

    alr = proj(OFF_AL, LANES).astype(BF16)
    la = jax.nn.log_sigmoid(_dot(alr, walpha_ref[...]) + balpha_ref[...]) * (1.0 / GLA_GATE_NORM)
    tri = _chunk_tri(rows, chunk)
    la_hi, la_lo = _split_bf16(la)
    b = _dot(tri, la_hi) + _dot(tri, la_lo)
    b3 = b.reshape(nchunks, chunk, D_CONV)
    bl3 = b3[:, chunk - 1:chunk, :]
    q = proj(OFF_Q, D_CONV)
    k = proj(OFF_K, D_CONV)
    qin = (q * (GLA_DK ** -0.5)) * jnp.exp(b)
    kin = k * jnp.exp(-b)
    kst = (k.reshape(nchunks, chunk, D_CONV) * jnp.exp(bl3 - b3)).reshape(rows, D_CONV)
    v = proj(OFF_V, GLA_HEADS * GLA_DV)
    if rows % LANES:
        b_pad = jnp.concatenate([b, jnp.zeros((LANES - rows % LANES, D_CONV), F32)], axis=0)
    else:
        b_pad = b
    b_t = b_pad.T
    ri = lax.broadcasted_iota(jnp.int32, (chunk, chunk), 0)
    ci = lax.broadcasted_iota(jnp.int32, (chunk, chunk), 1)
    causal = ci <= ri
    for n in range(nchunks):
        r0 = n * chunk
        for hd in range(GLA_HEADS):
            c0, v0 = hd * GLA_DK, hd * GLA_DV
            qb = qin[r0:r0 + chunk, c0:c0 + GLA_DK].astype(BF16)
            kb = kin[r0:r0 + chunk, c0:c0 + GLA_DK].astype(BF16)
            ksb = kst[r0:r0 + chunk, c0:c0 + GLA_DK].astype(BF16)
            vb = v[r0:r0 + chunk, v0:v0 + GLA_DV].astype(BF16)
            dcol = jnp.exp(b_t[c0:c0 + GLA_DK, r0 + chunk - 1:r0 + chunk])
            s_prev = s_ref[hd] if prompt else gla_in_ref[n, hd]
            o, s_new = _gla_block(qb, kb, ksb, vb, s_prev, dcol, causal)
            obuf_ref[r0:r0 + chunk, v0:v0 + GLA_DV] = o
            if prompt:
                s_ref[hd] = s_new
            else:
                gla_out_ref[n, hd] = s_new
    if prompt:
        gla_out_ref[0] = s_ref[...]

    g = proj(OFF_G, GLA_HEADS * GLA_DV)
    gated = []
    for hd in range(GLA_HEADS):
        v0 = hd * GLA_DV
        oh = _rmsnorm(obuf_ref[:, v0:v0 + GLA_DV], gnorm_ref[...])
        gh = g[:, v0:v0 + GLA_DV]
        gated.append(oh * (gh * jax.nn.sigmoid(gh)))
    yb = _dot(jnp.concatenate(gated, axis=1).astype(BF16), wgo_ref[...])

    m = (jax.nn.sigmoid(proj(OFF_GA, D_MODEL)) * ya + jax.nn.sigmoid(proj(OFF_GB, D_MODEL)) * yb)
    hh = x + _dot(m.astype(BF16), wo_ref[...])

    xn2 = _rmsnorm(hh, nffn_ref[...])
    route = _route(xn2, wrh_ref, wrl_ref, br_ref)
    if prompt:
        h_ref[0] = hh
        xn2_ref[0] = _pack_bf16_pairs(xn2)
        route_ref[0] = route
    else:
        h_ref[...] = hh.reshape(nseq, chunk, D_MODEL)
        xn2_ref[...] = _pack_bf16_pairs(xn2).reshape(nseq, chunk, D_MODEL // 2)
        route_ref[...] = route.reshape(nseq, chunk, LANES)


def _const_spec(shape):
    nd = len(shape)
    return pl.BlockSpec(shape, lambda *_: (0,) * nd, pipeline_mode=pl.Buffered(1))


def _mixer_call(x, conv_state, gla_state, weights, *, prompt):
    nb, seq, _ = x.shape
    if prompt:
        grid = (nb, seq // PROMPT_TILE)
        tok = lambda last: pl.BlockSpec((1, PROMPT_TILE, last), lambda b, i: (b, i, 0))
        conv_spec = pl.BlockSpec((1, CONV_WIDTH - 1, D_CONV), lambda b, i: (b, 0, 0))
        gla_spec = pl.BlockSpec((1, GLA_HEADS, GLA_DK, GLA_DV), lambda b, i: (b, 0, 0, 0))
        gla_in_spec = pl.BlockSpec((1, 1, 8, LANES), lambda b, i: (b, 0, 0, 0))
        scratch = [pltpu.VMEM((GLA_HEADS, GLA_DK, GLA_DV), F32),
                   pltpu.VMEM((1, PROMPT_TILE + 8, D_CONV), F32),
                   pltpu.VMEM((PROMPT_TILE, GLA_HEADS * GLA_DV), F32)]
    else:
        grid = (nb // SAMPLE_SEQS, 1)
        tok = lambda last: pl.BlockSpec((SAMPLE_SEQS, seq, last), lambda b, i: (b, 0, 0))
        conv_spec = pl.BlockSpec((SAMPLE_SEQS, CONV_WIDTH - 1, D_CONV), lambda b, i: (b, 0, 0))
        gla_spec = pl.BlockSpec((SAMPLE_SEQS, GLA_HEADS, GLA_DK, GLA_DV), lambda b, i: (b, 0, 0, 0))
        gla_in_spec = gla_spec
        scratch = [pltpu.VMEM((1, 8, LANES), F32),
                   pltpu.VMEM((SAMPLE_SEQS, 8 + seq, D_CONV), F32),
                   pltpu.VMEM((SAMPLE_SEQS * seq, GLA_HEADS * GLA_DV), F32)]
    out_shape = (jax.ShapeDtypeStruct((nb, seq, D_MODEL), F32),
                 jax.ShapeDtypeStruct((nb, seq, D_MODEL // 2), jnp.uint32),
                 jax.ShapeDtypeStruct((nb, seq, LANES), F32),
                 jax.ShapeDtypeStruct((nb, CONV_WIDTH - 1, D_CONV), F32),
                 jax.ShapeDtypeStruct((nb, GLA_HEADS, GLA_DK, GLA_DV), F32))
    return pl.pallas_call(
        functools.partial(_mixer_kernel, prompt=prompt),
        grid=grid,
        in_specs=[tok(D_MODEL), conv_spec, gla_in_spec] + [_const_spec(w.shape) for w in weights],
        out_specs=(tok(D_MODEL), tok(D_MODEL // 2), tok(LANES), conv_spec, gla_spec),
        out_shape=out_shape,
        scratch_shapes=scratch,
        compiler_params=pltpu.CompilerParams(
            dimension_semantics=("arbitrary", "arbitrary"), vmem_limit_bytes=VMEM_LIMIT),
        name="mixer_prompt" if prompt else "mixer_sample",
    )(x, conv_state, gla_state, *weights)


def _rank_kernel(eid_ref, rank_ref, cnt_ref, carry_ref):
    @pl.when(pl.program_id(0) == 0)
    def _():
        carry_ref[...] = jnp.zeros_like(carry_ref)

    n = eid_ref.shape[1]
    e_iota = lax.broadcasted_iota(jnp.int32, (N_EXPERTS, n), 0)
    oh0 = jnp.where(e_iota == eid_ref[0:1, :], 1.0, 0.0)
    oh1 = jnp.where(e_iota == eid_ref[1:2, :], 1.0, 0.0)
    cnt = oh0 + oh1
    r = lax.broadcasted_iota(jnp.int32, (n, n), 0)
    c = lax.broadcasted_iota(jnp.int32, (n, n), 1)
    before = jnp.where(r < c, 1.0, 0.0).astype(BF16)
    base = carry_ref[...] + _dot(cnt.astype(BF16), before)
    rank0 = jnp.sum(oh0 * base, axis=0, keepdims=True)
    rank1 = jnp.sum(oh1 * base, axis=0, keepdims=True)
    rank_ref[...] = jnp.concatenate([rank0, rank1], axis=0).astype(jnp.int32)
    carry_ref[...] = carry_ref[...] + jnp.sum(cnt, axis=1, keepdims=True)
    cnt_ref[...] = jnp.broadcast_to(carry_ref[...], cnt_ref.shape)


def _rank_call(eid):
    ntok = eid.shape[1]
    return pl.pallas_call(
        _rank_kernel,
        grid=(ntok // RANK_TILE,),
        in_specs=[pl.BlockSpec((2, RANK_TILE), lambda i: (0, i))],
        out_specs=(pl.BlockSpec((2, RANK_TILE), lambda i: (0, i)),
                   pl.BlockSpec((N_EXPERTS, LANES), lambda i: (0, 0))),
        out_shape=(jax.ShapeDtypeStruct((2, ntok), jnp.int32),
                   jax.ShapeDtypeStruct((N_EXPERTS, LANES), F32)),
        scratch_shapes=[pltpu.VMEM((N_EXPERTS, 1), F32)],
        compiler_params=pltpu.CompilerParams(dimension_semantics=("arbitrary",)),
        name="expert_rank",
    )(eid)


def _sc_mesh():
    return plsc.VectorSubcoreMesh(core_axis_name="c", subcore_axis_name="s")


def _sc_worker_windows(n_windows, body):
    info = plsc.get_sparse_core_info()
    n_workers = info.num_cores * info.num_subcores
    wid = lax.axis_index("s") * info.num_cores + lax.axis_index("c")

    @pl.loop(0, pl.cdiv(n_windows, n_workers))
    def _(j):
        win = wid + n_workers * j

        @pl.when(win < n_windows)
        def _():
            body(win)


def _dispatch_rows(x_prompt_rows, x_sample_rows, pos, n_rows):
    ntp, nts = x_prompt_rows.shape[0], x_sample_rows.shape[0]
    width, dtype = x_prompt_rows.shape[1], x_prompt_rows.dtype
    nwp, nws = ntp // SC_WINDOW, nts // SC_WINDOW
    pos3 = pos.reshape(2, nwp + nws, SC_WINDOW).transpose(1, 0, 2)

    @functools.partial(pl.kernel, mesh=_sc_mesh(), name="moe_dispatch",
                       out_type=jax.ShapeDtypeStruct((n_rows, width), dtype),
                       scratch_types=[pltpu.VMEM((2, SC_WINDOW), jnp.int32),
                                      pltpu.VMEM((SC_WINDOW, width), dtype)])
    def run(xp_hbm, xs_hbm, pos_hbm, out_hbm, idx_v, buf):
        def window(win):
            pltpu.sync_copy(pos_hbm.at[win], idx_v)

            @pl.when(win < nwp)
            def _():
                r0 = pl.multiple_of(win * SC_WINDOW, SC_WINDOW)
                pltpu.sync_copy(xp_hbm.at[pl.ds(r0, SC_WINDOW)], buf)

            @pl.when(win >= nwp)
            def _():
                r0 = pl.multiple_of((win - nwp) * SC_WINDOW, SC_WINDOW)
                pltpu.sync_copy(xs_hbm.at[pl.ds(r0, SC_WINDOW)], buf)

            pltpu.sync_copy(buf, out_hbm.at[idx_v.at[0]])
            pltpu.sync_copy(buf, out_hbm.at[idx_v.at[1]])

        _sc_worker_windows(nwp + nws, window)

    return run(x_prompt_rows, x_sample_rows, pos3)


def _return_rows(ys, idx):
    n_out, width = idx.shape[0], ys.shape[1]
    n_windows = n_out // SC_WINDOW

    @functools.partial(pl.kernel, mesh=_sc_mesh(), name="moe_return",
                       out_type=jax.ShapeDtypeStruct((n_out, width), ys.dtype),
                       scratch_types=[pltpu.VMEM((SC_WINDOW,), jnp.int32),
                                      pltpu.VMEM((SC_WINDOW, width), ys.dtype)])
    def run(ys_hbm, idx_hbm, out_hbm, idx_v, buf):
        def window(win):
            pltpu.sync_copy(idx_hbm.at[win], idx_v)
            pltpu.sync_copy(ys_hbm.at[idx_v], buf)
            r0 = pl.multiple_of(win * SC_WINDOW, SC_WINDOW)
            pltpu.sync_copy(buf, out_hbm.at[pl.ds(r0, SC_WINDOW)])

        _sc_worker_windows(n_windows, window)

    return run(ys, idx.reshape(n_windows, SC_WINDOW))


def _expert_kernel(te_ref, nv_ref, xs_ref, wgu_ref, wdn_ref, out_ref):
    i = pl.program_id(0)

    @pl.when(i < nv_ref[0])
    def _():
        lo, hi = _unpack_bf16_pairs(xs_ref[...])
        half = D_MODEL // 2
        gu = (_dot(lo.astype(BF16), wgu_ref[0, :half, :])
              + _dot(hi.astype(BF16), wgu_ref[0, half:, :]))
        gt, up = gu[:, :D_EXPERT], gu[:, D_EXPERT:]
        act = (gt * jax.nn.sigmoid(gt)) * up
        out_ref[...] = _pack_bf16_pairs(_dot(act.astype(BF16), wdn_ref[0]))

    @pl.when(i >= nv_ref[0])
    def _():
        out_ref[...] = jnp.zeros_like(out_ref)


def _expert_call(tile_expert, n_valid, xs, wgu, wdn):
    nrows = xs.shape[0]
    return pl.pallas_call(
        _expert_kernel,
        grid_spec=pltpu.PrefetchScalarGridSpec(
            num_scalar_prefetch=2,
            grid=(nrows // ROW_TILE,),
            in_specs=[pl.BlockSpec((ROW_TILE, D_MODEL // 2), lambda i, te, nv: (i, 0)),
                      pl.BlockSpec((1, D_MODEL, 2 * D_EXPERT), lambda i, te, nv: (te[i], 0, 0)),
                      pl.BlockSpec((1, D_EXPERT, D_MODEL), lambda i, te, nv: (te[i], 0, 0))],
            out_specs=pl.BlockSpec((ROW_TILE, D_MODEL // 2), lambda i, te, nv: (i, 0))),
        out_shape=jax.ShapeDtypeStruct((nrows, D_MODEL // 2), jnp.uint32),
        compiler_params=pltpu.CompilerParams(
            dimension_semantics=("arbitrary",), vmem_limit_bytes=VMEM_LIMIT),
        name="expert_mlp",
    )(tile_expert, n_valid, xs, wgu, wdn)


def _combine_kernel(h_ref, y2_ref, route_ref, nfin_ref, out_ref):
    route = route_ref[...]
    w1, w2 = route[:, 0:1], route[:, 1:2]
    lo1, hi1 = _unpack_bf16_pairs(y2_ref[0])
    lo2, hi2 = _unpack_bf16_pairs(y2_ref[1])
    y = h_ref[...] + jnp.concatenate([w1 * lo1 + w2 * lo2, w1 * hi1 + w2 * hi2], axis=1)
    out_ref[...] = _rmsnorm(y, nfin_ref[...])


def _combine_call(h, y2, route, nfin, first_token):
    n = h.shape[0]
    first_block = first_token // COMBINE_TILE
    return pl.pallas_call(
        _combine_kernel,
        grid=(n // COMBINE_TILE,),
        in_specs=[pl.BlockSpec((COMBINE_TILE, D_MODEL), lambda i: (i, 0)),
                  pl.BlockSpec((2, COMBINE_TILE, D_MODEL // 2), lambda i: (0, first_block + i, 0)),
                  pl.BlockSpec((COMBINE_TILE, LANES), lambda i: (i, 0)),
                  pl.BlockSpec((1, D_MODEL), lambda i: (0, 0))],
        out_specs=pl.BlockSpec((COMBINE_TILE, D_MODEL), lambda i: (i, 0)),
        out_shape=jax.ShapeDtypeStruct((n, D_MODEL), F32),
        compiler_params=pltpu.CompilerParams(
            dimension_semantics=("arbitrary",), vmem_limit_bytes=VMEM_LIMIT),
        name="moe_combine",
    )(h, y2, route, nfin)


def kernel(x_prompt, x_sample, state_conv, state_gla, norm_mix, w_in, w_conv, w_alpha_up, b_alpha,
           gla_norm, w_conv_out, w_gla_out, w_o, norm_ffn, w_group_router, b_group_router,
           w_expert_router, b_expert_router, w_gate, w_up, w_down, norm_final):
    nbp, seq_p, _ = x_prompt.shape
    nbs, seq_s, _ = x_sample.shape
    assert norm_mix.shape[0] == 1, "single layer"
    assert seq_p % PROMPT_TILE == 0 and nbs % SAMPLE_SEQS == 0 and seq_s == 8

    wi = w_in[0]
    al0 = OFF_GA + 0
    w_pack = jnp.concatenate(
        [wi[:, :al0], wi[:, al0 + GLA_RANK:], wi[:, al0:al0 + GLA_RANK],
         jnp.zeros((D_MODEL, LANES - GLA_RANK), F32)], axis=1).astype(BF16)
    walpha = jnp.concatenate(
        [w_alpha_up[0], jnp.zeros((LANES - GLA_RANK, D_CONV), F32)], axis=0).astype(BF16)
    wr = jnp.concatenate(
        [w_group_router[0], w_expert_router[0],
         jnp.zeros((D_MODEL, LANES - N_GROUPS - N_EXPERTS), F32)], axis=1)
    wr_hi = wr.astype(BF16)
    wr_lo = (wr - wr_hi.astype(F32)).astype(BF16)
    br = jnp.concatenate([b_group_router[0], b_expert_router[0],
                          jnp.zeros((LANES - N_GROUPS - N_EXPERTS,), F32)])[None, :]
    weights = (norm_mix, w_pack, w_conv[0], walpha, b_alpha, gla_norm,
               w_conv_out[0].astype(BF16), w_gla_out[0].astype(BF16), w_o[0].astype(BF16),
               norm_ffn, wr_hi, wr_lo, br)
    wgu = jnp.concatenate([w_gate[0], w_up[0]], axis=-1).astype(BF16)
    wdn = w_down[0].astype(BF16)

    conv0 = jnp.zeros((nbp, CONV_WIDTH - 1, D_CONV), F32)
    gla0 = jnp.zeros((nbp, 1, 8, LANES), F32)
    h_p, xn_p, rt_p, conv_p, gla_p = _mixer_call(x_prompt, conv0, gla0, weights, prompt=True)
    h_s, xn_s, rt_s, conv_s, gla_s = _mixer_call(x_sample, state_conv[0], state_gla[0], weights,
                                                 prompt=False)
    ntp, nts = nbp * seq_p, nbs * seq_s
    ntok = ntp + nts
    assert ntp % COMBINE_TILE == 0 and nts % COMBINE_TILE == 0 and ntok % RANK_TILE == 0
    h_p, xn_p, rt_p = (a.reshape(ntp, a.shape[-1]) for a in (h_p, xn_p, rt_p))
    h_s, xn_s, rt_s = (a.reshape(nts, a.shape[-1]) for a in (h_s, xn_s, rt_s))

    eid = jnp.concatenate([rt_p[:, 2:4], rt_s[:, 2:4]], axis=0).astype(jnp.int32).T
    rank, cnt = _rank_call(eid)
    counts = cnt[:, 0].astype(jnp.int32)
    padded = ((counts + ROW_TILE - 1) // ROW_TILE) * ROW_TILE
    ends = jnp.cumsum(padded)
    starts = ends - padded
    expert_iota = jnp.arange(N_EXPERTS, dtype=jnp.int32)[:, None, None]
    pos = rank + jnp.sum(jnp.where(eid[None] == expert_iota, starts[:, None, None], 0), axis=0)
    n_tiles = (2 * ntok + N_EXPERTS * (ROW_TILE - 1)) // ROW_TILE
    n_rows = n_tiles * ROW_TILE
    tile_start = jnp.arange(n_tiles, dtype=jnp.int32) * ROW_TILE
    tile_expert = jnp.minimum(
        jnp.sum((tile_start[:, None] >= ends[None, :]).astype(jnp.int32), axis=1), N_EXPERTS - 1)
    n_valid = (ends[-1] // ROW_TILE).astype(jnp.int32)[None]
    last_valid_expert = jnp.sum(jnp.where(tile_start == ends[-1] - ROW_TILE, tile_expert, 0))
    tile_expert = jnp.where(tile_start < ends[-1], tile_expert, last_valid_expert)

    xs = _dispatch_rows(xn_p, xn_s, pos, n_rows)
    ys = _expert_call(tile_expert, n_valid, xs, wgu, wdn)
    y2 = _return_rows(ys, pos.reshape(-1)).reshape(2, ntok, D_MODEL // 2)
    nfin = norm_final[None, :]
    y_prompt = _combine_call(h_p, y2, rt_p, nfin, 0).reshape(nbp, seq_p, D_MODEL)
    y_sample = _combine_call(h_s, y2, rt_s, nfin, ntp).reshape(nbs, seq_s, D_MODEL)
    return (y_prompt, y_sample, conv_p[None], gla_p[None], conv_s[None], gla_s[None])
```

```python
import functools

import jax
import jax.numpy as jnp
from jax import lax
from jax.experimental import pallas as pl
from jax.experimental.pallas import tpu as pltpu
from jax.experimental.pallas import tpu_sc as plsc

F32 = jnp.float32
BF16 = jnp.bfloat16

D_MODEL = 1024
D_CONV = 512
CONV_WIDTH = 3
GLA_HEADS = 4
GLA_DK = 128
GLA_DV = 256
GLA_RANK = 16
GLA_GATE_NORM = 16.0
GLA_CHUNK = 32
N_GROUPS = 4
EXPERTS_PER_GROUP = 8
N_EXPERTS = 32
D_EXPERT = 256
EPS = 1e-6

LANES = 128
OFF_CB, OFF_CC, OFF_CH = 0, 512, 1024
OFF_Q, OFF_K, OFF_V, OFF_G = 1536, 2048, 2560, 3584
OFF_GA, OFF_GB, OFF_AL = 4608, 5632, 6656
D_IN_PACKED = OFF_AL + LANES

PROMPT_TILE = 512
GLA_BLOCK = 256
SAMPLE_SEQS = 8
ROW_TILE = 256
SC_WINDOW = 128
COMBINE_TILE = 512
RANK_TILE = 512
VMEM_LIMIT = 56 * 1024 * 1024


def _rmsnorm(x, g):
    ms = jnp.mean(x * x, axis=-1, keepdims=True)
    return (x * lax.rsqrt(ms + EPS)) * g


def _split_bf16(x):
    hi = x.astype(BF16)
    lo = (x - hi.astype(F32)).astype(BF16)
    return hi, lo


def _pack_bf16_pairs(x):
    n = x.shape[1] // 2
    lo = lax.bitcast_convert_type(x[:, :n].astype(BF16).astype(F32), jnp.uint32)
    hi = lax.bitcast_convert_type(x[:, n:].astype(BF16).astype(F32), jnp.uint32)
    return lax.shift_right_logical(lo, jnp.uint32(16)) | hi


def _unpack_bf16_pairs(w):
    lo = lax.bitcast_convert_type(lax.shift_left(w, jnp.uint32(16)), F32)
    hi = lax.bitcast_convert_type(w & jnp.uint32(0xFFFF0000), F32)
    return lo, hi


def _dot(a, b):
    return jnp.dot(a, b, preferred_element_type=F32)


def _chunk_causal(n, chunk):
    r = lax.broadcasted_iota(jnp.int32, (n, n), 0)
    c = lax.broadcasted_iota(jnp.int32, (n, n), 1)
    shift = chunk.bit_length() - 1
    assert chunk == 1 << shift
    same = lax.shift_right_arithmetic(r, shift) == lax.shift_right_arithmetic(c, shift)
    return same & (c <= r)


def _route(xn2, wrh_ref, wrl_ref, br_ref):
    rows = xn2.shape[0]
    xh, xl = _split_bf16(xn2)
    lg = _dot(xh, wrh_ref[...]) + _dot(xl, wrh_ref[...]) + _dot(xh, wrl_ref[...]) + br_ref[...]
    lane_i = lax.broadcasted_iota(jnp.int32, (rows, LANES), 1)
    lane = lane_i.astype(F32)
    neg = -jnp.inf
    is_g = lane_i < N_GROUPS
    gm = jnp.max(jnp.where(is_g, lg, neg), axis=1, keepdims=True)
    gs = jnp.sum(jnp.where(is_g, jnp.exp(lg - gm), 0.0), axis=1, keepdims=True)
    g_top = 1.0 / gs
    g_idx = jnp.min(jnp.where(is_g & (lg == gm), lane, float(LANES)), axis=1, keepdims=True)
    e_group = lax.shift_right_arithmetic(lane_i - N_GROUPS, 3).astype(F32)
    sel = (lane_i >= N_GROUPS) & (lane_i < N_GROUPS + N_EXPERTS) & (e_group == g_idx)
    el = jnp.where(sel, lg, neg)
    e1 = jnp.max(el, axis=1, keepdims=True)
    i1 = jnp.min(jnp.where(sel & (el == e1), lane, float(LANES)), axis=1, keepdims=True)
    sel2 = sel & (lane != i1)
    el2 = jnp.where(sel2, lg, neg)
    e2 = jnp.max(el2, axis=1, keepdims=True)
    i2 = jnp.min(jnp.where(sel2 & (el2 == e2), lane, float(LANES)), axis=1, keepdims=True)
    d = jnp.exp(e2 - e1)
    w1 = g_top / (1.0 + d)
    w2 = (g_top * d) / (1.0 + d)
    id1 = i1 - float(N_GROUPS)
    id2 = i2 - float(N_GROUPS)
    return jnp.where(lane_i == 0, w1, jnp.where(lane_i == 1, w2, jnp.where(lane_i == 2, id1,
                     jnp.where(lane_i == 3, id2, 0.0))))


def _mixer_kernel(x_ref, conv_in_ref, gla_in_ref,
                  nmix_ref, w_ref, wconv_ref, walpha_ref, balpha_ref, gnorm_ref,
                  wco_ref, wgo_ref, wo_ref, nffn_ref, wrh_ref, wrl_ref, br_ref,
                  h_ref, xn2_ref, route_ref, conv_out_ref, gla_out_ref,
                  s_ref, ubuf_ref, *, prompt):
    if prompt:
        rows, chunk, nseq = PROMPT_TILE, GLA_CHUNK, 1
        x = x_ref[0]

        @pl.when(pl.program_id(1) == 0)
        def _():
            s_ref[...] = jnp.zeros_like(s_ref)
            ubuf_ref[...] = jnp.zeros_like(ubuf_ref)
    else:
        nseq, chunk = SAMPLE_SEQS, x_ref.shape[1]
        rows = nseq * chunk
        x = x_ref[...].reshape(rows, D_MODEL)
    nchunks = rows // chunk

    xn = _rmsnorm(x, nmix_ref[...]).astype(BF16)

    def proj(off, n):
        return _dot(xn, w_ref[:, off:off + n])

    alr = proj(OFF_AL, LANES).astype(BF16)
    lap = _dot(alr, walpha_ref[...])
    la = jax.nn.log_sigmoid(lap + balpha_ref[...]) * (1.0 / GLA_GATE_NORM)
    la_hi, la_lo = _split_bf16(la)
    blk = min(rows, GLA_BLOCK)
    nblk = rows // blk
    causal = _chunk_causal(blk, chunk)
    tri = jnp.where(causal, 1.0, 0.0).astype(BF16)
    cc = proj(OFF_CC, D_CONV)
    ch = proj(OFF_CH, D_CONV)
    cb = proj(OFF_CB, D_CONV)
    q = proj(OFF_Q, D_CONV)
    k = proj(OFF_K, D_CONV)
    b = jnp.concatenate([_dot(tri, la_hi[j * blk:(j + 1) * blk]) + _dot(tri, la_lo[j * blk:(j + 1) * blk])
                         for j in range(nblk)], axis=0)

    u = cc * ch
    wc = wconv_ref[...]
    if prompt:
        ubuf_ref[0, 8:8 + rows, :] = u
        z = (wc[0:1] * ubuf_ref[0, 6:6 + rows, :] + wc[1:2] * ubuf_ref[0, 7:7 + rows, :]
             + wc[2:3] * u)
        tail = u[rows - 2:rows, :]
        ubuf_ref[0, 6:8, :] = tail
        conv_out_ref[0] = tail
    else:
        u3 = u.reshape(nseq, chunk, D_CONV)
        ubuf_ref[:, 6:8, :] = conv_in_ref[...]
        ubuf_ref[:, 8:8 + chunk, :] = u3
        z3 = (wc[0:1] * ubuf_ref[:, 6:6 + chunk, :] + wc[1:2] * ubuf_ref[:, 7:7 + chunk, :]
              + wc[2:3] * u3)
        z = z3.reshape(rows, D_CONV)
        conv_out_ref[...] = ubuf_ref[:, 6 + chunk:8 + chunk, :]
    ya = _dot((cb * z).astype(BF16), wco_ref[...])
    v = proj(OFF_V, GLA_HEADS * GLA_DV)

    b3 = b.reshape(nchunks, chunk, D_CONV)
    bl3 = b3[:, chunk - 1:chunk, :]
    qin = (q * (GLA_DK ** -0.5)) * jnp.exp(b)
    kin = k * jnp.exp(-b)
    kst = (k.reshape(nchunks, chunk, D_CONV) * jnp.exp(bl3 - b3)).reshape(rows, D_CONV)
    pad = (-blk) % LANES
    blk_p = blk + pad
    cpb = blk // chunk

    def transposed(a):
        if pad:
            a = jnp.concatenate([a, jnp.zeros((pad, a.shape[1]), F32)], axis=0)
        return a.T

    blocks = range(nblk)
    b_ts = [transposed(b[j * blk:(j + 1) * blk]) for j in blocks]
    kst_ts = [transposed(kst[j * blk:(j + 1) * blk]) for j in blocks]
    col_chunk = lax.shift_right_arithmetic(
        lax.broadcasted_iota(jnp.int32, (GLA_DK, blk_p), 1), chunk.bit_length() - 1)
    g = proj(OFF_G, GLA_HEADS * GLA_DV)
    heads = range(GLA_HEADS)

    def v_of(j, hd):
        return v[j * blk:(j + 1) * blk, hd * GLA_DV:(hd + 1) * GLA_DV].astype(BF16)

    scs = [[lax.dot_general(qin[j * blk:(j + 1) * blk, hd * GLA_DK:(hd + 1) * GLA_DK].astype(BF16),
                            kin[j * blk:(j + 1) * blk, hd * GLA_DK:(hd + 1) * GLA_DK].astype(BF16),
                            (((1,), (1,)), ((), ())), preferred_element_type=F32)
            for hd in heads] for j in blocks]
    ga = proj(OFF_GA, D_MODEL)
    upds = []
    for j in blocks:
        upds.append([])
        for hd in heads:
            kst_h = kst_ts[j][hd * GLA_DK:(hd + 1) * GLA_DK, :]
            stacked = jnp.concatenate(
                [jnp.where(col_chunk == n, kst_h, 0.0).astype(BF16) for n in range(cpb)], axis=0)
            vh = v_of(j, hd)
            vh_p = jnp.concatenate([vh, jnp.zeros((pad, GLA_DV), BF16)], axis=0) if pad else vh
            upds[j].append(_dot(stacked, vh_p))
    o_intras = [[_dot(jnp.where(causal, scs[j][hd], 0.0).astype(BF16), v_of(j, hd))
                 for hd in heads] for j in blocks]
    gb = proj(OFF_GB, D_MODEL)
    merge_a = jax.nn.sigmoid(ga) * ya
    gate_b = jax.nn.sigmoid(gb)
    gated = []
    for hd in heads:
        c0, v0 = hd * GLA_DK, hd * GLA_DV
        s_run = s_ref[hd] if prompt else None
        o_inter = []
        for j in blocks:
            for n in range(cpb):
                r0 = j * blk + n * chunk
                s_prev = s_run if prompt else gla_in_ref[n, hd]
                qb = qin[r0:r0 + chunk, c0:c0 + GLA_DK].astype(BF16)
                o_inter.append(_dot(qb, s_prev.astype(BF16)))
                last = n * chunk + chunk - 1
                dcol = jnp.exp(b_ts[j][c0:c0 + GLA_DK, last:last + 1])
                s_new = s_prev * dcol + upds[j][hd][n * GLA_DK:(n + 1) * GLA_DK, :]
                if prompt:
                    s_run = s_new
                else:
                    gla_out_ref[n, hd] = s_new
        if prompt:
            s_ref[hd] = s_run
            gla_out_ref[0, hd] = s_run
        o_intra = jnp.concatenate([o_intras[j][hd] for j in blocks], axis=0)
        oh = _rmsnorm(o_intra + jnp.concatenate(o_inter, axis=0), gnorm_ref[...])
        gh = g[:, v0:v0 + GLA_DV]
        gated.append(oh * (gh * jax.nn.sigmoid(gh)))
    yb = _dot(jnp.concatenate(gated, axis=1).astype(BF16), wgo_ref[...])

    m = merge_a + gate_b * yb
    hh = x + _dot(m.astype(BF16), wo_ref[...])

    xn2 = _rmsnorm(hh, nffn_ref[...])
    route = _route(xn2, wrh_ref, wrl_ref, br_ref)
    if prompt:
        h_ref[0] = hh
        xn2_ref[0] = _pack_bf16_pairs(xn2)
        route_ref[0] = route
    else:
        h_ref[...] = hh.reshape(nseq, chunk, D_MODEL)
        xn2_ref[...] = _pack_bf16_pairs(xn2).reshape(nseq, chunk, D_MODEL // 2)
        route_ref[...] = route.reshape(nseq, chunk, LANES)


def _const_spec(shape):
    nd = len(shape)
    return pl.BlockSpec(shape, lambda *_: (0,) * nd, pipeline_mode=pl.Buffered(1))


def _mixer_call(x, conv_state, gla_state, weights, *, prompt):
    nb, seq, _ = x.shape
    if prompt:
        grid = (nb, seq // PROMPT_TILE)
        tok = lambda last: pl.BlockSpec((1, PROMPT_TILE, last), lambda b, i: (b, i, 0))
        conv_spec = pl.BlockSpec((1, CONV_WIDTH - 1, D_CONV), lambda b, i: (b, 0, 0))
        gla_spec = pl.BlockSpec((1, GLA_HEADS, GLA_DK, GLA_DV), lambda b, i: (b, 0, 0, 0))
        gla_in_spec = pl.BlockSpec((1, 1, 8, LANES), lambda b, i: (b, 0, 0, 0))
        scratch = [pltpu.VMEM((GLA_HEADS, GLA_DK, GLA_DV), F32),
                   pltpu.VMEM((1, PROMPT_TILE + 8, D_CONV), F32)]
    else:
        grid = (nb // SAMPLE_SEQS, 1)
        tok = lambda last: pl.BlockSpec((SAMPLE_SEQS, seq, last), lambda b, i: (b, 0, 0))
        conv_spec = pl.BlockSpec((SAMPLE_SEQS, CONV_WIDTH - 1, D_CONV), lambda b, i: (b, 0, 0))
        gla_spec = pl.BlockSpec((SAMPLE_SEQS, GLA_HEADS, GLA_DK, GLA_DV), lambda b, i: (b, 0, 0, 0))
        gla_in_spec = gla_spec
        scratch = [pltpu.VMEM((1, 8, LANES), F32),
                   pltpu.VMEM((SAMPLE_SEQS, 8 + seq, D_CONV), F32)]
    out_shape = (jax.ShapeDtypeStruct((nb, seq, D_MODEL), F32),
                 jax.ShapeDtypeStruct((nb, seq, D_MODEL // 2), jnp.uint32),
                 jax.ShapeDtypeStruct((nb, seq, LANES), F32),
                 jax.ShapeDtypeStruct((nb, CONV_WIDTH - 1, D_CONV), F32),
                 jax.ShapeDtypeStruct((nb, GLA_HEADS, GLA_DK, GLA_DV), F32))
    return pl.pallas_call(
        functools.partial(_mixer_kernel, prompt=prompt),
        grid=grid,
        in_specs=[tok(D_MODEL), conv_spec, gla_in_spec] + [_const_spec(w.shape) for w in weights],
        out_specs=(tok(D_MODEL), tok(D_MODEL // 2), tok(LANES), conv_spec, gla_spec),
        out_shape=out_shape,
        scratch_shapes=scratch,
        compiler_params=pltpu.CompilerParams(
            dimension_semantics=("arbitrary", "arbitrary"), vmem_limit_bytes=VMEM_LIMIT),
        name="mixer_prompt" if prompt else "mixer_sample",
    )(x, conv_state, gla_state, *weights)


def _rank_kernel(eid_ref, rank_ref, cnt_ref, carry_ref):
    @pl.when(pl.program_id(0) == 0)
    def _():
        carry_ref[...] = jnp.zeros_like(carry_ref)

    n = eid_ref.shape[1]
    e_iota = lax.broadcasted_iota(jnp.int32, (N_EXPERTS, n), 0)
    oh0 = jnp.where(e_iota == eid_ref[0:1, :], 1.0, 0.0)
    oh1 = jnp.where(e_iota == eid_ref[1:2, :], 1.0, 0.0)
    cnt = oh0 + oh1
    r = lax.broadcasted_iota(jnp.int32, (n, n), 0)
    c = lax.broadcasted_iota(jnp.int32, (n, n), 1)
    before = jnp.where(r < c, 1.0, 0.0).astype(BF16)
    base = carry_ref[...] + _dot(cnt.astype(BF16), before)
    rank0 = jnp.sum(oh0 * base, axis=0, keepdims=True)
    rank1 = jnp.sum(oh1 * base, axis=0, keepdims=True)
    rank_ref[...] = jnp.concatenate([rank0, rank1], axis=0).astype(jnp.int32)
    carry_ref[...] = carry_ref[...] + jnp.sum(cnt, axis=1, keepdims=True)
    cnt_ref[...] = jnp.broadcast_to(carry_ref[...], cnt_ref.shape)


def _rank_call(eid):
    ntok = eid.shape[1]
    return pl.pallas_call(
        _rank_kernel,
        grid=(ntok // RANK_TILE,),
        in_specs=[pl.BlockSpec((2, RANK_TILE), lambda i: (0, i))],
        out_specs=(pl.BlockSpec((2, RANK_TILE), lambda i: (0, i)),
                   pl.BlockSpec((N_EXPERTS, LANES), lambda i: (0, 0))),
        out_shape=(jax.ShapeDtypeStruct((2, ntok), jnp.int32),
                   jax.ShapeDtypeStruct((N_EXPERTS, LANES), F32)),
        scratch_shapes=[pltpu.VMEM((N_EXPERTS, 1), F32)],
        compiler_params=pltpu.CompilerParams(dimension_semantics=("arbitrary",)),
        name="expert_rank",
    )(eid)


def _sc_mesh():
    return plsc.VectorSubcoreMesh(core_axis_name="c", subcore_axis_name="s")


def _sc_worker_windows(n_windows, body):
    info = plsc.get_sparse_core_info()
    n_workers = info.num_cores * info.num_subcores
    wid = lax.axis_index("s") * info.num_cores + lax.axis_index("c")

    @pl.loop(0, pl.cdiv(n_windows, n_workers))
    def _(j):
        win = wid + n_workers * j

        @pl.when(win < n_windows)
        def _():
            body(win)


def _dispatch_rows(x_prompt_rows, x_sample_rows, pos, n_rows):
    ntp, nts = x_prompt_rows.shape[0], x_sample_rows.shape[0]
    width, dtype = x_prompt_rows.shape[1], x_prompt_rows.dtype
    nwp, nws = ntp // SC_WINDOW, nts // SC_WINDOW
    pos3 = pos.reshape(2, nwp + nws, SC_WINDOW).transpose(1, 0, 2)

    @functools.partial(pl.kernel, mesh=_sc_mesh(), name="moe_dispatch",
                       out_type=jax.ShapeDtypeStruct((n_rows, width), dtype),
                       scratch_types=[pltpu.VMEM((2, SC_WINDOW), jnp.int32),
                                      pltpu.VMEM((SC_WINDOW, width), dtype)])
    def run(xp_hbm, xs_hbm, pos_hbm, out_hbm, idx_v, buf):
        def window(win):
            pltpu.sync_copy(pos_hbm.at[win], idx_v)

            @pl.when(win < nwp)
            def _():
                r0 = pl.multiple_of(win * SC_WINDOW, SC_WINDOW)
                pltpu.sync_copy(xp_hbm.at[pl.ds(r0, SC_WINDOW)], buf)

            @pl.when(win >= nwp)
            def _():
                r0 = pl.multiple_of((win - nwp) * SC_WINDOW, SC_WINDOW)
                pltpu.sync_copy(xs_hbm.at[pl.ds(r0, SC_WINDOW)], buf)

            pltpu.sync_copy(buf, out_hbm.at[idx_v.at[0]])
            pltpu.sync_copy(buf, out_hbm.at[idx_v.at[1]])

        _sc_worker_windows(nwp + nws, window)

    return run(x_prompt_rows, x_sample_rows, pos3)


def _return_rows(ys, idx):
    n_out, width = idx.shape[0], ys.shape[1]
    n_windows = n_out // SC_WINDOW

    @functools.partial(pl.kernel, mesh=_sc_mesh(), name="moe_return",
                       out_type=jax.ShapeDtypeStruct((n_out, width), ys.dtype),
                       scratch_types=[pltpu.VMEM((SC_WINDOW,), jnp.int32),
                                      pltpu.VMEM((SC_WINDOW, width), ys.dtype)])
    def run(ys_hbm, idx_hbm, out_hbm, idx_v, buf):
        def window(win):
            pltpu.sync_copy(idx_hbm.at[win], idx_v)
            pltpu.sync_copy(ys_hbm.at[idx_v], buf)
            r0 = pl.multiple_of(win * SC_WINDOW, SC_WINDOW)
            pltpu.sync_copy(buf, out_hbm.at[pl.ds(r0, SC_WINDOW)])

        _sc_worker_windows(n_windows, window)

    return run(ys, idx.reshape(n_windows, SC_WINDOW))


def _expert_kernel(te_ref, nv_ref, xs_ref, wgu_ref, wdn_ref, out_ref):
    i = pl.program_id(0)

    @pl.when(i < nv_ref[0])
    def _():
        lo, hi = _unpack_bf16_pairs(xs_ref[...])
        half = D_MODEL // 2
        gu = (_dot(lo.astype(BF16), wgu_ref[0, :half, :])
              + _dot(hi.astype(BF16), wgu_ref[0, half:, :]))
        gt, up = gu[:, :D_EXPERT], gu[:, D_EXPERT:]
        act = (gt * jax.nn.sigmoid(gt)) * up
        out_ref[...] = _pack_bf16_pairs(_dot(act.astype(BF16), wdn_ref[0]))

    @pl.when(i >= nv_ref[0])
    def _():
        out_ref[...] = jnp.zeros_like(out_ref)


def _expert_call(tile_expert, n_valid, xs, wgu, wdn):
    nrows = xs.shape[0]
    return pl.pallas_call(
        _expert_kernel,
        grid_spec=pltpu.PrefetchScalarGridSpec(
            num_scalar_prefetch=2,
            grid=(nrows // ROW_TILE,),
            in_specs=[pl.BlockSpec((ROW_TILE, D_MODEL // 2), lambda i, te, nv: (i, 0)),
                      pl.BlockSpec((1, D_MODEL, 2 * D_EXPERT), lambda i, te, nv: (te[i], 0, 0)),
                      pl.BlockSpec((1, D_EXPERT, D_MODEL), lambda i, te, nv: (te[i], 0, 0))],
            out_specs=pl.BlockSpec((ROW_TILE, D_MODEL // 2), lambda i, te, nv: (i, 0))),
        out_shape=jax.ShapeDtypeStruct((nrows, D_MODEL // 2), jnp.uint32),
        compiler_params=pltpu.CompilerParams(
            dimension_semantics=("arbitrary",), vmem_limit_bytes=VMEM_LIMIT),
        name="expert_mlp",
    )(tile_expert, n_valid, xs, wgu, wdn)


def _combine_kernel(h_ref, y2_ref, route_ref, nfin_ref, out_ref):
    route = route_ref[...]
    w1, w2 = route[:, 0:1], route[:, 1:2]
    lo1, hi1 = _unpack_bf16_pairs(y2_ref[0])
    lo2, hi2 = _unpack_bf16_pairs(y2_ref[1])
    y = h_ref[...] + jnp.concatenate([w1 * lo1 + w2 * lo2, w1 * hi1 + w2 * hi2], axis=1)
    out_ref[...] = _rmsnorm(y, nfin_ref[...])


def _combine_call(h, y2, route, nfin, first_token):
    n = h.shape[0]
    first_block = first_token // COMBINE_TILE
    return pl.pallas_call(
        _combine_kernel,
        grid=(n // COMBINE_TILE,),
        in_specs=[pl.BlockSpec((COMBINE_TILE, D_MODEL), lambda i: (i, 0)),
                  pl.BlockSpec((2, COMBINE_TILE, D_MODEL // 2), lambda i: (0, first_block + i, 0)),
                  pl.BlockSpec((COMBINE_TILE, LANES), lambda i: (i, 0)),
                  pl.BlockSpec((1, D_MODEL), lambda i: (0, 0))],
        out_specs=pl.BlockSpec((COMBINE_TILE, D_MODEL), lambda i: (i, 0)),
        out_shape=jax.ShapeDtypeStruct((n, D_MODEL), F32),
        compiler_params=pltpu.CompilerParams(
            dimension_semantics=("arbitrary",), vmem_limit_bytes=VMEM_LIMIT),
        name="moe_combine",
    )(h, y2, route, nfin)


def kernel(x_prompt, x_sample, state_conv, state_gla, norm_mix, w_in, w_conv, w_alpha_up, b_alpha,
           gla_norm, w_conv_out, w_gla_out, w_o, norm_ffn, w_group_router, b_group_router,
           w_expert_router, b_expert_router, w_gate, w_up, w_down, norm_final):
    nbp, seq_p, _ = x_prompt.shape
    nbs, seq_s, _ = x_sample.shape
    assert norm_mix.shape[0] == 1, "single layer"
    assert seq_p % PROMPT_TILE == 0 and nbs % SAMPLE_SEQS == 0 and seq_s == 8

    wi = w_in[0]
    al0 = OFF_GA + 0
    w_pack = jnp.concatenate(
        [wi[:, :al0], wi[:, al0 + GLA_RANK:], wi[:, al0:al0 + GLA_RANK],
         jnp.zeros((D_MODEL, LANES - GLA_RANK), F32)], axis=1).astype(BF16)
    walpha = jnp.concatenate(
        [w_alpha_up[0], jnp.zeros((LANES - GLA_RANK, D_CONV), F32)], axis=0).astype(BF16)
    wr = jnp.concatenate(
        [w_group_router[0], w_expert_router[0],
         jnp.zeros((D_MODEL, LANES - N_GROUPS - N_EXPERTS), F32)], axis=1)
    wr_hi = wr.astype(BF16)
    wr_lo = (wr - wr_hi.astype(F32)).astype(BF16)
    br = jnp.concatenate([b_group_router[0], b_expert_router[0],
                          jnp.zeros((LANES - N_GROUPS - N_EXPERTS,), F32)])[None, :]
    weights = (norm_mix, w_pack, w_conv[0], walpha, b_alpha, gla_norm,
               w_conv_out[0].astype(BF16), w_gla_out[0].astype(BF16), w_o[0].astype(BF16),
               norm_ffn, wr_hi, wr_lo, br)
    wgu = jnp.concatenate([w_gate[0], w_up[0]], axis=-1).astype(BF16)
    wdn = w_down[0].astype(BF16)

    conv0 = jnp.zeros((nbp, CONV_WIDTH - 1, D_CONV), F32)
    gla0 = jnp.zeros((nbp, 1, 8, LANES), F32)
    h_p, xn_p, rt_p, conv_p, gla_p = _mixer_call(x_prompt, conv0, gla0, weights, prompt=True)
    h_s, xn_s, rt_s, conv_s, gla_s = _mixer_call(x_sample, state_conv[0], state_gla[0], weights,
                                                 prompt=False)
    ntp, nts = nbp * seq_p, nbs * seq_s
    ntok = ntp + nts
    assert ntp % COMBINE_TILE == 0 and nts % COMBINE_TILE == 0 and ntok % RANK_TILE == 0
    h_p, xn_p, rt_p = (a.reshape(ntp, a.shape[-1]) for a in (h_p, xn_p, rt_p))
    h_s, xn_s, rt_s = (a.reshape(nts, a.shape[-1]) for a in (h_s, xn_s, rt_s))

    eid = jnp.concatenate([rt_p[:, 2:4], rt_s[:, 2:4]], axis=0).astype(jnp.int32).T
    rank, cnt = _rank_call(eid)
    counts = cnt[:, 0].astype(jnp.int32)
    padded = ((counts + ROW_TILE - 1) // ROW_TILE) * ROW_TILE
    ends = jnp.cumsum(padded)
    starts = ends - padded
    expert_iota = jnp.arange(N_EXPERTS, dtype=jnp.int32)[:, None, None]
    pos = rank + jnp.sum(jnp.where(eid[None] == expert_iota, starts[:, None, None], 0), axis=0)
    n_tiles = (2 * ntok + N_EXPERTS * (ROW_TILE - 1)) // ROW_TILE
    n_rows = n_tiles * ROW_TILE
    tile_start = jnp.arange(n_tiles, dtype=jnp.int32) * ROW_TILE
    tile_expert = jnp.minimum(
        jnp.sum((tile_start[:, None] >= ends[None, :]).astype(jnp.int32), axis=1), N_EXPERTS - 1)
    n_valid = (ends[-1] // ROW_TILE).astype(jnp.int32)[None]
    last_valid_expert = jnp.sum(jnp.where(tile_start == ends[-1] - ROW_TILE, tile_expert, 0))
    tile_expert = jnp.where(tile_start < ends[-1], tile_expert, last_valid_expert)

    xs = _dispatch_rows(xn_p, xn_s, pos, n_rows)
    ys = _expert_call(tile_expert, n_valid, xs, wgu, wdn)
    y2 = _return_rows(ys, pos.reshape(-1)).reshape(2, ntok, D_MODEL // 2)
    nfin = norm_final[None, :]
    y_prompt = _combine_call(h_p, y2, rt_p, nfin, 0).reshape(nbp, seq_p, D_MODEL)
    y_sample = _combine_call(h_s, y2, rt_s, nfin, ntp).reshape(nbs, seq_s, D_MODEL)
    return (y_prompt, y_sample, conv_p[None], gla_p[None], conv_s[None], gla_s[None])
```

```python
import functools

import jax
import jax.numpy as jnp
from jax import lax
from jax.experimental import pallas as pl
from jax.experimental.pallas import tpu as pltpu
from jax.experimental.pallas import tpu_sc as plsc

F32 = jnp.float32
BF16 = jnp.bfloat16

D_MODEL = 1024
D_CONV = 512
CONV_WIDTH = 3
GLA_HEADS = 4
GLA_DK = 128
GLA_DV = 256
GLA_RANK = 16
GLA_GATE_NORM = 16.0
GLA_CHUNK = 32
N_GROUPS = 4
EXPERTS_PER_GROUP = 8
N_EXPERTS = 32
D_EXPERT = 256
EPS = 1e-6

LANES = 128
OFF_CB, OFF_CC, OFF_CH = 0, 512, 1024
OFF_Q, OFF_K, OFF_V, OFF_G = 1536, 2048, 2560, 3584
OFF_GA, OFF_GB, OFF_AL = 4608, 5632, 6656
D_IN_PACKED = OFF_AL + LANES

PROMPT_TILE = 512
GLA_BLOCK = 256
SAMPLE_SEQS = 8
ROW_TILE = 256
SC_WINDOW = 128
COMBINE_TILE = 512
RANK_TILE = 512
VMEM_LIMIT = 56 * 1024 * 1024


def _rmsnorm(x, g):
    ms = jnp.mean(x * x, axis=-1, keepdims=True)
    return (x * lax.rsqrt(ms + EPS)) * g


def _split_bf16(x):
    hi = x.astype(BF16)
    lo = (x - hi.astype(F32)).astype(BF16)
    return hi, lo


def _pack_bf16_pairs(x):
    n = x.shape[1] // 2
    lo = lax.bitcast_convert_type(x[:, :n].astype(BF16).astype(F32), jnp.uint32)
    hi = lax.bitcast_convert_type(x[:, n:].astype(BF16).astype(F32), jnp.uint32)
    return lax.shift_right_logical(lo, jnp.uint32(16)) | hi


def _unpack_bf16_pairs(w):
    lo = lax.bitcast_convert_type(lax.shift_left(w, jnp.uint32(16)), F32)
    hi = lax.bitcast_convert_type(w & jnp.uint32(0xFFFF0000), F32)
    return lo, hi


def _dot(a, b):
    return jnp.dot(a, b, preferred_element_type=F32)


def _chunk_causal(n, chunk):
    r = lax.broadcasted_iota(jnp.int32, (n, n), 0)
    c = lax.broadcasted_iota(jnp.int32, (n, n), 1)
    shift = chunk.bit_length() - 1
    assert chunk == 1 << shift
    same = lax.shift_right_arithmetic(r, shift) == lax.shift_right_arithmetic(c, shift)
    return same & (c <= r)


def _route(xn2, wrh_ref, wrl_ref, br_ref):
    rows = xn2.shape[0]
    xh, xl = _split_bf16(xn2)
    lg = _dot(xh, wrh_ref[...]) + _dot(xl, wrh_ref[...]) + _dot(xh, wrl_ref[...]) + br_ref[...]
    lane_i = lax.broadcasted_iota(jnp.int32, (rows, LANES), 1)
    lane = lane_i.astype(F32)
    neg = -jnp.inf
    is_g = lane_i < N_GROUPS
    gm = jnp.max(jnp.where(is_g, lg, neg), axis=1, keepdims=True)
    gs = jnp.sum(jnp.where(is_g, jnp.exp(lg - gm), 0.0), axis=1, keepdims=True)
    g_top = 1.0 / gs
    g_idx = jnp.min(jnp.where(is_g & (lg == gm), lane, float(LANES)), axis=1, keepdims=True)
    e_group = lax.shift_right_arithmetic(lane_i - N_GROUPS, 3).astype(F32)
    sel = (lane_i >= N_GROUPS) & (lane_i < N_GROUPS + N_EXPERTS) & (e_group == g_idx)
    el = jnp.where(sel, lg, neg)
    e1 = jnp.max(el, axis=1, keepdims=True)
    i1 = jnp.min(jnp.where(sel & (el == e1), lane, float(LANES)), axis=1, keepdims=True)
    sel2 = sel & (lane != i1)
    el2 = jnp.where(sel2, lg, neg)
    e2 = jnp.max(el2, axis=1, keepdims=True)
    i2 = jnp.min(jnp.where(sel2 & (el2 == e2), lane, float(LANES)), axis=1, keepdims=True)
    d = jnp.exp(e2 - e1)
    w1 = g_top / (1.0 + d)
    w2 = (g_top * d) / (1.0 + d)
    id1 = i1 - float(N_GROUPS)
    id2 = i2 - float(N_GROUPS)
    return jnp.where(lane_i == 0, w1, jnp.where(lane_i == 1, w2, jnp.where(lane_i == 2, id1,
                     jnp.where(lane_i == 3, id2, 0.0))))


def _mixer_kernel(x_ref, conv_in_ref, gla_in_ref,
                  nmix_ref, w_ref, wconv_ref, walpha_ref, balpha_ref, gnorm_ref,
                  wco_ref, wgo_ref, wo_ref, nffn_ref, wrh_ref, wrl_ref, br_ref,
                  h_ref, xn2_ref, route_ref, route_t_ref, conv_out_ref, gla_out_ref,
                  s_ref, ubuf_ref, *, prompt):
    if prompt:
        rows, chunk, nseq = PROMPT_TILE, GLA_CHUNK, 1
        x = x_ref[0]

        @pl.when(pl.program_id(1) == 0)
        def _():
            s_ref[...] = jnp.zeros_like(s_ref)
            ubuf_ref[...] = jnp.zeros_like(ubuf_ref)
    else:
        nseq, chunk = SAMPLE_SEQS, x_ref.shape[1]
        rows = nseq * chunk
        x = x_ref[...].reshape(rows, D_MODEL)
    nchunks = rows // chunk

    xn = _rmsnorm(x, nmix_ref[...]).astype(BF16)

    def proj(off, n):
        return _dot(xn, w_ref[:, off:off + n])

    alr = proj(OFF_AL, LANES).astype(BF16)
    lap = _dot(alr, walpha_ref[...])
    la = jax.nn.log_sigmoid(lap + balpha_ref[...]) * (1.0 / GLA_GATE_NORM)
    la_hi, la_lo = _split_bf16(la)
    blk = min(rows, GLA_BLOCK)
    nblk = rows // blk
    causal = _chunk_causal(blk, chunk)
    tri = jnp.where(causal, 1.0, 0.0).astype(BF16)
    cc = proj(OFF_CC, D_CONV)
    ch = proj(OFF_CH, D_CONV)
    cb = proj(OFF_CB, D_CONV)
    q = proj(OFF_Q, D_CONV)
    k = proj(OFF_K, D_CONV)
    b = jnp.concatenate([_dot(tri, la_hi[j * blk:(j + 1) * blk]) + _dot(tri, la_lo[j * blk:(j + 1) * blk])
                         for j in range(nblk)], axis=0)

    u = cc * ch
    wc = wconv_ref[...]
    if prompt:
        ubuf_ref[0, 8:8 + rows, :] = u
        z = (wc[0:1] * ubuf_ref[0, 6:6 + rows, :] + wc[1:2] * ubuf_ref[0, 7:7 + rows, :]
             + wc[2:3] * u)
        tail = u[rows - 2:rows, :]
        ubuf_ref[0, 6:8, :] = tail
        conv_out_ref[0] = tail
    else:
        u3 = u.reshape(nseq, chunk, D_CONV)
        ubuf_ref[:, 6:8, :] = conv_in_ref[...]
        ubuf_ref[:, 8:8 + chunk, :] = u3
        z3 = (wc[0:1] * ubuf_ref[:, 6:6 + chunk, :] + wc[1:2] * ubuf_ref[:, 7:7 + chunk, :]
              + wc[2:3] * u3)
        z = z3.reshape(rows, D_CONV)
        conv_out_ref[...] = ubuf_ref[:, 6 + chunk:8 + chunk, :]
    ya = _dot((cb * z).astype(BF16), wco_ref[...])
    v = proj(OFF_V, GLA_HEADS * GLA_DV)

    b3 = b.reshape(nchunks, chunk, D_CONV)
    bl3 = b3[:, chunk - 1:chunk, :]
    qin = (q * (GLA_DK ** -0.5)) * jnp.exp(b)
    kin = k * jnp.exp(-b)
    kst = (k.reshape(nchunks, chunk, D_CONV) * jnp.exp(bl3 - b3)).reshape(rows, D_CONV)
    pad = (-blk) % LANES
    blk_p = blk + pad
    cpb = blk // chunk

    def transposed(a):
        if pad:
            a = jnp.concatenate([a, jnp.zeros((pad, a.shape[1]), F32)], axis=0)
        return a.T

    blocks = range(nblk)
    b_ts = [transposed(b[j * blk:(j + 1) * blk]) for j in blocks]
    kst_ts = [transposed(kst[j * blk:(j + 1) * blk]) for j in blocks]
    col_chunk = lax.shift_right_arithmetic(
        lax.broadcasted_iota(jnp.int32, (GLA_DK, blk_p), 1), chunk.bit_length() - 1)
    g = proj(OFF_G, GLA_HEADS * GLA_DV)
    heads = range(GLA_HEADS)

    def v_of(j, hd):
        return v[j * blk:(j + 1) * blk, hd * GLA_DV:(hd + 1) * GLA_DV].astype(BF16)

    scs = [[lax.dot_general(qin[j * blk:(j + 1) * blk, hd * GLA_DK:(hd + 1) * GLA_DK].astype(BF16),
                            kin[j * blk:(j + 1) * blk, hd * GLA_DK:(hd + 1) * GLA_DK].astype(BF16),
                            (((1,), (1,)), ((), ())), preferred_element_type=F32)
            for hd in heads] for j in blocks]
    ga = proj(OFF_GA, D_MODEL)
    upds = []
    for j in blocks:
        upds.append([])
        for hd in heads:
            kst_h = kst_ts[j][hd * GLA_DK:(hd + 1) * GLA_DK, :]
            stacked = jnp.concatenate(
                [jnp.where(col_chunk == n, kst_h, 0.0).astype(BF16) for n in range(cpb)], axis=0)
            vh = v_of(j, hd)
            vh_p = jnp.concatenate([vh, jnp.zeros((pad, GLA_DV), BF16)], axis=0) if pad else vh
            upds[j].append(_dot(stacked, vh_p))
    o_intras = [[_dot(jnp.where(causal, scs[j][hd], 0.0).astype(BF16), v_of(j, hd))
                 for hd in heads] for j in blocks]
    gb = proj(OFF_GB, D_MODEL)
    merge_a = jax.nn.sigmoid(ga) * ya
    gate_b = jax.nn.sigmoid(gb)
    gated = []
    for hd in heads:
        c0, v0 = hd * GLA_DK, hd * GLA_DV
        s_run = s_ref[hd] if prompt else None
        o_inter = []
        for j in blocks:
            for n in range(cpb):
                r0 = j * blk + n * chunk
                s_prev = s_run if prompt else gla_in_ref[n, hd]
                qb = qin[r0:r0 + chunk, c0:c0 + GLA_DK].astype(BF16)
                o_inter.append(_dot(qb, s_prev.astype(BF16)))
                last = n * chunk + chunk - 1
                dcol = jnp.exp(b_ts[j][c0:c0 + GLA_DK, last:last + 1])
                s_new = s_prev * dcol + upds[j][hd][n * GLA_DK:(n + 1) * GLA_DK, :]
                if prompt:
                    s_run = s_new
                else:
                    gla_out_ref[n, hd] = s_new
        if prompt:
            s_ref[hd] = s_run
            gla_out_ref[0, hd] = s_run
        o_intra = jnp.concatenate([o_intras[j][hd] for j in blocks], axis=0)
        oh = _rmsnorm(o_intra + jnp.concatenate(o_inter, axis=0), gnorm_ref[...])
        gh = g[:, v0:v0 + GLA_DV]
        gated.append(oh * (gh * jax.nn.sigmoid(gh)))
    yb = _dot(jnp.concatenate(gated, axis=1).astype(BF16), wgo_ref[...])

    m = merge_a + gate_b * yb
    hh = x + _dot(m.astype(BF16), wo_ref[...])

    xn2 = _rmsnorm(hh, nffn_ref[...])
    route = _route(xn2, wrh_ref, wrl_ref, br_ref)
    rpad = (-rows) % LANES
    route_p = jnp.concatenate([route, jnp.zeros((rpad, LANES), F32)], axis=0) if rpad else route
    route_t_ref[0] = route_p.T[:8, :rows]
    if prompt:
        h_ref[0] = hh
        xn2_ref[0] = _pack_bf16_pairs(xn2)
        route_ref[0] = route
    else:
        h_ref[...] = hh.reshape(nseq, chunk, D_MODEL)
        xn2_ref[...] = _pack_bf16_pairs(xn2).reshape(nseq, chunk, D_MODEL // 2)
        route_ref[...] = route.reshape(nseq, chunk, LANES)


def _const_spec(shape):
    nd = len(shape)
    return pl.BlockSpec(shape, lambda *_: (0,) * nd, pipeline_mode=pl.Buffered(1))


def _mixer_call(x, conv_state, gla_state, weights, *, prompt):
    nb, seq, _ = x.shape
    if prompt:
        grid = (nb, seq // PROMPT_TILE)
        tok = lambda last: pl.BlockSpec((1, PROMPT_TILE, last), lambda b, i: (b, i, 0))
        conv_spec = pl.BlockSpec((1, CONV_WIDTH - 1, D_CONV), lambda b, i: (b, 0, 0))
        gla_spec = pl.BlockSpec((1, GLA_HEADS, GLA_DK, GLA_DV), lambda b, i: (b, 0, 0, 0))
        gla_in_spec = pl.BlockSpec((1, 1, 8, LANES), lambda b, i: (b, 0, 0, 0))
        scratch = [pltpu.VMEM((GLA_HEADS, GLA_DK, GLA_DV), F32),
                   pltpu.VMEM((1, PROMPT_TILE + 8, D_CONV), F32)]
        route_t_shape = (nb, 8, seq)
        route_t_spec = pl.BlockSpec((1, 8, PROMPT_TILE), lambda b, i: (b, 0, i))
    else:
        grid = (nb // SAMPLE_SEQS, 1)
        tok = lambda last: pl.BlockSpec((SAMPLE_SEQS, seq, last), lambda b, i: (b, 0, 0))
        conv_spec = pl.BlockSpec((SAMPLE_SEQS, CONV_WIDTH - 1, D_CONV), lambda b, i: (b, 0, 0))
        gla_spec = pl.BlockSpec((SAMPLE_SEQS, GLA_HEADS, GLA_DK, GLA_DV), lambda b, i: (b, 0, 0, 0))
        gla_in_spec = gla_spec
        scratch = [pltpu.VMEM((1, 8, LANES), F32),
                   pltpu.VMEM((SAMPLE_SEQS, 8 + seq, D_CONV), F32)]
        route_t_shape = (nb // SAMPLE_SEQS, 8, SAMPLE_SEQS * seq)
        route_t_spec = pl.BlockSpec((1, 8, SAMPLE_SEQS * seq), lambda b, i: (b, 0, 0))
    out_shape = (jax.ShapeDtypeStruct((nb, seq, D_MODEL), F32),
                 jax.ShapeDtypeStruct((nb, seq, D_MODEL // 2), jnp.uint32),
                 jax.ShapeDtypeStruct((nb, seq, LANES), F32),
                 jax.ShapeDtypeStruct(route_t_shape, F32),
                 jax.ShapeDtypeStruct((nb, CONV_WIDTH - 1, D_CONV), F32),
                 jax.ShapeDtypeStruct((nb, GLA_HEADS, GLA_DK, GLA_DV), F32))
    return pl.pallas_call(
        functools.partial(_mixer_kernel, prompt=prompt),
        grid=grid,
        in_specs=[tok(D_MODEL), conv_spec, gla_in_spec] + [_const_spec(w.shape) for w in weights],
        out_specs=(tok(D_MODEL), tok(D_MODEL // 2), tok(LANES), route_t_spec, conv_spec, gla_spec),
        out_shape=out_shape,
        scratch_shapes=scratch,
        compiler_params=pltpu.CompilerParams(
            dimension_semantics=("arbitrary", "arbitrary"), vmem_limit_bytes=VMEM_LIMIT),
        name="mixer_prompt" if prompt else "mixer_sample",
    )(x, conv_state, gla_state, *weights)


def _rank_kernel(eid_ref, rank_ref, cnt_ref, carry_ref):
    @pl.when(pl.program_id(0) == 0)
    def _():
        carry_ref[...] = jnp.zeros_like(carry_ref)

    n = eid_ref.shape[1]
    e_iota = lax.broadcasted_iota(jnp.int32, (N_EXPERTS, n), 0)
    oh0 = jnp.where(e_iota == eid_ref[0:1, :], 1.0, 0.0)
    oh1 = jnp.where(e_iota == eid_ref[1:2, :], 1.0, 0.0)
    cnt = oh0 + oh1
    r = lax.broadcasted_iota(jnp.int32, (n, n), 0)
    c = lax.broadcasted_iota(jnp.int32, (n, n), 1)
    before = jnp.where(r < c, 1.0, 0.0).astype(BF16)
    base = carry_ref[...] + _dot(cnt.astype(BF16), before)
    rank0 = jnp.sum(oh0 * base, axis=0, keepdims=True)
    rank1 = jnp.sum(oh1 * base, axis=0, keepdims=True)
    rank_ref[...] = jnp.concatenate([rank0, rank1], axis=0).astype(jnp.int32)
    carry_ref[...] = carry_ref[...] + jnp.sum(cnt, axis=1, keepdims=True)
    cnt_ref[...] = jnp.broadcast_to(carry_ref[...], cnt_ref.shape)


def _rank_call(eid):
    ntok = eid.shape[1]
    return pl.pallas_call(
        _rank_kernel,
        grid=(ntok // RANK_TILE,),
        in_specs=[pl.BlockSpec((2, RANK_TILE), lambda i: (0, i))],
        out_specs=(pl.BlockSpec((2, RANK_TILE), lambda i: (0, i)),
                   pl.BlockSpec((N_EXPERTS, LANES), lambda i: (0, 0))),
        out_shape=(jax.ShapeDtypeStruct((2, ntok), jnp.int32),
                   jax.ShapeDtypeStruct((N_EXPERTS, LANES), F32)),
        scratch_shapes=[pltpu.VMEM((N_EXPERTS, 1), F32)],
        compiler_params=pltpu.CompilerParams(dimension_semantics=("arbitrary",)),
        name="expert_rank",
    )(eid)


def _sc_mesh():
    return plsc.VectorSubcoreMesh(core_axis_name="c", subcore_axis_name="s")


def _sc_worker_windows(n_windows, body):
    info = plsc.get_sparse_core_info()
    n_workers = info.num_cores * info.num_subcores
    wid = lax.axis_index("s") * info.num_cores + lax.axis_index("c")

    @pl.loop(0, pl.cdiv(n_windows, n_workers))
    def _(j):
        win = wid + n_workers * j

        @pl.when(win < n_windows)
        def _():
            body(win)


def _dispatch_rows(x_prompt_rows, x_sample_rows, pos, n_rows):
    ntp, nts = x_prompt_rows.shape[0], x_sample_rows.shape[0]
    width, dtype = x_prompt_rows.shape[1], x_prompt_rows.dtype
    nwp, nws = ntp // SC_WINDOW, nts // SC_WINDOW
    pos3 = pos.reshape(2, nwp + nws, SC_WINDOW).transpose(1, 0, 2)

    @functools.partial(pl.kernel, mesh=_sc_mesh(), name="moe_dispatch",
                       out_type=jax.ShapeDtypeStruct((n_rows, width), dtype),
                       scratch_types=[pltpu.VMEM((2, SC_WINDOW), jnp.int32),
                                      pltpu.VMEM((SC_WINDOW, width), dtype)])
    def run(xp_hbm, xs_hbm, pos_hbm, out_hbm, idx_v, buf):
        def window(win):
            pltpu.sync_copy(pos_hbm.at[win], idx_v)

            @pl.when(win < nwp)
            def _():
                r0 = pl.multiple_of(win * SC_WINDOW, SC_WINDOW)
                pltpu.sync_copy(xp_hbm.at[pl.ds(r0, SC_WINDOW)], buf)

            @pl.when(win >= nwp)
            def _():
                r0 = pl.multiple_of((win - nwp) * SC_WINDOW, SC_WINDOW)
                pltpu.sync_copy(xs_hbm.at[pl.ds(r0, SC_WINDOW)], buf)

            pltpu.sync_copy(buf, out_hbm.at[idx_v.at[0]])
            pltpu.sync_copy(buf, out_hbm.at[idx_v.at[1]])

        _sc_worker_windows(nwp + nws, window)

    return run(x_prompt_rows, x_sample_rows, pos3)


def _return_rows(ys, idx):
    n_out, width = idx.shape[0], ys.shape[1]
    n_windows = n_out // SC_WINDOW

    @functools.partial(pl.kernel, mesh=_sc_mesh(), name="moe_return",
                       out_type=jax.ShapeDtypeStruct((n_out, width), ys.dtype),
                       scratch_types=[pltpu.VMEM((SC_WINDOW,), jnp.int32),
                                      pltpu.VMEM((SC_WINDOW, width), ys.dtype)])
    def run(ys_hbm, idx_hbm, out_hbm, idx_v, buf):
        def window(win):
            pltpu.sync_copy(idx_hbm.at[win], idx_v)
            pltpu.sync_copy(ys_hbm.at[idx_v], buf)
            r0 = pl.multiple_of(win * SC_WINDOW, SC_WINDOW)
            pltpu.sync_copy(buf, out_hbm.at[pl.ds(r0, SC_WINDOW)])

        _sc_worker_windows(n_windows, window)

    return run(ys, idx.reshape(n_windows, SC_WINDOW))


def _expert_kernel(te_ref, nv_ref, xs_ref, wg_ref, wu_ref, wd_ref, out_ref, wg_s, wu_s, wd_s):
    i = pl.program_id(0)
    valid = i < nv_ref[0]
    new_expert = (i == 0) | (te_ref[i] != te_ref[jnp.maximum(i - 1, 0)])

    @pl.when(valid & new_expert)
    def _():
        wg_s[...] = wg_ref[0].astype(BF16)
        wu_s[...] = wu_ref[0].astype(BF16)
        wd_s[...] = wd_ref[0].astype(BF16)

    @pl.when(valid)
    def _():
        lo, hi = _unpack_bf16_pairs(xs_ref[...])
        half = D_MODEL // 2
        sub = ROW_TILE // 2
        parts = []
        for r in (0, sub):
            xl, xh = lo[r:r + sub].astype(BF16), hi[r:r + sub].astype(BF16)
            gt = _dot(xl, wg_s[:half, :]) + _dot(xh, wg_s[half:, :])
            up = _dot(xl, wu_s[:half, :]) + _dot(xh, wu_s[half:, :])
            parts.append((gt, up))
        for r, (gt, up) in zip((0, sub), parts):
            act = (gt * jax.nn.sigmoid(gt)) * up
            out_ref[r:r + sub, :] = _pack_bf16_pairs(_dot(act.astype(BF16), wd_s[...]))

    @pl.when(jnp.logical_not(valid))
    def _():
        out_ref[...] = jnp.zeros_like(out_ref)


def _expert_call(tile_expert, n_valid, xs, w_gate, w_up, w_down):
    nrows = xs.shape[0]
    by_expert = lambda i, te, nv: (te[i], 0, 0)
    return pl.pallas_call(
        _expert_kernel,
        grid_spec=pltpu.PrefetchScalarGridSpec(
            num_scalar_prefetch=2,
            grid=(nrows // ROW_TILE,),
            in_specs=[pl.BlockSpec((ROW_TILE, D_MODEL // 2), lambda i, te, nv: (i, 0)),
                      pl.BlockSpec((1, D_MODEL, D_EXPERT), by_expert),
                      pl.BlockSpec((1, D_MODEL, D_EXPERT), by_expert),
                      pl.BlockSpec((1, D_EXPERT, D_MODEL), by_expert)],
            out_specs=pl.BlockSpec((ROW_TILE, D_MODEL // 2), lambda i, te, nv: (i, 0)),
            scratch_shapes=[pltpu.VMEM((D_MODEL, D_EXPERT), BF16),
                            pltpu.VMEM((D_MODEL, D_EXPERT), BF16),
                            pltpu.VMEM((D_EXPERT, D_MODEL), BF16)]),
        out_shape=jax.ShapeDtypeStruct((nrows, D_MODEL // 2), jnp.uint32),
        compiler_params=pltpu.CompilerParams(
            dimension_semantics=("arbitrary",), vmem_limit_bytes=VMEM_LIMIT),
        name="expert_mlp",
    )(tile_expert, n_valid, xs, w_gate, w_up, w_down)


def _combine_kernel(h_ref, y2_ref, route_ref, nfin_ref, out_ref):
    route = route_ref[...]
    w1, w2 = route[:, 0:1], route[:, 1:2]
    lo1, hi1 = _unpack_bf16_pairs(y2_ref[0])
    lo2, hi2 = _unpack_bf16_pairs(y2_ref[1])
    y = h_ref[...] + jnp.concatenate([w1 * lo1 + w2 * lo2, w1 * hi1 + w2 * hi2], axis=1)
    out_ref[...] = _rmsnorm(y, nfin_ref[...])


def _combine_call(h, y2, route, nfin, first_token):
    n = h.shape[0]
    first_block = first_token // COMBINE_TILE
    return pl.pallas_call(
        _combine_kernel,
        grid=(n // COMBINE_TILE,),
        in_specs=[pl.BlockSpec((COMBINE_TILE, D_MODEL), lambda i: (i, 0)),
                  pl.BlockSpec((2, COMBINE_TILE, D_MODEL // 2), lambda i: (0, first_block + i, 0)),
                  pl.BlockSpec((COMBINE_TILE, LANES), lambda i: (i, 0)),
                  pl.BlockSpec((1, D_MODEL), lambda i: (0, 0))],
        out_specs=pl.BlockSpec((COMBINE_TILE, D_MODEL), lambda i: (i, 0)),
        out_shape=jax.ShapeDtypeStruct((n, D_MODEL), F32),
        compiler_params=pltpu.CompilerParams(
            dimension_semantics=("arbitrary",), vmem_limit_bytes=VMEM_LIMIT),
        name="moe_combine",
    )(h, y2, route, nfin)


def kernel(x_prompt, x_sample, state_conv, state_gla, norm_mix, w_in, w_conv, w_alpha_up, b_alpha,
           gla_norm, w_conv_out, w_gla_out, w_o, norm_ffn, w_group_router, b_group_router,
           w_expert_router, b_expert_router, w_gate, w_up, w_down, norm_final):
    nbp, seq_p, _ = x_prompt.shape
    nbs, seq_s, _ = x_sample.shape
    assert norm_mix.shape[0] == 1, "single layer"
    assert seq_p % PROMPT_TILE == 0 and nbs % SAMPLE_SEQS == 0 and seq_s == 8

    wi = w_in[0]
    al0 = OFF_GA + 0
    w_pack = jnp.concatenate(
        [wi[:, :al0], wi[:, al0 + GLA_RANK:], wi[:, al0:al0 + GLA_RANK],
         jnp.zeros((D_MODEL, LANES - GLA_RANK), F32)], axis=1).astype(BF16)
    walpha = jnp.concatenate(
        [w_alpha_up[0], jnp.zeros((LANES - GLA_RANK, D_CONV), F32)], axis=0).astype(BF16)
    wr = jnp.concatenate(
        [w_group_router[0], w_expert_router[0],
         jnp.zeros((D_MODEL, LANES - N_GROUPS - N_EXPERTS), F32)], axis=1)
    wr_hi = wr.astype(BF16)
    wr_lo = (wr - wr_hi.astype(F32)).astype(BF16)
    br = jnp.concatenate([b_group_router[0], b_expert_router[0],
                          jnp.zeros((LANES - N_GROUPS - N_EXPERTS,), F32)])[None, :]
    weights = (norm_mix, w_pack, w_conv[0], walpha, b_alpha, gla_norm,
               w_conv_out[0].astype(BF16), w_gla_out[0].astype(BF16), w_o[0].astype(BF16),
               norm_ffn, wr_hi, wr_lo, br)

    conv0 = jnp.zeros((nbp, CONV_WIDTH - 1, D_CONV), F32)
    gla0 = jnp.zeros((nbp, 1, 8, LANES), F32)
    h_p, xn_p, rt_p, rtt_p, conv_p, gla_p = _mixer_call(x_prompt, conv0, gla0, weights, prompt=True)
    h_s, xn_s, rt_s, rtt_s, conv_s, gla_s = _mixer_call(x_sample, state_conv[0], state_gla[0],
                                                        weights, prompt=False)
    ntp, nts = nbp * seq_p, nbs * seq_s
    ntok = ntp + nts
    assert ntp % COMBINE_TILE == 0 and nts % COMBINE_TILE == 0 and ntok % RANK_TILE == 0
    h_p, xn_p, rt_p = (a.reshape(ntp, a.shape[-1]) for a in (h_p, xn_p, rt_p))
    h_s, xn_s, rt_s = (a.reshape(nts, a.shape[-1]) for a in (h_s, xn_s, rt_s))

    eid = jnp.concatenate([rtt_p[:, 2:4, :].transpose(1, 0, 2).reshape(2, ntp),
                           rtt_s[:, 2:4, :].transpose(1, 0, 2).reshape(2, nts)],
                          axis=1).astype(jnp.int32)
    rank, cnt = _rank_call(eid)
    counts = cnt[:, 0].astype(jnp.int32)
    padded = ((counts + ROW_TILE - 1) // ROW_TILE) * ROW_TILE
    ends = jnp.cumsum(padded)
    starts = ends - padded
    expert_iota = jnp.arange(N_EXPERTS, dtype=jnp.int32)[:, None, None]
    pos = rank + jnp.sum(jnp.where(eid[None] == expert_iota, starts[:, None, None], 0), axis=0)
    n_tiles = (2 * ntok + N_EXPERTS * (ROW_TILE - 1)) // ROW_TILE
    n_rows = n_tiles * ROW_TILE
    tile_start = jnp.arange(n_tiles, dtype=jnp.int32) * ROW_TILE
    tile_expert = jnp.minimum(
        jnp.sum((tile_start[:, None] >= ends[None, :]).astype(jnp.int32), axis=1), N_EXPERTS - 1)
    n_valid = (ends[-1] // ROW_TILE).astype(jnp.int32)[None]
    last_valid_expert = jnp.sum(jnp.where(tile_start == ends[-1] - ROW_TILE, tile_expert, 0))
    tile_expert = jnp.where(tile_start < ends[-1], tile_expert, last_valid_expert)

    xs = _dispatch_rows(xn_p, xn_s, pos, n_rows)
    ys = _expert_call(tile_expert, n_valid, xs, w_gate[0], w_up[0], w_down[0])
    y2 = _return_rows(ys, pos.reshape(-1)).reshape(2, ntok, D_MODEL // 2)
    nfin = norm_final[None, :]
    y_prompt = _combine_call(h_p, y2, rt_p, nfin, 0).reshape(nbp, seq_p, D_MODEL)
    y_sample = _combine_call(h_s, y2, rt_s, nfin, ntp).reshape(nbs, seq_s, D_MODEL)
    return (y_prompt, y_sample, conv_p[None], gla_p[None], conv_s[None], gla_s[None])
```

```python
import functools

import jax
import jax.numpy as jnp
from jax import lax
from jax.experimental import pallas as pl
from jax.experimental.pallas import tpu as pltpu
from jax.experimental.pallas import tpu_sc as plsc

F32 = jnp.float32
BF16 = jnp.bfloat16

D_MODEL = 1024
D_CONV = 512
CONV_WIDTH = 3
GLA_HEADS = 4
GLA_DK = 128
GLA_DV = 256
GLA_RANK = 16
GLA_GATE_NORM = 16.0
GLA_CHUNK = 32
N_GROUPS = 4
EXPERTS_PER_GROUP = 8
N_EXPERTS = 32
D_EXPERT = 256
EPS = 1e-6

LANES = 128
OFF_CB, OFF_CC, OFF_CH = 0, 512, 1024
OFF_Q, OFF_K, OFF_V, OFF_G = 1536, 2048, 2560, 3584
OFF_GA, OFF_GB, OFF_AL = 4608, 5632, 6656
D_IN_PACKED = OFF_AL + LANES

PROMPT_TILE = 512
GLA_BLOCK = 256
SAMPLE_SEQS = 8
ROW_TILE = 256
SC_WINDOW = 128
COMBINE_TILE = 512
RANK_TILE = 512
VMEM_LIMIT = 56 * 1024 * 1024


def _rmsnorm(x, g):
    ms = jnp.mean(x * x, axis=-1, keepdims=True)
    return (x * lax.rsqrt(ms + EPS)) * g


def _split_bf16(x):
    hi = x.astype(BF16)
    lo = (x - hi.astype(F32)).astype(BF16)
    return hi, lo


def _pack_bf16_pairs(x):
    n = x.shape[1] // 2
    lo = lax.bitcast_convert_type(x[:, :n].astype(BF16).astype(F32), jnp.uint32)
    hi = lax.bitcast_convert_type(x[:, n:].astype(BF16).astype(F32), jnp.uint32)
    return lax.shift_right_logical(lo, jnp.uint32(16)) | hi


def _unpack_bf16_pairs(w):
    lo = lax.bitcast_convert_type(lax.shift_left(w, jnp.uint32(16)), F32)
    hi = lax.bitcast_convert_type(w & jnp.uint32(0xFFFF0000), F32)
    return lo, hi


def _dot(a, b):
    return jnp.dot(a, b, preferred_element_type=F32)


def _chunk_causal(n, chunk):
    r = lax.broadcasted_iota(jnp.int32, (n, n), 0)
    c = lax.broadcasted_iota(jnp.int32, (n, n), 1)
    shift = chunk.bit_length() - 1
    assert chunk == 1 << shift
    same = lax.shift_right_arithmetic(r, shift) == lax.shift_right_arithmetic(c, shift)
    return same & (c <= r)


def _route(xn2, wr_ref, br_ref):
    rows = xn2.shape[0]
    xh, xl = _split_bf16(xn2)
    part = _dot(xh, wr_ref[...]) + _dot(xl, wr_ref[...])
    lg = part[:, :LANES] + part[:, LANES:] + br_ref[...]
    lane_i = lax.broadcasted_iota(jnp.int32, (rows, LANES), 1)
    lane = lane_i.astype(F32)
    neg = -jnp.inf
    is_g = lane_i < N_GROUPS
    gm = jnp.max(jnp.where(is_g, lg, neg), axis=1, keepdims=True)
    gs = jnp.sum(jnp.where(is_g, jnp.exp(lg - gm), 0.0), axis=1, keepdims=True)
    g_top = 1.0 / gs
    g_idx = jnp.min(jnp.where(is_g & (lg == gm), lane, float(LANES)), axis=1, keepdims=True)
    e_group = lax.shift_right_arithmetic(lane_i - N_GROUPS, 3).astype(F32)
    sel = (lane_i >= N_GROUPS) & (lane_i < N_GROUPS + N_EXPERTS) & (e_group == g_idx)
    el = jnp.where(sel, lg, neg)
    e1 = jnp.max(el, axis=1, keepdims=True)
    i1 = jnp.min(jnp.where(sel & (el == e1), lane, float(LANES)), axis=1, keepdims=True)
    sel2 = sel & (lane != i1)
    el2 = jnp.where(sel2, lg, neg)
    e2 = jnp.max(el2, axis=1, keepdims=True)
    i2 = jnp.min(jnp.where(sel2 & (el2 == e2), lane, float(LANES)), axis=1, keepdims=True)
    d = jnp.exp(e2 - e1)
    w1 = g_top / (1.0 + d)
    w2 = (g_top * d) / (1.0 + d)
    id1 = i1 - float(N_GROUPS)
    id2 = i2 - float(N_GROUPS)
    return jnp.where(lane_i == 0, w1, jnp.where(lane_i == 1, w2, jnp.where(lane_i == 2, id1,
                     jnp.where(lane_i == 3, id2, 0.0))))


def _mixer_kernel(x_ref, conv_in_ref, gla_in_ref,
                  nmix_ref, w_ref, wgate_ref, walr_ref, wconv_ref, walpha_ref, balpha_ref, gnorm_ref,
                  wco_ref, wgo_ref, wo_ref, nffn_ref, wr_ref, br_ref,
                  h_ref, xn2_ref, route_ref, route_t_ref, conv_out_ref, gla_out_ref,
                  s_ref, ubuf_ref, *, prompt):
    if prompt:
        rows, chunk, nseq = PROMPT_TILE, GLA_CHUNK, 1
        x = x_ref[0]

        @pl.when(pl.program_id(1) == 0)
        def _():
            s_ref[...] = jnp.zeros_like(s_ref)
            ubuf_ref[...] = jnp.zeros_like(ubuf_ref)
    else:
        nseq, chunk = SAMPLE_SEQS, x_ref.shape[1]
        rows = nseq * chunk
        x = x_ref[...].reshape(rows, D_MODEL)
    nchunks = rows // chunk

    xn = _rmsnorm(x, nmix_ref[...]).astype(BF16)

    def proj(off, n):
        if off == OFF_AL:
            return _dot(xn, walr_ref[...])
        if off >= OFF_GA:
            return _dot(xn, wgate_ref[:, off - OFF_GA:off - OFF_GA + n])
        return _dot(xn, w_ref[:, off:off + n])

    alr = proj(OFF_AL, LANES).astype(BF16)
    lap = _dot(alr, walpha_ref[...])
    la = jax.nn.log_sigmoid(lap + balpha_ref[...]) * (1.0 / GLA_GATE_NORM)
    la_hi, la_lo = _split_bf16(la)
    blk = min(rows, GLA_BLOCK)
    nblk = rows // blk
    causal = _chunk_causal(blk, chunk)
    tri = jnp.where(causal, 1.0, 0.0).astype(BF16)
    cc = proj(OFF_CC, D_CONV)
    ch = proj(OFF_CH, D_CONV)
    cb = proj(OFF_CB, D_CONV)
    q = proj(OFF_Q, D_CONV)
    k = proj(OFF_K, D_CONV)
    b = jnp.concatenate([_dot(tri, la_hi[j * blk:(j + 1) * blk]) + _dot(tri, la_lo[j * blk:(j + 1) * blk])
                         for j in range(nblk)], axis=0)

    u = cc * ch
    wc = wconv_ref[...]
    if prompt:
        ubuf_ref[0, 8:8 + rows, :] = u
        z = (wc[0:1] * ubuf_ref[0, 6:6 + rows, :] + wc[1:2] * ubuf_ref[0, 7:7 + rows, :]
             + wc[2:3] * u)
        tail = u[rows - 2:rows, :]
        ubuf_ref[0, 6:8, :] = tail
        conv_out_ref[0] = tail
    else:
        u3 = u.reshape(nseq, chunk, D_CONV)
        ubuf_ref[:, 6:8, :] = conv_in_ref[...]
        ubuf_ref[:, 8:8 + chunk, :] = u3
        z3 = (wc[0:1] * ubuf_ref[:, 6:6 + chunk, :] + wc[1:2] * ubuf_ref[:, 7:7 + chunk, :]
              + wc[2:3] * u3)
        z = z3.reshape(rows, D_CONV)
        conv_out_ref[...] = ubuf_ref[:, 6 + chunk:8 + chunk, :]
    ya = _dot((cb * z).astype(BF16), wco_ref[...])
    v = proj(OFF_V, GLA_HEADS * GLA_DV)

    b3 = b.reshape(nchunks, chunk, D_CONV)
    bl3 = b3[:, chunk - 1:chunk, :]
    qin = (q * (GLA_DK ** -0.5)) * jnp.exp(b)
    kin = k * jnp.exp(-b)
    kst = (k.reshape(nchunks, chunk, D_CONV) * jnp.exp(bl3 - b3)).reshape(rows, D_CONV)
    pad = (-blk) % LANES
    blk_p = blk + pad
    cpb = blk // chunk

    def transposed(a):
        if pad:
            a = jnp.concatenate([a, jnp.zeros((pad, a.shape[1]), F32)], axis=0)
        return a.T

    blocks = range(nblk)
    b_ts = [transposed(b[j * blk:(j + 1) * blk]) for j in blocks]
    kst_ts = [transposed(kst[j * blk:(j + 1) * blk]) for j in blocks]
    col_chunk = lax.shift_right_arithmetic(
        lax.broadcasted_iota(jnp.int32, (GLA_DK, blk_p), 1), chunk.bit_length() - 1)
    g = proj(OFF_G, GLA_HEADS * GLA_DV)
    heads = range(GLA_HEADS)

    def v_of(j, hd):
        return v[j * blk:(j + 1) * blk, hd * GLA_DV:(hd + 1) * GLA_DV].astype(BF16)

    scs = [[lax.dot_general(qin[j * blk:(j + 1) * blk, hd * GLA_DK:(hd + 1) * GLA_DK].astype(BF16),
                            kin[j * blk:(j + 1) * blk, hd * GLA_DK:(hd + 1) * GLA_DK].astype(BF16),
                            (((1,), (1,)), ((), ())), preferred_element_type=F32)
            for hd in heads] for j in blocks]
    ga = proj(OFF_GA, D_MODEL)
    upds = []
    for j in blocks:
        upds.append([])
        for hd in heads:
            kst_h = kst_ts[j][hd * GLA_DK:(hd + 1) * GLA_DK, :]
            stacked = jnp.concatenate(
                [jnp.where(col_chunk == n, kst_h, 0.0).astype(BF16) for n in range(cpb)], axis=0)
            vh = v_of(j, hd)
            vh_p = jnp.concatenate([vh, jnp.zeros((pad, GLA_DV), BF16)], axis=0) if pad else vh
            upds[j].append(_dot(stacked, vh_p))
    o_intras = [[_dot(jnp.where(causal, scs[j][hd], 0.0).astype(BF16), v_of(j, hd))
                 for hd in heads] for j in blocks]
    gb = proj(OFF_GB, D_MODEL)
    merge_a = jax.nn.sigmoid(ga) * ya
    gate_b = jax.nn.sigmoid(gb)
    gated = []
    for hd in heads:
        c0, v0 = hd * GLA_DK, hd * GLA_DV
        s_run = s_ref[hd] if prompt else None
        o_inter = []
        for j in blocks:
            for n in range(cpb):
                r0 = j * blk + n * chunk
                s_prev = s_run if prompt else gla_in_ref[n, hd]
                qb = qin[r0:r0 + chunk, c0:c0 + GLA_DK].astype(BF16)
                o_inter.append(_dot(qb, s_prev.astype(BF16)))
                last = n * chunk + chunk - 1
                dcol = jnp.exp(b_ts[j][c0:c0 + GLA_DK, last:last + 1])
                s_new = s_prev * dcol + upds[j][hd][n * GLA_DK:(n + 1) * GLA_DK, :]
                if prompt:
                    s_run = s_new
                else:
                    gla_out_ref[n, hd] = s_new
        if prompt:
            s_ref[hd] = s_run
            gla_out_ref[0, hd] = s_run
        o_intra = jnp.concatenate([o_intras[j][hd] for j in blocks], axis=0)
        oh = _rmsnorm(o_intra + jnp.concatenate(o_inter, axis=0), gnorm_ref[...])
        gh = g[:, v0:v0 + GLA_DV]
        gated.append(oh * (gh * jax.nn.sigmoid(gh)))
    yb = _dot(jnp.concatenate(gated, axis=1).astype(BF16), wgo_ref[...])

    m = merge_a + gate_b * yb
    hh = x + _dot(m.astype(BF16), wo_ref[...])

    xn2 = _rmsnorm(hh, nffn_ref[...])
    route = _route(xn2, wr_ref, br_ref)
    rpad = (-rows) % LANES
    route_p = jnp.concatenate([route, jnp.zeros((rpad, LANES), F32)], axis=0) if rpad else route
    route_t_ref[0] = route_p.T[:8, :rows]
    if prompt:
        h_ref[0] = hh
        xn2_ref[0] = _pack_bf16_pairs(xn2)
        route_ref[0] = route
    else:
        h_ref[...] = hh.reshape(nseq, chunk, D_MODEL)
        xn2_ref[...] = _pack_bf16_pairs(xn2).reshape(nseq, chunk, D_MODEL // 2)
        route_ref[...] = route.reshape(nseq, chunk, LANES)


def _mixer_weights(norm_mix, w_in, w_conv, w_alpha_up, b_alpha, gla_norm, w_conv_out, w_gla_out, w_o,
                   norm_ffn, w_group_router, b_group_router, w_expert_router, b_expert_router):
    wi = w_in[0]
    w_main = wi[:, :OFF_GA].astype(BF16)
    w_gates = wi[:, OFF_GA + GLA_RANK:].astype(BF16)
    w_alr = jnp.concatenate([wi[:, OFF_GA:OFF_GA + GLA_RANK],
                             jnp.zeros((D_MODEL, LANES - GLA_RANK), F32)], axis=1).astype(BF16)
    walpha = jnp.concatenate(
        [w_alpha_up[0], jnp.zeros((LANES - GLA_RANK, D_CONV), F32)], axis=0).astype(BF16)
    wr = jnp.concatenate(
        [w_group_router[0], w_expert_router[0],
         jnp.zeros((D_MODEL, LANES - N_GROUPS - N_EXPERTS), F32)], axis=1)
    wr_hi = wr.astype(BF16)
    wr_split = jnp.concatenate([wr_hi, (wr - wr_hi.astype(F32)).astype(BF16)], axis=1)
    br = jnp.concatenate([b_group_router[0], b_expert_router[0],
                          jnp.zeros((LANES - N_GROUPS - N_EXPERTS,), F32)])[None, :]
    return (norm_mix, w_main, w_gates, w_alr, w_conv[0], walpha, b_alpha, gla_norm,
            w_conv_out[0].astype(BF16), w_gla_out[0].astype(BF16), w_o[0].astype(BF16),
            norm_ffn, wr_split, br)


def _const_spec(shape):
    nd = len(shape)
    return pl.BlockSpec(shape, lambda *_: (0,) * nd, pipeline_mode=pl.Buffered(1))


def _mixer_call(x, conv_state, gla_state, weights, *, prompt):
    nb, seq, _ = x.shape
    if prompt:
        grid = (nb, seq // PROMPT_TILE)
        tok = lambda last: pl.BlockSpec((1, PROMPT_TILE, last), lambda b, i: (b, i, 0))
        conv_spec = pl.BlockSpec((1, CONV_WIDTH - 1, D_CONV), lambda b, i: (b, 0, 0))
        gla_spec = pl.BlockSpec((1, GLA_HEADS, GLA_DK, GLA_DV), lambda b, i: (b, 0, 0, 0))
        gla_in_spec = pl.BlockSpec((1, 1, 8, LANES), lambda b, i: (b, 0, 0, 0))
        scratch = [pltpu.VMEM((GLA_HEADS, GLA_DK, GLA_DV), F32),
                   pltpu.VMEM((1, PROMPT_TILE + 8, D_CONV), F32)]
        route_t_shape = (nb, 8, seq)
        route_t_spec = pl.BlockSpec((1, 8, PROMPT_TILE), lambda b, i: (b, 0, i))
    else:
        grid = (nb // SAMPLE_SEQS, 1)
        tok = lambda last: pl.BlockSpec((SAMPLE_SEQS, seq, last), lambda b, i: (b, 0, 0))
        conv_spec = pl.BlockSpec((SAMPLE_SEQS, CONV_WIDTH - 1, D_CONV), lambda b, i: (b, 0, 0))
        gla_spec = pl.BlockSpec((SAMPLE_SEQS, GLA_HEADS, GLA_DK, GLA_DV), lambda b, i: (b, 0, 0, 0))
        gla_in_spec = gla_spec
        scratch = [pltpu.VMEM((1, 8, LANES), F32),
                   pltpu.VMEM((SAMPLE_SEQS, 8 + seq, D_CONV), F32)]
        route_t_shape = (nb // SAMPLE_SEQS, 8, SAMPLE_SEQS * seq)
        route_t_spec = pl.BlockSpec((1, 8, SAMPLE_SEQS * seq), lambda b, i: (b, 0, 0))
    out_shape = (jax.ShapeDtypeStruct((nb, seq, D_MODEL), F32),
                 jax.ShapeDtypeStruct((nb, seq, D_MODEL // 2), jnp.uint32),
                 jax.ShapeDtypeStruct((nb, seq, LANES), F32),
                 jax.ShapeDtypeStruct(route_t_shape, F32),
                 jax.ShapeDtypeStruct((nb, CONV_WIDTH - 1, D_CONV), F32),
                 jax.ShapeDtypeStruct((nb, GLA_HEADS, GLA_DK, GLA_DV), F32))
    return pl.pallas_call(
        functools.partial(_mixer_kernel, prompt=prompt),
        grid=grid,
        in_specs=[tok(D_MODEL), conv_spec, gla_in_spec] + [_const_spec(w.shape) for w in weights],
        out_specs=(tok(D_MODEL), tok(D_MODEL // 2), tok(LANES), route_t_spec, conv_spec, gla_spec),
        out_shape=out_shape,
        scratch_shapes=scratch,
        compiler_params=pltpu.CompilerParams(
            dimension_semantics=("arbitrary", "arbitrary"), vmem_limit_bytes=VMEM_LIMIT),
        name="mixer_prompt" if prompt else "mixer_sample",
    )(x, conv_state, gla_state, *weights)


def _rank_kernel(eid_ref, rank_ref, cnt_ref, carry_ref):
    @pl.when(pl.program_id(0) == 0)
    def _():
        carry_ref[...] = jnp.zeros_like(carry_ref)

    n = eid_ref.shape[1]
    e_iota = lax.broadcasted_iota(jnp.int32, (N_EXPERTS, n), 0)
    oh0 = jnp.where(e_iota == eid_ref[0:1, :], 1.0, 0.0)
    oh1 = jnp.where(e_iota == eid_ref[1:2, :], 1.0, 0.0)
    cnt = oh0 + oh1
    r = lax.broadcasted_iota(jnp.int32, (n, n), 0)
    c = lax.broadcasted_iota(jnp.int32, (n, n), 1)
    before = jnp.where(r < c, 1.0, 0.0).astype(BF16)
    base = carry_ref[...] + _dot(cnt.astype(BF16), before)
    rank0 = jnp.sum(oh0 * base, axis=0, keepdims=True)
    rank1 = jnp.sum(oh1 * base, axis=0, keepdims=True)
    rank_ref[...] = jnp.concatenate([rank0, rank1], axis=0).astype(jnp.int32)
    carry_ref[...] = carry_ref[...] + jnp.sum(cnt, axis=1, keepdims=True)
    cnt_ref[...] = jnp.broadcast_to(carry_ref[...], cnt_ref.shape)


def _rank_call(eid):
    ntok = eid.shape[1]
    return pl.pallas_call(
        _rank_kernel,
        grid=(ntok // RANK_TILE,),
        in_specs=[pl.BlockSpec((2, RANK_TILE), lambda i: (0, i))],
        out_specs=(pl.BlockSpec((2, RANK_TILE), lambda i: (0, i)),
                   pl.BlockSpec((N_EXPERTS, LANES), lambda i: (0, 0))),
        out_shape=(jax.ShapeDtypeStruct((2, ntok), jnp.int32),
                   jax.ShapeDtypeStruct((N_EXPERTS, LANES), F32)),
        scratch_shapes=[pltpu.VMEM((N_EXPERTS, 1), F32)],
        compiler_params=pltpu.CompilerParams(dimension_semantics=("arbitrary",)),
        name="expert_rank",
    )(eid)


def _sc_mesh():
    return plsc.VectorSubcoreMesh(core_axis_name="c", subcore_axis_name="s")


def _sc_worker_windows(n_windows, body):
    info = plsc.get_sparse_core_info()
    n_workers = info.num_cores * info.num_subcores
    wid = lax.axis_index("s") * info.num_cores + lax.axis_index("c")

    @pl.loop(0, pl.cdiv(n_windows, n_workers))
    def _(j):
        win = wid + n_workers * j

        @pl.when(win < n_windows)
        def _():
            body(win)


def _dispatch_rows(x_prompt_rows, x_sample_rows, pos, n_rows):
    ntp, nts = x_prompt_rows.shape[0], x_sample_rows.shape[0]
    width, dtype = x_prompt_rows.shape[1], x_prompt_rows.dtype
    nwp, nws = ntp // SC_WINDOW, nts // SC_WINDOW
    pos3 = pos.reshape(2, nwp + nws, SC_WINDOW).transpose(1, 0, 2)

    @functools.partial(pl.kernel, mesh=_sc_mesh(), name="moe_dispatch",
                       out_type=jax.ShapeDtypeStruct((n_rows, width), dtype),
                       scratch_types=[pltpu.VMEM((2, SC_WINDOW), jnp.int32),
                                      pltpu.VMEM((SC_WINDOW, width), dtype)])
    def run(xp_hbm, xs_hbm, pos_hbm, out_hbm, idx_v, buf):
        def window(win):
            pltpu.sync_copy(pos_hbm.at[win], idx_v)

            @pl.when(win < nwp)
            def _():
                r0 = pl.multiple_of(win * SC_WINDOW, SC_WINDOW)
                pltpu.sync_copy(xp_hbm.at[pl.ds(r0, SC_WINDOW)], buf)

            @pl.when(win >= nwp)
            def _():
                r0 = pl.multiple_of((win - nwp) * SC_WINDOW, SC_WINDOW)
                pltpu.sync_copy(xs_hbm.at[pl.ds(r0, SC_WINDOW)], buf)

            pltpu.sync_copy(buf, out_hbm.at[idx_v.at[0]])
            pltpu.sync_copy(buf, out_hbm.at[idx_v.at[1]])

        _sc_worker_windows(nwp + nws, window)

    return run(x_prompt_rows, x_sample_rows, pos3)


def _return_rows(ys, idx):
    n_out, width = idx.shape[0], ys.shape[1]
    n_windows = n_out // SC_WINDOW

    @functools.partial(pl.kernel, mesh=_sc_mesh(), name="moe_return",
                       out_type=jax.ShapeDtypeStruct((n_out, width), ys.dtype),
                       scratch_types=[pltpu.VMEM((SC_WINDOW,), jnp.int32),
                                      pltpu.VMEM((SC_WINDOW, width), ys.dtype)])
    def run(ys_hbm, idx_hbm, out_hbm, idx_v, buf):
        def window(win):
            pltpu.sync_copy(idx_hbm.at[win], idx_v)
            pltpu.sync_copy(ys_hbm.at[idx_v], buf)
            r0 = pl.multiple_of(win * SC_WINDOW, SC_WINDOW)
            pltpu.sync_copy(buf, out_hbm.at[pl.ds(r0, SC_WINDOW)])

        _sc_worker_windows(n_windows, window)

    return run(ys, idx.reshape(n_windows, SC_WINDOW))


def _expert_kernel(te_ref, nv_ref, xs_ref, wg_ref, wu_ref, wd_ref, out_ref, wg_s, wu_s, wd_s):
    i = pl.program_id(0)
    valid = i < nv_ref[0]
    new_expert = (i == 0) | (te_ref[i] != te_ref[jnp.maximum(i - 1, 0)])

    @pl.when(valid & new_expert)
    def _():
        wg_s[...] = wg_ref[0].astype(BF16)
        wu_s[...] = wu_ref[0].astype(BF16)
        wd_s[...] = wd_ref[0].astype(BF16)

    @pl.when(valid)
    def _():
        lo, hi = _unpack_bf16_pairs(xs_ref[...])
        half = D_MODEL // 2
        sub = ROW_TILE // 2
        parts = []
        for r in (0, sub):
            xl, xh = lo[r:r + sub].astype(BF16), hi[r:r + sub].astype(BF16)
            gt = _dot(xl, wg_s[:half, :]) + _dot(xh, wg_s[half:, :])
            up = _dot(xl, wu_s[:half, :]) + _dot(xh, wu_s[half:, :])
            parts.append((gt, up))
        for r, (gt, up) in zip((0, sub), parts):
            act = (gt * jax.nn.sigmoid(gt)) * up
            out_ref[r:r + sub, :] = _pack_bf16_pairs(_dot(act.astype(BF16), wd_s[...]))

    @pl.when(jnp.logical_not(valid))
    def _():
        out_ref[...] = jnp.zeros_like(out_ref)


def _expert_call(tile_expert, n_valid, xs, w_gate, w_up, w_down):
    nrows = xs.shape[0]
    by_expert = lambda i, te, nv: (te[i], 0, 0)
    return pl.pallas_call(
        _expert_kernel,
        grid_spec=pltpu.PrefetchScalarGridSpec(
            num_scalar_prefetch=2,
            grid=(nrows // ROW_TILE,),
            in_specs=[pl.BlockSpec((ROW_TILE, D_MODEL // 2), lambda i, te, nv: (i, 0)),
                      pl.BlockSpec((1, D_MODEL, D_EXPERT), by_expert),
                      pl.BlockSpec((1, D_MODEL, D_EXPERT), by_expert),
                      pl.BlockSpec((1, D_EXPERT, D_MODEL), by_expert)],
            out_specs=pl.BlockSpec((ROW_TILE, D_MODEL // 2), lambda i, te, nv: (i, 0)),
            scratch_shapes=[pltpu.VMEM((D_MODEL, D_EXPERT), BF16),
                            pltpu.VMEM((D_MODEL, D_EXPERT), BF16),
                            pltpu.VMEM((D_EXPERT, D_MODEL), BF16)]),
        out_shape=jax.ShapeDtypeStruct((nrows, D_MODEL // 2), jnp.uint32),
        compiler_params=pltpu.CompilerParams(
            dimension_semantics=("arbitrary",), vmem_limit_bytes=VMEM_LIMIT),
        name="expert_mlp",
    )(tile_expert, n_valid, xs, w_gate, w_up, w_down)


def _combine_kernel(h_ref, y2_ref, route_ref, nfin_ref, out_ref):
    route = route_ref[...]
    w1, w2 = route[:, 0:1], route[:, 1:2]
    lo1, hi1 = _unpack_bf16_pairs(y2_ref[0])
    lo2, hi2 = _unpack_bf16_pairs(y2_ref[1])
    y = h_ref[...] + jnp.concatenate([w1 * lo1 + w2 * lo2, w1 * hi1 + w2 * hi2], axis=1)
    out_ref[...] = _rmsnorm(y, nfin_ref[...])


def _combine_call(h, y2, route, nfin, first_token):
    n = h.shape[0]
    first_block = first_token // COMBINE_TILE
    return pl.pallas_call(
        _combine_kernel,
        grid=(n // COMBINE_TILE,),
        in_specs=[pl.BlockSpec((COMBINE_TILE, D_MODEL), lambda i: (i, 0)),
                  pl.BlockSpec((2, COMBINE_TILE, D_MODEL // 2), lambda i: (0, first_block + i, 0)),
                  pl.BlockSpec((COMBINE_TILE, LANES), lambda i: (i, 0)),
                  pl.BlockSpec((1, D_MODEL), lambda i: (0, 0))],
        out_specs=pl.BlockSpec((COMBINE_TILE, D_MODEL), lambda i: (i, 0)),
        out_shape=jax.ShapeDtypeStruct((n, D_MODEL), F32),
        compiler_params=pltpu.CompilerParams(
            dimension_semantics=("arbitrary",), vmem_limit_bytes=VMEM_LIMIT),
        name="moe_combine",
    )(h, y2, route, nfin)


def kernel(x_prompt, x_sample, state_conv, state_gla, norm_mix, w_in, w_conv, w_alpha_up, b_alpha,
           gla_norm, w_conv_out, w_gla_out, w_o, norm_ffn, w_group_router, b_group_router,
           w_expert_router, b_expert_router, w_gate, w_up, w_down, norm_final):
    nbp, seq_p, _ = x_prompt.shape
    nbs, seq_s, _ = x_sample.shape
    assert norm_mix.shape[0] == 1, "single layer"
    assert seq_p % PROMPT_TILE == 0 and nbs % SAMPLE_SEQS == 0 and seq_s == 8

    weights = _mixer_weights(norm_mix, w_in, w_conv, w_alpha_up, b_alpha, gla_norm, w_conv_out,
                             w_gla_out, w_o, norm_ffn, w_group_router, b_group_router,
                             w_expert_router, b_expert_router)

    conv0 = jnp.zeros((nbp, CONV_WIDTH - 1, D_CONV), F32)
    gla0 = jnp.zeros((nbp, 1, 8, LANES), F32)
    h_p, xn_p, rt_p, rtt_p, conv_p, gla_p = _mixer_call(x_prompt, conv0, gla0, weights, prompt=True)
    h_s, xn_s, rt_s, rtt_s, conv_s, gla_s = _mixer_call(x_sample, state_conv[0], state_gla[0],
                                                        weights, prompt=False)
    ntp, nts = nbp * seq_p, nbs * seq_s
    ntok = ntp + nts
    assert ntp % COMBINE_TILE == 0 and nts % COMBINE_TILE == 0 and ntok % RANK_TILE == 0
    h_p, xn_p, rt_p = (a.reshape(ntp, a.shape[-1]) for a in (h_p, xn_p, rt_p))
    h_s, xn_s, rt_s = (a.reshape(nts, a.shape[-1]) for a in (h_s, xn_s, rt_s))

    eid = jnp.concatenate([rtt_p[:, 2:4, :].transpose(1, 0, 2).reshape(2, ntp),
                           rtt_s[:, 2:4, :].transpose(1, 0, 2).reshape(2, nts)],
                          axis=1).astype(jnp.int32)
    rank, cnt = _rank_call(eid)
    counts = cnt[:, 0].astype(jnp.int32)
    padded = ((counts + ROW_TILE - 1) // ROW_TILE) * ROW_TILE
    ends = jnp.cumsum(padded)
    starts = ends - padded
    expert_iota = jnp.arange(N_EXPERTS, dtype=jnp.int32)[:, None, None]
    pos = rank + jnp.sum(jnp.where(eid[None] == expert_iota, starts[:, None, None], 0), axis=0)
    n_tiles = (2 * ntok + N_EXPERTS * (ROW_TILE - 1)) // ROW_TILE
    n_rows = n_tiles * ROW_TILE
    tile_start = jnp.arange(n_tiles, dtype=jnp.int32) * ROW_TILE
    tile_expert = jnp.minimum(
        jnp.sum((tile_start[:, None] >= ends[None, :]).astype(jnp.int32), axis=1), N_EXPERTS - 1)
    n_valid = (ends[-1] // ROW_TILE).astype(jnp.int32)[None]
    last_valid_expert = jnp.sum(jnp.where(tile_start == ends[-1] - ROW_TILE, tile_expert, 0))
    tile_expert = jnp.where(tile_start < ends[-1], tile_expert, last_valid_expert)

    xs = _dispatch_rows(xn_p, xn_s, pos, n_rows)
    ys = _expert_call(tile_expert, n_valid, xs, w_gate[0], w_up[0], w_down[0])
    y2 = _return_rows(ys, pos.reshape(-1)).reshape(2, ntok, D_MODEL // 2)
    nfin = norm_final[None, :]
    y_prompt = _combine_call(h_p, y2, rt_p, nfin, 0).reshape(nbp, seq_p, D_MODEL)
    y_sample = _combine_call(h_s, y2, rt_s, nfin, ntp).reshape(nbs, seq_s, D_MODEL)
    return (y_prompt, y_sample, conv_p[None], gla_p[None], conv_s[None], gla_s[None])
```

```python
import functools

import jax
import jax.numpy as jnp
from jax import lax
from jax.experimental import pallas as pl
from jax.experimental.pallas import tpu as pltpu
from jax.experimental.pallas import tpu_sc as plsc

F32 = jnp.float32
BF16 = jnp.bfloat16

D_MODEL = 1024
D_CONV = 512
CONV_WIDTH = 3
GLA_HEADS = 4
GLA_DK = 128
GLA_DV = 256
GLA_RANK = 16
GLA_GATE_NORM = 16.0
GLA_CHUNK = 32
N_GROUPS = 4
EXPERTS_PER_GROUP = 8
N_EXPERTS = 32
D_EXPERT = 256
EPS = 1e-6

LANES = 128
OFF_CB, OFF_CC, OFF_CH = 0, 512, 1024
OFF_Q, OFF_K, OFF_V, OFF_G = 1536, 2048, 2560, 3584
OFF_GA, OFF_GB, OFF_AL = 4608, 5632, 6656
D_IN_PACKED = OFF_AL + LANES

PROMPT_TILE = 512
GLA_BLOCK = 256
SAMPLE_SEQS = 8
ROW_TILE = 256
SC_WINDOW = 128
COMBINE_TILE = 512
RANK_TILE = 512
VMEM_LIMIT = 56 * 1024 * 1024


def _rmsnorm(x, g):
    ms = jnp.mean(x * x, axis=-1, keepdims=True)
    return (x * lax.rsqrt(ms + EPS)) * g


def _split_bf16(x):
    hi = x.astype(BF16)
    lo = (x - hi.astype(F32)).astype(BF16)
    return hi, lo


def _pack_bf16_pairs(x):
    n = x.shape[1] // 2
    lo = lax.bitcast_convert_type(x[:, :n].astype(BF16).astype(F32), jnp.uint32)
    hi = lax.bitcast_convert_type(x[:, n:].astype(BF16).astype(F32), jnp.uint32)
    return lax.shift_right_logical(lo, jnp.uint32(16)) | hi


def _unpack_bf16_pairs(w):
    lo = lax.bitcast_convert_type(lax.shift_left(w, jnp.uint32(16)), F32)
    hi = lax.bitcast_convert_type(w & jnp.uint32(0xFFFF0000), F32)
    return lo, hi


def _dot(a, b):
    return jnp.dot(a, b, preferred_element_type=F32)


def _chunk_causal(n, chunk):
    r = lax.broadcasted_iota(jnp.int32, (n, n), 0)
    c = lax.broadcasted_iota(jnp.int32, (n, n), 1)
    shift = chunk.bit_length() - 1
    assert chunk == 1 << shift
    same = lax.shift_right_arithmetic(r, shift) == lax.shift_right_arithmetic(c, shift)
    return same & (c <= r)


def _route(xn2, wr_ref, br_ref):
    rows = xn2.shape[0]
    xh, xl = _split_bf16(xn2)
    part = _dot(xh, wr_ref[...]) + _dot(xl, wr_ref[...])
    lg = part[:, :LANES] + part[:, LANES:] + br_ref[...]
    lane_i = lax.broadcasted_iota(jnp.int32, (rows, LANES), 1)
    lane = lane_i.astype(F32)
    neg = -jnp.inf
    is_g = lane_i < N_GROUPS
    gm = jnp.max(jnp.where(is_g, lg, neg), axis=1, keepdims=True)
    gs = jnp.sum(jnp.where(is_g, jnp.exp(lg - gm), 0.0), axis=1, keepdims=True)
    g_top = 1.0 / gs
    g_idx = jnp.min(jnp.where(is_g & (lg == gm), lane, float(LANES)), axis=1, keepdims=True)
    e_group = lax.shift_right_arithmetic(lane_i - N_GROUPS, 3).astype(F32)
    sel = (lane_i >= N_GROUPS) & (lane_i < N_GROUPS + N_EXPERTS) & (e_group == g_idx)
    el = jnp.where(sel, lg, neg)
    e1 = jnp.max(el, axis=1, keepdims=True)
    i1 = jnp.min(jnp.where(sel & (el == e1), lane, float(LANES)), axis=1, keepdims=True)
    sel2 = sel & (lane != i1)
    el2 = jnp.where(sel2, lg, neg)
    e2 = jnp.max(el2, axis=1, keepdims=True)
    i2 = jnp.min(jnp.where(sel2 & (el2 == e2), lane, float(LANES)), axis=1, keepdims=True)
    d = jnp.exp(e2 - e1)
    w1 = g_top / (1.0 + d)
    w2 = (g_top * d) / (1.0 + d)
    id1 = i1 - float(N_GROUPS)
    id2 = i2 - float(N_GROUPS)
    return jnp.where(lane_i == 0, w1, jnp.where(lane_i == 1, w2, jnp.where(lane_i == 2, id1,
                     jnp.where(lane_i == 3, id2, 0.0))))


def _mixer_kernel(x_ref, conv_in_ref, gla_in_ref,
                  nmix_ref, w_ref, wgate_ref, walr_ref, wconv_ref, walpha_ref, balpha_ref, gnorm_ref,
                  wco_ref, wgo_ref, wo_ref, nffn_ref, wr_ref, br_ref,
                  h_ref, xn2_ref, route_ref, route_t_ref, conv_out_ref, gla_out_ref,
                  s_ref, ubuf_ref, *, prompt):
    if prompt:
        rows, chunk, nseq = PROMPT_TILE, GLA_CHUNK, 1
        x = x_ref[0]

        @pl.when(pl.program_id(1) == 0)
        def _():
            s_ref[...] = jnp.zeros_like(s_ref)
            ubuf_ref[...] = jnp.zeros_like(ubuf_ref)
    else:
        nseq, chunk = SAMPLE_SEQS, x_ref.shape[1]
        rows = nseq * chunk
        x = x_ref[...].reshape(rows, D_MODEL)
    nchunks = rows // chunk

    xn = _rmsnorm(x, nmix_ref[...]).astype(BF16)

    def proj(off, n):
        if off == OFF_AL:
            return _dot(xn, walr_ref[...])
        if off >= OFF_GA:
            return _dot(xn, wgate_ref[:, off - OFF_GA:off - OFF_GA + n])
        return _dot(xn, w_ref[:, off:off + n])

    alr = proj(OFF_AL, LANES).astype(BF16)
    lap = _dot(alr, walpha_ref[...])
    la = jax.nn.log_sigmoid(lap + balpha_ref[...]) * (1.0 / GLA_GATE_NORM)
    la_hi, la_lo = _split_bf16(la)
    blk = min(rows, GLA_BLOCK)
    nblk = rows // blk
    causal = _chunk_causal(blk, chunk)
    tri = jnp.where(causal, 1.0, 0.0).astype(BF16)
    cc = proj(OFF_CC, D_CONV)
    ch = proj(OFF_CH, D_CONV)
    cb = proj(OFF_CB, D_CONV)
    q = proj(OFF_Q, D_CONV)
    k = proj(OFF_K, D_CONV)
    b = jnp.concatenate([_dot(tri, la_hi[j * blk:(j + 1) * blk]) + _dot(tri, la_lo[j * blk:(j + 1) * blk])
                         for j in range(nblk)], axis=0)

    u = cc * ch
    wc = wconv_ref[...]
    if prompt:
        ubuf_ref[0, 8:8 + rows, :] = u
        z = (wc[0:1] * ubuf_ref[0, 6:6 + rows, :] + wc[1:2] * ubuf_ref[0, 7:7 + rows, :]
             + wc[2:3] * u)
        tail = u[rows - 2:rows, :]
        ubuf_ref[0, 6:8, :] = tail
        conv_out_ref[0] = tail
    else:
        u3 = u.reshape(nseq, chunk, D_CONV)
        ubuf_ref[:, 6:8, :] = conv_in_ref[...]
        ubuf_ref[:, 8:8 + chunk, :] = u3
        z3 = (wc[0:1] * ubuf_ref[:, 6:6 + chunk, :] + wc[1:2] * ubuf_ref[:, 7:7 + chunk, :]
              + wc[2:3] * u3)
        z = z3.reshape(rows, D_CONV)
        conv_out_ref[...] = ubuf_ref[:, 6 + chunk:8 + chunk, :]
    ya = _dot((cb * z).astype(BF16), wco_ref[...])
    v = proj(OFF_V, GLA_HEADS * GLA_DV)

    b3 = b.reshape(nchunks, chunk, D_CONV)
    bl3 = b3[:, chunk - 1:chunk, :]
    qin = (q * (GLA_DK ** -0.5)) * jnp.exp(b)
    kin = k * jnp.exp(-b)
    kst = (k.reshape(nchunks, chunk, D_CONV) * jnp.exp(bl3 - b3)).reshape(rows, D_CONV)
    pad = (-blk) % LANES
    blk_p = blk + pad
    cpb = blk // chunk

    def transposed(a):
        if pad:
            a = jnp.concatenate([a, jnp.zeros((pad, a.shape[1]), F32)], axis=0)
        return a.T

    blocks = range(nblk)
    b_ts = [transposed(b[j * blk:(j + 1) * blk]) for j in blocks]
    kst_ts = [transposed(kst[j * blk:(j + 1) * blk]) for j in blocks]
    col_chunk = lax.shift_right_arithmetic(
        lax.broadcasted_iota(jnp.int32, (GLA_DK, blk_p), 1), chunk.bit_length() - 1)
    g = proj(OFF_G, GLA_HEADS * GLA_DV)
    heads = range(GLA_HEADS)

    def v_of(j, hd):
        return v[j * blk:(j + 1) * blk, hd * GLA_DV:(hd + 1) * GLA_DV].astype(BF16)

    scs = [[lax.dot_general(qin[j * blk:(j + 1) * blk, hd * GLA_DK:(hd + 1) * GLA_DK].astype(BF16),
                            kin[j * blk:(j + 1) * blk, hd * GLA_DK:(hd + 1) * GLA_DK].astype(BF16),
                            (((1,), (1,)), ((), ())), preferred_element_type=F32)
            for hd in heads] for j in blocks]
    ga = proj(OFF_GA, D_MODEL)
    upds = []
    for j in blocks:
        upds.append([])
        for hd in heads:
            kst_h = kst_ts[j][hd * GLA_DK:(hd + 1) * GLA_DK, :]
            stacked = jnp.concatenate(
                [jnp.where(col_chunk == n, kst_h, 0.0).astype(BF16) for n in range(cpb)], axis=0)
            vh = v_of(j, hd)
            vh_p = jnp.concatenate([vh, jnp.zeros((pad, GLA_DV), BF16)], axis=0) if pad else vh
            upds[j].append(_dot(stacked, vh_p))
    o_intras = [[_dot(jnp.where(causal, scs[j][hd], 0.0).astype(BF16), v_of(j, hd))
                 for hd in heads] for j in blocks]
    gb = proj(OFF_GB, D_MODEL)
    merge_a = jax.nn.sigmoid(ga) * ya
    gate_b = jax.nn.sigmoid(gb)
    gated = []
    for hd in heads:
        c0, v0 = hd * GLA_DK, hd * GLA_DV
        s_run = s_ref[hd] if prompt else None
        o_inter = []
        for j in blocks:
            for n in range(cpb):
                r0 = j * blk + n * chunk
                s_prev = s_run if prompt else gla_in_ref[n, hd]
                qb = qin[r0:r0 + chunk, c0:c0 + GLA_DK].astype(BF16)
                o_inter.append(_dot(qb, s_prev.astype(BF16)))
                last = n * chunk + chunk - 1
                dcol = jnp.exp(b_ts[j][c0:c0 + GLA_DK, last:last + 1])
                s_new = s_prev * dcol + upds[j][hd][n * GLA_DK:(n + 1) * GLA_DK, :]
                if prompt:
                    s_run = s_new
                else:
                    gla_out_ref[n, hd] = s_new
        if prompt:
            s_ref[hd] = s_run
            gla_out_ref[0, hd] = s_run
        o_intra = jnp.concatenate([o_intras[j][hd] for j in blocks], axis=0)
        oh = _rmsnorm(o_intra + jnp.concatenate(o_inter, axis=0), gnorm_ref[...])
        gh = g[:, v0:v0 + GLA_DV]
        gated.append(oh * (gh * jax.nn.sigmoid(gh)))
    yb = _dot(jnp.concatenate(gated, axis=1).astype(BF16), wgo_ref[...])

    m = merge_a + gate_b * yb
    hh = x + _dot(m.astype(BF16), wo_ref[...])

    xn2 = _rmsnorm(hh, nffn_ref[...])
    route = _route(xn2, wr_ref, br_ref)
    rpad = (-rows) % LANES
    route_p = jnp.concatenate([route, jnp.zeros((rpad, LANES), F32)], axis=0) if rpad else route
    route_t_ref[0] = route_p.T[:8, :rows]
    if prompt:
        h_ref[0] = hh
        xn2_ref[0] = _pack_bf16_pairs(xn2)
        route_ref[0] = route
    else:
        h_ref[...] = hh.reshape(nseq, chunk, D_MODEL)
        xn2_ref[...] = _pack_bf16_pairs(xn2).reshape(nseq, chunk, D_MODEL // 2)
        route_ref[...] = route.reshape(nseq, chunk, LANES)


def _mixer_weights(norm_mix, w_in, w_conv, w_alpha_up, b_alpha, gla_norm, w_conv_out, w_gla_out, w_o,
                   norm_ffn, w_group_router, b_group_router, w_expert_router, b_expert_router):
    wi = w_in[0]
    w_main = wi[:, :OFF_GA].astype(BF16)
    w_gates = wi[:, OFF_GA + GLA_RANK:].astype(BF16)
    w_alr = jnp.concatenate([wi[:, OFF_GA:OFF_GA + GLA_RANK],
                             jnp.zeros((D_MODEL, LANES - GLA_RANK), F32)], axis=1).astype(BF16)
    walpha = jnp.concatenate(
        [w_alpha_up[0], jnp.zeros((LANES - GLA_RANK, D_CONV), F32)], axis=0).astype(BF16)
    wr = jnp.concatenate(
        [w_group_router[0], w_expert_router[0],
         jnp.zeros((D_MODEL, LANES - N_GROUPS - N_EXPERTS), F32)], axis=1)
    wr_hi = wr.astype(BF16)
    wr_split = jnp.concatenate([wr_hi, (wr - wr_hi.astype(F32)).astype(BF16)], axis=1)
    br = jnp.concatenate([b_group_router[0], b_expert_router[0],
                          jnp.zeros((LANES - N_GROUPS - N_EXPERTS,), F32)])[None, :]
    return (norm_mix, w_main, w_gates, w_alr, w_conv[0], walpha, b_alpha, gla_norm,
            w_conv_out[0].astype(BF16), w_gla_out[0].astype(BF16), w_o[0].astype(BF16),
            norm_ffn, wr_split, br)


def _const_spec(shape):
    nd = len(shape)
    return pl.BlockSpec(shape, lambda *_: (0,) * nd, pipeline_mode=pl.Buffered(1))


def _mixer_call(x, conv_state, gla_state, weights, *, prompt):
    nb, seq, _ = x.shape
    if prompt:
        grid = (nb, seq // PROMPT_TILE)
        tok = lambda last: pl.BlockSpec((1, PROMPT_TILE, last), lambda b, i: (b, i, 0))
        conv_spec = pl.BlockSpec((1, CONV_WIDTH - 1, D_CONV), lambda b, i: (b, 0, 0))
        gla_spec = pl.BlockSpec((1, GLA_HEADS, GLA_DK, GLA_DV), lambda b, i: (b, 0, 0, 0))
        gla_in_spec = pl.BlockSpec((1, 1, 8, LANES), lambda b, i: (b, 0, 0, 0))
        scratch = [pltpu.VMEM((GLA_HEADS, GLA_DK, GLA_DV), F32),
                   pltpu.VMEM((1, PROMPT_TILE + 8, D_CONV), F32)]
        route_t_shape = (nb, 8, seq)
        route_t_spec = pl.BlockSpec((1, 8, PROMPT_TILE), lambda b, i: (b, 0, i))
    else:
        grid = (nb // SAMPLE_SEQS, 1)
        tok = lambda last: pl.BlockSpec((SAMPLE_SEQS, seq, last), lambda b, i: (b, 0, 0))
        conv_spec = pl.BlockSpec((SAMPLE_SEQS, CONV_WIDTH - 1, D_CONV), lambda b, i: (b, 0, 0))
        gla_spec = pl.BlockSpec((SAMPLE_SEQS, GLA_HEADS, GLA_DK, GLA_DV), lambda b, i: (b, 0, 0, 0))
        gla_in_spec = gla_spec
        scratch = [pltpu.VMEM((1, 8, LANES), F32),
                   pltpu.VMEM((SAMPLE_SEQS, 8 + seq, D_CONV), F32)]
        route_t_shape = (nb // SAMPLE_SEQS, 8, SAMPLE_SEQS * seq)
        route_t_spec = pl.BlockSpec((1, 8, SAMPLE_SEQS * seq), lambda b, i: (b, 0, 0))
    out_shape = (jax.ShapeDtypeStruct((nb, seq, D_MODEL), F32),
                 jax.ShapeDtypeStruct((nb, seq, D_MODEL // 2), jnp.uint32),
                 jax.ShapeDtypeStruct((nb, seq, LANES), F32),
                 jax.ShapeDtypeStruct(route_t_shape, F32),
                 jax.ShapeDtypeStruct((nb, CONV_WIDTH - 1, D_CONV), F32),
                 jax.ShapeDtypeStruct((nb, GLA_HEADS, GLA_DK, GLA_DV), F32))
    return pl.pallas_call(
        functools.partial(_mixer_kernel, prompt=prompt),
        grid=grid,
        in_specs=[tok(D_MODEL), conv_spec, gla_in_spec] + [_const_spec(w.shape) for w in weights],
        out_specs=(tok(D_MODEL), tok(D_MODEL // 2), tok(LANES), route_t_spec, conv_spec, gla_spec),
        out_shape=out_shape,
        scratch_shapes=scratch,
        compiler_params=pltpu.CompilerParams(
            dimension_semantics=("arbitrary", "arbitrary"), vmem_limit_bytes=VMEM_LIMIT),
        name="mixer_prompt" if prompt else "mixer_sample",
    )(x, conv_state, gla_state, *weights)


def _rank_kernel(eid_ref, rank_ref, cnt_ref, carry_ref):
    @pl.when(pl.program_id(0) == 0)
    def _():
        carry_ref[...] = jnp.zeros_like(carry_ref)

    n = eid_ref.shape[1]
    e_iota = lax.broadcasted_iota(jnp.int32, (N_EXPERTS, n), 0)
    oh0 = jnp.where(e_iota == eid_ref[0:1, :], 1.0, 0.0)
    oh1 = jnp.where(e_iota == eid_ref[1:2, :], 1.0, 0.0)
    cnt = oh0 + oh1
    r = lax.broadcasted_iota(jnp.int32, (n, n), 0)
    c = lax.broadcasted_iota(jnp.int32, (n, n), 1)
    before = jnp.where(r < c, 1.0, 0.0).astype(BF16)
    base = carry_ref[...] + _dot(cnt.astype(BF16), before)
    rank0 = jnp.sum(oh0 * base, axis=0, keepdims=True)
    rank1 = jnp.sum(oh1 * base, axis=0, keepdims=True)
    rank_ref[...] = jnp.concatenate([rank0, rank1], axis=0).astype(jnp.int32)
    carry_ref[...] = carry_ref[...] + jnp.sum(cnt, axis=1, keepdims=True)
    cnt_ref[...] = jnp.broadcast_to(carry_ref[...], cnt_ref.shape)


def _rank_call(eid):
    ntok = eid.shape[1]
    return pl.pallas_call(
        _rank_kernel,
        grid=(ntok // RANK_TILE,),
        in_specs=[pl.BlockSpec((2, RANK_TILE), lambda i: (0, i))],
        out_specs=(pl.BlockSpec((2, RANK_TILE), lambda i: (0, i)),
                   pl.BlockSpec((N_EXPERTS, LANES), lambda i: (0, 0))),
        out_shape=(jax.ShapeDtypeStruct((2, ntok), jnp.int32),
                   jax.ShapeDtypeStruct((N_EXPERTS, LANES), F32)),
        scratch_shapes=[pltpu.VMEM((N_EXPERTS, 1), F32)],
        compiler_params=pltpu.CompilerParams(dimension_semantics=("arbitrary",)),
        name="expert_rank",
    )(eid)


def _sc_mesh():
    return plsc.VectorSubcoreMesh(core_axis_name="c", subcore_axis_name="s")


def _sc_worker_windows(n_windows, body):
    info = plsc.get_sparse_core_info()
    n_workers = info.num_cores * info.num_subcores
    wid = lax.axis_index("s") * info.num_cores + lax.axis_index("c")

    @pl.loop(0, pl.cdiv(n_windows, n_workers))
    def _(j):
        win = wid + n_workers * j

        @pl.when(win < n_windows)
        def _():
            body(win)


def _dispatch_rows(x_prompt_rows, x_sample_rows, pos, n_rows):
    ntp, nts = x_prompt_rows.shape[0], x_sample_rows.shape[0]
    width, dtype = x_prompt_rows.shape[1], x_prompt_rows.dtype
    nwp, nws = ntp // SC_WINDOW, nts // SC_WINDOW
    pos3 = pos.reshape(2, nwp + nws, SC_WINDOW).transpose(1, 0, 2)

    @functools.partial(pl.kernel, mesh=_sc_mesh(), name="moe_dispatch",
                       out_type=jax.ShapeDtypeStruct((n_rows, width), dtype),
                       scratch_types=[pltpu.VMEM((2, SC_WINDOW), jnp.int32),
                                      pltpu.VMEM((SC_WINDOW, width), dtype)])
    def run(xp_hbm, xs_hbm, pos_hbm, out_hbm, idx_v, buf):
        def window(win):
            pltpu.sync_copy(pos_hbm.at[win], idx_v)

            @pl.when(win < nwp)
            def _():
                r0 = pl.multiple_of(win * SC_WINDOW, SC_WINDOW)
                pltpu.sync_copy(xp_hbm.at[pl.ds(r0, SC_WINDOW)], buf)

            @pl.when(win >= nwp)
            def _():
                r0 = pl.multiple_of((win - nwp) * SC_WINDOW, SC_WINDOW)
                pltpu.sync_copy(xs_hbm.at[pl.ds(r0, SC_WINDOW)], buf)

            pltpu.sync_copy(buf, out_hbm.at[idx_v.at[0]])
            pltpu.sync_copy(buf, out_hbm.at[idx_v.at[1]])

        _sc_worker_windows(nwp + nws, window)

    return run(x_prompt_rows, x_sample_rows, pos3)


def _return_rows(ys, idx):
    n_out, width = idx.shape[0], ys.shape[1]
    n_windows = n_out // SC_WINDOW

    @functools.partial(pl.kernel, mesh=_sc_mesh(), name="moe_return",
                       out_type=jax.ShapeDtypeStruct((n_out, width), ys.dtype),
                       scratch_types=[pltpu.VMEM((SC_WINDOW,), jnp.int32),
                                      pltpu.VMEM((SC_WINDOW, width), ys.dtype)])
    def run(ys_hbm, idx_hbm, out_hbm, idx_v, buf):
        def window(win):
            pltpu.sync_copy(idx_hbm.at[win], idx_v)
            pltpu.sync_copy(ys_hbm.at[idx_v], buf)
            r0 = pl.multiple_of(win * SC_WINDOW, SC_WINDOW)
            pltpu.sync_copy(buf, out_hbm.at[pl.ds(r0, SC_WINDOW)])

        _sc_worker_windows(n_windows, window)

    return run(ys, idx.reshape(n_windows, SC_WINDOW))


def _expert_kernel(start_ref, ntile_ref, xs_hbm, wg_ref, wu_ref, wd_ref, out_hbm,
                   xbuf, obuf, wg_s, wu_s, wd_s, sem_in, sem_out):
    e = pl.program_id(0)
    nt = ntile_ref[e]
    base = start_ref[e]

    def rows_of(t):
        return pl.ds(pl.multiple_of(base + t * ROW_TILE, ROW_TILE), ROW_TILE)

    def in_copy(t, slot):
        return pltpu.make_async_copy(xs_hbm.at[rows_of(t)], xbuf.at[slot], sem_in.at[slot])

    def out_copy(t, slot):
        return pltpu.make_async_copy(obuf.at[slot], out_hbm.at[rows_of(t)], sem_out.at[slot])

    @pl.when(nt > 0)
    def _():
        in_copy(0, 0).start()
        wg_s[...] = wg_ref[0].astype(BF16)
        wu_s[...] = wu_ref[0].astype(BF16)
        wd_s[...] = wd_ref[0].astype(BF16)

        def tile(t, carry):
            slot = lax.rem(t, 2)
            in_copy(t, slot).wait()

            @pl.when(t + 1 < nt)
            def _():
                in_copy(t + 1, 1 - slot).start()

            @pl.when(t >= 2)
            def _():
                out_copy(t - 2, slot).wait()

            lo, hi = _unpack_bf16_pairs(xbuf[slot])
            half = D_MODEL // 2
            sub = ROW_TILE // 2
            parts = []
            for r in (0, sub):
                xl, xh = lo[r:r + sub].astype(BF16), hi[r:r + sub].astype(BF16)
                gt = _dot(xl, wg_s[:half, :]) + _dot(xh, wg_s[half:, :])
                up = _dot(xl, wu_s[:half, :]) + _dot(xh, wu_s[half:, :])
                parts.append((gt, up))
            for r, (gt, up) in zip((0, sub), parts):
                act = (gt * jax.nn.sigmoid(gt)) * up
                obuf[slot, r:r + sub, :] = _pack_bf16_pairs(_dot(act.astype(BF16), wd_s[...]))
            out_copy(t, slot).start()
            return carry

        lax.fori_loop(0, nt, tile, 0)

        @pl.when(nt >= 2)
        def _():
            out_copy(nt - 2, lax.rem(nt, 2)).wait()
        out_copy(nt - 1, lax.rem(nt - 1, 2)).wait()


def _expert_call(row_start, n_tiles_per_expert, xs, w_gate, w_up, w_down):
    nrows = xs.shape[0]
    by_expert = lambda e, st, nt: (e, 0, 0)
    half_row = D_MODEL // 2
    return pl.pallas_call(
        _expert_kernel,
        grid_spec=pltpu.PrefetchScalarGridSpec(
            num_scalar_prefetch=2,
            grid=(N_EXPERTS,),
            in_specs=[pl.BlockSpec(memory_space=pl.ANY),
                      pl.BlockSpec((1, D_MODEL, D_EXPERT), by_expert),
                      pl.BlockSpec((1, D_MODEL, D_EXPERT), by_expert),
                      pl.BlockSpec((1, D_EXPERT, D_MODEL), by_expert)],
            out_specs=pl.BlockSpec(memory_space=pl.ANY),
            scratch_shapes=[pltpu.VMEM((2, ROW_TILE, half_row), jnp.uint32),
                            pltpu.VMEM((2, ROW_TILE, half_row), jnp.uint32),
                            pltpu.VMEM((D_MODEL, D_EXPERT), BF16),
                            pltpu.VMEM((D_MODEL, D_EXPERT), BF16),
                            pltpu.VMEM((D_EXPERT, D_MODEL), BF16),
                            pltpu.SemaphoreType.DMA((2,)),
                            pltpu.SemaphoreType.DMA((2,))]),
        out_shape=jax.ShapeDtypeStruct((nrows, half_row), jnp.uint32),
        compiler_params=pltpu.CompilerParams(
            dimension_semantics=("arbitrary",), vmem_limit_bytes=VMEM_LIMIT),
        name="expert_mlp",
    )(row_start, n_tiles_per_expert, xs, w_gate, w_up, w_down)


def _combine_kernel(h_ref, y2_ref, route_ref, nfin_ref, out_ref):
    route = route_ref[...]
    w1, w2 = route[:, 0:1], route[:, 1:2]
    lo1, hi1 = _unpack_bf16_pairs(y2_ref[0])
    lo2, hi2 = _unpack_bf16_pairs(y2_ref[1])
    y = h_ref[...] + jnp.concatenate([w1 * lo1 + w2 * lo2, w1 * hi1 + w2 * hi2], axis=1)
    out_ref[...] = _rmsnorm(y, nfin_ref[...])


def _combine_call(h, y2, route, nfin, first_token):
    n = h.shape[0]
    first_block = first_token // COMBINE_TILE
    return pl.pallas_call(
        _combine_kernel,
        grid=(n // COMBINE_TILE,),
        in_specs=[pl.BlockSpec((COMBINE_TILE, D_MODEL), lambda i: (i, 0)),
                  pl.BlockSpec((2, COMBINE_TILE, D_MODEL // 2), lambda i: (0, first_block + i, 0)),
                  pl.BlockSpec((COMBINE_TILE, LANES), lambda i: (i, 0)),
                  pl.BlockSpec((1, D_MODEL), lambda i: (0, 0))],
        out_specs=pl.BlockSpec((COMBINE_TILE, D_MODEL), lambda i: (i, 0)),
        out_shape=jax.ShapeDtypeStruct((n, D_MODEL), F32),
        compiler_params=pltpu.CompilerParams(
            dimension_semantics=("arbitrary",), vmem_limit_bytes=VMEM_LIMIT),
        name="moe_combine",
    )(h, y2, route, nfin)


def kernel(x_prompt, x_sample, state_conv, state_gla, norm_mix, w_in, w_conv, w_alpha_up, b_alpha,
           gla_norm, w_conv_out, w_gla_out, w_o, norm_ffn, w_group_router, b_group_router,
           w_expert_router, b_expert_router, w_gate, w_up, w_down, norm_final):
    nbp, seq_p, _ = x_prompt.shape
    nbs, seq_s, _ = x_sample.shape
    assert norm_mix.shape[0] == 1, "single layer"
    assert seq_p % PROMPT_TILE == 0 and nbs % SAMPLE_SEQS == 0 and seq_s == 8

    weights = _mixer_weights(norm_mix, w_in, w_conv, w_alpha_up, b_alpha, gla_norm, w_conv_out,
                             w_gla_out, w_o, norm_ffn, w_group_router, b_group_router,
                             w_expert_router, b_expert_router)

    conv0 = jnp.zeros((nbp, CONV_WIDTH - 1, D_CONV), F32)
    gla0 = jnp.zeros((nbp, 1, 8, LANES), F32)
    h_p, xn_p, rt_p, rtt_p, conv_p, gla_p = _mixer_call(x_prompt, conv0, gla0, weights, prompt=True)
    h_s, xn_s, rt_s, rtt_s, conv_s, gla_s = _mixer_call(x_sample, state_conv[0], state_gla[0],
                                                        weights, prompt=False)
    ntp, nts = nbp * seq_p, nbs * seq_s
    ntok = ntp + nts
    assert ntp % COMBINE_TILE == 0 and nts % COMBINE_TILE == 0 and ntok % RANK_TILE == 0
    h_p, xn_p, rt_p = (a.reshape(ntp, a.shape[-1]) for a in (h_p, xn_p, rt_p))
    h_s, xn_s, rt_s = (a.reshape(nts, a.shape[-1]) for a in (h_s, xn_s, rt_s))

    eid = jnp.concatenate([rtt_p[:, 2:4, :].transpose(1, 0, 2).reshape(2, ntp),
                           rtt_s[:, 2:4, :].transpose(1, 0, 2).reshape(2, nts)],
                          axis=1).astype(jnp.int32)
    rank, cnt = _rank_call(eid)
    counts = cnt[:, 0].astype(jnp.int32)
    padded = ((counts + ROW_TILE - 1) // ROW_TILE) * ROW_TILE
    ends = jnp.cumsum(padded)
    starts = ends - padded
    expert_iota = jnp.arange(N_EXPERTS, dtype=jnp.int32)[:, None, None]
    pos = rank + jnp.sum(jnp.where(eid[None] == expert_iota, starts[:, None, None], 0), axis=0)
    n_tiles = (2 * ntok + N_EXPERTS * (ROW_TILE - 1)) // ROW_TILE
    n_rows = n_tiles * ROW_TILE

    xs = _dispatch_rows(xn_p, xn_s, pos, n_rows)
    ys = _expert_call(starts, padded // ROW_TILE, xs, w_gate[0], w_up[0], w_down[0])
    y2 = _return_rows(ys, pos.reshape(-1)).reshape(2, ntok, D_MODEL // 2)
    nfin = norm_final[None, :]
    y_prompt = _combine_call(h_p, y2, rt_p, nfin, 0).reshape(nbp, seq_p, D_MODEL)
    y_sample = _combine_call(h_s, y2, rt_s, nfin, ntp).reshape(nbs, seq_s, D_MODEL)
    return (y_prompt, y_sample, conv_p[None], gla_p[None], conv_s[None], gla_s[None])
```

```python
import functools

import jax
import jax.numpy as jnp
from jax import lax
from jax.experimental import pallas as pl
from jax.experimental.pallas import tpu as pltpu
from jax.experimental.pallas import tpu_sc as plsc

F32 = jnp.float32
BF16 = jnp.bfloat16

D_MODEL = 1024
D_CONV = 512
CONV_WIDTH = 3
GLA_HEADS = 4
GLA_DK = 128
GLA_DV = 256
GLA_RANK = 16
GLA_GATE_NORM = 16.0
GLA_CHUNK = 32
N_GROUPS = 4
EXPERTS_PER_GROUP = 8
N_EXPERTS = 32
D_EXPERT = 256
EPS = 1e-6

LANES = 128
OFF_CB, OFF_CC, OFF_CH = 0, 512, 1024
OFF_Q, OFF_K, OFF_V, OFF_G = 1536, 2048, 2560, 3584
OFF_GA, OFF_GB, OFF_AL = 4608, 5632, 6656
D_IN_PACKED = OFF_AL + LANES

PROMPT_TILE = 512
GLA_BLOCK = 256
SAMPLE_SEQS = 8
ROW_TILE = 256
EXPERT_RING = 4
SC_WINDOW = 128
COMBINE_TILE = 512
RANK_TILE = 512
VMEM_LIMIT = 56 * 1024 * 1024


def _rmsnorm(x, g):
    ms = jnp.mean(x * x, axis=-1, keepdims=True)
    return (x * lax.rsqrt(ms + EPS)) * g


def _split_bf16(x):
    hi = x.astype(BF16)
    lo = (x - hi.astype(F32)).astype(BF16)
    return hi, lo


def _pack_bf16_pairs(x):
    n = x.shape[1] // 2
    lo = lax.bitcast_convert_type(x[:, :n].astype(BF16).astype(F32), jnp.uint32)
    hi = lax.bitcast_convert_type(x[:, n:].astype(BF16).astype(F32), jnp.uint32)
    return lax.shift_right_logical(lo, jnp.uint32(16)) | hi


def _unpack_bf16_pairs(w):
    lo = lax.bitcast_convert_type(lax.shift_left(w, jnp.uint32(16)), F32)
    hi = lax.bitcast_convert_type(w & jnp.uint32(0xFFFF0000), F32)
    return lo, hi


def _dot(a, b):
    return jnp.dot(a, b, preferred_element_type=F32)


def _chunk_causal(n, chunk):
    r = lax.broadcasted_iota(jnp.int32, (n, n), 0)
    c = lax.broadcasted_iota(jnp.int32, (n, n), 1)
    shift = chunk.bit_length() - 1
    assert chunk == 1 << shift
    same = lax.shift_right_arithmetic(r, shift) == lax.shift_right_arithmetic(c, shift)
    return same & (c <= r)


def _route(xn2, wr_ref, br_ref):
    rows = xn2.shape[0]
    xh, xl = _split_bf16(xn2)
    part = _dot(xh, wr_ref[...]) + _dot(xl, wr_ref[...])
    lg = part[:, :LANES] + part[:, LANES:] + br_ref[...]
    lane_i = lax.broadcasted_iota(jnp.int32, (rows, LANES), 1)
    lane = lane_i.astype(F32)
    neg = -jnp.inf
    is_g = lane_i < N_GROUPS
    gm = jnp.max(jnp.where(is_g, lg, neg), axis=1, keepdims=True)
    gs = jnp.sum(jnp.where(is_g, jnp.exp(lg - gm), 0.0), axis=1, keepdims=True)
    g_top = 1.0 / gs
    g_idx = jnp.min(jnp.where(is_g & (lg == gm), lane, float(LANES)), axis=1, keepdims=True)
    e_group = lax.shift_right_arithmetic(lane_i - N_GROUPS, 3).astype(F32)
    sel = (lane_i >= N_GROUPS) & (lane_i < N_GROUPS + N_EXPERTS) & (e_group == g_idx)
    el = jnp.where(sel, lg, neg)
    e1 = jnp.max(el, axis=1, keepdims=True)
    i1 = jnp.min(jnp.where(sel & (el == e1), lane, float(LANES)), axis=1, keepdims=True)
    sel2 = sel & (lane != i1)
    el2 = jnp.where(sel2, lg, neg)
    e2 = jnp.max(el2, axis=1, keepdims=True)
    i2 = jnp.min(jnp.where(sel2 & (el2 == e2), lane, float(LANES)), axis=1, keepdims=True)
    d = jnp.exp(e2 - e1)
    w1 = g_top / (1.0 + d)
    w2 = (g_top * d) / (1.0 + d)
    id1 = i1 - float(N_GROUPS)
    id2 = i2 - float(N_GROUPS)
    return jnp.where(lane_i == 0, w1, jnp.where(lane_i == 1, w2, jnp.where(lane_i == 2, id1,
                     jnp.where(lane_i == 3, id2, 0.0))))


def _mixer_kernel(x_ref, conv_in_ref, gla_in_ref,
                  nmix_ref, w_ref, wgate_ref, walr_ref, wconv_ref, walpha_ref, balpha_ref, gnorm_ref,
                  wco_ref, wgo_ref, wo_ref, nffn_ref, wr_ref, br_ref,
                  h_ref, xn2_ref, route_ref, route_t_ref, conv_out_ref, gla_out_ref,
                  s_ref, ubuf_ref, *, prompt):
    if prompt:
        rows, chunk, nseq = PROMPT_TILE, GLA_CHUNK, 1
        x = x_ref[0]

        @pl.when(pl.program_id(1) == 0)
        def _():
            s_ref[...] = jnp.zeros_like(s_ref)
            ubuf_ref[...] = jnp.zeros_like(ubuf_ref)
    else:
        nseq, chunk = SAMPLE_SEQS, x_ref.shape[1]
        rows = nseq * chunk
        x = x_ref[...].reshape(rows, D_MODEL)
    nchunks = rows // chunk

    xn = _rmsnorm(x, nmix_ref[...]).astype(BF16)

    def proj(off, n):
        if off == OFF_AL:
            return _dot(xn, walr_ref[...])
        if off >= OFF_GA:
            return _dot(xn, wgate_ref[:, off - OFF_GA:off - OFF_GA + n])
        return _dot(xn, w_ref[:, off:off + n])

    alr = proj(OFF_AL, LANES).astype(BF16)
    lap = _dot(alr, walpha_ref[...])
    la = jax.nn.log_sigmoid(lap + balpha_ref[...]) * (1.0 / GLA_GATE_NORM)
    la_hi, la_lo = _split_bf16(la)
    blk = min(rows, GLA_BLOCK)
    nblk = rows // blk
    causal = _chunk_causal(blk, chunk)
    tri = jnp.where(causal, 1.0, 0.0).astype(BF16)
    cc = proj(OFF_CC, D_CONV)
    ch = proj(OFF_CH, D_CONV)
    cb = proj(OFF_CB, D_CONV)
    q = proj(OFF_Q, D_CONV)
    k = proj(OFF_K, D_CONV)
    b = jnp.concatenate([_dot(tri, la_hi[j * blk:(j + 1) * blk]) + _dot(tri, la_lo[j * blk:(j + 1) * blk])
                         for j in range(nblk)], axis=0)

    u = cc * ch
    wc = wconv_ref[...]
    if prompt:
        ubuf_ref[0, 8:8 + rows, :] = u
        z = (wc[0:1] * ubuf_ref[0, 6:6 + rows, :] + wc[1:2] * ubuf_ref[0, 7:7 + rows, :]
             + wc[2:3] * u)
        tail = u[rows - 2:rows, :]
        ubuf_ref[0, 6:8, :] = tail
        conv_out_ref[0] = tail
    else:
        u3 = u.reshape(nseq, chunk, D_CONV)
        ubuf_ref[:, 6:8, :] = conv_in_ref[...]
        ubuf_ref[:, 8:8 + chunk, :] = u3
        z3 = (wc[0:1] * ubuf_ref[:, 6:6 + chunk, :] + wc[1:2] * ubuf_ref[:, 7:7 + chunk, :]
              + wc[2:3] * u3)
        z = z3.reshape(rows, D_CONV)
        conv_out_ref[...] = ubuf_ref[:, 6 + chunk:8 + chunk, :]
    ya = _dot((cb * z).astype(BF16), wco_ref[...])
    v = proj(OFF_V, GLA_HEADS * GLA_DV)

    b3 = b.reshape(nchunks, chunk, D_CONV)
    bl3 = b3[:, chunk - 1:chunk, :]
    qin = (q * (GLA_DK ** -0.5)) * jnp.exp(b)
    kin = k * jnp.exp(-b)
    kst = (k.reshape(nchunks, chunk, D_CONV) * jnp.exp(bl3 - b3)).reshape(rows, D_CONV)
    pad = (-blk) % LANES
    blk_p = blk + pad
    cpb = blk // chunk

    def transposed(a):
        if pad:
            a = jnp.concatenate([a, jnp.zeros((pad, a.shape[1]), F32)], axis=0)
        return a.T

    blocks = range(nblk)
    b_ts = [transposed(b[j * blk:(j + 1) * blk]) for j in blocks]
    kst_ts = [transposed(kst[j * blk:(j + 1) * blk]) for j in blocks]
    col_chunk = lax.shift_right_arithmetic(
        lax.broadcasted_iota(jnp.int32, (GLA_DK, blk_p), 1), chunk.bit_length() - 1)
    g = proj(OFF_G, GLA_HEADS * GLA_DV)
    heads = range(GLA_HEADS)

    def v_of(j, hd):
        return v[j * blk:(j + 1) * blk, hd * GLA_DV:(hd + 1) * GLA_DV].astype(BF16)

    scs = [[lax.dot_general(qin[j * blk:(j + 1) * blk, hd * GLA_DK:(hd + 1) * GLA_DK].astype(BF16),
                            kin[j * blk:(j + 1) * blk, hd * GLA_DK:(hd + 1) * GLA_DK].astype(BF16),
                            (((1,), (1,)), ((), ())), preferred_element_type=F32)
            for hd in heads] for j in blocks]
    ga = proj(OFF_GA, D_MODEL)
    upds = []
    for j in blocks:
        upds.append([])
        for hd in heads:
            kst_h = kst_ts[j][hd * GLA_DK:(hd + 1) * GLA_DK, :]
            stacked = jnp.concatenate(
                [jnp.where(col_chunk == n, kst_h, 0.0).astype(BF16) for n in range(cpb)], axis=0)
            vh = v_of(j, hd)
            vh_p = jnp.concatenate([vh, jnp.zeros((pad, GLA_DV), BF16)], axis=0) if pad else vh
            upds[j].append(_dot(stacked, vh_p))
    o_intras = [[_dot(jnp.where(causal, scs[j][hd], 0.0).astype(BF16), v_of(j, hd))
                 for hd in heads] for j in blocks]
    gb = proj(OFF_GB, D_MODEL)
    merge_a = jax.nn.sigmoid(ga) * ya
    gate_b = jax.nn.sigmoid(gb)
    gated = []
    for hd in heads:
        c0, v0 = hd * GLA_DK, hd * GLA_DV
        s_run = s_ref[hd] if prompt else None
        o_inter = []
        for j in blocks:
            for n in range(cpb):
                r0 = j * blk + n * chunk
                s_prev = s_run if prompt else gla_in_ref[n, hd]
                qb = qin[r0:r0 + chunk, c0:c0 + GLA_DK].astype(BF16)
                o_inter.append(_dot(qb, s_prev.astype(BF16)))
                last = n * chunk + chunk - 1
                dcol = jnp.exp(b_ts[j][c0:c0 + GLA_DK, last:last + 1])
                s_new = s_prev * dcol + upds[j][hd][n * GLA_DK:(n + 1) * GLA_DK, :]
                if prompt:
                    s_run = s_new
                else:
                    gla_out_ref[n, hd] = s_new
        if prompt:
            s_ref[hd] = s_run
            gla_out_ref[0, hd] = s_run
        o_intra = jnp.concatenate([o_intras[j][hd] for j in blocks], axis=0)
        oh = _rmsnorm(o_intra + jnp.concatenate(o_inter, axis=0), gnorm_ref[...])
        gh = g[:, v0:v0 + GLA_DV]
        gated.append(oh * (gh * jax.nn.sigmoid(gh)))
    yb = _dot(jnp.concatenate(gated, axis=1).astype(BF16), wgo_ref[...])

    m = merge_a + gate_b * yb
    hh = x + _dot(m.astype(BF16), wo_ref[...])

    xn2 = _rmsnorm(hh, nffn_ref[...])
    route = _route(xn2, wr_ref, br_ref)
    rpad = (-rows) % LANES
    route_p = jnp.concatenate([route, jnp.zeros((rpad, LANES), F32)], axis=0) if rpad else route
    route_t_ref[0] = route_p.T[:8, :rows]
    if prompt:
        h_ref[0] = hh
        xn2_ref[0] = _pack_bf16_pairs(xn2)
        route_ref[0] = route
    else:
        h_ref[...] = hh.reshape(nseq, chunk, D_MODEL)
        xn2_ref[...] = _pack_bf16_pairs(xn2).reshape(nseq, chunk, D_MODEL // 2)
        route_ref[...] = route.reshape(nseq, chunk, LANES)


def _mixer_weights(norm_mix, w_in, w_conv, w_alpha_up, b_alpha, gla_norm, w_conv_out, w_gla_out, w_o,
                   norm_ffn, w_group_router, b_group_router, w_expert_router, b_expert_router):
    wi = w_in[0]
    w_main = wi[:, :OFF_GA].astype(BF16)
    w_gates = wi[:, OFF_GA + GLA_RANK:].astype(BF16)
    w_alr = jnp.concatenate([wi[:, OFF_GA:OFF_GA + GLA_RANK],
                             jnp.zeros((D_MODEL, LANES - GLA_RANK), F32)], axis=1).astype(BF16)
    walpha = jnp.concatenate(
        [w_alpha_up[0], jnp.zeros((LANES - GLA_RANK, D_CONV), F32)], axis=0).astype(BF16)
    wr = jnp.concatenate(
        [w_group_router[0], w_expert_router[0],
         jnp.zeros((D_MODEL, LANES - N_GROUPS - N_EXPERTS), F32)], axis=1)
    wr_hi = wr.astype(BF16)
    wr_split = jnp.concatenate([wr_hi, (wr - wr_hi.astype(F32)).astype(BF16)], axis=1)
    br = jnp.concatenate([b_group_router[0], b_expert_router[0],
                          jnp.zeros((LANES - N_GROUPS - N_EXPERTS,), F32)])[None, :]
    return (norm_mix, w_main, w_gates, w_alr, w_conv[0], walpha, b_alpha, gla_norm,
            w_conv_out[0].astype(BF16), w_gla_out[0].astype(BF16), w_o[0].astype(BF16),
            norm_ffn, wr_split, br)


def _const_spec(shape):
    nd = len(shape)
    return pl.BlockSpec(shape, lambda *_: (0,) * nd, pipeline_mode=pl.Buffered(1))


def _mixer_call(x, conv_state, gla_state, weights, *, prompt):
    nb, seq, _ = x.shape
    if prompt:
        grid = (nb, seq // PROMPT_TILE)
        tok = lambda last: pl.BlockSpec((1, PROMPT_TILE, last), lambda b, i: (b, i, 0))
        conv_spec = pl.BlockSpec((1, CONV_WIDTH - 1, D_CONV), lambda b, i: (b, 0, 0))
        gla_spec = pl.BlockSpec((1, GLA_HEADS, GLA_DK, GLA_DV), lambda b, i: (b, 0, 0, 0))
        gla_in_spec = pl.BlockSpec((1, 1, 8, LANES), lambda b, i: (b, 0, 0, 0))
        scratch = [pltpu.VMEM((GLA_HEADS, GLA_DK, GLA_DV), F32),
                   pltpu.VMEM((1, PROMPT_TILE + 8, D_CONV), F32)]
        route_t_shape = (nb, 8, seq)
        route_t_spec = pl.BlockSpec((1, 8, PROMPT_TILE), lambda b, i: (b, 0, i))
    else:
        grid = (nb // SAMPLE_SEQS, 1)
        tok = lambda last: pl.BlockSpec((SAMPLE_SEQS, seq, last), lambda b, i: (b, 0, 0))
        conv_spec = pl.BlockSpec((SAMPLE_SEQS, CONV_WIDTH - 1, D_CONV), lambda b, i: (b, 0, 0))
        gla_spec = pl.BlockSpec((SAMPLE_SEQS, GLA_HEADS, GLA_DK, GLA_DV), lambda b, i: (b, 0, 0, 0))
        gla_in_spec = gla_spec
        scratch = [pltpu.VMEM((1, 8, LANES), F32),
                   pltpu.VMEM((SAMPLE_SEQS, 8 + seq, D_CONV), F32)]
        route_t_shape = (nb // SAMPLE_SEQS, 8, SAMPLE_SEQS * seq)
        route_t_spec = pl.BlockSpec((1, 8, SAMPLE_SEQS * seq), lambda b, i: (b, 0, 0))
    out_shape = (jax.ShapeDtypeStruct((nb, seq, D_MODEL), F32),
                 jax.ShapeDtypeStruct((nb, seq, D_MODEL // 2), jnp.uint32),
                 jax.ShapeDtypeStruct((nb, seq, LANES), F32),
                 jax.ShapeDtypeStruct(route_t_shape, F32),
                 jax.ShapeDtypeStruct((nb, CONV_WIDTH - 1, D_CONV), F32),
                 jax.ShapeDtypeStruct((nb, GLA_HEADS, GLA_DK, GLA_DV), F32))
    return pl.pallas_call(
        functools.partial(_mixer_kernel, prompt=prompt),
        grid=grid,
        in_specs=[tok(D_MODEL), conv_spec, gla_in_spec] + [_const_spec(w.shape) for w in weights],
        out_specs=(tok(D_MODEL), tok(D_MODEL // 2), tok(LANES), route_t_spec, conv_spec, gla_spec),
        out_shape=out_shape,
        scratch_shapes=scratch,
        compiler_params=pltpu.CompilerParams(
            dimension_semantics=("arbitrary", "arbitrary"), vmem_limit_bytes=VMEM_LIMIT),
        name="mixer_prompt" if prompt else "mixer_sample",
    )(x, conv_state, gla_state, *weights)


def _rank_kernel(eid_ref, rank_ref, cnt_ref, carry_ref):
    @pl.when(pl.program_id(0) == 0)
    def _():
        carry_ref[...] = jnp.zeros_like(carry_ref)

    n = eid_ref.shape[1]
    e_iota = lax.broadcasted_iota(jnp.int32, (N_EXPERTS, n), 0)
    oh0 = jnp.where(e_iota == eid_ref[0:1, :], 1.0, 0.0)
    oh1 = jnp.where(e_iota == eid_ref[1:2, :], 1.0, 0.0)
    cnt = oh0 + oh1
    r = lax.broadcasted_iota(jnp.int32, (n, n), 0)
    c = lax.broadcasted_iota(jnp.int32, (n, n), 1)
    before = jnp.where(r < c, 1.0, 0.0).astype(BF16)
    base = carry_ref[...] + _dot(cnt.astype(BF16), before)
    rank0 = jnp.sum(oh0 * base, axis=0, keepdims=True)
    rank1 = jnp.sum(oh1 * base, axis=0, keepdims=True)
    rank_ref[...] = jnp.concatenate([rank0, rank1], axis=0).astype(jnp.int32)
    carry_ref[...] = carry_ref[...] + jnp.sum(cnt, axis=1, keepdims=True)
    cnt_ref[...] = jnp.broadcast_to(carry_ref[...], cnt_ref.shape)


def _rank_call(eid):
    ntok = eid.shape[1]
    return pl.pallas_call(
        _rank_kernel,
        grid=(ntok // RANK_TILE,),
        in_specs=[pl.BlockSpec((2, RANK_TILE), lambda i: (0, i))],
        out_specs=(pl.BlockSpec((2, RANK_TILE), lambda i: (0, i)),
                   pl.BlockSpec((N_EXPERTS, LANES), lambda i: (0, 0))),
        out_shape=(jax.ShapeDtypeStruct((2, ntok), jnp.int32),
                   jax.ShapeDtypeStruct((N_EXPERTS, LANES), F32)),
        scratch_shapes=[pltpu.VMEM((N_EXPERTS, 1), F32)],
        compiler_params=pltpu.CompilerParams(dimension_semantics=("arbitrary",)),
        name="expert_rank",
    )(eid)


def _sc_mesh():
    return plsc.VectorSubcoreMesh(core_axis_name="c", subcore_axis_name="s")


def _sc_worker_windows(n_windows, body):
    info = plsc.get_sparse_core_info()
    n_workers = info.num_cores * info.num_subcores
    wid = lax.axis_index("s") * info.num_cores + lax.axis_index("c")

    @pl.loop(0, pl.cdiv(n_windows, n_workers))
    def _(j):
        win = wid + n_workers * j

        @pl.when(win < n_windows)
        def _():
            body(win)


def _dispatch_rows(x_prompt_rows, x_sample_rows, pos, n_rows):
    ntp, nts = x_prompt_rows.shape[0], x_sample_rows.shape[0]
    width, dtype = x_prompt_rows.shape[1], x_prompt_rows.dtype
    nwp, nws = ntp // SC_WINDOW, nts // SC_WINDOW
    pos3 = pos.reshape(2, nwp + nws, SC_WINDOW).transpose(1, 0, 2)

    @functools.partial(pl.kernel, mesh=_sc_mesh(), name="moe_dispatch",
                       out_type=jax.ShapeDtypeStruct((n_rows, width), dtype),
                       scratch_types=[pltpu.VMEM((2, SC_WINDOW), jnp.int32),
                                      pltpu.VMEM((SC_WINDOW, width), dtype)])
    def run(xp_hbm, xs_hbm, pos_hbm, out_hbm, idx_v, buf):
        def window(win):
            pltpu.sync_copy(pos_hbm.at[win], idx_v)

            @pl.when(win < nwp)
            def _():
                r0 = pl.multiple_of(win * SC_WINDOW, SC_WINDOW)
                pltpu.sync_copy(xp_hbm.at[pl.ds(r0, SC_WINDOW)], buf)

            @pl.when(win >= nwp)
            def _():
                r0 = pl.multiple_of((win - nwp) * SC_WINDOW, SC_WINDOW)
                pltpu.sync_copy(xs_hbm.at[pl.ds(r0, SC_WINDOW)], buf)

            pltpu.sync_copy(buf, out_hbm.at[idx_v.at[0]])
            pltpu.sync_copy(buf, out_hbm.at[idx_v.at[1]])

        _sc_worker_windows(nwp + nws, window)

    return run(x_prompt_rows, x_sample_rows, pos3)


def _return_rows(ys, idx):
    n_out, width = idx.shape[0], ys.shape[1]
    n_windows = n_out // SC_WINDOW

    @functools.partial(pl.kernel, mesh=_sc_mesh(), name="moe_return",
                       out_type=jax.ShapeDtypeStruct((n_out, width), ys.dtype),
                       scratch_types=[pltpu.VMEM((SC_WINDOW,), jnp.int32),
                                      pltpu.VMEM((SC_WINDOW, width), ys.dtype)])
    def run(ys_hbm, idx_hbm, out_hbm, idx_v, buf):
        def window(win):
            pltpu.sync_copy(idx_hbm.at[win], idx_v)
            pltpu.sync_copy(ys_hbm.at[idx_v], buf)
            r0 = pl.multiple_of(win * SC_WINDOW, SC_WINDOW)
            pltpu.sync_copy(buf, out_hbm.at[pl.ds(r0, SC_WINDOW)])

        _sc_worker_windows(n_windows, window)

    return run(ys, idx.reshape(n_windows, SC_WINDOW))


def _expert_kernel(first_ref, ntile_ref, total_ref, xs_hbm, wg_ref, wu_ref, wd_ref, out_hbm,
                   xbuf, obuf, wg_s, wu_s, wd_s, sem_in, sem_out):
    e = pl.program_id(0)
    nt = ntile_ref[e]
    g0 = first_ref[e]
    total = total_ref[0]
    ring = EXPERT_RING

    def rows_of(g):
        return pl.ds(pl.multiple_of(g * ROW_TILE, ROW_TILE), ROW_TILE)

    def in_copy(g):
        slot = lax.rem(g, ring)
        return pltpu.make_async_copy(xs_hbm.at[rows_of(g)], xbuf.at[slot], sem_in.at[slot])

    def out_copy(g):
        slot = lax.rem(g, ring)
        return pltpu.make_async_copy(obuf.at[slot], out_hbm.at[rows_of(g)], sem_out.at[slot])

    @pl.when(e == 0)
    def _():
        for g in range(ring - 1):
            @pl.when(g < total)
            def _():
                in_copy(g).start()

    @pl.when(nt > 0)
    def _():
        wg_s[...] = wg_ref[0].astype(BF16)
        wu_s[...] = wu_ref[0].astype(BF16)
        wd_s[...] = wd_ref[0].astype(BF16)

    def tile(g, carry):
        slot = lax.rem(g, ring)
        in_copy(g).wait()

        @pl.when(g + (ring - 1) < total)
        def _():
            in_copy(g + (ring - 1)).start()

        @pl.when(g >= ring)
        def _():
            out_copy(g - ring).wait()

        lo, hi = _unpack_bf16_pairs(xbuf[slot])
        half = D_MODEL // 2
        sub = ROW_TILE // 2
        parts = []
        for r in (0, sub):
            xl, xh = lo[r:r + sub].astype(BF16), hi[r:r + sub].astype(BF16)
            gt = _dot(xl, wg_s[:half, :]) + _dot(xh, wg_s[half:, :])
            up = _dot(xl, wu_s[:half, :]) + _dot(xh, wu_s[half:, :])
            parts.append((gt, up))
        for r, (gt, up) in zip((0, sub), parts):
            act = (gt * jax.nn.sigmoid(gt)) * up
            obuf[slot, r:r + sub, :] = _pack_bf16_pairs(_dot(act.astype(BF16), wd_s[...]))
        out_copy(g).start()
        return carry

    lax.fori_loop(g0, g0 + nt, tile, 0)

    @pl.when(e == pl.num_programs(0) - 1)
    def _():
        for k in range(1, ring + 1):
            @pl.when(total >= k)
            def _():
                out_copy(total - k).wait()


def _expert_call(first_tile, n_tiles_per_expert, total_tiles, xs, w_gate, w_up, w_down):
    nrows = xs.shape[0]
    by_expert = lambda e, *_: (e, 0, 0)
    half_row = D_MODEL // 2
    return pl.pallas_call(
        _expert_kernel,
        grid_spec=pltpu.PrefetchScalarGridSpec(
            num_scalar_prefetch=3,
            grid=(N_EXPERTS,),
            in_specs=[pl.BlockSpec(memory_space=pl.ANY),
                      pl.BlockSpec((1, D_MODEL, D_EXPERT), by_expert),
                      pl.BlockSpec((1, D_MODEL, D_EXPERT), by_expert),
                      pl.BlockSpec((1, D_EXPERT, D_MODEL), by_expert)],
            out_specs=pl.BlockSpec(memory_space=pl.ANY),
            scratch_shapes=[pltpu.VMEM((EXPERT_RING, ROW_TILE, half_row), jnp.uint32),
                            pltpu.VMEM((EXPERT_RING, ROW_TILE, half_row), jnp.uint32),
                            pltpu.VMEM((D_MODEL, D_EXPERT), BF16),
                            pltpu.VMEM((D_MODEL, D_EXPERT), BF16),
                            pltpu.VMEM((D_EXPERT, D_MODEL), BF16),
                            pltpu.SemaphoreType.DMA((EXPERT_RING,)),
                            pltpu.SemaphoreType.DMA((EXPERT_RING,))]),
        out_shape=jax.ShapeDtypeStruct((nrows, half_row), jnp.uint32),
        compiler_params=pltpu.CompilerParams(
            dimension_semantics=("arbitrary",), vmem_limit_bytes=VMEM_LIMIT),
        name="expert_mlp",
    )(first_tile, n_tiles_per_expert, total_tiles, xs, w_gate, w_up, w_down)


def _combine_kernel(h_ref, y2_ref, route_ref, nfin_ref, out_ref):
    route = route_ref[...]
    w1, w2 = route[:, 0:1], route[:, 1:2]
    lo1, hi1 = _unpack_bf16_pairs(y2_ref[0])
    lo2, hi2 = _unpack_bf16_pairs(y2_ref[1])
    y = h_ref[...] + jnp.concatenate([w1 * lo1 + w2 * lo2, w1 * hi1 + w2 * hi2], axis=1)
    out_ref[...] = _rmsnorm(y, nfin_ref[...])


def _combine_call(h, y2, route, nfin, first_token):
    n = h.shape[0]
    first_block = first_token // COMBINE_TILE
    return pl.pallas_call(
        _combine_kernel,
        grid=(n // COMBINE_TILE,),
        in_specs=[pl.BlockSpec((COMBINE_TILE, D_MODEL), lambda i: (i, 0)),
                  pl.BlockSpec((2, COMBINE_TILE, D_MODEL // 2), lambda i: (0, first_block + i, 0)),
                  pl.BlockSpec((COMBINE_TILE, LANES), lambda i: (i, 0)),
                  pl.BlockSpec((1, D_MODEL), lambda i: (0, 0))],
        out_specs=pl.BlockSpec((COMBINE_TILE, D_MODEL), lambda i: (i, 0)),
        out_shape=jax.ShapeDtypeStruct((n, D_MODEL), F32),
        compiler_params=pltpu.CompilerParams(
            dimension_semantics=("arbitrary",), vmem_limit_bytes=VMEM_LIMIT),
        name="moe_combine",
    )(h, y2, route, nfin)


def kernel(x_prompt, x_sample, state_conv, state_gla, norm_mix, w_in, w_conv, w_alpha_up, b_alpha,
           gla_norm, w_conv_out, w_gla_out, w_o, norm_ffn, w_group_router, b_group_router,
           w_expert_router, b_expert_router, w_gate, w_up, w_down, norm_final):
    nbp, seq_p, _ = x_prompt.shape
    nbs, seq_s, _ = x_sample.shape
    assert norm_mix.shape[0] == 1, "single layer"
    assert seq_p % PROMPT_TILE == 0 and nbs % SAMPLE_SEQS == 0 and seq_s == 8

    weights = _mixer_weights(norm_mix, w_in, w_conv, w_alpha_up, b_alpha, gla_norm, w_conv_out,
                             w_gla_out, w_o, norm_ffn, w_group_router, b_group_router,
                             w_expert_router, b_expert_router)

    conv0 = jnp.zeros((nbp, CONV_WIDTH - 1, D_CONV), F32)
    gla0 = jnp.zeros((nbp, 1, 8, LANES), F32)
    h_p, xn_p, rt_p, rtt_p, conv_p, gla_p = _mixer_call(x_prompt, conv0, gla0, weights, prompt=True)
    h_s, xn_s, rt_s, rtt_s, conv_s, gla_s = _mixer_call(x_sample, state_conv[0], state_gla[0],
                                                        weights, prompt=False)
    ntp, nts = nbp * seq_p, nbs * seq_s
    ntok = ntp + nts
    assert ntp % COMBINE_TILE == 0 and nts % COMBINE_TILE == 0 and ntok % RANK_TILE == 0
    h_p, xn_p, rt_p = (a.reshape(ntp, a.shape[-1]) for a in (h_p, xn_p, rt_p))
    h_s, xn_s, rt_s = (a.reshape(nts, a.shape[-1]) for a in (h_s, xn_s, rt_s))

    eid = jnp.concatenate([rtt_p[:, 2:4, :].transpose(1, 0, 2).reshape(2, ntp),
                           rtt_s[:, 2:4, :].transpose(1, 0, 2).reshape(2, nts)],
                          axis=1).astype(jnp.int32)
    rank, cnt = _rank_call(eid)
    counts = cnt[:, 0].astype(jnp.int32)
    padded = ((counts + ROW_TILE - 1) // ROW_TILE) * ROW_TILE
    ends = jnp.cumsum(padded)
    starts = ends - padded
    expert_iota = jnp.arange(N_EXPERTS, dtype=jnp.int32)[:, None, None]
    pos = rank + jnp.sum(jnp.where(eid[None] == expert_iota, starts[:, None, None], 0), axis=0)
    n_tiles = (2 * ntok + N_EXPERTS * (ROW_TILE - 1)) // ROW_TILE
    n_rows = n_tiles * ROW_TILE

    xs = _dispatch_rows(xn_p, xn_s, pos, n_rows)
    ys = _expert_call(starts // ROW_TILE, padded // ROW_TILE, ends[-1:] // ROW_TILE, xs,
                      w_gate[0], w_up[0], w_down[0])
    y2 = _return_rows(ys, pos.reshape(-1)).reshape(2, ntok, D_MODEL // 2)
    nfin = norm_final[None, :]
    y_prompt = _combine_call(h_p, y2, rt_p, nfin, 0).reshape(nbp, seq_p, D_MODEL)
    y_sample = _combine_call(h_s, y2, rt_s, nfin, ntp).reshape(nbs, seq_s, D_MODEL)
    return (y_prompt, y_sample, conv_p[None], gla_p[None], conv_s[None], gla_s[None])
```

```python
import functools

import jax
import jax.numpy as jnp
from jax import lax
from jax.experimental import pallas as pl
from jax.experimental.pallas import tpu as pltpu
from jax.experimental.pallas import tpu_sc as plsc

F32 = jnp.float32
BF16 = jnp.bfloat16

D_MODEL = 1024
D_CONV = 512
CONV_WIDTH = 3
GLA_HEADS = 4
GLA_DK = 128
GLA_DV = 256
GLA_RANK = 16
GLA_GATE_NORM = 16.0
GLA_CHUNK = 32
N_GROUPS = 4
EXPERTS_PER_GROUP = 8
N_EXPERTS = 32
D_EXPERT = 256
EPS = 1e-6

LANES = 128
OFF_CB, OFF_CC, OFF_CH = 0, 512, 1024
OFF_Q, OFF_K, OFF_V, OFF_G = 1536, 2048, 2560, 3584
OFF_GA, OFF_GB, OFF_AL = 4608, 5632, 6656
D_IN_PACKED = OFF_AL + LANES

PROMPT_TILE = 512
GLA_BLOCK = 256
STAGE_SKEW = 0
SAMPLE_SEQS = 8
ROW_TILE = 256
EXPERT_RING = 4
SC_WINDOW = 128
COMBINE_TILE = 512
RANK_TILE = 512
VMEM_LIMIT = 56 * 1024 * 1024


def _rmsnorm(x, g):
    ms = jnp.mean(x * x, axis=-1, keepdims=True)
    return (x * lax.rsqrt(ms + EPS)) * g


def _split_bf16(x):
    hi = x.astype(BF16)
    lo = (x - hi.astype(F32)).astype(BF16)
    return hi, lo


def _pack_bf16_pairs(x):
    n = x.shape[1] // 2
    lo = lax.bitcast_convert_type(x[:, :n].astype(BF16).astype(F32), jnp.uint32)
    hi = lax.bitcast_convert_type(x[:, n:].astype(BF16).astype(F32), jnp.uint32)
    return lax.shift_right_logical(lo, jnp.uint32(16)) | hi


def _unpack_bf16_pairs(w):
    lo = lax.bitcast_convert_type(lax.shift_left(w, jnp.uint32(16)), F32)
    hi = lax.bitcast_convert_type(w & jnp.uint32(0xFFFF0000), F32)
    return lo, hi


def _dot(a, b):
    return jnp.dot(a, b, preferred_element_type=F32)


def _chunk_causal(n, chunk):
    r = lax.broadcasted_iota(jnp.int32, (n, n), 0)
    c = lax.broadcasted_iota(jnp.int32, (n, n), 1)
    shift = chunk.bit_length() - 1
    assert chunk == 1 << shift
    same = lax.shift_right_arithmetic(r, shift) == lax.shift_right_arithmetic(c, shift)
    return same & (c <= r)


def _route(xn2, wr_ref, br_ref):
    rows = xn2.shape[0]
    xh, xl = _split_bf16(xn2)
    part = _dot(xh, wr_ref[...]) + _dot(xl, wr_ref[...])
    lg = part[:, :LANES] + part[:, LANES:] + br_ref[...]
    lane_i = lax.broadcasted_iota(jnp.int32, (rows, LANES), 1)
    lane = lane_i.astype(F32)
    neg = -jnp.inf
    is_g = lane_i < N_GROUPS
    gm = jnp.max(jnp.where(is_g, lg, neg), axis=1, keepdims=True)
    gs = jnp.sum(jnp.where(is_g, jnp.exp(lg - gm), 0.0), axis=1, keepdims=True)
    g_top = 1.0 / gs
    g_idx = jnp.min(jnp.where(is_g & (lg == gm), lane, float(LANES)), axis=1, keepdims=True)
    e_group = lax.shift_right_arithmetic(lane_i - N_GROUPS, 3).astype(F32)
    sel = (lane_i >= N_GROUPS) & (lane_i < N_GROUPS + N_EXPERTS) & (e_group == g_idx)
    el = jnp.where(sel, lg, neg)
    e1 = jnp.max(el, axis=1, keepdims=True)
    i1 = jnp.min(jnp.where(sel & (el == e1), lane, float(LANES)), axis=1, keepdims=True)
    sel2 = sel & (lane != i1)
    el2 = jnp.where(sel2, lg, neg)
    e2 = jnp.max(el2, axis=1, keepdims=True)
    i2 = jnp.min(jnp.where(sel2 & (el2 == e2), lane, float(LANES)), axis=1, keepdims=True)
    d = jnp.exp(e2 - e1)
    w1 = g_top / (1.0 + d)
    w2 = (g_top * d) / (1.0 + d)
    id1 = i1 - float(N_GROUPS)
    id2 = i2 - float(N_GROUPS)
    return jnp.where(lane_i == 0, w1, jnp.where(lane_i == 1, w2, jnp.where(lane_i == 2, id1,
                     jnp.where(lane_i == 3, id2, 0.0))))


def _mixer_stages(sub, nsub, x_ref, conv_in_ref, gla_in_ref,
                  nmix_ref, w_ref, wgate_ref, walr_ref, wconv_ref, walpha_ref, balpha_ref, gnorm_ref,
                  wco_ref, wgo_ref, wo_ref, nffn_ref, wr_ref, br_ref,
                  h_ref, xn2_ref, route_ref, route_t_ref, conv_out_ref, gla_out_ref,
                  s_ref, ubuf_ref, *, prompt):
    if prompt:
        rows, chunk, nseq = GLA_BLOCK, GLA_CHUNK, 1
        r0 = sub * rows
        x = x_ref[0, r0:r0 + rows, :]
    else:
        nseq, chunk = SAMPLE_SEQS, x_ref.shape[1]
        rows = nseq * chunk
        r0 = 0
        x = x_ref[...].reshape(rows, D_MODEL)
    nchunks = rows // chunk
    last_sub = sub == nsub - 1

    xn = _rmsnorm(x, nmix_ref[...]).astype(BF16)
    yield

    def proj(off, n):
        if off == OFF_AL:
            return _dot(xn, walr_ref[...])
        if off >= OFF_GA:
            return _dot(xn, wgate_ref[:, off - OFF_GA:off - OFF_GA + n])
        return _dot(xn, w_ref[:, off:off + n])

    alr = proj(OFF_AL, LANES).astype(BF16)
    lap = _dot(alr, walpha_ref[...])
    la = jax.nn.log_sigmoid(lap + balpha_ref[...]) * (1.0 / GLA_GATE_NORM)
    la_hi, la_lo = _split_bf16(la)
    causal = _chunk_causal(rows, chunk)
    tri = jnp.where(causal, 1.0, 0.0).astype(BF16)
    yield
    cc = proj(OFF_CC, D_CONV)
    ch = proj(OFF_CH, D_CONV)
    cb = proj(OFF_CB, D_CONV)
    q = proj(OFF_Q, D_CONV)
    k = proj(OFF_K, D_CONV)
    yield
    b = _dot(tri, la_hi) + _dot(tri, la_lo)

    u = cc * ch
    wc = wconv_ref[...]
    if prompt:
        ubuf_ref[0, 8 + r0:8 + r0 + rows, :] = u
        z = (wc[0:1] * ubuf_ref[0, 6 + r0:6 + r0 + rows, :]
             + wc[1:2] * ubuf_ref[0, 7 + r0:7 + r0 + rows, :] + wc[2:3] * u)
        if last_sub:
            tail = u[rows - 2:rows, :]
            ubuf_ref[0, 6:8, :] = tail
            conv_out_ref[0] = tail
    else:
        u3 = u.reshape(nseq, chunk, D_CONV)
        ubuf_ref[:, 6:8, :] = conv_in_ref[...]
        ubuf_ref[:, 8:8 + chunk, :] = u3
        z3 = (wc[0:1] * ubuf_ref[:, 6:6 + chunk, :] + wc[1:2] * ubuf_ref[:, 7:7 + chunk, :]
              + wc[2:3] * u3)
        z = z3.reshape(rows, D_CONV)
        conv_out_ref[...] = ubuf_ref[:, 6 + chunk:8 + chunk, :]
    ya = _dot((cb * z).astype(BF16), wco_ref[...])
    v = proj(OFF_V, GLA_HEADS * GLA_DV)
    yield

    b3 = b.reshape(nchunks, chunk, D_CONV)
    bl3 = b3[:, chunk - 1:chunk, :]
    qin = (q * (GLA_DK ** -0.5)) * jnp.exp(b)
    kin = k * jnp.exp(-b)
    kst = (k.reshape(nchunks, chunk, D_CONV) * jnp.exp(bl3 - b3)).reshape(rows, D_CONV)
    pad = (-rows) % LANES
    rows_p = rows + pad

    def transposed(a):
        if pad:
            a = jnp.concatenate([a, jnp.zeros((pad, a.shape[1]), F32)], axis=0)
        return a.T

    b_t = transposed(b)
    kst_t = transposed(kst)
    col_chunk = lax.shift_right_arithmetic(
        lax.broadcasted_iota(jnp.int32, (GLA_DK, rows_p), 1), chunk.bit_length() - 1)
    g = proj(OFF_G, GLA_HEADS * GLA_DV)
    yield
    heads = range(GLA_HEADS)
    vhs = [v[:, hd * GLA_DV:(hd + 1) * GLA_DV].astype(BF16) for hd in heads]
    scs = [lax.dot_general(qin[:, hd * GLA_DK:(hd + 1) * GLA_DK].astype(BF16),
                           kin[:, hd * GLA_DK:(hd + 1) * GLA_DK].astype(BF16),
                           (((1,), (1,)), ((), ())), preferred_element_type=F32) for hd in heads]
    ga = proj(OFF_GA, D_MODEL)
    yield
    upds = []
    for hd in heads:
        kst_h = kst_t[hd * GLA_DK:(hd + 1) * GLA_DK, :]
        stacked = jnp.concatenate(
            [jnp.where(col_chunk == n, kst_h, 0.0).astype(BF16) for n in range(nchunks)], axis=0)
        vh_p = jnp.concatenate([vhs[hd], jnp.zeros((pad, GLA_DV), BF16)], axis=0) if pad else vhs[hd]
        upds.append(_dot(stacked, vh_p))
    o_intras = [_dot(jnp.where(causal, scs[hd], 0.0).astype(BF16), vhs[hd]) for hd in heads]
    gb = proj(OFF_GB, D_MODEL)
    merge_a = jax.nn.sigmoid(ga) * ya
    gate_b = jax.nn.sigmoid(gb)
    yield
    gated = []
    for hd in heads:
        c0, v0 = hd * GLA_DK, hd * GLA_DV
        s_run = s_ref[hd] if prompt else None
        o_inter = []
        for n in range(nchunks):
            c_r0 = n * chunk
            s_prev = s_run if prompt else gla_in_ref[n, hd]
            qb = qin[c_r0:c_r0 + chunk, c0:c0 + GLA_DK].astype(BF16)
            o_inter.append(_dot(qb, s_prev.astype(BF16)))
            last = c_r0 + chunk - 1
            dcol = jnp.exp(b_t[c0:c0 + GLA_DK, last:last + 1])
            s_new = s_prev * dcol + upds[hd][n * GLA_DK:(n + 1) * GLA_DK, :]
            if prompt:
                s_run = s_new
            else:
                gla_out_ref[n, hd] = s_new
        if prompt:
            s_ref[hd] = s_run
            if last_sub:
                gla_out_ref[0, hd] = s_run
        oh = _rmsnorm(o_intras[hd] + jnp.concatenate(o_inter, axis=0), gnorm_ref[...])
        gh = g[:, v0:v0 + GLA_DV]
        gated.append(oh * (gh * jax.nn.sigmoid(gh)))
    yield
    yb = _dot(jnp.concatenate(gated, axis=1).astype(BF16), wgo_ref[...])
    yield

    m = merge_a + gate_b * yb
    hh = x + _dot(m.astype(BF16), wo_ref[...])
    yield
    xn2 = _rmsnorm(hh, nffn_ref[...])
    yield
    route = _route(xn2, wr_ref, br_ref)
    rpad = (-rows) % LANES
    route_p = jnp.concatenate([route, jnp.zeros((rpad, LANES), F32)], axis=0) if rpad else route
    route_t_ref[0, :, r0:r0 + rows] = route_p.T[:8, :rows]
    if prompt:
        h_ref[0, r0:r0 + rows, :] = hh
        xn2_ref[0, r0:r0 + rows, :] = _pack_bf16_pairs(xn2)
        route_ref[0, r0:r0 + rows, :] = route
    else:
        h_ref[...] = hh.reshape(nseq, chunk, D_MODEL)
        xn2_ref[...] = _pack_bf16_pairs(xn2).reshape(nseq, chunk, D_MODEL // 2)
        route_ref[...] = route.reshape(nseq, chunk, LANES)


def _mixer_kernel(*refs, prompt):
    s_ref, ubuf_ref = refs[-2:]
    if prompt:
        @pl.when(pl.program_id(1) == 0)
        def _():
            s_ref[...] = jnp.zeros_like(s_ref)
            ubuf_ref[...] = jnp.zeros_like(ubuf_ref)
    nsub = PROMPT_TILE // GLA_BLOCK if prompt else 1
    stages = [_mixer_stages(sub, nsub, *refs, prompt=prompt) for sub in range(nsub)]
    done = [False] * nsub
    step = 0
    while not all(done):
        for sub, st in enumerate(stages):
            if step >= sub * STAGE_SKEW and not done[sub]:
                done[sub] = next(st, "end") == "end"
        step += 1


def _mixer_weights(norm_mix, w_in, w_conv, w_alpha_up, b_alpha, gla_norm, w_conv_out, w_gla_out, w_o,
                   norm_ffn, w_group_router, b_group_router, w_expert_router, b_expert_router):
    wi = w_in[0]
    w_main = wi[:, :OFF_GA].astype(BF16)
    w_gates = wi[:, OFF_GA + GLA_RANK:].astype(BF16)
    w_alr = jnp.concatenate([wi[:, OFF_GA:OFF_GA + GLA_RANK],
                             jnp.zeros((D_MODEL, LANES - GLA_RANK), F32)], axis=1).astype(BF16)
    walpha = jnp.concatenate(
        [w_alpha_up[0], jnp.zeros((LANES - GLA_RANK, D_CONV), F32)], axis=0).astype(BF16)
    wr = jnp.concatenate(
        [w_group_router[0], w_expert_router[0],
         jnp.zeros((D_MODEL, LANES - N_GROUPS - N_EXPERTS), F32)], axis=1)
    wr_hi = wr.astype(BF16)
    wr_split = jnp.concatenate([wr_hi, (wr - wr_hi.astype(F32)).astype(BF16)], axis=1)
    br = jnp.concatenate([b_group_router[0], b_expert_router[0],
                          jnp.zeros((LANES - N_GROUPS - N_EXPERTS,), F32)])[None, :]
    return (norm_mix, w_main, w_gates, w_alr, w_conv[0], walpha, b_alpha, gla_norm,
            w_conv_out[0].astype(BF16), w_gla_out[0].astype(BF16), w_o[0].astype(BF16),
            norm_ffn, wr_split, br)


def _const_spec(shape):
    nd = len(shape)
    return pl.BlockSpec(shape, lambda *_: (0,) * nd, pipeline_mode=pl.Buffered(1))


def _mixer_call(x, conv_state, gla_state, weights, *, prompt):
    nb, seq, _ = x.shape
    if prompt:
        grid = (nb, seq // PROMPT_TILE)
        tok = lambda last: pl.BlockSpec((1, PROMPT_TILE, last), lambda b, i: (b, i, 0))
        conv_spec = pl.BlockSpec((1, CONV_WIDTH - 1, D_CONV), lambda b, i: (b, 0, 0))
        gla_spec = pl.BlockSpec((1, GLA_HEADS, GLA_DK, GLA_DV), lambda b, i: (b, 0, 0, 0))
        gla_in_spec = pl.BlockSpec((1, 1, 8, LANES), lambda b, i: (b, 0, 0, 0))
        scratch = [pltpu.VMEM((GLA_HEADS, GLA_DK, GLA_DV), F32),
                   pltpu.VMEM((1, PROMPT_TILE + 8, D_CONV), F32)]
        route_t_shape = (nb, 8, seq)
        route_t_spec = pl.BlockSpec((1, 8, PROMPT_TILE), lambda b, i: (b, 0, i))
    else:
        grid = (nb // SAMPLE_SEQS, 1)
        tok = lambda last: pl.BlockSpec((SAMPLE_SEQS, seq, last), lambda b, i: (b, 0, 0))
        conv_spec = pl.BlockSpec((SAMPLE_SEQS, CONV_WIDTH - 1, D_CONV), lambda b, i: (b, 0, 0))
        gla_spec = pl.BlockSpec((SAMPLE_SEQS, GLA_HEADS, GLA_DK, GLA_DV), lambda b, i: (b, 0, 0, 0))
        gla_in_spec = gla_spec
        scratch = [pltpu.VMEM((1, 8, LANES), F32),
                   pltpu.VMEM((SAMPLE_SEQS, 8 + seq, D_CONV), F32)]
        route_t_shape = (nb // SAMPLE_SEQS, 8, SAMPLE_SEQS * seq)
        route_t_spec = pl.BlockSpec((1, 8, SAMPLE_SEQS * seq), lambda b, i: (b, 0, 0))
    out_shape = (jax.ShapeDtypeStruct((nb, seq, D_MODEL), F32),
                 jax.ShapeDtypeStruct((nb, seq, D_MODEL // 2), jnp.uint32),
                 jax.ShapeDtypeStruct((nb, seq, LANES), F32),
                 jax.ShapeDtypeStruct(route_t_shape, F32),
                 jax.ShapeDtypeStruct((nb, CONV_WIDTH - 1, D_CONV), F32),
                 jax.ShapeDtypeStruct((nb, GLA_HEADS, GLA_DK, GLA_DV), F32))
    return pl.pallas_call(
        functools.partial(_mixer_kernel, prompt=prompt),
        grid=grid,
        in_specs=[tok(D_MODEL), conv_spec, gla_in_spec] + [_const_spec(w.shape) for w in weights],
        out_specs=(tok(D_MODEL), tok(D_MODEL // 2), tok(LANES), route_t_spec, conv_spec, gla_spec),
        out_shape=out_shape,
        scratch_shapes=scratch,
        compiler_params=pltpu.CompilerParams(
            dimension_semantics=("arbitrary", "arbitrary"), vmem_limit_bytes=VMEM_LIMIT),
        name="mixer_prompt" if prompt else "mixer_sample",
    )(x, conv_state, gla_state, *weights)


def _rank_kernel(eid_ref, rank_ref, cnt_ref, carry_ref):
    @pl.when(pl.program_id(0) == 0)
    def _():
        carry_ref[...] = jnp.zeros_like(carry_ref)

    n = eid_ref.shape[1]
    e_iota = lax.broadcasted_iota(jnp.int32, (N_EXPERTS, n), 0)
    oh0 = jnp.where(e_iota == eid_ref[0:1, :], 1.0, 0.0)
    oh1 = jnp.where(e_iota == eid_ref[1:2, :], 1.0, 0.0)
    cnt = oh0 + oh1
    r = lax.broadcasted_iota(jnp.int32, (n, n), 0)
    c = lax.broadcasted_iota(jnp.int32, (n, n), 1)
    before = jnp.where(r < c, 1.0, 0.0).astype(BF16)
    base = carry_ref[...] + _dot(cnt.astype(BF16), before)
    rank0 = jnp.sum(oh0 * base, axis=0, keepdims=True)
    rank1 = jnp.sum(oh1 * base, axis=0, keepdims=True)
    rank_ref[...] = jnp.concatenate([rank0, rank1], axis=0).astype(jnp.int32)
    carry_ref[...] = carry_ref[...] + jnp.sum(cnt, axis=1, keepdims=True)
    cnt_ref[...] = jnp.broadcast_to(carry_ref[...], cnt_ref.shape)


def _rank_call(eid):
    ntok = eid.shape[1]
    return pl.pallas_call(
        _rank_kernel,
        grid=(ntok // RANK_TILE,),
        in_specs=[pl.BlockSpec((2, RANK_TILE), lambda i: (0, i))],
        out_specs=(pl.BlockSpec((2, RANK_TILE), lambda i: (0, i)),
                   pl.BlockSpec((N_EXPERTS, LANES), lambda i: (0, 0))),
        out_shape=(jax.ShapeDtypeStruct((2, ntok), jnp.int32),
                   jax.ShapeDtypeStruct((N_EXPERTS, LANES), F32)),
        scratch_shapes=[pltpu.VMEM((N_EXPERTS, 1), F32)],
        compiler_params=pltpu.CompilerParams(dimension_semantics=("arbitrary",)),
        name="expert_rank",
    )(eid)


def _sc_mesh():
    return plsc.VectorSubcoreMesh(core_axis_name="c", subcore_axis_name="s")


def _sc_worker_windows(n_windows, body):
    info = plsc.get_sparse_core_info()
    n_workers = info.num_cores * info.num_subcores
    wid = lax.axis_index("s") * info.num_cores + lax.axis_index("c")

    @pl.loop(0, pl.cdiv(n_windows, n_workers))
    def _(j):
        win = wid + n_workers * j

        @pl.when(win < n_windows)
        def _():
            body(win)


def _dispatch_rows(x_prompt_rows, x_sample_rows, pos, n_rows):
    ntp, nts = x_prompt_rows.shape[0], x_sample_rows.shape[0]
    width, dtype = x_prompt_rows.shape[1], x_prompt_rows.dtype
    nwp, nws = ntp // SC_WINDOW, nts // SC_WINDOW
    pos3 = pos.reshape(2, nwp + nws, SC_WINDOW).transpose(1, 0, 2)

    @functools.partial(pl.kernel, mesh=_sc_mesh(), name="moe_dispatch",
                       out_type=jax.ShapeDtypeStruct((n_rows, width), dtype),
                       scratch_types=[pltpu.VMEM((2, SC_WINDOW), jnp.int32),
                                      pltpu.VMEM((SC_WINDOW, width), dtype)])
    def run(xp_hbm, xs_hbm, pos_hbm, out_hbm, idx_v, buf):
        def window(win):
            pltpu.sync_copy(pos_hbm.at[win], idx_v)

            @pl.when(win < nwp)
            def _():
                r0 = pl.multiple_of(win * SC_WINDOW, SC_WINDOW)
                pltpu.sync_copy(xp_hbm.at[pl.ds(r0, SC_WINDOW)], buf)

            @pl.when(win >= nwp)
            def _():
                r0 = pl.multiple_of((win - nwp) * SC_WINDOW, SC_WINDOW)
                pltpu.sync_copy(xs_hbm.at[pl.ds(r0, SC_WINDOW)], buf)

            pltpu.sync_copy(buf, out_hbm.at[idx_v.at[0]])
            pltpu.sync_copy(buf, out_hbm.at[idx_v.at[1]])

        _sc_worker_windows(nwp + nws, window)

    return run(x_prompt_rows, x_sample_rows, pos3)


def _return_rows(ys, idx):
    n_out, width = idx.shape[0], ys.shape[1]
    n_windows = n_out // SC_WINDOW

    @functools.partial(pl.kernel, mesh=_sc_mesh(), name="moe_return",
                       out_type=jax.ShapeDtypeStruct((n_out, width), ys.dtype),
                       scratch_types=[pltpu.VMEM((SC_WINDOW,), jnp.int32),
                                      pltpu.VMEM((SC_WINDOW, width), ys.dtype)])
    def run(ys_hbm, idx_hbm, out_hbm, idx_v, buf):
        def window(win):
            pltpu.sync_copy(idx_hbm.at[win], idx_v)
            pltpu.sync_copy(ys_hbm.at[idx_v], buf)
            r0 = pl.multiple_of(win * SC_WINDOW, SC_WINDOW)
            pltpu.sync_copy(buf, out_hbm.at[pl.ds(r0, SC_WINDOW)])

        _sc_worker_windows(n_windows, window)

    return run(ys, idx.reshape(n_windows, SC_WINDOW))


def _expert_kernel(first_ref, ntile_ref, total_ref, xs_hbm, wg_ref, wu_ref, wd_ref, out_hbm,
                   xbuf, obuf, wg_s, wu_s, wd_s, sem_in, sem_out):
    e = pl.program_id(0)
    nt = ntile_ref[e]
    g0 = first_ref[e]
    total = total_ref[0]
    ring = EXPERT_RING

    def rows_of(g):
        return pl.ds(pl.multiple_of(g * ROW_TILE, ROW_TILE), ROW_TILE)

    def in_copy(g):
        slot = lax.rem(g, ring)
        return pltpu.make_async_copy(xs_hbm.at[rows_of(g)], xbuf.at[slot], sem_in.at[slot])

    def out_copy(g):
        slot = lax.rem(g, ring)
        return pltpu.make_async_copy(obuf.at[slot], out_hbm.at[rows_of(g)], sem_out.at[slot])

    @pl.when(e == 0)
    def _():
        for g in range(ring - 1):
            @pl.when(g < total)
            def _():
                in_copy(g).start()

    @pl.when(nt > 0)
    def _():
        wg_s[...] = wg_ref[0].astype(BF16)
        wu_s[...] = wu_ref[0].astype(BF16)
        wd_s[...] = wd_ref[0].astype(BF16)

    def tile(g, carry):
        slot = lax.rem(g, ring)
        in_copy(g).wait()

        @pl.when(g + (ring - 1) < total)
        def _():
            in_copy(g + (ring - 1)).start()

        @pl.when(g >= ring)
        def _():
            out_copy(g - ring).wait()

        lo, hi = _unpack_bf16_pairs(xbuf[slot])
        half = D_MODEL // 2
        sub = ROW_TILE // 2
        parts = []
        for r in (0, sub):
            xl, xh = lo[r:r + sub].astype(BF16), hi[r:r + sub].astype(BF16)
            gt = _dot(xl, wg_s[:half, :]) + _dot(xh, wg_s[half:, :])
            up = _dot(xl, wu_s[:half, :]) + _dot(xh, wu_s[half:, :])
            parts.append((gt, up))
        for r, (gt, up) in zip((0, sub), parts):
            act = (gt * jax.nn.sigmoid(gt)) * up
            obuf[slot, r:r + sub, :] = _pack_bf16_pairs(_dot(act.astype(BF16), wd_s[...]))
        out_copy(g).start()
        return carry

    lax.fori_loop(g0, g0 + nt, tile, 0)

    @pl.when(e == pl.num_programs(0) - 1)
    def _():
        for k in range(1, ring + 1):
            @pl.when(total >= k)
            def _():
                out_copy(total - k).wait()


def _expert_call(first_tile, n_tiles_per_expert, total_tiles, xs, w_gate, w_up, w_down):
    nrows = xs.shape[0]
    by_expert = lambda e, *_: (e, 0, 0)
    half_row = D_MODEL // 2
    return pl.pallas_call(
        _expert_kernel,
        grid_spec=pltpu.PrefetchScalarGridSpec(
            num_scalar_prefetch=3,
            grid=(N_EXPERTS,),
            in_specs=[pl.BlockSpec(memory_space=pl.ANY),
                      pl.BlockSpec((1, D_MODEL, D_EXPERT), by_expert),
                      pl.BlockSpec((1, D_MODEL, D_EXPERT), by_expert),
                      pl.BlockSpec((1, D_EXPERT, D_MODEL), by_expert)],
            out_specs=pl.BlockSpec(memory_space=pl.ANY),
            scratch_shapes=[pltpu.VMEM((EXPERT_RING, ROW_TILE, half_row), jnp.uint32),
                            pltpu.VMEM((EXPERT_RING, ROW_TILE, half_row), jnp.uint32),
                            pltpu.VMEM((D_MODEL, D_EXPERT), BF16),
                            pltpu.VMEM((D_MODEL, D_EXPERT), BF16),
                            pltpu.VMEM((D_EXPERT, D_MODEL), BF16),
                            pltpu.SemaphoreType.DMA((EXPERT_RING,)),
                            pltpu.SemaphoreType.DMA((EXPERT_RING,))]),
        out_shape=jax.ShapeDtypeStruct((nrows, half_row), jnp.uint32),
        compiler_params=pltpu.CompilerParams(
            dimension_semantics=("arbitrary",), vmem_limit_bytes=VMEM_LIMIT),
        name="expert_mlp",
    )(first_tile, n_tiles_per_expert, total_tiles, xs, w_gate, w_up, w_down)


def _combine_kernel(h_ref, y2_ref, route_ref, nfin_ref, out_ref):
    route = route_ref[...]
    w1, w2 = route[:, 0:1], route[:, 1:2]
    lo1, hi1 = _unpack_bf16_pairs(y2_ref[0])
    lo2, hi2 = _unpack_bf16_pairs(y2_ref[1])
    y = h_ref[...] + jnp.concatenate([w1 * lo1 + w2 * lo2, w1 * hi1 + w2 * hi2], axis=1)
    out_ref[...] = _rmsnorm(y, nfin_ref[...])


def _combine_call(h, y2, route, nfin, first_token):
    n = h.shape[0]
    first_block = first_token // COMBINE_TILE
    return pl.pallas_call(
        _combine_kernel,
        grid=(n // COMBINE_TILE,),
        in_specs=[pl.BlockSpec((COMBINE_TILE, D_MODEL), lambda i: (i, 0)),
                  pl.BlockSpec((2, COMBINE_TILE, D_MODEL // 2), lambda i: (0, first_block + i, 0)),
                  pl.BlockSpec((COMBINE_TILE, LANES), lambda i: (i, 0)),
                  pl.BlockSpec((1, D_MODEL), lambda i: (0, 0))],
        out_specs=pl.BlockSpec((COMBINE_TILE, D_MODEL), lambda i: (i, 0)),
        out_shape=jax.ShapeDtypeStruct((n, D_MODEL), F32),
        compiler_params=pltpu.CompilerParams(
            dimension_semantics=("arbitrary",), vmem_limit_bytes=VMEM_LIMIT),
        name="moe_combine",
    )(h, y2, route, nfin)


def kernel(x_prompt, x_sample, state_conv, state_gla, norm_mix, w_in, w_conv, w_alpha_up, b_alpha,
           gla_norm, w_conv_out, w_gla_out, w_o, norm_ffn, w_group_router, b_group_router,
           w_expert_router, b_expert_router, w_gate, w_up, w_down, norm_final):
    nbp, seq_p, _ = x_prompt.shape
    nbs, seq_s, _ = x_sample.shape
    assert norm_mix.shape[0] == 1, "single layer"
    assert seq_p % PROMPT_TILE == 0 and nbs % SAMPLE_SEQS == 0 and seq_s == 8

    weights = _mixer_weights(norm_mix, w_in, w_conv, w_alpha_up, b_alpha, gla_norm, w_conv_out,
                             w_gla_out, w_o, norm_ffn, w_group_router, b_group_router,
                             w_expert_router, b_expert_router)

    conv0 = jnp.zeros((nbp, CONV_WIDTH - 1, D_CONV), F32)
    gla0 = jnp.zeros((nbp, 1, 8, LANES), F32)
    h_p, xn_p, rt_p, rtt_p, conv_p, gla_p = _mixer_call(x_prompt, conv0, gla0, weights, prompt=True)
    h_s, xn_s, rt_s, rtt_s, conv_s, gla_s = _mixer_call(x_sample, state_conv[0], state_gla[0],
                                                        weights, prompt=False)
    ntp, nts = nbp * seq_p, nbs * seq_s
    ntok = ntp + nts
    assert ntp % COMBINE_TILE == 0 and nts % COMBINE_TILE == 0 and ntok % RANK_TILE == 0
    h_p, xn_p, rt_p = (a.reshape(ntp, a.shape[-1]) for a in (h_p, xn_p, rt_p))
    h_s, xn_s, rt_s = (a.reshape(nts, a.shape[-1]) for a in (h_s, xn_s, rt_s))

    eid = jnp.concatenate([rtt_p[:, 2:4, :].transpose(1, 0, 2).reshape(2, ntp),
                           rtt_s[:, 2:4, :].transpose(1, 0, 2).reshape(2, nts)],
                          axis=1).astype(jnp.int32)
    rank, cnt = _rank_call(eid)
    counts = cnt[:, 0].astype(jnp.int32)
    padded = ((counts + ROW_TILE - 1) // ROW_TILE) * ROW_TILE
    ends = jnp.cumsum(padded)
    starts = ends - padded
    expert_iota = jnp.arange(N_EXPERTS, dtype=jnp.int32)[:, None, None]
    pos = rank + jnp.sum(jnp.where(eid[None] == expert_iota, starts[:, None, None], 0), axis=0)
    n_tiles = (2 * ntok + N_EXPERTS * (ROW_TILE - 1)) // ROW_TILE
    n_rows = n_tiles * ROW_TILE

    xs = _dispatch_rows(xn_p, xn_s, pos, n_rows)
    ys = _expert_call(starts // ROW_TILE, padded // ROW_TILE, ends[-1:] // ROW_TILE, xs,
                      w_gate[0], w_up[0], w_down[0])
    y2 = _return_rows(ys, pos.reshape(-1)).reshape(2, ntok, D_MODEL // 2)
    nfin = norm_final[None, :]
    y_prompt = _combine_call(h_p, y2, rt_p, nfin, 0).reshape(nbp, seq_p, D_MODEL)
    y_sample = _combine_call(h_s, y2, rt_s, nfin, ntp).reshape(nbs, seq_s, D_MODEL)
    return (y_prompt, y_sample, conv_p[None], gla_p[None], conv_s[None], gla_s[None])
```

```python
import functools

import jax
import jax.numpy as jnp
from jax import lax
from jax.experimental import pallas as pl
from jax.experimental.pallas import tpu as pltpu
from jax.experimental.pallas import tpu_sc as plsc

F32 = jnp.float32
BF16 = jnp.bfloat16

D_MODEL = 1024
D_CONV = 512
CONV_WIDTH = 3
GLA_HEADS = 4
GLA_DK = 128
GLA_DV = 256
GLA_RANK = 16
GLA_GATE_NORM = 16.0
GLA_CHUNK = 32
N_GROUPS = 4
EXPERTS_PER_GROUP = 8
N_EXPERTS = 32
D_EXPERT = 256
EPS = 1e-6

LANES = 128
OFF_CB, OFF_CC, OFF_CH = 0, 512, 1024
OFF_Q, OFF_K, OFF_V, OFF_G = 1536, 2048, 2560, 3584
OFF_GA, OFF_GB, OFF_AL = 4608, 5632, 6656
D_IN_PACKED = OFF_AL + LANES

PROMPT_TILE = 512
GLA_BLOCK = 256
STAGE_SKEW = 0
SAMPLE_SEQS = 8
ROW_TILE = 256
EXPERT_RING = 4
SC_WINDOW = 128
COMBINE_TILE = 512
RANK_TILE = 512
VMEM_LIMIT = 56 * 1024 * 1024


def _rmsnorm(x, g):
    ms = jnp.mean(x * x, axis=-1, keepdims=True)
    return (x * lax.rsqrt(ms + EPS)) * g


def _split_bf16(x):
    hi = x.astype(BF16)
    lo = (x - hi.astype(F32)).astype(BF16)
    return hi, lo


def _pack_bf16_pairs(x):
    n = x.shape[1] // 2
    lo = lax.bitcast_convert_type(x[:, :n].astype(BF16).astype(F32), jnp.uint32)
    hi = lax.bitcast_convert_type(x[:, n:].astype(BF16).astype(F32), jnp.uint32)
    return lax.shift_right_logical(lo, jnp.uint32(16)) | hi


def _unpack_bf16_pairs(w):
    lo = lax.bitcast_convert_type(lax.shift_left(w, jnp.uint32(16)), F32)
    hi = lax.bitcast_convert_type(w & jnp.uint32(0xFFFF0000), F32)
    return lo, hi


def _dot(a, b):
    return jnp.dot(a, b, preferred_element_type=F32)


def _chunk_causal(n, chunk):
    r = lax.broadcasted_iota(jnp.int32, (n, n), 0)
    c = lax.broadcasted_iota(jnp.int32, (n, n), 1)
    shift = chunk.bit_length() - 1
    assert chunk == 1 << shift
    same = lax.shift_right_arithmetic(r, shift) == lax.shift_right_arithmetic(c, shift)
    return same & (c <= r)


def _route(xn2, wr_ref, br_ref):
    rows = xn2.shape[0]
    xh, xl = _split_bf16(xn2)
    part = _dot(xh, wr_ref[...]) + _dot(xl, wr_ref[...])
    lg = part[:, :LANES] + part[:, LANES:] + br_ref[...]
    lane_i = lax.broadcasted_iota(jnp.int32, (rows, LANES), 1)
    lane = lane_i.astype(F32)
    neg = -jnp.inf
    is_g = lane_i < N_GROUPS
    gm = jnp.max(jnp.where(is_g, lg, neg), axis=1, keepdims=True)
    gs = jnp.sum(jnp.where(is_g, jnp.exp(lg - gm), 0.0), axis=1, keepdims=True)
    g_top = 1.0 / gs
    g_idx = jnp.min(jnp.where(is_g & (lg == gm), lane, float(LANES)), axis=1, keepdims=True)
    e_group = lax.shift_right_arithmetic(lane_i - N_GROUPS, 3).astype(F32)
    sel = (lane_i >= N_GROUPS) & (lane_i < N_GROUPS + N_EXPERTS) & (e_group == g_idx)
    el = jnp.where(sel, lg, neg)
    e1 = jnp.max(el, axis=1, keepdims=True)
    i1 = jnp.min(jnp.where(sel & (el == e1), lane, float(LANES)), axis=1, keepdims=True)
    sel2 = sel & (lane != i1)
    el2 = jnp.where(sel2, lg, neg)
    e2 = jnp.max(el2, axis=1, keepdims=True)
    i2 = jnp.min(jnp.where(sel2 & (el2 == e2), lane, float(LANES)), axis=1, keepdims=True)
    d = jnp.exp(e2 - e1)
    w1 = g_top / (1.0 + d)
    w2 = (g_top * d) / (1.0 + d)
    id1 = i1 - float(N_GROUPS)
    id2 = i2 - float(N_GROUPS)
    return jnp.where(lane_i == 0, w1, jnp.where(lane_i == 1, w2, jnp.where(lane_i == 2, id1,
                     jnp.where(lane_i == 3, id2, 0.0))))


def _mixer_stages(sub, nsub, x_ref, conv_in_ref, gla_in_ref,
                  nmix_ref, w_ref, wgate_ref, walr_ref, wconv_ref, walpha_ref, balpha_ref, gnorm_ref,
                  wco_ref, wgo_ref, wo_ref, nffn_ref, wr_ref, br_ref,
                  h_ref, xn2_ref, route_ref, route_t_ref, conv_out_ref, gla_out_ref,
                  s_ref, ubuf_ref, *, prompt):
    if prompt:
        rows, chunk, nseq = GLA_BLOCK, GLA_CHUNK, 1
        r0 = sub * rows
        x = x_ref[0, r0:r0 + rows, :]
    else:
        nseq, chunk = SAMPLE_SEQS, x_ref.shape[1]
        rows = nseq * chunk
        r0 = 0
        x = x_ref[...].reshape(rows, D_MODEL)
    nchunks = rows // chunk
    last_sub = sub == nsub - 1

    xn = _rmsnorm(x, nmix_ref[...]).astype(BF16)
    yield

    def proj(off, n):
        if off == OFF_AL:
            return _dot(xn, walr_ref[...])
        if off >= OFF_GA:
            return _dot(xn, wgate_ref[:, off - OFF_GA:off - OFF_GA + n])
        return _dot(xn, w_ref[:, off:off + n])

    alr = proj(OFF_AL, LANES).astype(BF16)
    lap = _dot(alr, walpha_ref[...])
    la = jax.nn.log_sigmoid(lap + balpha_ref[...]) * (1.0 / GLA_GATE_NORM)
    la_hi, la_lo = _split_bf16(la)
    causal = _chunk_causal(rows, chunk)
    tri = jnp.where(causal, 1.0, 0.0).astype(BF16)
    yield
    cc = proj(OFF_CC, D_CONV)
    ch = proj(OFF_CH, D_CONV)
    cb = proj(OFF_CB, D_CONV)
    q = proj(OFF_Q, D_CONV)
    k = proj(OFF_K, D_CONV)
    yield
    b = _dot(tri, la_hi) + _dot(tri, la_lo)

    u = cc * ch
    wc = wconv_ref[...]
    if prompt:
        ubuf_ref[0, 8 + r0:8 + r0 + rows, :] = u
        z = (wc[0:1] * ubuf_ref[0, 6 + r0:6 + r0 + rows, :]
             + wc[1:2] * ubuf_ref[0, 7 + r0:7 + r0 + rows, :] + wc[2:3] * u)
        if last_sub:
            tail = u[rows - 2:rows, :]
            ubuf_ref[0, 6:8, :] = tail
            conv_out_ref[0] = tail
    else:
        u3 = u.reshape(nseq, chunk, D_CONV)
        ubuf_ref[:, 6:8, :] = conv_in_ref[...]
        ubuf_ref[:, 8:8 + chunk, :] = u3
        z3 = (wc[0:1] * ubuf_ref[:, 6:6 + chunk, :] + wc[1:2] * ubuf_ref[:, 7:7 + chunk, :]
              + wc[2:3] * u3)
        z = z3.reshape(rows, D_CONV)
        conv_out_ref[...] = ubuf_ref[:, 6 + chunk:8 + chunk, :]
    ya = _dot((cb * z).astype(BF16), wco_ref[...])
    v = proj(OFF_V, GLA_HEADS * GLA_DV)
    yield

    b3 = b.reshape(nchunks, chunk, D_CONV)
    bl3 = b3[:, chunk - 1:chunk, :]
    qin = (q * (GLA_DK ** -0.5)) * jnp.exp(b)
    kin = k * jnp.exp(-b)
    kst = (k.reshape(nchunks, chunk, D_CONV) * jnp.exp(bl3 - b3)).reshape(rows, D_CONV)
    pad = (-rows) % LANES
    rows_p = rows + pad

    def transposed(a):
        if pad:
            a = jnp.concatenate([a, jnp.zeros((pad, a.shape[1]), F32)], axis=0)
        return a.T

    b_t = transposed(b)
    kst_t = transposed(kst)
    col_chunk = lax.shift_right_arithmetic(
        lax.broadcasted_iota(jnp.int32, (GLA_DK, rows_p), 1), chunk.bit_length() - 1)
    g = proj(OFF_G, GLA_HEADS * GLA_DV)
    yield
    heads = range(GLA_HEADS)
    vhs = [v[:, hd * GLA_DV:(hd + 1) * GLA_DV].astype(BF16) for hd in heads]
    scs = [lax.dot_general(qin[:, hd * GLA_DK:(hd + 1) * GLA_DK].astype(BF16),
                           kin[:, hd * GLA_DK:(hd + 1) * GLA_DK].astype(BF16),
                           (((1,), (1,)), ((), ())), preferred_element_type=F32) for hd in heads]
    ga = proj(OFF_GA, D_MODEL)
    yield
    upds = []
    for hd in heads:
        kst_h = kst_t[hd * GLA_DK:(hd + 1) * GLA_DK, :]
        stacked = jnp.concatenate(
            [jnp.where(col_chunk == n, kst_h, 0.0).astype(BF16) for n in range(nchunks)], axis=0)
        vh_p = jnp.concatenate([vhs[hd], jnp.zeros((pad, GLA_DV), BF16)], axis=0) if pad else vhs[hd]
        upds.append(_dot(stacked, vh_p))
    o_intras = [_dot(jnp.where(causal, scs[hd], 0.0).astype(BF16), vhs[hd]) for hd in heads]
    gb = proj(OFF_GB, D_MODEL)
    merge_a = jax.nn.sigmoid(ga) * ya
    gate_b = jax.nn.sigmoid(gb)
    yield
    gated = []
    for hd in heads:
        c0, v0 = hd * GLA_DK, hd * GLA_DV
        s_run = s_ref[hd] if prompt else None
        o_inter = []
        for n in range(nchunks):
            c_r0 = n * chunk
            s_prev = s_run if prompt else gla_in_ref[n, hd]
            qb = qin[c_r0:c_r0 + chunk, c0:c0 + GLA_DK].astype(BF16)
            o_inter.append(_dot(qb, s_prev.astype(BF16)))
            last = c_r0 + chunk - 1
            dcol = jnp.exp(b_t[c0:c0 + GLA_DK, last:last + 1])
            s_new = s_prev * dcol + upds[hd][n * GLA_DK:(n + 1) * GLA_DK, :]
            if prompt:
                s_run = s_new
            else:
                gla_out_ref[n, hd] = s_new
        if prompt:
            s_ref[hd] = s_run
            if last_sub:
                gla_out_ref[0, hd] = s_run
        oh = _rmsnorm(o_intras[hd] + jnp.concatenate(o_inter, axis=0), gnorm_ref[...])
        gh = g[:, v0:v0 + GLA_DV]
        gated.append(oh * (gh * jax.nn.sigmoid(gh)))
    yield
    yb = _dot(jnp.concatenate(gated, axis=1).astype(BF16), wgo_ref[...])
    yield

    m = merge_a + gate_b * yb
    hh = x + _dot(m.astype(BF16), wo_ref[...])
    yield
    xn2 = _rmsnorm(hh, nffn_ref[...])
    yield
    route = _route(xn2, wr_ref, br_ref)
    rpad = (-rows) % LANES
    route_p = jnp.concatenate([route, jnp.zeros((rpad, LANES), F32)], axis=0) if rpad else route
    route_t_ref[0, :, r0:r0 + rows] = route_p.T[:8, :rows]
    if prompt:
        h_ref[0, r0:r0 + rows, :] = hh
        xn2_ref[0, r0:r0 + rows, :] = _pack_bf16_pairs(xn2)
        route_ref[0, r0:r0 + rows, :] = route
    else:
        h_ref[...] = hh.reshape(nseq, chunk, D_MODEL)
        xn2_ref[...] = _pack_bf16_pairs(xn2).reshape(nseq, chunk, D_MODEL // 2)
        route_ref[...] = route.reshape(nseq, chunk, LANES)


def _mixer_kernel(*refs, prompt):
    s_ref, ubuf_ref = refs[-2:]
    if prompt:
        @pl.when(pl.program_id(1) == 0)
        def _():
            s_ref[...] = jnp.zeros_like(s_ref)
            ubuf_ref[...] = jnp.zeros_like(ubuf_ref)
    nsub = PROMPT_TILE // GLA_BLOCK if prompt else 1
    stages = [_mixer_stages(sub, nsub, *refs, prompt=prompt) for sub in range(nsub)]
    done = [False] * nsub
    step = 0
    while not all(done):
        for sub, st in enumerate(stages):
            if step >= sub * STAGE_SKEW and not done[sub]:
                done[sub] = next(st, "end") == "end"
        step += 1


def _mixer_weights(norm_mix, w_in, w_conv, w_alpha_up, b_alpha, gla_norm, w_conv_out, w_gla_out, w_o,
                   norm_ffn, w_group_router, b_group_router, w_expert_router, b_expert_router):
    wi = w_in[0]
    w_main = wi[:, :OFF_GA].astype(BF16)
    w_gates = wi[:, OFF_GA + GLA_RANK:].astype(BF16)
    w_alr = jnp.concatenate([wi[:, OFF_GA:OFF_GA + GLA_RANK],
                             jnp.zeros((D_MODEL, LANES - GLA_RANK), F32)], axis=1).astype(BF16)
    walpha = jnp.concatenate(
        [w_alpha_up[0], jnp.zeros((LANES - GLA_RANK, D_CONV), F32)], axis=0).astype(BF16)
    wr = jnp.concatenate(
        [w_group_router[0], w_expert_router[0],
         jnp.zeros((D_MODEL, LANES - N_GROUPS - N_EXPERTS), F32)], axis=1)
    wr_hi = wr.astype(BF16)
    wr_split = jnp.concatenate([wr_hi, (wr - wr_hi.astype(F32)).astype(BF16)], axis=1)
    br = jnp.concatenate([b_group_router[0], b_expert_router[0],
                          jnp.zeros((LANES - N_GROUPS - N_EXPERTS,), F32)])[None, :]
    return (norm_mix, w_main, w_gates, w_alr, w_conv[0], walpha, b_alpha, gla_norm,
            w_conv_out[0].astype(BF16), w_gla_out[0].astype(BF16), w_o[0].astype(BF16),
            norm_ffn, wr_split, br)


def _const_spec(shape):
    nd = len(shape)
    return pl.BlockSpec(shape, lambda *_: (0,) * nd, pipeline_mode=pl.Buffered(1))


def _mixer_call(x, conv_state, gla_state, weights, *, prompt, first_seq=0, n_seqs=None):
    _, seq, _ = x.shape
    nb = x.shape[0] if n_seqs is None else n_seqs
    if prompt:
        grid = (nb, seq // PROMPT_TILE)
        tok = lambda last: pl.BlockSpec((1, PROMPT_TILE, last), lambda b, i: (b, i, 0))
        x_spec = pl.BlockSpec((1, PROMPT_TILE, D_MODEL), lambda b, i: (b + first_seq, i, 0))
        conv_spec = pl.BlockSpec((1, CONV_WIDTH - 1, D_CONV), lambda b, i: (b, 0, 0))
        gla_spec = pl.BlockSpec((1, GLA_HEADS, GLA_DK, GLA_DV), lambda b, i: (b, 0, 0, 0))
        gla_in_spec = pl.BlockSpec((1, 1, 8, LANES), lambda b, i: (b, 0, 0, 0))
        scratch = [pltpu.VMEM((GLA_HEADS, GLA_DK, GLA_DV), F32),
                   pltpu.VMEM((1, PROMPT_TILE + 8, D_CONV), F32)]
        route_t_shape = (nb, 8, seq)
        route_t_spec = pl.BlockSpec((1, 8, PROMPT_TILE), lambda b, i: (b, 0, i))
    else:
        assert first_seq == 0 and n_seqs is None
        grid = (nb // SAMPLE_SEQS, 1)
        tok = lambda last: pl.BlockSpec((SAMPLE_SEQS, seq, last), lambda b, i: (b, 0, 0))
        x_spec = tok(D_MODEL)
        conv_spec = pl.BlockSpec((SAMPLE_SEQS, CONV_WIDTH - 1, D_CONV), lambda b, i: (b, 0, 0))
        gla_spec = pl.BlockSpec((SAMPLE_SEQS, GLA_HEADS, GLA_DK, GLA_DV), lambda b, i: (b, 0, 0, 0))
        gla_in_spec = gla_spec
        scratch = [pltpu.VMEM((1, 8, LANES), F32),
                   pltpu.VMEM((SAMPLE_SEQS, 8 + seq, D_CONV), F32)]
        route_t_shape = (nb // SAMPLE_SEQS, 8, SAMPLE_SEQS * seq)
        route_t_spec = pl.BlockSpec((1, 8, SAMPLE_SEQS * seq), lambda b, i: (b, 0, 0))
    out_shape = (jax.ShapeDtypeStruct((nb, seq, D_MODEL), F32),
                 jax.ShapeDtypeStruct((nb, seq, D_MODEL // 2), jnp.uint32),
                 jax.ShapeDtypeStruct((nb, seq, LANES), F32),
                 jax.ShapeDtypeStruct(route_t_shape, F32),
                 jax.ShapeDtypeStruct((nb, CONV_WIDTH - 1, D_CONV), F32),
                 jax.ShapeDtypeStruct((nb, GLA_HEADS, GLA_DK, GLA_DV), F32))
    return pl.pallas_call(
        functools.partial(_mixer_kernel, prompt=prompt),
        grid=grid,
        in_specs=[x_spec, conv_spec, gla_in_spec] + [_const_spec(w.shape) for w in weights],
        out_specs=(tok(D_MODEL), tok(D_MODEL // 2), tok(LANES), route_t_spec, conv_spec, gla_spec),
        out_shape=out_shape,
        scratch_shapes=scratch,
        compiler_params=pltpu.CompilerParams(
            dimension_semantics=("arbitrary", "arbitrary"), vmem_limit_bytes=VMEM_LIMIT),
        name="mixer_prompt" if prompt else "mixer_sample",
    )(x, conv_state, gla_state, *weights)


def _rank_kernel(eid_ref, rank_ref, cnt_ref, carry_ref):
    @pl.when(pl.program_id(0) == 0)
    def _():
        carry_ref[...] = jnp.zeros_like(carry_ref)

    n = eid_ref.shape[1]
    e_iota = lax.broadcasted_iota(jnp.int32, (N_EXPERTS, n), 0)
    oh0 = jnp.where(e_iota == eid_ref[0:1, :], 1.0, 0.0)
    oh1 = jnp.where(e_iota == eid_ref[1:2, :], 1.0, 0.0)
    cnt = oh0 + oh1
    r = lax.broadcasted_iota(jnp.int32, (n, n), 0)
    c = lax.broadcasted_iota(jnp.int32, (n, n), 1)
    before = jnp.where(r < c, 1.0, 0.0).astype(BF16)
    base = carry_ref[...] + _dot(cnt.astype(BF16), before)
    rank0 = jnp.sum(oh0 * base, axis=0, keepdims=True)
    rank1 = jnp.sum(oh1 * base, axis=0, keepdims=True)
    rank_ref[...] = jnp.concatenate([rank0, rank1], axis=0).astype(jnp.int32)
    carry_ref[...] = carry_ref[...] + jnp.sum(cnt, axis=1, keepdims=True)
    cnt_ref[...] = jnp.broadcast_to(carry_ref[...], cnt_ref.shape)


def _rank_call(eid):
    ntok = eid.shape[1]
    return pl.pallas_call(
        _rank_kernel,
        grid=(ntok // RANK_TILE,),
        in_specs=[pl.BlockSpec((2, RANK_TILE), lambda i: (0, i))],
        out_specs=(pl.BlockSpec((2, RANK_TILE), lambda i: (0, i)),
                   pl.BlockSpec((N_EXPERTS, LANES), lambda i: (0, 0))),
        out_shape=(jax.ShapeDtypeStruct((2, ntok), jnp.int32),
                   jax.ShapeDtypeStruct((N_EXPERTS, LANES), F32)),
        scratch_shapes=[pltpu.VMEM((N_EXPERTS, 1), F32)],
        compiler_params=pltpu.CompilerParams(dimension_semantics=("arbitrary",)),
        name="expert_rank",
    )(eid)


def _sc_mesh():
    return plsc.VectorSubcoreMesh(core_axis_name="c", subcore_axis_name="s")


def _sc_worker_windows(n_windows, body):
    info = plsc.get_sparse_core_info()
    n_workers = info.num_cores * info.num_subcores
    wid = lax.axis_index("s") * info.num_cores + lax.axis_index("c")

    @pl.loop(0, pl.cdiv(n_windows, n_workers))
    def _(j):
        win = wid + n_workers * j

        @pl.when(win < n_windows)
        def _():
            body(win)


def _dispatch_rows(sources, pos, n_rows):
    width, dtype = sources[0].shape[1], sources[0].dtype
    n_win = [src.shape[0] // SC_WINDOW for src in sources]
    first_win = [sum(n_win[:i]) for i in range(len(sources))]
    total_win = sum(n_win)
    pos3 = pos.reshape(2, total_win, SC_WINDOW).transpose(1, 0, 2)

    @functools.partial(pl.kernel, mesh=_sc_mesh(), name="moe_dispatch",
                       out_type=jax.ShapeDtypeStruct((n_rows, width), dtype),
                       scratch_types=[pltpu.VMEM((2, SC_WINDOW), jnp.int32),
                                      pltpu.VMEM((SC_WINDOW, width), dtype)])
    def run(*refs):
        src_hbm, (pos_hbm, out_hbm, idx_v, buf) = refs[:len(sources)], refs[len(sources):]

        def window(win):
            pltpu.sync_copy(pos_hbm.at[win], idx_v)
            for src, w0, nw in zip(src_hbm, first_win, n_win):
                @pl.when((win >= w0) & (win < w0 + nw))
                def _():
                    r0 = pl.multiple_of((win - w0) * SC_WINDOW, SC_WINDOW)
                    pltpu.sync_copy(src.at[pl.ds(r0, SC_WINDOW)], buf)

            pltpu.sync_copy(buf, out_hbm.at[idx_v.at[0]])
            pltpu.sync_copy(buf, out_hbm.at[idx_v.at[1]])

        _sc_worker_windows(total_win, window)

    return run(*sources, pos3)


def _return_rows(ys, idx):
    n_out, width = idx.shape[0], ys.shape[1]
    n_windows = n_out // SC_WINDOW

    @functools.partial(pl.kernel, mesh=_sc_mesh(), name="moe_return",
                       out_type=jax.ShapeDtypeStruct((n_out, width), ys.dtype),
                       scratch_types=[pltpu.VMEM((SC_WINDOW,), jnp.int32),
                                      pltpu.VMEM((SC_WINDOW, width), ys.dtype)])
    def run(ys_hbm, idx_hbm, out_hbm, idx_v, buf):
        def window(win):
            pltpu.sync_copy(idx_hbm.at[win], idx_v)
            pltpu.sync_copy(ys_hbm.at[idx_v], buf)
            r0 = pl.multiple_of(win * SC_WINDOW, SC_WINDOW)
            pltpu.sync_copy(buf, out_hbm.at[pl.ds(r0, SC_WINDOW)])

        _sc_worker_windows(n_windows, window)

    return run(ys, idx.reshape(n_windows, SC_WINDOW))


def _expert_kernel(first_ref, ntile_ref, total_ref, xs_hbm, wg_ref, wu_ref, wd_ref, out_hbm,
                   xbuf, obuf, wg_s, wu_s, wd_s, sem_in, sem_out):
    e = pl.program_id(0)
    nt = ntile_ref[e]
    g0 = first_ref[e]
    total = total_ref[0]
    ring = EXPERT_RING

    def rows_of(g):
        return pl.ds(pl.multiple_of(g * ROW_TILE, ROW_TILE), ROW_TILE)

    def in_copy(g):
        slot = lax.rem(g, ring)
        return pltpu.make_async_copy(xs_hbm.at[rows_of(g)], xbuf.at[slot], sem_in.at[slot])

    def out_copy(g):
        slot = lax.rem(g, ring)
        return pltpu.make_async_copy(obuf.at[slot], out_hbm.at[rows_of(g)], sem_out.at[slot])

    @pl.when(e == 0)
    def _():
        for g in range(ring - 1):
            @pl.when(g < total)
            def _():
                in_copy(g).start()

    @pl.when(nt > 0)
    def _():
        wg_s[...] = wg_ref[0].astype(BF16)
        wu_s[...] = wu_ref[0].astype(BF16)
        wd_s[...] = wd_ref[0].astype(BF16)

    def tile(g, carry):
        slot = lax.rem(g, ring)
        in_copy(g).wait()

        @pl.when(g + (ring - 1) < total)
        def _():
            in_copy(g + (ring - 1)).start()

        @pl.when(g >= ring)
        def _():
            out_copy(g - ring).wait()

        lo, hi = _unpack_bf16_pairs(xbuf[slot])
        half = D_MODEL // 2
        sub = ROW_TILE // 2
        parts = []
        for r in (0, sub):
            xl, xh = lo[r:r + sub].astype(BF16), hi[r:r + sub].astype(BF16)
            gt = _dot(xl, wg_s[:half, :]) + _dot(xh, wg_s[half:, :])
            up = _dot(xl, wu_s[:half, :]) + _dot(xh, wu_s[half:, :])
            parts.append((gt, up))
        for r, (gt, up) in zip((0, sub), parts):
            act = (gt * jax.nn.sigmoid(gt)) * up
            obuf[slot, r:r + sub, :] = _pack_bf16_pairs(_dot(act.astype(BF16), wd_s[...]))
        out_copy(g).start()
        return carry

    lax.fori_loop(g0, g0 + nt, tile, 0)

    @pl.when(e == pl.num_programs(0) - 1)
    def _():
        for k in range(1, ring + 1):
            @pl.when(total >= k)
            def _():
                out_copy(total - k).wait()


def _expert_call(first_tile, n_tiles_per_expert, total_tiles, xs, w_gate, w_up, w_down):
    nrows = xs.shape[0]
    by_expert = lambda e, *_: (e, 0, 0)
    half_row = D_MODEL // 2
    return pl.pallas_call(
        _expert_kernel,
        grid_spec=pltpu.PrefetchScalarGridSpec(
            num_scalar_prefetch=3,
            grid=(N_EXPERTS,),
            in_specs=[pl.BlockSpec(memory_space=pl.ANY),
                      pl.BlockSpec((1, D_MODEL, D_EXPERT), by_expert),
                      pl.BlockSpec((1, D_MODEL, D_EXPERT), by_expert),
                      pl.BlockSpec((1, D_EXPERT, D_MODEL), by_expert)],
            out_specs=pl.BlockSpec(memory_space=pl.ANY),
            scratch_shapes=[pltpu.VMEM((EXPERT_RING, ROW_TILE, half_row), jnp.uint32),
                            pltpu.VMEM((EXPERT_RING, ROW_TILE, half_row), jnp.uint32),
                            pltpu.VMEM((D_MODEL, D_EXPERT), BF16),
                            pltpu.VMEM((D_MODEL, D_EXPERT), BF16),
                            pltpu.VMEM((D_EXPERT, D_MODEL), BF16),
                            pltpu.SemaphoreType.DMA((EXPERT_RING,)),
                            pltpu.SemaphoreType.DMA((EXPERT_RING,))]),
        out_shape=jax.ShapeDtypeStruct((nrows, half_row), jnp.uint32),
        compiler_params=pltpu.CompilerParams(
            dimension_semantics=("arbitrary",), vmem_limit_bytes=VMEM_LIMIT),
        name="expert_mlp",
    )(first_tile, n_tiles_per_expert, total_tiles, xs, w_gate, w_up, w_down)


def _combine_kernel(h_ref, y2_ref, route_ref, nfin_ref, out_ref):
    route = route_ref[...]
    w1, w2 = route[:, 0:1], route[:, 1:2]
    lo1, hi1 = _unpack_bf16_pairs(y2_ref[0])
    lo2, hi2 = _unpack_bf16_pairs(y2_ref[1])
    y = h_ref[...] + jnp.concatenate([w1 * lo1 + w2 * lo2, w1 * hi1 + w2 * hi2], axis=1)
    out_ref[...] = _rmsnorm(y, nfin_ref[...])


def _combine_call(h, y2, route, nfin, first_token, *, out_tokens=None, out_first_token=0,
                  partial=None):
    n = h.shape[0]
    out_tokens = n if out_tokens is None else out_tokens
    first_block = first_token // COMBINE_TILE
    out_block = out_first_token // COMBINE_TILE
    in_specs = [pl.BlockSpec((COMBINE_TILE, D_MODEL), lambda i: (i, 0)),
                pl.BlockSpec((2, COMBINE_TILE, D_MODEL // 2), lambda i: (0, first_block + i, 0)),
                pl.BlockSpec((COMBINE_TILE, LANES), lambda i: (i, 0)),
                pl.BlockSpec((1, D_MODEL), lambda i: (0, 0))]
    args = [h, y2, route, nfin]
    aliases = {}
    kern = _combine_kernel
    if partial is not None:
        in_specs.append(pl.BlockSpec(memory_space=pl.ANY))
        args.append(partial)
        aliases = {4: 0}
        kern = lambda h_ref, y2_ref, route_ref, nfin_ref, _partial, out_ref: _combine_kernel(
            h_ref, y2_ref, route_ref, nfin_ref, out_ref)
    return pl.pallas_call(
        kern,
        grid=(n // COMBINE_TILE,),
        in_specs=in_specs,
        out_specs=pl.BlockSpec((COMBINE_TILE, D_MODEL), lambda i: (out_block + i, 0)),
        out_shape=jax.ShapeDtypeStruct((out_tokens, D_MODEL), F32),
        input_output_aliases=aliases,
        compiler_params=pltpu.CompilerParams(
            dimension_semantics=("arbitrary",), vmem_limit_bytes=VMEM_LIMIT),
        name="moe_combine",
    )(*args)


def _moe_rows(parts, w_gate, w_up, w_down):
    eid = jnp.concatenate([ids for _, ids in parts], axis=1).astype(jnp.int32)
    ntok = eid.shape[1]
    assert ntok % RANK_TILE == 0 and ntok % SC_WINDOW == 0
    rank, cnt = _rank_call(eid)
    counts = cnt[:, 0].astype(jnp.int32)
    padded = ((counts + ROW_TILE - 1) // ROW_TILE) * ROW_TILE
    ends = jnp.cumsum(padded)
    starts = ends - padded
    expert_iota = jnp.arange(N_EXPERTS, dtype=jnp.int32)[:, None, None]
    pos = rank + jnp.sum(jnp.where(eid[None] == expert_iota, starts[:, None, None], 0), axis=0)
    n_rows = (2 * ntok + N_EXPERTS * (ROW_TILE - 1)) // ROW_TILE * ROW_TILE
    xs = _dispatch_rows([rows for rows, _ in parts], pos, n_rows)
    ys = _expert_call(starts // ROW_TILE, padded // ROW_TILE, ends[-1:] // ROW_TILE, xs,
                      w_gate, w_up, w_down)
    return _return_rows(ys, pos.reshape(-1)).reshape(2, ntok, D_MODEL // 2)


def kernel(x_prompt, x_sample, state_conv, state_gla, norm_mix, w_in, w_conv, w_alpha_up, b_alpha,
           gla_norm, w_conv_out, w_gla_out, w_o, norm_ffn, w_group_router, b_group_router,
           w_expert_router, b_expert_router, w_gate, w_up, w_down, norm_final):
    nbp, seq_p, _ = x_prompt.shape
    nbs, seq_s, _ = x_sample.shape
    assert norm_mix.shape[0] == 1, "single layer"
    assert seq_p % PROMPT_TILE == 0 and nbs % SAMPLE_SEQS == 0 and seq_s == 8

    weights = _mixer_weights(norm_mix, w_in, w_conv, w_alpha_up, b_alpha, gla_norm, w_conv_out,
                             w_gla_out, w_o, norm_ffn, w_group_router, b_group_router,
                             w_expert_router, b_expert_router)

    half = nbp // 2
    ntp_half, nts = half * seq_p, nbs * seq_s
    assert nbp == 2 * half and ntp_half % COMBINE_TILE == 0 and nts % COMBINE_TILE == 0
    conv0 = jnp.zeros((half, CONV_WIDTH - 1, D_CONV), F32)
    gla0 = jnp.zeros((half, 1, 8, LANES), F32)
    flat = lambda a: a.reshape(-1, a.shape[-1])
    ids_of = lambda route_t: route_t[:, 2:4, :].transpose(1, 0, 2).reshape(2, -1)
    prompt_parts = []
    for g in range(2):
        h, xn, rt, rtt, conv, gla = _mixer_call(x_prompt, conv0, gla0, weights, prompt=True,
                                                first_seq=g * half, n_seqs=half)
        prompt_parts.append(dict(h=flat(h), xn=flat(xn), rt=flat(rt), ids=ids_of(rtt),
                                 conv=conv, gla=gla))
    h_s, xn_s, rt_s, rtt_s, conv_s, gla_s = _mixer_call(x_sample, state_conv[0], state_gla[0],
                                                        weights, prompt=False)
    pa, pb = prompt_parts
    experts = (w_gate[0], w_up[0], w_down[0])
    y2_a = _moe_rows([(pa["xn"], pa["ids"])], *experts)
    y2_b = _moe_rows([(pb["xn"], pb["ids"]), (flat(xn_s), ids_of(rtt_s))], *experts)

    nfin = norm_final[None, :]
    y_p = _combine_call(pa["h"], y2_a, pa["rt"], nfin, 0, out_tokens=2 * ntp_half)
    y_p = _combine_call(pb["h"], y2_b, pb["rt"], nfin, 0, out_tokens=2 * ntp_half,
                        out_first_token=ntp_half, partial=y_p)
    y_s = _combine_call(flat(h_s), y2_b, flat(rt_s), nfin, ntp_half)
    conv_p = jnp.concatenate([pa["conv"], pb["conv"]], axis=0)
    gla_p = jnp.concatenate([pa["gla"], pb["gla"]], axis=0)
    return (y_p.reshape(nbp, seq_p, D_MODEL), y_s.reshape(nbs, seq_s, D_MODEL),
            conv_p[None], gla_p[None], conv_s[None], gla_s[None])
```

```python
import functools

import jax
import jax.numpy as jnp
from jax import lax
from jax.experimental import pallas as pl
from jax.experimental.pallas import tpu as pltpu
from jax.experimental.pallas import tpu_sc as plsc

F32 = jnp.float32
BF16 = jnp.bfloat16

D_MODEL = 1024
D_CONV = 512
CONV_WIDTH = 3
GLA_HEADS = 4
GLA_DK = 128
GLA_DV = 256
GLA_RANK = 16
GLA_GATE_NORM = 16.0
GLA_CHUNK = 32
N_GROUPS = 4
EXPERTS_PER_GROUP = 8
N_EXPERTS = 32
D_EXPERT = 256
EPS = 1e-6

LANES = 128
OFF_CB, OFF_CC, OFF_CH = 0, 512, 1024
OFF_Q, OFF_K, OFF_V, OFF_G = 1536, 2048, 2560, 3584
OFF_GA, OFF_GB, OFF_AL = 4608, 5632, 6656
D_IN_PACKED = OFF_AL + LANES

PROMPT_TILE = 512
GLA_BLOCK = 256
STAGE_SKEW = 0
SAMPLE_SEQS = 32
STATE_CHUNK = 4
STATE_IN_SLOTS = 4
STATE_OUT_SLOTS = 2
ROW_TILE = 256
EXPERT_RING = 4
SC_WINDOW = 128
COMBINE_TILE = 512
RANK_TILE = 1024
RANK_BLOCK = 512
VMEM_LIMIT = 56 * 1024 * 1024


def _rmsnorm(x, g):
    ms = jnp.mean(x * x, axis=-1, keepdims=True)
    return (x * lax.rsqrt(ms + EPS)) * g


def _split_bf16(x):
    hi = x.astype(BF16)
    lo = (x - hi.astype(F32)).astype(BF16)
    return hi, lo


def _pack_bf16_pairs(x):
    n = x.shape[1] // 2
    lo = lax.bitcast_convert_type(x[:, :n].astype(BF16).astype(F32), jnp.uint32)
    hi = lax.bitcast_convert_type(x[:, n:].astype(BF16).astype(F32), jnp.uint32)
    return lax.shift_right_logical(lo, jnp.uint32(16)) | hi


def _unpack_bf16_pairs(w):
    lo = lax.bitcast_convert_type(lax.shift_left(w, jnp.uint32(16)), F32)
    hi = lax.bitcast_convert_type(w & jnp.uint32(0xFFFF0000), F32)
    return lo, hi


def _dot(a, b):
    return jnp.dot(a, b, preferred_element_type=F32)


def _chunk_causal(n, chunk):
    r = lax.broadcasted_iota(jnp.int32, (n, n), 0)
    c = lax.broadcasted_iota(jnp.int32, (n, n), 1)
    shift = chunk.bit_length() - 1
    assert chunk == 1 << shift
    same = lax.shift_right_arithmetic(r, shift) == lax.shift_right_arithmetic(c, shift)
    return same & (c <= r)


def _route(xn2, wr_ref, br_ref):
    rows = xn2.shape[0]
    xh, xl = _split_bf16(xn2)
    part = _dot(xh, wr_ref[...]) + _dot(xl, wr_ref[...])
    lg = part[:, :LANES] + part[:, LANES:] + br_ref[...]
    lane_i = lax.broadcasted_iota(jnp.int32, (rows, LANES), 1)
    lane = lane_i.astype(F32)
    neg = -jnp.inf
    is_g = lane_i < N_GROUPS
    gm = jnp.max(jnp.where(is_g, lg, neg), axis=1, keepdims=True)
    gs = jnp.sum(jnp.where(is_g, jnp.exp(lg - gm), 0.0), axis=1, keepdims=True)
    g_top = 1.0 / gs
    g_idx = jnp.min(jnp.where(is_g & (lg == gm), lane, float(LANES)), axis=1, keepdims=True)
    e_group = lax.shift_right_arithmetic(lane_i - N_GROUPS, 3).astype(F32)
    sel = (lane_i >= N_GROUPS) & (lane_i < N_GROUPS + N_EXPERTS) & (e_group == g_idx)
    el = jnp.where(sel, lg, neg)
    e1 = jnp.max(el, axis=1, keepdims=True)
    i1 = jnp.min(jnp.where(sel & (el == e1), lane, float(LANES)), axis=1, keepdims=True)
    sel2 = sel & (lane != i1)
    el2 = jnp.where(sel2, lg, neg)
    e2 = jnp.max(el2, axis=1, keepdims=True)
    i2 = jnp.min(jnp.where(sel2 & (el2 == e2), lane, float(LANES)), axis=1, keepdims=True)
    d = jnp.exp(e2 - e1)
    w1 = g_top / (1.0 + d)
    w2 = (g_top * d) / (1.0 + d)
    id1 = i1 - float(N_GROUPS)
    id2 = i2 - float(N_GROUPS)
    return jnp.where(lane_i == 0, w1, jnp.where(lane_i == 1, w2, jnp.where(lane_i == 2, id1,
                     jnp.where(lane_i == 3, id2, 0.0))))


def _mixer_stages(sub, nsub, x_ref, conv_in_ref, gla_in_ref,
                  nmix_ref, w_ref, wgate_ref, walr_ref, wconv_ref, walpha_ref, balpha_ref, gnorm_ref,
                  wco_ref, wgo_ref, wo_ref, nffn_ref, wr_ref, br_ref,
                  h_ref, xn2_ref, route_ref, route_t_ref, conv_out_ref, gla_out_ref,
                  s_ref, ubuf_ref, *state_ring, prompt):
    if prompt:
        rows, chunk, nseq = GLA_BLOCK, GLA_CHUNK, 1
        r0 = sub * rows
        x = x_ref[0, r0:r0 + rows, :]
    else:
        nseq, chunk = SAMPLE_SEQS, x_ref.shape[1]
        rows = nseq * chunk
        r0 = 0
        x = x_ref[...].reshape(rows, D_MODEL)
        sin_ref, sout_ref, sem_in, sem_out = state_ring
        n_state_chunks = nseq // STATE_CHUNK
        seq0 = pl.program_id(0) * nseq

        def chunk_seqs(c):
            return pl.ds(pl.multiple_of(seq0 + c * STATE_CHUNK, STATE_CHUNK), STATE_CHUNK)

        def in_copy(c):
            slot = c % STATE_IN_SLOTS
            return pltpu.make_async_copy(gla_in_ref.at[chunk_seqs(c)], sin_ref.at[slot],
                                         sem_in.at[slot])

        def out_copy(c):
            slot = c % STATE_OUT_SLOTS
            return pltpu.make_async_copy(sout_ref.at[slot], gla_out_ref.at[chunk_seqs(c)],
                                         sem_out.at[slot])

        for c in range(min(STATE_IN_SLOTS - 1, n_state_chunks)):
            in_copy(c).start()
    nchunks = rows // chunk
    last_sub = sub == nsub - 1

    xn = _rmsnorm(x, nmix_ref[...]).astype(BF16)
    yield

    def proj(off, n):
        if off == OFF_AL:
            return _dot(xn, walr_ref[...])
        if off >= OFF_GA:
            return _dot(xn, wgate_ref[:, off - OFF_GA:off - OFF_GA + n])
        return _dot(xn, w_ref[:, off:off + n])

    alr = proj(OFF_AL, LANES).astype(BF16)
    lap = _dot(alr, walpha_ref[...])
    la = jax.nn.log_sigmoid(lap + balpha_ref[...]) * (1.0 / GLA_GATE_NORM)
    la_hi, la_lo = _split_bf16(la)
    causal = _chunk_causal(rows, chunk)
    tri = jnp.where(causal, 1.0, 0.0).astype(BF16)
    yield
    cc = proj(OFF_CC, D_CONV)
    ch = proj(OFF_CH, D_CONV)
    cb = proj(OFF_CB, D_CONV)
    q = proj(OFF_Q, D_CONV)
    k = proj(OFF_K, D_CONV)
    yield
    b = _dot(tri, la_hi) + _dot(tri, la_lo)

    u = cc * ch
    wc = wconv_ref[...]
    if prompt:
        ubuf_ref[0, 8 + r0:8 + r0 + rows, :] = u
        z = (wc[0:1] * ubuf_ref[0, 6 + r0:6 + r0 + rows, :]
             + wc[1:2] * ubuf_ref[0, 7 + r0:7 + r0 + rows, :] + wc[2:3] * u)
        if last_sub:
            tail = u[rows - 2:rows, :]
            ubuf_ref[0, 6:8, :] = tail
            conv_out_ref[0] = tail
    else:
        u3 = u.reshape(nseq, chunk, D_CONV)
        ubuf_ref[:, 6:8, :] = conv_in_ref[...]
        ubuf_ref[:, 8:8 + chunk, :] = u3
        z3 = (wc[0:1] * ubuf_ref[:, 6:6 + chunk, :] + wc[1:2] * ubuf_ref[:, 7:7 + chunk, :]
              + wc[2:3] * u3)
        z = z3.reshape(rows, D_CONV)
        conv_out_ref[...] = ubuf_ref[:, 6 + chunk:8 + chunk, :]
    ya = _dot((cb * z).astype(BF16), wco_ref[...])
    v = proj(OFF_V, GLA_HEADS * GLA_DV)
    yield

    b3 = b.reshape(nchunks, chunk, D_CONV)
    bl3 = b3[:, chunk - 1:chunk, :]
    qin = (q * (GLA_DK ** -0.5)) * jnp.exp(b)
    kin = k * jnp.exp(-b)
    kst = (k.reshape(nchunks, chunk, D_CONV) * jnp.exp(bl3 - b3)).reshape(rows, D_CONV)
    pad = (-rows) % LANES
    rows_p = rows + pad

    def transposed(a):
        if pad:
            a = jnp.concatenate([a, jnp.zeros((pad, a.shape[1]), F32)], axis=0)
        return a.T

    b_t = transposed(b)
    kst_t = transposed(kst) if prompt else None
    col_chunk = lax.shift_right_arithmetic(
        lax.broadcasted_iota(jnp.int32, (GLA_DK, rows_p), 1), chunk.bit_length() - 1)
    g = proj(OFF_G, GLA_HEADS * GLA_DV)
    yield
    heads = range(GLA_HEADS)
    vhs = [v[:, hd * GLA_DV:(hd + 1) * GLA_DV].astype(BF16) for hd in heads]
    scs = [lax.dot_general(qin[:, hd * GLA_DK:(hd + 1) * GLA_DK].astype(BF16),
                           kin[:, hd * GLA_DK:(hd + 1) * GLA_DK].astype(BF16),
                           (((1,), (1,)), ((), ())), preferred_element_type=F32) for hd in heads]
    ga = proj(OFF_GA, D_MODEL)
    yield
    upds = []
    for hd in heads if prompt else ():
        kst_h = kst_t[hd * GLA_DK:(hd + 1) * GLA_DK, :]
        stacked = jnp.concatenate(
            [jnp.where(col_chunk == n, kst_h, 0.0).astype(BF16) for n in range(nchunks)], axis=0)
        vh_p = jnp.concatenate([vhs[hd], jnp.zeros((pad, GLA_DV), BF16)], axis=0) if pad else vhs[hd]
        upds.append(_dot(stacked, vh_p))
    o_intras = [_dot(jnp.where(causal, scs[hd], 0.0).astype(BF16), vhs[hd]) for hd in heads]
    gb = proj(OFF_GB, D_MODEL)
    merge_a = jax.nn.sigmoid(ga) * ya
    gate_b = jax.nn.sigmoid(gb)
    yield
    def q_block(n, hd):
        return qin[n * chunk:(n + 1) * chunk, hd * GLA_DK:(hd + 1) * GLA_DK].astype(BF16)

    def decay_col(n, hd):
        last = n * chunk + chunk - 1
        return jnp.exp(b_t[hd * GLA_DK:(hd + 1) * GLA_DK, last:last + 1])

    o_inter = [[None] * nchunks for _ in heads]
    if prompt:
        for hd in heads:
            s_run = s_ref[hd]
            for n in range(nchunks):
                o_inter[hd][n] = _dot(q_block(n, hd), s_run.astype(BF16))
                s_run = s_run * decay_col(n, hd) + upds[hd][n * GLA_DK:(n + 1) * GLA_DK, :]
            s_ref[hd] = s_run
            if last_sub:
                gla_out_ref[0, hd] = s_run
    else:
        for c in range(n_state_chunks):
            in_copy(c).wait()
            if c + STATE_IN_SLOTS - 1 < n_state_chunks:
                in_copy(c + STATE_IN_SLOTS - 1).start()
            if c >= STATE_OUT_SLOTS:
                out_copy(c - STATE_OUT_SLOTS).wait()
            for j in range(STATE_CHUNK):
                n = c * STATE_CHUNK + j
                for hd in heads:
                    s_prev = sin_ref[c % STATE_IN_SLOTS, j, hd]
                    o_inter[hd][n] = _dot(q_block(n, hd), s_prev.astype(BF16))
                    ksb = kst[n * chunk:(n + 1) * chunk, hd * GLA_DK:(hd + 1) * GLA_DK].astype(BF16)
                    vb = v[n * chunk:(n + 1) * chunk, hd * GLA_DV:(hd + 1) * GLA_DV].astype(BF16)
                    upd = lax.dot_general(ksb, vb, (((0,), (0,)), ((), ())),
                                          preferred_element_type=F32)
                    sout_ref[c % STATE_OUT_SLOTS, j, hd] = s_prev * decay_col(n, hd) + upd
            out_copy(c).start()
        for c in range(max(n_state_chunks - STATE_OUT_SLOTS, 0), n_state_chunks):
            out_copy(c).wait()
    gated = []
    for hd in heads:
        v0 = hd * GLA_DV
        oh = _rmsnorm(o_intras[hd] + jnp.concatenate(o_inter[hd], axis=0), gnorm_ref[...])
        gh = g[:, v0:v0 + GLA_DV]
        gated.append(oh * (gh * jax.nn.sigmoid(gh)))
    yield
    yb = _dot(jnp.concatenate(gated, axis=1).astype(BF16), wgo_ref[...])
    yield

    m = merge_a + gate_b * yb
    hh = x + _dot(m.astype(BF16), wo_ref[...])
    yield
    xn2 = _rmsnorm(hh, nffn_ref[...])
    yield
    route = _route(xn2, wr_ref, br_ref)
    rpad = (-rows) % LANES
    route_p = jnp.concatenate([route, jnp.zeros((rpad, LANES), F32)], axis=0) if rpad else route
    route_t_ref[0, :, r0:r0 + rows] = route_p.T[:8, :rows]
    if prompt:
        h_ref[0, r0:r0 + rows, :] = hh
        xn2_ref[0, r0:r0 + rows, :] = _pack_bf16_pairs(xn2)
        route_ref[0, r0:r0 + rows, :] = route
    else:
        h_ref[...] = hh.reshape(nseq, chunk, D_MODEL)
        xn2_ref[...] = _pack_bf16_pairs(xn2).reshape(nseq, chunk, D_MODEL // 2)
        route_ref[...] = route.reshape(nseq, chunk, LANES)


def _mixer_kernel(*refs, prompt):
    if prompt:
        s_ref, ubuf_ref = refs[-2:]

        @pl.when(pl.program_id(1) == 0)
        def _():
            s_ref[...] = jnp.zeros_like(s_ref)
            ubuf_ref[...] = jnp.zeros_like(ubuf_ref)
    nsub = PROMPT_TILE // GLA_BLOCK if prompt else 1
    stages = [_mixer_stages(sub, nsub, *refs, prompt=prompt) for sub in range(nsub)]
    done = [False] * nsub
    step = 0
    while not all(done):
        for sub, st in enumerate(stages):
            if step >= sub * STAGE_SKEW and not done[sub]:
                done[sub] = next(st, "end") == "end"
        step += 1


def _mixer_weights(norm_mix, w_in, w_conv, w_alpha_up, b_alpha, gla_norm, w_conv_out, w_gla_out, w_o,
                   norm_ffn, w_group_router, b_group_router, w_expert_router, b_expert_router):
    wi = w_in[0]
    w_main = wi[:, :OFF_GA].astype(BF16)
    w_gates = wi[:, OFF_GA + GLA_RANK:].astype(BF16)
    w_alr = jnp.concatenate([wi[:, OFF_GA:OFF_GA + GLA_RANK],
                             jnp.zeros((D_MODEL, LANES - GLA_RANK), F32)], axis=1).astype(BF16)
    walpha = jnp.concatenate(
        [w_alpha_up[0], jnp.zeros((LANES - GLA_RANK, D_CONV), F32)], axis=0).astype(BF16)
    wr = jnp.concatenate(
        [w_group_router[0], w_expert_router[0],
         jnp.zeros((D_MODEL, LANES - N_GROUPS - N_EXPERTS), F32)], axis=1)
    wr_hi = wr.astype(BF16)
    wr_split = jnp.concatenate([wr_hi, (wr - wr_hi.astype(F32)).astype(BF16)], axis=1)
    br = jnp.concatenate([b_group_router[0], b_expert_router[0],
                          jnp.zeros((LANES - N_GROUPS - N_EXPERTS,), F32)])[None, :]
    return (norm_mix, w_main, w_gates, w_alr, w_conv[0], walpha, b_alpha, gla_norm,
            w_conv_out[0].astype(BF16), w_gla_out[0].astype(BF16), w_o[0].astype(BF16),
            norm_ffn, wr_split, br)


def _const_spec(shape):
    nd = len(shape)
    return pl.BlockSpec(shape, lambda *_: (0,) * nd, pipeline_mode=pl.Buffered(1))


def _mixer_call(x, conv_state, gla_state, weights, *, prompt):
    nb, seq, _ = x.shape
    if prompt:
        grid = (nb, seq // PROMPT_TILE)
        tok = lambda last: pl.BlockSpec((1, PROMPT_TILE, last), lambda b, i: (b, i, 0))
        conv_spec = pl.BlockSpec((1, CONV_WIDTH - 1, D_CONV), lambda b, i: (b, 0, 0))
        gla_spec = pl.BlockSpec((1, GLA_HEADS, GLA_DK, GLA_DV), lambda b, i: (b, 0, 0, 0))
        gla_in_spec = pl.BlockSpec((1, 1, 8, LANES), lambda b, i: (b, 0, 0, 0))
        scratch = [pltpu.VMEM((GLA_HEADS, GLA_DK, GLA_DV), F32),
                   pltpu.VMEM((1, PROMPT_TILE + 8, D_CONV), F32)]
        route_t_shape = (nb, 8, seq)
        route_t_spec = pl.BlockSpec((1, 8, PROMPT_TILE), lambda b, i: (b, 0, i))
    else:
        grid = (nb // SAMPLE_SEQS, 1)
        tok = lambda last: pl.BlockSpec((SAMPLE_SEQS, seq, last), lambda b, i: (b, 0, 0))
        conv_spec = pl.BlockSpec((SAMPLE_SEQS, CONV_WIDTH - 1, D_CONV), lambda b, i: (b, 0, 0))
        gla_spec = pl.BlockSpec(memory_space=pl.ANY)
        gla_in_spec = gla_spec
        state_chunk = (STATE_CHUNK, GLA_HEADS, GLA_DK, GLA_DV)
        scratch = [pltpu.VMEM((1, 8, LANES), F32),
                   pltpu.VMEM((SAMPLE_SEQS, 8 + seq, D_CONV), F32),
                   pltpu.VMEM((STATE_IN_SLOTS,) + state_chunk, F32),
                   pltpu.VMEM((STATE_OUT_SLOTS,) + state_chunk, F32),
                   pltpu.SemaphoreType.DMA((STATE_IN_SLOTS,)),
                   pltpu.SemaphoreType.DMA((STATE_OUT_SLOTS,))]
        route_t_shape = (nb // SAMPLE_SEQS, 8, SAMPLE_SEQS * seq)
        route_t_spec = pl.BlockSpec((1, 8, SAMPLE_SEQS * seq), lambda b, i: (b, 0, 0))
    out_shape = (jax.ShapeDtypeStruct((nb, seq, D_MODEL), F32),
                 jax.ShapeDtypeStruct((nb, seq, D_MODEL // 2), jnp.uint32),
                 jax.ShapeDtypeStruct((nb, seq, LANES), F32),
                 jax.ShapeDtypeStruct(route_t_shape, F32),
                 jax.ShapeDtypeStruct((nb, CONV_WIDTH - 1, D_CONV), F32),
                 jax.ShapeDtypeStruct((nb, GLA_HEADS, GLA_DK, GLA_DV), F32))
    return pl.pallas_call(
        functools.partial(_mixer_kernel, prompt=prompt),
        grid=grid,
        in_specs=[tok(D_MODEL), conv_spec, gla_in_spec] + [_const_spec(w.shape) for w in weights],
        out_specs=(tok(D_MODEL), tok(D_MODEL // 2), tok(LANES), route_t_spec, conv_spec, gla_spec),
        out_shape=out_shape,
        scratch_shapes=scratch,
        compiler_params=pltpu.CompilerParams(
            dimension_semantics=("arbitrary", "arbitrary"), vmem_limit_bytes=VMEM_LIMIT),
        name="mixer_prompt" if prompt else "mixer_sample",
    )(x, conv_state, gla_state, *weights)


def _rank_kernel(eid_ref, rank_ref, cnt_ref, carry_ref):
    @pl.when(pl.program_id(0) == 0)
    def _():
        carry_ref[...] = jnp.zeros_like(carry_ref)

    n = RANK_BLOCK
    e_iota = lax.broadcasted_iota(jnp.int32, (N_EXPERTS, n), 0)
    r = lax.broadcasted_iota(jnp.int32, (n, n), 0)
    c = lax.broadcasted_iota(jnp.int32, (n, n), 1)
    before = jnp.where(r < c, 1.0, 0.0).astype(BF16)
    carry = carry_ref[...]
    for j in range(eid_ref.shape[1] // n):
        cols = slice(j * n, (j + 1) * n)
        oh0 = jnp.where(e_iota == eid_ref[0:1, cols], 1.0, 0.0)
        oh1 = jnp.where(e_iota == eid_ref[1:2, cols], 1.0, 0.0)
        cnt = oh0 + oh1
        base = carry + _dot(cnt.astype(BF16), before)
        rank0 = jnp.sum(oh0 * base, axis=0, keepdims=True)
        rank1 = jnp.sum(oh1 * base, axis=0, keepdims=True)
        rank_ref[:, cols] = jnp.concatenate([rank0, rank1], axis=0).astype(jnp.int32)
        carry = carry + jnp.sum(cnt, axis=1, keepdims=True)
    carry_ref[...] = carry
    cnt_ref[...] = jnp.broadcast_to(carry, cnt_ref.shape)


def _rank_call(eid):
    ntok = eid.shape[1]
    return pl.pallas_call(
        _rank_kernel,
        grid=(ntok // RANK_TILE,),
        in_specs=[pl.BlockSpec((2, RANK_TILE), lambda i: (0, i))],
        out_specs=(pl.BlockSpec((2, RANK_TILE), lambda i: (0, i)),
                   pl.BlockSpec((N_EXPERTS, LANES), lambda i: (0, 0))),
        out_shape=(jax.ShapeDtypeStruct((2, ntok), jnp.int32),
                   jax.ShapeDtypeStruct((N_EXPERTS, LANES), F32)),
        scratch_shapes=[pltpu.VMEM((N_EXPERTS, 1), F32)],
        compiler_params=pltpu.CompilerParams(dimension_semantics=("arbitrary",)),
        name="expert_rank",
    )(eid)


def _sc_mesh():
    return plsc.VectorSubcoreMesh(core_axis_name="c", subcore_axis_name="s")


def _sc_worker_windows(n_windows, body):
    info = plsc.get_sparse_core_info()
    n_workers = info.num_cores * info.num_subcores
    wid = lax.axis_index("s") * info.num_cores + lax.axis_index("c")

    @pl.loop(0, pl.cdiv(n_windows, n_workers))
    def _(j):
        win = wid + n_workers * j

        @pl.when(win < n_windows)
        def _():
            body(win)


def _dispatch_rows(sources, pos, n_rows):
    width, dtype = sources[0].shape[1], sources[0].dtype
    n_win = [src.shape[0] // SC_WINDOW for src in sources]
    first_win = [sum(n_win[:i]) for i in range(len(sources))]
    total_win = sum(n_win)
    pos3 = pos.reshape(2, total_win, SC_WINDOW).transpose(1, 0, 2)

    @functools.partial(pl.kernel, mesh=_sc_mesh(), name="moe_dispatch",
                       out_type=jax.ShapeDtypeStruct((n_rows, width), dtype),
                       scratch_types=[pltpu.VMEM((2, SC_WINDOW), jnp.int32),
                                      pltpu.VMEM((SC_WINDOW, width), dtype)])
    def run(*refs):
        src_hbm, (pos_hbm, out_hbm, idx_v, buf) = refs[:len(sources)], refs[len(sources):]

        def window(win):
            pltpu.sync_copy(pos_hbm.at[win], idx_v)
            for src, w0, nw in zip(src_hbm, first_win, n_win):
                @pl.when((win >= w0) & (win < w0 + nw))
                def _():
                    r0 = pl.multiple_of((win - w0) * SC_WINDOW, SC_WINDOW)
                    pltpu.sync_copy(src.at[pl.ds(r0, SC_WINDOW)], buf)

            pltpu.sync_copy(buf, out_hbm.at[idx_v.at[0]])
            pltpu.sync_copy(buf, out_hbm.at[idx_v.at[1]])

        _sc_worker_windows(total_win, window)

    return run(*sources, pos3)


def _return_rows(ys, idx):
    n_out, width = idx.shape[0], ys.shape[1]
    n_windows = n_out // SC_WINDOW

    @functools.partial(pl.kernel, mesh=_sc_mesh(), name="moe_return",
                       out_type=jax.ShapeDtypeStruct((n_out, width), ys.dtype),
                       scratch_types=[pltpu.VMEM((SC_WINDOW,), jnp.int32),
                                      pltpu.VMEM((SC_WINDOW, width), ys.dtype)])
    def run(ys_hbm, idx_hbm, out_hbm, idx_v, buf):
        def window(win):
            pltpu.sync_copy(idx_hbm.at[win], idx_v)
            pltpu.sync_copy(ys_hbm.at[idx_v], buf)
            r0 = pl.multiple_of(win * SC_WINDOW, SC_WINDOW)
            pltpu.sync_copy(buf, out_hbm.at[pl.ds(r0, SC_WINDOW)])

        _sc_worker_windows(n_windows, window)

    return run(ys, idx.reshape(n_windows, SC_WINDOW))


def _expert_kernel(first_ref, ntile_ref, total_ref, xs_hbm, wg_ref, wu_ref, wd_ref, out_hbm,
                   xbuf, obuf, wg_s, wu_s, wd_s, sem_in, sem_out):
    e = pl.program_id(0)
    nt = ntile_ref[e]
    g0 = first_ref[e]
    total = total_ref[0]
    ring = EXPERT_RING

    def rows_of(g):
        return pl.ds(pl.multiple_of(g * ROW_TILE, ROW_TILE), ROW_TILE)

    def in_copy(g):
        slot = lax.rem(g, ring)
        return pltpu.make_async_copy(xs_hbm.at[rows_of(g)], xbuf.at[slot], sem_in.at[slot])

    def out_copy(g):
        slot = lax.rem(g, ring)
        return pltpu.make_async_copy(obuf.at[slot], out_hbm.at[rows_of(g)], sem_out.at[slot])

    @pl.when(e == 0)
    def _():
        for g in range(ring - 1):
            @pl.when(g < total)
            def _():
                in_copy(g).start()

    @pl.when(nt > 0)
    def _():
        wg_s[...] = wg_ref[0].astype(BF16)
        wu_s[...] = wu_ref[0].astype(BF16)
        wd_s[...] = wd_ref[0].astype(BF16)

    def tile(g, carry):
        slot = lax.rem(g, ring)
        in_copy(g).wait()

        @pl.when(g + (ring - 1) < total)
        def _():
            in_copy(g + (ring - 1)).start()

        @pl.when(g >= ring)
        def _():
            out_copy(g - ring).wait()

        lo, hi = _unpack_bf16_pairs(xbuf[slot])
        half = D_MODEL // 2
        sub = ROW_TILE // 2
        parts = []
        for r in (0, sub):
            xl, xh = lo[r:r + sub].astype(BF16), hi[r:r + sub].astype(BF16)
            gt = _dot(xl, wg_s[:half, :]) + _dot(xh, wg_s[half:, :])
            up = _dot(xl, wu_s[:half, :]) + _dot(xh, wu_s[half:, :])
            parts.append((gt, up))
        for r, (gt, up) in zip((0, sub), parts):
            act = (gt * jax.nn.sigmoid(gt)) * up
            obuf[slot, r:r + sub, :] = _pack_bf16_pairs(_dot(act.astype(BF16), wd_s[...]))
        out_copy(g).start()
        return carry

    lax.fori_loop(g0, g0 + nt, tile, 0)

    @pl.when(e == pl.num_programs(0) - 1)
    def _():
        for k in range(1, ring + 1):
            @pl.when(total >= k)
            def _():
                out_copy(total - k).wait()


def _expert_call(first_tile, n_tiles_per_expert, total_tiles, xs, w_gate, w_up, w_down):
    nrows = xs.shape[0]
    by_expert = lambda e, *_: (e, 0, 0)
    half_row = D_MODEL // 2
    return pl.pallas_call(
        _expert_kernel,
        grid_spec=pltpu.PrefetchScalarGridSpec(
            num_scalar_prefetch=3,
            grid=(N_EXPERTS,),
            in_specs=[pl.BlockSpec(memory_space=pl.ANY),
                      pl.BlockSpec((1, D_MODEL, D_EXPERT), by_expert),
                      pl.BlockSpec((1, D_MODEL, D_EXPERT), by_expert),
                      pl.BlockSpec((1, D_EXPERT, D_MODEL), by_expert)],
            out_specs=pl.BlockSpec(memory_space=pl.ANY),
            scratch_shapes=[pltpu.VMEM((EXPERT_RING, ROW_TILE, half_row), jnp.uint32),
                            pltpu.VMEM((EXPERT_RING, ROW_TILE, half_row), jnp.uint32),
                            pltpu.VMEM((D_MODEL, D_EXPERT), BF16),
                            pltpu.VMEM((D_MODEL, D_EXPERT), BF16),
                            pltpu.VMEM((D_EXPERT, D_MODEL), BF16),
                            pltpu.SemaphoreType.DMA((EXPERT_RING,)),
                            pltpu.SemaphoreType.DMA((EXPERT_RING,))]),
        out_shape=jax.ShapeDtypeStruct((nrows, half_row), jnp.uint32),
        compiler_params=pltpu.CompilerParams(
            dimension_semantics=("arbitrary",), vmem_limit_bytes=VMEM_LIMIT),
        name="expert_mlp",
    )(first_tile, n_tiles_per_expert, total_tiles, xs, w_gate, w_up, w_down)


def _combine_kernel(h_ref, y2_ref, route_ref, nfin_ref, out_ref):
    route = route_ref[...]
    w1, w2 = route[:, 0:1], route[:, 1:2]
    lo1, hi1 = _unpack_bf16_pairs(y2_ref[0])
    lo2, hi2 = _unpack_bf16_pairs(y2_ref[1])
    y = h_ref[...] + jnp.concatenate([w1 * lo1 + w2 * lo2, w1 * hi1 + w2 * hi2], axis=1)
    out_ref[...] = _rmsnorm(y, nfin_ref[...])


def _combine_call(h, y2, route, nfin, first_token):
    n = h.shape[0]
    first_block = first_token // COMBINE_TILE
    return pl.pallas_call(
        _combine_kernel,
        grid=(n // COMBINE_TILE,),
        in_specs=[pl.BlockSpec((COMBINE_TILE, D_MODEL), lambda i: (i, 0)),
                  pl.BlockSpec((2, COMBINE_TILE, D_MODEL // 2), lambda i: (0, first_block + i, 0)),
                  pl.BlockSpec((COMBINE_TILE, LANES), lambda i: (i, 0)),
                  pl.BlockSpec((1, D_MODEL), lambda i: (0, 0))],
        out_specs=pl.BlockSpec((COMBINE_TILE, D_MODEL), lambda i: (i, 0)),
        out_shape=jax.ShapeDtypeStruct((n, D_MODEL), F32),
        compiler_params=pltpu.CompilerParams(
            dimension_semantics=("arbitrary",), vmem_limit_bytes=VMEM_LIMIT),
        name="moe_combine",
    )(h, y2, route, nfin)


def _moe_rows(parts, w_gate, w_up, w_down):
    eid = jnp.concatenate([ids for _, ids in parts], axis=1).astype(jnp.int32)
    ntok = eid.shape[1]
    assert ntok % RANK_TILE == 0 and ntok % SC_WINDOW == 0
    rank, cnt = _rank_call(eid)
    counts = cnt[:, 0].astype(jnp.int32)
    padded = ((counts + ROW_TILE - 1) // ROW_TILE) * ROW_TILE
    ends = jnp.cumsum(padded)
    starts = ends - padded
    expert_iota = jnp.arange(N_EXPERTS, dtype=jnp.int32)[:, None, None]
    pos = rank + jnp.sum(jnp.where(eid[None] == expert_iota, starts[:, None, None], 0), axis=0)
    n_rows = (2 * ntok + N_EXPERTS * (ROW_TILE - 1)) // ROW_TILE * ROW_TILE
    xs = _dispatch_rows([rows for rows, _ in parts], pos, n_rows)
    ys = _expert_call(starts // ROW_TILE, padded // ROW_TILE, ends[-1:] // ROW_TILE, xs,
                      w_gate, w_up, w_down)
    return _return_rows(ys, pos.reshape(-1)).reshape(2, ntok, D_MODEL // 2)


def kernel(x_prompt, x_sample, state_conv, state_gla, norm_mix, w_in, w_conv, w_alpha_up, b_alpha,
           gla_norm, w_conv_out, w_gla_out, w_o, norm_ffn, w_group_router, b_group_router,
           w_expert_router, b_expert_router, w_gate, w_up, w_down, norm_final):
    nbp, seq_p, _ = x_prompt.shape
    nbs, seq_s, _ = x_sample.shape
    assert norm_mix.shape[0] == 1, "single layer"
    assert seq_p % PROMPT_TILE == 0 and nbs % SAMPLE_SEQS == 0 and seq_s == 8

    weights = _mixer_weights(norm_mix, w_in, w_conv, w_alpha_up, b_alpha, gla_norm, w_conv_out,
                             w_gla_out, w_o, norm_ffn, w_group_router, b_group_router,
                             w_expert_router, b_expert_router)

    ntp, nts = nbp * seq_p, nbs * seq_s
    assert ntp % COMBINE_TILE == 0 and nts % COMBINE_TILE == 0
    conv0 = jnp.zeros((nbp, CONV_WIDTH - 1, D_CONV), F32)
    gla0 = jnp.zeros((nbp, 1, 8, LANES), F32)
    flat = lambda a: a.reshape(-1, a.shape[-1])
    ids_of = lambda route_t: route_t[:, 2:4, :].transpose(1, 0, 2).reshape(2, -1)
    h_p, xn_p, rt_p, rtt_p, conv_p, gla_p = _mixer_call(x_prompt, conv0, gla0, weights, prompt=True)
    h_s, xn_s, rt_s, rtt_s, conv_s, gla_s = _mixer_call(x_sample, state_conv[0], state_gla[0],
                                                        weights, prompt=False)
    y2 = _moe_rows([(flat(xn_p), ids_of(rtt_p)), (flat(xn_s), ids_of(rtt_s))],
                   w_gate[0], w_up[0], w_down[0])
    nfin = norm_final[None, :]
    y_p = _combine_call(flat(h_p), y2, flat(rt_p), nfin, 0)
    y_s = _combine_call(flat(h_s), y2, flat(rt_s), nfin, ntp)
    return (y_p.reshape(nbp, seq_p, D_MODEL), y_s.reshape(nbs, seq_s, D_MODEL),
            conv_p[None], gla_p[None], conv_s[None], gla_s[None])
```

```python
import functools

import jax
import jax.numpy as jnp
from jax import lax
from jax.experimental import pallas as pl
from jax.experimental.pallas import tpu as pltpu
from jax.experimental.pallas import tpu_sc as plsc

F32 = jnp.float32
BF16 = jnp.bfloat16

D_MODEL = 1024
D_CONV = 512
CONV_WIDTH = 3
GLA_HEADS = 4
GLA_DK = 128
GLA_DV = 256
GLA_RANK = 16
GLA_GATE_NORM = 16.0
GLA_CHUNK = 32
N_GROUPS = 4
EXPERTS_PER_GROUP = 8
N_EXPERTS = 32
D_EXPERT = 256
EPS = 1e-6

LANES = 128
OFF_CB, OFF_CC, OFF_CH = 0, 512, 1024
OFF_Q, OFF_K, OFF_V, OFF_G = 1536, 2048, 2560, 3584
OFF_GA, OFF_GB, OFF_AL = 4608, 5632, 6656
D_IN_PACKED = OFF_AL + LANES

PROMPT_TILE = 512
GLA_BLOCK = 256
STAGE_SKEW = 0
SAMPLE_SEQS = 32
STATE_CHUNK = 4
STATE_IN_SLOTS = 8
STATE_OUT_SLOTS = 2
ROW_TILE = 256
EXPERT_RING = 4
SC_WINDOW = 128
COMBINE_TILE = 512
RANK_TILE = 1024
RANK_BLOCK = 512
VMEM_LIMIT = 56 * 1024 * 1024


def _rmsnorm(x, g):
    ms = jnp.mean(x * x, axis=-1, keepdims=True)
    return (x * lax.rsqrt(ms + EPS)) * g


def _split_bf16(x):
    hi = x.astype(BF16)
    lo = (x - hi.astype(F32)).astype(BF16)
    return hi, lo


def _pack_bf16_pairs(x):
    n = x.shape[1] // 2
    lo = lax.bitcast_convert_type(x[:, :n].astype(BF16).astype(F32), jnp.uint32)
    hi = lax.bitcast_convert_type(x[:, n:].astype(BF16).astype(F32), jnp.uint32)
    return lax.shift_right_logical(lo, jnp.uint32(16)) | hi


def _unpack_bf16_pairs(w):
    lo = lax.bitcast_convert_type(lax.shift_left(w, jnp.uint32(16)), F32)
    hi = lax.bitcast_convert_type(w & jnp.uint32(0xFFFF0000), F32)
    return lo, hi


def _dot(a, b):
    return jnp.dot(a, b, preferred_element_type=F32)


def _chunk_causal(n, chunk):
    r = lax.broadcasted_iota(jnp.int32, (n, n), 0)
    c = lax.broadcasted_iota(jnp.int32, (n, n), 1)
    shift = chunk.bit_length() - 1
    assert chunk == 1 << shift
    same = lax.shift_right_arithmetic(r, shift) == lax.shift_right_arithmetic(c, shift)
    return same & (c <= r)


def _route(xn2, wr_ref, br_ref):
    rows = xn2.shape[0]
    xh, xl = _split_bf16(xn2)
    part = _dot(xh, wr_ref[...]) + _dot(xl, wr_ref[...])
    lg = part[:, :LANES] + part[:, LANES:] + br_ref[...]
    lane_i = lax.broadcasted_iota(jnp.int32, (rows, LANES), 1)
    lane = lane_i.astype(F32)
    neg = -jnp.inf
    is_g = lane_i < N_GROUPS
    gm = jnp.max(jnp.where(is_g, lg, neg), axis=1, keepdims=True)
    gs = jnp.sum(jnp.where(is_g, jnp.exp(lg - gm), 0.0), axis=1, keepdims=True)
    g_top = 1.0 / gs
    g_idx = jnp.min(jnp.where(is_g & (lg == gm), lane, float(LANES)), axis=1, keepdims=True)
    e_group = lax.shift_right_arithmetic(lane_i - N_GROUPS, 3).astype(F32)
    sel = (lane_i >= N_GROUPS) & (lane_i < N_GROUPS + N_EXPERTS) & (e_group == g_idx)
    el = jnp.where(sel, lg, neg)
    e1 = jnp.max(el, axis=1, keepdims=True)
    i1 = jnp.min(jnp.where(sel & (el == e1), lane, float(LANES)), axis=1, keepdims=True)
    sel2 = sel & (lane != i1)
    el2 = jnp.where(sel2, lg, neg)
    e2 = jnp.max(el2, axis=1, keepdims=True)
    i2 = jnp.min(jnp.where(sel2 & (el2 == e2), lane, float(LANES)), axis=1, keepdims=True)
    d = jnp.exp(e2 - e1)
    w1 = g_top / (1.0 + d)
    w2 = (g_top * d) / (1.0 + d)
    id1 = i1 - float(N_GROUPS)
    id2 = i2 - float(N_GROUPS)
    return jnp.where(lane_i == 0, w1, jnp.where(lane_i == 1, w2, jnp.where(lane_i == 2, id1,
                     jnp.where(lane_i == 3, id2, 0.0))))


def _mixer_stages(sub, nsub, x_ref, conv_in_ref, gla_in_ref,
                  nmix_ref, w_ref, wgate_ref, walr_ref, wconv_ref, walpha_ref, balpha_ref, gnorm_ref,
                  wco_ref, wgo_ref, wo_ref, nffn_ref, wr_ref, br_ref,
                  h_ref, xn2_ref, route_ref, route_t_ref, conv_out_ref, gla_out_ref,
                  s_ref, ubuf_ref, *state_ring, prompt):
    if prompt:
        rows, chunk, nseq = GLA_BLOCK, GLA_CHUNK, 1
        r0 = sub * rows
        x = x_ref[0, r0:r0 + rows, :]
    else:
        nseq, chunk = SAMPLE_SEQS, x_ref.shape[1]
        rows = nseq * chunk
        r0 = 0
        x = x_ref[...].reshape(rows, D_MODEL)
        sin_ref, sout_ref, sem_in, sem_out = state_ring
        n_state_chunks = nseq // STATE_CHUNK
        seq0 = pl.program_id(0) * nseq

        def chunk_seqs(c):
            return pl.ds(pl.multiple_of(seq0 + c * STATE_CHUNK, STATE_CHUNK), STATE_CHUNK)

        def in_copy(c):
            slot = c % STATE_IN_SLOTS
            return pltpu.make_async_copy(gla_in_ref.at[chunk_seqs(c)], sin_ref.at[slot],
                                         sem_in.at[slot])

        def out_copy(c):
            slot = c % STATE_OUT_SLOTS
            return pltpu.make_async_copy(sout_ref.at[slot], gla_out_ref.at[chunk_seqs(c)],
                                         sem_out.at[slot])

        for c in range(min(STATE_IN_SLOTS - 1, n_state_chunks)):
            in_copy(c).start()
    nchunks = rows // chunk
    last_sub = sub == nsub - 1

    xn = _rmsnorm(x, nmix_ref[...]).astype(BF16)
    yield

    def proj(off, n):
        if off == OFF_AL:
            return _dot(xn, walr_ref[...])
        if off >= OFF_GA:
            return _dot(xn, wgate_ref[:, off - OFF_GA:off - OFF_GA + n])
        return _dot(xn, w_ref[:, off:off + n])

    alr = proj(OFF_AL, LANES).astype(BF16)
    lap = _dot(alr, walpha_ref[...])
    la = jax.nn.log_sigmoid(lap + balpha_ref[...]) * (1.0 / GLA_GATE_NORM)
    la_hi, la_lo = _split_bf16(la)
    causal = _chunk_causal(rows, chunk)
    tri = jnp.where(causal, 1.0, 0.0).astype(BF16)
    yield
    cc = proj(OFF_CC, D_CONV)
    ch = proj(OFF_CH, D_CONV)
    cb = proj(OFF_CB, D_CONV)
    q = proj(OFF_Q, D_CONV)
    k = proj(OFF_K, D_CONV)
    yield
    b = _dot(tri, la_hi) + _dot(tri, la_lo)

    u = cc * ch
    wc = wconv_ref[...]
    if prompt:
        ubuf_ref[0, 8 + r0:8 + r0 + rows, :] = u
        z = (wc[0:1] * ubuf_ref[0, 6 + r0:6 + r0 + rows, :]
             + wc[1:2] * ubuf_ref[0, 7 + r0:7 + r0 + rows, :] + wc[2:3] * u)
        if last_sub:
            tail = u[rows - 2:rows, :]
            ubuf_ref[0, 6:8, :] = tail
            conv_out_ref[0] = tail
    else:
        u3 = u.reshape(nseq, chunk, D_CONV)
        ubuf_ref[:, 6:8, :] = conv_in_ref[...]
        ubuf_ref[:, 8:8 + chunk, :] = u3
        z3 = (wc[0:1] * ubuf_ref[:, 6:6 + chunk, :] + wc[1:2] * ubuf_ref[:, 7:7 + chunk, :]
              + wc[2:3] * u3)
        z = z3.reshape(rows, D_CONV)
        conv_out_ref[...] = ubuf_ref[:, 6 + chunk:8 + chunk, :]
    ya = _dot((cb * z).astype(BF16), wco_ref[...])
    v = proj(OFF_V, GLA_HEADS * GLA_DV)
    yield

    b3 = b.reshape(nchunks, chunk, D_CONV)
    bl3 = b3[:, chunk - 1:chunk, :]
    qin = (q * (GLA_DK ** -0.5)) * jnp.exp(b)
    kin = k * jnp.exp(-b)
    kst = (k.reshape(nchunks, chunk, D_CONV) * jnp.exp(bl3 - b3)).reshape(rows, D_CONV)
    pad = (-rows) % LANES
    rows_p = rows + pad

    def transposed(a):
        if pad:
            a = jnp.concatenate([a, jnp.zeros((pad, a.shape[1]), F32)], axis=0)
        return a.T

    b_t = transposed(b)
    kst_t = transposed(kst) if prompt else None
    col_chunk = lax.shift_right_arithmetic(
        lax.broadcasted_iota(jnp.int32, (GLA_DK, rows_p), 1), chunk.bit_length() - 1)
    g = proj(OFF_G, GLA_HEADS * GLA_DV)
    yield
    heads = range(GLA_HEADS)
    vhs = [v[:, hd * GLA_DV:(hd + 1) * GLA_DV].astype(BF16) for hd in heads]
    scs = [lax.dot_general(qin[:, hd * GLA_DK:(hd + 1) * GLA_DK].astype(BF16),
                           kin[:, hd * GLA_DK:(hd + 1) * GLA_DK].astype(BF16),
                           (((1,), (1,)), ((), ())), preferred_element_type=F32) for hd in heads]
    ga = proj(OFF_GA, D_MODEL)
    yield
    upds = []
    for hd in heads if prompt else ():
        kst_h = kst_t[hd * GLA_DK:(hd + 1) * GLA_DK, :]
        stacked = jnp.concatenate(
            [jnp.where(col_chunk == n, kst_h, 0.0).astype(BF16) for n in range(nchunks)], axis=0)
        vh_p = jnp.concatenate([vhs[hd], jnp.zeros((pad, GLA_DV), BF16)], axis=0) if pad else vhs[hd]
        upds.append(_dot(stacked, vh_p))
    o_intras = [_dot(jnp.where(causal, scs[hd], 0.0).astype(BF16), vhs[hd]) for hd in heads]
    gb = proj(OFF_GB, D_MODEL)
    merge_a = jax.nn.sigmoid(ga) * ya
    gate_b = jax.nn.sigmoid(gb)
    yield
    def q_block(n, hd):
        return qin[n * chunk:(n + 1) * chunk, hd * GLA_DK:(hd + 1) * GLA_DK].astype(BF16)

    def decay_col(n, hd):
        last = n * chunk + chunk - 1
        return jnp.exp(b_t[hd * GLA_DK:(hd + 1) * GLA_DK, last:last + 1])

    o_inter = [[None] * nchunks for _ in heads]
    if prompt:
        for hd in heads:
            s_run = s_ref[hd]
            for n in range(nchunks):
                o_inter[hd][n] = _dot(q_block(n, hd), s_run.astype(BF16))
                s_run = s_run * decay_col(n, hd) + upds[hd][n * GLA_DK:(n + 1) * GLA_DK, :]
            s_ref[hd] = s_run
            if last_sub:
                gla_out_ref[0, hd] = s_run
    else:
        for c in range(n_state_chunks):
            in_copy(c).wait()
            if c + STATE_IN_SLOTS - 1 < n_state_chunks:
                in_copy(c + STATE_IN_SLOTS - 1).start()
            if c >= STATE_OUT_SLOTS:
                out_copy(c - STATE_OUT_SLOTS).wait()
            for j in range(STATE_CHUNK):
                n = c * STATE_CHUNK + j
                for hd in heads:
                    s_prev = sin_ref[c % STATE_IN_SLOTS, j, hd]
                    o_inter[hd][n] = _dot(q_block(n, hd), s_prev.astype(BF16))
                    ksb = kst[n * chunk:(n + 1) * chunk, hd * GLA_DK:(hd + 1) * GLA_DK].astype(BF16)
                    vb = v[n * chunk:(n + 1) * chunk, hd * GLA_DV:(hd + 1) * GLA_DV].astype(BF16)
                    upd = lax.dot_general(ksb, vb, (((0,), (0,)), ((), ())),
                                          preferred_element_type=F32)
                    sout_ref[c % STATE_OUT_SLOTS, j, hd] = s_prev * decay_col(n, hd) + upd
            out_copy(c).start()
        for c in range(max(n_state_chunks - STATE_OUT_SLOTS, 0), n_state_chunks):
            out_copy(c).wait()
    gated = []
    for hd in heads:
        v0 = hd * GLA_DV
        oh = _rmsnorm(o_intras[hd] + jnp.concatenate(o_inter[hd], axis=0), gnorm_ref[...])
        gh = g[:, v0:v0 + GLA_DV]
        gated.append(oh * (gh * jax.nn.sigmoid(gh)))
    yield
    yb = _dot(jnp.concatenate(gated, axis=1).astype(BF16), wgo_ref[...])
    yield

    m = merge_a + gate_b * yb
    hh = x + _dot(m.astype(BF16), wo_ref[...])
    yield
    xn2 = _rmsnorm(hh, nffn_ref[...])
    yield
    route = _route(xn2, wr_ref, br_ref)
    rpad = (-rows) % LANES
    route_p = jnp.concatenate([route, jnp.zeros((rpad, LANES), F32)], axis=0) if rpad else route
    route_t_ref[0, :, r0:r0 + rows] = route_p.T[:8, :rows]
    if prompt:
        h_ref[0, r0:r0 + rows, :] = hh
        xn2_ref[0, r0:r0 + rows, :] = _pack_bf16_pairs(xn2)
        route_ref[0, r0:r0 + rows, :] = route
    else:
        h_ref[...] = hh.reshape(nseq, chunk, D_MODEL)
        xn2_ref[...] = _pack_bf16_pairs(xn2).reshape(nseq, chunk, D_MODEL // 2)
        route_ref[...] = route.reshape(nseq, chunk, LANES)


def _mixer_kernel(*refs, prompt):
    if prompt:
        s_ref, ubuf_ref = refs[-2:]

        @pl.when(pl.program_id(1) == 0)
        def _():
            s_ref[...] = jnp.zeros_like(s_ref)
            ubuf_ref[...] = jnp.zeros_like(ubuf_ref)
    nsub = PROMPT_TILE // GLA_BLOCK if prompt else 1
    stages = [_mixer_stages(sub, nsub, *refs, prompt=prompt) for sub in range(nsub)]
    done = [False] * nsub
    step = 0
    while not all(done):
        for sub, st in enumerate(stages):
            if step >= sub * STAGE_SKEW and not done[sub]:
                done[sub] = next(st, "end") == "end"
        step += 1


def _mixer_weights(norm_mix, w_in, w_conv, w_alpha_up, b_alpha, gla_norm, w_conv_out, w_gla_out, w_o,
                   norm_ffn, w_group_router, b_group_router, w_expert_router, b_expert_router):
    wi = w_in[0]
    w_main = wi[:, :OFF_GA].astype(BF16)
    w_gates = wi[:, OFF_GA + GLA_RANK:].astype(BF16)
    w_alr = jnp.concatenate([wi[:, OFF_GA:OFF_GA + GLA_RANK],
                             jnp.zeros((D_MODEL, LANES - GLA_RANK), F32)], axis=1).astype(BF16)
    walpha = jnp.concatenate(
        [w_alpha_up[0], jnp.zeros((LANES - GLA_RANK, D_CONV), F32)], axis=0).astype(BF16)
    wr = jnp.concatenate(
        [w_group_router[0], w_expert_router[0],
         jnp.zeros((D_MODEL, LANES - N_GROUPS - N_EXPERTS), F32)], axis=1)
    wr_hi = wr.astype(BF16)
    wr_split = jnp.concatenate([wr_hi, (wr - wr_hi.astype(F32)).astype(BF16)], axis=1)
    br = jnp.concatenate([b_group_router[0], b_expert_router[0],
                          jnp.zeros((LANES - N_GROUPS - N_EXPERTS,), F32)])[None, :]
    return (norm_mix, w_main, w_gates, w_alr, w_conv[0], walpha, b_alpha, gla_norm,
            w_conv_out[0].astype(BF16), w_gla_out[0].astype(BF16), w_o[0].astype(BF16),
            norm_ffn, wr_split, br)


def _const_spec(shape):
    nd = len(shape)
    return pl.BlockSpec(shape, lambda *_: (0,) * nd, pipeline_mode=pl.Buffered(1))


def _mixer_call(x, conv_state, gla_state, weights, *, prompt):
    nb, seq, _ = x.shape
    if prompt:
        grid = (nb, seq // PROMPT_TILE)
        tok = lambda last: pl.BlockSpec((1, PROMPT_TILE, last), lambda b, i: (b, i, 0))
        conv_spec = pl.BlockSpec((1, CONV_WIDTH - 1, D_CONV), lambda b, i: (b, 0, 0))
        gla_spec = pl.BlockSpec((1, GLA_HEADS, GLA_DK, GLA_DV), lambda b, i: (b, 0, 0, 0))
        gla_in_spec = pl.BlockSpec((1, 1, 8, LANES), lambda b, i: (b, 0, 0, 0))
        scratch = [pltpu.VMEM((GLA_HEADS, GLA_DK, GLA_DV), F32),
                   pltpu.VMEM((1, PROMPT_TILE + 8, D_CONV), F32)]
        route_t_shape = (nb, 8, seq)
        route_t_spec = pl.BlockSpec((1, 8, PROMPT_TILE), lambda b, i: (b, 0, i))
    else:
        grid = (nb // SAMPLE_SEQS, 1)
        tok = lambda last: pl.BlockSpec((SAMPLE_SEQS, seq, last), lambda b, i: (b, 0, 0))
        conv_spec = pl.BlockSpec((SAMPLE_SEQS, CONV_WIDTH - 1, D_CONV), lambda b, i: (b, 0, 0))
        gla_spec = pl.BlockSpec(memory_space=pl.ANY)
        gla_in_spec = gla_spec
        state_chunk = (STATE_CHUNK, GLA_HEADS, GLA_DK, GLA_DV)
        scratch = [pltpu.VMEM((1, 8, LANES), F32),
                   pltpu.VMEM((SAMPLE_SEQS, 8 + seq, D_CONV), F32),
                   pltpu.VMEM((STATE_IN_SLOTS,) + state_chunk, F32),
                   pltpu.VMEM((STATE_OUT_SLOTS,) + state_chunk, F32),
                   pltpu.SemaphoreType.DMA((STATE_IN_SLOTS,)),
                   pltpu.SemaphoreType.DMA((STATE_OUT_SLOTS,))]
        route_t_shape = (nb // SAMPLE_SEQS, 8, SAMPLE_SEQS * seq)
        route_t_spec = pl.BlockSpec((1, 8, SAMPLE_SEQS * seq), lambda b, i: (b, 0, 0))
    out_shape = (jax.ShapeDtypeStruct((nb, seq, D_MODEL), F32),
                 jax.ShapeDtypeStruct((nb, seq, D_MODEL // 2), jnp.uint32),
                 jax.ShapeDtypeStruct((nb, seq, LANES), F32),
                 jax.ShapeDtypeStruct(route_t_shape, F32),
                 jax.ShapeDtypeStruct((nb, CONV_WIDTH - 1, D_CONV), F32),
                 jax.ShapeDtypeStruct((nb, GLA_HEADS, GLA_DK, GLA_DV), F32))
    return pl.pallas_call(
        functools.partial(_mixer_kernel, prompt=prompt),
        grid=grid,
        in_specs=[tok(D_MODEL), conv_spec, gla_in_spec] + [_const_spec(w.shape) for w in weights],
        out_specs=(tok(D_MODEL), tok(D_MODEL // 2), tok(LANES), route_t_spec, conv_spec, gla_spec),
        out_shape=out_shape,
        scratch_shapes=scratch,
        compiler_params=pltpu.CompilerParams(
            dimension_semantics=("arbitrary", "arbitrary"), vmem_limit_bytes=VMEM_LIMIT),
        name="mixer_prompt" if prompt else "mixer_sample",
    )(x, conv_state, gla_state, *weights)


def _rank_kernel(eid_ref, rank_ref, cnt_ref, carry_ref):
    @pl.when(pl.program_id(0) == 0)
    def _():
        carry_ref[...] = jnp.zeros_like(carry_ref)

    n = RANK_BLOCK
    e_iota = lax.broadcasted_iota(jnp.int32, (N_EXPERTS, n), 0)
    r = lax.broadcasted_iota(jnp.int32, (n, n), 0)
    c = lax.broadcasted_iota(jnp.int32, (n, n), 1)
    before = jnp.where(r < c, 1.0, 0.0).astype(BF16)
    carry = carry_ref[...]
    for j in range(eid_ref.shape[1] // n):
        cols = slice(j * n, (j + 1) * n)
        oh0 = jnp.where(e_iota == eid_ref[0:1, cols], 1.0, 0.0)
        oh1 = jnp.where(e_iota == eid_ref[1:2, cols], 1.0, 0.0)
        cnt = oh0 + oh1
        base = carry + _dot(cnt.astype(BF16), before)
        rank0 = jnp.sum(oh0 * base, axis=0, keepdims=True)
        rank1 = jnp.sum(oh1 * base, axis=0, keepdims=True)
        rank_ref[:, cols] = jnp.concatenate([rank0, rank1], axis=0).astype(jnp.int32)
        carry = carry + jnp.sum(cnt, axis=1, keepdims=True)
    carry_ref[...] = carry
    cnt_ref[...] = jnp.broadcast_to(carry, cnt_ref.shape)


def _rank_call(eid):
    ntok = eid.shape[1]
    return pl.pallas_call(
        _rank_kernel,
        grid=(ntok // RANK_TILE,),
        in_specs=[pl.BlockSpec((2, RANK_TILE), lambda i: (0, i))],
        out_specs=(pl.BlockSpec((2, RANK_TILE), lambda i: (0, i)),
                   pl.BlockSpec((N_EXPERTS, LANES), lambda i: (0, 0))),
        out_shape=(jax.ShapeDtypeStruct((2, ntok), jnp.int32),
                   jax.ShapeDtypeStruct((N_EXPERTS, LANES), F32)),
        scratch_shapes=[pltpu.VMEM((N_EXPERTS, 1), F32)],
        compiler_params=pltpu.CompilerParams(dimension_semantics=("arbitrary",)),
        name="expert_rank",
    )(eid)


def _sc_mesh():
    return plsc.VectorSubcoreMesh(core_axis_name="c", subcore_axis_name="s")


def _sc_worker_windows(n_windows, body):
    info = plsc.get_sparse_core_info()
    n_workers = info.num_cores * info.num_subcores
    wid = lax.axis_index("s") * info.num_cores + lax.axis_index("c")

    @pl.loop(0, pl.cdiv(n_windows, n_workers))
    def _(j):
        win = wid + n_workers * j

        @pl.when(win < n_windows)
        def _():
            body(win)


def _dispatch_rows(sources, pos, n_rows):
    width, dtype = sources[0].shape[1], sources[0].dtype
    n_win = [src.shape[0] // SC_WINDOW for src in sources]
    first_win = [sum(n_win[:i]) for i in range(len(sources))]
    total_win = sum(n_win)
    pos3 = pos.reshape(2, total_win, SC_WINDOW).transpose(1, 0, 2)

    @functools.partial(pl.kernel, mesh=_sc_mesh(), name="moe_dispatch",
                       out_type=jax.ShapeDtypeStruct((n_rows, width), dtype),
                       scratch_types=[pltpu.VMEM((2, SC_WINDOW), jnp.int32),
                                      pltpu.VMEM((SC_WINDOW, width), dtype)])
    def run(*refs):
        src_hbm, (pos_hbm, out_hbm, idx_v, buf) = refs[:len(sources)], refs[len(sources):]

        def window(win):
            pltpu.sync_copy(pos_hbm.at[win], idx_v)
            for src, w0, nw in zip(src_hbm, first_win, n_win):
                @pl.when((win >= w0) & (win < w0 + nw))
                def _():
                    r0 = pl.multiple_of((win - w0) * SC_WINDOW, SC_WINDOW)
                    pltpu.sync_copy(src.at[pl.ds(r0, SC_WINDOW)], buf)

            pltpu.sync_copy(buf, out_hbm.at[idx_v.at[0]])
            pltpu.sync_copy(buf, out_hbm.at[idx_v.at[1]])

        _sc_worker_windows(total_win, window)

    return run(*sources, pos3)


def _return_rows(ys, idx):
    n_out, width = idx.shape[0], ys.shape[1]
    n_windows = n_out // SC_WINDOW

    @functools.partial(pl.kernel, mesh=_sc_mesh(), name="moe_return",
                       out_type=jax.ShapeDtypeStruct((n_out, width), ys.dtype),
                       scratch_types=[pltpu.VMEM((SC_WINDOW,), jnp.int32),
                                      pltpu.VMEM((SC_WINDOW, width), ys.dtype)])
    def run(ys_hbm, idx_hbm, out_hbm, idx_v, buf):
        def window(win):
            pltpu.sync_copy(idx_hbm.at[win], idx_v)
            pltpu.sync_copy(ys_hbm.at[idx_v], buf)
            r0 = pl.multiple_of(win * SC_WINDOW, SC_WINDOW)
            pltpu.sync_copy(buf, out_hbm.at[pl.ds(r0, SC_WINDOW)])

        _sc_worker_windows(n_windows, window)

    return run(ys, idx.reshape(n_windows, SC_WINDOW))


def _expert_kernel(first_ref, ntile_ref, total_ref, xs_hbm, wg_ref, wu_ref, wd_ref, out_hbm,
                   xbuf, obuf, wg_s, wu_s, wd_s, sem_in, sem_out):
    e = pl.program_id(0)
    nt = ntile_ref[e]
    g0 = first_ref[e]
    total = total_ref[0]
    ring = EXPERT_RING

    def rows_of(g):
        return pl.ds(pl.multiple_of(g * ROW_TILE, ROW_TILE), ROW_TILE)

    def in_copy(g):
        slot = lax.rem(g, ring)
        return pltpu.make_async_copy(xs_hbm.at[rows_of(g)], xbuf.at[slot], sem_in.at[slot])

    def out_copy(g):
        slot = lax.rem(g, ring)
        return pltpu.make_async_copy(obuf.at[slot], out_hbm.at[rows_of(g)], sem_out.at[slot])

    @pl.when(e == 0)
    def _():
        for g in range(ring - 1):
            @pl.when(g < total)
            def _():
                in_copy(g).start()

    @pl.when(nt > 0)
    def _():
        wg_s[...] = wg_ref[0].astype(BF16)
        wu_s[...] = wu_ref[0].astype(BF16)
        wd_s[...] = wd_ref[0].astype(BF16)

    def tile(g, carry):
        slot = lax.rem(g, ring)
        in_copy(g).wait()

        @pl.when(g + (ring - 1) < total)
        def _():
            in_copy(g + (ring - 1)).start()

        @pl.when(g >= ring)
        def _():
            out_copy(g - ring).wait()

        lo, hi = _unpack_bf16_pairs(xbuf[slot])
        half = D_MODEL // 2
        sub = ROW_TILE // 2
        parts = []
        for r in (0, sub):
            xl, xh = lo[r:r + sub].astype(BF16), hi[r:r + sub].astype(BF16)
            gt = _dot(xl, wg_s[:half, :]) + _dot(xh, wg_s[half:, :])
            up = _dot(xl, wu_s[:half, :]) + _dot(xh, wu_s[half:, :])
            parts.append((gt, up))
        for r, (gt, up) in zip((0, sub), parts):
            act = (gt * jax.nn.sigmoid(gt)) * up
            obuf[slot, r:r + sub, :] = _pack_bf16_pairs(_dot(act.astype(BF16), wd_s[...]))
        out_copy(g).start()
        return carry

    lax.fori_loop(g0, g0 + nt, tile, 0)

    @pl.when(e == pl.num_programs(0) - 1)
    def _():
        for k in range(1, ring + 1):
            @pl.when(total >= k)
            def _():
                out_copy(total - k).wait()


def _expert_call(first_tile, n_tiles_per_expert, total_tiles, xs, w_gate, w_up, w_down):
    nrows = xs.shape[0]
    by_expert = lambda e, *_: (e, 0, 0)
    half_row = D_MODEL // 2
    return pl.pallas_call(
        _expert_kernel,
        grid_spec=pltpu.PrefetchScalarGridSpec(
            num_scalar_prefetch=3,
            grid=(N_EXPERTS,),
            in_specs=[pl.BlockSpec(memory_space=pl.ANY),
                      pl.BlockSpec((1, D_MODEL, D_EXPERT), by_expert),
                      pl.BlockSpec((1, D_MODEL, D_EXPERT), by_expert),
                      pl.BlockSpec((1, D_EXPERT, D_MODEL), by_expert)],
            out_specs=pl.BlockSpec(memory_space=pl.ANY),
            scratch_shapes=[pltpu.VMEM((EXPERT_RING, ROW_TILE, half_row), jnp.uint32),
                            pltpu.VMEM((EXPERT_RING, ROW_TILE, half_row), jnp.uint32),
                            pltpu.VMEM((D_MODEL, D_EXPERT), BF16),
                            pltpu.VMEM((D_MODEL, D_EXPERT), BF16),
                            pltpu.VMEM((D_EXPERT, D_MODEL), BF16),
                            pltpu.SemaphoreType.DMA((EXPERT_RING,)),
                            pltpu.SemaphoreType.DMA((EXPERT_RING,))]),
        out_shape=jax.ShapeDtypeStruct((nrows, half_row), jnp.uint32),
        compiler_params=pltpu.CompilerParams(
            dimension_semantics=("arbitrary",), vmem_limit_bytes=VMEM_LIMIT),
        name="expert_mlp",
    )(first_tile, n_tiles_per_expert, total_tiles, xs, w_gate, w_up, w_down)


def _combine_kernel(h_ref, y2_ref, route_ref, nfin_ref, out_ref):
    route = route_ref[...]
    w1, w2 = route[:, 0:1], route[:, 1:2]
    lo1, hi1 = _unpack_bf16_pairs(y2_ref[0])
    lo2, hi2 = _unpack_bf16_pairs(y2_ref[1])
    y = h_ref[...] + jnp.concatenate([w1 * lo1 + w2 * lo2, w1 * hi1 + w2 * hi2], axis=1)
    out_ref[...] = _rmsnorm(y, nfin_ref[...])


def _combine_call(h, y2, route, nfin, first_token):
    n = h.shape[0]
    first_block = first_token // COMBINE_TILE
    return pl.pallas_call(
        _combine_kernel,
        grid=(n // COMBINE_TILE,),
        in_specs=[pl.BlockSpec((COMBINE_TILE, D_MODEL), lambda i: (i, 0)),
                  pl.BlockSpec((2, COMBINE_TILE, D_MODEL // 2), lambda i: (0, first_block + i, 0)),
                  pl.BlockSpec((COMBINE_TILE, LANES), lambda i: (i, 0)),
                  pl.BlockSpec((1, D_MODEL), lambda i: (0, 0))],
        out_specs=pl.BlockSpec((COMBINE_TILE, D_MODEL), lambda i: (i, 0)),
        out_shape=jax.ShapeDtypeStruct((n, D_MODEL), F32),
        compiler_params=pltpu.CompilerParams(
            dimension_semantics=("arbitrary",), vmem_limit_bytes=VMEM_LIMIT),
        name="moe_combine",
    )(h, y2, route, nfin)


def _moe_rows(parts, w_gate, w_up, w_down):
    eid = jnp.concatenate([ids for _, ids in parts], axis=1).astype(jnp.int32)
    ntok = eid.shape[1]
    assert ntok % RANK_TILE == 0 and ntok % SC_WINDOW == 0
    rank, cnt = _rank_call(eid)
    counts = cnt[:, 0].astype(jnp.int32)
    padded = ((counts + ROW_TILE - 1) // ROW_TILE) * ROW_TILE
    ends = jnp.cumsum(padded)
    starts = ends - padded
    expert_iota = jnp.arange(N_EXPERTS, dtype=jnp.int32)[:, None, None]
    pos = rank + jnp.sum(jnp.where(eid[None] == expert_iota, starts[:, None, None], 0), axis=0)
    n_rows = (2 * ntok + N_EXPERTS * (ROW_TILE - 1)) // ROW_TILE * ROW_TILE
    xs = _dispatch_rows([rows for rows, _ in parts], pos, n_rows)
    ys = _expert_call(starts // ROW_TILE, padded // ROW_TILE, ends[-1:] // ROW_TILE, xs,
                      w_gate, w_up, w_down)
    return _return_rows(ys, pos.reshape(-1)).reshape(2, ntok, D_MODEL // 2)


def kernel(x_prompt, x_sample, state_conv, state_gla, norm_mix, w_in, w_conv, w_alpha_up, b_alpha,
           gla_norm, w_conv_out, w_gla_out, w_o, norm_ffn, w_group_router, b_group_router,
           w_expert_router, b_expert_router, w_gate, w_up, w_down, norm_final):
    nbp, seq_p, _ = x_prompt.shape
    nbs, seq_s, _ = x_sample.shape
    assert norm_mix.shape[0] == 1, "single layer"
    assert seq_p % PROMPT_TILE == 0 and nbs % SAMPLE_SEQS == 0 and seq_s == 8

    weights = _mixer_weights(norm_mix, w_in, w_conv, w_alpha_up, b_alpha, gla_norm, w_conv_out,
                             w_gla_out, w_o, norm_ffn, w_group_router, b_group_router,
                             w_expert_router, b_expert_router)

    ntp, nts = nbp * seq_p, nbs * seq_s
    assert ntp % COMBINE_TILE == 0 and nts % COMBINE_TILE == 0
    conv0 = jnp.zeros((nbp, CONV_WIDTH - 1, D_CONV), F32)
    gla0 = jnp.zeros((nbp, 1, 8, LANES), F32)
    flat = lambda a: a.reshape(-1, a.shape[-1])
    ids_of = lambda route_t: route_t[:, 2:4, :].transpose(1, 0, 2).reshape(2, -1)
    h_p, xn_p, rt_p, rtt_p, conv_p, gla_p = _mixer_call(x_prompt, conv0, gla0, weights, prompt=True)
    h_s, xn_s, rt_s, rtt_s, conv_s, gla_s = _mixer_call(x_sample, state_conv[0], state_gla[0],
                                                        weights, prompt=False)
    y2 = _moe_rows([(flat(xn_p), ids_of(rtt_p)), (flat(xn_s), ids_of(rtt_s))],
                   w_gate[0], w_up[0], w_down[0])
    nfin = norm_final[None, :]
    y_p = _combine_call(flat(h_p), y2, flat(rt_p), nfin, 0)
    y_s = _combine_call(flat(h_s), y2, flat(rt_s), nfin, ntp)
    return (y_p.reshape(nbp, seq_p, D_MODEL), y_s.reshape(nbs, seq_s, D_MODEL),
            conv_p[None], gla_p[None], conv_s[None], gla_s[None])
```

```python
import functools

import jax
import jax.numpy as jnp
from jax import lax
from jax.experimental import pallas as pl
from jax.experimental.pallas import tpu as pltpu
from jax.experimental.pallas import tpu_sc as plsc

F32 = jnp.float32
BF16 = jnp.bfloat16

D_MODEL = 1024
D_CONV = 512
CONV_WIDTH = 3
GLA_HEADS = 4
GLA_DK = 128
GLA_DV = 256
GLA_RANK = 16
GLA_GATE_NORM = 16.0
GLA_CHUNK = 32
N_GROUPS = 4
EXPERTS_PER_GROUP = 8
N_EXPERTS = 32
D_EXPERT = 256
EPS = 1e-6

LANES = 128
OFF_CB, OFF_CC, OFF_CH = 0, 512, 1024
OFF_Q, OFF_K, OFF_V, OFF_G = 1536, 2048, 2560, 3584
OFF_GA, OFF_GB, OFF_AL = 4608, 5632, 6656
D_IN_PACKED = OFF_AL + LANES

PROMPT_TILE = 512
GLA_BLOCK = 256
STAGE_SKEW = 0
SAMPLE_SEQS = 32
STATE_CHUNK = 4
STATE_IN_SLOTS = 8
STATE_OUT_SLOTS = 2
ROW_TILE = 256
EXPERT_RING = 6
SC_WINDOW = 128
COMBINE_TILE = 512
RANK_TILE = 1024
RANK_BLOCK = 512
VMEM_LIMIT = 56 * 1024 * 1024


def _rmsnorm(x, g):
    ms = jnp.mean(x * x, axis=-1, keepdims=True)
    return (x * lax.rsqrt(ms + EPS)) * g


def _split_bf16(x):
    hi = x.astype(BF16)
    lo = (x - hi.astype(F32)).astype(BF16)
    return hi, lo


def _pack_bf16_pairs(x):
    n = x.shape[1] // 2
    lo = lax.bitcast_convert_type(x[:, :n].astype(BF16).astype(F32), jnp.uint32)
    hi = lax.bitcast_convert_type(x[:, n:].astype(BF16).astype(F32), jnp.uint32)
    return lax.shift_right_logical(lo, jnp.uint32(16)) | hi


def _unpack_bf16_pairs(w):
    lo = lax.bitcast_convert_type(lax.shift_left(w, jnp.uint32(16)), F32)
    hi = lax.bitcast_convert_type(w & jnp.uint32(0xFFFF0000), F32)
    return lo, hi


def _dot(a, b):
    return jnp.dot(a, b, preferred_element_type=F32)


def _chunk_causal(n, chunk):
    r = lax.broadcasted_iota(jnp.int32, (n, n), 0)
    c = lax.broadcasted_iota(jnp.int32, (n, n), 1)
    shift = chunk.bit_length() - 1
    assert chunk == 1 << shift
    same = lax.shift_right_arithmetic(r, shift) == lax.shift_right_arithmetic(c, shift)
    return same & (c <= r)


def _route(xn2, wr_ref, br_ref):
    rows = xn2.shape[0]
    xh, xl = _split_bf16(xn2)
    part = _dot(xh, wr_ref[...]) + _dot(xl, wr_ref[...])
    lg = part[:, :LANES] + part[:, LANES:] + br_ref[...]
    lane_i = lax.broadcasted_iota(jnp.int32, (rows, LANES), 1)
    lane = lane_i.astype(F32)
    neg = -jnp.inf
    is_g = lane_i < N_GROUPS
    gm = jnp.max(jnp.where(is_g, lg, neg), axis=1, keepdims=True)
    gs = jnp.sum(jnp.where(is_g, jnp.exp(lg - gm), 0.0), axis=1, keepdims=True)
    g_top = 1.0 / gs
    g_idx = jnp.min(jnp.where(is_g & (lg == gm), lane, float(LANES)), axis=1, keepdims=True)
    e_group = lax.shift_right_arithmetic(lane_i - N_GROUPS, 3).astype(F32)
    sel = (lane_i >= N_GROUPS) & (lane_i < N_GROUPS + N_EXPERTS) & (e_group == g_idx)
    el = jnp.where(sel, lg, neg)
    e1 = jnp.max(el, axis=1, keepdims=True)
    i1 = jnp.min(jnp.where(sel & (el == e1), lane, float(LANES)), axis=1, keepdims=True)
    sel2 = sel & (lane != i1)
    el2 = jnp.where(sel2, lg, neg)
    e2 = jnp.max(el2, axis=1, keepdims=True)
    i2 = jnp.min(jnp.where(sel2 & (el2 == e2), lane, float(LANES)), axis=1, keepdims=True)
    d = jnp.exp(e2 - e1)
    w1 = g_top / (1.0 + d)
    w2 = (g_top * d) / (1.0 + d)
    id1 = i1 - float(N_GROUPS)
    id2 = i2 - float(N_GROUPS)
    return jnp.where(lane_i == 0, w1, jnp.where(lane_i == 1, w2, jnp.where(lane_i == 2, id1,
                     jnp.where(lane_i == 3, id2, 0.0))))


def _mixer_stages(sub, nsub, x_ref, conv_in_ref, gla_in_ref,
                  nmix_ref, w_ref, wgate_ref, walr_ref, wconv_ref, walpha_ref, balpha_ref, gnorm_ref,
                  wco_ref, wgo_ref, wo_ref, nffn_ref, wr_ref, br_ref,
                  h_ref, xn2_ref, route_ref, route_t_ref, conv_out_ref, gla_out_ref,
                  s_ref, ubuf_ref, *state_ring, prompt):
    if prompt:
        rows, chunk, nseq = GLA_BLOCK, GLA_CHUNK, 1
        r0 = sub * rows
        x = x_ref[0, r0:r0 + rows, :]
    else:
        nseq, chunk = SAMPLE_SEQS, x_ref.shape[1]
        rows = nseq * chunk
        r0 = 0
        x = x_ref[...].reshape(rows, D_MODEL)
        sin_ref, sout_ref, sem_in, sem_out = state_ring
        n_state_chunks = nseq // STATE_CHUNK
        seq0 = pl.program_id(0) * nseq

        def chunk_seqs(c):
            return pl.ds(pl.multiple_of(seq0 + c * STATE_CHUNK, STATE_CHUNK), STATE_CHUNK)

        def in_copy(c):
            slot = c % STATE_IN_SLOTS
            return pltpu.make_async_copy(gla_in_ref.at[chunk_seqs(c)], sin_ref.at[slot],
                                         sem_in.at[slot])

        def out_copy(c):
            slot = c % STATE_OUT_SLOTS
            return pltpu.make_async_copy(sout_ref.at[slot], gla_out_ref.at[chunk_seqs(c)],
                                         sem_out.at[slot])

        for c in range(min(STATE_IN_SLOTS - 1, n_state_chunks)):
            in_copy(c).start()
    nchunks = rows // chunk
    last_sub = sub == nsub - 1

    xn = _rmsnorm(x, nmix_ref[...]).astype(BF16)
    yield

    def proj(off, n):
        if off == OFF_AL:
            return _dot(xn, walr_ref[...])
        if off >= OFF_GA:
            return _dot(xn, wgate_ref[:, off - OFF_GA:off - OFF_GA + n])
        return _dot(xn, w_ref[:, off:off + n])

    alr = proj(OFF_AL, LANES).astype(BF16)
    lap = _dot(alr, walpha_ref[...])
    la = jax.nn.log_sigmoid(lap + balpha_ref[...]) * (1.0 / GLA_GATE_NORM)
    la_hi, la_lo = _split_bf16(la)
    causal = _chunk_causal(rows, chunk)
    tri = jnp.where(causal, 1.0, 0.0).astype(BF16)
    yield
    cc = proj(OFF_CC, D_CONV)
    ch = proj(OFF_CH, D_CONV)
    cb = proj(OFF_CB, D_CONV)
    q = proj(OFF_Q, D_CONV)
    k = proj(OFF_K, D_CONV)
    yield
    b = _dot(tri, la_hi) + _dot(tri, la_lo)

    u = cc * ch
    wc = wconv_ref[...]
    if prompt:
        ubuf_ref[0, 8 + r0:8 + r0 + rows, :] = u
        z = (wc[0:1] * ubuf_ref[0, 6 + r0:6 + r0 + rows, :]
             + wc[1:2] * ubuf_ref[0, 7 + r0:7 + r0 + rows, :] + wc[2:3] * u)
        if last_sub:
            tail = u[rows - 2:rows, :]
            ubuf_ref[0, 6:8, :] = tail
            conv_out_ref[0] = tail
    else:
        u3 = u.reshape(nseq, chunk, D_CONV)
        ubuf_ref[:, 6:8, :] = conv_in_ref[...]
        ubuf_ref[:, 8:8 + chunk, :] = u3
        z3 = (wc[0:1] * ubuf_ref[:, 6:6 + chunk, :] + wc[1:2] * ubuf_ref[:, 7:7 + chunk, :]
              + wc[2:3] * u3)
        z = z3.reshape(rows, D_CONV)
        conv_out_ref[...] = ubuf_ref[:, 6 + chunk:8 + chunk, :]
    ya = _dot((cb * z).astype(BF16), wco_ref[...])
    v = proj(OFF_V, GLA_HEADS * GLA_DV)
    yield

    b3 = b.reshape(nchunks, chunk, D_CONV)
    bl3 = b3[:, chunk - 1:chunk, :]
    qin = (q * (GLA_DK ** -0.5)) * jnp.exp(b)
    kin = k * jnp.exp(-b)
    kst = (k.reshape(nchunks, chunk, D_CONV) * jnp.exp(bl3 - b3)).reshape(rows, D_CONV)
    pad = (-rows) % LANES
    rows_p = rows + pad

    def transposed(a):
        if pad:
            a = jnp.concatenate([a, jnp.zeros((pad, a.shape[1]), F32)], axis=0)
        return a.T

    b_t = transposed(b)
    kst_t = transposed(kst) if prompt else None
    col_chunk = lax.shift_right_arithmetic(
        lax.broadcasted_iota(jnp.int32, (GLA_DK, rows_p), 1), chunk.bit_length() - 1)
    g = proj(OFF_G, GLA_HEADS * GLA_DV)
    yield
    heads = range(GLA_HEADS)
    vhs = [v[:, hd * GLA_DV:(hd + 1) * GLA_DV].astype(BF16) for hd in heads]
    scs = [lax.dot_general(qin[:, hd * GLA_DK:(hd + 1) * GLA_DK].astype(BF16),
                           kin[:, hd * GLA_DK:(hd + 1) * GLA_DK].astype(BF16),
                           (((1,), (1,)), ((), ())), preferred_element_type=F32) for hd in heads]
    ga = proj(OFF_GA, D_MODEL)
    yield
    upds = []
    for hd in heads if prompt else ():
        kst_h = kst_t[hd * GLA_DK:(hd + 1) * GLA_DK, :]
        stacked = jnp.concatenate(
            [jnp.where(col_chunk == n, kst_h, 0.0).astype(BF16) for n in range(nchunks)], axis=0)
        vh_p = jnp.concatenate([vhs[hd], jnp.zeros((pad, GLA_DV), BF16)], axis=0) if pad else vhs[hd]
        upds.append(_dot(stacked, vh_p))
    o_intras = [_dot(jnp.where(causal, scs[hd], 0.0).astype(BF16), vhs[hd]) for hd in heads]
    gb = proj(OFF_GB, D_MODEL)
    merge_a = jax.nn.sigmoid(ga) * ya
    gate_b = jax.nn.sigmoid(gb)
    yield
    def q_block(n, hd):
        return qin[n * chunk:(n + 1) * chunk, hd * GLA_DK:(hd + 1) * GLA_DK].astype(BF16)

    def decay_col(n, hd):
        last = n * chunk + chunk - 1
        return jnp.exp(b_t[hd * GLA_DK:(hd + 1) * GLA_DK, last:last + 1])

    o_inter = [[None] * nchunks for _ in heads]
    if prompt:
        for hd in heads:
            s_run = s_ref[hd]
            for n in range(nchunks):
                o_inter[hd][n] = _dot(q_block(n, hd), s_run.astype(BF16))
                s_run = s_run * decay_col(n, hd) + upds[hd][n * GLA_DK:(n + 1) * GLA_DK, :]
            s_ref[hd] = s_run
            if last_sub:
                gla_out_ref[0, hd] = s_run
    else:
        for c in range(n_state_chunks):
            in_copy(c).wait()
            if c + STATE_IN_SLOTS - 1 < n_state_chunks:
                in_copy(c + STATE_IN_SLOTS - 1).start()
            if c >= STATE_OUT_SLOTS:
                out_copy(c - STATE_OUT_SLOTS).wait()
            for j in range(STATE_CHUNK):
                n = c * STATE_CHUNK + j
                for hd in heads:
                    s_prev = sin_ref[c % STATE_IN_SLOTS, j, hd]
                    o_inter[hd][n] = _dot(q_block(n, hd), s_prev.astype(BF16))
                    ksb = kst[n * chunk:(n + 1) * chunk, hd * GLA_DK:(hd + 1) * GLA_DK].astype(BF16)
                    vb = v[n * chunk:(n + 1) * chunk, hd * GLA_DV:(hd + 1) * GLA_DV].astype(BF16)
                    upd = lax.dot_general(ksb, vb, (((0,), (0,)), ((), ())),
                                          preferred_element_type=F32)
                    sout_ref[c % STATE_OUT_SLOTS, j, hd] = s_prev * decay_col(n, hd) + upd
            out_copy(c).start()
        for c in range(max(n_state_chunks - STATE_OUT_SLOTS, 0), n_state_chunks):
            out_copy(c).wait()
    gated = []
    for hd in heads:
        v0 = hd * GLA_DV
        oh = _rmsnorm(o_intras[hd] + jnp.concatenate(o_inter[hd], axis=0), gnorm_ref[...])
        gh = g[:, v0:v0 + GLA_DV]
        gated.append(oh * (gh * jax.nn.sigmoid(gh)))
    yield
    yb = _dot(jnp.concatenate(gated, axis=1).astype(BF16), wgo_ref[...])
    yield

    m = merge_a + gate_b * yb
    hh = x + _dot(m.astype(BF16), wo_ref[...])
    yield
    xn2 = _rmsnorm(hh, nffn_ref[...])
    yield
    route = _route(xn2, wr_ref, br_ref)
    rpad = (-rows) % LANES
    route_p = jnp.concatenate([route, jnp.zeros((rpad, LANES), F32)], axis=0) if rpad else route
    route_t_ref[0, :, r0:r0 + rows] = route_p.T[:8, :rows]
    if prompt:
        h_ref[0, r0:r0 + rows, :] = hh
        xn2_ref[0, r0:r0 + rows, :] = _pack_bf16_pairs(xn2)
        route_ref[0, r0:r0 + rows, :] = route
    else:
        h_ref[...] = hh.reshape(nseq, chunk, D_MODEL)
        xn2_ref[...] = _pack_bf16_pairs(xn2).reshape(nseq, chunk, D_MODEL // 2)
        route_ref[...] = route.reshape(nseq, chunk, LANES)


def _mixer_kernel(*refs, prompt):
    if prompt:
        s_ref, ubuf_ref = refs[-2:]

        @pl.when(pl.program_id(1) == 0)
        def _():
            s_ref[...] = jnp.zeros_like(s_ref)
            ubuf_ref[...] = jnp.zeros_like(ubuf_ref)
    nsub = PROMPT_TILE // GLA_BLOCK if prompt else 1
    stages = [_mixer_stages(sub, nsub, *refs, prompt=prompt) for sub in range(nsub)]
    done = [False] * nsub
    step = 0
    while not all(done):
        for sub, st in enumerate(stages):
            if step >= sub * STAGE_SKEW and not done[sub]:
                done[sub] = next(st, "end") == "end"
        step += 1


def _mixer_weights(norm_mix, w_in, w_conv, w_alpha_up, b_alpha, gla_norm, w_conv_out, w_gla_out, w_o,
                   norm_ffn, w_group_router, b_group_router, w_expert_router, b_expert_router):
    wi = w_in[0]
    w_main = wi[:, :OFF_GA].astype(BF16)
    w_gates = wi[:, OFF_GA + GLA_RANK:].astype(BF16)
    w_alr = jnp.concatenate([wi[:, OFF_GA:OFF_GA + GLA_RANK],
                             jnp.zeros((D_MODEL, LANES - GLA_RANK), F32)], axis=1).astype(BF16)
    walpha = jnp.concatenate(
        [w_alpha_up[0], jnp.zeros((LANES - GLA_RANK, D_CONV), F32)], axis=0).astype(BF16)
    wr = jnp.concatenate(
        [w_group_router[0], w_expert_router[0],
         jnp.zeros((D_MODEL, LANES - N_GROUPS - N_EXPERTS), F32)], axis=1)
    wr_hi = wr.astype(BF16)
    wr_split = jnp.concatenate([wr_hi, (wr - wr_hi.astype(F32)).astype(BF16)], axis=1)
    br = jnp.concatenate([b_group_router[0], b_expert_router[0],
                          jnp.zeros((LANES - N_GROUPS - N_EXPERTS,), F32)])[None, :]
    return (norm_mix, w_main, w_gates, w_alr, w_conv[0], walpha, b_alpha, gla_norm,
            w_conv_out[0].astype(BF16), w_gla_out[0].astype(BF16), w_o[0].astype(BF16),
            norm_ffn, wr_split, br)


def _const_spec(shape):
    nd = len(shape)
    return pl.BlockSpec(shape, lambda *_: (0,) * nd, pipeline_mode=pl.Buffered(1))


def _mixer_call(x, conv_state, gla_state, weights, *, prompt):
    nb, seq, _ = x.shape
    if prompt:
        grid = (nb, seq // PROMPT_TILE)
        tok = lambda last: pl.BlockSpec((1, PROMPT_TILE, last), lambda b, i: (b, i, 0))
        conv_spec = pl.BlockSpec((1, CONV_WIDTH - 1, D_CONV), lambda b, i: (b, 0, 0))
        gla_spec = pl.BlockSpec((1, GLA_HEADS, GLA_DK, GLA_DV), lambda b, i: (b, 0, 0, 0))
        gla_in_spec = pl.BlockSpec((1, 1, 8, LANES), lambda b, i: (b, 0, 0, 0))
        scratch = [pltpu.VMEM((GLA_HEADS, GLA_DK, GLA_DV), F32),
                   pltpu.VMEM((1, PROMPT_TILE + 8, D_CONV), F32)]
        route_t_shape = (nb, 8, seq)
        route_t_spec = pl.BlockSpec((1, 8, PROMPT_TILE), lambda b, i: (b, 0, i))
    else:
        grid = (nb // SAMPLE_SEQS, 1)
        tok = lambda last: pl.BlockSpec((SAMPLE_SEQS, seq, last), lambda b, i: (b, 0, 0))
        conv_spec = pl.BlockSpec((SAMPLE_SEQS, CONV_WIDTH - 1, D_CONV), lambda b, i: (b, 0, 0))
        gla_spec = pl.BlockSpec(memory_space=pl.ANY)
        gla_in_spec = gla_spec
        state_chunk = (STATE_CHUNK, GLA_HEADS, GLA_DK, GLA_DV)
        scratch = [pltpu.VMEM((1, 8, LANES), F32),
                   pltpu.VMEM((SAMPLE_SEQS, 8 + seq, D_CONV), F32),
                   pltpu.VMEM((STATE_IN_SLOTS,) + state_chunk, F32),
                   pltpu.VMEM((STATE_OUT_SLOTS,) + state_chunk, F32),
                   pltpu.SemaphoreType.DMA((STATE_IN_SLOTS,)),
                   pltpu.SemaphoreType.DMA((STATE_OUT_SLOTS,))]
        route_t_shape = (nb // SAMPLE_SEQS, 8, SAMPLE_SEQS * seq)
        route_t_spec = pl.BlockSpec((1, 8, SAMPLE_SEQS * seq), lambda b, i: (b, 0, 0))
    out_shape = (jax.ShapeDtypeStruct((nb, seq, D_MODEL), F32),
                 jax.ShapeDtypeStruct((nb, seq, D_MODEL // 2), jnp.uint32),
                 jax.ShapeDtypeStruct((nb, seq, LANES), F32),
                 jax.ShapeDtypeStruct(route_t_shape, F32),
                 jax.ShapeDtypeStruct((nb, CONV_WIDTH - 1, D_CONV), F32),
                 jax.ShapeDtypeStruct((nb, GLA_HEADS, GLA_DK, GLA_DV), F32))
    return pl.pallas_call(
        functools.partial(_mixer_kernel, prompt=prompt),
        grid=grid,
        in_specs=[tok(D_MODEL), conv_spec, gla_in_spec] + [_const_spec(w.shape) for w in weights],
        out_specs=(tok(D_MODEL), tok(D_MODEL // 2), tok(LANES), route_t_spec, conv_spec, gla_spec),
        out_shape=out_shape,
        scratch_shapes=scratch,
        compiler_params=pltpu.CompilerParams(
            dimension_semantics=("arbitrary", "arbitrary"), vmem_limit_bytes=VMEM_LIMIT),
        name="mixer_prompt" if prompt else "mixer_sample",
    )(x, conv_state, gla_state, *weights)


def _rank_kernel(eid_ref, rank_ref, cnt_ref, carry_ref):
    @pl.when(pl.program_id(0) == 0)
    def _():
        carry_ref[...] = jnp.zeros_like(carry_ref)

    n = RANK_BLOCK
    e_iota = lax.broadcasted_iota(jnp.int32, (N_EXPERTS, n), 0)
    r = lax.broadcasted_iota(jnp.int32, (n, n), 0)
    c = lax.broadcasted_iota(jnp.int32, (n, n), 1)
    before = jnp.where(r < c, 1.0, 0.0).astype(BF16)
    carry = carry_ref[...]
    for j in range(eid_ref.shape[1] // n):
        cols = slice(j * n, (j + 1) * n)
        oh0 = jnp.where(e_iota == eid_ref[0:1, cols], 1.0, 0.0)
        oh1 = jnp.where(e_iota == eid_ref[1:2, cols], 1.0, 0.0)
        cnt = oh0 + oh1
        base = carry + _dot(cnt.astype(BF16), before)
        rank0 = jnp.sum(oh0 * base, axis=0, keepdims=True)
        rank1 = jnp.sum(oh1 * base, axis=0, keepdims=True)
        rank_ref[:, cols] = jnp.concatenate([rank0, rank1], axis=0).astype(jnp.int32)
        carry = carry + jnp.sum(cnt, axis=1, keepdims=True)
    carry_ref[...] = carry
    cnt_ref[...] = jnp.broadcast_to(carry, cnt_ref.shape)


def _rank_call(eid):
    ntok = eid.shape[1]
    return pl.pallas_call(
        _rank_kernel,
        grid=(ntok // RANK_TILE,),
        in_specs=[pl.BlockSpec((2, RANK_TILE), lambda i: (0, i))],
        out_specs=(pl.BlockSpec((2, RANK_TILE), lambda i: (0, i)),
                   pl.BlockSpec((N_EXPERTS, LANES), lambda i: (0, 0))),
        out_shape=(jax.ShapeDtypeStruct((2, ntok), jnp.int32),
                   jax.ShapeDtypeStruct((N_EXPERTS, LANES), F32)),
        scratch_shapes=[pltpu.VMEM((N_EXPERTS, 1), F32)],
        compiler_params=pltpu.CompilerParams(dimension_semantics=("arbitrary",)),
        name="expert_rank",
    )(eid)


def _sc_mesh():
    return plsc.VectorSubcoreMesh(core_axis_name="c", subcore_axis_name="s")


def _sc_worker():
    info = plsc.get_sparse_core_info()
    n_workers = info.num_cores * info.num_subcores
    return lax.axis_index("s") * info.num_cores + lax.axis_index("c"), n_workers


def _dispatch_rows(sources, pos, n_rows):
    width, dtype = sources[0].shape[1], sources[0].dtype
    n_win = [src.shape[0] // SC_WINDOW for src in sources]
    first_win = [sum(n_win[:i]) for i in range(len(sources))]
    total_win = sum(n_win)
    pos3 = pos.reshape(2, total_win, SC_WINDOW).transpose(1, 0, 2)

    @functools.partial(pl.kernel, mesh=_sc_mesh(), name="moe_dispatch",
                       out_type=jax.ShapeDtypeStruct((n_rows, width), dtype),
                       scratch_types=[pltpu.VMEM((2, SC_WINDOW), jnp.int32),
                                      pltpu.VMEM((SC_WINDOW, width), dtype),
                                      pltpu.SemaphoreType.DMA((2,))])
    def run(*refs):
        src_hbm, (pos_hbm, out_hbm, idx_v, buf, sem) = refs[:len(sources)], refs[len(sources):]
        wid, n_workers = _sc_worker()

        @pl.loop(0, pl.cdiv(total_win - wid, n_workers))
        def _(j):
            win = wid + n_workers * j
            pltpu.sync_copy(pos_hbm.at[win], idx_v)
            for src, w0, nw in zip(src_hbm, first_win, n_win):
                @pl.when((win >= w0) & (win < w0 + nw))
                def _():
                    r0 = pl.multiple_of((win - w0) * SC_WINDOW, SC_WINDOW)
                    pltpu.sync_copy(src.at[pl.ds(r0, SC_WINDOW)], buf)

            copies = [pltpu.async_copy(buf, out_hbm.at[idx_v.at[k]], sem.at[k]) for k in range(2)]
            for cp in copies:
                cp.wait()

    return run(*sources, pos3)


def _return_rows(ys, idx):
    n_out, width = idx.shape[0], ys.shape[1]
    n_windows = n_out // SC_WINDOW
    sub = SC_WINDOW // 2

    @functools.partial(pl.kernel, mesh=_sc_mesh(), name="moe_return",
                       out_type=jax.ShapeDtypeStruct((n_out, width), ys.dtype),
                       scratch_types=[pltpu.VMEM((SC_WINDOW,), jnp.int32),
                                      pltpu.VMEM((2, sub, width), ys.dtype),
                                      pltpu.SemaphoreType.DMA,
                                      pltpu.SemaphoreType.DMA((2,))])
    def run(ys_hbm, idx_hbm, out_hbm, idx_v, buf, sem_g, sem_w):
        wid, n_workers = _sc_worker()
        n_mine = pl.cdiv(n_windows - wid, n_workers)

        def write(win, half):
            r0 = pl.multiple_of(win * SC_WINDOW + half * sub, sub)
            return pltpu.make_async_copy(buf.at[half], out_hbm.at[pl.ds(r0, sub)], sem_w.at[half])

        @pl.loop(0, n_mine)
        def _(j):
            win = wid + n_workers * j
            pltpu.sync_copy(idx_hbm.at[win], idx_v)
            for half in range(2):
                @pl.when(j > 0)
                def _():
                    write(win - n_workers, half).wait()

                pltpu.async_copy(ys_hbm.at[idx_v.at[pl.ds(half * sub, sub)]], buf.at[half],
                                 sem_g).wait()
                write(win, half).start()

        @pl.when(n_mine > 0)
        def _():
            for half in range(2):
                write(wid + n_workers * (n_mine - 1), half).wait()

    return run(ys, idx.reshape(n_windows, SC_WINDOW))


def _expert_kernel(first_ref, ntile_ref, total_ref, xs_hbm, wg_ref, wu_ref, wd_ref, out_hbm,
                   xbuf, obuf, wg_s, wu_s, wd_s, sem_in, sem_out):
    e = pl.program_id(0)
    nt = ntile_ref[e]
    g0 = first_ref[e]
    total = total_ref[0]
    ring = EXPERT_RING

    def rows_of(g):
        return pl.ds(pl.multiple_of(g * ROW_TILE, ROW_TILE), ROW_TILE)

    def in_copy(g):
        slot = lax.rem(g, ring)
        return pltpu.make_async_copy(xs_hbm.at[rows_of(g)], xbuf.at[slot], sem_in.at[slot])

    def out_copy(g):
        slot = lax.rem(g, ring)
        return pltpu.make_async_copy(obuf.at[slot], out_hbm.at[rows_of(g)], sem_out.at[slot])

    @pl.when(e == 0)
    def _():
        for g in range(ring - 1):
            @pl.when(g < total)
            def _():
                in_copy(g).start()

    @pl.when(nt > 0)
    def _():
        wg_s[...] = wg_ref[0].astype(BF16)
        wu_s[...] = wu_ref[0].astype(BF16)
        wd_s[...] = wd_ref[0].astype(BF16)

    def tile(g, carry):
        slot = lax.rem(g, ring)
        in_copy(g).wait()

        @pl.when(g + (ring - 1) < total)
        def _():
            in_copy(g + (ring - 1)).start()

        @pl.when(g >= ring)
        def _():
            out_copy(g - ring).wait()

        lo, hi = _unpack_bf16_pairs(xbuf[slot])
        half = D_MODEL // 2
        sub = ROW_TILE // 2
        parts = []
        for r in (0, sub):
            xl, xh = lo[r:r + sub].astype(BF16), hi[r:r + sub].astype(BF16)
            gt = _dot(xl, wg_s[:half, :]) + _dot(xh, wg_s[half:, :])
            up = _dot(xl, wu_s[:half, :]) + _dot(xh, wu_s[half:, :])
            parts.append((gt, up))
        for r, (gt, up) in zip((0, sub), parts):
            act = (gt * jax.nn.sigmoid(gt)) * up
            obuf[slot, r:r + sub, :] = _pack_bf16_pairs(_dot(act.astype(BF16), wd_s[...]))
        out_copy(g).start()
        return carry

    lax.fori_loop(g0, g0 + nt, tile, 0)

    @pl.when(e == pl.num_programs(0) - 1)
    def _():
        for k in range(1, ring + 1):
            @pl.when(total >= k)
            def _():
                out_copy(total - k).wait()


def _expert_call(first_tile, n_tiles_per_expert, total_tiles, xs, w_gate, w_up, w_down):
    nrows = xs.shape[0]
    by_expert = lambda e, *_: (e, 0, 0)
    half_row = D_MODEL // 2
    return pl.pallas_call(
        _expert_kernel,
        grid_spec=pltpu.PrefetchScalarGridSpec(
            num_scalar_prefetch=3,
            grid=(N_EXPERTS,),
            in_specs=[pl.BlockSpec(memory_space=pl.ANY),
                      pl.BlockSpec((1, D_MODEL, D_EXPERT), by_expert),
                      pl.BlockSpec((1, D_MODEL, D_EXPERT), by_expert),
                      pl.BlockSpec((1, D_EXPERT, D_MODEL), by_expert)],
            out_specs=pl.BlockSpec(memory_space=pl.ANY),
            scratch_shapes=[pltpu.VMEM((EXPERT_RING, ROW_TILE, half_row), jnp.uint32),
                            pltpu.VMEM((EXPERT_RING, ROW_TILE, half_row), jnp.uint32),
                            pltpu.VMEM((D_MODEL, D_EXPERT), BF16),
                            pltpu.VMEM((D_MODEL, D_EXPERT), BF16),
                            pltpu.VMEM((D_EXPERT, D_MODEL), BF16),
                            pltpu.SemaphoreType.DMA((EXPERT_RING,)),
                            pltpu.SemaphoreType.DMA((EXPERT_RING,))]),
        out_shape=jax.ShapeDtypeStruct((nrows, half_row), jnp.uint32),
        compiler_params=pltpu.CompilerParams(
            dimension_semantics=("arbitrary",), vmem_limit_bytes=VMEM_LIMIT),
        name="expert_mlp",
    )(first_tile, n_tiles_per_expert, total_tiles, xs, w_gate, w_up, w_down)


def _combine_kernel(h_ref, y2_ref, route_ref, nfin_ref, out_ref):
    route = route_ref[...]
    w1, w2 = route[:, 0:1], route[:, 1:2]
    lo1, hi1 = _unpack_bf16_pairs(y2_ref[0])
    lo2, hi2 = _unpack_bf16_pairs(y2_ref[1])
    y = h_ref[...] + jnp.concatenate([w1 * lo1 + w2 * lo2, w1 * hi1 + w2 * hi2], axis=1)
    out_ref[...] = _rmsnorm(y, nfin_ref[...])


def _combine_call(h, y2, route, nfin, first_token):
    n = h.shape[0]
    first_block = first_token // COMBINE_TILE
    return pl.pallas_call(
        _combine_kernel,
        grid=(n // COMBINE_TILE,),
        in_specs=[pl.BlockSpec((COMBINE_TILE, D_MODEL), lambda i: (i, 0)),
                  pl.BlockSpec((2, COMBINE_TILE, D_MODEL // 2), lambda i: (0, first_block + i, 0)),
                  pl.BlockSpec((COMBINE_TILE, LANES), lambda i: (i, 0)),
                  pl.BlockSpec((1, D_MODEL), lambda i: (0, 0))],
        out_specs=pl.BlockSpec((COMBINE_TILE, D_MODEL), lambda i: (i, 0)),
        out_shape=jax.ShapeDtypeStruct((n, D_MODEL), F32),
        compiler_params=pltpu.CompilerParams(
            dimension_semantics=("arbitrary",), vmem_limit_bytes=VMEM_LIMIT),
        name="moe_combine",
    )(h, y2, route, nfin)


def _moe_rows(parts, w_gate, w_up, w_down):
    eid = jnp.concatenate([ids for _, ids in parts], axis=1).astype(jnp.int32)
    ntok = eid.shape[1]
    assert ntok % RANK_TILE == 0 and ntok % SC_WINDOW == 0
    rank, cnt = _rank_call(eid)
    counts = cnt[:, 0].astype(jnp.int32)
    padded = ((counts + ROW_TILE - 1) // ROW_TILE) * ROW_TILE
    ends = jnp.cumsum(padded)
    starts = ends - padded
    expert_iota = jnp.arange(N_EXPERTS, dtype=jnp.int32)[:, None, None]
    pos = rank + jnp.sum(jnp.where(eid[None] == expert_iota, starts[:, None, None], 0), axis=0)
    n_rows = (2 * ntok + N_EXPERTS * (ROW_TILE - 1)) // ROW_TILE * ROW_TILE
    xs = _dispatch_rows([rows for rows, _ in parts], pos, n_rows)
    ys = _expert_call(starts // ROW_TILE, padded // ROW_TILE, ends[-1:] // ROW_TILE, xs,
                      w_gate, w_up, w_down)
    return _return_rows(ys, pos.reshape(-1)).reshape(2, ntok, D_MODEL // 2)


def kernel(x_prompt, x_sample, state_conv, state_gla, norm_mix, w_in, w_conv, w_alpha_up, b_alpha,
           gla_norm, w_conv_out, w_gla_out, w_o, norm_ffn, w_group_router, b_group_router,
           w_expert_router, b_expert_router, w_gate, w_up, w_down, norm_final):
    nbp, seq_p, _ = x_prompt.shape
    nbs, seq_s, _ = x_sample.shape
    assert norm_mix.shape[0] == 1, "single layer"
    assert seq_p % PROMPT_TILE == 0 and nbs % SAMPLE_SEQS == 0 and seq_s == 8

    weights = _mixer_weights(norm_mix, w_in, w_conv, w_alpha_up, b_alpha, gla_norm, w_conv_out,
                             w_gla_out, w_o, norm_ffn, w_group_router, b_group_router,
                             w_expert_router, b_expert_router)

    ntp, nts = nbp * seq_p, nbs * seq_s
    assert ntp % COMBINE_TILE == 0 and nts % COMBINE_TILE == 0
    conv0 = jnp.zeros((nbp, CONV_WIDTH - 1, D_CONV), F32)
    gla0 = jnp.zeros((nbp, 1, 8, LANES), F32)
    flat = lambda a: a.reshape(-1, a.shape[-1])
    ids_of = lambda route_t: route_t[:, 2:4, :].transpose(1, 0, 2).reshape(2, -1)
    h_p, xn_p, rt_p, rtt_p, conv_p, gla_p = _mixer_call(x_prompt, conv0, gla0, weights, prompt=True)
    h_s, xn_s, rt_s, rtt_s, conv_s, gla_s = _mixer_call(x_sample, state_conv[0], state_gla[0],
                                                        weights, prompt=False)
    y2 = _moe_rows([(flat(xn_p), ids_of(rtt_p)), (flat(xn_s), ids_of(rtt_s))],
                   w_gate[0], w_up[0], w_down[0])
    nfin = norm_final[None, :]
    y_p = _combine_call(flat(h_p), y2, flat(rt_p), nfin, 0)
    y_s = _combine_call(flat(h_s), y2, flat(rt_s), nfin, ntp)
    return (y_p.reshape(nbp, seq_p, D_MODEL), y_s.reshape(nbs, seq_s, D_MODEL),
            conv_p[None], gla_p[None], conv_s[None], gla_s[None])
```

```python
import functools

import jax
import jax.numpy as jnp
from jax import lax
from jax.experimental import pallas as pl
from jax.experimental.pallas import tpu as pltpu
from jax.experimental.pallas import tpu_sc as plsc

F32 = jnp.float32
BF16 = jnp.bfloat16

D_MODEL = 1024
D_CONV = 512
CONV_WIDTH = 3
GLA_HEADS = 4
GLA_DK = 128
GLA_DV = 256
GLA_RANK = 16
GLA_GATE_NORM = 16.0
GLA_CHUNK = 32
N_GROUPS = 4
EXPERTS_PER_GROUP = 8
N_EXPERTS = 32
D_EXPERT = 256
EPS = 1e-6

LANES = 128
OFF_CB, OFF_CC, OFF_CH = 0, 512, 1024
OFF_Q, OFF_K, OFF_V, OFF_G = 1536, 2048, 2560, 3584
OFF_GA, OFF_GB, OFF_AL = 4608, 5632, 6656
D_IN_PACKED = OFF_AL + LANES

PROMPT_TILE = 512
GLA_BLOCK = 256
STAGE_SKEW = 0
SAMPLE_SEQS = 32
STATE_CHUNK = 4
STATE_OUT_SLOTS = 2
ROW_TILE = 256
EXPERT_RING = 4
SC_WINDOW = 128
COMBINE_TILE = 512
RANK_TILE = 1024
RANK_BLOCK = 512
VMEM_LIMIT = 56 * 1024 * 1024


def _rmsnorm(x, g):
    ms = jnp.mean(x * x, axis=-1, keepdims=True)
    return (x * lax.rsqrt(ms + EPS)) * g


def _split_bf16(x):
    hi = x.astype(BF16)
    lo = (x - hi.astype(F32)).astype(BF16)
    return hi, lo


def _pack_bf16_pairs(x):
    n = x.shape[1] // 2
    lo = lax.bitcast_convert_type(x[:, :n].astype(BF16).astype(F32), jnp.uint32)
    hi = lax.bitcast_convert_type(x[:, n:].astype(BF16).astype(F32), jnp.uint32)
    return lax.shift_right_logical(lo, jnp.uint32(16)) | hi


def _unpack_bf16_pairs(w):
    lo = lax.bitcast_convert_type(lax.shift_left(w, jnp.uint32(16)), F32)
    hi = lax.bitcast_convert_type(w & jnp.uint32(0xFFFF0000), F32)
    return lo, hi


def _dot(a, b):
    return jnp.dot(a, b, preferred_element_type=F32)


def _chunk_causal(n, chunk):
    r = lax.broadcasted_iota(jnp.int32, (n, n), 0)
    c = lax.broadcasted_iota(jnp.int32, (n, n), 1)
    shift = chunk.bit_length() - 1
    assert chunk == 1 << shift
    same = lax.shift_right_arithmetic(r, shift) == lax.shift_right_arithmetic(c, shift)
    return same & (c <= r)


def _route(xn2, wr_ref, br_ref):
    rows = xn2.shape[0]
    xh, xl = _split_bf16(xn2)
    part = _dot(xh, wr_ref[...]) + _dot(xl, wr_ref[...])
    lg = part[:, :LANES] + part[:, LANES:] + br_ref[...]
    lane_i = lax.broadcasted_iota(jnp.int32, (rows, LANES), 1)
    lane = lane_i.astype(F32)
    neg = -jnp.inf
    is_g = lane_i < N_GROUPS
    gm = jnp.max(jnp.where(is_g, lg, neg), axis=1, keepdims=True)
    gs = jnp.sum(jnp.where(is_g, jnp.exp(lg - gm), 0.0), axis=1, keepdims=True)
    g_top = 1.0 / gs
    g_idx = jnp.min(jnp.where(is_g & (lg == gm), lane, float(LANES)), axis=1, keepdims=True)
    e_group = lax.shift_right_arithmetic(lane_i - N_GROUPS, 3).astype(F32)
    sel = (lane_i >= N_GROUPS) & (lane_i < N_GROUPS + N_EXPERTS) & (e_group == g_idx)
    el = jnp.where(sel, lg, neg)
    e1 = jnp.max(el, axis=1, keepdims=True)
    i1 = jnp.min(jnp.where(sel & (el == e1), lane, float(LANES)), axis=1, keepdims=True)
    sel2 = sel & (lane != i1)
    el2 = jnp.where(sel2, lg, neg)
    e2 = jnp.max(el2, axis=1, keepdims=True)
    i2 = jnp.min(jnp.where(sel2 & (el2 == e2), lane, float(LANES)), axis=1, keepdims=True)
    d = jnp.exp(e2 - e1)
    w1 = g_top / (1.0 + d)
    w2 = (g_top * d) / (1.0 + d)
    id1 = i1 - float(N_GROUPS)
    id2 = i2 - float(N_GROUPS)
    return jnp.where(lane_i == 0, w1, jnp.where(lane_i == 1, w2, jnp.where(lane_i == 2, id1,
                     jnp.where(lane_i == 3, id2, 0.0))))


def _mixer_stages(sub, nsub, x_ref, conv_in_ref, gla_in_ref,
                  nmix_ref, w_ref, wgate_ref, walr_ref, wconv_ref, walpha_ref, balpha_ref, gnorm_ref,
                  wco_ref, wgo_ref, wo_ref, nffn_ref, wr_ref, br_ref,
                  h_ref, xn2_ref, route_ref, route_t_ref, conv_out_ref, gla_out_ref,
                  s_ref, ubuf_ref, *state_ring, prompt):
    if prompt:
        rows, chunk, nseq = GLA_BLOCK, GLA_CHUNK, 1
        r0 = sub * rows
        x = x_ref[0, r0:r0 + rows, :]
    else:
        nseq, chunk = SAMPLE_SEQS, x_ref.shape[1]
        rows = nseq * chunk
        r0 = 0
        x = x_ref[...].reshape(rows, D_MODEL)
        sin_ref, sout_ref, sem_in, sem_out = state_ring
        n_state_chunks = nseq // STATE_CHUNK
        step, n_steps = pl.program_id(0), pl.num_programs(0)

        def chunk_seqs(c, of_step):
            first = of_step * nseq + c * STATE_CHUNK
            return pl.ds(pl.multiple_of(first, STATE_CHUNK), STATE_CHUNK)

        def in_copy(c, of_step):
            return pltpu.make_async_copy(gla_in_ref.at[chunk_seqs(c, of_step)], sin_ref.at[c],
                                         sem_in.at[c])

        def out_copy(c, of_step):
            slot = c % STATE_OUT_SLOTS
            return pltpu.make_async_copy(sout_ref.at[slot], gla_out_ref.at[chunk_seqs(c, of_step)],
                                         sem_out.at[slot])

        @pl.when(step == 0)
        def _():
            for c in range(n_state_chunks):
                in_copy(c, step).start()
    nchunks = rows // chunk
    last_sub = sub == nsub - 1

    xn = _rmsnorm(x, nmix_ref[...]).astype(BF16)
    yield

    def proj(off, n):
        if off == OFF_AL:
            return _dot(xn, walr_ref[...])
        if off >= OFF_GA:
            return _dot(xn, wgate_ref[:, off - OFF_GA:off - OFF_GA + n])
        return _dot(xn, w_ref[:, off:off + n])

    alr = proj(OFF_AL, LANES).astype(BF16)
    lap = _dot(alr, walpha_ref[...])
    la = jax.nn.log_sigmoid(lap + balpha_ref[...]) * (1.0 / GLA_GATE_NORM)
    la_hi, la_lo = _split_bf16(la)
    causal = _chunk_causal(rows, chunk)
    tri = jnp.where(causal, 1.0, 0.0).astype(BF16)
    yield
    cc = proj(OFF_CC, D_CONV)
    ch = proj(OFF_CH, D_CONV)
    cb = proj(OFF_CB, D_CONV)
    q = proj(OFF_Q, D_CONV)
    k = proj(OFF_K, D_CONV)
    yield
    b = _dot(tri, la_hi) + _dot(tri, la_lo)

    u = cc * ch
    wc = wconv_ref[...]
    if prompt:
        ubuf_ref[0, 8 + r0:8 + r0 + rows, :] = u
        z = (wc[0:1] * ubuf_ref[0, 6 + r0:6 + r0 + rows, :]
             + wc[1:2] * ubuf_ref[0, 7 + r0:7 + r0 + rows, :] + wc[2:3] * u)
        if last_sub:
            tail = u[rows - 2:rows, :]
            ubuf_ref[0, 6:8, :] = tail
            conv_out_ref[0] = tail
    else:
        u3 = u.reshape(nseq, chunk, D_CONV)
        ubuf_ref[:, 6:8, :] = conv_in_ref[...]
        ubuf_ref[:, 8:8 + chunk, :] = u3
        z3 = (wc[0:1] * ubuf_ref[:, 6:6 + chunk, :] + wc[1:2] * ubuf_ref[:, 7:7 + chunk, :]
              + wc[2:3] * u3)
        z = z3.reshape(rows, D_CONV)
        conv_out_ref[...] = ubuf_ref[:, 6 + chunk:8 + chunk, :]
    ya = _dot((cb * z).astype(BF16), wco_ref[...])
    v = proj(OFF_V, GLA_HEADS * GLA_DV)
    yield

    b3 = b.reshape(nchunks, chunk, D_CONV)
    bl3 = b3[:, chunk - 1:chunk, :]
    qin = (q * (GLA_DK ** -0.5)) * jnp.exp(b)
    kin = k * jnp.exp(-b)
    kst = (k.reshape(nchunks, chunk, D_CONV) * jnp.exp(bl3 - b3)).reshape(rows, D_CONV)
    pad = (-rows) % LANES
    rows_p = rows + pad

    def transposed(a):
        if pad:
            a = jnp.concatenate([a, jnp.zeros((pad, a.shape[1]), F32)], axis=0)
        return a.T

    b_t = transposed(b)
    kst_t = transposed(kst) if prompt else None
    col_chunk = lax.shift_right_arithmetic(
        lax.broadcasted_iota(jnp.int32, (GLA_DK, rows_p), 1), chunk.bit_length() - 1)
    g = proj(OFF_G, GLA_HEADS * GLA_DV)
    yield
    heads = range(GLA_HEADS)
    vhs = [v[:, hd * GLA_DV:(hd + 1) * GLA_DV].astype(BF16) for hd in heads]
    scs = [lax.dot_general(qin[:, hd * GLA_DK:(hd + 1) * GLA_DK].astype(BF16),
                           kin[:, hd * GLA_DK:(hd + 1) * GLA_DK].astype(BF16),
                           (((1,), (1,)), ((), ())), preferred_element_type=F32) for hd in heads]
    ga = proj(OFF_GA, D_MODEL)
    yield
    upds = []
    for hd in heads if prompt else ():
        kst_h = kst_t[hd * GLA_DK:(hd + 1) * GLA_DK, :]
        stacked = jnp.concatenate(
            [jnp.where(col_chunk == n, kst_h, 0.0).astype(BF16) for n in range(nchunks)], axis=0)
        vh_p = jnp.concatenate([vhs[hd], jnp.zeros((pad, GLA_DV), BF16)], axis=0) if pad else vhs[hd]
        upds.append(_dot(stacked, vh_p))
    o_intras = [_dot(jnp.where(causal, scs[hd], 0.0).astype(BF16), vhs[hd]) for hd in heads]
    gb = proj(OFF_GB, D_MODEL)
    merge_a = jax.nn.sigmoid(ga) * ya
    gate_b = jax.nn.sigmoid(gb)
    yield
    def q_block(n, hd):
        return qin[n * chunk:(n + 1) * chunk, hd * GLA_DK:(hd + 1) * GLA_DK].astype(BF16)

    def decay_col(n, hd):
        last = n * chunk + chunk - 1
        return jnp.exp(b_t[hd * GLA_DK:(hd + 1) * GLA_DK, last:last + 1])

    o_inter = [[None] * nchunks for _ in heads]
    if prompt:
        for hd in heads:
            s_run = s_ref[hd]
            for n in range(nchunks):
                o_inter[hd][n] = _dot(q_block(n, hd), s_run.astype(BF16))
                s_run = s_run * decay_col(n, hd) + upds[hd][n * GLA_DK:(n + 1) * GLA_DK, :]
            s_ref[hd] = s_run
            if last_sub:
                gla_out_ref[0, hd] = s_run
    else:
        for c in range(n_state_chunks):
            in_copy(c, step).wait()
            if c >= STATE_OUT_SLOTS:
                out_copy(c - STATE_OUT_SLOTS, step).wait()
            else:
                @pl.when(step > 0)
                def _():
                    out_copy(c + n_state_chunks - STATE_OUT_SLOTS, step - 1).wait()
            for j in range(STATE_CHUNK):
                n = c * STATE_CHUNK + j
                for hd in heads:
                    s_prev = sin_ref[c, j, hd]
                    o_inter[hd][n] = _dot(q_block(n, hd), s_prev.astype(BF16))
                    ksb = kst[n * chunk:(n + 1) * chunk, hd * GLA_DK:(hd + 1) * GLA_DK].astype(BF16)
                    vb = v[n * chunk:(n + 1) * chunk, hd * GLA_DV:(hd + 1) * GLA_DV].astype(BF16)
                    upd = lax.dot_general(ksb, vb, (((0,), (0,)), ((), ())),
                                          preferred_element_type=F32)
                    sout_ref[c % STATE_OUT_SLOTS, j, hd] = s_prev * decay_col(n, hd) + upd
            out_copy(c, step).start()

        @pl.when(step + 1 < n_steps)
        def _():
            for c in range(n_state_chunks):
                in_copy(c, step + 1).start()

        @pl.when(step + 1 == n_steps)
        def _():
            for c in range(n_state_chunks - STATE_OUT_SLOTS, n_state_chunks):
                out_copy(c, step).wait()
    gated = []
    for hd in heads:
        v0 = hd * GLA_DV
        oh = _rmsnorm(o_intras[hd] + jnp.concatenate(o_inter[hd], axis=0), gnorm_ref[...])
        gh = g[:, v0:v0 + GLA_DV]
        gated.append(oh * (gh * jax.nn.sigmoid(gh)))
    yield
    yb = _dot(jnp.concatenate(gated, axis=1).astype(BF16), wgo_ref[...])
    yield

    m = merge_a + gate_b * yb
    hh = x + _dot(m.astype(BF16), wo_ref[...])
    yield
    xn2 = _rmsnorm(hh, nffn_ref[...])
    yield
    route = _route(xn2, wr_ref, br_ref)
    rpad = (-rows) % LANES
    route_p = jnp.concatenate([route, jnp.zeros((rpad, LANES), F32)], axis=0) if rpad else route
    route_t_ref[0, :, r0:r0 + rows] = route_p.T[:8, :rows]
    if prompt:
        h_ref[0, r0:r0 + rows, :] = hh
        xn2_ref[0, r0:r0 + rows, :] = _pack_bf16_pairs(xn2)
        route_ref[0, r0:r0 + rows, :] = route
    else:
        h_ref[...] = hh.reshape(nseq, chunk, D_MODEL)
        xn2_ref[...] = _pack_bf16_pairs(xn2).reshape(nseq, chunk, D_MODEL // 2)
        route_ref[...] = route.reshape(nseq, chunk, LANES)


def _mixer_kernel(*refs, prompt):
    if prompt:
        s_ref, ubuf_ref = refs[-2:]

        @pl.when(pl.program_id(1) == 0)
        def _():
            s_ref[...] = jnp.zeros_like(s_ref)
            ubuf_ref[...] = jnp.zeros_like(ubuf_ref)
    nsub = PROMPT_TILE // GLA_BLOCK if prompt else 1
    stages = [_mixer_stages(sub, nsub, *refs, prompt=prompt) for sub in range(nsub)]
    done = [False] * nsub
    step = 0
    while not all(done):
        for sub, st in enumerate(stages):
            if step >= sub * STAGE_SKEW and not done[sub]:
                done[sub] = next(st, "end") == "end"
        step += 1


def _mixer_weights(norm_mix, w_in, w_conv, w_alpha_up, b_alpha, gla_norm, w_conv_out, w_gla_out, w_o,
                   norm_ffn, w_group_router, b_group_router, w_expert_router, b_expert_router):
    wi = w_in[0]
    w_main = wi[:, :OFF_GA].astype(BF16)
    w_gates = wi[:, OFF_GA + GLA_RANK:].astype(BF16)
    w_alr = jnp.concatenate([wi[:, OFF_GA:OFF_GA + GLA_RANK],
                             jnp.zeros((D_MODEL, LANES - GLA_RANK), F32)], axis=1).astype(BF16)
    walpha = jnp.concatenate(
        [w_alpha_up[0], jnp.zeros((LANES - GLA_RANK, D_CONV), F32)], axis=0).astype(BF16)
    wr = jnp.concatenate(
        [w_group_router[0], w_expert_router[0],
         jnp.zeros((D_MODEL, LANES - N_GROUPS - N_EXPERTS), F32)], axis=1)
    wr_hi = wr.astype(BF16)
    wr_split = jnp.concatenate([wr_hi, (wr - wr_hi.astype(F32)).astype(BF16)], axis=1)
    br = jnp.concatenate([b_group_router[0], b_expert_router[0],
                          jnp.zeros((LANES - N_GROUPS - N_EXPERTS,), F32)])[None, :]
    return (norm_mix, w_main, w_gates, w_alr, w_conv[0], walpha, b_alpha, gla_norm,
            w_conv_out[0].astype(BF16), w_gla_out[0].astype(BF16), w_o[0].astype(BF16),
            norm_ffn, wr_split, br)


def _const_spec(shape):
    nd = len(shape)
    return pl.BlockSpec(shape, lambda *_: (0,) * nd, pipeline_mode=pl.Buffered(1))


def _mixer_call(x, conv_state, gla_state, weights, *, prompt):
    nb, seq, _ = x.shape
    if prompt:
        grid = (nb, seq // PROMPT_TILE)
        tok = lambda last: pl.BlockSpec((1, PROMPT_TILE, last), lambda b, i: (b, i, 0))
        conv_spec = pl.BlockSpec((1, CONV_WIDTH - 1, D_CONV), lambda b, i: (b, 0, 0))
        gla_spec = pl.BlockSpec((1, GLA_HEADS, GLA_DK, GLA_DV), lambda b, i: (b, 0, 0, 0))
        gla_in_spec = pl.BlockSpec((1, 1, 8, LANES), lambda b, i: (b, 0, 0, 0))
        scratch = [pltpu.VMEM((GLA_HEADS, GLA_DK, GLA_DV), F32),
                   pltpu.VMEM((1, PROMPT_TILE + 8, D_CONV), F32)]
        route_t_shape = (nb, 8, seq)
        route_t_spec = pl.BlockSpec((1, 8, PROMPT_TILE), lambda b, i: (b, 0, i))
    else:
        grid = (nb // SAMPLE_SEQS, 1)
        tok = lambda last: pl.BlockSpec((SAMPLE_SEQS, seq, last), lambda b, i: (b, 0, 0))
        conv_spec = pl.BlockSpec((SAMPLE_SEQS, CONV_WIDTH - 1, D_CONV), lambda b, i: (b, 0, 0))
        gla_spec = pl.BlockSpec(memory_space=pl.ANY)
        gla_in_spec = gla_spec
        state_chunk = (STATE_CHUNK, GLA_HEADS, GLA_DK, GLA_DV)
        scratch = [pltpu.VMEM((1, 8, LANES), F32),
                   pltpu.VMEM((SAMPLE_SEQS, 8 + seq, D_CONV), F32),
                   pltpu.VMEM((SAMPLE_SEQS // STATE_CHUNK,) + state_chunk, F32),
                   pltpu.VMEM((STATE_OUT_SLOTS,) + state_chunk, F32),
                   pltpu.SemaphoreType.DMA((SAMPLE_SEQS // STATE_CHUNK,)),
                   pltpu.SemaphoreType.DMA((STATE_OUT_SLOTS,))]
        route_t_shape = (nb // SAMPLE_SEQS, 8, SAMPLE_SEQS * seq)
        route_t_spec = pl.BlockSpec((1, 8, SAMPLE_SEQS * seq), lambda b, i: (b, 0, 0))
    out_shape = (jax.ShapeDtypeStruct((nb, seq, D_MODEL), F32),
                 jax.ShapeDtypeStruct((nb, seq, D_MODEL // 2), jnp.uint32),
                 jax.ShapeDtypeStruct((nb, seq, LANES), F32),
                 jax.ShapeDtypeStruct(route_t_shape, F32),
                 jax.ShapeDtypeStruct((nb, CONV_WIDTH - 1, D_CONV), F32),
                 jax.ShapeDtypeStruct((nb, GLA_HEADS, GLA_DK, GLA_DV), F32))
    return pl.pallas_call(
        functools.partial(_mixer_kernel, prompt=prompt),
        grid=grid,
        in_specs=[tok(D_MODEL), conv_spec, gla_in_spec] + [_const_spec(w.shape) for w in weights],
        out_specs=(tok(D_MODEL), tok(D_MODEL // 2), tok(LANES), route_t_spec, conv_spec, gla_spec),
        out_shape=out_shape,
        scratch_shapes=scratch,
        compiler_params=pltpu.CompilerParams(
            dimension_semantics=("arbitrary", "arbitrary"), vmem_limit_bytes=VMEM_LIMIT),
        name="mixer_prompt" if prompt else "mixer_sample",
    )(x, conv_state, gla_state, *weights)


def _rank_kernel(eid_ref, rank_ref, cnt_ref, carry_ref):
    @pl.when(pl.program_id(0) == 0)
    def _():
        carry_ref[...] = jnp.zeros_like(carry_ref)

    n = RANK_BLOCK
    e_iota = lax.broadcasted_iota(jnp.int32, (N_EXPERTS, n), 0)
    r = lax.broadcasted_iota(jnp.int32, (n, n), 0)
    c = lax.broadcasted_iota(jnp.int32, (n, n), 1)
    before = jnp.where(r < c, 1.0, 0.0).astype(BF16)
    carry = carry_ref[...]
    for j in range(eid_ref.shape[1] // n):
        cols = slice(j * n, (j + 1) * n)
        oh0 = jnp.where(e_iota == eid_ref[0:1, cols], 1.0, 0.0)
        oh1 = jnp.where(e_iota == eid_ref[1:2, cols], 1.0, 0.0)
        cnt = oh0 + oh1
        base = carry + _dot(cnt.astype(BF16), before)
        rank0 = jnp.sum(oh0 * base, axis=0, keepdims=True)
        rank1 = jnp.sum(oh1 * base, axis=0, keepdims=True)
        rank_ref[:, cols] = jnp.concatenate([rank0, rank1], axis=0).astype(jnp.int32)
        carry = carry + jnp.sum(cnt, axis=1, keepdims=True)
    carry_ref[...] = carry
    cnt_ref[...] = jnp.broadcast_to(carry, cnt_ref.shape)


def _rank_call(eid):
    ntok = eid.shape[1]
    return pl.pallas_call(
        _rank_kernel,
        grid=(ntok // RANK_TILE,),
        in_specs=[pl.BlockSpec((2, RANK_TILE), lambda i: (0, i))],
        out_specs=(pl.BlockSpec((2, RANK_TILE), lambda i: (0, i)),
                   pl.BlockSpec((N_EXPERTS, LANES), lambda i: (0, 0))),
        out_shape=(jax.ShapeDtypeStruct((2, ntok), jnp.int32),
                   jax.ShapeDtypeStruct((N_EXPERTS, LANES), F32)),
        scratch_shapes=[pltpu.VMEM((N_EXPERTS, 1), F32)],
        compiler_params=pltpu.CompilerParams(dimension_semantics=("arbitrary",)),
        name="expert_rank",
    )(eid)


def _sc_mesh():
    return plsc.VectorSubcoreMesh(core_axis_name="c", subcore_axis_name="s")


def _sc_worker_windows(n_windows, body):
    info = plsc.get_sparse_core_info()
    n_workers = info.num_cores * info.num_subcores
    wid = lax.axis_index("s") * info.num_cores + lax.axis_index("c")

    @pl.loop(0, pl.cdiv(n_windows, n_workers))
    def _(j):
        win = wid + n_workers * j

        @pl.when(win < n_windows)
        def _():
            body(win)


def _dispatch_rows(sources, pos, n_rows):
    width, dtype = sources[0].shape[1], sources[0].dtype
    n_win = [src.shape[0] // SC_WINDOW for src in sources]
    first_win = [sum(n_win[:i]) for i in range(len(sources))]
    total_win = sum(n_win)
    pos3 = pos.reshape(2, total_win, SC_WINDOW).transpose(1, 0, 2)

    @functools.partial(pl.kernel, mesh=_sc_mesh(), name="moe_dispatch",
                       out_type=jax.ShapeDtypeStruct((n_rows, width), dtype),
                       scratch_types=[pltpu.VMEM((2, SC_WINDOW), jnp.int32),
                                      pltpu.VMEM((SC_WINDOW, width), dtype)])
    def run(*refs):
        src_hbm, (pos_hbm, out_hbm, idx_v, buf) = refs[:len(sources)], refs[len(sources):]

        def window(win):
            pltpu.sync_copy(pos_hbm.at[win], idx_v)
            for src, w0, nw in zip(src_hbm, first_win, n_win):
                @pl.when((win >= w0) & (win < w0 + nw))
                def _():
                    r0 = pl.multiple_of((win - w0) * SC_WINDOW, SC_WINDOW)
                    pltpu.sync_copy(src.at[pl.ds(r0, SC_WINDOW)], buf)

            pltpu.sync_copy(buf, out_hbm.at[idx_v.at[0]])
            pltpu.sync_copy(buf, out_hbm.at[idx_v.at[1]])

        _sc_worker_windows(total_win, window)

    return run(*sources, pos3)


def _return_rows(ys, idx):
    n_out, width = idx.shape[0], ys.shape[1]
    n_windows = n_out // SC_WINDOW

    @functools.partial(pl.kernel, mesh=_sc_mesh(), name="moe_return",
                       out_type=jax.ShapeDtypeStruct((n_out, width), ys.dtype),
                       scratch_types=[pltpu.VMEM((SC_WINDOW,), jnp.int32),
                                      pltpu.VMEM((SC_WINDOW, width), ys.dtype)])
    def run(ys_hbm, idx_hbm, out_hbm, idx_v, buf):
        def window(win):
            pltpu.sync_copy(idx_hbm.at[win], idx_v)
            pltpu.sync_copy(ys_hbm.at[idx_v], buf)
            r0 = pl.multiple_of(win * SC_WINDOW, SC_WINDOW)
            pltpu.sync_copy(buf, out_hbm.at[pl.ds(r0, SC_WINDOW)])

        _sc_worker_windows(n_windows, window)

    return run(ys, idx.reshape(n_windows, SC_WINDOW))


def _expert_kernel(first_ref, ntile_ref, total_ref, xs_hbm, wg_ref, wu_ref, wd_ref, out_hbm,
                   xbuf, obuf, wg_s, wu_s, wd_s, sem_in, sem_out):
    e = pl.program_id(0)
    nt = ntile_ref[e]
    g0 = first_ref[e]
    total = total_ref[0]
    ring = EXPERT_RING

    def rows_of(g):
        return pl.ds(pl.multiple_of(g * ROW_TILE, ROW_TILE), ROW_TILE)

    def in_copy(g):
        slot = lax.rem(g, ring)
        return pltpu.make_async_copy(xs_hbm.at[rows_of(g)], xbuf.at[slot], sem_in.at[slot])

    def out_copy(g):
        slot = lax.rem(g, ring)
        return pltpu.make_async_copy(obuf.at[slot], out_hbm.at[rows_of(g)], sem_out.at[slot])

    @pl.when(e == 0)
    def _():
        for g in range(ring - 1):
            @pl.when(g < total)
            def _():
                in_copy(g).start()

    @pl.when(nt > 0)
    def _():
        wg_s[...] = wg_ref[0].astype(BF16)
        wu_s[...] = wu_ref[0].astype(BF16)
        wd_s[...] = wd_ref[0].astype(BF16)

    def tile(g, carry):
        slot = lax.rem(g, ring)
        in_copy(g).wait()

        @pl.when(g + (ring - 1) < total)
        def _():
            in_copy(g + (ring - 1)).start()

        @pl.when(g >= ring)
        def _():
            out_copy(g - ring).wait()

        lo, hi = _unpack_bf16_pairs(xbuf[slot])
        half = D_MODEL // 2
        sub = ROW_TILE // 2
        parts = []
        for r in (0, sub):
            xl, xh = lo[r:r + sub].astype(BF16), hi[r:r + sub].astype(BF16)
            gt = _dot(xl, wg_s[:half, :]) + _dot(xh, wg_s[half:, :])
            up = _dot(xl, wu_s[:half, :]) + _dot(xh, wu_s[half:, :])
            parts.append((gt, up))
        for r, (gt, up) in zip((0, sub), parts):
            act = (gt * jax.nn.sigmoid(gt)) * up
            obuf[slot, r:r + sub, :] = _pack_bf16_pairs(_dot(act.astype(BF16), wd_s[...]))
        out_copy(g).start()
        return carry

    lax.fori_loop(g0, g0 + nt, tile, 0)

    @pl.when(e == pl.num_programs(0) - 1)
    def _():
        for k in range(1, ring + 1):
            @pl.when(total >= k)
            def _():
                out_copy(total - k).wait()


def _expert_call(first_tile, n_tiles_per_expert, total_tiles, xs, w_gate, w_up, w_down):
    nrows = xs.shape[0]
    by_expert = lambda e, *_: (e, 0, 0)
    half_row = D_MODEL // 2
    return pl.pallas_call(
        _expert_kernel,
        grid_spec=pltpu.PrefetchScalarGridSpec(
            num_scalar_prefetch=3,
            grid=(N_EXPERTS,),
            in_specs=[pl.BlockSpec(memory_space=pl.ANY),
                      pl.BlockSpec((1, D_MODEL, D_EXPERT), by_expert),
                      pl.BlockSpec((1, D_MODEL, D_EXPERT), by_expert),
                      pl.BlockSpec((1, D_EXPERT, D_MODEL), by_expert)],
            out_specs=pl.BlockSpec(memory_space=pl.ANY),
            scratch_shapes=[pltpu.VMEM((EXPERT_RING, ROW_TILE, half_row), jnp.uint32),
                            pltpu.VMEM((EXPERT_RING, ROW_TILE, half_row), jnp.uint32),
                            pltpu.VMEM((D_MODEL, D_EXPERT), BF16),
                            pltpu.VMEM((D_MODEL, D_EXPERT), BF16),
                            pltpu.VMEM((D_EXPERT, D_MODEL), BF16),
                            pltpu.SemaphoreType.DMA((EXPERT_RING,)),
                            pltpu.SemaphoreType.DMA((EXPERT_RING,))]),
        out_shape=jax.ShapeDtypeStruct((nrows, half_row), jnp.uint32),
        compiler_params=pltpu.CompilerParams(
            dimension_semantics=("arbitrary",), vmem_limit_bytes=VMEM_LIMIT),
        name="expert_mlp",
    )(first_tile, n_tiles_per_expert, total_tiles, xs, w_gate, w_up, w_down)


def _combine_kernel(h_ref, y2_ref, route_ref, nfin_ref, out_ref):
    route = route_ref[...]
    w1, w2 = route[:, 0:1], route[:, 1:2]
    lo1, hi1 = _unpack_bf16_pairs(y2_ref[0])
    lo2, hi2 = _unpack_bf16_pairs(y2_ref[1])
    y = h_ref[...] + jnp.concatenate([w1 * lo1 + w2 * lo2, w1 * hi1 + w2 * hi2], axis=1)
    out_ref[...] = _rmsnorm(y, nfin_ref[...])


def _combine_call(h, y2, route, nfin, first_token):
    n = h.shape[0]
    first_block = first_token // COMBINE_TILE
    return pl.pallas_call(
        _combine_kernel,
        grid=(n // COMBINE_TILE,),
        in_specs=[pl.BlockSpec((COMBINE_TILE, D_MODEL), lambda i: (i, 0)),
                  pl.BlockSpec((2, COMBINE_TILE, D_MODEL // 2), lambda i: (0, first_block + i, 0)),
                  pl.BlockSpec((COMBINE_TILE, LANES), lambda i: (i, 0)),
                  pl.BlockSpec((1, D_MODEL), lambda i: (0, 0))],
        out_specs=pl.BlockSpec((COMBINE_TILE, D_MODEL), lambda i: (i, 0)),
        out_shape=jax.ShapeDtypeStruct((n, D_MODEL), F32),
        compiler_params=pltpu.CompilerParams(
            dimension_semantics=("arbitrary",), vmem_limit_bytes=VMEM_LIMIT),
        name="moe_combine",
    )(h, y2, route, nfin)


def _moe_rows(parts, w_gate, w_up, w_down):
    eid = jnp.concatenate([ids for _, ids in parts], axis=1).astype(jnp.int32)
    ntok = eid.shape[1]
    assert ntok % RANK_TILE == 0 and ntok % SC_WINDOW == 0
    rank, cnt = _rank_call(eid)
    counts = cnt[:, 0].astype(jnp.int32)
    padded = ((counts + ROW_TILE - 1) // ROW_TILE) * ROW_TILE
    ends = jnp.cumsum(padded)
    starts = ends - padded
    expert_iota = jnp.arange(N_EXPERTS, dtype=jnp.int32)[:, None, None]
    pos = rank + jnp.sum(jnp.where(eid[None] == expert_iota, starts[:, None, None], 0), axis=0)
    n_rows = (2 * ntok + N_EXPERTS * (ROW_TILE - 1)) // ROW_TILE * ROW_TILE
    xs = _dispatch_rows([rows for rows, _ in parts], pos, n_rows)
    ys = _expert_call(starts // ROW_TILE, padded // ROW_TILE, ends[-1:] // ROW_TILE, xs,
                      w_gate, w_up, w_down)
    return _return_rows(ys, pos.reshape(-1)).reshape(2, ntok, D_MODEL // 2)


def kernel(x_prompt, x_sample, state_conv, state_gla, norm_mix, w_in, w_conv, w_alpha_up, b_alpha,
           gla_norm, w_conv_out, w_gla_out, w_o, norm_ffn, w_group_router, b_group_router,
           w_expert_router, b_expert_router, w_gate, w_up, w_down, norm_final):
    nbp, seq_p, _ = x_prompt.shape
    nbs, seq_s, _ = x_sample.shape
    assert norm_mix.shape[0] == 1, "single layer"
    assert seq_p % PROMPT_TILE == 0 and nbs % SAMPLE_SEQS == 0 and seq_s == 8

    weights = _mixer_weights(norm_mix, w_in, w_conv, w_alpha_up, b_alpha, gla_norm, w_conv_out,
                             w_gla_out, w_o, norm_ffn, w_group_router, b_group_router,
                             w_expert_router, b_expert_router)

    ntp, nts = nbp * seq_p, nbs * seq_s
    assert ntp % COMBINE_TILE == 0 and nts % COMBINE_TILE == 0
    conv0 = jnp.zeros((nbp, CONV_WIDTH - 1, D_CONV), F32)
    gla0 = jnp.zeros((nbp, 1, 8, LANES), F32)
    flat = lambda a: a.reshape(-1, a.shape[-1])
    ids_of = lambda route_t: route_t[:, 2:4, :].transpose(1, 0, 2).reshape(2, -1)
    h_p, xn_p, rt_p, rtt_p, conv_p, gla_p = _mixer_call(x_prompt, conv0, gla0, weights, prompt=True)
    h_s, xn_s, rt_s, rtt_s, conv_s, gla_s = _mixer_call(x_sample, state_conv[0], state_gla[0],
                                                        weights, prompt=False)
    y2 = _moe_rows([(flat(xn_p), ids_of(rtt_p)), (flat(xn_s), ids_of(rtt_s))],
                   w_gate[0], w_up[0], w_down[0])
    nfin = norm_final[None, :]
    y_p = _combine_call(flat(h_p), y2, flat(rt_p), nfin, 0)
    y_s = _combine_call(flat(h_s), y2, flat(rt_s), nfin, ntp)
    return (y_p.reshape(nbp, seq_p, D_MODEL), y_s.reshape(nbs, seq_s, D_MODEL),
            conv_p[None], gla_p[None], conv_s[None], gla_s[None])
```

```python
import functools

import jax
import jax.numpy as jnp
from jax import lax
from jax.experimental import pallas as pl
from jax.experimental.pallas import tpu as pltpu
from jax.experimental.pallas import tpu_sc as plsc

F32 = jnp.float32
BF16 = jnp.bfloat16

D_MODEL = 1024
D_CONV = 512
CONV_WIDTH = 3
GLA_HEADS = 4
GLA_DK = 128
GLA_DV = 256
GLA_RANK = 16
GLA_GATE_NORM = 16.0
GLA_CHUNK = 32
N_GROUPS = 4
EXPERTS_PER_GROUP = 8
N_EXPERTS = 32
D_EXPERT = 256
EPS = 1e-6

LANES = 128
V7X_VMEM_BYTES = 64 * 1024 * 1024
OFF_CB, OFF_CC, OFF_CH = 0, 512, 1024
OFF_Q, OFF_K, OFF_V, OFF_G = 1536, 2048, 2560, 3584
OFF_GA, OFF_GB, OFF_AL = 4608, 5632, 6656

PROMPT_TILE = 512
GLA_BLOCK = 256
STAGE_SKEW = 0
SAMPLE_SEQS = 32
STATE_CHUNK = 4
STATE_OUT_SLOTS = 2
ROW_TILE = 256
EXPERT_RING = 4
SC_WINDOW = 128
COMBINE_TILE = 512
RANK_TILE = 1024
RANK_BLOCK = 512
MIXER_VMEM_LIMIT = V7X_VMEM_BYTES - 8 * 1024 * 1024
STREAM_VMEM_LIMIT = V7X_VMEM_BYTES // 4


def _rmsnorm(x, g):
    ms = jnp.mean(x * x, axis=-1, keepdims=True)
    return (x * lax.rsqrt(ms + EPS)) * g


def _split_bf16(x):
    hi = x.astype(BF16)
    lo = (x - hi.astype(F32)).astype(BF16)
    return hi, lo


def _pack_bf16_pairs(x):
    n = x.shape[1] // 2
    lo = lax.bitcast_convert_type(x[:, :n].astype(BF16).astype(F32), jnp.uint32)
    hi = lax.bitcast_convert_type(x[:, n:].astype(BF16).astype(F32), jnp.uint32)
    return lax.shift_right_logical(lo, jnp.uint32(16)) | hi


def _unpack_bf16_pairs(w):
    lo = lax.bitcast_convert_type(lax.shift_left(w, jnp.uint32(16)), F32)
    hi = lax.bitcast_convert_type(w & jnp.uint32(0xFFFF0000), F32)
    return lo, hi


def _dot(a, b):
    return jnp.dot(a, b, preferred_element_type=F32)


def _chunk_causal(n, chunk):
    r = lax.broadcasted_iota(jnp.int32, (n, n), 0)
    c = lax.broadcasted_iota(jnp.int32, (n, n), 1)
    shift = chunk.bit_length() - 1
    assert chunk == 1 << shift
    same = lax.shift_right_arithmetic(r, shift) == lax.shift_right_arithmetic(c, shift)
    return same & (c <= r)


def _route(xn2, wr_ref, br_ref):
    rows = xn2.shape[0]
    xh, xl = _split_bf16(xn2)
    part = _dot(xh, wr_ref[...]) + _dot(xl, wr_ref[...])
    lg = part[:, :LANES] + part[:, LANES:] + br_ref[...]
    lane_i = lax.broadcasted_iota(jnp.int32, (rows, LANES), 1)
    lane = lane_i.astype(F32)
    neg = -jnp.inf
    is_g = lane_i < N_GROUPS
    gm = jnp.max(jnp.where(is_g, lg, neg), axis=1, keepdims=True)
    gs = jnp.sum(jnp.where(is_g, jnp.exp(lg - gm), 0.0), axis=1, keepdims=True)
    g_top = 1.0 / gs
    g_idx = jnp.min(jnp.where(is_g & (lg == gm), lane, float(LANES)), axis=1, keepdims=True)
    e_group = lax.shift_right_arithmetic(lane_i - N_GROUPS, 3).astype(F32)
    sel = (lane_i >= N_GROUPS) & (lane_i < N_GROUPS + N_EXPERTS) & (e_group == g_idx)
    el = jnp.where(sel, lg, neg)
    e1 = jnp.max(el, axis=1, keepdims=True)
    i1 = jnp.min(jnp.where(sel & (el == e1), lane, float(LANES)), axis=1, keepdims=True)
    sel2 = sel & (lane != i1)
    el2 = jnp.where(sel2, lg, neg)
    e2 = jnp.max(el2, axis=1, keepdims=True)
    i2 = jnp.min(jnp.where(sel2 & (el2 == e2), lane, float(LANES)), axis=1, keepdims=True)
    d = jnp.exp(e2 - e1)
    w1 = g_top / (1.0 + d)
    w2 = (g_top * d) / (1.0 + d)
    id1 = i1 - float(N_GROUPS)
    id2 = i2 - float(N_GROUPS)
    return jnp.where(lane_i == 0, w1, jnp.where(lane_i == 1, w2, jnp.where(lane_i == 2, id1,
                     jnp.where(lane_i == 3, id2, 0.0))))


def _mixer_stages(sub, nsub, x_ref, conv_in_ref, gla_in_ref,
                  nmix_ref, w_ref, wgate_ref, walr_ref, wconv_ref, walpha_ref, balpha_ref, gnorm_ref,
                  wco_ref, wgo_ref, wo_ref, nffn_ref, wr_ref, br_ref,
                  h_ref, xn2_ref, route_ref, route_t_ref, conv_out_ref, gla_out_ref,
                  s_ref, ubuf_ref, *state_ring, prompt):
    if prompt:
        rows, chunk, nseq = GLA_BLOCK, GLA_CHUNK, 1
        r0 = sub * rows
        x = x_ref[0, r0:r0 + rows, :]
    else:
        nseq, chunk = SAMPLE_SEQS, x_ref.shape[1]
        rows = nseq * chunk
        r0 = 0
        x = x_ref[...].reshape(rows, D_MODEL)
        sin_ref, sout_ref, sem_in, sem_out = state_ring
        n_state_chunks = nseq // STATE_CHUNK
        step, n_steps = pl.program_id(0), pl.num_programs(0)

        def chunk_seqs(c, of_step):
            first = of_step * nseq + c * STATE_CHUNK
            return pl.ds(pl.multiple_of(first, STATE_CHUNK), STATE_CHUNK)

        def in_copy(c, of_step):
            return pltpu.make_async_copy(gla_in_ref.at[chunk_seqs(c, of_step)], sin_ref.at[c],
                                         sem_in.at[c])

        def out_copy(c, of_step):
            slot = c % STATE_OUT_SLOTS
            return pltpu.make_async_copy(sout_ref.at[slot], gla_out_ref.at[chunk_seqs(c, of_step)],
                                         sem_out.at[slot])

        @pl.when(step == 0)
        def _():
            for c in range(n_state_chunks):
                in_copy(c, step).start()
    nchunks = rows // chunk
    last_sub = sub == nsub - 1

    xn = _rmsnorm(x, nmix_ref[...]).astype(BF16)
    yield

    def proj(off, n):
        if off == OFF_AL:
            return _dot(xn, walr_ref[...])
        if off >= OFF_GA:
            return _dot(xn, wgate_ref[:, off - OFF_GA:off - OFF_GA + n])
        return _dot(xn, w_ref[:, off:off + n])

    alr = proj(OFF_AL, LANES).astype(BF16)
    lap = _dot(alr, walpha_ref[...])
    la = jax.nn.log_sigmoid(lap + balpha_ref[...]) * (1.0 / GLA_GATE_NORM)
    la_hi, la_lo = _split_bf16(la)
    causal = _chunk_causal(rows, chunk)
    tri = jnp.where(causal, 1.0, 0.0).astype(BF16)
    yield
    cc = proj(OFF_CC, D_CONV)
    ch = proj(OFF_CH, D_CONV)
    cb = proj(OFF_CB, D_CONV)
    q = proj(OFF_Q, D_CONV)
    k = proj(OFF_K, D_CONV)
    yield
    b = _dot(tri, la_hi) + _dot(tri, la_lo)

    u = cc * ch
    wc = wconv_ref[...]
    if prompt:
        ubuf_ref[0, 8 + r0:8 + r0 + rows, :] = u
        z = (wc[0:1] * ubuf_ref[0, 6 + r0:6 + r0 + rows, :]
             + wc[1:2] * ubuf_ref[0, 7 + r0:7 + r0 + rows, :] + wc[2:3] * u)
        if last_sub:
            tail = u[rows - 2:rows, :]
            ubuf_ref[0, 6:8, :] = tail
            conv_out_ref[0] = tail
    else:
        u3 = u.reshape(nseq, chunk, D_CONV)
        ubuf_ref[:, 6:8, :] = conv_in_ref[...]
        ubuf_ref[:, 8:8 + chunk, :] = u3
        z3 = (wc[0:1] * ubuf_ref[:, 6:6 + chunk, :] + wc[1:2] * ubuf_ref[:, 7:7 + chunk, :]
              + wc[2:3] * u3)
        z = z3.reshape(rows, D_CONV)
        conv_out_ref[...] = ubuf_ref[:, 6 + chunk:8 + chunk, :]
    ya = _dot((cb * z).astype(BF16), wco_ref[...])
    v = proj(OFF_V, GLA_HEADS * GLA_DV)
    yield

    b3 = b.reshape(nchunks, chunk, D_CONV)
    bl3 = b3[:, chunk - 1:chunk, :]
    qin = (q * (GLA_DK ** -0.5)) * jnp.exp(b)
    kin = k * jnp.exp(-b)
    kst = (k.reshape(nchunks, chunk, D_CONV) * jnp.exp(bl3 - b3)).reshape(rows, D_CONV)
    pad = (-rows) % LANES
    rows_p = rows + pad

    def transposed(a):
        if pad:
            a = jnp.concatenate([a, jnp.zeros((pad, a.shape[1]), F32)], axis=0)
        return a.T

    b_t = transposed(b)
    kst_t = transposed(kst) if prompt else None
    col_chunk = lax.shift_right_arithmetic(
        lax.broadcasted_iota(jnp.int32, (GLA_DK, rows_p), 1), chunk.bit_length() - 1)
    g = proj(OFF_G, GLA_HEADS * GLA_DV)
    yield
    heads = range(GLA_HEADS)
    vhs = [v[:, hd * GLA_DV:(hd + 1) * GLA_DV].astype(BF16) for hd in heads]
    scs = [lax.dot_general(qin[:, hd * GLA_DK:(hd + 1) * GLA_DK].astype(BF16),
                           kin[:, hd * GLA_DK:(hd + 1) * GLA_DK].astype(BF16),
                           (((1,), (1,)), ((), ())), preferred_element_type=F32) for hd in heads]
    ga = proj(OFF_GA, D_MODEL)
    yield
    upds = []
    for hd in heads if prompt else ():
        kst_h = kst_t[hd * GLA_DK:(hd + 1) * GLA_DK, :]
        stacked = jnp.concatenate(
            [jnp.where(col_chunk == n, kst_h, 0.0).astype(BF16) for n in range(nchunks)], axis=0)
        vh_p = jnp.concatenate([vhs[hd], jnp.zeros((pad, GLA_DV), BF16)], axis=0) if pad else vhs[hd]
        upds.append(_dot(stacked, vh_p))
    o_intras = [_dot(jnp.where(causal, scs[hd], 0.0).astype(BF16), vhs[hd]) for hd in heads]
    gb = proj(OFF_GB, D_MODEL)
    merge_a = jax.nn.sigmoid(ga) * ya
    gate_b = jax.nn.sigmoid(gb)
    yield
    def q_block(n, hd):
        return qin[n * chunk:(n + 1) * chunk, hd * GLA_DK:(hd + 1) * GLA_DK].astype(BF16)

    def decay_col(n, hd):
        last = n * chunk + chunk - 1
        return jnp.exp(b_t[hd * GLA_DK:(hd + 1) * GLA_DK, last:last + 1])

    o_inter = [[None] * nchunks for _ in heads]
    if prompt:
        for hd in heads:
            s_run = s_ref[hd]
            for n in range(nchunks):
                o_inter[hd][n] = _dot(q_block(n, hd), s_run.astype(BF16))
                s_run = s_run * decay_col(n, hd) + upds[hd][n * GLA_DK:(n + 1) * GLA_DK, :]
            s_ref[hd] = s_run
            if last_sub:
                gla_out_ref[0, hd] = s_run
    else:
        for c in range(n_state_chunks):
            in_copy(c, step).wait()
            if c >= STATE_OUT_SLOTS:
                out_copy(c - STATE_OUT_SLOTS, step).wait()
            else:
                @pl.when(step > 0)
                def _():
                    out_copy(c + n_state_chunks - STATE_OUT_SLOTS, step - 1).wait()
            for j in range(STATE_CHUNK):
                n = c * STATE_CHUNK + j
                for hd in heads:
                    s_prev = sin_ref[c, j, hd]
                    o_inter[hd][n] = _dot(q_block(n, hd), s_prev.astype(BF16))
                    ksb = kst[n * chunk:(n + 1) * chunk, hd * GLA_DK:(hd + 1) * GLA_DK].astype(BF16)
                    vb = v[n * chunk:(n + 1) * chunk, hd * GLA_DV:(hd + 1) * GLA_DV].astype(BF16)
                    upd = lax.dot_general(ksb, vb, (((0,), (0,)), ((), ())),
                                          preferred_element_type=F32)
                    sout_ref[c % STATE_OUT_SLOTS, j, hd] = s_prev * decay_col(n, hd) + upd
            out_copy(c, step).start()

        @pl.when(step + 1 < n_steps)
        def _():
            for c in range(n_state_chunks):
                in_copy(c, step + 1).start()

        @pl.when(step + 1 == n_steps)
        def _():
            for c in range(n_state_chunks - STATE_OUT_SLOTS, n_state_chunks):
                out_copy(c, step).wait()
    gated = []
    for hd in heads:
        v0 = hd * GLA_DV
        oh = _rmsnorm(o_intras[hd] + jnp.concatenate(o_inter[hd], axis=0), gnorm_ref[...])
        gh = g[:, v0:v0 + GLA_DV]
        gated.append(oh * (gh * jax.nn.sigmoid(gh)))
    yield
    yb = _dot(jnp.concatenate(gated, axis=1).astype(BF16), wgo_ref[...])
    yield

    m = merge_a + gate_b * yb
    hh = x + _dot(m.astype(BF16), wo_ref[...])
    yield
    xn2 = _rmsnorm(hh, nffn_ref[...])
    yield
    route = _route(xn2, wr_ref, br_ref)
    rpad = (-rows) % LANES
    route_p = jnp.concatenate([route, jnp.zeros((rpad, LANES), F32)], axis=0) if rpad else route
    route_t_ref[0, :, r0:r0 + rows] = route_p.T[:8, :rows]
    if prompt:
        h_ref[0, r0:r0 + rows, :] = hh
        xn2_ref[0, r0:r0 + rows, :] = _pack_bf16_pairs(xn2)
        route_ref[0, r0:r0 + rows, :] = route
    else:
        h_ref[...] = hh.reshape(nseq, chunk, D_MODEL)
        xn2_ref[...] = _pack_bf16_pairs(xn2).reshape(nseq, chunk, D_MODEL // 2)
        route_ref[...] = route.reshape(nseq, chunk, LANES)


def _mixer_kernel(*refs, prompt):
    if prompt:
        s_ref, ubuf_ref = refs[-2:]

        @pl.when(pl.program_id(1) == 0)
        def _():
            s_ref[...] = jnp.zeros_like(s_ref)
            ubuf_ref[...] = jnp.zeros_like(ubuf_ref)
    nsub = PROMPT_TILE // GLA_BLOCK if prompt else 1
    stages = [_mixer_stages(sub, nsub, *refs, prompt=prompt) for sub in range(nsub)]
    done = [False] * nsub
    step = 0
    while not all(done):
        for sub, st in enumerate(stages):
            if step >= sub * STAGE_SKEW and not done[sub]:
                done[sub] = next(st, "end") == "end"
        step += 1


def _mixer_weights(norm_mix, w_in, w_conv, w_alpha_up, b_alpha, gla_norm, w_conv_out, w_gla_out, w_o,
                   norm_ffn, w_group_router, b_group_router, w_expert_router, b_expert_router):
    wi = w_in[0]
    w_main = wi[:, :OFF_GA].astype(BF16)
    w_gates = wi[:, OFF_GA + GLA_RANK:].astype(BF16)
    w_alr = jnp.concatenate([wi[:, OFF_GA:OFF_GA + GLA_RANK],
                             jnp.zeros((D_MODEL, LANES - GLA_RANK), F32)], axis=1).astype(BF16)
    walpha = jnp.concatenate(
        [w_alpha_up[0], jnp.zeros((LANES - GLA_RANK, D_CONV), F32)], axis=0).astype(BF16)
    wr = jnp.concatenate(
        [w_group_router[0], w_expert_router[0],
         jnp.zeros((D_MODEL, LANES - N_GROUPS - N_EXPERTS), F32)], axis=1)
    wr_hi = wr.astype(BF16)
    wr_split = jnp.concatenate([wr_hi, (wr - wr_hi.astype(F32)).astype(BF16)], axis=1)
    br = jnp.concatenate([b_group_router[0], b_expert_router[0],
                          jnp.zeros((LANES - N_GROUPS - N_EXPERTS,), F32)])[None, :]
    return (norm_mix, w_main, w_gates, w_alr, w_conv[0], walpha, b_alpha, gla_norm,
            w_conv_out[0].astype(BF16), w_gla_out[0].astype(BF16), w_o[0].astype(BF16),
            norm_ffn, wr_split, br)


def _const_spec(shape):
    nd = len(shape)
    return pl.BlockSpec(shape, lambda *_: (0,) * nd, pipeline_mode=pl.Buffered(1))


def _mixer_call(x, conv_state, gla_state, weights, *, prompt):
    nb, seq, _ = x.shape
    if prompt:
        grid = (nb, seq // PROMPT_TILE)
        tok = lambda last: pl.BlockSpec((1, PROMPT_TILE, last), lambda b, i: (b, i, 0))
        conv_spec = pl.BlockSpec((1, CONV_WIDTH - 1, D_CONV), lambda b, i: (b, 0, 0))
        gla_spec = pl.BlockSpec((1, GLA_HEADS, GLA_DK, GLA_DV), lambda b, i: (b, 0, 0, 0))
        gla_in_spec = pl.BlockSpec((1, 1, 8, LANES), lambda b, i: (b, 0, 0, 0))
        scratch = [pltpu.VMEM((GLA_HEADS, GLA_DK, GLA_DV), F32),
                   pltpu.VMEM((1, PROMPT_TILE + 8, D_CONV), F32)]
        route_t_shape = (nb, 8, seq)
        route_t_spec = pl.BlockSpec((1, 8, PROMPT_TILE), lambda b, i: (b, 0, i))
    else:
        grid = (nb // SAMPLE_SEQS, 1)
        tok = lambda last: pl.BlockSpec((SAMPLE_SEQS, seq, last), lambda b, i: (b, 0, 0))
        conv_spec = pl.BlockSpec((SAMPLE_SEQS, CONV_WIDTH - 1, D_CONV), lambda b, i: (b, 0, 0))
        gla_spec = pl.BlockSpec(memory_space=pl.ANY)
        gla_in_spec = gla_spec
        state_chunk = (STATE_CHUNK, GLA_HEADS, GLA_DK, GLA_DV)
        scratch = [pltpu.VMEM((1, 8, LANES), F32),
                   pltpu.VMEM((SAMPLE_SEQS, 8 + seq, D_CONV), F32),
                   pltpu.VMEM((SAMPLE_SEQS // STATE_CHUNK,) + state_chunk, F32),
                   pltpu.VMEM((STATE_OUT_SLOTS,) + state_chunk, F32),
                   pltpu.SemaphoreType.DMA((SAMPLE_SEQS // STATE_CHUNK,)),
                   pltpu.SemaphoreType.DMA((STATE_OUT_SLOTS,))]
        route_t_shape = (nb // SAMPLE_SEQS, 8, SAMPLE_SEQS * seq)
        route_t_spec = pl.BlockSpec((1, 8, SAMPLE_SEQS * seq), lambda b, i: (b, 0, 0))
    out_shape = (jax.ShapeDtypeStruct((nb, seq, D_MODEL), F32),
                 jax.ShapeDtypeStruct((nb, seq, D_MODEL // 2), jnp.uint32),
                 jax.ShapeDtypeStruct((nb, seq, LANES), F32),
                 jax.ShapeDtypeStruct(route_t_shape, F32),
                 jax.ShapeDtypeStruct((nb, CONV_WIDTH - 1, D_CONV), F32),
                 jax.ShapeDtypeStruct((nb, GLA_HEADS, GLA_DK, GLA_DV), F32))
    return pl.pallas_call(
        functools.partial(_mixer_kernel, prompt=prompt),
        grid=grid,
        in_specs=[tok(D_MODEL), conv_spec, gla_in_spec] + [_const_spec(w.shape) for w in weights],
        out_specs=(tok(D_MODEL), tok(D_MODEL // 2), tok(LANES), route_t_spec, conv_spec, gla_spec),
        out_shape=out_shape,
        scratch_shapes=scratch,
        compiler_params=pltpu.CompilerParams(
            dimension_semantics=("arbitrary", "arbitrary"), vmem_limit_bytes=MIXER_VMEM_LIMIT),
        name="mixer_prompt" if prompt else "mixer_sample",
    )(x, conv_state, gla_state, *weights)


def _rank_kernel(eid_ref, rank_ref, cnt_ref, carry_ref):
    @pl.when(pl.program_id(0) == 0)
    def _():
        carry_ref[...] = jnp.zeros_like(carry_ref)

    n = RANK_BLOCK
    e_iota = lax.broadcasted_iota(jnp.int32, (N_EXPERTS, n), 0)
    r = lax.broadcasted_iota(jnp.int32, (n, n), 0)
    c = lax.broadcasted_iota(jnp.int32, (n, n), 1)
    before = jnp.where(r < c, 1.0, 0.0).astype(BF16)
    carry = carry_ref[...]
    for j in range(eid_ref.shape[1] // n):
        cols = slice(j * n, (j + 1) * n)
        oh0 = jnp.where(e_iota == eid_ref[0:1, cols], 1.0, 0.0)
        oh1 = jnp.where(e_iota == eid_ref[1:2, cols], 1.0, 0.0)
        cnt = oh0 + oh1
        base = carry + _dot(cnt.astype(BF16), before)
        rank0 = jnp.sum(oh0 * base, axis=0, keepdims=True)
        rank1 = jnp.sum(oh1 * base, axis=0, keepdims=True)
        rank_ref[:, cols] = jnp.concatenate([rank0, rank1], axis=0).astype(jnp.int32)
        carry = carry + jnp.sum(cnt, axis=1, keepdims=True)
    carry_ref[...] = carry
    cnt_ref[...] = jnp.broadcast_to(carry, cnt_ref.shape)


def _rank_call(eid):
    ntok = eid.shape[1]
    return pl.pallas_call(
        _rank_kernel,
        grid=(ntok // RANK_TILE,),
        in_specs=[pl.BlockSpec((2, RANK_TILE), lambda i: (0, i))],
        out_specs=(pl.BlockSpec((2, RANK_TILE), lambda i: (0, i)),
                   pl.BlockSpec((N_EXPERTS, LANES), lambda i: (0, 0))),
        out_shape=(jax.ShapeDtypeStruct((2, ntok), jnp.int32),
                   jax.ShapeDtypeStruct((N_EXPERTS, LANES), F32)),
        scratch_shapes=[pltpu.VMEM((N_EXPERTS, 1), F32)],
        compiler_params=pltpu.CompilerParams(dimension_semantics=("arbitrary",)),
        name="expert_rank",
    )(eid)


def _sc_mesh():
    return plsc.VectorSubcoreMesh(core_axis_name="c", subcore_axis_name="s")


def _sc_worker_windows(n_windows, body):
    info = plsc.get_sparse_core_info()
    n_workers = info.num_cores * info.num_subcores
    wid = lax.axis_index("s") * info.num_cores + lax.axis_index("c")

    @pl.loop(0, pl.cdiv(n_windows, n_workers))
    def _(j):
        win = wid + n_workers * j

        @pl.when(win < n_windows)
        def _():
            body(win)


def _dispatch_rows(sources, pos, pad_rows, n_rows):
    width, dtype = sources[0].shape[1], sources[0].dtype
    n_win = [src.shape[0] // SC_WINDOW for src in sources]
    first_win = [sum(n_win[:i]) for i in range(len(sources))]
    total_win = sum(n_win)
    pos3 = pos.reshape(2, total_win, SC_WINDOW).transpose(1, 0, 2)
    pad2 = pad_rows.reshape(-1, SC_WINDOW)
    zeros = jnp.zeros((SC_WINDOW, width), dtype)

    @functools.partial(pl.kernel, mesh=_sc_mesh(), name="moe_dispatch",
                       out_type=jax.ShapeDtypeStruct((n_rows, width), dtype),
                       scratch_types=[pltpu.VMEM((2, SC_WINDOW), jnp.int32),
                                      pltpu.VMEM((SC_WINDOW, width), dtype)])
    def run(*refs):
        src_hbm = refs[:len(sources)]
        pos_hbm, pad_hbm, zeros_hbm, out_hbm, idx_v, buf = refs[len(sources):]

        def window(win):
            pltpu.sync_copy(pos_hbm.at[win], idx_v)
            for src, w0, nw in zip(src_hbm, first_win, n_win):
                @pl.when((win >= w0) & (win < w0 + nw))
                def _():
                    r0 = pl.multiple_of((win - w0) * SC_WINDOW, SC_WINDOW)
                    pltpu.sync_copy(src.at[pl.ds(r0, SC_WINDOW)], buf)

            pltpu.sync_copy(buf, out_hbm.at[idx_v.at[0]])
            pltpu.sync_copy(buf, out_hbm.at[idx_v.at[1]])

        _sc_worker_windows(total_win, window)

        pltpu.sync_copy(zeros_hbm, buf)

        def pad_window(win):
            pltpu.sync_copy(pad_hbm.at[win], idx_v.at[0])
            pltpu.sync_copy(buf, out_hbm.at[idx_v.at[0]])

        _sc_worker_windows(pad2.shape[0], pad_window)

    return run(*sources, pos3, pad2, zeros)


def _return_rows(ys, idx):
    n_out, width = idx.shape[0], ys.shape[1]
    n_windows = n_out // SC_WINDOW

    @functools.partial(pl.kernel, mesh=_sc_mesh(), name="moe_return",
                       out_type=jax.ShapeDtypeStruct((n_out, width), ys.dtype),
                       scratch_types=[pltpu.VMEM((SC_WINDOW,), jnp.int32),
                                      pltpu.VMEM((SC_WINDOW, width), ys.dtype)])
    def run(ys_hbm, idx_hbm, out_hbm, idx_v, buf):
        def window(win):
            pltpu.sync_copy(idx_hbm.at[win], idx_v)
            pltpu.sync_copy(ys_hbm.at[idx_v], buf)
            r0 = pl.multiple_of(win * SC_WINDOW, SC_WINDOW)
            pltpu.sync_copy(buf, out_hbm.at[pl.ds(r0, SC_WINDOW)])

        _sc_worker_windows(n_windows, window)

    return run(ys, idx.reshape(n_windows, SC_WINDOW))


def _expert_kernel(first_ref, ntile_ref, total_ref, xs_hbm, wg_ref, wu_ref, wd_ref, out_hbm,
                   xbuf, obuf, wg_s, wu_s, wd_s, sem_in, sem_out):
    e = pl.program_id(0)
    nt = ntile_ref[e]
    g0 = first_ref[e]
    total = total_ref[0]
    ring = EXPERT_RING

    def rows_of(g):
        return pl.ds(pl.multiple_of(g * ROW_TILE, ROW_TILE), ROW_TILE)

    def in_copy(g):
        slot = lax.rem(g, ring)
        return pltpu.make_async_copy(xs_hbm.at[rows_of(g)], xbuf.at[slot], sem_in.at[slot])

    def out_copy(g):
        slot = lax.rem(g, ring)
        return pltpu.make_async_copy(obuf.at[slot], out_hbm.at[rows_of(g)], sem_out.at[slot])

    @pl.when(e == 0)
    def _():
        for g in range(ring - 1):
            @pl.when(g < total)
            def _():
                in_copy(g).start()

    @pl.when(nt > 0)
    def _():
        wg_s[...] = wg_ref[0].astype(BF16)
        wu_s[...] = wu_ref[0].astype(BF16)
        wd_s[...] = wd_ref[0].astype(BF16)

    def tile(g, carry):
        slot = lax.rem(g, ring)
        in_copy(g).wait()

        @pl.when(g + (ring - 1) < total)
        def _():
            in_copy(g + (ring - 1)).start()

        @pl.when(g >= ring)
        def _():
            out_copy(g - ring).wait()

        lo, hi = _unpack_bf16_pairs(xbuf[slot])
        half = D_MODEL // 2
        sub = ROW_TILE // 2
        parts = []
        for r in (0, sub):
            xl, xh = lo[r:r + sub].astype(BF16), hi[r:r + sub].astype(BF16)
            gt = _dot(xl, wg_s[:half, :]) + _dot(xh, wg_s[half:, :])
            up = _dot(xl, wu_s[:half, :]) + _dot(xh, wu_s[half:, :])
            parts.append((gt, up))
        for r, (gt, up) in zip((0, sub), parts):
            act = (gt * jax.nn.sigmoid(gt)) * up
            obuf[slot, r:r + sub, :] = _pack_bf16_pairs(_dot(act.astype(BF16), wd_s[...]))
        out_copy(g).start()
        return carry

    lax.fori_loop(g0, g0 + nt, tile, 0)

    @pl.when(e == pl.num_programs(0) - 1)
    def _():
        for k in range(1, ring + 1):
            @pl.when(total >= k)
            def _():
                out_copy(total - k).wait()


def _expert_call(first_tile, n_tiles_per_expert, total_tiles, xs, w_gate, w_up, w_down):
    nrows = xs.shape[0]
    by_expert = lambda e, *_: (e, 0, 0)
    half_row = D_MODEL // 2
    return pl.pallas_call(
        _expert_kernel,
        grid_spec=pltpu.PrefetchScalarGridSpec(
            num_scalar_prefetch=3,
            grid=(N_EXPERTS,),
            in_specs=[pl.BlockSpec(memory_space=pl.ANY),
                      pl.BlockSpec((1, D_MODEL, D_EXPERT), by_expert),
                      pl.BlockSpec((1, D_MODEL, D_EXPERT), by_expert),
                      pl.BlockSpec((1, D_EXPERT, D_MODEL), by_expert)],
            out_specs=pl.BlockSpec(memory_space=pl.ANY),
            scratch_shapes=[pltpu.VMEM((EXPERT_RING, ROW_TILE, half_row), jnp.uint32),
                            pltpu.VMEM((EXPERT_RING, ROW_TILE, half_row), jnp.uint32),
                            pltpu.VMEM((D_MODEL, D_EXPERT), BF16),
                            pltpu.VMEM((D_MODEL, D_EXPERT), BF16),
                            pltpu.VMEM((D_EXPERT, D_MODEL), BF16),
                            pltpu.SemaphoreType.DMA((EXPERT_RING,)),
                            pltpu.SemaphoreType.DMA((EXPERT_RING,))]),
        out_shape=jax.ShapeDtypeStruct((nrows, half_row), jnp.uint32),
        compiler_params=pltpu.CompilerParams(
            dimension_semantics=("arbitrary",), vmem_limit_bytes=STREAM_VMEM_LIMIT),
        name="expert_mlp",
    )(first_tile, n_tiles_per_expert, total_tiles, xs, w_gate, w_up, w_down)


def _combine_kernel(h_ref, y2_ref, route_ref, nfin_ref, out_ref):
    route = route_ref[...]
    w1, w2 = route[:, 0:1], route[:, 1:2]
    lo1, hi1 = _unpack_bf16_pairs(y2_ref[0])
    lo2, hi2 = _unpack_bf16_pairs(y2_ref[1])
    y = h_ref[...] + jnp.concatenate([w1 * lo1 + w2 * lo2, w1 * hi1 + w2 * hi2], axis=1)
    out_ref[...] = _rmsnorm(y, nfin_ref[...])


def _combine_call(h, y2, route, nfin, first_token):
    n = h.shape[0]
    first_block = first_token // COMBINE_TILE
    return pl.pallas_call(
        _combine_kernel,
        grid=(n // COMBINE_TILE,),
        in_specs=[pl.BlockSpec((COMBINE_TILE, D_MODEL), lambda i: (i, 0)),
                  pl.BlockSpec((2, COMBINE_TILE, D_MODEL // 2), lambda i: (0, first_block + i, 0)),
                  pl.BlockSpec((COMBINE_TILE, LANES), lambda i: (i, 0)),
                  pl.BlockSpec((1, D_MODEL), lambda i: (0, 0))],
        out_specs=pl.BlockSpec((COMBINE_TILE, D_MODEL), lambda i: (i, 0)),
        out_shape=jax.ShapeDtypeStruct((n, D_MODEL), F32),
        compiler_params=pltpu.CompilerParams(
            dimension_semantics=("arbitrary",), vmem_limit_bytes=STREAM_VMEM_LIMIT),
        name="moe_combine",
    )(h, y2, route, nfin)


def _moe_rows(parts, w_gate, w_up, w_down):
    eid = jnp.concatenate([ids for _, ids in parts], axis=1).astype(jnp.int32)
    ntok = eid.shape[1]
    assert ntok % RANK_TILE == 0 and ntok % SC_WINDOW == 0
    rank, cnt = _rank_call(eid)
    counts = cnt[:, 0].astype(jnp.int32)
    padded = ((counts + ROW_TILE - 1) // ROW_TILE) * ROW_TILE
    ends = jnp.cumsum(padded)
    starts = ends - padded
    expert_iota = jnp.arange(N_EXPERTS, dtype=jnp.int32)[:, None, None]
    pos = rank + jnp.sum(jnp.where(eid[None] == expert_iota, starts[:, None, None], 0), axis=0)
    n_rows = (2 * ntok + N_EXPERTS * (ROW_TILE - 1)) // ROW_TILE * ROW_TILE
    pad_j = jnp.arange(ROW_TILE, dtype=jnp.int32)[None, :]
    pad_rows = jnp.where(pad_j < (padded - counts)[:, None], (starts + counts)[:, None] + pad_j,
                         n_rows)
    xs = _dispatch_rows([rows for rows, _ in parts], pos, pad_rows, n_rows + SC_WINDOW)
    ys = _expert_call(starts // ROW_TILE, padded // ROW_TILE, ends[-1:] // ROW_TILE, xs,
                      w_gate, w_up, w_down)
    return _return_rows(ys, pos.reshape(-1)).reshape(2, ntok, D_MODEL // 2)


def kernel(x_prompt, x_sample, state_conv, state_gla, norm_mix, w_in, w_conv, w_alpha_up, b_alpha,
           gla_norm, w_conv_out, w_gla_out, w_o, norm_ffn, w_group_router, b_group_router,
           w_expert_router, b_expert_router, w_gate, w_up, w_down, norm_final):
    nbp, seq_p, _ = x_prompt.shape
    nbs, seq_s, _ = x_sample.shape
    assert norm_mix.shape[0] == 1, "single layer"
    assert seq_p % PROMPT_TILE == 0 and nbs % SAMPLE_SEQS == 0 and seq_s == 8

    weights = _mixer_weights(norm_mix, w_in, w_conv, w_alpha_up, b_alpha, gla_norm, w_conv_out,
                             w_gla_out, w_o, norm_ffn, w_group_router, b_group_router,
                             w_expert_router, b_expert_router)

    ntp, nts = nbp * seq_p, nbs * seq_s
    assert ntp % COMBINE_TILE == 0 and nts % COMBINE_TILE == 0
    conv0 = jnp.zeros((nbp, CONV_WIDTH - 1, D_CONV), F32)
    gla0 = jnp.zeros((nbp, 1, 8, LANES), F32)
    flat = lambda a: a.reshape(-1, a.shape[-1])
    ids_of = lambda route_t: route_t[:, 2:4, :].transpose(1, 0, 2).reshape(2, -1)
    h_p, xn_p, rt_p, rtt_p, conv_p, gla_p = _mixer_call(x_prompt, conv0, gla0, weights, prompt=True)
    h_s, xn_s, rt_s, rtt_s, conv_s, gla_s = _mixer_call(x_sample, state_conv[0], state_gla[0],
                                                        weights, prompt=False)
    y2 = _moe_rows([(flat(xn_p), ids_of(rtt_p)), (flat(xn_s), ids_of(rtt_s))],
                   w_gate[0], w_up[0], w_down[0])
    nfin = norm_final[None, :]
    y_p = _combine_call(flat(h_p), y2, flat(rt_p), nfin, 0)
    y_s = _combine_call(flat(h_s), y2, flat(rt_s), nfin, ntp)
    return (y_p.reshape(nbp, seq_p, D_MODEL), y_s.reshape(nbs, seq_s, D_MODEL),
            conv_p[None], gla_p[None], conv_s[None], gla_s[None])
```

```python
import functools

import jax
import jax.numpy as jnp
from jax import lax
from jax.experimental import pallas as pl
from jax.experimental.pallas import tpu as pltpu
from jax.experimental.pallas import tpu_sc as plsc

F32 = jnp.float32
BF16 = jnp.bfloat16

D_MODEL = 1024
D_CONV = 512
CONV_WIDTH = 3
GLA_HEADS = 4
GLA_DK = 128
GLA_DV = 256
GLA_RANK = 16
GLA_GATE_NORM = 16.0
GLA_CHUNK = 32
N_GROUPS = 4
EXPERTS_PER_GROUP = 8
N_EXPERTS = 32
D_EXPERT = 256
EPS = 1e-6

LANES = 128
V7X_VMEM_BYTES = 64 * 1024 * 1024
OFF_CB, OFF_CC, OFF_CH = 0, 512, 1024
OFF_Q, OFF_K, OFF_V, OFF_G = 1536, 2048, 2560, 3584
OFF_GA, OFF_GB, OFF_AL = 4608, 5632, 6656

PROMPT_TILE = 512
GLA_BLOCK = 256
STAGE_SKEW = 0
SAMPLE_SEQS = 32
STATE_CHUNK = 4
STATE_OUT_SLOTS = 2
ROW_TILE = 256
EXPERT_RING = 4
SC_WINDOW = 128
COMBINE_TILE = 512
RANK_TILE = 1024
RANK_BLOCK = 512
MIXER_VMEM_LIMIT = V7X_VMEM_BYTES - 8 * 1024 * 1024
STREAM_VMEM_LIMIT = V7X_VMEM_BYTES // 4


def _rmsnorm(x, g):
    ms = jnp.mean(x * x, axis=-1, keepdims=True)
    return (x * lax.rsqrt(ms + EPS)) * g


def _split_bf16(x):
    hi = x.astype(BF16)
    lo = (x - hi.astype(F32)).astype(BF16)
    return hi, lo


def _pack_bf16_pairs(x):
    n = x.shape[1] // 2
    lo = lax.bitcast_convert_type(x[:, :n].astype(BF16).astype(F32), jnp.uint32)
    hi = lax.bitcast_convert_type(x[:, n:].astype(BF16).astype(F32), jnp.uint32)
    return lax.shift_right_logical(lo, jnp.uint32(16)) | hi


def _unpack_bf16_pairs(w):
    lo = lax.bitcast_convert_type(lax.shift_left(w, jnp.uint32(16)), F32)
    hi = lax.bitcast_convert_type(w & jnp.uint32(0xFFFF0000), F32)
    return lo, hi


def _dot(a, b):
    return jnp.dot(a, b, preferred_element_type=F32)


def _chunk_causal(n, chunk):
    r = lax.broadcasted_iota(jnp.int32, (n, n), 0)
    c = lax.broadcasted_iota(jnp.int32, (n, n), 1)
    shift = chunk.bit_length() - 1
    assert chunk == 1 << shift
    same = lax.shift_right_arithmetic(r, shift) == lax.shift_right_arithmetic(c, shift)
    return same & (c <= r)


def _route(xn2, wr_ref, br_ref):
    rows = xn2.shape[0]
    xh, xl = _split_bf16(xn2)
    part = _dot(xh, wr_ref[...]) + _dot(xl, wr_ref[...])
    lg = part[:, :LANES] + part[:, LANES:] + br_ref[...]
    lane_i = lax.broadcasted_iota(jnp.int32, (rows, LANES), 1)
    lane = lane_i.astype(F32)
    neg = -jnp.inf
    is_g = lane_i < N_GROUPS
    gm = jnp.max(jnp.where(is_g, lg, neg), axis=1, keepdims=True)
    gs = jnp.sum(jnp.where(is_g, jnp.exp(lg - gm), 0.0), axis=1, keepdims=True)
    g_top = 1.0 / gs
    g_idx = jnp.min(jnp.where(is_g & (lg == gm), lane, float(LANES)), axis=1, keepdims=True)
    e_group = lax.shift_right_arithmetic(lane_i - N_GROUPS, 3).astype(F32)
    sel = (lane_i >= N_GROUPS) & (lane_i < N_GROUPS + N_EXPERTS) & (e_group == g_idx)
    el = jnp.where(sel, lg, neg)
    e1 = jnp.max(el, axis=1, keepdims=True)
    i1 = jnp.min(jnp.where(sel & (el == e1), lane, float(LANES)), axis=1, keepdims=True)
    sel2 = sel & (lane != i1)
    el2 = jnp.where(sel2, lg, neg)
    e2 = jnp.max(el2, axis=1, keepdims=True)
    i2 = jnp.min(jnp.where(sel2 & (el2 == e2), lane, float(LANES)), axis=1, keepdims=True)
    d = jnp.exp(e2 - e1)
    w1 = g_top / (1.0 + d)
    w2 = (g_top * d) / (1.0 + d)
    id1 = i1 - float(N_GROUPS)
    id2 = i2 - float(N_GROUPS)
    return jnp.where(lane_i == 0, w1, jnp.where(lane_i == 1, w2, jnp.where(lane_i == 2, id1,
                     jnp.where(lane_i == 3, id2, 0.0))))


def _mixer_stages(sub, nsub, x_ref, conv_in_ref, gla_in_ref,
                  nmix_ref, w_ref, wgate_ref, walr_ref, wconv_ref, walpha_ref, balpha_ref, gnorm_ref,
                  wco_ref, wgo_ref, wo_ref, nffn_ref, wr_ref, br_ref,
                  h_ref, xn2_ref, route_ref, route_t_ref, conv_out_ref, gla_out_ref,
                  s_ref, ubuf_ref, *state_ring, prompt):
    if prompt:
        rows, chunk, nseq = GLA_BLOCK, GLA_CHUNK, 1
        r0 = sub * rows
        x = x_ref[0, r0:r0 + rows, :]
    else:
        nseq, chunk = SAMPLE_SEQS, x_ref.shape[1]
        rows = nseq * chunk
        r0 = 0
        x = x_ref[...].reshape(rows, D_MODEL)
        sin_ref, sout_ref, sem_in, sem_out = state_ring
        n_state_chunks = nseq // STATE_CHUNK
        step, n_steps = pl.program_id(0), pl.num_programs(0)

        def chunk_seqs(c, of_step):
            first = of_step * nseq + c * STATE_CHUNK
            return pl.ds(pl.multiple_of(first, STATE_CHUNK), STATE_CHUNK)

        def in_copy(c, of_step):
            return pltpu.make_async_copy(gla_in_ref.at[chunk_seqs(c, of_step)], sin_ref.at[c],
                                         sem_in.at[c])

        def out_copy(c, of_step):
            slot = c % STATE_OUT_SLOTS
            return pltpu.make_async_copy(sout_ref.at[slot], gla_out_ref.at[chunk_seqs(c, of_step)],
                                         sem_out.at[slot])

        @pl.when(step == 0)
        def _():
            for c in range(n_state_chunks):
                in_copy(c, step).start()
    nchunks = rows // chunk
    last_sub = sub == nsub - 1

    xn = _rmsnorm(x, nmix_ref[...]).astype(BF16)
    yield

    def proj(off, n):
        if off == OFF_AL:
            return _dot(xn, walr_ref[...])
        if off >= OFF_GA:
            return _dot(xn, wgate_ref[:, off - OFF_GA:off - OFF_GA + n])
        return _dot(xn, w_ref[:, off:off + n])

    alr = proj(OFF_AL, LANES).astype(BF16)
    lap = _dot(alr, walpha_ref[...])
    la = jax.nn.log_sigmoid(lap + balpha_ref[...]) * (1.0 / GLA_GATE_NORM)
    la_hi, la_lo = _split_bf16(la)
    causal = _chunk_causal(rows, chunk)
    tri = jnp.where(causal, 1.0, 0.0).astype(BF16)
    yield
    cc = proj(OFF_CC, D_CONV)
    ch = proj(OFF_CH, D_CONV)
    cb = proj(OFF_CB, D_CONV)
    q = proj(OFF_Q, D_CONV)
    k = proj(OFF_K, D_CONV)
    yield
    b = _dot(tri, la_hi) + _dot(tri, la_lo)

    u = cc * ch
    wc = wconv_ref[...]
    if prompt:
        ubuf_ref[0, 8 + r0:8 + r0 + rows, :] = u
        z = (wc[0:1] * ubuf_ref[0, 6 + r0:6 + r0 + rows, :]
             + wc[1:2] * ubuf_ref[0, 7 + r0:7 + r0 + rows, :] + wc[2:3] * u)
        if last_sub:
            tail = u[rows - 2:rows, :]
            ubuf_ref[0, 6:8, :] = tail
            conv_out_ref[0] = tail
    else:
        u3 = u.reshape(nseq, chunk, D_CONV)
        ubuf_ref[:, 6:8, :] = conv_in_ref[...]
        ubuf_ref[:, 8:8 + chunk, :] = u3
        z3 = (wc[0:1] * ubuf_ref[:, 6:6 + chunk, :] + wc[1:2] * ubuf_ref[:, 7:7 + chunk, :]
              + wc[2:3] * u3)
        z = z3.reshape(rows, D_CONV)
        conv_out_ref[...] = ubuf_ref[:, 6 + chunk:8 + chunk, :]
    ya = _dot((cb * z).astype(BF16), wco_ref[...])
    v = proj(OFF_V, GLA_HEADS * GLA_DV)
    yield

    b3 = b.reshape(nchunks, chunk, D_CONV)
    bl3 = b3[:, chunk - 1:chunk, :]
    qin = (q * (GLA_DK ** -0.5)) * jnp.exp(b)
    kin = k * jnp.exp(-b)
    kst = (k.reshape(nchunks, chunk, D_CONV) * jnp.exp(bl3 - b3)).reshape(rows, D_CONV)
    pad = (-rows) % LANES
    rows_p = rows + pad

    def transposed(a):
        if pad:
            a = jnp.concatenate([a, jnp.zeros((pad, a.shape[1]), F32)], axis=0)
        return a.T

    b_t = transposed(b)
    kst_t = transposed(kst) if prompt else None
    col_chunk = lax.shift_right_arithmetic(
        lax.broadcasted_iota(jnp.int32, (GLA_DK, rows_p), 1), chunk.bit_length() - 1)
    g = proj(OFF_G, GLA_HEADS * GLA_DV)
    yield
    heads = range(GLA_HEADS)
    vhs = [v[:, hd * GLA_DV:(hd + 1) * GLA_DV].astype(BF16) for hd in heads]
    scs = [lax.dot_general(qin[:, hd * GLA_DK:(hd + 1) * GLA_DK].astype(BF16),
                           kin[:, hd * GLA_DK:(hd + 1) * GLA_DK].astype(BF16),
                           (((1,), (1,)), ((), ())), preferred_element_type=F32) for hd in heads]
    ga = proj(OFF_GA, D_MODEL)
    yield
    upds = []
    for hd in heads if prompt else ():
        kst_h = kst_t[hd * GLA_DK:(hd + 1) * GLA_DK, :]
        stacked = jnp.concatenate(
            [jnp.where(col_chunk == n, kst_h, 0.0).astype(BF16) for n in range(nchunks)], axis=0)
        vh_p = jnp.concatenate([vhs[hd], jnp.zeros((pad, GLA_DV), BF16)], axis=0) if pad else vhs[hd]
        upds.append(_dot(stacked, vh_p))
    o_intras = [_dot(jnp.where(causal, scs[hd], 0.0).astype(BF16), vhs[hd]) for hd in heads]
    gb = proj(OFF_GB, D_MODEL)
    merge_a = jax.nn.sigmoid(ga) * ya
    gate_b = jax.nn.sigmoid(gb)
    yield
    def q_block(n, hd):
        return qin[n * chunk:(n + 1) * chunk, hd * GLA_DK:(hd + 1) * GLA_DK].astype(BF16)

    def decay_col(n, hd):
        last = n * chunk + chunk - 1
        return jnp.exp(b_t[hd * GLA_DK:(hd + 1) * GLA_DK, last:last + 1])

    o_inter = [[None] * nchunks for _ in heads]
    if prompt:
        for hd in heads:
            s_run = s_ref[hd]
            for n in range(nchunks):
                o_inter[hd][n] = _dot(q_block(n, hd), s_run.astype(BF16))
                s_run = s_run * decay_col(n, hd) + upds[hd][n * GLA_DK:(n + 1) * GLA_DK, :]
            s_ref[hd] = s_run
            if last_sub:
                gla_out_ref[0, hd] = s_run
    else:
        for c in range(n_state_chunks):
            in_copy(c, step).wait()
            if c >= STATE_OUT_SLOTS:
                out_copy(c - STATE_OUT_SLOTS, step).wait()
            else:
                @pl.when(step > 0)
                def _():
                    out_copy(c + n_state_chunks - STATE_OUT_SLOTS, step - 1).wait()
            for j in range(STATE_CHUNK):
                n = c * STATE_CHUNK + j
                for hd in heads:
                    s_prev = sin_ref[c, j, hd]
                    o_inter[hd][n] = _dot(q_block(n, hd), s_prev.astype(BF16))
                    ksb = kst[n * chunk:(n + 1) * chunk, hd * GLA_DK:(hd + 1) * GLA_DK].astype(BF16)
                    vb = v[n * chunk:(n + 1) * chunk, hd * GLA_DV:(hd + 1) * GLA_DV].astype(BF16)
                    upd = lax.dot_general(ksb, vb, (((0,), (0,)), ((), ())),
                                          preferred_element_type=F32)
                    sout_ref[c % STATE_OUT_SLOTS, j, hd] = s_prev * decay_col(n, hd) + upd
            out_copy(c, step).start()

        @pl.when(step + 1 < n_steps)
        def _():
            for c in range(n_state_chunks):
                in_copy(c, step + 1).start()

        @pl.when(step + 1 == n_steps)
        def _():
            for c in range(n_state_chunks - STATE_OUT_SLOTS, n_state_chunks):
                out_copy(c, step).wait()
    gated = []
    for hd in heads:
        v0 = hd * GLA_DV
        oh = _rmsnorm(o_intras[hd] + jnp.concatenate(o_inter[hd], axis=0), gnorm_ref[...])
        gh = g[:, v0:v0 + GLA_DV]
        gated.append(oh * (gh * jax.nn.sigmoid(gh)))
    yield
    yb = _dot(jnp.concatenate(gated, axis=1).astype(BF16), wgo_ref[...])
    yield

    m = merge_a + gate_b * yb
    hh = x + _dot(m.astype(BF16), wo_ref[...])
    yield
    xn2 = _rmsnorm(hh, nffn_ref[...])
    yield
    route = _route(xn2, wr_ref, br_ref)
    rpad = (-rows) % LANES
    route_p = jnp.concatenate([route, jnp.zeros((rpad, LANES), F32)], axis=0) if rpad else route
    route_t_ref[0, :, r0:r0 + rows] = route_p.T[:8, :rows]
    if prompt:
        h_ref[0, r0:r0 + rows, :] = hh
        xn2_ref[0, r0:r0 + rows, :] = _pack_bf16_pairs(xn2)
        route_ref[0, r0:r0 + rows, :] = route
    else:
        h_ref[...] = hh.reshape(nseq, chunk, D_MODEL)
        xn2_ref[...] = _pack_bf16_pairs(xn2).reshape(nseq, chunk, D_MODEL // 2)
        route_ref[...] = route.reshape(nseq, chunk, LANES)


def _mixer_kernel(*refs, prompt):
    if prompt:
        s_ref, ubuf_ref = refs[-2:]

        @pl.when(pl.program_id(1) == 0)
        def _():
            s_ref[...] = jnp.zeros_like(s_ref)
            ubuf_ref[...] = jnp.zeros_like(ubuf_ref)
    nsub = PROMPT_TILE // GLA_BLOCK if prompt else 1
    stages = [_mixer_stages(sub, nsub, *refs, prompt=prompt) for sub in range(nsub)]
    done = [False] * nsub
    step = 0
    while not all(done):
        for sub, st in enumerate(stages):
            if step >= sub * STAGE_SKEW and not done[sub]:
                done[sub] = next(st, "end") == "end"
        step += 1


def _mixer_weights(norm_mix, w_in, w_conv, w_alpha_up, b_alpha, gla_norm, w_conv_out, w_gla_out, w_o,
                   norm_ffn, w_group_router, b_group_router, w_expert_router, b_expert_router):
    wi = w_in[0]
    w_main = wi[:, :OFF_GA].astype(BF16)
    w_gates = wi[:, OFF_GA + GLA_RANK:].astype(BF16)
    w_alr = jnp.concatenate([wi[:, OFF_GA:OFF_GA + GLA_RANK],
                             jnp.zeros((D_MODEL, LANES - GLA_RANK), F32)], axis=1).astype(BF16)
    walpha = jnp.concatenate(
        [w_alpha_up[0], jnp.zeros((LANES - GLA_RANK, D_CONV), F32)], axis=0).astype(BF16)
    wr = jnp.concatenate(
        [w_group_router[0], w_expert_router[0],
         jnp.zeros((D_MODEL, LANES - N_GROUPS - N_EXPERTS), F32)], axis=1)
    wr_hi = wr.astype(BF16)
    wr_split = jnp.concatenate([wr_hi, (wr - wr_hi.astype(F32)).astype(BF16)], axis=1)
    br = jnp.concatenate([b_group_router[0], b_expert_router[0],
                          jnp.zeros((LANES - N_GROUPS - N_EXPERTS,), F32)])[None, :]
    return (norm_mix, w_main, w_gates, w_alr, w_conv[0], walpha, b_alpha, gla_norm,
            w_conv_out[0].astype(BF16), w_gla_out[0].astype(BF16), w_o[0].astype(BF16),
            norm_ffn, wr_split, br)


def _const_spec(shape):
    nd = len(shape)
    return pl.BlockSpec(shape, lambda *_: (0,) * nd, pipeline_mode=pl.Buffered(1))


def _mixer_call(x, conv_state, gla_state, weights, *, prompt):
    nb, seq, _ = x.shape
    if prompt:
        grid = (nb, seq // PROMPT_TILE)
        tok = lambda last: pl.BlockSpec((1, PROMPT_TILE, last), lambda b, i: (b, i, 0))
        conv_spec = pl.BlockSpec((1, CONV_WIDTH - 1, D_CONV), lambda b, i: (b, 0, 0))
        gla_spec = pl.BlockSpec((1, GLA_HEADS, GLA_DK, GLA_DV), lambda b, i: (b, 0, 0, 0))
        gla_in_spec = pl.BlockSpec((1, 1, 8, LANES), lambda b, i: (b, 0, 0, 0))
        scratch = [pltpu.VMEM((GLA_HEADS, GLA_DK, GLA_DV), F32),
                   pltpu.VMEM((1, PROMPT_TILE + 8, D_CONV), F32)]
        route_t_shape = (nb, 8, seq)
        route_t_spec = pl.BlockSpec((1, 8, PROMPT_TILE), lambda b, i: (b, 0, i))
    else:
        grid = (nb // SAMPLE_SEQS, 1)
        tok = lambda last: pl.BlockSpec((SAMPLE_SEQS, seq, last), lambda b, i: (b, 0, 0))
        conv_spec = pl.BlockSpec((SAMPLE_SEQS, CONV_WIDTH - 1, D_CONV), lambda b, i: (b, 0, 0))
        gla_spec = pl.BlockSpec(memory_space=pl.ANY)
        gla_in_spec = gla_spec
        state_chunk = (STATE_CHUNK, GLA_HEADS, GLA_DK, GLA_DV)
        scratch = [pltpu.VMEM((1, 8, LANES), F32),
                   pltpu.VMEM((SAMPLE_SEQS, 8 + seq, D_CONV), F32),
                   pltpu.VMEM((SAMPLE_SEQS // STATE_CHUNK,) + state_chunk, F32),
                   pltpu.VMEM((STATE_OUT_SLOTS,) + state_chunk, F32),
                   pltpu.SemaphoreType.DMA((SAMPLE_SEQS // STATE_CHUNK,)),
                   pltpu.SemaphoreType.DMA((STATE_OUT_SLOTS,))]
        route_t_shape = (nb // SAMPLE_SEQS, 8, SAMPLE_SEQS * seq)
        route_t_spec = pl.BlockSpec((1, 8, SAMPLE_SEQS * seq), lambda b, i: (b, 0, 0))
    out_shape = (jax.ShapeDtypeStruct((nb, seq, D_MODEL), F32),
                 jax.ShapeDtypeStruct((nb, seq, D_MODEL // 2), jnp.uint32),
                 jax.ShapeDtypeStruct((nb, seq, LANES), F32),
                 jax.ShapeDtypeStruct(route_t_shape, F32),
                 jax.ShapeDtypeStruct((nb, CONV_WIDTH - 1, D_CONV), F32),
                 jax.ShapeDtypeStruct((nb, GLA_HEADS, GLA_DK, GLA_DV), F32))
    return pl.pallas_call(
        functools.partial(_mixer_kernel, prompt=prompt),
        grid=grid,
        in_specs=[tok(D_MODEL), conv_spec, gla_in_spec] + [_const_spec(w.shape) for w in weights],
        out_specs=(tok(D_MODEL), tok(D_MODEL // 2), tok(LANES), route_t_spec, conv_spec, gla_spec),
        out_shape=out_shape,
        scratch_shapes=scratch,
        compiler_params=pltpu.CompilerParams(
            dimension_semantics=("arbitrary", "arbitrary"), vmem_limit_bytes=MIXER_VMEM_LIMIT),
        name="mixer_prompt" if prompt else "mixer_sample",
    )(x, conv_state, gla_state, *weights)


def _rank_kernel(eid_ref, rank_ref, cnt_ref, carry_ref):
    @pl.when(pl.program_id(0) == 0)
    def _():
        carry_ref[...] = jnp.zeros_like(carry_ref)

    n = RANK_BLOCK
    e_iota = lax.broadcasted_iota(jnp.int32, (N_EXPERTS, n), 0)
    r = lax.broadcasted_iota(jnp.int32, (n, n), 0)
    c = lax.broadcasted_iota(jnp.int32, (n, n), 1)
    before = jnp.where(r < c, 1.0, 0.0).astype(BF16)
    carry = carry_ref[...]
    for j in range(eid_ref.shape[1] // n):
        cols = slice(j * n, (j + 1) * n)
        oh0 = jnp.where(e_iota == eid_ref[0:1, cols], 1.0, 0.0)
        oh1 = jnp.where(e_iota == eid_ref[1:2, cols], 1.0, 0.0)
        cnt = oh0 + oh1
        base = carry + _dot(cnt.astype(BF16), before)
        rank0 = jnp.sum(oh0 * base, axis=0, keepdims=True)
        rank1 = jnp.sum(oh1 * base, axis=0, keepdims=True)
        rank_ref[:, cols] = jnp.concatenate([rank0, rank1], axis=0).astype(jnp.int32)
        carry = carry + jnp.sum(cnt, axis=1, keepdims=True)
    carry_ref[...] = carry
    cnt_ref[...] = jnp.broadcast_to(carry, cnt_ref.shape)


def _rank_call(eid):
    ntok = eid.shape[1]
    return pl.pallas_call(
        _rank_kernel,
        grid=(ntok // RANK_TILE,),
        in_specs=[pl.BlockSpec((2, RANK_TILE), lambda i: (0, i))],
        out_specs=(pl.BlockSpec((2, RANK_TILE), lambda i: (0, i)),
                   pl.BlockSpec((N_EXPERTS, LANES), lambda i: (0, 0))),
        out_shape=(jax.ShapeDtypeStruct((2, ntok), jnp.int32),
                   jax.ShapeDtypeStruct((N_EXPERTS, LANES), F32)),
        scratch_shapes=[pltpu.VMEM((N_EXPERTS, 1), F32)],
        compiler_params=pltpu.CompilerParams(dimension_semantics=("arbitrary",)),
        name="expert_rank",
    )(eid)


def _sc_mesh():
    return plsc.VectorSubcoreMesh(core_axis_name="c", subcore_axis_name="s")


def _sc_worker_windows(n_windows, body):
    info = plsc.get_sparse_core_info()
    n_workers = info.num_cores * info.num_subcores
    wid = lax.axis_index("s") * info.num_cores + lax.axis_index("c")

    @pl.loop(0, pl.cdiv(n_windows, n_workers))
    def _(j):
        win = wid + n_workers * j

        @pl.when(win < n_windows)
        def _():
            body(win)


def _dispatch_rows(sources, pos, pad_rows, n_rows):
    width, dtype = sources[0].shape[1], sources[0].dtype
    n_win = [src.shape[0] // SC_WINDOW for src in sources]
    first_win = [sum(n_win[:i]) for i in range(len(sources))]
    total_win = sum(n_win)
    pos3 = pos.reshape(2, total_win, SC_WINDOW).transpose(1, 0, 2)
    pad2 = pad_rows.reshape(-1, SC_WINDOW)
    zeros = jnp.zeros((SC_WINDOW, width), dtype)

    @functools.partial(pl.kernel, mesh=_sc_mesh(), name="moe_dispatch",
                       out_type=jax.ShapeDtypeStruct((n_rows, width), dtype),
                       scratch_types=[pltpu.VMEM((2, SC_WINDOW), jnp.int32),
                                      pltpu.VMEM((SC_WINDOW, width), dtype)])
    def run(*refs):
        src_hbm = refs[:len(sources)]
        pos_hbm, pad_hbm, zeros_hbm, out_hbm, idx_v, buf = refs[len(sources):]

        def window(win):
            pltpu.sync_copy(pos_hbm.at[win], idx_v)
            for src, w0, nw in zip(src_hbm, first_win, n_win):
                @pl.when((win >= w0) & (win < w0 + nw))
                def _():
                    r0 = pl.multiple_of((win - w0) * SC_WINDOW, SC_WINDOW)
                    pltpu.sync_copy(src.at[pl.ds(r0, SC_WINDOW)], buf)

            pltpu.sync_copy(buf, out_hbm.at[idx_v.at[0]])
            pltpu.sync_copy(buf, out_hbm.at[idx_v.at[1]])

        _sc_worker_windows(total_win, window)

        pltpu.sync_copy(zeros_hbm, buf)

        def pad_window(win):
            pltpu.sync_copy(pad_hbm.at[win], idx_v.at[0])
            pltpu.sync_copy(buf, out_hbm.at[idx_v.at[0]])

        _sc_worker_windows(pad2.shape[0], pad_window)

    return run(*sources, pos3, pad2, zeros)


def _return_rows(ys, idx):
    n_out, width = idx.shape[0], ys.shape[1]
    n_windows = n_out // SC_WINDOW

    @functools.partial(pl.kernel, mesh=_sc_mesh(), name="moe_return",
                       out_type=jax.ShapeDtypeStruct((n_out, width), ys.dtype),
                       scratch_types=[pltpu.VMEM((SC_WINDOW,), jnp.int32),
                                      pltpu.VMEM((SC_WINDOW, width), ys.dtype)])
    def run(ys_hbm, idx_hbm, out_hbm, idx_v, buf):
        def window(win):
            pltpu.sync_copy(idx_hbm.at[win], idx_v)
            pltpu.sync_copy(ys_hbm.at[idx_v], buf)
            r0 = pl.multiple_of(win * SC_WINDOW, SC_WINDOW)
            pltpu.sync_copy(buf, out_hbm.at[pl.ds(r0, SC_WINDOW)])

        _sc_worker_windows(n_windows, window)

    return run(ys, idx.reshape(n_windows, SC_WINDOW))


def _expert_kernel(first_ref, ntile_ref, total_ref, xs_hbm, wg_ref, wu_ref, wd_ref, out_hbm,
                   xbuf, obuf, wg_s, wu_s, wd_s, sem_in, sem_out):
    e = pl.program_id(0)
    nt = ntile_ref[e]
    g0 = first_ref[e]
    total = total_ref[0]
    ring = EXPERT_RING

    def rows_of(g):
        return pl.ds(pl.multiple_of(g * ROW_TILE, ROW_TILE), ROW_TILE)

    def in_copy(g):
        slot = lax.rem(g, ring)
        return pltpu.make_async_copy(xs_hbm.at[rows_of(g)], xbuf.at[slot], sem_in.at[slot])

    def out_copy(g):
        slot = lax.rem(g, ring)
        return pltpu.make_async_copy(obuf.at[slot], out_hbm.at[rows_of(g)], sem_out.at[slot])

    @pl.when(e == 0)
    def _():
        for g in range(ring - 1):
            @pl.when(g < total)
            def _():
                in_copy(g).start()

    @pl.when(nt > 0)
    def _():
        wg_s[...] = wg_ref[0].astype(BF16)
        wu_s[...] = wu_ref[0].astype(BF16)
        wd_s[...] = wd_ref[0].astype(BF16)

    def tile(g, carry):
        slot = lax.rem(g, ring)
        in_copy(g).wait()

        @pl.when(g + (ring - 1) < total)
        def _():
            in_copy(g + (ring - 1)).start()

        @pl.when(g >= ring)
        def _():
            out_copy(g - ring).wait()

        lo, hi = _unpack_bf16_pairs(xbuf[slot])
        half = D_MODEL // 2
        sub = ROW_TILE // 2
        parts = []
        for r in (0, sub):
            xl, xh = lo[r:r + sub].astype(BF16), hi[r:r + sub].astype(BF16)
            gt = _dot(xl, wg_s[:half, :]) + _dot(xh, wg_s[half:, :])
            up = _dot(xl, wu_s[:half, :]) + _dot(xh, wu_s[half:, :])
            parts.append((gt, up))
        for r, (gt, up) in zip((0, sub), parts):
            act = (gt * jax.nn.sigmoid(gt)) * up
            obuf[slot, r:r + sub, :] = _pack_bf16_pairs(_dot(act.astype(BF16), wd_s[...]))
        out_copy(g).start()
        return carry

    lax.fori_loop(g0, g0 + nt, tile, 0)

    @pl.when(e == pl.num_programs(0) - 1)
    def _():
        for k in range(1, ring + 1):
            @pl.when(total >= k)
            def _():
                out_copy(total - k).wait()


def _expert_call(first_tile, n_tiles_per_expert, total_tiles, xs, w_gate, w_up, w_down):
    nrows = xs.shape[0]
    by_expert = lambda e, *_: (e, 0, 0)
    half_row = D_MODEL // 2
    return pl.pallas_call(
        _expert_kernel,
        grid_spec=pltpu.PrefetchScalarGridSpec(
            num_scalar_prefetch=3,
            grid=(N_EXPERTS,),
            in_specs=[pl.BlockSpec(memory_space=pl.ANY),
                      pl.BlockSpec((1, D_MODEL, D_EXPERT), by_expert),
                      pl.BlockSpec((1, D_MODEL, D_EXPERT), by_expert),
                      pl.BlockSpec((1, D_EXPERT, D_MODEL), by_expert)],
            out_specs=pl.BlockSpec(memory_space=pl.ANY),
            scratch_shapes=[pltpu.VMEM((EXPERT_RING, ROW_TILE, half_row), jnp.uint32),
                            pltpu.VMEM((EXPERT_RING, ROW_TILE, half_row), jnp.uint32),
                            pltpu.VMEM((D_MODEL, D_EXPERT), BF16),
                            pltpu.VMEM((D_MODEL, D_EXPERT), BF16),
                            pltpu.VMEM((D_EXPERT, D_MODEL), BF16),
                            pltpu.SemaphoreType.DMA((EXPERT_RING,)),
                            pltpu.SemaphoreType.DMA((EXPERT_RING,))]),
        out_shape=jax.ShapeDtypeStruct((nrows, half_row), jnp.uint32),
        compiler_params=pltpu.CompilerParams(
            dimension_semantics=("arbitrary",), vmem_limit_bytes=STREAM_VMEM_LIMIT),
        name="expert_mlp",
    )(first_tile, n_tiles_per_expert, total_tiles, xs, w_gate, w_up, w_down)


def _combine_kernel(h_ref, y2_ref, route_ref, nfin_ref, out_ref):
    route = route_ref[...]
    w1, w2 = route[:, 0:1], route[:, 1:2]
    lo1, hi1 = _unpack_bf16_pairs(y2_ref[0])
    lo2, hi2 = _unpack_bf16_pairs(y2_ref[1])
    y = h_ref[...] + jnp.concatenate([w1 * lo1 + w2 * lo2, w1 * hi1 + w2 * hi2], axis=1)
    out_ref[...] = _rmsnorm(y, nfin_ref[...])


def _combine_call(h, y2, route, nfin, first_token):
    n = h.shape[0]
    first_block = first_token // COMBINE_TILE
    return pl.pallas_call(
        _combine_kernel,
        grid=(n // COMBINE_TILE,),
        in_specs=[pl.BlockSpec((COMBINE_TILE, D_MODEL), lambda i: (i, 0)),
                  pl.BlockSpec((2, COMBINE_TILE, D_MODEL // 2), lambda i: (0, first_block + i, 0)),
                  pl.BlockSpec((COMBINE_TILE, LANES), lambda i: (i, 0)),
                  pl.BlockSpec((1, D_MODEL), lambda i: (0, 0))],
        out_specs=pl.BlockSpec((COMBINE_TILE, D_MODEL), lambda i: (i, 0)),
        out_shape=jax.ShapeDtypeStruct((n, D_MODEL), F32),
        compiler_params=pltpu.CompilerParams(
            dimension_semantics=("arbitrary",), vmem_limit_bytes=STREAM_VMEM_LIMIT),
        name="moe_combine",
    )(h, y2, route, nfin)


def _moe_rows(parts, w_gate, w_up, w_down):
    eid = jnp.concatenate([ids for _, ids in parts], axis=1).astype(jnp.int32)
    ntok = eid.shape[1]
    assert ntok % RANK_TILE == 0 and ntok % SC_WINDOW == 0
    rank, cnt = _rank_call(eid)
    counts = cnt[:, 0].astype(jnp.int32)
    padded = ((counts + ROW_TILE - 1) // ROW_TILE) * ROW_TILE
    ends = jnp.cumsum(padded)
    starts = ends - padded
    expert_iota = jnp.arange(N_EXPERTS, dtype=jnp.int32)[:, None, None]
    pos = rank + jnp.sum(jnp.where(eid[None] == expert_iota, starts[:, None, None], 0), axis=0)
    n_rows = (2 * ntok + N_EXPERTS * (ROW_TILE - 1)) // ROW_TILE * ROW_TILE
    pad_j = jnp.arange(ROW_TILE, dtype=jnp.int32)[None, :]
    spare = n_rows + jnp.arange(N_EXPERTS, dtype=jnp.int32)[:, None] * ROW_TILE + pad_j
    pad_rows = jnp.where(pad_j < (padded - counts)[:, None], (starts + counts)[:, None] + pad_j,
                         spare)
    xs = _dispatch_rows([rows for rows, _ in parts], pos, pad_rows,
                        n_rows + N_EXPERTS * ROW_TILE)
    ys = _expert_call(starts // ROW_TILE, padded // ROW_TILE, ends[-1:] // ROW_TILE, xs,
                      w_gate, w_up, w_down)
    return _return_rows(ys, pos.reshape(-1)).reshape(2, ntok, D_MODEL // 2)


def kernel(x_prompt, x_sample, state_conv, state_gla, norm_mix, w_in, w_conv, w_alpha_up, b_alpha,
           gla_norm, w_conv_out, w_gla_out, w_o, norm_ffn, w_group_router, b_group_router,
           w_expert_router, b_expert_router, w_gate, w_up, w_down, norm_final):
    nbp, seq_p, _ = x_prompt.shape
    nbs, seq_s, _ = x_sample.shape
    assert norm_mix.shape[0] == 1, "single layer"
    assert seq_p % PROMPT_TILE == 0 and nbs % SAMPLE_SEQS == 0 and seq_s == 8

    weights = _mixer_weights(norm_mix, w_in, w_conv, w_alpha_up, b_alpha, gla_norm, w_conv_out,
                             w_gla_out, w_o, norm_ffn, w_group_router, b_group_router,
                             w_expert_router, b_expert_router)

    ntp, nts = nbp * seq_p, nbs * seq_s
    assert ntp % COMBINE_TILE == 0 and nts % COMBINE_TILE == 0
    conv0 = jnp.zeros((nbp, CONV_WIDTH - 1, D_CONV), F32)
    gla0 = jnp.zeros((nbp, 1, 8, LANES), F32)
    flat = lambda a: a.reshape(-1, a.shape[-1])
    ids_of = lambda route_t: route_t[:, 2:4, :].transpose(1, 0, 2).reshape(2, -1)
    h_p, xn_p, rt_p, rtt_p, conv_p, gla_p = _mixer_call(x_prompt, conv0, gla0, weights, prompt=True)
    h_s, xn_s, rt_s, rtt_s, conv_s, gla_s = _mixer_call(x_sample, state_conv[0], state_gla[0],
                                                        weights, prompt=False)
    y2 = _moe_rows([(flat(xn_p), ids_of(rtt_p)), (flat(xn_s), ids_of(rtt_s))],
                   w_gate[0], w_up[0], w_down[0])
    nfin = norm_final[None, :]
    y_p = _combine_call(flat(h_p), y2, flat(rt_p), nfin, 0)
    y_s = _combine_call(flat(h_s), y2, flat(rt_s), nfin, ntp)
    return (y_p.reshape(nbp, seq_p, D_MODEL), y_s.reshape(nbs, seq_s, D_MODEL),
            conv_p[None], gla_p[None], conv_s[None], gla_s[None])
```

```python
import functools

import jax
import jax.numpy as jnp
from jax import lax
from jax.experimental import pallas as pl
from jax.experimental.pallas import tpu as pltpu
from jax.experimental.pallas import tpu_sc as plsc

F32 = jnp.float32
BF16 = jnp.bfloat16

D_MODEL = 1024
D_CONV = 512
CONV_WIDTH = 3
GLA_HEADS = 4
GLA_DK = 128
GLA_DV = 256
GLA_RANK = 16
GLA_GATE_NORM = 16.0
GLA_CHUNK = 32
N_GROUPS = 4
EXPERTS_PER_GROUP = 8
N_EXPERTS = 32
D_EXPERT = 256
EPS = 1e-6

LANES = 128
V7X_VMEM_BYTES = 64 * 1024 * 1024
OFF_CB, OFF_CC, OFF_CH = 0, 512, 1024
OFF_Q, OFF_K, OFF_V, OFF_G = 1536, 2048, 2560, 3584
OFF_GA, OFF_GB, OFF_AL = 4608, 5632, 6656

PROMPT_TILE = 512
GLA_BLOCK = 256
STAGE_SKEW = 0
SAMPLE_SEQS = 32
STATE_CHUNK = 4
STATE_OUT_SLOTS = 2
ROW_TILE = 256
EXPERT_RING = 4
SC_WINDOW = 128
COMBINE_TILE = 512
RANK_TILE = 1024
RANK_BLOCK = 512
MIXER_VMEM_LIMIT = V7X_VMEM_BYTES - 8 * 1024 * 1024
STREAM_VMEM_LIMIT = V7X_VMEM_BYTES // 4


def _rmsnorm(x, g):
    ms = jnp.mean(x * x, axis=-1, keepdims=True)
    return (x * lax.rsqrt(ms + EPS)) * g


def _split_bf16(x):
    hi = x.astype(BF16)
    lo = (x - hi.astype(F32)).astype(BF16)
    return hi, lo


def _pack_bf16_pairs(x):
    n = x.shape[1] // 2
    lo = lax.bitcast_convert_type(x[:, :n].astype(BF16).astype(F32), jnp.uint32)
    hi = lax.bitcast_convert_type(x[:, n:].astype(BF16).astype(F32), jnp.uint32)
    return lax.shift_right_logical(lo, jnp.uint32(16)) | hi


def _unpack_bf16_pairs(w):
    lo = lax.bitcast_convert_type(lax.shift_left(w, jnp.uint32(16)), F32)
    hi = lax.bitcast_convert_type(w & jnp.uint32(0xFFFF0000), F32)
    return lo, hi


def _dot(a, b):
    return jnp.dot(a, b, preferred_element_type=F32)


def _chunk_causal(n, chunk):
    r = lax.broadcasted_iota(jnp.int32, (n, n), 0)
    c = lax.broadcasted_iota(jnp.int32, (n, n), 1)
    shift = chunk.bit_length() - 1
    assert chunk == 1 << shift
    same = lax.shift_right_arithmetic(r, shift) == lax.shift_right_arithmetic(c, shift)
    return same & (c <= r)


def _route(xn2, wr_ref, br_ref):
    rows = xn2.shape[0]
    xh, xl = _split_bf16(xn2)
    part = _dot(xh, wr_ref[...]) + _dot(xl, wr_ref[...])
    lg = part[:, :LANES] + part[:, LANES:] + br_ref[...]
    lane_i = lax.broadcasted_iota(jnp.int32, (rows, LANES), 1)
    lane = lane_i.astype(F32)
    neg = -jnp.inf
    is_g = lane_i < N_GROUPS
    gm = jnp.max(jnp.where(is_g, lg, neg), axis=1, keepdims=True)
    gs = jnp.sum(jnp.where(is_g, jnp.exp(lg - gm), 0.0), axis=1, keepdims=True)
    g_top = 1.0 / gs
    g_idx = jnp.min(jnp.where(is_g & (lg == gm), lane, float(LANES)), axis=1, keepdims=True)
    e_group = lax.shift_right_arithmetic(lane_i - N_GROUPS, 3).astype(F32)
    sel = (lane_i >= N_GROUPS) & (lane_i < N_GROUPS + N_EXPERTS) & (e_group == g_idx)
    el = jnp.where(sel, lg, neg)
    e1 = jnp.max(el, axis=1, keepdims=True)
    i1 = jnp.min(jnp.where(sel & (el == e1), lane, float(LANES)), axis=1, keepdims=True)
    sel2 = sel & (lane != i1)
    el2 = jnp.where(sel2, lg, neg)
    e2 = jnp.max(el2, axis=1, keepdims=True)
    i2 = jnp.min(jnp.where(sel2 & (el2 == e2), lane, float(LANES)), axis=1, keepdims=True)
    d = jnp.exp(e2 - e1)
    w1 = g_top / (1.0 + d)
    w2 = (g_top * d) / (1.0 + d)
    id1 = i1 - float(N_GROUPS)
    id2 = i2 - float(N_GROUPS)
    return jnp.where(lane_i == 0, w1, jnp.where(lane_i == 1, w2, jnp.where(lane_i == 2, id1,
                     jnp.where(lane_i == 3, id2, 0.0))))


def _mixer_stages(sub, nsub, x_ref, conv_in_ref, gla_in_ref,
                  nmix_ref, w_ref, wgate_ref, walr_ref, wconv_ref, walpha_ref, balpha_ref, gnorm_ref,
                  wco_ref, wgo_ref, wo_ref, nffn_ref, wr_ref, br_ref,
                  h_ref, xn2_ref, route_ref, route_t_ref, conv_out_ref, gla_out_ref,
                  s_ref, ubuf_ref, *state_ring, prompt):
    if prompt:
        rows, chunk, nseq = GLA_BLOCK, GLA_CHUNK, 1
        r0 = sub * rows
        x = x_ref[0, r0:r0 + rows, :]
    else:
        nseq, chunk = SAMPLE_SEQS, x_ref.shape[1]
        rows = nseq * chunk
        r0 = 0
        x = x_ref[...].reshape(rows, D_MODEL)
        sin_ref, sout_ref, sem_in, sem_out = state_ring
        n_state_chunks = nseq // STATE_CHUNK
        step, n_steps = pl.program_id(0), pl.num_programs(0)

        def chunk_seqs(c, of_step):
            first = of_step * nseq + c * STATE_CHUNK
            return pl.ds(pl.multiple_of(first, STATE_CHUNK), STATE_CHUNK)

        def in_copy(c, of_step):
            return pltpu.make_async_copy(gla_in_ref.at[chunk_seqs(c, of_step)], sin_ref.at[c],
                                         sem_in.at[c])

        def out_copy(c, of_step):
            slot = c % STATE_OUT_SLOTS
            return pltpu.make_async_copy(sout_ref.at[slot], gla_out_ref.at[chunk_seqs(c, of_step)],
                                         sem_out.at[slot])

        @pl.when(step == 0)
        def _():
            for c in range(n_state_chunks):
                in_copy(c, step).start()
    nchunks = rows // chunk
    last_sub = sub == nsub - 1

    xn = _rmsnorm(x, nmix_ref[...]).astype(BF16)
    yield

    def proj(off, n):
        if off == OFF_AL:
            return _dot(xn, walr_ref[...])
        if off >= OFF_GA:
            return _dot(xn, wgate_ref[:, off - OFF_GA:off - OFF_GA + n])
        return _dot(xn, w_ref[:, off:off + n])

    alr = proj(OFF_AL, LANES).astype(BF16)
    lap = _dot(alr, walpha_ref[...])
    la = jax.nn.log_sigmoid(lap + balpha_ref[...]) * (1.0 / GLA_GATE_NORM)
    la_hi, la_lo = _split_bf16(la)
    causal = _chunk_causal(rows, chunk)
    tri = jnp.where(causal, 1.0, 0.0).astype(BF16)
    yield
    cc = proj(OFF_CC, D_CONV)
    ch = proj(OFF_CH, D_CONV)
    cb = proj(OFF_CB, D_CONV)
    q = proj(OFF_Q, D_CONV)
    k = proj(OFF_K, D_CONV)
    yield
    b = _dot(tri, la_hi) + _dot(tri, la_lo)

    u = cc * ch
    wc = wconv_ref[...]
    if prompt:
        ubuf_ref[0, 8 + r0:8 + r0 + rows, :] = u
        z = (wc[0:1] * ubuf_ref[0, 6 + r0:6 + r0 + rows, :]
             + wc[1:2] * ubuf_ref[0, 7 + r0:7 + r0 + rows, :] + wc[2:3] * u)
        if last_sub:
            tail = u[rows - 2:rows, :]
            ubuf_ref[0, 6:8, :] = tail
            conv_out_ref[0] = tail
    else:
        u3 = u.reshape(nseq, chunk, D_CONV)
        ubuf_ref[:, 6:8, :] = conv_in_ref[...]
        ubuf_ref[:, 8:8 + chunk, :] = u3
        z3 = (wc[0:1] * ubuf_ref[:, 6:6 + chunk, :] + wc[1:2] * ubuf_ref[:, 7:7 + chunk, :]
              + wc[2:3] * u3)
        z = z3.reshape(rows, D_CONV)
        conv_out_ref[...] = ubuf_ref[:, 6 + chunk:8 + chunk, :]
    ya = _dot((cb * z).astype(BF16), wco_ref[...])
    v = proj(OFF_V, GLA_HEADS * GLA_DV)
    yield

    b3 = b.reshape(nchunks, chunk, D_CONV)
    bl3 = b3[:, chunk - 1:chunk, :]
    qin = (q * (GLA_DK ** -0.5)) * jnp.exp(b)
    kin = k * jnp.exp(-b)
    kst = (k.reshape(nchunks, chunk, D_CONV) * jnp.exp(bl3 - b3)).reshape(rows, D_CONV)
    pad = (-rows) % LANES
    rows_p = rows + pad

    def transposed(a):
        if pad:
            a = jnp.concatenate([a, jnp.zeros((pad, a.shape[1]), F32)], axis=0)
        return a.T

    b_t = transposed(b)
    kst_t = transposed(kst) if prompt else None
    col_chunk = lax.shift_right_arithmetic(
        lax.broadcasted_iota(jnp.int32, (GLA_DK, rows_p), 1), chunk.bit_length() - 1)
    g = proj(OFF_G, GLA_HEADS * GLA_DV)
    yield
    heads = range(GLA_HEADS)
    vhs = [v[:, hd * GLA_DV:(hd + 1) * GLA_DV].astype(BF16) for hd in heads]
    scs = [lax.dot_general(qin[:, hd * GLA_DK:(hd + 1) * GLA_DK].astype(BF16),
                           kin[:, hd * GLA_DK:(hd + 1) * GLA_DK].astype(BF16),
                           (((1,), (1,)), ((), ())), preferred_element_type=F32) for hd in heads]
    ga = proj(OFF_GA, D_MODEL)
    yield
    upds = []
    for hd in heads if prompt else ():
        kst_h = kst_t[hd * GLA_DK:(hd + 1) * GLA_DK, :]
        stacked = jnp.concatenate(
            [jnp.where(col_chunk == n, kst_h, 0.0).astype(BF16) for n in range(nchunks)], axis=0)
        vh_p = jnp.concatenate([vhs[hd], jnp.zeros((pad, GLA_DV), BF16)], axis=0) if pad else vhs[hd]
        upds.append(_dot(stacked, vh_p))
    o_intras = [_dot(jnp.where(causal, scs[hd], 0.0).astype(BF16), vhs[hd]) for hd in heads]
    gb = proj(OFF_GB, D_MODEL)
    merge_a = jax.nn.sigmoid(ga) * ya
    gate_b = jax.nn.sigmoid(gb)
    yield
    def q_block(n, hd):
        return qin[n * chunk:(n + 1) * chunk, hd * GLA_DK:(hd + 1) * GLA_DK].astype(BF16)

    def decay_col(n, hd):
        last = n * chunk + chunk - 1
        return jnp.exp(b_t[hd * GLA_DK:(hd + 1) * GLA_DK, last:last + 1])

    o_inter = [[None] * nchunks for _ in heads]
    if prompt:
        for hd in heads:
            s_run = s_ref[hd]
            for n in range(nchunks):
                o_inter[hd][n] = _dot(q_block(n, hd), s_run.astype(BF16))
                s_run = s_run * decay_col(n, hd) + upds[hd][n * GLA_DK:(n + 1) * GLA_DK, :]
            s_ref[hd] = s_run
            if last_sub:
                gla_out_ref[0, hd] = s_run
    else:
        for c in range(n_state_chunks):
            in_copy(c, step).wait()
            if c >= STATE_OUT_SLOTS:
                out_copy(c - STATE_OUT_SLOTS, step).wait()
            else:
                @pl.when(step > 0)
                def _():
                    out_copy(c + n_state_chunks - STATE_OUT_SLOTS, step - 1).wait()
            for j in range(STATE_CHUNK):
                n = c * STATE_CHUNK + j
                for hd in heads:
                    s_prev = sin_ref[c, j, hd]
                    o_inter[hd][n] = _dot(q_block(n, hd), s_prev.astype(BF16))
                    ksb = kst[n * chunk:(n + 1) * chunk, hd * GLA_DK:(hd + 1) * GLA_DK].astype(BF16)
                    vb = v[n * chunk:(n + 1) * chunk, hd * GLA_DV:(hd + 1) * GLA_DV].astype(BF16)
                    upd = lax.dot_general(ksb, vb, (((0,), (0,)), ((), ())),
                                          preferred_element_type=F32)
                    sout_ref[c % STATE_OUT_SLOTS, j, hd] = s_prev * decay_col(n, hd) + upd
            out_copy(c, step).start()

        @pl.when(step + 1 < n_steps)
        def _():
            for c in range(n_state_chunks):
                in_copy(c, step + 1).start()

        @pl.when(step + 1 == n_steps)
        def _():
            for c in range(n_state_chunks - STATE_OUT_SLOTS, n_state_chunks):
                out_copy(c, step).wait()
    gated = []
    for hd in heads:
        v0 = hd * GLA_DV
        oh = _rmsnorm(o_intras[hd] + jnp.concatenate(o_inter[hd], axis=0), gnorm_ref[...])
        gh = g[:, v0:v0 + GLA_DV]
        gated.append(oh * (gh * jax.nn.sigmoid(gh)))
    yield
    yb = _dot(jnp.concatenate(gated, axis=1).astype(BF16), wgo_ref[...])
    yield

    m = merge_a + gate_b * yb
    hh = x + _dot(m.astype(BF16), wo_ref[...])
    yield
    xn2 = _rmsnorm(hh, nffn_ref[...])
    yield
    route = _route(xn2, wr_ref, br_ref)
    rpad = (-rows) % LANES
    route_p = jnp.concatenate([route, jnp.zeros((rpad, LANES), F32)], axis=0) if rpad else route
    route_t_ref[0, :, r0:r0 + rows] = route_p.T[:8, :rows]
    if prompt:
        h_ref[0, r0:r0 + rows, :] = hh
        xn2_ref[0, r0:r0 + rows, :] = _pack_bf16_pairs(xn2)
        route_ref[0, r0:r0 + rows, :] = route
    else:
        h_ref[...] = hh.reshape(nseq, chunk, D_MODEL)
        xn2_ref[...] = _pack_bf16_pairs(xn2).reshape(nseq, chunk, D_MODEL // 2)
        route_ref[...] = route.reshape(nseq, chunk, LANES)


def _mixer_kernel(*refs, prompt):
    if prompt:
        s_ref, ubuf_ref = refs[-2:]

        @pl.when(pl.program_id(1) == 0)
        def _():
            s_ref[...] = jnp.zeros_like(s_ref)
            ubuf_ref[...] = jnp.zeros_like(ubuf_ref)
    nsub = PROMPT_TILE // GLA_BLOCK if prompt else 1
    stages = [_mixer_stages(sub, nsub, *refs, prompt=prompt) for sub in range(nsub)]
    done = [False] * nsub
    step = 0
    while not all(done):
        for sub, st in enumerate(stages):
            if step >= sub * STAGE_SKEW and not done[sub]:
                done[sub] = next(st, "end") == "end"
        step += 1


def _mixer_weights(norm_mix, w_in, w_conv, w_alpha_up, b_alpha, gla_norm, w_conv_out, w_gla_out, w_o,
                   norm_ffn, w_group_router, b_group_router, w_expert_router, b_expert_router):
    wi = w_in[0]
    w_main = wi[:, :OFF_GA].astype(BF16)
    w_gates = wi[:, OFF_GA + GLA_RANK:].astype(BF16)
    w_alr = jnp.concatenate([wi[:, OFF_GA:OFF_GA + GLA_RANK],
                             jnp.zeros((D_MODEL, LANES - GLA_RANK), F32)], axis=1).astype(BF16)
    walpha = jnp.concatenate(
        [w_alpha_up[0], jnp.zeros((LANES - GLA_RANK, D_CONV), F32)], axis=0).astype(BF16)
    wr = jnp.concatenate(
        [w_group_router[0], w_expert_router[0],
         jnp.zeros((D_MODEL, LANES - N_GROUPS - N_EXPERTS), F32)], axis=1)
    wr_hi = wr.astype(BF16)
    wr_split = jnp.concatenate([wr_hi, (wr - wr_hi.astype(F32)).astype(BF16)], axis=1)
    br = jnp.concatenate([b_group_router[0], b_expert_router[0],
                          jnp.zeros((LANES - N_GROUPS - N_EXPERTS,), F32)])[None, :]
    return (norm_mix, w_main, w_gates, w_alr, w_conv[0], walpha, b_alpha, gla_norm,
            w_conv_out[0].astype(BF16), w_gla_out[0].astype(BF16), w_o[0].astype(BF16),
            norm_ffn, wr_split, br)


def _const_spec(shape):
    nd = len(shape)
    return pl.BlockSpec(shape, lambda *_: (0,) * nd, pipeline_mode=pl.Buffered(1))


def _mixer_call(x, conv_state, gla_state, weights, *, prompt):
    nb, seq, _ = x.shape
    if prompt:
        grid = (nb, seq // PROMPT_TILE)
        tok = lambda last: pl.BlockSpec((1, PROMPT_TILE, last), lambda b, i: (b, i, 0))
        conv_spec = pl.BlockSpec((1, CONV_WIDTH - 1, D_CONV), lambda b, i: (b, 0, 0))
        gla_spec = pl.BlockSpec((1, GLA_HEADS, GLA_DK, GLA_DV), lambda b, i: (b, 0, 0, 0))
        gla_in_spec = pl.BlockSpec((1, 1, 8, LANES), lambda b, i: (b, 0, 0, 0))
        scratch = [pltpu.VMEM((GLA_HEADS, GLA_DK, GLA_DV), F32),
                   pltpu.VMEM((1, PROMPT_TILE + 8, D_CONV), F32)]
        route_t_shape = (nb, 8, seq)
        route_t_spec = pl.BlockSpec((1, 8, PROMPT_TILE), lambda b, i: (b, 0, i))
    else:
        grid = (nb // SAMPLE_SEQS, 1)
        tok = lambda last: pl.BlockSpec((SAMPLE_SEQS, seq, last), lambda b, i: (b, 0, 0))
        conv_spec = pl.BlockSpec((SAMPLE_SEQS, CONV_WIDTH - 1, D_CONV), lambda b, i: (b, 0, 0))
        gla_spec = pl.BlockSpec(memory_space=pl.ANY)
        gla_in_spec = gla_spec
        state_chunk = (STATE_CHUNK, GLA_HEADS, GLA_DK, GLA_DV)
        scratch = [pltpu.VMEM((1, 8, LANES), F32),
                   pltpu.VMEM((SAMPLE_SEQS, 8 + seq, D_CONV), F32),
                   pltpu.VMEM((SAMPLE_SEQS // STATE_CHUNK,) + state_chunk, F32),
                   pltpu.VMEM((STATE_OUT_SLOTS,) + state_chunk, F32),
                   pltpu.SemaphoreType.DMA((SAMPLE_SEQS // STATE_CHUNK,)),
                   pltpu.SemaphoreType.DMA((STATE_OUT_SLOTS,))]
        route_t_shape = (nb // SAMPLE_SEQS, 8, SAMPLE_SEQS * seq)
        route_t_spec = pl.BlockSpec((1, 8, SAMPLE_SEQS * seq), lambda b, i: (b, 0, 0))
    out_shape = (jax.ShapeDtypeStruct((nb, seq, D_MODEL), F32),
                 jax.ShapeDtypeStruct((nb, seq, D_MODEL // 2), jnp.uint32),
                 jax.ShapeDtypeStruct((nb, seq, LANES), F32),
                 jax.ShapeDtypeStruct(route_t_shape, F32),
                 jax.ShapeDtypeStruct((nb, CONV_WIDTH - 1, D_CONV), F32),
                 jax.ShapeDtypeStruct((nb, GLA_HEADS, GLA_DK, GLA_DV), F32))
    return pl.pallas_call(
        functools.partial(_mixer_kernel, prompt=prompt),
        grid=grid,
        in_specs=[tok(D_MODEL), conv_spec, gla_in_spec] + [_const_spec(w.shape) for w in weights],
        out_specs=(tok(D_MODEL), tok(D_MODEL // 2), tok(LANES), route_t_spec, conv_spec, gla_spec),
        out_shape=out_shape,
        scratch_shapes=scratch,
        compiler_params=pltpu.CompilerParams(
            dimension_semantics=("arbitrary", "arbitrary"), vmem_limit_bytes=MIXER_VMEM_LIMIT),
        name="mixer_prompt" if prompt else "mixer_sample",
    )(x, conv_state, gla_state, *weights)


def _rank_kernel(eid_ref, rank_ref, cnt_ref, carry_ref):
    @pl.when(pl.program_id(0) == 0)
    def _():
        carry_ref[...] = jnp.zeros_like(carry_ref)

    n = RANK_BLOCK
    e_iota = lax.broadcasted_iota(jnp.int32, (N_EXPERTS, n), 0)
    r = lax.broadcasted_iota(jnp.int32, (n, n), 0)
    c = lax.broadcasted_iota(jnp.int32, (n, n), 1)
    before = jnp.where(r < c, 1.0, 0.0).astype(BF16)
    carry = carry_ref[...]
    for j in range(eid_ref.shape[1] // n):
        cols = slice(j * n, (j + 1) * n)
        oh0 = jnp.where(e_iota == eid_ref[0:1, cols], 1.0, 0.0)
        oh1 = jnp.where(e_iota == eid_ref[1:2, cols], 1.0, 0.0)
        cnt = oh0 + oh1
        base = carry + _dot(cnt.astype(BF16), before)
        rank0 = jnp.sum(oh0 * base, axis=0, keepdims=True)
        rank1 = jnp.sum(oh1 * base, axis=0, keepdims=True)
        rank_ref[:, cols] = jnp.concatenate([rank0, rank1], axis=0).astype(jnp.int32)
        carry = carry + jnp.sum(cnt, axis=1, keepdims=True)
    carry_ref[...] = carry
    cnt_ref[...] = jnp.broadcast_to(carry, cnt_ref.shape)


def _rank_call(eid):
    ntok = eid.shape[1]
    return pl.pallas_call(
        _rank_kernel,
        grid=(ntok // RANK_TILE,),
        in_specs=[pl.BlockSpec((2, RANK_TILE), lambda i: (0, i))],
        out_specs=(pl.BlockSpec((2, RANK_TILE), lambda i: (0, i)),
                   pl.BlockSpec((N_EXPERTS, LANES), lambda i: (0, 0))),
        out_shape=(jax.ShapeDtypeStruct((2, ntok), jnp.int32),
                   jax.ShapeDtypeStruct((N_EXPERTS, LANES), F32)),
        scratch_shapes=[pltpu.VMEM((N_EXPERTS, 1), F32)],
        compiler_params=pltpu.CompilerParams(dimension_semantics=("arbitrary",)),
        name="expert_rank",
    )(eid)


def _sc_mesh():
    return plsc.VectorSubcoreMesh(core_axis_name="c", subcore_axis_name="s")


def _sc_worker_windows(n_windows, body):
    info = plsc.get_sparse_core_info()
    n_workers = info.num_cores * info.num_subcores
    wid = lax.axis_index("s") * info.num_cores + lax.axis_index("c")

    @pl.loop(0, pl.cdiv(n_windows, n_workers))
    def _(j):
        win = wid + n_workers * j

        @pl.when(win < n_windows)
        def _():
            body(win)


def _dispatch_rows(sources, pos, n_rows):
    width, dtype = sources[0].shape[1], sources[0].dtype
    n_win = [src.shape[0] // SC_WINDOW for src in sources]
    first_win = [sum(n_win[:i]) for i in range(len(sources))]
    total_win = sum(n_win)
    pos3 = pos.reshape(2, total_win, SC_WINDOW).transpose(1, 0, 2)

    @functools.partial(pl.kernel, mesh=_sc_mesh(), name="moe_dispatch",
                       out_type=jax.ShapeDtypeStruct((n_rows, width), dtype),
                       scratch_types=[pltpu.VMEM((2, SC_WINDOW), jnp.int32),
                                      pltpu.VMEM((SC_WINDOW, width), dtype)])
    def run(*refs):
        src_hbm, (pos_hbm, out_hbm, idx_v, buf) = refs[:len(sources)], refs[len(sources):]

        def window(win):
            pltpu.sync_copy(pos_hbm.at[win], idx_v)
            for src, w0, nw in zip(src_hbm, first_win, n_win):
                @pl.when((win >= w0) & (win < w0 + nw))
                def _():
                    r0 = pl.multiple_of((win - w0) * SC_WINDOW, SC_WINDOW)
                    pltpu.sync_copy(src.at[pl.ds(r0, SC_WINDOW)], buf)

            pltpu.sync_copy(buf, out_hbm.at[idx_v.at[0]])
            pltpu.sync_copy(buf, out_hbm.at[idx_v.at[1]])

        _sc_worker_windows(total_win, window)

    return run(*sources, pos3)


def _return_rows(ys, idx):
    n_out, width = idx.shape[0], ys.shape[1]
    n_windows = n_out // SC_WINDOW

    @functools.partial(pl.kernel, mesh=_sc_mesh(), name="moe_return",
                       out_type=jax.ShapeDtypeStruct((n_out, width), ys.dtype),
                       scratch_types=[pltpu.VMEM((SC_WINDOW,), jnp.int32),
                                      pltpu.VMEM((SC_WINDOW, width), ys.dtype)])
    def run(ys_hbm, idx_hbm, out_hbm, idx_v, buf):
        def window(win):
            pltpu.sync_copy(idx_hbm.at[win], idx_v)
            pltpu.sync_copy(ys_hbm.at[idx_v], buf)
            r0 = pl.multiple_of(win * SC_WINDOW, SC_WINDOW)
            pltpu.sync_copy(buf, out_hbm.at[pl.ds(r0, SC_WINDOW)])

        _sc_worker_windows(n_windows, window)

    return run(ys, idx.reshape(n_windows, SC_WINDOW))


def _expert_kernel(first_ref, ntile_ref, nrow_ref, total_ref, xs_hbm, wg_ref, wu_ref, wd_ref, out_hbm,
                   xbuf, obuf, wg_s, wu_s, wd_s, sem_in, sem_out):
    e = pl.program_id(0)
    nt = ntile_ref[e]
    g0 = first_ref[e]
    total = total_ref[0]
    ring = EXPERT_RING

    def rows_of(g):
        return pl.ds(pl.multiple_of(g * ROW_TILE, ROW_TILE), ROW_TILE)

    def in_copy(g):
        slot = lax.rem(g, ring)
        return pltpu.make_async_copy(xs_hbm.at[rows_of(g)], xbuf.at[slot], sem_in.at[slot])

    def out_copy(g):
        slot = lax.rem(g, ring)
        return pltpu.make_async_copy(obuf.at[slot], out_hbm.at[rows_of(g)], sem_out.at[slot])

    @pl.when(e == 0)
    def _():
        for g in range(ring - 1):
            @pl.when(g < total)
            def _():
                in_copy(g).start()

    @pl.when(nt > 0)
    def _():
        wg_s[...] = wg_ref[0].astype(BF16)
        wu_s[...] = wu_ref[0].astype(BF16)
        wd_s[...] = wd_ref[0].astype(BF16)

    def tile(g, carry):
        slot = lax.rem(g, ring)
        in_copy(g).wait()

        @pl.when(g + (ring - 1) < total)
        def _():
            in_copy(g + (ring - 1)).start()

        @pl.when(g >= ring)
        def _():
            out_copy(g - ring).wait()

        n_valid = nrow_ref[e] - (g - g0) * ROW_TILE
        row = lax.broadcasted_iota(jnp.int32, (ROW_TILE, 1), 0)
        lo, hi = _unpack_bf16_pairs(jnp.where(row < n_valid, xbuf[slot], jnp.uint32(0)))
        half = D_MODEL // 2
        sub = ROW_TILE // 2
        parts = []
        for r in (0, sub):
            xl, xh = lo[r:r + sub].astype(BF16), hi[r:r + sub].astype(BF16)
            gt = _dot(xl, wg_s[:half, :]) + _dot(xh, wg_s[half:, :])
            up = _dot(xl, wu_s[:half, :]) + _dot(xh, wu_s[half:, :])
            parts.append((gt, up))
        for r, (gt, up) in zip((0, sub), parts):
            act = (gt * jax.nn.sigmoid(gt)) * up
            obuf[slot, r:r + sub, :] = _pack_bf16_pairs(_dot(act.astype(BF16), wd_s[...]))
        out_copy(g).start()
        return carry

    lax.fori_loop(g0, g0 + nt, tile, 0)

    @pl.when(e == pl.num_programs(0) - 1)
    def _():
        for k in range(1, ring + 1):
            @pl.when(total >= k)
            def _():
                out_copy(total - k).wait()


def _expert_call(first_tile, n_tiles_per_expert, n_rows_per_expert, total_tiles, xs,
                 w_gate, w_up, w_down):
    nrows = xs.shape[0]
    by_expert = lambda e, *_: (e, 0, 0)
    half_row = D_MODEL // 2
    return pl.pallas_call(
        _expert_kernel,
        grid_spec=pltpu.PrefetchScalarGridSpec(
            num_scalar_prefetch=4,
            grid=(N_EXPERTS,),
            in_specs=[pl.BlockSpec(memory_space=pl.ANY),
                      pl.BlockSpec((1, D_MODEL, D_EXPERT), by_expert),
                      pl.BlockSpec((1, D_MODEL, D_EXPERT), by_expert),
                      pl.BlockSpec((1, D_EXPERT, D_MODEL), by_expert)],
            out_specs=pl.BlockSpec(memory_space=pl.ANY),
            scratch_shapes=[pltpu.VMEM((EXPERT_RING, ROW_TILE, half_row), jnp.uint32),
                            pltpu.VMEM((EXPERT_RING, ROW_TILE, half_row), jnp.uint32),
                            pltpu.VMEM((D_MODEL, D_EXPERT), BF16),
                            pltpu.VMEM((D_MODEL, D_EXPERT), BF16),
                            pltpu.VMEM((D_EXPERT, D_MODEL), BF16),
                            pltpu.SemaphoreType.DMA((EXPERT_RING,)),
                            pltpu.SemaphoreType.DMA((EXPERT_RING,))]),
        out_shape=jax.ShapeDtypeStruct((nrows, half_row), jnp.uint32),
        compiler_params=pltpu.CompilerParams(
            dimension_semantics=("arbitrary",), vmem_limit_bytes=STREAM_VMEM_LIMIT),
        name="expert_mlp",
    )(first_tile, n_tiles_per_expert, n_rows_per_expert, total_tiles, xs, w_gate, w_up, w_down)


def _combine_kernel(h_ref, y2_ref, route_ref, nfin_ref, out_ref):
    route = route_ref[...]
    w1, w2 = route[:, 0:1], route[:, 1:2]
    lo1, hi1 = _unpack_bf16_pairs(y2_ref[0])
    lo2, hi2 = _unpack_bf16_pairs(y2_ref[1])
    y = h_ref[...] + jnp.concatenate([w1 * lo1 + w2 * lo2, w1 * hi1 + w2 * hi2], axis=1)
    out_ref[...] = _rmsnorm(y, nfin_ref[...])


def _combine_call(h, y2, route, nfin, first_token):
    n = h.shape[0]
    first_block = first_token // COMBINE_TILE
    return pl.pallas_call(
        _combine_kernel,
        grid=(n // COMBINE_TILE,),
        in_specs=[pl.BlockSpec((COMBINE_TILE, D_MODEL), lambda i: (i, 0)),
                  pl.BlockSpec((2, COMBINE_TILE, D_MODEL // 2), lambda i: (0, first_block + i, 0)),
                  pl.BlockSpec((COMBINE_TILE, LANES), lambda i: (i, 0)),
                  pl.BlockSpec((1, D_MODEL), lambda i: (0, 0))],
        out_specs=pl.BlockSpec((COMBINE_TILE, D_MODEL), lambda i: (i, 0)),
        out_shape=jax.ShapeDtypeStruct((n, D_MODEL), F32),
        compiler_params=pltpu.CompilerParams(
            dimension_semantics=("arbitrary",), vmem_limit_bytes=STREAM_VMEM_LIMIT),
        name="moe_combine",
    )(h, y2, route, nfin)


def _moe_rows(parts, w_gate, w_up, w_down):
    eid = jnp.concatenate([ids for _, ids in parts], axis=1).astype(jnp.int32)
    ntok = eid.shape[1]
    assert ntok % RANK_TILE == 0 and ntok % SC_WINDOW == 0
    rank, cnt = _rank_call(eid)
    counts = cnt[:, 0].astype(jnp.int32)
    padded = ((counts + ROW_TILE - 1) // ROW_TILE) * ROW_TILE
    ends = jnp.cumsum(padded)
    starts = ends - padded
    expert_iota = jnp.arange(N_EXPERTS, dtype=jnp.int32)[:, None, None]
    pos = rank + jnp.sum(jnp.where(eid[None] == expert_iota, starts[:, None, None], 0), axis=0)
    n_rows = (2 * ntok + N_EXPERTS * (ROW_TILE - 1)) // ROW_TILE * ROW_TILE
    xs = _dispatch_rows([rows for rows, _ in parts], pos, n_rows)
    ys = _expert_call(starts // ROW_TILE, padded // ROW_TILE, counts, ends[-1:] // ROW_TILE, xs,
                      w_gate, w_up, w_down)
    return _return_rows(ys, pos.reshape(-1)).reshape(2, ntok, D_MODEL // 2)


def kernel(x_prompt, x_sample, state_conv, state_gla, norm_mix, w_in, w_conv, w_alpha_up, b_alpha,
           gla_norm, w_conv_out, w_gla_out, w_o, norm_ffn, w_group_router, b_group_router,
           w_expert_router, b_expert_router, w_gate, w_up, w_down, norm_final):
    nbp, seq_p, _ = x_prompt.shape
    nbs, seq_s, _ = x_sample.shape
    assert norm_mix.shape[0] == 1, "single layer"
    assert seq_p % PROMPT_TILE == 0 and nbs % SAMPLE_SEQS == 0 and seq_s == 8

    weights = _mixer_weights(norm_mix, w_in, w_conv, w_alpha_up, b_alpha, gla_norm, w_conv_out,
                             w_gla_out, w_o, norm_ffn, w_group_router, b_group_router,
                             w_expert_router, b_expert_router)

    ntp, nts = nbp * seq_p, nbs * seq_s
    assert ntp % COMBINE_TILE == 0 and nts % COMBINE_TILE == 0
    conv0 = jnp.zeros((nbp, CONV_WIDTH - 1, D_CONV), F32)
    gla0 = jnp.zeros((nbp, 1, 8, LANES), F32)
    flat = lambda a: a.reshape(-1, a.shape[-1])
    ids_of = lambda route_t: route_t[:, 2:4, :].transpose(1, 0, 2).reshape(2, -1)
    h_p, xn_p, rt_p, rtt_p, conv_p, gla_p = _mixer_call(x_prompt, conv0, gla0, weights, prompt=True)
    h_s, xn_s, rt_s, rtt_s, conv_s, gla_s = _mixer_call(x_sample, state_conv[0], state_gla[0],
                                                        weights, prompt=False)
    y2 = _moe_rows([(flat(xn_p), ids_of(rtt_p)), (flat(xn_s), ids_of(rtt_s))],
                   w_gate[0], w_up[0], w_down[0])
    nfin = norm_final[None, :]
    y_p = _combine_call(flat(h_p), y2, flat(rt_p), nfin, 0)
    y_s = _combine_call(flat(h_s), y2, flat(rt_s), nfin, ntp)
    return (y_p.reshape(nbp, seq_p, D_MODEL), y_s.reshape(nbs, seq_s, D_MODEL),
            conv_p[None], gla_p[None], conv_s[None], gla_s[None])
```

```python
import functools

import jax
import jax.numpy as jnp
from jax import lax
from jax.experimental import pallas as pl
from jax.experimental.pallas import tpu as pltpu
from jax.experimental.pallas import tpu_sc as plsc

F32 = jnp.float32
BF16 = jnp.bfloat16

D_MODEL = 1024
D_CONV = 512
CONV_WIDTH = 3
GLA_HEADS = 4
GLA_DK = 128
GLA_DV = 256
GLA_RANK = 16
GLA_GATE_NORM = 16.0
GLA_CHUNK = 32
N_GROUPS = 4
EXPERTS_PER_GROUP = 8
N_EXPERTS = 32
D_EXPERT = 256
EPS = 1e-6

LANES = 128
V7X_VMEM_BYTES = 64 * 1024 * 1024
OFF_CB, OFF_CC, OFF_CH = 0, 512, 1024
OFF_Q, OFF_K, OFF_V, OFF_G = 1536, 2048, 2560, 3584
OFF_GA, OFF_GB, OFF_AL = 4608, 5632, 6656

PROMPT_TILE = 512
GLA_BLOCK = 256
STAGE_SKEW = 0
SAMPLE_SEQS = 32
STATE_CHUNK = 4
STATE_OUT_SLOTS = 2
ROW_TILE = 256
EXPERT_RING = 4
SC_WINDOW = 128
COMBINE_TILE = 512
RANK_TILE = 1024
RANK_BLOCK = 512
MIXER_VMEM_LIMIT = V7X_VMEM_BYTES - 8 * 1024 * 1024
STREAM_VMEM_LIMIT = V7X_VMEM_BYTES // 4


def _rmsnorm(x, g):
    ms = jnp.mean(x * x, axis=-1, keepdims=True)
    return (x * lax.rsqrt(ms + EPS)) * g


def _split_bf16(x):
    hi = x.astype(BF16)
    lo = (x - hi.astype(F32)).astype(BF16)
    return hi, lo


def _pack_bf16_pairs(x):
    n = x.shape[1] // 2
    lo = lax.bitcast_convert_type(x[:, :n].astype(BF16).astype(F32), jnp.uint32)
    hi = lax.bitcast_convert_type(x[:, n:].astype(BF16).astype(F32), jnp.uint32)
    return lax.shift_right_logical(lo, jnp.uint32(16)) | hi


def _unpack_bf16_pairs(w):
    lo = lax.bitcast_convert_type(lax.shift_left(w, jnp.uint32(16)), F32)
    hi = lax.bitcast_convert_type(w & jnp.uint32(0xFFFF0000), F32)
    return lo, hi


def _dot(a, b):
    return jnp.dot(a, b, preferred_element_type=F32)


def _chunk_causal(n, chunk):
    r = lax.broadcasted_iota(jnp.int32, (n, n), 0)
    c = lax.broadcasted_iota(jnp.int32, (n, n), 1)
    shift = chunk.bit_length() - 1
    assert chunk == 1 << shift
    same = lax.shift_right_arithmetic(r, shift) == lax.shift_right_arithmetic(c, shift)
    return same & (c <= r)


def _router_logits(xn2, wr_ref, br_ref):
    xh, xl = _split_bf16(xn2)
    part = _dot(xh, wr_ref[...]) + _dot(xl, wr_ref[...])
    return part[:, :LANES] + part[:, LANES:] + br_ref[...]


def _route(lg):
    n = lg.shape[0]
    n_rows = 40
    lt = lg.T[:n_rows, :]
    row_i = lax.broadcasted_iota(jnp.int32, (n_rows, n), 0)
    row = row_i.astype(F32)
    neg = -jnp.inf
    none = float(LANES)
    is_g = row_i < N_GROUPS
    gm = jnp.max(jnp.where(is_g, lt, neg), axis=0, keepdims=True)
    gs = jnp.sum(jnp.where(is_g, jnp.exp(lt - gm), 0.0), axis=0, keepdims=True)
    g_top = 1.0 / gs
    g_idx = jnp.min(jnp.where(is_g & (lt == gm), row, none), axis=0, keepdims=True)
    e_group = lax.shift_right_arithmetic(row_i - N_GROUPS, 3).astype(F32)
    sel = (row_i >= N_GROUPS) & (row_i < N_GROUPS + N_EXPERTS) & (e_group == g_idx)
    el = jnp.where(sel, lt, neg)
    e1 = jnp.max(el, axis=0, keepdims=True)
    i1 = jnp.min(jnp.where(sel & (el == e1), row, none), axis=0, keepdims=True)
    sel2 = sel & (row != i1)
    el2 = jnp.where(sel2, lt, neg)
    e2 = jnp.max(el2, axis=0, keepdims=True)
    i2 = jnp.min(jnp.where(sel2 & (el2 == e2), row, none), axis=0, keepdims=True)
    d = jnp.exp(e2 - e1)
    w1 = g_top / (1.0 + d)
    w2 = (g_top * d) / (1.0 + d)
    id1 = i1 - float(N_GROUPS)
    id2 = i2 - float(N_GROUPS)
    out_i = lax.broadcasted_iota(jnp.int32, (8, n), 0)
    route_t = jnp.where(out_i == 0, w1, jnp.where(out_i == 1, w2, jnp.where(out_i == 2, id1,
                        jnp.where(out_i == 3, id2, 0.0))))
    route = jnp.concatenate([route_t, jnp.zeros((LANES - 8, n), F32)], axis=0).T
    return route, route_t


def _mixer_stages(sub, nsub, x_ref, conv_in_ref, gla_in_ref,
                  nmix_ref, w_ref, wgate_ref, walr_ref, wconv_ref, walpha_ref, balpha_ref, gnorm_ref,
                  wco_ref, wgo_ref, wo_ref, nffn_ref, wr_ref, br_ref,
                  h_ref, xn2_ref, route_ref, route_t_ref, conv_out_ref, gla_out_ref,
                  s_ref, ubuf_ref, *state_ring, prompt, logits_out):
    if prompt:
        rows, chunk, nseq = GLA_BLOCK, GLA_CHUNK, 1
        r0 = sub * rows
        x = x_ref[0, r0:r0 + rows, :]
    else:
        nseq, chunk = SAMPLE_SEQS, x_ref.shape[1]
        rows = nseq * chunk
        r0 = 0
        x = x_ref[...].reshape(rows, D_MODEL)
        sin_ref, sout_ref, sem_in, sem_out = state_ring
        n_state_chunks = nseq // STATE_CHUNK
        step, n_steps = pl.program_id(0), pl.num_programs(0)

        def chunk_seqs(c, of_step):
            first = of_step * nseq + c * STATE_CHUNK
            return pl.ds(pl.multiple_of(first, STATE_CHUNK), STATE_CHUNK)

        def in_copy(c, of_step):
            return pltpu.make_async_copy(gla_in_ref.at[chunk_seqs(c, of_step)], sin_ref.at[c],
                                         sem_in.at[c])

        def out_copy(c, of_step):
            slot = c % STATE_OUT_SLOTS
            return pltpu.make_async_copy(sout_ref.at[slot], gla_out_ref.at[chunk_seqs(c, of_step)],
                                         sem_out.at[slot])

        @pl.when(step == 0)
        def _():
            for c in range(n_state_chunks):
                in_copy(c, step).start()
    nchunks = rows // chunk
    last_sub = sub == nsub - 1

    xn = _rmsnorm(x, nmix_ref[...]).astype(BF16)
    yield

    def proj(off, n):
        if off == OFF_AL:
            return _dot(xn, walr_ref[...])
        if off >= OFF_GA:
            return _dot(xn, wgate_ref[:, off - OFF_GA:off - OFF_GA + n])
        return _dot(xn, w_ref[:, off:off + n])

    alr = proj(OFF_AL, LANES).astype(BF16)
    lap = _dot(alr, walpha_ref[...])
    la = jax.nn.log_sigmoid(lap + balpha_ref[...]) * (1.0 / GLA_GATE_NORM)
    la_hi, la_lo = _split_bf16(la)
    causal = _chunk_causal(rows, chunk)
    tri = jnp.where(causal, 1.0, 0.0).astype(BF16)
    yield
    cc = proj(OFF_CC, D_CONV)
    ch = proj(OFF_CH, D_CONV)
    cb = proj(OFF_CB, D_CONV)
    q = proj(OFF_Q, D_CONV)
    k = proj(OFF_K, D_CONV)
    yield
    b = _dot(tri, la_hi) + _dot(tri, la_lo)

    u = cc * ch
    wc = wconv_ref[...]
    if prompt:
        ubuf_ref[0, 8 + r0:8 + r0 + rows, :] = u
        z = (wc[0:1] * ubuf_ref[0, 6 + r0:6 + r0 + rows, :]
             + wc[1:2] * ubuf_ref[0, 7 + r0:7 + r0 + rows, :] + wc[2:3] * u)
        if last_sub:
            tail = u[rows - 2:rows, :]
            ubuf_ref[0, 6:8, :] = tail
            conv_out_ref[0] = tail
    else:
        u3 = u.reshape(nseq, chunk, D_CONV)
        ubuf_ref[:, 6:8, :] = conv_in_ref[...]
        ubuf_ref[:, 8:8 + chunk, :] = u3
        z3 = (wc[0:1] * ubuf_ref[:, 6:6 + chunk, :] + wc[1:2] * ubuf_ref[:, 7:7 + chunk, :]
              + wc[2:3] * u3)
        z = z3.reshape(rows, D_CONV)
        conv_out_ref[...] = ubuf_ref[:, 6 + chunk:8 + chunk, :]
    ya = _dot((cb * z).astype(BF16), wco_ref[...])
    v = proj(OFF_V, GLA_HEADS * GLA_DV)
    yield

    b3 = b.reshape(nchunks, chunk, D_CONV)
    bl3 = b3[:, chunk - 1:chunk, :]
    qin = (q * (GLA_DK ** -0.5)) * jnp.exp(b)
    kin = k * jnp.exp(-b)
    kst = (k.reshape(nchunks, chunk, D_CONV) * jnp.exp(bl3 - b3)).reshape(rows, D_CONV)
    pad = (-rows) % LANES
    rows_p = rows + pad

    def transposed(a):
        if pad:
            a = jnp.concatenate([a, jnp.zeros((pad, a.shape[1]), F32)], axis=0)
        return a.T

    b_t = transposed(b)
    kst_t = transposed(kst) if prompt else None
    col_chunk = lax.shift_right_arithmetic(
        lax.broadcasted_iota(jnp.int32, (GLA_DK, rows_p), 1), chunk.bit_length() - 1)
    g = proj(OFF_G, GLA_HEADS * GLA_DV)
    yield
    heads = range(GLA_HEADS)
    vhs = [v[:, hd * GLA_DV:(hd + 1) * GLA_DV].astype(BF16) for hd in heads]
    scs = [lax.dot_general(qin[:, hd * GLA_DK:(hd + 1) * GLA_DK].astype(BF16),
                           kin[:, hd * GLA_DK:(hd + 1) * GLA_DK].astype(BF16),
                           (((1,), (1,)), ((), ())), preferred_element_type=F32) for hd in heads]
    ga = proj(OFF_GA, D_MODEL)
    yield
    upds = []
    for hd in heads if prompt else ():
        kst_h = kst_t[hd * GLA_DK:(hd + 1) * GLA_DK, :]
        stacked = jnp.concatenate(
            [jnp.where(col_chunk == n, kst_h, 0.0).astype(BF16) for n in range(nchunks)], axis=0)
        vh_p = jnp.concatenate([vhs[hd], jnp.zeros((pad, GLA_DV), BF16)], axis=0) if pad else vhs[hd]
        upds.append(_dot(stacked, vh_p))
    o_intras = [_dot(jnp.where(causal, scs[hd], 0.0).astype(BF16), vhs[hd]) for hd in heads]
    gb = proj(OFF_GB, D_MODEL)
    merge_a = jax.nn.sigmoid(ga) * ya
    gate_b = jax.nn.sigmoid(gb)
    yield
    def q_block(n, hd):
        return qin[n * chunk:(n + 1) * chunk, hd * GLA_DK:(hd + 1) * GLA_DK].astype(BF16)

    def decay_col(n, hd):
        last = n * chunk + chunk - 1
        return jnp.exp(b_t[hd * GLA_DK:(hd + 1) * GLA_DK, last:last + 1])

    o_inter = [[None] * nchunks for _ in heads]
    if prompt:
        for hd in heads:
            s_run = s_ref[hd]
            for n in range(nchunks):
                o_inter[hd][n] = _dot(q_block(n, hd), s_run.astype(BF16))
                s_run = s_run * decay_col(n, hd) + upds[hd][n * GLA_DK:(n + 1) * GLA_DK, :]
            s_ref[hd] = s_run
            if last_sub:
                gla_out_ref[0, hd] = s_run
    else:
        for c in range(n_state_chunks):
            in_copy(c, step).wait()
            if c >= STATE_OUT_SLOTS:
                out_copy(c - STATE_OUT_SLOTS, step).wait()
            else:
                @pl.when(step > 0)
                def _():
                    out_copy(c + n_state_chunks - STATE_OUT_SLOTS, step - 1).wait()
            for j in range(STATE_CHUNK):
                n = c * STATE_CHUNK + j
                for hd in heads:
                    s_prev = sin_ref[c, j, hd]
                    o_inter[hd][n] = _dot(q_block(n, hd), s_prev.astype(BF16))
                    ksb = kst[n * chunk:(n + 1) * chunk, hd * GLA_DK:(hd + 1) * GLA_DK].astype(BF16)
                    vb = v[n * chunk:(n + 1) * chunk, hd * GLA_DV:(hd + 1) * GLA_DV].astype(BF16)
                    upd = lax.dot_general(ksb, vb, (((0,), (0,)), ((), ())),
                                          preferred_element_type=F32)
                    sout_ref[c % STATE_OUT_SLOTS, j, hd] = s_prev * decay_col(n, hd) + upd
            out_copy(c, step).start()

        @pl.when(step + 1 < n_steps)
        def _():
            for c in range(n_state_chunks):
                in_copy(c, step + 1).start()

        @pl.when(step + 1 == n_steps)
        def _():
            for c in range(n_state_chunks - STATE_OUT_SLOTS, n_state_chunks):
                out_copy(c, step).wait()
    gated = []
    for hd in heads:
        v0 = hd * GLA_DV
        oh = _rmsnorm(o_intras[hd] + jnp.concatenate(o_inter[hd], axis=0), gnorm_ref[...])
        gh = g[:, v0:v0 + GLA_DV]
        gated.append(oh * (gh * jax.nn.sigmoid(gh)))
    yield
    yb = _dot(jnp.concatenate(gated, axis=1).astype(BF16), wgo_ref[...])
    yield

    m = merge_a + gate_b * yb
    hh = x + _dot(m.astype(BF16), wo_ref[...])
    yield
    xn2 = _rmsnorm(hh, nffn_ref[...])
    yield
    logits_out.append(_router_logits(xn2, wr_ref, br_ref))
    if prompt:
        h_ref[0, r0:r0 + rows, :] = hh
        xn2_ref[0, r0:r0 + rows, :] = _pack_bf16_pairs(xn2)
    else:
        h_ref[...] = hh.reshape(nseq, chunk, D_MODEL)
        xn2_ref[...] = _pack_bf16_pairs(xn2).reshape(nseq, chunk, D_MODEL // 2)


def _mixer_kernel(*refs, prompt):
    n_scratch = 2 if prompt else 6
    outs = refs[len(refs) - n_scratch - 6:len(refs) - n_scratch]
    route_ref, route_t_ref = outs[2], outs[3]
    if prompt:
        s_ref, ubuf_ref = refs[-2:]

        @pl.when(pl.program_id(1) == 0)
        def _():
            s_ref[...] = jnp.zeros_like(s_ref)
            ubuf_ref[...] = jnp.zeros_like(ubuf_ref)
    nsub = PROMPT_TILE // GLA_BLOCK if prompt else 1
    logits = []
    stages = [_mixer_stages(sub, nsub, *refs, prompt=prompt, logits_out=logits)
              for sub in range(nsub)]
    done = [False] * nsub
    step = 0
    while not all(done):
        for sub, st in enumerate(stages):
            if step >= sub * STAGE_SKEW and not done[sub]:
                done[sub] = next(st, "end") == "end"
        step += 1
    route, route_t = _route(jnp.concatenate(logits, axis=0))
    if prompt:
        route_ref[0] = route
    else:
        route_ref[...] = route.reshape(route_ref.shape)
    route_t_ref[0] = route_t


def _mixer_weights(norm_mix, w_in, w_conv, w_alpha_up, b_alpha, gla_norm, w_conv_out, w_gla_out, w_o,
                   norm_ffn, w_group_router, b_group_router, w_expert_router, b_expert_router):
    wi = w_in[0]
    w_main = wi[:, :OFF_GA].astype(BF16)
    w_gates = wi[:, OFF_GA + GLA_RANK:].astype(BF16)
    w_alr = jnp.concatenate([wi[:, OFF_GA:OFF_GA + GLA_RANK],
                             jnp.zeros((D_MODEL, LANES - GLA_RANK), F32)], axis=1).astype(BF16)
    walpha = jnp.concatenate(
        [w_alpha_up[0], jnp.zeros((LANES - GLA_RANK, D_CONV), F32)], axis=0).astype(BF16)
    wr = jnp.concatenate(
        [w_group_router[0], w_expert_router[0],
         jnp.zeros((D_MODEL, LANES - N_GROUPS - N_EXPERTS), F32)], axis=1)
    wr_hi = wr.astype(BF16)
    wr_split = jnp.concatenate([wr_hi, (wr - wr_hi.astype(F32)).astype(BF16)], axis=1)
    br = jnp.concatenate([b_group_router[0], b_expert_router[0],
                          jnp.zeros((LANES - N_GROUPS - N_EXPERTS,), F32)])[None, :]
    return (norm_mix, w_main, w_gates, w_alr, w_conv[0], walpha, b_alpha, gla_norm,
            w_conv_out[0].astype(BF16), w_gla_out[0].astype(BF16), w_o[0].astype(BF16),
            norm_ffn, wr_split, br)


def _const_spec(shape):
    nd = len(shape)
    return pl.BlockSpec(shape, lambda *_: (0,) * nd, pipeline_mode=pl.Buffered(1))


def _mixer_call(x, conv_state, gla_state, weights, *, prompt):
    nb, seq, _ = x.shape
    if prompt:
        grid = (nb, seq // PROMPT_TILE)
        tok = lambda last: pl.BlockSpec((1, PROMPT_TILE, last), lambda b, i: (b, i, 0))
        conv_spec = pl.BlockSpec((1, CONV_WIDTH - 1, D_CONV), lambda b, i: (b, 0, 0))
        gla_spec = pl.BlockSpec((1, GLA_HEADS, GLA_DK, GLA_DV), lambda b, i: (b, 0, 0, 0))
        gla_in_spec = pl.BlockSpec((1, 1, 8, LANES), lambda b, i: (b, 0, 0, 0))
        scratch = [pltpu.VMEM((GLA_HEADS, GLA_DK, GLA_DV), F32),
                   pltpu.VMEM((1, PROMPT_TILE + 8, D_CONV), F32)]
        route_t_shape = (nb, 8, seq)
        route_t_spec = pl.BlockSpec((1, 8, PROMPT_TILE), lambda b, i: (b, 0, i))
    else:
        grid = (nb // SAMPLE_SEQS, 1)
        tok = lambda last: pl.BlockSpec((SAMPLE_SEQS, seq, last), lambda b, i: (b, 0, 0))
        conv_spec = pl.BlockSpec((SAMPLE_SEQS, CONV_WIDTH - 1, D_CONV), lambda b, i: (b, 0, 0))
        gla_spec = pl.BlockSpec(memory_space=pl.ANY)
        gla_in_spec = gla_spec
        state_chunk = (STATE_CHUNK, GLA_HEADS, GLA_DK, GLA_DV)
        scratch = [pltpu.VMEM((1, 8, LANES), F32),
                   pltpu.VMEM((SAMPLE_SEQS, 8 + seq, D_CONV), F32),
                   pltpu.VMEM((SAMPLE_SEQS // STATE_CHUNK,) + state_chunk, F32),
                   pltpu.VMEM((STATE_OUT_SLOTS,) + state_chunk, F32),
                   pltpu.SemaphoreType.DMA((SAMPLE_SEQS // STATE_CHUNK,)),
                   pltpu.SemaphoreType.DMA((STATE_OUT_SLOTS,))]
        route_t_shape = (nb // SAMPLE_SEQS, 8, SAMPLE_SEQS * seq)
        route_t_spec = pl.BlockSpec((1, 8, SAMPLE_SEQS * seq), lambda b, i: (b, 0, 0))
    out_shape = (jax.ShapeDtypeStruct((nb, seq, D_MODEL), F32),
                 jax.ShapeDtypeStruct((nb, seq, D_MODEL // 2), jnp.uint32),
                 jax.ShapeDtypeStruct((nb, seq, LANES), F32),
                 jax.ShapeDtypeStruct(route_t_shape, F32),
                 jax.ShapeDtypeStruct((nb, CONV_WIDTH - 1, D_CONV), F32),
                 jax.ShapeDtypeStruct((nb, GLA_HEADS, GLA_DK, GLA_DV), F32))
    return pl.pallas_call(
        functools.partial(_mixer_kernel, prompt=prompt),
        grid=grid,
        in_specs=[tok(D_MODEL), conv_spec, gla_in_spec] + [_const_spec(w.shape) for w in weights],
        out_specs=(tok(D_MODEL), tok(D_MODEL // 2), tok(LANES), route_t_spec, conv_spec, gla_spec),
        out_shape=out_shape,
        scratch_shapes=scratch,
        compiler_params=pltpu.CompilerParams(
            dimension_semantics=("arbitrary", "arbitrary"), vmem_limit_bytes=MIXER_VMEM_LIMIT),
        name="mixer_prompt" if prompt else "mixer_sample",
    )(x, conv_state, gla_state, *weights)


def _rank_kernel(eid_ref, rank_ref, cnt_ref, carry_ref):
    @pl.when(pl.program_id(0) == 0)
    def _():
        carry_ref[...] = jnp.zeros_like(carry_ref)

    n = RANK_BLOCK
    e_iota = lax.broadcasted_iota(jnp.int32, (N_EXPERTS, n), 0)
    r = lax.broadcasted_iota(jnp.int32, (n, n), 0)
    c = lax.broadcasted_iota(jnp.int32, (n, n), 1)
    before = jnp.where(r < c, 1.0, 0.0).astype(BF16)
    carry = carry_ref[...]
    for j in range(eid_ref.shape[1] // n):
        cols = slice(j * n, (j + 1) * n)
        oh0 = jnp.where(e_iota == eid_ref[0:1, cols], 1.0, 0.0)
        oh1 = jnp.where(e_iota == eid_ref[1:2, cols], 1.0, 0.0)
        cnt = oh0 + oh1
        base = carry + _dot(cnt.astype(BF16), before)
        rank0 = jnp.sum(oh0 * base, axis=0, keepdims=True)
        rank1 = jnp.sum(oh1 * base, axis=0, keepdims=True)
        rank_ref[:, cols] = jnp.concatenate([rank0, rank1], axis=0).astype(jnp.int32)
        carry = carry + jnp.sum(cnt, axis=1, keepdims=True)
    carry_ref[...] = carry
    cnt_ref[...] = jnp.broadcast_to(carry, cnt_ref.shape)


def _rank_call(eid):
    ntok = eid.shape[1]
    return pl.pallas_call(
        _rank_kernel,
        grid=(ntok // RANK_TILE,),
        in_specs=[pl.BlockSpec((2, RANK_TILE), lambda i: (0, i))],
        out_specs=(pl.BlockSpec((2, RANK_TILE), lambda i: (0, i)),
                   pl.BlockSpec((N_EXPERTS, LANES), lambda i: (0, 0))),
        out_shape=(jax.ShapeDtypeStruct((2, ntok), jnp.int32),
                   jax.ShapeDtypeStruct((N_EXPERTS, LANES), F32)),
        scratch_shapes=[pltpu.VMEM((N_EXPERTS, 1), F32)],
        compiler_params=pltpu.CompilerParams(dimension_semantics=("arbitrary",)),
        name="expert_rank",
    )(eid)


def _sc_mesh():
    return plsc.VectorSubcoreMesh(core_axis_name="c", subcore_axis_name="s")


def _sc_worker_windows(n_windows, body):
    info = plsc.get_sparse_core_info()
    n_workers = info.num_cores * info.num_subcores
    wid = lax.axis_index("s") * info.num_cores + lax.axis_index("c")

    @pl.loop(0, pl.cdiv(n_windows, n_workers))
    def _(j):
        win = wid + n_workers * j

        @pl.when(win < n_windows)
        def _():
            body(win)


def _dispatch_rows(sources, pos, n_rows):
    width, dtype = sources[0].shape[1], sources[0].dtype
    n_win = [src.shape[0] // SC_WINDOW for src in sources]
    first_win = [sum(n_win[:i]) for i in range(len(sources))]
    total_win = sum(n_win)
    pos3 = pos.reshape(2, total_win, SC_WINDOW).transpose(1, 0, 2)

    @functools.partial(pl.kernel, mesh=_sc_mesh(), name="moe_dispatch",
                       out_type=jax.ShapeDtypeStruct((n_rows, width), dtype),
                       scratch_types=[pltpu.VMEM((2, SC_WINDOW), jnp.int32),
                                      pltpu.VMEM((SC_WINDOW, width), dtype)])
    def run(*refs):
        src_hbm, (pos_hbm, out_hbm, idx_v, buf) = refs[:len(sources)], refs[len(sources):]

        def window(win):
            pltpu.sync_copy(pos_hbm.at[win], idx_v)
            for src, w0, nw in zip(src_hbm, first_win, n_win):
                @pl.when((win >= w0) & (win < w0 + nw))
                def _():
                    r0 = pl.multiple_of((win - w0) * SC_WINDOW, SC_WINDOW)
                    pltpu.sync_copy(src.at[pl.ds(r0, SC_WINDOW)], buf)

            pltpu.sync_copy(buf, out_hbm.at[idx_v.at[0]])
            pltpu.sync_copy(buf, out_hbm.at[idx_v.at[1]])

        _sc_worker_windows(total_win, window)

    return run(*sources, pos3)


def _return_rows(ys, idx):
    n_out, width = idx.shape[0], ys.shape[1]
    n_windows = n_out // SC_WINDOW

    @functools.partial(pl.kernel, mesh=_sc_mesh(), name="moe_return",
                       out_type=jax.ShapeDtypeStruct((n_out, width), ys.dtype),
                       scratch_types=[pltpu.VMEM((SC_WINDOW,), jnp.int32),
                                      pltpu.VMEM((SC_WINDOW, width), ys.dtype)])
    def run(ys_hbm, idx_hbm, out_hbm, idx_v, buf):
        def window(win):
            pltpu.sync_copy(idx_hbm.at[win], idx_v)
            pltpu.sync_copy(ys_hbm.at[idx_v], buf)
            r0 = pl.multiple_of(win * SC_WINDOW, SC_WINDOW)
            pltpu.sync_copy(buf, out_hbm.at[pl.ds(r0, SC_WINDOW)])

        _sc_worker_windows(n_windows, window)

    return run(ys, idx.reshape(n_windows, SC_WINDOW))


def _expert_kernel(first_ref, ntile_ref, nrow_ref, total_ref, xs_hbm, wg_ref, wu_ref, wd_ref, out_hbm,
                   xbuf, obuf, wg_s, wu_s, wd_s, sem_in, sem_out):
    e = pl.program_id(0)
    nt = ntile_ref[e]
    g0 = first_ref[e]
    total = total_ref[0]
    ring = EXPERT_RING

    def rows_of(g):
        return pl.ds(pl.multiple_of(g * ROW_TILE, ROW_TILE), ROW_TILE)

    def in_copy(g):
        slot = lax.rem(g, ring)
        return pltpu.make_async_copy(xs_hbm.at[rows_of(g)], xbuf.at[slot], sem_in.at[slot])

    def out_copy(g):
        slot = lax.rem(g, ring)
        return pltpu.make_async_copy(obuf.at[slot], out_hbm.at[rows_of(g)], sem_out.at[slot])

    @pl.when(e == 0)
    def _():
        for g in range(ring - 1):
            @pl.when(g < total)
            def _():
                in_copy(g).start()

    @pl.when(nt > 0)
    def _():
        wg_s[...] = wg_ref[0].astype(BF16)
        wu_s[...] = wu_ref[0].astype(BF16)
        wd_s[...] = wd_ref[0].astype(BF16)

    def tile(g, carry):
        slot = lax.rem(g, ring)
        in_copy(g).wait()

        @pl.when(g + (ring - 1) < total)
        def _():
            in_copy(g + (ring - 1)).start()

        @pl.when(g >= ring)
        def _():
            out_copy(g - ring).wait()

        n_valid = nrow_ref[e] - (g - g0) * ROW_TILE
        row = lax.broadcasted_iota(jnp.int32, (ROW_TILE, 1), 0)
        lo, hi = _unpack_bf16_pairs(jnp.where(row < n_valid, xbuf[slot], jnp.uint32(0)))
        half = D_MODEL // 2
        sub = ROW_TILE // 2
        parts = []
        for r in (0, sub):
            xl, xh = lo[r:r + sub].astype(BF16), hi[r:r + sub].astype(BF16)
            gt = _dot(xl, wg_s[:half, :]) + _dot(xh, wg_s[half:, :])
            up = _dot(xl, wu_s[:half, :]) + _dot(xh, wu_s[half:, :])
            parts.append((gt, up))
        for r, (gt, up) in zip((0, sub), parts):
            act = (gt * jax.nn.sigmoid(gt)) * up
            obuf[slot, r:r + sub, :] = _pack_bf16_pairs(_dot(act.astype(BF16), wd_s[...]))
        out_copy(g).start()
        return carry

    lax.fori_loop(g0, g0 + nt, tile, 0)

    @pl.when(e == pl.num_programs(0) - 1)
    def _():
        for k in range(1, ring + 1):
            @pl.when(total >= k)
            def _():
                out_copy(total - k).wait()


def _expert_call(first_tile, n_tiles_per_expert, n_rows_per_expert, total_tiles, xs,
                 w_gate, w_up, w_down):
    nrows = xs.shape[0]
    by_expert = lambda e, *_: (e, 0, 0)
    half_row = D_MODEL // 2
    return pl.pallas_call(
        _expert_kernel,
        grid_spec=pltpu.PrefetchScalarGridSpec(
            num_scalar_prefetch=4,
            grid=(N_EXPERTS,),
            in_specs=[pl.BlockSpec(memory_space=pl.ANY),
                      pl.BlockSpec((1, D_MODEL, D_EXPERT), by_expert),
                      pl.BlockSpec((1, D_MODEL, D_EXPERT), by_expert),
                      pl.BlockSpec((1, D_EXPERT, D_MODEL), by_expert)],
            out_specs=pl.BlockSpec(memory_space=pl.ANY),
            scratch_shapes=[pltpu.VMEM((EXPERT_RING, ROW_TILE, half_row), jnp.uint32),
                            pltpu.VMEM((EXPERT_RING, ROW_TILE, half_row), jnp.uint32),
                            pltpu.VMEM((D_MODEL, D_EXPERT), BF16),
                            pltpu.VMEM((D_MODEL, D_EXPERT), BF16),
                            pltpu.VMEM((D_EXPERT, D_MODEL), BF16),
                            pltpu.SemaphoreType.DMA((EXPERT_RING,)),
                            pltpu.SemaphoreType.DMA((EXPERT_RING,))]),
        out_shape=jax.ShapeDtypeStruct((nrows, half_row), jnp.uint32),
        compiler_params=pltpu.CompilerParams(
            dimension_semantics=("arbitrary",), vmem_limit_bytes=STREAM_VMEM_LIMIT),
        name="expert_mlp",
    )(first_tile, n_tiles_per_expert, n_rows_per_expert, total_tiles, xs, w_gate, w_up, w_down)


def _combine_kernel(h_ref, y2_ref, route_ref, nfin_ref, out_ref):
    route = route_ref[...]
    w1, w2 = route[:, 0:1], route[:, 1:2]
    lo1, hi1 = _unpack_bf16_pairs(y2_ref[0])
    lo2, hi2 = _unpack_bf16_pairs(y2_ref[1])
    y = h_ref[...] + jnp.concatenate([w1 * lo1 + w2 * lo2, w1 * hi1 + w2 * hi2], axis=1)
    out_ref[...] = _rmsnorm(y, nfin_ref[...])


def _combine_call(h, y2, route, nfin, first_token):
    n = h.shape[0]
    first_block = first_token // COMBINE_TILE
    return pl.pallas_call(
        _combine_kernel,
        grid=(n // COMBINE_TILE,),
        in_specs=[pl.BlockSpec((COMBINE_TILE, D_MODEL), lambda i: (i, 0)),
                  pl.BlockSpec((2, COMBINE_TILE, D_MODEL // 2), lambda i: (0, first_block + i, 0)),
                  pl.BlockSpec((COMBINE_TILE, LANES), lambda i: (i, 0)),
                  pl.BlockSpec((1, D_MODEL), lambda i: (0, 0))],
        out_specs=pl.BlockSpec((COMBINE_TILE, D_MODEL), lambda i: (i, 0)),
        out_shape=jax.ShapeDtypeStruct((n, D_MODEL), F32),
        compiler_params=pltpu.CompilerParams(
            dimension_semantics=("arbitrary",), vmem_limit_bytes=STREAM_VMEM_LIMIT),
        name="moe_combine",
    )(h, y2, route, nfin)


def _moe_rows(parts, w_gate, w_up, w_down):
    eid = jnp.concatenate([ids for _, ids in parts], axis=1).astype(jnp.int32)
    ntok = eid.shape[1]
    assert ntok % RANK_TILE == 0 and ntok % SC_WINDOW == 0
    rank, cnt = _rank_call(eid)
    counts = cnt[:, 0].astype(jnp.int32)
    padded = ((counts + ROW_TILE - 1) // ROW_TILE) * ROW_TILE
    ends = jnp.cumsum(padded)
    starts = ends - padded
    expert_iota = jnp.arange(N_EXPERTS, dtype=jnp.int32)[:, None, None]
    pos = rank + jnp.sum(jnp.where(eid[None] == expert_iota, starts[:, None, None], 0), axis=0)
    n_rows = (2 * ntok + N_EXPERTS * (ROW_TILE - 1)) // ROW_TILE * ROW_TILE
    xs = _dispatch_rows([rows for rows, _ in parts], pos, n_rows)
    ys = _expert_call(starts // ROW_TILE, padded // ROW_TILE, counts, ends[-1:] // ROW_TILE, xs,
                      w_gate, w_up, w_down)
    return _return_rows(ys, pos.reshape(-1)).reshape(2, ntok, D_MODEL // 2)


def kernel(x_prompt, x_sample, state_conv, state_gla, norm_mix, w_in, w_conv, w_alpha_up, b_alpha,
           gla_norm, w_conv_out, w_gla_out, w_o, norm_ffn, w_group_router, b_group_router,
           w_expert_router, b_expert_router, w_gate, w_up, w_down, norm_final):
    nbp, seq_p, _ = x_prompt.shape
    nbs, seq_s, _ = x_sample.shape
    assert norm_mix.shape[0] == 1, "single layer"
    assert seq_p % PROMPT_TILE == 0 and nbs % SAMPLE_SEQS == 0 and seq_s == 8

    weights = _mixer_weights(norm_mix, w_in, w_conv, w_alpha_up, b_alpha, gla_norm, w_conv_out,
                             w_gla_out, w_o, norm_ffn, w_group_router, b_group_router,
                             w_expert_router, b_expert_router)

    ntp, nts = nbp * seq_p, nbs * seq_s
    assert ntp % COMBINE_TILE == 0 and nts % COMBINE_TILE == 0
    conv0 = jnp.zeros((nbp, CONV_WIDTH - 1, D_CONV), F32)
    gla0 = jnp.zeros((nbp, 1, 8, LANES), F32)
    flat = lambda a: a.reshape(-1, a.shape[-1])
    ids_of = lambda route_t: route_t[:, 2:4, :].transpose(1, 0, 2).reshape(2, -1)
    h_p, xn_p, rt_p, rtt_p, conv_p, gla_p = _mixer_call(x_prompt, conv0, gla0, weights, prompt=True)
    h_s, xn_s, rt_s, rtt_s, conv_s, gla_s = _mixer_call(x_sample, state_conv[0], state_gla[0],
                                                        weights, prompt=False)
    y2 = _moe_rows([(flat(xn_p), ids_of(rtt_p)), (flat(xn_s), ids_of(rtt_s))],
                   w_gate[0], w_up[0], w_down[0])
    nfin = norm_final[None, :]
    y_p = _combine_call(flat(h_p), y2, flat(rt_p), nfin, 0)
    y_s = _combine_call(flat(h_s), y2, flat(rt_s), nfin, ntp)
    return (y_p.reshape(nbp, seq_p, D_MODEL), y_s.reshape(nbs, seq_s, D_MODEL),
            conv_p[None], gla_p[None], conv_s[None], gla_s[None])
```

```python
import functools

import jax
import jax.numpy as jnp
from jax import lax
from jax.experimental import pallas as pl
from jax.experimental.pallas import tpu as pltpu
from jax.experimental.pallas import tpu_sc as plsc

F32 = jnp.float32
BF16 = jnp.bfloat16

D_MODEL = 1024
D_CONV = 512
CONV_WIDTH = 3
GLA_HEADS = 4
GLA_DK = 128
GLA_DV = 256
GLA_RANK = 16
GLA_GATE_NORM = 16.0
GLA_CHUNK = 32
N_GROUPS = 4
EXPERTS_PER_GROUP = 8
N_EXPERTS = 32
D_EXPERT = 256
EPS = 1e-6

LANES = 128
V7X_VMEM_BYTES = 64 * 1024 * 1024
OFF_CB, OFF_CC, OFF_CH = 0, 512, 1024
OFF_Q, OFF_K, OFF_V, OFF_G = 1536, 2048, 2560, 3584
OFF_GA, OFF_GB, OFF_AL = 4608, 5632, 6656

PROMPT_TILE = 512
GLA_BLOCK = 256
STAGE_SKEW = 0
SAMPLE_SEQS = 32
STATE_CHUNK = 4
STATE_OUT_SLOTS = 4
ROW_TILE = 256
EXPERT_RING = 4
SC_WINDOW = 128
COMBINE_TILE = 512
RANK_TILE = 1024
RANK_BLOCK = 512
MIXER_VMEM_LIMIT = V7X_VMEM_BYTES - 6 * 1024 * 1024
STREAM_VMEM_LIMIT = V7X_VMEM_BYTES // 4


def _rmsnorm(x, g):
    ms = jnp.mean(x * x, axis=-1, keepdims=True)
    return (x * lax.rsqrt(ms + EPS)) * g


def _split_bf16(x):
    hi = x.astype(BF16)
    lo = (x - hi.astype(F32)).astype(BF16)
    return hi, lo


def _pack_bf16_pairs(x):
    n = x.shape[1] // 2
    lo = lax.bitcast_convert_type(x[:, :n].astype(BF16).astype(F32), jnp.uint32)
    hi = lax.bitcast_convert_type(x[:, n:].astype(BF16).astype(F32), jnp.uint32)
    return lax.shift_right_logical(lo, jnp.uint32(16)) | hi


def _unpack_bf16_pairs(w):
    lo = lax.bitcast_convert_type(lax.shift_left(w, jnp.uint32(16)), F32)
    hi = lax.bitcast_convert_type(w & jnp.uint32(0xFFFF0000), F32)
    return lo, hi


def _dot(a, b):
    return jnp.dot(a, b, preferred_element_type=F32)


def _chunk_causal(n, chunk):
    r = lax.broadcasted_iota(jnp.int32, (n, n), 0)
    c = lax.broadcasted_iota(jnp.int32, (n, n), 1)
    shift = chunk.bit_length() - 1
    assert chunk == 1 << shift
    same = lax.shift_right_arithmetic(r, shift) == lax.shift_right_arithmetic(c, shift)
    return same & (c <= r)


def _router_logits(xn2, wr_ref, br_ref):
    xh, xl = _split_bf16(xn2)
    part = _dot(xh, wr_ref[...]) + _dot(xl, wr_ref[...])
    return part[:, :LANES] + part[:, LANES:] + br_ref[...]


def _route(lg):
    n = lg.shape[0]
    n_rows = 40
    lt = lg.T[:n_rows, :]
    row_i = lax.broadcasted_iota(jnp.int32, (n_rows, n), 0)
    row = row_i.astype(F32)
    neg = -jnp.inf
    none = float(LANES)
    is_g = row_i < N_GROUPS
    gm = jnp.max(jnp.where(is_g, lt, neg), axis=0, keepdims=True)
    gs = jnp.sum(jnp.where(is_g, jnp.exp(lt - gm), 0.0), axis=0, keepdims=True)
    g_top = 1.0 / gs
    g_idx = jnp.min(jnp.where(is_g & (lt == gm), row, none), axis=0, keepdims=True)
    e_group = lax.shift_right_arithmetic(row_i - N_GROUPS, 3).astype(F32)
    sel = (row_i >= N_GROUPS) & (row_i < N_GROUPS + N_EXPERTS) & (e_group == g_idx)
    el = jnp.where(sel, lt, neg)
    e1 = jnp.max(el, axis=0, keepdims=True)
    i1 = jnp.min(jnp.where(sel & (el == e1), row, none), axis=0, keepdims=True)
    sel2 = sel & (row != i1)
    el2 = jnp.where(sel2, lt, neg)
    e2 = jnp.max(el2, axis=0, keepdims=True)
    i2 = jnp.min(jnp.where(sel2 & (el2 == e2), row, none), axis=0, keepdims=True)
    d = jnp.exp(e2 - e1)
    w1 = g_top / (1.0 + d)
    w2 = (g_top * d) / (1.0 + d)
    id1 = i1 - float(N_GROUPS)
    id2 = i2 - float(N_GROUPS)
    out_i = lax.broadcasted_iota(jnp.int32, (8, n), 0)
    route_t = jnp.where(out_i == 0, w1, jnp.where(out_i == 1, w2, jnp.where(out_i == 2, id1,
                        jnp.where(out_i == 3, id2, 0.0))))
    route = jnp.concatenate([route_t, jnp.zeros((LANES - 8, n), F32)], axis=0).T
    return route, route_t


def _mixer_stages(sub, nsub, x_ref, conv_in_ref, gla_in_ref,
                  nmix_ref, w_ref, wgate_ref, walr_ref, wconv_ref, walpha_ref, balpha_ref, gnorm_ref,
                  wco_ref, wgo_ref, wo_ref, nffn_ref, wr_ref, br_ref,
                  h_ref, xn2_ref, route_ref, route_t_ref, conv_out_ref, gla_out_ref,
                  s_ref, ubuf_ref, *state_ring, prompt, logits_out):
    if prompt:
        rows, chunk, nseq = GLA_BLOCK, GLA_CHUNK, 1
        r0 = sub * rows
        x = x_ref[0, r0:r0 + rows, :]
    else:
        nseq, chunk = SAMPLE_SEQS, x_ref.shape[1]
        rows = nseq * chunk
        r0 = 0
        x = x_ref[...].reshape(rows, D_MODEL)
        sin_ref, sout_ref, sem_in, sem_out = state_ring
        n_state_chunks = nseq // STATE_CHUNK
        step, n_steps = pl.program_id(0), pl.num_programs(0)

        def chunk_seqs(c, of_step):
            first = of_step * nseq + c * STATE_CHUNK
            return pl.ds(pl.multiple_of(first, STATE_CHUNK), STATE_CHUNK)

        def in_copy(c, of_step):
            return pltpu.make_async_copy(gla_in_ref.at[chunk_seqs(c, of_step)], sin_ref.at[c],
                                         sem_in.at[c])

        def out_copy(c, of_step):
            slot = c % STATE_OUT_SLOTS
            return pltpu.make_async_copy(sout_ref.at[slot], gla_out_ref.at[chunk_seqs(c, of_step)],
                                         sem_out.at[slot])

        @pl.when(step == 0)
        def _():
            for c in range(n_state_chunks):
                in_copy(c, step).start()
    nchunks = rows // chunk
    last_sub = sub == nsub - 1

    xn = _rmsnorm(x, nmix_ref[...]).astype(BF16)
    yield

    def proj(off, n):
        if off == OFF_AL:
            return _dot(xn, walr_ref[...])
        if off >= OFF_GA:
            return _dot(xn, wgate_ref[:, off - OFF_GA:off - OFF_GA + n])
        return _dot(xn, w_ref[:, off:off + n])

    alr = proj(OFF_AL, LANES).astype(BF16)
    lap = _dot(alr, walpha_ref[...])
    la = jax.nn.log_sigmoid(lap + balpha_ref[...]) * (1.0 / GLA_GATE_NORM)
    la_hi, la_lo = _split_bf16(la)
    causal = _chunk_causal(rows, chunk)
    tri = jnp.where(causal, 1.0, 0.0).astype(BF16)
    yield
    cc = proj(OFF_CC, D_CONV)
    ch = proj(OFF_CH, D_CONV)
    cb = proj(OFF_CB, D_CONV)
    q = proj(OFF_Q, D_CONV)
    k = proj(OFF_K, D_CONV)
    yield
    b = _dot(tri, la_hi) + _dot(tri, la_lo)

    u = cc * ch
    wc = wconv_ref[...]
    if prompt:
        ubuf_ref[0, 8 + r0:8 + r0 + rows, :] = u
        z = (wc[0:1] * ubuf_ref[0, 6 + r0:6 + r0 + rows, :]
             + wc[1:2] * ubuf_ref[0, 7 + r0:7 + r0 + rows, :] + wc[2:3] * u)
        if last_sub:
            tail = u[rows - 2:rows, :]
            ubuf_ref[0, 6:8, :] = tail
            conv_out_ref[0] = tail
    else:
        u3 = u.reshape(nseq, chunk, D_CONV)
        ubuf_ref[:, 6:8, :] = conv_in_ref[...]
        ubuf_ref[:, 8:8 + chunk, :] = u3
        z3 = (wc[0:1] * ubuf_ref[:, 6:6 + chunk, :] + wc[1:2] * ubuf_ref[:, 7:7 + chunk, :]
              + wc[2:3] * u3)
        z = z3.reshape(rows, D_CONV)
        conv_out_ref[...] = ubuf_ref[:, 6 + chunk:8 + chunk, :]
    ya = _dot((cb * z).astype(BF16), wco_ref[...])
    v = proj(OFF_V, GLA_HEADS * GLA_DV)
    yield

    b3 = b.reshape(nchunks, chunk, D_CONV)
    bl3 = b3[:, chunk - 1:chunk, :]
    qin = (q * (GLA_DK ** -0.5)) * jnp.exp(b)
    kin = k * jnp.exp(-b)
    kst = (k.reshape(nchunks, chunk, D_CONV) * jnp.exp(bl3 - b3)).reshape(rows, D_CONV)
    pad = (-rows) % LANES
    rows_p = rows + pad

    def transposed(a):
        if pad:
            a = jnp.concatenate([a, jnp.zeros((pad, a.shape[1]), F32)], axis=0)
        return a.T

    b_t = transposed(b)
    kst_t = transposed(kst) if prompt else None
    col_chunk = lax.shift_right_arithmetic(
        lax.broadcasted_iota(jnp.int32, (GLA_DK, rows_p), 1), chunk.bit_length() - 1)
    g = proj(OFF_G, GLA_HEADS * GLA_DV)
    yield
    heads = range(GLA_HEADS)
    vhs = [v[:, hd * GLA_DV:(hd + 1) * GLA_DV].astype(BF16) for hd in heads]
    scs = [lax.dot_general(qin[:, hd * GLA_DK:(hd + 1) * GLA_DK].astype(BF16),
                           kin[:, hd * GLA_DK:(hd + 1) * GLA_DK].astype(BF16),
                           (((1,), (1,)), ((), ())), preferred_element_type=F32) for hd in heads]
    ga = proj(OFF_GA, D_MODEL)
    yield
    upds = []
    for hd in heads if prompt else ():
        kst_h = kst_t[hd * GLA_DK:(hd + 1) * GLA_DK, :]
        stacked = jnp.concatenate(
            [jnp.where(col_chunk == n, kst_h, 0.0).astype(BF16) for n in range(nchunks)], axis=0)
        vh_p = jnp.concatenate([vhs[hd], jnp.zeros((pad, GLA_DV), BF16)], axis=0) if pad else vhs[hd]
        upds.append(_dot(stacked, vh_p))
    o_intras = [_dot(jnp.where(causal, scs[hd], 0.0).astype(BF16), vhs[hd]) for hd in heads]
    gb = proj(OFF_GB, D_MODEL)
    merge_a = jax.nn.sigmoid(ga) * ya
    gate_b = jax.nn.sigmoid(gb)
    yield
    def q_block(n, hd):
        return qin[n * chunk:(n + 1) * chunk, hd * GLA_DK:(hd + 1) * GLA_DK].astype(BF16)

    def decay_col(n, hd):
        last = n * chunk + chunk - 1
        return jnp.exp(b_t[hd * GLA_DK:(hd + 1) * GLA_DK, last:last + 1])

    o_inter = [[None] * nchunks for _ in heads]
    if prompt:
        for hd in heads:
            s_run = s_ref[hd]
            for n in range(nchunks):
                o_inter[hd][n] = _dot(q_block(n, hd), s_run.astype(BF16))
                s_run = s_run * decay_col(n, hd) + upds[hd][n * GLA_DK:(n + 1) * GLA_DK, :]
            s_ref[hd] = s_run
            if last_sub:
                gla_out_ref[0, hd] = s_run
    else:
        for c in range(n_state_chunks):
            in_copy(c, step).wait()
            if c >= STATE_OUT_SLOTS:
                out_copy(c - STATE_OUT_SLOTS, step).wait()
            else:
                @pl.when(step > 0)
                def _():
                    out_copy(c + n_state_chunks - STATE_OUT_SLOTS, step - 1).wait()
            for j in range(STATE_CHUNK):
                n = c * STATE_CHUNK + j
                for hd in heads:
                    s_prev = sin_ref[c, j, hd]
                    o_inter[hd][n] = _dot(q_block(n, hd), s_prev.astype(BF16))
                    ksb = kst[n * chunk:(n + 1) * chunk, hd * GLA_DK:(hd + 1) * GLA_DK].astype(BF16)
                    vb = v[n * chunk:(n + 1) * chunk, hd * GLA_DV:(hd + 1) * GLA_DV].astype(BF16)
                    upd = lax.dot_general(ksb, vb, (((0,), (0,)), ((), ())),
                                          preferred_element_type=F32)
                    sout_ref[c % STATE_OUT_SLOTS, j, hd] = s_prev * decay_col(n, hd) + upd
            out_copy(c, step).start()

        @pl.when(step + 1 < n_steps)
        def _():
            for c in range(n_state_chunks):
                in_copy(c, step + 1).start()

        @pl.when(step + 1 == n_steps)
        def _():
            for c in range(n_state_chunks - STATE_OUT_SLOTS, n_state_chunks):
                out_copy(c, step).wait()
    gated = []
    for hd in heads:
        v0 = hd * GLA_DV
        oh = _rmsnorm(o_intras[hd] + jnp.concatenate(o_inter[hd], axis=0), gnorm_ref[...])
        gh = g[:, v0:v0 + GLA_DV]
        gated.append(oh * (gh * jax.nn.sigmoid(gh)))
    yield
    yb = _dot(jnp.concatenate(gated, axis=1).astype(BF16), wgo_ref[...])
    yield

    m = merge_a + gate_b * yb
    hh = x + _dot(m.astype(BF16), wo_ref[...])
    yield
    xn2 = _rmsnorm(hh, nffn_ref[...])
    yield
    logits_out.append(_router_logits(xn2, wr_ref, br_ref))
    if prompt:
        h_ref[0, r0:r0 + rows, :] = hh
        xn2_ref[0, r0:r0 + rows, :] = _pack_bf16_pairs(xn2)
    else:
        h_ref[...] = hh.reshape(nseq, chunk, D_MODEL)
        xn2_ref[...] = _pack_bf16_pairs(xn2).reshape(nseq, chunk, D_MODEL // 2)


def _mixer_kernel(*refs, prompt):
    n_scratch = 2 if prompt else 6
    outs = refs[len(refs) - n_scratch - 6:len(refs) - n_scratch]
    route_ref, route_t_ref = outs[2], outs[3]
    if prompt:
        s_ref, ubuf_ref = refs[-2:]

        @pl.when(pl.program_id(1) == 0)
        def _():
            s_ref[...] = jnp.zeros_like(s_ref)
            ubuf_ref[...] = jnp.zeros_like(ubuf_ref)
    nsub = PROMPT_TILE // GLA_BLOCK if prompt else 1
    logits = []
    stages = [_mixer_stages(sub, nsub, *refs, prompt=prompt, logits_out=logits)
              for sub in range(nsub)]
    done = [False] * nsub
    step = 0
    while not all(done):
        for sub, st in enumerate(stages):
            if step >= sub * STAGE_SKEW and not done[sub]:
                done[sub] = next(st, "end") == "end"
        step += 1
    route, route_t = _route(jnp.concatenate(logits, axis=0))
    if prompt:
        route_ref[0] = route
    else:
        route_ref[...] = route.reshape(route_ref.shape)
    route_t_ref[0] = route_t


def _mixer_weights(norm_mix, w_in, w_conv, w_alpha_up, b_alpha, gla_norm, w_conv_out, w_gla_out, w_o,
                   norm_ffn, w_group_router, b_group_router, w_expert_router, b_expert_router):
    wi = w_in[0]
    w_main = wi[:, :OFF_GA].astype(BF16)
    w_gates = wi[:, OFF_GA + GLA_RANK:].astype(BF16)
    w_alr = jnp.concatenate([wi[:, OFF_GA:OFF_GA + GLA_RANK],
                             jnp.zeros((D_MODEL, LANES - GLA_RANK), F32)], axis=1).astype(BF16)
    walpha = jnp.concatenate(
        [w_alpha_up[0], jnp.zeros((LANES - GLA_RANK, D_CONV), F32)], axis=0).astype(BF16)
    wr = jnp.concatenate(
        [w_group_router[0], w_expert_router[0],
         jnp.zeros((D_MODEL, LANES - N_GROUPS - N_EXPERTS), F32)], axis=1)
    wr_hi = wr.astype(BF16)
    wr_split = jnp.concatenate([wr_hi, (wr - wr_hi.astype(F32)).astype(BF16)], axis=1)
    br = jnp.concatenate([b_group_router[0], b_expert_router[0],
                          jnp.zeros((LANES - N_GROUPS - N_EXPERTS,), F32)])[None, :]
    return (norm_mix, w_main, w_gates, w_alr, w_conv[0], walpha, b_alpha, gla_norm,
            w_conv_out[0].astype(BF16), w_gla_out[0].astype(BF16), w_o[0].astype(BF16),
            norm_ffn, wr_split, br)


def _const_spec(shape):
    nd = len(shape)
    return pl.BlockSpec(shape, lambda *_: (0,) * nd, pipeline_mode=pl.Buffered(1))


def _mixer_call(x, conv_state, gla_state, weights, *, prompt):
    nb, seq, _ = x.shape
    if prompt:
        grid = (nb, seq // PROMPT_TILE)
        tok = lambda last: pl.BlockSpec((1, PROMPT_TILE, last), lambda b, i: (b, i, 0))
        conv_spec = pl.BlockSpec((1, CONV_WIDTH - 1, D_CONV), lambda b, i: (b, 0, 0))
        gla_spec = pl.BlockSpec((1, GLA_HEADS, GLA_DK, GLA_DV), lambda b, i: (b, 0, 0, 0))
        gla_in_spec = pl.BlockSpec((1, 1, 8, LANES), lambda b, i: (b, 0, 0, 0))
        scratch = [pltpu.VMEM((GLA_HEADS, GLA_DK, GLA_DV), F32),
                   pltpu.VMEM((1, PROMPT_TILE + 8, D_CONV), F32)]
        route_t_shape = (nb, 8, seq)
        route_t_spec = pl.BlockSpec((1, 8, PROMPT_TILE), lambda b, i: (b, 0, i))
    else:
        grid = (nb // SAMPLE_SEQS, 1)
        tok = lambda last: pl.BlockSpec((SAMPLE_SEQS, seq, last), lambda b, i: (b, 0, 0))
        conv_spec = pl.BlockSpec((SAMPLE_SEQS, CONV_WIDTH - 1, D_CONV), lambda b, i: (b, 0, 0))
        gla_spec = pl.BlockSpec(memory_space=pl.ANY)
        gla_in_spec = gla_spec
        state_chunk = (STATE_CHUNK, GLA_HEADS, GLA_DK, GLA_DV)
        scratch = [pltpu.VMEM((1, 8, LANES), F32),
                   pltpu.VMEM((SAMPLE_SEQS, 8 + seq, D_CONV), F32),
                   pltpu.VMEM((SAMPLE_SEQS // STATE_CHUNK,) + state_chunk, F32),
                   pltpu.VMEM((STATE_OUT_SLOTS,) + state_chunk, F32),
                   pltpu.SemaphoreType.DMA((SAMPLE_SEQS // STATE_CHUNK,)),
                   pltpu.SemaphoreType.DMA((STATE_OUT_SLOTS,))]
        route_t_shape = (nb // SAMPLE_SEQS, 8, SAMPLE_SEQS * seq)
        route_t_spec = pl.BlockSpec((1, 8, SAMPLE_SEQS * seq), lambda b, i: (b, 0, 0))
    out_shape = (jax.ShapeDtypeStruct((nb, seq, D_MODEL), F32),
                 jax.ShapeDtypeStruct((nb, seq, D_MODEL // 2), jnp.uint32),
                 jax.ShapeDtypeStruct((nb, seq, LANES), F32),
                 jax.ShapeDtypeStruct(route_t_shape, F32),
                 jax.ShapeDtypeStruct((nb, CONV_WIDTH - 1, D_CONV), F32),
                 jax.ShapeDtypeStruct((nb, GLA_HEADS, GLA_DK, GLA_DV), F32))
    return pl.pallas_call(
        functools.partial(_mixer_kernel, prompt=prompt),
        grid=grid,
        in_specs=[tok(D_MODEL), conv_spec, gla_in_spec] + [_const_spec(w.shape) for w in weights],
        out_specs=(tok(D_MODEL), tok(D_MODEL // 2), tok(LANES), route_t_spec, conv_spec, gla_spec),
        out_shape=out_shape,
        scratch_shapes=scratch,
        compiler_params=pltpu.CompilerParams(
            dimension_semantics=("arbitrary", "arbitrary"), vmem_limit_bytes=MIXER_VMEM_LIMIT),
        name="mixer_prompt" if prompt else "mixer_sample",
    )(x, conv_state, gla_state, *weights)


def _rank_kernel(eid_ref, rank_ref, cnt_ref, carry_ref):
    @pl.when(pl.program_id(0) == 0)
    def _():
        carry_ref[...] = jnp.zeros_like(carry_ref)

    n = RANK_BLOCK
    e_iota = lax.broadcasted_iota(jnp.int32, (N_EXPERTS, n), 0)
    r = lax.broadcasted_iota(jnp.int32, (n, n), 0)
    c = lax.broadcasted_iota(jnp.int32, (n, n), 1)
    before = jnp.where(r < c, 1.0, 0.0).astype(BF16)
    carry = carry_ref[...]
    for j in range(eid_ref.shape[1] // n):
        cols = slice(j * n, (j + 1) * n)
        oh0 = jnp.where(e_iota == eid_ref[0:1, cols], 1.0, 0.0)
        oh1 = jnp.where(e_iota == eid_ref[1:2, cols], 1.0, 0.0)
        cnt = oh0 + oh1
        base = carry + _dot(cnt.astype(BF16), before)
        rank0 = jnp.sum(oh0 * base, axis=0, keepdims=True)
        rank1 = jnp.sum(oh1 * base, axis=0, keepdims=True)
        rank_ref[:, cols] = jnp.concatenate([rank0, rank1], axis=0).astype(jnp.int32)
        carry = carry + jnp.sum(cnt, axis=1, keepdims=True)
    carry_ref[...] = carry
    cnt_ref[...] = jnp.broadcast_to(carry, cnt_ref.shape)


def _rank_call(eid):
    ntok = eid.shape[1]
    return pl.pallas_call(
        _rank_kernel,
        grid=(ntok // RANK_TILE,),
        in_specs=[pl.BlockSpec((2, RANK_TILE), lambda i: (0, i))],
        out_specs=(pl.BlockSpec((2, RANK_TILE), lambda i: (0, i)),
                   pl.BlockSpec((N_EXPERTS, LANES), lambda i: (0, 0))),
        out_shape=(jax.ShapeDtypeStruct((2, ntok), jnp.int32),
                   jax.ShapeDtypeStruct((N_EXPERTS, LANES), F32)),
        scratch_shapes=[pltpu.VMEM((N_EXPERTS, 1), F32)],
        compiler_params=pltpu.CompilerParams(dimension_semantics=("arbitrary",)),
        name="expert_rank",
    )(eid)


def _sc_mesh():
    return plsc.VectorSubcoreMesh(core_axis_name="c", subcore_axis_name="s")


def _sc_worker_windows(n_windows, body):
    info = plsc.get_sparse_core_info()
    n_workers = info.num_cores * info.num_subcores
    wid = lax.axis_index("s") * info.num_cores + lax.axis_index("c")

    @pl.loop(0, pl.cdiv(n_windows, n_workers))
    def _(j):
        win = wid + n_workers * j

        @pl.when(win < n_windows)
        def _():
            body(win)


def _dispatch_rows(sources, pos, n_rows):
    width, dtype = sources[0].shape[1], sources[0].dtype
    n_win = [src.shape[0] // SC_WINDOW for src in sources]
    first_win = [sum(n_win[:i]) for i in range(len(sources))]
    total_win = sum(n_win)
    pos3 = pos.reshape(2, total_win, SC_WINDOW).transpose(1, 0, 2)

    @functools.partial(pl.kernel, mesh=_sc_mesh(), name="moe_dispatch",
                       out_type=jax.ShapeDtypeStruct((n_rows, width), dtype),
                       scratch_types=[pltpu.VMEM((2, SC_WINDOW), jnp.int32),
                                      pltpu.VMEM((SC_WINDOW, width), dtype)])
    def run(*refs):
        src_hbm, (pos_hbm, out_hbm, idx_v, buf) = refs[:len(sources)], refs[len(sources):]

        def window(win):
            pltpu.sync_copy(pos_hbm.at[win], idx_v)
            for src, w0, nw in zip(src_hbm, first_win, n_win):
                @pl.when((win >= w0) & (win < w0 + nw))
                def _():
                    r0 = pl.multiple_of((win - w0) * SC_WINDOW, SC_WINDOW)
                    pltpu.sync_copy(src.at[pl.ds(r0, SC_WINDOW)], buf)

            pltpu.sync_copy(buf, out_hbm.at[idx_v.at[0]])
            pltpu.sync_copy(buf, out_hbm.at[idx_v.at[1]])

        _sc_worker_windows(total_win, window)

    return run(*sources, pos3)


def _return_rows(ys, idx):
    n_out, width = idx.shape[0], ys.shape[1]
    n_windows = n_out // SC_WINDOW

    @functools.partial(pl.kernel, mesh=_sc_mesh(), name="moe_return",
                       out_type=jax.ShapeDtypeStruct((n_out, width), ys.dtype),
                       scratch_types=[pltpu.VMEM((SC_WINDOW,), jnp.int32),
                                      pltpu.VMEM((SC_WINDOW, width), ys.dtype)])
    def run(ys_hbm, idx_hbm, out_hbm, idx_v, buf):
        def window(win):
            pltpu.sync_copy(idx_hbm.at[win], idx_v)
            pltpu.sync_copy(ys_hbm.at[idx_v], buf)
            r0 = pl.multiple_of(win * SC_WINDOW, SC_WINDOW)
            pltpu.sync_copy(buf, out_hbm.at[pl.ds(r0, SC_WINDOW)])

        _sc_worker_windows(n_windows, window)

    return run(ys, idx.reshape(n_windows, SC_WINDOW))


def _expert_kernel(first_ref, ntile_ref, nrow_ref, total_ref, xs_hbm, wg_ref, wu_ref, wd_ref, out_hbm,
                   xbuf, obuf, wg_s, wu_s, wd_s, sem_in, sem_out):
    e = pl.program_id(0)
    nt = ntile_ref[e]
    g0 = first_ref[e]
    total = total_ref[0]
    ring = EXPERT_RING

    def rows_of(g):
        return pl.ds(pl.multiple_of(g * ROW_TILE, ROW_TILE), ROW_TILE)

    def in_copy(g):
        slot = lax.rem(g, ring)
        return pltpu.make_async_copy(xs_hbm.at[rows_of(g)], xbuf.at[slot], sem_in.at[slot])

    def out_copy(g):
        slot = lax.rem(g, ring)
        return pltpu.make_async_copy(obuf.at[slot], out_hbm.at[rows_of(g)], sem_out.at[slot])

    @pl.when(e == 0)
    def _():
        for g in range(ring - 1):
            @pl.when(g < total)
            def _():
                in_copy(g).start()

    @pl.when(nt > 0)
    def _():
        wg_s[...] = wg_ref[0].astype(BF16)
        wu_s[...] = wu_ref[0].astype(BF16)
        wd_s[...] = wd_ref[0].astype(BF16)

    def tile(g, carry):
        slot = lax.rem(g, ring)
        in_copy(g).wait()

        @pl.when(g + (ring - 1) < total)
        def _():
            in_copy(g + (ring - 1)).start()

        @pl.when(g >= ring)
        def _():
            out_copy(g - ring).wait()

        n_valid = nrow_ref[e] - (g - g0) * ROW_TILE
        row = lax.broadcasted_iota(jnp.int32, (ROW_TILE, 1), 0)
        lo, hi = _unpack_bf16_pairs(jnp.where(row < n_valid, xbuf[slot], jnp.uint32(0)))
        half = D_MODEL // 2
        sub = ROW_TILE // 2
        parts = []
        for r in (0, sub):
            xl, xh = lo[r:r + sub].astype(BF16), hi[r:r + sub].astype(BF16)
            gt = _dot(xl, wg_s[:half, :]) + _dot(xh, wg_s[half:, :])
            up = _dot(xl, wu_s[:half, :]) + _dot(xh, wu_s[half:, :])
            parts.append((gt, up))
        for r, (gt, up) in zip((0, sub), parts):
            act = (gt * jax.nn.sigmoid(gt)) * up
            obuf[slot, r:r + sub, :] = _pack_bf16_pairs(_dot(act.astype(BF16), wd_s[...]))
        out_copy(g).start()
        return carry

    lax.fori_loop(g0, g0 + nt, tile, 0)

    @pl.when(e == pl.num_programs(0) - 1)
    def _():
        for k in range(1, ring + 1):
            @pl.when(total >= k)
            def _():
                out_copy(total - k).wait()


def _expert_call(first_tile, n_tiles_per_expert, n_rows_per_expert, total_tiles, xs,
                 w_gate, w_up, w_down):
    nrows = xs.shape[0]
    by_expert = lambda e, *_: (e, 0, 0)
    half_row = D_MODEL // 2
    return pl.pallas_call(
        _expert_kernel,
        grid_spec=pltpu.PrefetchScalarGridSpec(
            num_scalar_prefetch=4,
            grid=(N_EXPERTS,),
            in_specs=[pl.BlockSpec(memory_space=pl.ANY),
                      pl.BlockSpec((1, D_MODEL, D_EXPERT), by_expert),
                      pl.BlockSpec((1, D_MODEL, D_EXPERT), by_expert),
                      pl.BlockSpec((1, D_EXPERT, D_MODEL), by_expert)],
            out_specs=pl.BlockSpec(memory_space=pl.ANY),
            scratch_shapes=[pltpu.VMEM((EXPERT_RING, ROW_TILE, half_row), jnp.uint32),
                            pltpu.VMEM((EXPERT_RING, ROW_TILE, half_row), jnp.uint32),
                            pltpu.VMEM((D_MODEL, D_EXPERT), BF16),
                            pltpu.VMEM((D_MODEL, D_EXPERT), BF16),
                            pltpu.VMEM((D_EXPERT, D_MODEL), BF16),
                            pltpu.SemaphoreType.DMA((EXPERT_RING,)),
                            pltpu.SemaphoreType.DMA((EXPERT_RING,))]),
        out_shape=jax.ShapeDtypeStruct((nrows, half_row), jnp.uint32),
        compiler_params=pltpu.CompilerParams(
            dimension_semantics=("arbitrary",), vmem_limit_bytes=STREAM_VMEM_LIMIT),
        name="expert_mlp",
    )(first_tile, n_tiles_per_expert, n_rows_per_expert, total_tiles, xs, w_gate, w_up, w_down)


def _combine_kernel(h_ref, y2_ref, route_ref, nfin_ref, out_ref):
    route = route_ref[...]
    w1, w2 = route[:, 0:1], route[:, 1:2]
    lo1, hi1 = _unpack_bf16_pairs(y2_ref[0])
    lo2, hi2 = _unpack_bf16_pairs(y2_ref[1])
    y = h_ref[...] + jnp.concatenate([w1 * lo1 + w2 * lo2, w1 * hi1 + w2 * hi2], axis=1)
    out_ref[...] = _rmsnorm(y, nfin_ref[...])


def _combine_call(h, y2, route, nfin, first_token):
    n = h.shape[0]
    first_block = first_token // COMBINE_TILE
    return pl.pallas_call(
        _combine_kernel,
        grid=(n // COMBINE_TILE,),
        in_specs=[pl.BlockSpec((COMBINE_TILE, D_MODEL), lambda i: (i, 0)),
                  pl.BlockSpec((2, COMBINE_TILE, D_MODEL // 2), lambda i: (0, first_block + i, 0)),
                  pl.BlockSpec((COMBINE_TILE, LANES), lambda i: (i, 0)),
                  pl.BlockSpec((1, D_MODEL), lambda i: (0, 0))],
        out_specs=pl.BlockSpec((COMBINE_TILE, D_MODEL), lambda i: (i, 0)),
        out_shape=jax.ShapeDtypeStruct((n, D_MODEL), F32),
        compiler_params=pltpu.CompilerParams(
            dimension_semantics=("arbitrary",), vmem_limit_bytes=STREAM_VMEM_LIMIT),
        name="moe_combine",
    )(h, y2, route, nfin)


def _moe_rows(parts, w_gate, w_up, w_down):
    eid = jnp.concatenate([ids for _, ids in parts], axis=1).astype(jnp.int32)
    ntok = eid.shape[1]
    assert ntok % RANK_TILE == 0 and ntok % SC_WINDOW == 0
    rank, cnt = _rank_call(eid)
    counts = cnt[:, 0].astype(jnp.int32)
    padded = ((counts + ROW_TILE - 1) // ROW_TILE) * ROW_TILE
    ends = jnp.cumsum(padded)
    starts = ends - padded
    expert_iota = jnp.arange(N_EXPERTS, dtype=jnp.int32)[:, None, None]
    pos = rank + jnp.sum(jnp.where(eid[None] == expert_iota, starts[:, None, None], 0), axis=0)
    n_rows = (2 * ntok + N_EXPERTS * (ROW_TILE - 1)) // ROW_TILE * ROW_TILE
    xs = _dispatch_rows([rows for rows, _ in parts], pos, n_rows)
    ys = _expert_call(starts // ROW_TILE, padded // ROW_TILE, counts, ends[-1:] // ROW_TILE, xs,
                      w_gate, w_up, w_down)
    return _return_rows(ys, pos.reshape(-1)).reshape(2, ntok, D_MODEL // 2)


def kernel(x_prompt, x_sample, state_conv, state_gla, norm_mix, w_in, w_conv, w_alpha_up, b_alpha,
           gla_norm, w_conv_out, w_gla_out, w_o, norm_ffn, w_group_router, b_group_router,
           w_expert_router, b_expert_router, w_gate, w_up, w_down, norm_final):
    nbp, seq_p, _ = x_prompt.shape
    nbs, seq_s, _ = x_sample.shape
    assert norm_mix.shape[0] == 1, "single layer"
    assert seq_p % PROMPT_TILE == 0 and nbs % SAMPLE_SEQS == 0 and seq_s == 8

    weights = _mixer_weights(norm_mix, w_in, w_conv, w_alpha_up, b_alpha, gla_norm, w_conv_out,
                             w_gla_out, w_o, norm_ffn, w_group_router, b_group_router,
                             w_expert_router, b_expert_router)

    ntp, nts = nbp * seq_p, nbs * seq_s
    assert ntp % COMBINE_TILE == 0 and nts % COMBINE_TILE == 0
    conv0 = jnp.zeros((nbp, CONV_WIDTH - 1, D_CONV), F32)
    gla0 = jnp.zeros((nbp, 1, 8, LANES), F32)
    flat = lambda a: a.reshape(-1, a.shape[-1])
    ids_of = lambda route_t: route_t[:, 2:4, :].transpose(1, 0, 2).reshape(2, -1)
    h_p, xn_p, rt_p, rtt_p, conv_p, gla_p = _mixer_call(x_prompt, conv0, gla0, weights, prompt=True)
    h_s, xn_s, rt_s, rtt_s, conv_s, gla_s = _mixer_call(x_sample, state_conv[0], state_gla[0],
                                                        weights, prompt=False)
    y2 = _moe_rows([(flat(xn_p), ids_of(rtt_p)), (flat(xn_s), ids_of(rtt_s))],
                   w_gate[0], w_up[0], w_down[0])
    nfin = norm_final[None, :]
    y_p = _combine_call(flat(h_p), y2, flat(rt_p), nfin, 0)
    y_s = _combine_call(flat(h_s), y2, flat(rt_s), nfin, ntp)
    return (y_p.reshape(nbp, seq_p, D_MODEL), y_s.reshape(nbs, seq_s, D_MODEL),
            conv_p[None], gla_p[None], conv_s[None], gla_s[None])
```

```python
import functools

import jax
import jax.numpy as jnp
from jax import lax
from jax.experimental import pallas as pl
from jax.experimental.pallas import tpu as pltpu
from jax.experimental.pallas import tpu_sc as plsc

F32 = jnp.float32
BF16 = jnp.bfloat16

D_MODEL = 1024
D_CONV = 512
CONV_WIDTH = 3
GLA_HEADS = 4
GLA_DK = 128
GLA_DV = 256
GLA_RANK = 16
GLA_GATE_NORM = 16.0
GLA_CHUNK = 32
N_GROUPS = 4
EXPERTS_PER_GROUP = 8
N_EXPERTS = 32
D_EXPERT = 256
EPS = 1e-6

LANES = 128
V7X_VMEM_BYTES = 64 * 1024 * 1024
OFF_CB, OFF_CC, OFF_CH = 0, 512, 1024
OFF_Q, OFF_K, OFF_V, OFF_G = 1536, 2048, 2560, 3584
OFF_GA, OFF_GB, OFF_AL = 4608, 5632, 6656

PROMPT_TILE = 512
GLA_BLOCK = 256
STAGE_SKEW = 0
SAMPLE_SEQS = 32
STATE_CHUNK = 4
STATE_OUT_SLOTS = 4
ROW_TILE = 256
EXPERT_RING = 4
SC_WINDOW = 128
COMBINE_TILE = 512
RANK_TILE = 1024
RANK_BLOCK = 512
MIXER_VMEM_LIMIT = V7X_VMEM_BYTES - 6 * 1024 * 1024
STREAM_VMEM_LIMIT = MIXER_VMEM_LIMIT


def _rmsnorm(x, g):
    ms = jnp.mean(x * x, axis=-1, keepdims=True)
    return (x * lax.rsqrt(ms + EPS)) * g


def _split_bf16(x):
    hi = x.astype(BF16)
    lo = (x - hi.astype(F32)).astype(BF16)
    return hi, lo


def _pack_bf16_pairs(x):
    n = x.shape[1] // 2
    lo = lax.bitcast_convert_type(x[:, :n].astype(BF16).astype(F32), jnp.uint32)
    hi = lax.bitcast_convert_type(x[:, n:].astype(BF16).astype(F32), jnp.uint32)
    return lax.shift_right_logical(lo, jnp.uint32(16)) | hi


def _unpack_bf16_pairs(w):
    lo = lax.bitcast_convert_type(lax.shift_left(w, jnp.uint32(16)), F32)
    hi = lax.bitcast_convert_type(w & jnp.uint32(0xFFFF0000), F32)
    return lo, hi


def _dot(a, b):
    return jnp.dot(a, b, preferred_element_type=F32)


def _chunk_causal(n, chunk):
    r = lax.broadcasted_iota(jnp.int32, (n, n), 0)
    c = lax.broadcasted_iota(jnp.int32, (n, n), 1)
    shift = chunk.bit_length() - 1
    assert chunk == 1 << shift
    same = lax.shift_right_arithmetic(r, shift) == lax.shift_right_arithmetic(c, shift)
    return same & (c <= r)


def _router_logits(xn2, wr_ref, br_ref):
    xh, xl = _split_bf16(xn2)
    part = _dot(xh, wr_ref[...]) + _dot(xl, wr_ref[...])
    return part[:, :LANES] + part[:, LANES:] + br_ref[...]


def _route(lg):
    n = lg.shape[0]
    n_rows = 40
    lt = lg.T[:n_rows, :]
    row_i = lax.broadcasted_iota(jnp.int32, (n_rows, n), 0)
    row = row_i.astype(F32)
    neg = -jnp.inf
    none = float(LANES)
    is_g = row_i < N_GROUPS
    gm = jnp.max(jnp.where(is_g, lt, neg), axis=0, keepdims=True)
    gs = jnp.sum(jnp.where(is_g, jnp.exp(lt - gm), 0.0), axis=0, keepdims=True)
    g_top = 1.0 / gs
    g_idx = jnp.min(jnp.where(is_g & (lt == gm), row, none), axis=0, keepdims=True)
    e_group = lax.shift_right_arithmetic(row_i - N_GROUPS, 3).astype(F32)
    sel = (row_i >= N_GROUPS) & (row_i < N_GROUPS + N_EXPERTS) & (e_group == g_idx)
    el = jnp.where(sel, lt, neg)
    e1 = jnp.max(el, axis=0, keepdims=True)
    i1 = jnp.min(jnp.where(sel & (el == e1), row, none), axis=0, keepdims=True)
    sel2 = sel & (row != i1)
    el2 = jnp.where(sel2, lt, neg)
    e2 = jnp.max(el2, axis=0, keepdims=True)
    i2 = jnp.min(jnp.where(sel2 & (el2 == e2), row, none), axis=0, keepdims=True)
    d = jnp.exp(e2 - e1)
    w1 = g_top / (1.0 + d)
    w2 = (g_top * d) / (1.0 + d)
    id1 = i1 - float(N_GROUPS)
    id2 = i2 - float(N_GROUPS)
    out_i = lax.broadcasted_iota(jnp.int32, (8, n), 0)
    route_t = jnp.where(out_i == 0, w1, jnp.where(out_i == 1, w2, jnp.where(out_i == 2, id1,
                        jnp.where(out_i == 3, id2, 0.0))))
    route = jnp.concatenate([route_t, jnp.zeros((LANES - 8, n), F32)], axis=0).T
    return route, route_t


def _mixer_stages(sub, nsub, x_ref, conv_in_ref, gla_in_ref,
                  nmix_ref, w_ref, wgate_ref, walr_ref, wconv_ref, walpha_ref, balpha_ref, gnorm_ref,
                  wco_ref, wgo_ref, wo_ref, nffn_ref, wr_ref, br_ref,
                  h_ref, xn2_ref, route_ref, route_t_ref, conv_out_ref, gla_out_ref,
                  s_ref, ubuf_ref, *state_ring, prompt, logits_out):
    if prompt:
        rows, chunk, nseq = GLA_BLOCK, GLA_CHUNK, 1
        r0 = sub * rows
        x = x_ref[0, r0:r0 + rows, :]
    else:
        nseq, chunk = SAMPLE_SEQS, x_ref.shape[1]
        rows = nseq * chunk
        r0 = 0
        x = x_ref[...].reshape(rows, D_MODEL)
        sin_ref, sout_ref, sem_in, sem_out = state_ring
        n_state_chunks = nseq // STATE_CHUNK
        step, n_steps = pl.program_id(0), pl.num_programs(0)

        def chunk_seqs(c, of_step):
            first = of_step * nseq + c * STATE_CHUNK
            return pl.ds(pl.multiple_of(first, STATE_CHUNK), STATE_CHUNK)

        def in_copy(c, of_step):
            return pltpu.make_async_copy(gla_in_ref.at[chunk_seqs(c, of_step)], sin_ref.at[c],
                                         sem_in.at[c])

        def out_copy(c, of_step):
            slot = c % STATE_OUT_SLOTS
            return pltpu.make_async_copy(sout_ref.at[slot], gla_out_ref.at[chunk_seqs(c, of_step)],
                                         sem_out.at[slot])

        @pl.when(step == 0)
        def _():
            for c in range(n_state_chunks):
                in_copy(c, step).start()
    nchunks = rows // chunk
    last_sub = sub == nsub - 1

    xn = _rmsnorm(x, nmix_ref[...]).astype(BF16)
    yield

    def proj(off, n):
        if off == OFF_AL:
            return _dot(xn, walr_ref[...])
        if off >= OFF_GA:
            return _dot(xn, wgate_ref[:, off - OFF_GA:off - OFF_GA + n])
        return _dot(xn, w_ref[:, off:off + n])

    alr = proj(OFF_AL, LANES).astype(BF16)
    lap = _dot(alr, walpha_ref[...])
    la = jax.nn.log_sigmoid(lap + balpha_ref[...]) * (1.0 / GLA_GATE_NORM)
    la_hi, la_lo = _split_bf16(la)
    causal = _chunk_causal(rows, chunk)
    tri = jnp.where(causal, 1.0, 0.0).astype(BF16)
    yield
    cc = proj(OFF_CC, D_CONV)
    ch = proj(OFF_CH, D_CONV)
    cb = proj(OFF_CB, D_CONV)
    q = proj(OFF_Q, D_CONV)
    k = proj(OFF_K, D_CONV)
    yield
    b = _dot(tri, la_hi) + _dot(tri, la_lo)

    u = cc * ch
    wc = wconv_ref[...]
    if prompt:
        ubuf_ref[0, 8 + r0:8 + r0 + rows, :] = u
        z = (wc[0:1] * ubuf_ref[0, 6 + r0:6 + r0 + rows, :]
             + wc[1:2] * ubuf_ref[0, 7 + r0:7 + r0 + rows, :] + wc[2:3] * u)
        if last_sub:
            tail = u[rows - 2:rows, :]
            ubuf_ref[0, 6:8, :] = tail
            conv_out_ref[0] = tail
    else:
        u3 = u.reshape(nseq, chunk, D_CONV)
        ubuf_ref[:, 6:8, :] = conv_in_ref[...]
        ubuf_ref[:, 8:8 + chunk, :] = u3
        z3 = (wc[0:1] * ubuf_ref[:, 6:6 + chunk, :] + wc[1:2] * ubuf_ref[:, 7:7 + chunk, :]
              + wc[2:3] * u3)
        z = z3.reshape(rows, D_CONV)
        conv_out_ref[...] = ubuf_ref[:, 6 + chunk:8 + chunk, :]
    ya = _dot((cb * z).astype(BF16), wco_ref[...])
    v = proj(OFF_V, GLA_HEADS * GLA_DV)
    yield

    b3 = b.reshape(nchunks, chunk, D_CONV)
    bl3 = b3[:, chunk - 1:chunk, :]
    qin = (q * (GLA_DK ** -0.5)) * jnp.exp(b)
    kin = k * jnp.exp(-b)
    kst = (k.reshape(nchunks, chunk, D_CONV) * jnp.exp(bl3 - b3)).reshape(rows, D_CONV)
    pad = (-rows) % LANES
    rows_p = rows + pad

    def transposed(a):
        if pad:
            a = jnp.concatenate([a, jnp.zeros((pad, a.shape[1]), F32)], axis=0)
        return a.T

    b_t = transposed(b)
    kst_t = transposed(kst) if prompt else None
    col_chunk = lax.shift_right_arithmetic(
        lax.broadcasted_iota(jnp.int32, (GLA_DK, rows_p), 1), chunk.bit_length() - 1)
    g = proj(OFF_G, GLA_HEADS * GLA_DV)
    yield
    heads = range(GLA_HEADS)
    vhs = [v[:, hd * GLA_DV:(hd + 1) * GLA_DV].astype(BF16) for hd in heads]
    scs = [lax.dot_general(qin[:, hd * GLA_DK:(hd + 1) * GLA_DK].astype(BF16),
                           kin[:, hd * GLA_DK:(hd + 1) * GLA_DK].astype(BF16),
                           (((1,), (1,)), ((), ())), preferred_element_type=F32) for hd in heads]
    ga = proj(OFF_GA, D_MODEL)
    yield
    upds = []
    for hd in heads if prompt else ():
        kst_h = kst_t[hd * GLA_DK:(hd + 1) * GLA_DK, :]
        stacked = jnp.concatenate(
            [jnp.where(col_chunk == n, kst_h, 0.0).astype(BF16) for n in range(nchunks)], axis=0)
        vh_p = jnp.concatenate([vhs[hd], jnp.zeros((pad, GLA_DV), BF16)], axis=0) if pad else vhs[hd]
        upds.append(_dot(stacked, vh_p))
    o_intras = [_dot(jnp.where(causal, scs[hd], 0.0).astype(BF16), vhs[hd]) for hd in heads]
    gb = proj(OFF_GB, D_MODEL)
    merge_a = jax.nn.sigmoid(ga) * ya
    gate_b = jax.nn.sigmoid(gb)
    yield
    def q_block(n, hd):
        return qin[n * chunk:(n + 1) * chunk, hd * GLA_DK:(hd + 1) * GLA_DK].astype(BF16)

    def decay_col(n, hd):
        last = n * chunk + chunk - 1
        return jnp.exp(b_t[hd * GLA_DK:(hd + 1) * GLA_DK, last:last + 1])

    o_inter = [[None] * nchunks for _ in heads]
    if prompt:
        for hd in heads:
            s_run = s_ref[hd]
            for n in range(nchunks):
                o_inter[hd][n] = _dot(q_block(n, hd), s_run.astype(BF16))
                s_run = s_run * decay_col(n, hd) + upds[hd][n * GLA_DK:(n + 1) * GLA_DK, :]
            s_ref[hd] = s_run
            if last_sub:
                gla_out_ref[0, hd] = s_run
    else:
        for c in range(n_state_chunks):
            in_copy(c, step).wait()
            if c >= STATE_OUT_SLOTS:
                out_copy(c - STATE_OUT_SLOTS, step).wait()
            else:
                @pl.when(step > 0)
                def _():
                    out_copy(c + n_state_chunks - STATE_OUT_SLOTS, step - 1).wait()
            for j in range(STATE_CHUNK):
                n = c * STATE_CHUNK + j
                for hd in heads:
                    s_prev = sin_ref[c, j, hd]
                    o_inter[hd][n] = _dot(q_block(n, hd), s_prev.astype(BF16))
                    ksb = kst[n * chunk:(n + 1) * chunk, hd * GLA_DK:(hd + 1) * GLA_DK].astype(BF16)
                    vb = v[n * chunk:(n + 1) * chunk, hd * GLA_DV:(hd + 1) * GLA_DV].astype(BF16)
                    upd = lax.dot_general(ksb, vb, (((0,), (0,)), ((), ())),
                                          preferred_element_type=F32)
                    sout_ref[c % STATE_OUT_SLOTS, j, hd] = s_prev * decay_col(n, hd) + upd
            out_copy(c, step).start()

        @pl.when(step + 1 < n_steps)
        def _():
            for c in range(n_state_chunks):
                in_copy(c, step + 1).start()

        @pl.when(step + 1 == n_steps)
        def _():
            for c in range(n_state_chunks - STATE_OUT_SLOTS, n_state_chunks):
                out_copy(c, step).wait()
    gated = []
    for hd in heads:
        v0 = hd * GLA_DV
        oh = _rmsnorm(o_intras[hd] + jnp.concatenate(o_inter[hd], axis=0), gnorm_ref[...])
        gh = g[:, v0:v0 + GLA_DV]
        gated.append(oh * (gh * jax.nn.sigmoid(gh)))
    yield
    yb = _dot(jnp.concatenate(gated, axis=1).astype(BF16), wgo_ref[...])
    yield

    m = merge_a + gate_b * yb
    hh = x + _dot(m.astype(BF16), wo_ref[...])
    yield
    xn2 = _rmsnorm(hh, nffn_ref[...])
    yield
    logits_out.append(_router_logits(xn2, wr_ref, br_ref))
    if prompt:
        h_ref[0, r0:r0 + rows, :] = hh
        xn2_ref[0, r0:r0 + rows, :] = _pack_bf16_pairs(xn2)
    else:
        h_ref[...] = hh.reshape(nseq, chunk, D_MODEL)
        xn2_ref[...] = _pack_bf16_pairs(xn2).reshape(nseq, chunk, D_MODEL // 2)


def _mixer_kernel(*refs, prompt):
    n_scratch = 2 if prompt else 6
    outs = refs[len(refs) - n_scratch - 6:len(refs) - n_scratch]
    route_ref, route_t_ref = outs[2], outs[3]
    if prompt:
        s_ref, ubuf_ref = refs[-2:]

        @pl.when(pl.program_id(1) == 0)
        def _():
            s_ref[...] = jnp.zeros_like(s_ref)
            ubuf_ref[...] = jnp.zeros_like(ubuf_ref)
    nsub = PROMPT_TILE // GLA_BLOCK if prompt else 1
    logits = []
    stages = [_mixer_stages(sub, nsub, *refs, prompt=prompt, logits_out=logits)
              for sub in range(nsub)]
    done = [False] * nsub
    step = 0
    while not all(done):
        for sub, st in enumerate(stages):
            if step >= sub * STAGE_SKEW and not done[sub]:
                done[sub] = next(st, "end") == "end"
        step += 1
    route, route_t = _route(jnp.concatenate(logits, axis=0))
    if prompt:
        route_ref[0] = route
    else:
        route_ref[...] = route.reshape(route_ref.shape)
    route_t_ref[0] = route_t


def _mixer_weights(norm_mix, w_in, w_conv, w_alpha_up, b_alpha, gla_norm, w_conv_out, w_gla_out, w_o,
                   norm_ffn, w_group_router, b_group_router, w_expert_router, b_expert_router):
    wi = w_in[0]
    w_main = wi[:, :OFF_GA].astype(BF16)
    w_gates = wi[:, OFF_GA + GLA_RANK:].astype(BF16)
    w_alr = jnp.concatenate([wi[:, OFF_GA:OFF_GA + GLA_RANK],
                             jnp.zeros((D_MODEL, LANES - GLA_RANK), F32)], axis=1).astype(BF16)
    walpha = jnp.concatenate(
        [w_alpha_up[0], jnp.zeros((LANES - GLA_RANK, D_CONV), F32)], axis=0).astype(BF16)
    wr = jnp.concatenate(
        [w_group_router[0], w_expert_router[0],
         jnp.zeros((D_MODEL, LANES - N_GROUPS - N_EXPERTS), F32)], axis=1)
    wr_hi = wr.astype(BF16)
    wr_split = jnp.concatenate([wr_hi, (wr - wr_hi.astype(F32)).astype(BF16)], axis=1)
    br = jnp.concatenate([b_group_router[0], b_expert_router[0],
                          jnp.zeros((LANES - N_GROUPS - N_EXPERTS,), F32)])[None, :]
    return (norm_mix, w_main, w_gates, w_alr, w_conv[0], walpha, b_alpha, gla_norm,
            w_conv_out[0].astype(BF16), w_gla_out[0].astype(BF16), w_o[0].astype(BF16),
            norm_ffn, wr_split, br)


def _const_spec(shape):
    nd = len(shape)
    return pl.BlockSpec(shape, lambda *_: (0,) * nd, pipeline_mode=pl.Buffered(1))


def _mixer_call(x, conv_state, gla_state, weights, *, prompt):
    nb, seq, _ = x.shape
    if prompt:
        grid = (nb, seq // PROMPT_TILE)
        tok = lambda last: pl.BlockSpec((1, PROMPT_TILE, last), lambda b, i: (b, i, 0))
        conv_spec = pl.BlockSpec((1, CONV_WIDTH - 1, D_CONV), lambda b, i: (b, 0, 0))
        gla_spec = pl.BlockSpec((1, GLA_HEADS, GLA_DK, GLA_DV), lambda b, i: (b, 0, 0, 0))
        gla_in_spec = pl.BlockSpec((1, 1, 8, LANES), lambda b, i: (b, 0, 0, 0))
        scratch = [pltpu.VMEM((GLA_HEADS, GLA_DK, GLA_DV), F32),
                   pltpu.VMEM((1, PROMPT_TILE + 8, D_CONV), F32)]
        route_t_shape = (nb, 8, seq)
        route_t_spec = pl.BlockSpec((1, 8, PROMPT_TILE), lambda b, i: (b, 0, i))
    else:
        grid = (nb // SAMPLE_SEQS, 1)
        tok = lambda last: pl.BlockSpec((SAMPLE_SEQS, seq, last), lambda b, i: (b, 0, 0))
        conv_spec = pl.BlockSpec((SAMPLE_SEQS, CONV_WIDTH - 1, D_CONV), lambda b, i: (b, 0, 0))
        gla_spec = pl.BlockSpec(memory_space=pl.ANY)
        gla_in_spec = gla_spec
        state_chunk = (STATE_CHUNK, GLA_HEADS, GLA_DK, GLA_DV)
        scratch = [pltpu.VMEM((1, 8, LANES), F32),
                   pltpu.VMEM((SAMPLE_SEQS, 8 + seq, D_CONV), F32),
                   pltpu.VMEM((SAMPLE_SEQS // STATE_CHUNK,) + state_chunk, F32),
                   pltpu.VMEM((STATE_OUT_SLOTS,) + state_chunk, F32),
                   pltpu.SemaphoreType.DMA((SAMPLE_SEQS // STATE_CHUNK,)),
                   pltpu.SemaphoreType.DMA((STATE_OUT_SLOTS,))]
        route_t_shape = (nb // SAMPLE_SEQS, 8, SAMPLE_SEQS * seq)
        route_t_spec = pl.BlockSpec((1, 8, SAMPLE_SEQS * seq), lambda b, i: (b, 0, 0))
    out_shape = (jax.ShapeDtypeStruct((nb, seq, D_MODEL), F32),
                 jax.ShapeDtypeStruct((nb, seq, D_MODEL // 2), jnp.uint32),
                 jax.ShapeDtypeStruct((nb, seq, LANES), F32),
                 jax.ShapeDtypeStruct(route_t_shape, F32),
                 jax.ShapeDtypeStruct((nb, CONV_WIDTH - 1, D_CONV), F32),
                 jax.ShapeDtypeStruct((nb, GLA_HEADS, GLA_DK, GLA_DV), F32))
    return pl.pallas_call(
        functools.partial(_mixer_kernel, prompt=prompt),
        grid=grid,
        in_specs=[tok(D_MODEL), conv_spec, gla_in_spec] + [_const_spec(w.shape) for w in weights],
        out_specs=(tok(D_MODEL), tok(D_MODEL // 2), tok(LANES), route_t_spec, conv_spec, gla_spec),
        out_shape=out_shape,
        scratch_shapes=scratch,
        compiler_params=pltpu.CompilerParams(
            dimension_semantics=("arbitrary", "arbitrary"), vmem_limit_bytes=MIXER_VMEM_LIMIT),
        name="mixer_prompt" if prompt else "mixer_sample",
    )(x, conv_state, gla_state, *weights)


def _rank_kernel(eid_ref, rank_ref, cnt_ref, carry_ref):
    @pl.when(pl.program_id(0) == 0)
    def _():
        carry_ref[...] = jnp.zeros_like(carry_ref)

    n = RANK_BLOCK
    e_iota = lax.broadcasted_iota(jnp.int32, (N_EXPERTS, n), 0)
    r = lax.broadcasted_iota(jnp.int32, (n, n), 0)
    c = lax.broadcasted_iota(jnp.int32, (n, n), 1)
    before = jnp.where(r < c, 1.0, 0.0).astype(BF16)
    carry = carry_ref[...]
    for j in range(eid_ref.shape[1] // n):
        cols = slice(j * n, (j + 1) * n)
        oh0 = jnp.where(e_iota == eid_ref[0:1, cols], 1.0, 0.0)
        oh1 = jnp.where(e_iota == eid_ref[1:2, cols], 1.0, 0.0)
        cnt = oh0 + oh1
        base = carry + _dot(cnt.astype(BF16), before)
        rank0 = jnp.sum(oh0 * base, axis=0, keepdims=True)
        rank1 = jnp.sum(oh1 * base, axis=0, keepdims=True)
        rank_ref[:, cols] = jnp.concatenate([rank0, rank1], axis=0).astype(jnp.int32)
        carry = carry + jnp.sum(cnt, axis=1, keepdims=True)
    carry_ref[...] = carry
    cnt_ref[...] = jnp.broadcast_to(carry, cnt_ref.shape)


def _rank_call(eid):
    ntok = eid.shape[1]
    return pl.pallas_call(
        _rank_kernel,
        grid=(ntok // RANK_TILE,),
        in_specs=[pl.BlockSpec((2, RANK_TILE), lambda i: (0, i))],
        out_specs=(pl.BlockSpec((2, RANK_TILE), lambda i: (0, i)),
                   pl.BlockSpec((N_EXPERTS, LANES), lambda i: (0, 0))),
        out_shape=(jax.ShapeDtypeStruct((2, ntok), jnp.int32),
                   jax.ShapeDtypeStruct((N_EXPERTS, LANES), F32)),
        scratch_shapes=[pltpu.VMEM((N_EXPERTS, 1), F32)],
        compiler_params=pltpu.CompilerParams(dimension_semantics=("arbitrary",)),
        name="expert_rank",
    )(eid)


def _sc_mesh():
    return plsc.VectorSubcoreMesh(core_axis_name="c", subcore_axis_name="s")


def _sc_worker_windows(n_windows, body):
    info = plsc.get_sparse_core_info()
    n_workers = info.num_cores * info.num_subcores
    wid = lax.axis_index("s") * info.num_cores + lax.axis_index("c")

    @pl.loop(0, pl.cdiv(n_windows, n_workers))
    def _(j):
        win = wid + n_workers * j

        @pl.when(win < n_windows)
        def _():
            body(win)


def _dispatch_rows(sources, pos, n_rows):
    width, dtype = sources[0].shape[1], sources[0].dtype
    n_win = [src.shape[0] // SC_WINDOW for src in sources]
    first_win = [sum(n_win[:i]) for i in range(len(sources))]
    total_win = sum(n_win)
    pos3 = pos.reshape(2, total_win, SC_WINDOW).transpose(1, 0, 2)

    @functools.partial(pl.kernel, mesh=_sc_mesh(), name="moe_dispatch",
                       out_type=jax.ShapeDtypeStruct((n_rows, width), dtype),
                       scratch_types=[pltpu.VMEM((2, SC_WINDOW), jnp.int32),
                                      pltpu.VMEM((SC_WINDOW, width), dtype)])
    def run(*refs):
        src_hbm, (pos_hbm, out_hbm, idx_v, buf) = refs[:len(sources)], refs[len(sources):]

        def window(win):
            pltpu.sync_copy(pos_hbm.at[win], idx_v)
            for src, w0, nw in zip(src_hbm, first_win, n_win):
                @pl.when((win >= w0) & (win < w0 + nw))
                def _():
                    r0 = pl.multiple_of((win - w0) * SC_WINDOW, SC_WINDOW)
                    pltpu.sync_copy(src.at[pl.ds(r0, SC_WINDOW)], buf)

            pltpu.sync_copy(buf, out_hbm.at[idx_v.at[0]])
            pltpu.sync_copy(buf, out_hbm.at[idx_v.at[1]])

        _sc_worker_windows(total_win, window)

    return run(*sources, pos3)


def _return_rows(ys, idx):
    n_out, width = idx.shape[0], ys.shape[1]
    n_windows = n_out // SC_WINDOW

    @functools.partial(pl.kernel, mesh=_sc_mesh(), name="moe_return",
                       out_type=jax.ShapeDtypeStruct((n_out, width), ys.dtype),
                       scratch_types=[pltpu.VMEM((SC_WINDOW,), jnp.int32),
                                      pltpu.VMEM((SC_WINDOW, width), ys.dtype)])
    def run(ys_hbm, idx_hbm, out_hbm, idx_v, buf):
        def window(win):
            pltpu.sync_copy(idx_hbm.at[win], idx_v)
            pltpu.sync_copy(ys_hbm.at[idx_v], buf)
            r0 = pl.multiple_of(win * SC_WINDOW, SC_WINDOW)
            pltpu.sync_copy(buf, out_hbm.at[pl.ds(r0, SC_WINDOW)])

        _sc_worker_windows(n_windows, window)

    return run(ys, idx.reshape(n_windows, SC_WINDOW))


def _expert_kernel(first_ref, ntile_ref, nrow_ref, total_ref, xs_hbm, wg_ref, wu_ref, wd_ref, out_hbm,
                   xbuf, obuf, wg_s, wu_s, wd_s, sem_in, sem_out):
    e = pl.program_id(0)
    nt = ntile_ref[e]
    g0 = first_ref[e]
    total = total_ref[0]
    ring = EXPERT_RING

    def rows_of(g):
        return pl.ds(pl.multiple_of(g * ROW_TILE, ROW_TILE), ROW_TILE)

    def in_copy(g):
        slot = lax.rem(g, ring)
        return pltpu.make_async_copy(xs_hbm.at[rows_of(g)], xbuf.at[slot], sem_in.at[slot])

    def out_copy(g):
        slot = lax.rem(g, ring)
        return pltpu.make_async_copy(obuf.at[slot], out_hbm.at[rows_of(g)], sem_out.at[slot])

    @pl.when(e == 0)
    def _():
        for g in range(ring - 1):
            @pl.when(g < total)
            def _():
                in_copy(g).start()

    @pl.when(nt > 0)
    def _():
        wg_s[...] = wg_ref[0].astype(BF16)
        wu_s[...] = wu_ref[0].astype(BF16)
        wd_s[...] = wd_ref[0].astype(BF16)

    def tile(g, carry):
        slot = lax.rem(g, ring)
        in_copy(g).wait()

        @pl.when(g + (ring - 1) < total)
        def _():
            in_copy(g + (ring - 1)).start()

        @pl.when(g >= ring)
        def _():
            out_copy(g - ring).wait()

        n_valid = nrow_ref[e] - (g - g0) * ROW_TILE
        row = lax.broadcasted_iota(jnp.int32, (ROW_TILE, 1), 0)
        lo, hi = _unpack_bf16_pairs(jnp.where(row < n_valid, xbuf[slot], jnp.uint32(0)))
        half = D_MODEL // 2
        sub = ROW_TILE // 2
        parts = []
        for r in (0, sub):
            xl, xh = lo[r:r + sub].astype(BF16), hi[r:r + sub].astype(BF16)
            gt = _dot(xl, wg_s[:half, :]) + _dot(xh, wg_s[half:, :])
            up = _dot(xl, wu_s[:half, :]) + _dot(xh, wu_s[half:, :])
            parts.append((gt, up))
        for r, (gt, up) in zip((0, sub), parts):
            act = (gt * jax.nn.sigmoid(gt)) * up
            obuf[slot, r:r + sub, :] = _pack_bf16_pairs(_dot(act.astype(BF16), wd_s[...]))
        out_copy(g).start()
        return carry

    lax.fori_loop(g0, g0 + nt, tile, 0)

    @pl.when(e == pl.num_programs(0) - 1)
    def _():
        for k in range(1, ring + 1):
            @pl.when(total >= k)
            def _():
                out_copy(total - k).wait()


def _expert_call(first_tile, n_tiles_per_expert, n_rows_per_expert, total_tiles, xs,
                 w_gate, w_up, w_down):
    nrows = xs.shape[0]
    by_expert = lambda e, *_: (e, 0, 0)
    half_row = D_MODEL // 2
    return pl.pallas_call(
        _expert_kernel,
        grid_spec=pltpu.PrefetchScalarGridSpec(
            num_scalar_prefetch=4,
            grid=(N_EXPERTS,),
            in_specs=[pl.BlockSpec(memory_space=pl.ANY),
                      pl.BlockSpec((1, D_MODEL, D_EXPERT), by_expert),
                      pl.BlockSpec((1, D_MODEL, D_EXPERT), by_expert),
                      pl.BlockSpec((1, D_EXPERT, D_MODEL), by_expert)],
            out_specs=pl.BlockSpec(memory_space=pl.ANY),
            scratch_shapes=[pltpu.VMEM((EXPERT_RING, ROW_TILE, half_row), jnp.uint32),
                            pltpu.VMEM((EXPERT_RING, ROW_TILE, half_row), jnp.uint32),
                            pltpu.VMEM((D_MODEL, D_EXPERT), BF16),
                            pltpu.VMEM((D_MODEL, D_EXPERT), BF16),
                            pltpu.VMEM((D_EXPERT, D_MODEL), BF16),
                            pltpu.SemaphoreType.DMA((EXPERT_RING,)),
                            pltpu.SemaphoreType.DMA((EXPERT_RING,))]),
        out_shape=jax.ShapeDtypeStruct((nrows, half_row), jnp.uint32),
        compiler_params=pltpu.CompilerParams(
            dimension_semantics=("arbitrary",), vmem_limit_bytes=STREAM_VMEM_LIMIT),
        name="expert_mlp",
    )(first_tile, n_tiles_per_expert, n_rows_per_expert, total_tiles, xs, w_gate, w_up, w_down)


def _combine_kernel(h_ref, y2_ref, route_ref, nfin_ref, out_ref):
    route = route_ref[...]
    w1, w2 = route[:, 0:1], route[:, 1:2]
    lo1, hi1 = _unpack_bf16_pairs(y2_ref[0])
    lo2, hi2 = _unpack_bf16_pairs(y2_ref[1])
    y = h_ref[...] + jnp.concatenate([w1 * lo1 + w2 * lo2, w1 * hi1 + w2 * hi2], axis=1)
    out_ref[...] = _rmsnorm(y, nfin_ref[...])


def _combine_call(h, y2, route, nfin, first_token):
    n = h.shape[0]
    first_block = first_token // COMBINE_TILE
    return pl.pallas_call(
        _combine_kernel,
        grid=(n // COMBINE_TILE,),
        in_specs=[pl.BlockSpec((COMBINE_TILE, D_MODEL), lambda i: (i, 0)),
                  pl.BlockSpec((2, COMBINE_TILE, D_MODEL // 2), lambda i: (0, first_block + i, 0)),
                  pl.BlockSpec((COMBINE_TILE, LANES), lambda i: (i, 0)),
                  pl.BlockSpec((1, D_MODEL), lambda i: (0, 0))],
        out_specs=pl.BlockSpec((COMBINE_TILE, D_MODEL), lambda i: (i, 0)),
        out_shape=jax.ShapeDtypeStruct((n, D_MODEL), F32),
        compiler_params=pltpu.CompilerParams(
            dimension_semantics=("arbitrary",), vmem_limit_bytes=STREAM_VMEM_LIMIT),
        name="moe_combine",
    )(h, y2, route, nfin)


def _moe_rows(parts, w_gate, w_up, w_down):
    eid = jnp.concatenate([ids for _, ids in parts], axis=1).astype(jnp.int32)
    ntok = eid.shape[1]
    assert ntok % RANK_TILE == 0 and ntok % SC_WINDOW == 0
    rank, cnt = _rank_call(eid)
    counts = cnt[:, 0].astype(jnp.int32)
    padded = ((counts + ROW_TILE - 1) // ROW_TILE) * ROW_TILE
    ends = jnp.cumsum(padded)
    starts = ends - padded
    expert_iota = jnp.arange(N_EXPERTS, dtype=jnp.int32)[:, None, None]
    pos = rank + jnp.sum(jnp.where(eid[None] == expert_iota, starts[:, None, None], 0), axis=0)
    n_rows = (2 * ntok + N_EXPERTS * (ROW_TILE - 1)) // ROW_TILE * ROW_TILE
    xs = _dispatch_rows([rows for rows, _ in parts], pos, n_rows)
    ys = _expert_call(starts // ROW_TILE, padded // ROW_TILE, counts, ends[-1:] // ROW_TILE, xs,
                      w_gate, w_up, w_down)
    return _return_rows(ys, pos.reshape(-1)).reshape(2, ntok, D_MODEL // 2)


def kernel(x_prompt, x_sample, state_conv, state_gla, norm_mix, w_in, w_conv, w_alpha_up, b_alpha,
           gla_norm, w_conv_out, w_gla_out, w_o, norm_ffn, w_group_router, b_group_router,
           w_expert_router, b_expert_router, w_gate, w_up, w_down, norm_final):
    nbp, seq_p, _ = x_prompt.shape
    nbs, seq_s, _ = x_sample.shape
    assert norm_mix.shape[0] == 1, "single layer"
    assert seq_p % PROMPT_TILE == 0 and nbs % SAMPLE_SEQS == 0 and seq_s == 8

    weights = _mixer_weights(norm_mix, w_in, w_conv, w_alpha_up, b_alpha, gla_norm, w_conv_out,
                             w_gla_out, w_o, norm_ffn, w_group_router, b_group_router,
                             w_expert_router, b_expert_router)

    ntp, nts = nbp * seq_p, nbs * seq_s
    assert ntp % COMBINE_TILE == 0 and nts % COMBINE_TILE == 0
    conv0 = jnp.zeros((nbp, CONV_WIDTH - 1, D_CONV), F32)
    gla0 = jnp.zeros((nbp, 1, 8, LANES), F32)
    flat = lambda a: a.reshape(-1, a.shape[-1])
    ids_of = lambda route_t: route_t[:, 2:4, :].transpose(1, 0, 2).reshape(2, -1)
    h_p, xn_p, rt_p, rtt_p, conv_p, gla_p = _mixer_call(x_prompt, conv0, gla0, weights, prompt=True)
    h_s, xn_s, rt_s, rtt_s, conv_s, gla_s = _mixer_call(x_sample, state_conv[0], state_gla[0],
                                                        weights, prompt=False)
    y2 = _moe_rows([(flat(xn_p), ids_of(rtt_p)), (flat(xn_s), ids_of(rtt_s))],
                   w_gate[0], w_up[0], w_down[0])
    nfin = norm_final[None, :]
    y_p = _combine_call(flat(h_p), y2, flat(rt_p), nfin, 0)
    y_s = _combine_call(flat(h_s), y2, flat(rt_s), nfin, ntp)
    return (y_p.reshape(nbp, seq_p, D_MODEL), y_s.reshape(nbs, seq_s, D_MODEL),
            conv_p[None], gla_p[None], conv_s[None], gla_s[None])
```

```python
import functools

import jax
import jax.numpy as jnp
from jax import lax
from jax.experimental import pallas as pl
from jax.experimental.pallas import tpu as pltpu
from jax.experimental.pallas import tpu_sc as plsc

F32 = jnp.float32
BF16 = jnp.bfloat16

D_MODEL = 1024
D_CONV = 512
CONV_WIDTH = 3
GLA_HEADS = 4
GLA_DK = 128
GLA_DV = 256
GLA_RANK = 16
GLA_GATE_NORM = 16.0
GLA_CHUNK = 32
N_GROUPS = 4
EXPERTS_PER_GROUP = 8
N_EXPERTS = 32
D_EXPERT = 256
EPS = 1e-6

LANES = 128
V7X_VMEM_BYTES = 64 * 1024 * 1024
OFF_CB, OFF_CC, OFF_CH = 0, 512, 1024
OFF_Q, OFF_K, OFF_V, OFF_G = 1536, 2048, 2560, 3584
OFF_GA, OFF_GB, OFF_AL = 4608, 5632, 6656

PROMPT_TILE = 512
GLA_BLOCK = 256
STAGE_SKEW = 0
SAMPLE_SEQS = 32
STATE_CHUNK = 4
STATE_OUT_SLOTS = 4
ROW_TILE = 256
EXPERT_RING = 6
EXPERT_AHEAD = 4
SC_WINDOW = 128
COMBINE_TILE = 512
RANK_TILE = 1024
RANK_BLOCK = 512
MIXER_VMEM_LIMIT = V7X_VMEM_BYTES - 6 * 1024 * 1024
STREAM_VMEM_LIMIT = MIXER_VMEM_LIMIT


def _rmsnorm(x, g):
    ms = jnp.mean(x * x, axis=-1, keepdims=True)
    return (x * lax.rsqrt(ms + EPS)) * g


def _split_bf16(x):
    hi = x.astype(BF16)
    lo = (x - hi.astype(F32)).astype(BF16)
    return hi, lo


def _pack_bf16_pairs(x):
    n = x.shape[1] // 2
    lo = lax.bitcast_convert_type(x[:, :n].astype(BF16).astype(F32), jnp.uint32)
    hi = lax.bitcast_convert_type(x[:, n:].astype(BF16).astype(F32), jnp.uint32)
    return lax.shift_right_logical(lo, jnp.uint32(16)) | hi


def _unpack_bf16_pairs(w):
    lo = lax.bitcast_convert_type(lax.shift_left(w, jnp.uint32(16)), F32)
    hi = lax.bitcast_convert_type(w & jnp.uint32(0xFFFF0000), F32)
    return lo, hi


def _dot(a, b):
    return jnp.dot(a, b, preferred_element_type=F32)


def _chunk_causal(n, chunk):
    r = lax.broadcasted_iota(jnp.int32, (n, n), 0)
    c = lax.broadcasted_iota(jnp.int32, (n, n), 1)
    shift = chunk.bit_length() - 1
    assert chunk == 1 << shift
    same = lax.shift_right_arithmetic(r, shift) == lax.shift_right_arithmetic(c, shift)
    return same & (c <= r)


def _router_logits(xn2, wr_ref, br_ref):
    xh, xl = _split_bf16(xn2)
    part = _dot(xh, wr_ref[...]) + _dot(xl, wr_ref[...])
    return part[:, :LANES] + part[:, LANES:] + br_ref[...]


def _route(lg):
    n = lg.shape[0]
    n_rows = 40
    lt = lg.T[:n_rows, :]
    row_i = lax.broadcasted_iota(jnp.int32, (n_rows, n), 0)
    row = row_i.astype(F32)
    neg = -jnp.inf
    none = float(LANES)
    is_g = row_i < N_GROUPS
    gm = jnp.max(jnp.where(is_g, lt, neg), axis=0, keepdims=True)
    gs = jnp.sum(jnp.where(is_g, jnp.exp(lt - gm), 0.0), axis=0, keepdims=True)
    g_top = 1.0 / gs
    g_idx = jnp.min(jnp.where(is_g & (lt == gm), row, none), axis=0, keepdims=True)
    e_group = lax.shift_right_arithmetic(row_i - N_GROUPS, 3).astype(F32)
    sel = (row_i >= N_GROUPS) & (row_i < N_GROUPS + N_EXPERTS) & (e_group == g_idx)
    el = jnp.where(sel, lt, neg)
    e1 = jnp.max(el, axis=0, keepdims=True)
    i1 = jnp.min(jnp.where(sel & (el == e1), row, none), axis=0, keepdims=True)
    sel2 = sel & (row != i1)
    el2 = jnp.where(sel2, lt, neg)
    e2 = jnp.max(el2, axis=0, keepdims=True)
    i2 = jnp.min(jnp.where(sel2 & (el2 == e2), row, none), axis=0, keepdims=True)
    d = jnp.exp(e2 - e1)
    w1 = g_top / (1.0 + d)
    w2 = (g_top * d) / (1.0 + d)
    id1 = i1 - float(N_GROUPS)
    id2 = i2 - float(N_GROUPS)
    out_i = lax.broadcasted_iota(jnp.int32, (8, n), 0)
    route_t = jnp.where(out_i == 0, w1, jnp.where(out_i == 1, w2, jnp.where(out_i == 2, id1,
                        jnp.where(out_i == 3, id2, 0.0))))
    route = jnp.concatenate([route_t, jnp.zeros((LANES - 8, n), F32)], axis=0).T
    return route, route_t


def _mixer_stages(sub, nsub, x_ref, conv_in_ref, gla_in_ref,
                  nmix_ref, w_ref, wgate_ref, walr_ref, wconv_ref, walpha_ref, balpha_ref, gnorm_ref,
                  wco_ref, wgo_ref, wo_ref, nffn_ref, wr_ref, br_ref,
                  h_ref, xn2_ref, route_ref, route_t_ref, conv_out_ref, gla_out_ref,
                  s_ref, ubuf_ref, *state_ring, prompt, logits_out):
    if prompt:
        rows, chunk, nseq = GLA_BLOCK, GLA_CHUNK, 1
        r0 = sub * rows
        x = x_ref[0, r0:r0 + rows, :]
    else:
        nseq, chunk = SAMPLE_SEQS, x_ref.shape[1]
        rows = nseq * chunk
        r0 = 0
        x = x_ref[...].reshape(rows, D_MODEL)
        sin_ref, sout_ref, sem_in, sem_out = state_ring
        n_state_chunks = nseq // STATE_CHUNK
        step, n_steps = pl.program_id(0), pl.num_programs(0)

        def chunk_seqs(c, of_step):
            first = of_step * nseq + c * STATE_CHUNK
            return pl.ds(pl.multiple_of(first, STATE_CHUNK), STATE_CHUNK)

        def in_copy(c, of_step):
            return pltpu.make_async_copy(gla_in_ref.at[chunk_seqs(c, of_step)], sin_ref.at[c],
                                         sem_in.at[c])

        def out_copy(c, of_step):
            slot = c % STATE_OUT_SLOTS
            return pltpu.make_async_copy(sout_ref.at[slot], gla_out_ref.at[chunk_seqs(c, of_step)],
                                         sem_out.at[slot])

        @pl.when(step == 0)
        def _():
            for c in range(n_state_chunks):
                in_copy(c, step).start()
    nchunks = rows // chunk
    last_sub = sub == nsub - 1

    xn = _rmsnorm(x, nmix_ref[...]).astype(BF16)
    yield

    def proj(off, n):
        if off == OFF_AL:
            return _dot(xn, walr_ref[...])
        if off >= OFF_GA:
            return _dot(xn, wgate_ref[:, off - OFF_GA:off - OFF_GA + n])
        return _dot(xn, w_ref[:, off:off + n])

    alr = proj(OFF_AL, LANES).astype(BF16)
    lap = _dot(alr, walpha_ref[...])
    la = jax.nn.log_sigmoid(lap + balpha_ref[...]) * (1.0 / GLA_GATE_NORM)
    la_hi, la_lo = _split_bf16(la)
    causal = _chunk_causal(rows, chunk)
    tri = jnp.where(causal, 1.0, 0.0).astype(BF16)
    yield
    cc = proj(OFF_CC, D_CONV)
    ch = proj(OFF_CH, D_CONV)
    cb = proj(OFF_CB, D_CONV)
    q = proj(OFF_Q, D_CONV)
    k = proj(OFF_K, D_CONV)
    yield
    b = _dot(tri, la_hi) + _dot(tri, la_lo)

    u = cc * ch
    wc = wconv_ref[...]
    if prompt:
        ubuf_ref[0, 8 + r0:8 + r0 + rows, :] = u
        z = (wc[0:1] * ubuf_ref[0, 6 + r0:6 + r0 + rows, :]
             + wc[1:2] * ubuf_ref[0, 7 + r0:7 + r0 + rows, :] + wc[2:3] * u)
        if last_sub:
            tail = u[rows - 2:rows, :]
            ubuf_ref[0, 6:8, :] = tail
            conv_out_ref[0] = tail
    else:
        u3 = u.reshape(nseq, chunk, D_CONV)
        ubuf_ref[:, 6:8, :] = conv_in_ref[...]
        ubuf_ref[:, 8:8 + chunk, :] = u3
        z3 = (wc[0:1] * ubuf_ref[:, 6:6 + chunk, :] + wc[1:2] * ubuf_ref[:, 7:7 + chunk, :]
              + wc[2:3] * u3)
        z = z3.reshape(rows, D_CONV)
        conv_out_ref[...] = ubuf_ref[:, 6 + chunk:8 + chunk, :]
    ya = _dot((cb * z).astype(BF16), wco_ref[...])
    v = proj(OFF_V, GLA_HEADS * GLA_DV)
    yield

    b3 = b.reshape(nchunks, chunk, D_CONV)
    bl3 = b3[:, chunk - 1:chunk, :]
    qin = (q * (GLA_DK ** -0.5)) * jnp.exp(b)
    kin = k * jnp.exp(-b)
    kst = (k.reshape(nchunks, chunk, D_CONV) * jnp.exp(bl3 - b3)).reshape(rows, D_CONV)
    pad = (-rows) % LANES
    rows_p = rows + pad

    def transposed(a):
        if pad:
            a = jnp.concatenate([a, jnp.zeros((pad, a.shape[1]), F32)], axis=0)
        return a.T

    b_t = transposed(b)
    kst_t = transposed(kst) if prompt else None
    col_chunk = lax.shift_right_arithmetic(
        lax.broadcasted_iota(jnp.int32, (GLA_DK, rows_p), 1), chunk.bit_length() - 1)
    g = proj(OFF_G, GLA_HEADS * GLA_DV)
    yield
    heads = range(GLA_HEADS)
    vhs = [v[:, hd * GLA_DV:(hd + 1) * GLA_DV].astype(BF16) for hd in heads]
    scs = [lax.dot_general(qin[:, hd * GLA_DK:(hd + 1) * GLA_DK].astype(BF16),
                           kin[:, hd * GLA_DK:(hd + 1) * GLA_DK].astype(BF16),
                           (((1,), (1,)), ((), ())), preferred_element_type=F32) for hd in heads]
    ga = proj(OFF_GA, D_MODEL)
    yield
    upds = []
    for hd in heads if prompt else ():
        kst_h = kst_t[hd * GLA_DK:(hd + 1) * GLA_DK, :]
        stacked = jnp.concatenate(
            [jnp.where(col_chunk == n, kst_h, 0.0).astype(BF16) for n in range(nchunks)], axis=0)
        vh_p = jnp.concatenate([vhs[hd], jnp.zeros((pad, GLA_DV), BF16)], axis=0) if pad else vhs[hd]
        upds.append(_dot(stacked, vh_p))
    o_intras = [_dot(jnp.where(causal, scs[hd], 0.0).astype(BF16), vhs[hd]) for hd in heads]
    gb = proj(OFF_GB, D_MODEL)
    merge_a = jax.nn.sigmoid(ga) * ya
    gate_b = jax.nn.sigmoid(gb)
    yield
    def q_block(n, hd):
        return qin[n * chunk:(n + 1) * chunk, hd * GLA_DK:(hd + 1) * GLA_DK].astype(BF16)

    def decay_col(n, hd):
        last = n * chunk + chunk - 1
        return jnp.exp(b_t[hd * GLA_DK:(hd + 1) * GLA_DK, last:last + 1])

    o_inter = [[None] * nchunks for _ in heads]
    if prompt:
        for hd in heads:
            s_run = s_ref[hd]
            for n in range(nchunks):
                o_inter[hd][n] = _dot(q_block(n, hd), s_run.astype(BF16))
                s_run = s_run * decay_col(n, hd) + upds[hd][n * GLA_DK:(n + 1) * GLA_DK, :]
            s_ref[hd] = s_run
            if last_sub:
                gla_out_ref[0, hd] = s_run
    else:
        for c in range(n_state_chunks):
            in_copy(c, step).wait()
            if c >= STATE_OUT_SLOTS:
                out_copy(c - STATE_OUT_SLOTS, step).wait()
            else:
                @pl.when(step > 0)
                def _():
                    out_copy(c + n_state_chunks - STATE_OUT_SLOTS, step - 1).wait()
            for j in range(STATE_CHUNK):
                n = c * STATE_CHUNK + j
                for hd in heads:
                    s_prev = sin_ref[c, j, hd]
                    o_inter[hd][n] = _dot(q_block(n, hd), s_prev.astype(BF16))
                    ksb = kst[n * chunk:(n + 1) * chunk, hd * GLA_DK:(hd + 1) * GLA_DK].astype(BF16)
                    vb = v[n * chunk:(n + 1) * chunk, hd * GLA_DV:(hd + 1) * GLA_DV].astype(BF16)
                    upd = lax.dot_general(ksb, vb, (((0,), (0,)), ((), ())),
                                          preferred_element_type=F32)
                    sout_ref[c % STATE_OUT_SLOTS, j, hd] = s_prev * decay_col(n, hd) + upd
            out_copy(c, step).start()

        @pl.when(step + 1 < n_steps)
        def _():
            for c in range(n_state_chunks):
                in_copy(c, step + 1).start()

        @pl.when(step + 1 == n_steps)
        def _():
            for c in range(n_state_chunks - STATE_OUT_SLOTS, n_state_chunks):
                out_copy(c, step).wait()
    gated = []
    for hd in heads:
        v0 = hd * GLA_DV
        oh = _rmsnorm(o_intras[hd] + jnp.concatenate(o_inter[hd], axis=0), gnorm_ref[...])
        gh = g[:, v0:v0 + GLA_DV]
        gated.append(oh * (gh * jax.nn.sigmoid(gh)))
    yield
    yb = _dot(jnp.concatenate(gated, axis=1).astype(BF16), wgo_ref[...])
    yield

    m = merge_a + gate_b * yb
    hh = x + _dot(m.astype(BF16), wo_ref[...])
    yield
    xn2 = _rmsnorm(hh, nffn_ref[...])
    yield
    logits_out.append(_router_logits(xn2, wr_ref, br_ref))
    if prompt:
        h_ref[0, r0:r0 + rows, :] = hh
        xn2_ref[0, r0:r0 + rows, :] = _pack_bf16_pairs(xn2)
    else:
        h_ref[...] = hh.reshape(nseq, chunk, D_MODEL)
        xn2_ref[...] = _pack_bf16_pairs(xn2).reshape(nseq, chunk, D_MODEL // 2)


def _mixer_kernel(*refs, prompt):
    n_scratch = 2 if prompt else 6
    outs = refs[len(refs) - n_scratch - 6:len(refs) - n_scratch]
    route_ref, route_t_ref = outs[2], outs[3]
    if prompt:
        s_ref, ubuf_ref = refs[-2:]

        @pl.when(pl.program_id(1) == 0)
        def _():
            s_ref[...] = jnp.zeros_like(s_ref)
            ubuf_ref[...] = jnp.zeros_like(ubuf_ref)
    nsub = PROMPT_TILE // GLA_BLOCK if prompt else 1
    logits = []
    stages = [_mixer_stages(sub, nsub, *refs, prompt=prompt, logits_out=logits)
              for sub in range(nsub)]
    done = [False] * nsub
    step = 0
    while not all(done):
        for sub, st in enumerate(stages):
            if step >= sub * STAGE_SKEW and not done[sub]:
                done[sub] = next(st, "end") == "end"
        step += 1
    route, route_t = _route(jnp.concatenate(logits, axis=0))
    if prompt:
        route_ref[0] = route
    else:
        route_ref[...] = route.reshape(route_ref.shape)
    route_t_ref[0] = route_t


def _mixer_weights(norm_mix, w_in, w_conv, w_alpha_up, b_alpha, gla_norm, w_conv_out, w_gla_out, w_o,
                   norm_ffn, w_group_router, b_group_router, w_expert_router, b_expert_router):
    wi = w_in[0]
    w_main = wi[:, :OFF_GA].astype(BF16)
    w_gates = wi[:, OFF_GA + GLA_RANK:].astype(BF16)
    w_alr = jnp.concatenate([wi[:, OFF_GA:OFF_GA + GLA_RANK],
                             jnp.zeros((D_MODEL, LANES - GLA_RANK), F32)], axis=1).astype(BF16)
    walpha = jnp.concatenate(
        [w_alpha_up[0], jnp.zeros((LANES - GLA_RANK, D_CONV), F32)], axis=0).astype(BF16)
    wr = jnp.concatenate(
        [w_group_router[0], w_expert_router[0],
         jnp.zeros((D_MODEL, LANES - N_GROUPS - N_EXPERTS), F32)], axis=1)
    wr_hi = wr.astype(BF16)
    wr_split = jnp.concatenate([wr_hi, (wr - wr_hi.astype(F32)).astype(BF16)], axis=1)
    br = jnp.concatenate([b_group_router[0], b_expert_router[0],
                          jnp.zeros((LANES - N_GROUPS - N_EXPERTS,), F32)])[None, :]
    return (norm_mix, w_main, w_gates, w_alr, w_conv[0], walpha, b_alpha, gla_norm,
            w_conv_out[0].astype(BF16), w_gla_out[0].astype(BF16), w_o[0].astype(BF16),
            norm_ffn, wr_split, br)


def _const_spec(shape):
    nd = len(shape)
    return pl.BlockSpec(shape, lambda *_: (0,) * nd, pipeline_mode=pl.Buffered(1))


def _mixer_call(x, conv_state, gla_state, weights, *, prompt):
    nb, seq, _ = x.shape
    if prompt:
        grid = (nb, seq // PROMPT_TILE)
        tok = lambda last: pl.BlockSpec((1, PROMPT_TILE, last), lambda b, i: (b, i, 0))
        conv_spec = pl.BlockSpec((1, CONV_WIDTH - 1, D_CONV), lambda b, i: (b, 0, 0))
        gla_spec = pl.BlockSpec((1, GLA_HEADS, GLA_DK, GLA_DV), lambda b, i: (b, 0, 0, 0))
        gla_in_spec = pl.BlockSpec((1, 1, 8, LANES), lambda b, i: (b, 0, 0, 0))
        scratch = [pltpu.VMEM((GLA_HEADS, GLA_DK, GLA_DV), F32),
                   pltpu.VMEM((1, PROMPT_TILE + 8, D_CONV), F32)]
        route_t_shape = (nb, 8, seq)
        route_t_spec = pl.BlockSpec((1, 8, PROMPT_TILE), lambda b, i: (b, 0, i))
    else:
        grid = (nb // SAMPLE_SEQS, 1)
        tok = lambda last: pl.BlockSpec((SAMPLE_SEQS, seq, last), lambda b, i: (b, 0, 0))
        conv_spec = pl.BlockSpec((SAMPLE_SEQS, CONV_WIDTH - 1, D_CONV), lambda b, i: (b, 0, 0))
        gla_spec = pl.BlockSpec(memory_space=pl.ANY)
        gla_in_spec = gla_spec
        state_chunk = (STATE_CHUNK, GLA_HEADS, GLA_DK, GLA_DV)
        scratch = [pltpu.VMEM((1, 8, LANES), F32),
                   pltpu.VMEM((SAMPLE_SEQS, 8 + seq, D_CONV), F32),
                   pltpu.VMEM((SAMPLE_SEQS // STATE_CHUNK,) + state_chunk, F32),
                   pltpu.VMEM((STATE_OUT_SLOTS,) + state_chunk, F32),
                   pltpu.SemaphoreType.DMA((SAMPLE_SEQS // STATE_CHUNK,)),
                   pltpu.SemaphoreType.DMA((STATE_OUT_SLOTS,))]
        route_t_shape = (nb // SAMPLE_SEQS, 8, SAMPLE_SEQS * seq)
        route_t_spec = pl.BlockSpec((1, 8, SAMPLE_SEQS * seq), lambda b, i: (b, 0, 0))
    out_shape = (jax.ShapeDtypeStruct((nb, seq, D_MODEL), F32),
                 jax.ShapeDtypeStruct((nb, seq, D_MODEL // 2), jnp.uint32),
                 jax.ShapeDtypeStruct((nb, seq, LANES), F32),
                 jax.ShapeDtypeStruct(route_t_shape, F32),
                 jax.ShapeDtypeStruct((nb, CONV_WIDTH - 1, D_CONV), F32),
                 jax.ShapeDtypeStruct((nb, GLA_HEADS, GLA_DK, GLA_DV), F32))
    return pl.pallas_call(
        functools.partial(_mixer_kernel, prompt=prompt),
        grid=grid,
        in_specs=[tok(D_MODEL), conv_spec, gla_in_spec] + [_const_spec(w.shape) for w in weights],
        out_specs=(tok(D_MODEL), tok(D_MODEL // 2), tok(LANES), route_t_spec, conv_spec, gla_spec),
        out_shape=out_shape,
        scratch_shapes=scratch,
        compiler_params=pltpu.CompilerParams(
            dimension_semantics=("arbitrary", "arbitrary"), vmem_limit_bytes=MIXER_VMEM_LIMIT),
        name="mixer_prompt" if prompt else "mixer_sample",
    )(x, conv_state, gla_state, *weights)


def _rank_kernel(eid_ref, rank_ref, cnt_ref, carry_ref):
    @pl.when(pl.program_id(0) == 0)
    def _():
        carry_ref[...] = jnp.zeros_like(carry_ref)

    n = RANK_BLOCK
    e_iota = lax.broadcasted_iota(jnp.int32, (N_EXPERTS, n), 0)
    r = lax.broadcasted_iota(jnp.int32, (n, n), 0)
    c = lax.broadcasted_iota(jnp.int32, (n, n), 1)
    before = jnp.where(r < c, 1.0, 0.0).astype(BF16)
    carry = carry_ref[...]
    for j in range(eid_ref.shape[1] // n):
        cols = slice(j * n, (j + 1) * n)
        oh0 = jnp.where(e_iota == eid_ref[0:1, cols], 1.0, 0.0)
        oh1 = jnp.where(e_iota == eid_ref[1:2, cols], 1.0, 0.0)
        cnt = oh0 + oh1
        base = carry + _dot(cnt.astype(BF16), before)
        rank0 = jnp.sum(oh0 * base, axis=0, keepdims=True)
        rank1 = jnp.sum(oh1 * base, axis=0, keepdims=True)
        rank_ref[:, cols] = jnp.concatenate([rank0, rank1], axis=0).astype(jnp.int32)
        carry = carry + jnp.sum(cnt, axis=1, keepdims=True)
    carry_ref[...] = carry
    cnt_ref[...] = jnp.broadcast_to(carry, cnt_ref.shape)


def _rank_call(eid):
    ntok = eid.shape[1]
    return pl.pallas_call(
        _rank_kernel,
        grid=(ntok // RANK_TILE,),
        in_specs=[pl.BlockSpec((2, RANK_TILE), lambda i: (0, i))],
        out_specs=(pl.BlockSpec((2, RANK_TILE), lambda i: (0, i)),
                   pl.BlockSpec((N_EXPERTS, LANES), lambda i: (0, 0))),
        out_shape=(jax.ShapeDtypeStruct((2, ntok), jnp.int32),
                   jax.ShapeDtypeStruct((N_EXPERTS, LANES), F32)),
        scratch_shapes=[pltpu.VMEM((N_EXPERTS, 1), F32)],
        compiler_params=pltpu.CompilerParams(dimension_semantics=("arbitrary",)),
        name="expert_rank",
    )(eid)


def _sc_mesh():
    return plsc.VectorSubcoreMesh(core_axis_name="c", subcore_axis_name="s")


def _sc_worker_windows(n_windows, body):
    info = plsc.get_sparse_core_info()
    n_workers = info.num_cores * info.num_subcores
    wid = lax.axis_index("s") * info.num_cores + lax.axis_index("c")

    @pl.loop(0, pl.cdiv(n_windows, n_workers))
    def _(j):
        win = wid + n_workers * j

        @pl.when(win < n_windows)
        def _():
            body(win)


def _dispatch_rows(sources, pos, n_rows):
    width, dtype = sources[0].shape[1], sources[0].dtype
    n_win = [src.shape[0] // SC_WINDOW for src in sources]
    first_win = [sum(n_win[:i]) for i in range(len(sources))]
    total_win = sum(n_win)
    pos3 = pos.reshape(2, total_win, SC_WINDOW).transpose(1, 0, 2)

    @functools.partial(pl.kernel, mesh=_sc_mesh(), name="moe_dispatch",
                       out_type=jax.ShapeDtypeStruct((n_rows, width), dtype),
                       scratch_types=[pltpu.VMEM((2, SC_WINDOW), jnp.int32),
                                      pltpu.VMEM((SC_WINDOW, width), dtype)])
    def run(*refs):
        src_hbm, (pos_hbm, out_hbm, idx_v, buf) = refs[:len(sources)], refs[len(sources):]

        def window(win):
            pltpu.sync_copy(pos_hbm.at[win], idx_v)
            for src, w0, nw in zip(src_hbm, first_win, n_win):
                @pl.when((win >= w0) & (win < w0 + nw))
                def _():
                    r0 = pl.multiple_of((win - w0) * SC_WINDOW, SC_WINDOW)
                    pltpu.sync_copy(src.at[pl.ds(r0, SC_WINDOW)], buf)

            pltpu.sync_copy(buf, out_hbm.at[idx_v.at[0]])
            pltpu.sync_copy(buf, out_hbm.at[idx_v.at[1]])

        _sc_worker_windows(total_win, window)

    return run(*sources, pos3)


def _return_rows(ys, idx):
    n_out, width = idx.shape[0], ys.shape[1]
    n_windows = n_out // SC_WINDOW

    @functools.partial(pl.kernel, mesh=_sc_mesh(), name="moe_return",
                       out_type=jax.ShapeDtypeStruct((n_out, width), ys.dtype),
                       scratch_types=[pltpu.VMEM((SC_WINDOW,), jnp.int32),
                                      pltpu.VMEM((SC_WINDOW, width), ys.dtype)])
    def run(ys_hbm, idx_hbm, out_hbm, idx_v, buf):
        def window(win):
            pltpu.sync_copy(idx_hbm.at[win], idx_v)
            pltpu.sync_copy(ys_hbm.at[idx_v], buf)
            r0 = pl.multiple_of(win * SC_WINDOW, SC_WINDOW)
            pltpu.sync_copy(buf, out_hbm.at[pl.ds(r0, SC_WINDOW)])

        _sc_worker_windows(n_windows, window)

    return run(ys, idx.reshape(n_windows, SC_WINDOW))


def _expert_kernel(first_ref, ntile_ref, nrow_ref, total_ref, xs_hbm, wg_ref, wu_ref, wd_ref, out_hbm,
                   xbuf, obuf, wg_s, wu_s, wd_s, sem_in, sem_out):
    e = pl.program_id(0)
    nt = ntile_ref[e]
    g0 = first_ref[e]
    total = total_ref[0]
    ring = EXPERT_RING

    def rows_of(g):
        return pl.ds(pl.multiple_of(g * ROW_TILE, ROW_TILE), ROW_TILE)

    def in_copy(g):
        slot = lax.rem(g, ring)
        return pltpu.make_async_copy(xs_hbm.at[rows_of(g)], xbuf.at[slot], sem_in.at[slot])

    def out_copy(g):
        slot = lax.rem(g, ring)
        return pltpu.make_async_copy(obuf.at[slot], out_hbm.at[rows_of(g)], sem_out.at[slot])

    @pl.when(e == 0)
    def _():
        for g in range(EXPERT_AHEAD):
            @pl.when(g < total)
            def _():
                in_copy(g).start()

    @pl.when(nt > 0)
    def _():
        wg_s[...] = wg_ref[0].astype(BF16)
        wu_s[...] = wu_ref[0].astype(BF16)
        wd_s[...] = wd_ref[0].astype(BF16)

    half = D_MODEL // 2
    sub = ROW_TILE // 2

    def tiles(gs):
        halves = []
        for g in gs:
            slot = lax.rem(g, ring)
            in_copy(g).wait()

            @pl.when(g + EXPERT_AHEAD < total)
            def _():
                in_copy(g + EXPERT_AHEAD).start()

            @pl.when(g >= ring)
            def _():
                out_copy(g - ring).wait()

            n_valid = nrow_ref[e] - (g - g0) * ROW_TILE
            row = lax.broadcasted_iota(jnp.int32, (ROW_TILE, 1), 0)
            lo, hi = _unpack_bf16_pairs(jnp.where(row < n_valid, xbuf[slot], jnp.uint32(0)))
            for r in (0, sub):
                xl, xh = lo[r:r + sub].astype(BF16), hi[r:r + sub].astype(BF16)
                gt = _dot(xl, wg_s[:half, :]) + _dot(xh, wg_s[half:, :])
                up = _dot(xl, wu_s[:half, :]) + _dot(xh, wu_s[half:, :])
                halves.append((slot, r, gt, up))
        for slot, r, gt, up in halves:
            act = (gt * jax.nn.sigmoid(gt)) * up
            obuf[slot, r:r + sub, :] = _pack_bf16_pairs(_dot(act.astype(BF16), wd_s[...]))
        for g in gs:
            out_copy(g).start()

    def pair(i, carry):
        tiles([g0 + 2 * i, g0 + 2 * i + 1])
        return carry

    lax.fori_loop(0, nt // 2, pair, 0)

    @pl.when(lax.rem(nt, 2) == 1)
    def _():
        tiles([g0 + nt - 1])

    @pl.when(e == pl.num_programs(0) - 1)
    def _():
        for k in range(1, ring + 1):
            @pl.when(total >= k)
            def _():
                out_copy(total - k).wait()


def _expert_call(first_tile, n_tiles_per_expert, n_rows_per_expert, total_tiles, xs,
                 w_gate, w_up, w_down):
    nrows = xs.shape[0]
    by_expert = lambda e, *_: (e, 0, 0)
    half_row = D_MODEL // 2
    return pl.pallas_call(
        _expert_kernel,
        grid_spec=pltpu.PrefetchScalarGridSpec(
            num_scalar_prefetch=4,
            grid=(N_EXPERTS,),
            in_specs=[pl.BlockSpec(memory_space=pl.ANY),
                      pl.BlockSpec((1, D_MODEL, D_EXPERT), by_expert),
                      pl.BlockSpec((1, D_MODEL, D_EXPERT), by_expert),
                      pl.BlockSpec((1, D_EXPERT, D_MODEL), by_expert)],
            out_specs=pl.BlockSpec(memory_space=pl.ANY),
            scratch_shapes=[pltpu.VMEM((EXPERT_RING, ROW_TILE, half_row), jnp.uint32),
                            pltpu.VMEM((EXPERT_RING, ROW_TILE, half_row), jnp.uint32),
                            pltpu.VMEM((D_MODEL, D_EXPERT), BF16),
                            pltpu.VMEM((D_MODEL, D_EXPERT), BF16),
                            pltpu.VMEM((D_EXPERT, D_MODEL), BF16),
                            pltpu.SemaphoreType.DMA((EXPERT_RING,)),
                            pltpu.SemaphoreType.DMA((EXPERT_RING,))]),
        out_shape=jax.ShapeDtypeStruct((nrows, half_row), jnp.uint32),
        compiler_params=pltpu.CompilerParams(
            dimension_semantics=("arbitrary",), vmem_limit_bytes=STREAM_VMEM_LIMIT),
        name="expert_mlp",
    )(first_tile, n_tiles_per_expert, n_rows_per_expert, total_tiles, xs, w_gate, w_up, w_down)


def _combine_kernel(h_ref, y2_ref, route_ref, nfin_ref, out_ref):
    route = route_ref[...]
    w1, w2 = route[:, 0:1], route[:, 1:2]
    lo1, hi1 = _unpack_bf16_pairs(y2_ref[0])
    lo2, hi2 = _unpack_bf16_pairs(y2_ref[1])
    y = h_ref[...] + jnp.concatenate([w1 * lo1 + w2 * lo2, w1 * hi1 + w2 * hi2], axis=1)
    out_ref[...] = _rmsnorm(y, nfin_ref[...])


def _combine_call(h, y2, route, nfin, first_token):
    n = h.shape[0]
    first_block = first_token // COMBINE_TILE
    return pl.pallas_call(
        _combine_kernel,
        grid=(n // COMBINE_TILE,),
        in_specs=[pl.BlockSpec((COMBINE_TILE, D_MODEL), lambda i: (i, 0)),
                  pl.BlockSpec((2, COMBINE_TILE, D_MODEL // 2), lambda i: (0, first_block + i, 0)),
                  pl.BlockSpec((COMBINE_TILE, LANES), lambda i: (i, 0)),
                  pl.BlockSpec((1, D_MODEL), lambda i: (0, 0))],
        out_specs=pl.BlockSpec((COMBINE_TILE, D_MODEL), lambda i: (i, 0)),
        out_shape=jax.ShapeDtypeStruct((n, D_MODEL), F32),
        compiler_params=pltpu.CompilerParams(
            dimension_semantics=("arbitrary",), vmem_limit_bytes=STREAM_VMEM_LIMIT),
        name="moe_combine",
    )(h, y2, route, nfin)


def _moe_rows(parts, w_gate, w_up, w_down):
    eid = jnp.concatenate([ids for _, ids in parts], axis=1).astype(jnp.int32)
    ntok = eid.shape[1]
    assert ntok % RANK_TILE == 0 and ntok % SC_WINDOW == 0
    rank, cnt = _rank_call(eid)
    counts = cnt[:, 0].astype(jnp.int32)
    padded = ((counts + ROW_TILE - 1) // ROW_TILE) * ROW_TILE
    ends = jnp.cumsum(padded)
    starts = ends - padded
    expert_iota = jnp.arange(N_EXPERTS, dtype=jnp.int32)[:, None, None]
    pos = rank + jnp.sum(jnp.where(eid[None] == expert_iota, starts[:, None, None], 0), axis=0)
    n_rows = (2 * ntok + N_EXPERTS * (ROW_TILE - 1)) // ROW_TILE * ROW_TILE
    xs = _dispatch_rows([rows for rows, _ in parts], pos, n_rows)
    ys = _expert_call(starts // ROW_TILE, padded // ROW_TILE, counts, ends[-1:] // ROW_TILE, xs,
                      w_gate, w_up, w_down)
    return _return_rows(ys, pos.reshape(-1)).reshape(2, ntok, D_MODEL // 2)


def kernel(x_prompt, x_sample, state_conv, state_gla, norm_mix, w_in, w_conv, w_alpha_up, b_alpha,
           gla_norm, w_conv_out, w_gla_out, w_o, norm_ffn, w_group_router, b_group_router,
           w_expert_router, b_expert_router, w_gate, w_up, w_down, norm_final):
    nbp, seq_p, _ = x_prompt.shape
    nbs, seq_s, _ = x_sample.shape
    assert norm_mix.shape[0] == 1, "single layer"
    assert seq_p % PROMPT_TILE == 0 and nbs % SAMPLE_SEQS == 0 and seq_s == 8

    weights = _mixer_weights(norm_mix, w_in, w_conv, w_alpha_up, b_alpha, gla_norm, w_conv_out,
                             w_gla_out, w_o, norm_ffn, w_group_router, b_group_router,
                             w_expert_router, b_expert_router)

    ntp, nts = nbp * seq_p, nbs * seq_s
    assert ntp % COMBINE_TILE == 0 and nts % COMBINE_TILE == 0
    conv0 = jnp.zeros((nbp, CONV_WIDTH - 1, D_CONV), F32)
    gla0 = jnp.zeros((nbp, 1, 8, LANES), F32)
    flat = lambda a: a.reshape(-1, a.shape[-1])
    ids_of = lambda route_t: route_t[:, 2:4, :].transpose(1, 0, 2).reshape(2, -1)
    h_p, xn_p, rt_p, rtt_p, conv_p, gla_p = _mixer_call(x_prompt, conv0, gla0, weights, prompt=True)
    h_s, xn_s, rt_s, rtt_s, conv_s, gla_s = _mixer_call(x_sample, state_conv[0], state_gla[0],
                                                        weights, prompt=False)
    y2 = _moe_rows([(flat(xn_p), ids_of(rtt_p)), (flat(xn_s), ids_of(rtt_s))],
                   w_gate[0], w_up[0], w_down[0])
    nfin = norm_final[None, :]
    y_p = _combine_call(flat(h_p), y2, flat(rt_p), nfin, 0)
    y_s = _combine_call(flat(h_s), y2, flat(rt_s), nfin, ntp)
    return (y_p.reshape(nbp, seq_p, D_MODEL), y_s.reshape(nbs, seq_s, D_MODEL),
            conv_p[None], gla_p[None], conv_s[None], gla_s[None])
```

```python
import functools
import itertools

import jax
import jax.numpy as jnp
from jax import lax
from jax.experimental import pallas as pl
from jax.experimental.pallas import tpu as pltpu
from jax.experimental.pallas import tpu_sc as plsc

F32 = jnp.float32
BF16 = jnp.bfloat16

D_MODEL = 1024
D_CONV = 512
CONV_WIDTH = 3
GLA_HEADS = 4
GLA_DK = 128
GLA_DV = 256
GLA_RANK = 16
GLA_GATE_NORM = 16.0
GLA_CHUNK = 32
N_GROUPS = 4
EXPERTS_PER_GROUP = 8
N_EXPERTS = 32
D_EXPERT = 256
EPS = 1e-6

LANES = 128
V7X_VMEM_BYTES = 64 * 1024 * 1024
OFF_CB, OFF_CC, OFF_CH = 0, 512, 1024
OFF_Q, OFF_K, OFF_V, OFF_G = 1536, 2048, 2560, 3584
OFF_GA, OFF_GB, OFF_AL = 4608, 5632, 6656

PROMPT_TILE = 512
GLA_BLOCK = 256
SAMPLE_SEQS = 32
STATE_CHUNK = 4
STATE_OUT_SLOTS = 4
ROW_TILE = 256
EXPERT_RING = 6
EXPERT_AHEAD = 4
SC_WINDOW = 128
COMBINE_TILE = 512
RANK_TILE = 1024
RANK_BLOCK = 512
VMEM_HEADROOM_BYTES = 6 * 1024 * 1024
MIXER_VMEM_LIMIT = V7X_VMEM_BYTES - VMEM_HEADROOM_BYTES
STREAM_VMEM_LIMIT = MIXER_VMEM_LIMIT


def _rmsnorm(x, g):
    ms = jnp.mean(x * x, axis=-1, keepdims=True)
    return (x * lax.rsqrt(ms + EPS)) * g


def _split_bf16(x):
    hi = x.astype(BF16)
    lo = (x - hi.astype(F32)).astype(BF16)
    return hi, lo


def _pack_bf16_pairs(x):
    n = x.shape[1] // 2
    lo = lax.bitcast_convert_type(x[:, :n].astype(BF16).astype(F32), jnp.uint32)
    hi = lax.bitcast_convert_type(x[:, n:].astype(BF16).astype(F32), jnp.uint32)
    return lax.shift_right_logical(lo, jnp.uint32(16)) | hi


def _unpack_bf16_pairs(w):
    lo = lax.bitcast_convert_type(lax.shift_left(w, jnp.uint32(16)), F32)
    hi = lax.bitcast_convert_type(w & jnp.uint32(0xFFFF0000), F32)
    return lo, hi


def _dot(a, b):
    return jnp.dot(a, b, preferred_element_type=F32)


def _chunk_causal(n, chunk):
    r = lax.broadcasted_iota(jnp.int32, (n, n), 0)
    c = lax.broadcasted_iota(jnp.int32, (n, n), 1)
    shift = chunk.bit_length() - 1
    assert chunk == 1 << shift
    same = lax.shift_right_arithmetic(r, shift) == lax.shift_right_arithmetic(c, shift)
    return same & (c <= r)


def _router_logits(xn2, wr_ref, br_ref):
    xh, xl = _split_bf16(xn2)
    part = _dot(xh, wr_ref[...]) + _dot(xl, wr_ref[...])
    return part[:, :LANES] + part[:, LANES:] + br_ref[...]


def _route(lg):
    n = lg.shape[0]
    n_rows = 40
    lt = lg.T[:n_rows, :]
    row_i = lax.broadcasted_iota(jnp.int32, (n_rows, n), 0)
    row = row_i.astype(F32)
    neg = -jnp.inf
    none = float(LANES)
    is_g = row_i < N_GROUPS
    gm = jnp.max(jnp.where(is_g, lt, neg), axis=0, keepdims=True)
    gs = jnp.sum(jnp.where(is_g, jnp.exp(lt - gm), 0.0), axis=0, keepdims=True)
    g_top = 1.0 / gs
    g_idx = jnp.min(jnp.where(is_g & (lt == gm), row, none), axis=0, keepdims=True)
    e_group = lax.shift_right_arithmetic(row_i - N_GROUPS, 3).astype(F32)
    sel = (row_i >= N_GROUPS) & (row_i < N_GROUPS + N_EXPERTS) & (e_group == g_idx)
    el = jnp.where(sel, lt, neg)
    e1 = jnp.max(el, axis=0, keepdims=True)
    i1 = jnp.min(jnp.where(sel & (el == e1), row, none), axis=0, keepdims=True)
    sel2 = sel & (row != i1)
    el2 = jnp.where(sel2, lt, neg)
    e2 = jnp.max(el2, axis=0, keepdims=True)
    i2 = jnp.min(jnp.where(sel2 & (el2 == e2), row, none), axis=0, keepdims=True)
    d = jnp.exp(e2 - e1)
    w1 = g_top / (1.0 + d)
    w2 = (g_top * d) / (1.0 + d)
    id1 = i1 - float(N_GROUPS)
    id2 = i2 - float(N_GROUPS)
    out_i = lax.broadcasted_iota(jnp.int32, (8, n), 0)
    route_t = jnp.where(out_i == 0, w1, jnp.where(out_i == 1, w2, jnp.where(out_i == 2, id1,
                        jnp.where(out_i == 3, id2, 0.0))))
    route = jnp.concatenate([route_t, jnp.zeros((LANES - 8, n), F32)], axis=0).T
    return route, route_t


def _mixer_stages(sub, nsub, x_ref, conv_in_ref, gla_in_ref,
                  nmix_ref, w_ref, wgate_ref, walr_ref, wconv_ref, walpha_ref, balpha_ref, gnorm_ref,
                  wco_ref, wgo_ref, wo_ref, nffn_ref, wr_ref, br_ref,
                  h_ref, xn2_ref, route_ref, route_t_ref, conv_out_ref, gla_out_ref,
                  s_ref, ubuf_ref, *state_ring, prompt, logits_out):
    if prompt:
        rows, chunk, nseq = GLA_BLOCK, GLA_CHUNK, 1
        r0 = sub * rows
        x = x_ref[0, r0:r0 + rows, :]
    else:
        nseq, chunk = SAMPLE_SEQS, x_ref.shape[1]
        rows = nseq * chunk
        r0 = 0
        x = x_ref[...].reshape(rows, D_MODEL)
        sin_ref, sout_ref, sem_in, sem_out = state_ring
        n_state_chunks = nseq // STATE_CHUNK
        step, n_steps = pl.program_id(0), pl.num_programs(0)

        def chunk_seqs(c, of_step):
            first = of_step * nseq + c * STATE_CHUNK
            return pl.ds(pl.multiple_of(first, STATE_CHUNK), STATE_CHUNK)

        def in_copy(c, of_step):
            return pltpu.make_async_copy(gla_in_ref.at[chunk_seqs(c, of_step)], sin_ref.at[c],
                                         sem_in.at[c])

        def out_copy(c, of_step):
            slot = c % STATE_OUT_SLOTS
            return pltpu.make_async_copy(sout_ref.at[slot], gla_out_ref.at[chunk_seqs(c, of_step)],
                                         sem_out.at[slot])

        @pl.when(step == 0)
        def _():
            for c in range(n_state_chunks):
                in_copy(c, step).start()
    nchunks = rows // chunk
    last_sub = sub == nsub - 1

    xn = _rmsnorm(x, nmix_ref[...]).astype(BF16)
    yield

    def proj(off, n):
        if off == OFF_AL:
            return _dot(xn, walr_ref[...])
        if off >= OFF_GA:
            return _dot(xn, wgate_ref[:, off - OFF_GA:off - OFF_GA + n])
        return _dot(xn, w_ref[:, off:off + n])

    alr = proj(OFF_AL, LANES).astype(BF16)
    lap = _dot(alr, walpha_ref[...])
    la = jax.nn.log_sigmoid(lap + balpha_ref[...]) * (1.0 / GLA_GATE_NORM)
    la_hi, la_lo = _split_bf16(la)
    causal = _chunk_causal(rows, chunk)
    tri = jnp.where(causal, 1.0, 0.0).astype(BF16)
    yield
    cc = proj(OFF_CC, D_CONV)
    ch = proj(OFF_CH, D_CONV)
    cb = proj(OFF_CB, D_CONV)
    q = proj(OFF_Q, D_CONV)
    k = proj(OFF_K, D_CONV)
    yield
    b = _dot(tri, la_hi) + _dot(tri, la_lo)

    u = cc * ch
    wc = wconv_ref[...]
    if prompt:
        ubuf_ref[0, 8 + r0:8 + r0 + rows, :] = u
        z = (wc[0:1] * ubuf_ref[0, 6 + r0:6 + r0 + rows, :]
             + wc[1:2] * ubuf_ref[0, 7 + r0:7 + r0 + rows, :] + wc[2:3] * u)
        if last_sub:
            tail = u[rows - 2:rows, :]
            ubuf_ref[0, 6:8, :] = tail
            conv_out_ref[0] = tail
    else:
        u3 = u.reshape(nseq, chunk, D_CONV)
        ubuf_ref[:, 6:8, :] = conv_in_ref[...]
        ubuf_ref[:, 8:8 + chunk, :] = u3
        z3 = (wc[0:1] * ubuf_ref[:, 6:6 + chunk, :] + wc[1:2] * ubuf_ref[:, 7:7 + chunk, :]
              + wc[2:3] * u3)
        z = z3.reshape(rows, D_CONV)
        conv_out_ref[...] = ubuf_ref[:, 6 + chunk:8 + chunk, :]
    ya = _dot((cb * z).astype(BF16), wco_ref[...])
    v = proj(OFF_V, GLA_HEADS * GLA_DV)
    yield

    b3 = b.reshape(nchunks, chunk, D_CONV)
    bl3 = b3[:, chunk - 1:chunk, :]
    qin = (q * (GLA_DK ** -0.5)) * jnp.exp(b)
    kin = k * jnp.exp(-b)
    kst = (k.reshape(nchunks, chunk, D_CONV) * jnp.exp(bl3 - b3)).reshape(rows, D_CONV)
    pad = (-rows) % LANES
    rows_p = rows + pad

    def transposed(a):
        if pad:
            a = jnp.concatenate([a, jnp.zeros((pad, a.shape[1]), F32)], axis=0)
        return a.T

    b_t = transposed(b)
    kst_t = transposed(kst) if prompt else None
    col_chunk = lax.shift_right_arithmetic(
        lax.broadcasted_iota(jnp.int32, (GLA_DK, rows_p), 1), chunk.bit_length() - 1)
    g = proj(OFF_G, GLA_HEADS * GLA_DV)
    yield
    heads = range(GLA_HEADS)
    vhs = [v[:, hd * GLA_DV:(hd + 1) * GLA_DV].astype(BF16) for hd in heads]
    scs = [lax.dot_general(qin[:, hd * GLA_DK:(hd + 1) * GLA_DK].astype(BF16),
                           kin[:, hd * GLA_DK:(hd + 1) * GLA_DK].astype(BF16),
                           (((1,), (1,)), ((), ())), preferred_element_type=F32) for hd in heads]
    ga = proj(OFF_GA, D_MODEL)
    yield
    upds = []
    for hd in heads if prompt else ():
        kst_h = kst_t[hd * GLA_DK:(hd + 1) * GLA_DK, :]
        stacked = jnp.concatenate(
            [jnp.where(col_chunk == n, kst_h, 0.0).astype(BF16) for n in range(nchunks)], axis=0)
        vh_p = jnp.concatenate([vhs[hd], jnp.zeros((pad, GLA_DV), BF16)], axis=0) if pad else vhs[hd]
        upds.append(_dot(stacked, vh_p))
    o_intras = [_dot(jnp.where(causal, scs[hd], 0.0).astype(BF16), vhs[hd]) for hd in heads]
    gb = proj(OFF_GB, D_MODEL)
    merge_a = jax.nn.sigmoid(ga) * ya
    gate_b = jax.nn.sigmoid(gb)
    yield
    def q_block(n, hd):
        return qin[n * chunk:(n + 1) * chunk, hd * GLA_DK:(hd + 1) * GLA_DK].astype(BF16)

    def decay_col(n, hd):
        last = n * chunk + chunk - 1
        return jnp.exp(b_t[hd * GLA_DK:(hd + 1) * GLA_DK, last:last + 1])

    o_inter = [[None] * nchunks for _ in heads]
    if prompt:
        for hd in heads:
            s_run = s_ref[hd]
            for n in range(nchunks):
                o_inter[hd][n] = _dot(q_block(n, hd), s_run.astype(BF16))
                s_run = s_run * decay_col(n, hd) + upds[hd][n * GLA_DK:(n + 1) * GLA_DK, :]
            s_ref[hd] = s_run
            if last_sub:
                gla_out_ref[0, hd] = s_run
    else:
        for c in range(n_state_chunks):
            in_copy(c, step).wait()
            if c >= STATE_OUT_SLOTS:
                out_copy(c - STATE_OUT_SLOTS, step).wait()
            else:
                @pl.when(step > 0)
                def _():
                    out_copy(c + n_state_chunks - STATE_OUT_SLOTS, step - 1).wait()
            for j in range(STATE_CHUNK):
                n = c * STATE_CHUNK + j
                for hd in heads:
                    s_prev = sin_ref[c, j, hd]
                    o_inter[hd][n] = _dot(q_block(n, hd), s_prev.astype(BF16))
                    ksb = kst[n * chunk:(n + 1) * chunk, hd * GLA_DK:(hd + 1) * GLA_DK].astype(BF16)
                    vb = v[n * chunk:(n + 1) * chunk, hd * GLA_DV:(hd + 1) * GLA_DV].astype(BF16)
                    upd = lax.dot_general(ksb, vb, (((0,), (0,)), ((), ())),
                                          preferred_element_type=F32)
                    sout_ref[c % STATE_OUT_SLOTS, j, hd] = s_prev * decay_col(n, hd) + upd
            out_copy(c, step).start()

        @pl.when(step + 1 < n_steps)
        def _():
            for c in range(n_state_chunks):
                in_copy(c, step + 1).start()

        @pl.when(step + 1 == n_steps)
        def _():
            for c in range(n_state_chunks - STATE_OUT_SLOTS, n_state_chunks):
                out_copy(c, step).wait()
    gated = []
    for hd in heads:
        v0 = hd * GLA_DV
        oh = _rmsnorm(o_intras[hd] + jnp.concatenate(o_inter[hd], axis=0), gnorm_ref[...])
        gh = g[:, v0:v0 + GLA_DV]
        gated.append(oh * (gh * jax.nn.sigmoid(gh)))
    yield
    yb = _dot(jnp.concatenate(gated, axis=1).astype(BF16), wgo_ref[...])
    yield

    m = merge_a + gate_b * yb
    hh = x + _dot(m.astype(BF16), wo_ref[...])
    yield
    xn2 = _rmsnorm(hh, nffn_ref[...])
    yield
    logits_out.append(_router_logits(xn2, wr_ref, br_ref))
    if prompt:
        h_ref[0, r0:r0 + rows, :] = hh
        xn2_ref[0, r0:r0 + rows, :] = _pack_bf16_pairs(xn2)
    else:
        h_ref[...] = hh.reshape(nseq, chunk, D_MODEL)
        xn2_ref[...] = _pack_bf16_pairs(xn2).reshape(nseq, chunk, D_MODEL // 2)


def _mixer_kernel(*refs, prompt):
    n_scratch = 2 if prompt else 6
    outs = refs[len(refs) - n_scratch - 6:len(refs) - n_scratch]
    route_ref, route_t_ref = outs[2], outs[3]
    if prompt:
        s_ref, ubuf_ref = refs[-2:]

        @pl.when(pl.program_id(1) == 0)
        def _():
            s_ref[...] = jnp.zeros_like(s_ref)
            ubuf_ref[...] = jnp.zeros_like(ubuf_ref)
    nsub = PROMPT_TILE // GLA_BLOCK if prompt else 1
    logits = []
    stages = [_mixer_stages(sub, nsub, *refs, prompt=prompt, logits_out=logits)
              for sub in range(nsub)]
    for _ in itertools.zip_longest(*stages):
        pass
    route, route_t = _route(jnp.concatenate(logits, axis=0))
    if prompt:
        route_ref[0] = route
    else:
        route_ref[...] = route.reshape(route_ref.shape)
    route_t_ref[0] = route_t


def _mixer_weights(norm_mix, w_in, w_conv, w_alpha_up, b_alpha, gla_norm, w_conv_out, w_gla_out, w_o,
                   norm_ffn, w_group_router, b_group_router, w_expert_router, b_expert_router):
    wi = w_in[0]
    w_main = wi[:, :OFF_GA].astype(BF16)
    w_gates = wi[:, OFF_GA + GLA_RANK:].astype(BF16)
    w_alr = jnp.concatenate([wi[:, OFF_GA:OFF_GA + GLA_RANK],
                             jnp.zeros((D_MODEL, LANES - GLA_RANK), F32)], axis=1).astype(BF16)
    walpha = jnp.concatenate(
        [w_alpha_up[0], jnp.zeros((LANES - GLA_RANK, D_CONV), F32)], axis=0).astype(BF16)
    wr = jnp.concatenate(
        [w_group_router[0], w_expert_router[0],
         jnp.zeros((D_MODEL, LANES - N_GROUPS - N_EXPERTS), F32)], axis=1)
    wr_hi = wr.astype(BF16)
    wr_split = jnp.concatenate([wr_hi, (wr - wr_hi.astype(F32)).astype(BF16)], axis=1)
    br = jnp.concatenate([b_group_router[0], b_expert_router[0],
                          jnp.zeros((LANES - N_GROUPS - N_EXPERTS,), F32)])[None, :]
    return (norm_mix, w_main, w_gates, w_alr, w_conv[0], walpha, b_alpha, gla_norm,
            w_conv_out[0].astype(BF16), w_gla_out[0].astype(BF16), w_o[0].astype(BF16),
            norm_ffn, wr_split, br)


def _const_spec(shape):
    nd = len(shape)
    return pl.BlockSpec(shape, lambda *_: (0,) * nd, pipeline_mode=pl.Buffered(1))


def _mixer_call(x, conv_state, gla_state, weights, *, prompt):
    nb, seq, _ = x.shape
    if prompt:
        grid = (nb, seq // PROMPT_TILE)
        tok = lambda last: pl.BlockSpec((1, PROMPT_TILE, last), lambda b, i: (b, i, 0))
        conv_spec = pl.BlockSpec((1, CONV_WIDTH - 1, D_CONV), lambda b, i: (b, 0, 0))
        gla_spec = pl.BlockSpec((1, GLA_HEADS, GLA_DK, GLA_DV), lambda b, i: (b, 0, 0, 0))
        gla_in_spec = pl.BlockSpec((1, 1, 8, LANES), lambda b, i: (b, 0, 0, 0))
        scratch = [pltpu.VMEM((GLA_HEADS, GLA_DK, GLA_DV), F32),
                   pltpu.VMEM((1, PROMPT_TILE + 8, D_CONV), F32)]
        route_t_shape = (nb, 8, seq)
        route_t_spec = pl.BlockSpec((1, 8, PROMPT_TILE), lambda b, i: (b, 0, i))
    else:
        grid = (nb // SAMPLE_SEQS, 1)
        tok = lambda last: pl.BlockSpec((SAMPLE_SEQS, seq, last), lambda b, i: (b, 0, 0))
        conv_spec = pl.BlockSpec((SAMPLE_SEQS, CONV_WIDTH - 1, D_CONV), lambda b, i: (b, 0, 0))
        gla_spec = pl.BlockSpec(memory_space=pl.ANY)
        gla_in_spec = gla_spec
        state_chunk = (STATE_CHUNK, GLA_HEADS, GLA_DK, GLA_DV)
        scratch = [pltpu.VMEM((1, 8, LANES), F32),
                   pltpu.VMEM((SAMPLE_SEQS, 8 + seq, D_CONV), F32),
                   pltpu.VMEM((SAMPLE_SEQS // STATE_CHUNK,) + state_chunk, F32),
                   pltpu.VMEM((STATE_OUT_SLOTS,) + state_chunk, F32),
                   pltpu.SemaphoreType.DMA((SAMPLE_SEQS // STATE_CHUNK,)),
                   pltpu.SemaphoreType.DMA((STATE_OUT_SLOTS,))]
        route_t_shape = (nb // SAMPLE_SEQS, 8, SAMPLE_SEQS * seq)
        route_t_spec = pl.BlockSpec((1, 8, SAMPLE_SEQS * seq), lambda b, i: (b, 0, 0))
    out_shape = (jax.ShapeDtypeStruct((nb, seq, D_MODEL), F32),
                 jax.ShapeDtypeStruct((nb, seq, D_MODEL // 2), jnp.uint32),
                 jax.ShapeDtypeStruct((nb, seq, LANES), F32),
                 jax.ShapeDtypeStruct(route_t_shape, F32),
                 jax.ShapeDtypeStruct((nb, CONV_WIDTH - 1, D_CONV), F32),
                 jax.ShapeDtypeStruct((nb, GLA_HEADS, GLA_DK, GLA_DV), F32))
    return pl.pallas_call(
        functools.partial(_mixer_kernel, prompt=prompt),
        grid=grid,
        in_specs=[tok(D_MODEL), conv_spec, gla_in_spec] + [_const_spec(w.shape) for w in weights],
        out_specs=(tok(D_MODEL), tok(D_MODEL // 2), tok(LANES), route_t_spec, conv_spec, gla_spec),
        out_shape=out_shape,
        scratch_shapes=scratch,
        compiler_params=pltpu.CompilerParams(
            dimension_semantics=("arbitrary", "arbitrary"), vmem_limit_bytes=MIXER_VMEM_LIMIT),
        name="mixer_prompt" if prompt else "mixer_sample",
    )(x, conv_state, gla_state, *weights)


def _rank_kernel(eid_ref, rank_ref, cnt_ref, carry_ref):
    @pl.when(pl.program_id(0) == 0)
    def _():
        carry_ref[...] = jnp.zeros_like(carry_ref)

    n = RANK_BLOCK
    e_iota = lax.broadcasted_iota(jnp.int32, (N_EXPERTS, n), 0)
    r = lax.broadcasted_iota(jnp.int32, (n, n), 0)
    c = lax.broadcasted_iota(jnp.int32, (n, n), 1)
    before = jnp.where(r < c, 1.0, 0.0).astype(BF16)
    carry = carry_ref[...]
    for j in range(eid_ref.shape[1] // n):
        cols = slice(j * n, (j + 1) * n)
        oh0 = jnp.where(e_iota == eid_ref[0:1, cols], 1.0, 0.0)
        oh1 = jnp.where(e_iota == eid_ref[1:2, cols], 1.0, 0.0)
        cnt = oh0 + oh1
        base = carry + _dot(cnt.astype(BF16), before)
        rank0 = jnp.sum(oh0 * base, axis=0, keepdims=True)
        rank1 = jnp.sum(oh1 * base, axis=0, keepdims=True)
        rank_ref[:, cols] = jnp.concatenate([rank0, rank1], axis=0).astype(jnp.int32)
        carry = carry + jnp.sum(cnt, axis=1, keepdims=True)
    carry_ref[...] = carry
    cnt_ref[...] = jnp.broadcast_to(carry, cnt_ref.shape)


def _rank_call(eid):
    ntok = eid.shape[1]
    return pl.pallas_call(
        _rank_kernel,
        grid=(ntok // RANK_TILE,),
        in_specs=[pl.BlockSpec((2, RANK_TILE), lambda i: (0, i))],
        out_specs=(pl.BlockSpec((2, RANK_TILE), lambda i: (0, i)),
                   pl.BlockSpec((N_EXPERTS, LANES), lambda i: (0, 0))),
        out_shape=(jax.ShapeDtypeStruct((2, ntok), jnp.int32),
                   jax.ShapeDtypeStruct((N_EXPERTS, LANES), F32)),
        scratch_shapes=[pltpu.VMEM((N_EXPERTS, 1), F32)],
        compiler_params=pltpu.CompilerParams(dimension_semantics=("arbitrary",)),
        name="expert_rank",
    )(eid)


def _sc_mesh():
    return plsc.VectorSubcoreMesh(core_axis_name="c", subcore_axis_name="s")


def _sc_worker_windows(n_windows, body):
    info = plsc.get_sparse_core_info()
    n_workers = info.num_cores * info.num_subcores
    wid = lax.axis_index("s") * info.num_cores + lax.axis_index("c")

    @pl.loop(0, pl.cdiv(n_windows, n_workers))
    def _(j):
        win = wid + n_workers * j

        @pl.when(win < n_windows)
        def _():
            body(win)


def _dispatch_rows(sources, pos, n_rows):
    width, dtype = sources[0].shape[1], sources[0].dtype
    n_win = [src.shape[0] // SC_WINDOW for src in sources]
    first_win = [sum(n_win[:i]) for i in range(len(sources))]
    total_win = sum(n_win)
    pos3 = pos.reshape(2, total_win, SC_WINDOW).transpose(1, 0, 2)

    @functools.partial(pl.kernel, mesh=_sc_mesh(), name="moe_dispatch",
                       out_type=jax.ShapeDtypeStruct((n_rows, width), dtype),
                       scratch_types=[pltpu.VMEM((2, SC_WINDOW), jnp.int32),
                                      pltpu.VMEM((SC_WINDOW, width), dtype)])
    def run(*refs):
        src_hbm, (pos_hbm, out_hbm, idx_v, buf) = refs[:len(sources)], refs[len(sources):]

        def window(win):
            pltpu.sync_copy(pos_hbm.at[win], idx_v)
            for src, w0, nw in zip(src_hbm, first_win, n_win):
                @pl.when((win >= w0) & (win < w0 + nw))
                def _():
                    r0 = pl.multiple_of((win - w0) * SC_WINDOW, SC_WINDOW)
                    pltpu.sync_copy(src.at[pl.ds(r0, SC_WINDOW)], buf)

            pltpu.sync_copy(buf, out_hbm.at[idx_v.at[0]])
            pltpu.sync_copy(buf, out_hbm.at[idx_v.at[1]])

        _sc_worker_windows(total_win, window)

    return run(*sources, pos3)


def _return_rows(ys, idx):
    n_out, width = idx.shape[0], ys.shape[1]
    n_windows = n_out // SC_WINDOW

    @functools.partial(pl.kernel, mesh=_sc_mesh(), name="moe_return",
                       out_type=jax.ShapeDtypeStruct((n_out, width), ys.dtype),
                       scratch_types=[pltpu.VMEM((SC_WINDOW,), jnp.int32),
                                      pltpu.VMEM((SC_WINDOW, width), ys.dtype)])
    def run(ys_hbm, idx_hbm, out_hbm, idx_v, buf):
        def window(win):
            pltpu.sync_copy(idx_hbm.at[win], idx_v)
            pltpu.sync_copy(ys_hbm.at[idx_v], buf)
            r0 = pl.multiple_of(win * SC_WINDOW, SC_WINDOW)
            pltpu.sync_copy(buf, out_hbm.at[pl.ds(r0, SC_WINDOW)])

        _sc_worker_windows(n_windows, window)

    return run(ys, idx.reshape(n_windows, SC_WINDOW))


def _expert_kernel(first_ref, ntile_ref, nrow_ref, total_ref, xs_hbm, wg_ref, wu_ref, wd_ref, out_hbm,
                   xbuf, obuf, wg_s, wu_s, wd_s, sem_in, sem_out):
    e = pl.program_id(0)
    nt = ntile_ref[e]
    g0 = first_ref[e]
    total = total_ref[0]
    ring = EXPERT_RING

    def rows_of(g):
        return pl.ds(pl.multiple_of(g * ROW_TILE, ROW_TILE), ROW_TILE)

    def in_copy(g):
        slot = lax.rem(g, ring)
        return pltpu.make_async_copy(xs_hbm.at[rows_of(g)], xbuf.at[slot], sem_in.at[slot])

    def out_copy(g):
        slot = lax.rem(g, ring)
        return pltpu.make_async_copy(obuf.at[slot], out_hbm.at[rows_of(g)], sem_out.at[slot])

    @pl.when(e == 0)
    def _():
        for g in range(EXPERT_AHEAD):
            @pl.when(g < total)
            def _():
                in_copy(g).start()

    @pl.when(nt > 0)
    def _():
        wg_s[...] = wg_ref[0].astype(BF16)
        wu_s[...] = wu_ref[0].astype(BF16)
        wd_s[...] = wd_ref[0].astype(BF16)

    half = D_MODEL // 2
    sub = ROW_TILE // 2

    def tiles(gs):
        halves = []
        for g in gs:
            slot = lax.rem(g, ring)
            in_copy(g).wait()

            @pl.when(g + EXPERT_AHEAD < total)
            def _():
                in_copy(g + EXPERT_AHEAD).start()

            @pl.when(g >= ring)
            def _():
                out_copy(g - ring).wait()

            n_valid = nrow_ref[e] - (g - g0) * ROW_TILE
            row = lax.broadcasted_iota(jnp.int32, (ROW_TILE, 1), 0)
            lo, hi = _unpack_bf16_pairs(jnp.where(row < n_valid, xbuf[slot], jnp.uint32(0)))
            for r in (0, sub):
                xl, xh = lo[r:r + sub].astype(BF16), hi[r:r + sub].astype(BF16)
                gt = _dot(xl, wg_s[:half, :]) + _dot(xh, wg_s[half:, :])
                up = _dot(xl, wu_s[:half, :]) + _dot(xh, wu_s[half:, :])
                halves.append((slot, r, gt, up))
        for slot, r, gt, up in halves:
            act = (gt * jax.nn.sigmoid(gt)) * up
            obuf[slot, r:r + sub, :] = _pack_bf16_pairs(_dot(act.astype(BF16), wd_s[...]))
        for g in gs:
            out_copy(g).start()

    def pair(i, carry):
        tiles([g0 + 2 * i, g0 + 2 * i + 1])
        return carry

    lax.fori_loop(0, nt // 2, pair, 0)

    @pl.when(lax.rem(nt, 2) == 1)
    def _():
        tiles([g0 + nt - 1])

    @pl.when(e == pl.num_programs(0) - 1)
    def _():
        for k in range(1, ring + 1):
            @pl.when(total >= k)
            def _():
                out_copy(total - k).wait()


def _expert_call(first_tile, n_tiles_per_expert, n_rows_per_expert, total_tiles, xs,
                 w_gate, w_up, w_down):
    nrows = xs.shape[0]
    by_expert = lambda e, *_: (e, 0, 0)
    half_row = D_MODEL // 2
    return pl.pallas_call(
        _expert_kernel,
        grid_spec=pltpu.PrefetchScalarGridSpec(
            num_scalar_prefetch=4,
            grid=(N_EXPERTS,),
            in_specs=[pl.BlockSpec(memory_space=pl.ANY),
                      pl.BlockSpec((1, D_MODEL, D_EXPERT), by_expert),
                      pl.BlockSpec((1, D_MODEL, D_EXPERT), by_expert),
                      pl.BlockSpec((1, D_EXPERT, D_MODEL), by_expert)],
            out_specs=pl.BlockSpec(memory_space=pl.ANY),
            scratch_shapes=[pltpu.VMEM((EXPERT_RING, ROW_TILE, half_row), jnp.uint32),
                            pltpu.VMEM((EXPERT_RING, ROW_TILE, half_row), jnp.uint32),
                            pltpu.VMEM((D_MODEL, D_EXPERT), BF16),
                            pltpu.VMEM((D_MODEL, D_EXPERT), BF16),
                            pltpu.VMEM((D_EXPERT, D_MODEL), BF16),
                            pltpu.SemaphoreType.DMA((EXPERT_RING,)),
                            pltpu.SemaphoreType.DMA((EXPERT_RING,))]),
        out_shape=jax.ShapeDtypeStruct((nrows, half_row), jnp.uint32),
        compiler_params=pltpu.CompilerParams(
            dimension_semantics=("arbitrary",), vmem_limit_bytes=STREAM_VMEM_LIMIT),
        name="expert_mlp",
    )(first_tile, n_tiles_per_expert, n_rows_per_expert, total_tiles, xs, w_gate, w_up, w_down)


def _combine_kernel(h_ref, y2_ref, route_ref, nfin_ref, out_ref):
    route = route_ref[...]
    w1, w2 = route[:, 0:1], route[:, 1:2]
    lo1, hi1 = _unpack_bf16_pairs(y2_ref[0])
    lo2, hi2 = _unpack_bf16_pairs(y2_ref[1])
    y = h_ref[...] + jnp.concatenate([w1 * lo1 + w2 * lo2, w1 * hi1 + w2 * hi2], axis=1)
    out_ref[...] = _rmsnorm(y, nfin_ref[...])


def _combine_call(h, y2, route, nfin, first_token):
    n = h.shape[0]
    first_block = first_token // COMBINE_TILE
    return pl.pallas_call(
        _combine_kernel,
        grid=(n // COMBINE_TILE,),
        in_specs=[pl.BlockSpec((COMBINE_TILE, D_MODEL), lambda i: (i, 0)),
                  pl.BlockSpec((2, COMBINE_TILE, D_MODEL // 2), lambda i: (0, first_block + i, 0)),
                  pl.BlockSpec((COMBINE_TILE, LANES), lambda i: (i, 0)),
                  pl.BlockSpec((1, D_MODEL), lambda i: (0, 0))],
        out_specs=pl.BlockSpec((COMBINE_TILE, D_MODEL), lambda i: (i, 0)),
        out_shape=jax.ShapeDtypeStruct((n, D_MODEL), F32),
        compiler_params=pltpu.CompilerParams(
            dimension_semantics=("arbitrary",), vmem_limit_bytes=STREAM_VMEM_LIMIT),
        name="moe_combine",
    )(h, y2, route, nfin)


def _moe_rows(parts, w_gate, w_up, w_down):
    eid = jnp.concatenate([ids for _, ids in parts], axis=1).astype(jnp.int32)
    ntok = eid.shape[1]
    assert ntok % RANK_TILE == 0 and ntok % SC_WINDOW == 0
    rank, cnt = _rank_call(eid)
    counts = cnt[:, 0].astype(jnp.int32)
    padded = ((counts + ROW_TILE - 1) // ROW_TILE) * ROW_TILE
    ends = jnp.cumsum(padded)
    starts = ends - padded
    expert_iota = jnp.arange(N_EXPERTS, dtype=jnp.int32)[:, None, None]
    pos = rank + jnp.sum(jnp.where(eid[None] == expert_iota, starts[:, None, None], 0), axis=0)
    n_rows = (2 * ntok + N_EXPERTS * (ROW_TILE - 1)) // ROW_TILE * ROW_TILE
    xs = _dispatch_rows([rows for rows, _ in parts], pos, n_rows)
    ys = _expert_call(starts // ROW_TILE, padded // ROW_TILE, counts, ends[-1:] // ROW_TILE, xs,
                      w_gate, w_up, w_down)
    return _return_rows(ys, pos.reshape(-1)).reshape(2, ntok, D_MODEL // 2)


def kernel(x_prompt, x_sample, state_conv, state_gla, norm_mix, w_in, w_conv, w_alpha_up, b_alpha,
           gla_norm, w_conv_out, w_gla_out, w_o, norm_ffn, w_group_router, b_group_router,
           w_expert_router, b_expert_router, w_gate, w_up, w_down, norm_final):
    nbp, seq_p, _ = x_prompt.shape
    nbs, seq_s, _ = x_sample.shape
    assert norm_mix.shape[0] == 1, "single layer"
    assert seq_p % PROMPT_TILE == 0 and nbs % SAMPLE_SEQS == 0 and seq_s == 8

    weights = _mixer_weights(norm_mix, w_in, w_conv, w_alpha_up, b_alpha, gla_norm, w_conv_out,
                             w_gla_out, w_o, norm_ffn, w_group_router, b_group_router,
                             w_expert_router, b_expert_router)

    ntp, nts = nbp * seq_p, nbs * seq_s
    assert ntp % COMBINE_TILE == 0 and nts % COMBINE_TILE == 0
    conv0 = jnp.zeros((nbp, CONV_WIDTH - 1, D_CONV), F32)
    gla0 = jnp.zeros((nbp, 1, 8, LANES), F32)
    flat = lambda a: a.reshape(-1, a.shape[-1])
    ids_of = lambda route_t: route_t[:, 2:4, :].transpose(1, 0, 2).reshape(2, -1)
    h_p, xn_p, rt_p, rtt_p, conv_p, gla_p = _mixer_call(x_prompt, conv0, gla0, weights, prompt=True)
    h_s, xn_s, rt_s, rtt_s, conv_s, gla_s = _mixer_call(x_sample, state_conv[0], state_gla[0],
                                                        weights, prompt=False)
    y2 = _moe_rows([(flat(xn_p), ids_of(rtt_p)), (flat(xn_s), ids_of(rtt_s))],
                   w_gate[0], w_up[0], w_down[0])
    nfin = norm_final[None, :]
    y_p = _combine_call(flat(h_p), y2, flat(rt_p), nfin, 0)
    y_s = _combine_call(flat(h_s), y2, flat(rt_s), nfin, ntp)
    return (y_p.reshape(nbp, seq_p, D_MODEL), y_s.reshape(nbs, seq_s, D_MODEL),
            conv_p[None], gla_p[None], conv_s[None], gla_s[None])
```

```python
import functools
import itertools

import jax
import jax.numpy as jnp
from jax import lax
from jax.experimental import pallas as pl
from jax.experimental.pallas import tpu as pltpu
from jax.experimental.pallas import tpu_sc as plsc

F32 = jnp.float32
BF16 = jnp.bfloat16

D_MODEL = 1024
D_CONV = 512
CONV_WIDTH = 3
GLA_HEADS = 4
GLA_DK = 128
GLA_DV = 256
GLA_RANK = 16
GLA_GATE_NORM = 16.0
GLA_CHUNK = 32
N_GROUPS = 4
EXPERTS_PER_GROUP = 8
N_EXPERTS = 32
D_EXPERT = 256
EPS = 1e-6

LANES = 128
V7X_VMEM_BYTES = 64 * 1024 * 1024
OFF_CB, OFF_CC, OFF_CH = 0, 512, 1024
OFF_Q, OFF_K, OFF_V, OFF_G = 1536, 2048, 2560, 3584
OFF_GA, OFF_GB, OFF_AL = 4608, 5632, 6656

PROMPT_TILE = 512
GLA_BLOCK = 256
SAMPLE_SEQS = 32
STATE_CHUNK = 4
STATE_OUT_SLOTS = 4
ROW_TILE = 256
EXPERT_RING = 6
EXPERT_AHEAD = 4
SC_WINDOW = 128
COMBINE_TILE = 1024
RANK_TILE = 1024
RANK_BLOCK = 512
VMEM_HEADROOM_BYTES = 6 * 1024 * 1024
MIXER_VMEM_LIMIT = V7X_VMEM_BYTES - VMEM_HEADROOM_BYTES
STREAM_VMEM_LIMIT = MIXER_VMEM_LIMIT


def _rmsnorm(x, g):
    ms = jnp.mean(x * x, axis=-1, keepdims=True)
    return (x * lax.rsqrt(ms + EPS)) * g


def _split_bf16(x):
    hi = x.astype(BF16)
    lo = (x - hi.astype(F32)).astype(BF16)
    return hi, lo


def _pack_bf16_pairs(x):
    n = x.shape[1] // 2
    lo = lax.bitcast_convert_type(x[:, :n].astype(BF16).astype(F32), jnp.uint32)
    hi = lax.bitcast_convert_type(x[:, n:].astype(BF16).astype(F32), jnp.uint32)
    return lax.shift_right_logical(lo, jnp.uint32(16)) | hi


def _unpack_bf16_pairs(w):
    lo = lax.bitcast_convert_type(lax.shift_left(w, jnp.uint32(16)), F32)
    hi = lax.bitcast_convert_type(w & jnp.uint32(0xFFFF0000), F32)
    return lo, hi


def _dot(a, b):
    return jnp.dot(a, b, preferred_element_type=F32)


def _chunk_causal(n, chunk):
    r = lax.broadcasted_iota(jnp.int32, (n, n), 0)
    c = lax.broadcasted_iota(jnp.int32, (n, n), 1)
    shift = chunk.bit_length() - 1
    assert chunk == 1 << shift
    same = lax.shift_right_arithmetic(r, shift) == lax.shift_right_arithmetic(c, shift)
    return same & (c <= r)


def _router_logits(xn2, wr_ref, br_ref):
    xh, xl = _split_bf16(xn2)
    part = _dot(xh, wr_ref[...]) + _dot(xl, wr_ref[...])
    return part[:, :LANES] + part[:, LANES:] + br_ref[...]


def _route(lg):
    n = lg.shape[0]
    n_rows = 40
    lt = lg.T[:n_rows, :]
    row_i = lax.broadcasted_iota(jnp.int32, (n_rows, n), 0)
    row = row_i.astype(F32)
    neg = -jnp.inf
    none = float(LANES)
    is_g = row_i < N_GROUPS
    gm = jnp.max(jnp.where(is_g, lt, neg), axis=0, keepdims=True)
    gs = jnp.sum(jnp.where(is_g, jnp.exp(lt - gm), 0.0), axis=0, keepdims=True)
    g_top = 1.0 / gs
    g_idx = jnp.min(jnp.where(is_g & (lt == gm), row, none), axis=0, keepdims=True)
    e_group = lax.shift_right_arithmetic(row_i - N_GROUPS, 3).astype(F32)
    sel = (row_i >= N_GROUPS) & (row_i < N_GROUPS + N_EXPERTS) & (e_group == g_idx)
    el = jnp.where(sel, lt, neg)
    e1 = jnp.max(el, axis=0, keepdims=True)
    i1 = jnp.min(jnp.where(sel & (el == e1), row, none), axis=0, keepdims=True)
    sel2 = sel & (row != i1)
    el2 = jnp.where(sel2, lt, neg)
    e2 = jnp.max(el2, axis=0, keepdims=True)
    i2 = jnp.min(jnp.where(sel2 & (el2 == e2), row, none), axis=0, keepdims=True)
    d = jnp.exp(e2 - e1)
    w1 = g_top / (1.0 + d)
    w2 = (g_top * d) / (1.0 + d)
    id1 = i1 - float(N_GROUPS)
    id2 = i2 - float(N_GROUPS)
    out_i = lax.broadcasted_iota(jnp.int32, (8, n), 0)
    route_t = jnp.where(out_i == 0, w1, jnp.where(out_i == 1, w2, jnp.where(out_i == 2, id1,
                        jnp.where(out_i == 3, id2, 0.0))))
    route = jnp.concatenate([route_t, jnp.zeros((LANES - 8, n), F32)], axis=0).T
    return route, route_t


def _mixer_stages(sub, nsub, x_ref, conv_in_ref, gla_in_ref,
                  nmix_ref, w_ref, wgate_ref, walr_ref, wconv_ref, walpha_ref, balpha_ref, gnorm_ref,
                  wco_ref, wgo_ref, wo_ref, nffn_ref, wr_ref, br_ref,
                  h_ref, xn2_ref, route_ref, route_t_ref, conv_out_ref, gla_out_ref,
                  s_ref, ubuf_ref, *state_ring, prompt, logits_out):
    if prompt:
        rows, chunk, nseq = GLA_BLOCK, GLA_CHUNK, 1
        r0 = sub * rows
        x = x_ref[0, r0:r0 + rows, :]
    else:
        nseq, chunk = SAMPLE_SEQS, x_ref.shape[1]
        rows = nseq * chunk
        r0 = 0
        x = x_ref[...].reshape(rows, D_MODEL)
        sin_ref, sout_ref, sem_in, sem_out = state_ring
        n_state_chunks = nseq // STATE_CHUNK
        step, n_steps = pl.program_id(0), pl.num_programs(0)

        def chunk_seqs(c, of_step):
            first = of_step * nseq + c * STATE_CHUNK
            return pl.ds(pl.multiple_of(first, STATE_CHUNK), STATE_CHUNK)

        def in_copy(c, of_step):
            return pltpu.make_async_copy(gla_in_ref.at[chunk_seqs(c, of_step)], sin_ref.at[c],
                                         sem_in.at[c])

        def out_copy(c, of_step):
            slot = c % STATE_OUT_SLOTS
            return pltpu.make_async_copy(sout_ref.at[slot], gla_out_ref.at[chunk_seqs(c, of_step)],
                                         sem_out.at[slot])

        @pl.when(step == 0)
        def _():
            for c in range(n_state_chunks):
                in_copy(c, step).start()
    nchunks = rows // chunk
    last_sub = sub == nsub - 1

    xn = _rmsnorm(x, nmix_ref[...]).astype(BF16)
    yield

    def proj(off, n):
        if off == OFF_AL:
            return _dot(xn, walr_ref[...])
        if off >= OFF_GA:
            return _dot(xn, wgate_ref[:, off - OFF_GA:off - OFF_GA + n])
        return _dot(xn, w_ref[:, off:off + n])

    alr = proj(OFF_AL, LANES).astype(BF16)
    lap = _dot(alr, walpha_ref[...])
    la = jax.nn.log_sigmoid(lap + balpha_ref[...]) * (1.0 / GLA_GATE_NORM)
    la_hi, la_lo = _split_bf16(la)
    causal = _chunk_causal(rows, chunk)
    tri = jnp.where(causal, 1.0, 0.0).astype(BF16)
    yield
    cc = proj(OFF_CC, D_CONV)
    ch = proj(OFF_CH, D_CONV)
    cb = proj(OFF_CB, D_CONV)
    q = proj(OFF_Q, D_CONV)
    k = proj(OFF_K, D_CONV)
    yield
    b = _dot(tri, la_hi) + _dot(tri, la_lo)

    u = cc * ch
    wc = wconv_ref[...]
    if prompt:
        ubuf_ref[0, 8 + r0:8 + r0 + rows, :] = u
        z = (wc[0:1] * ubuf_ref[0, 6 + r0:6 + r0 + rows, :]
             + wc[1:2] * ubuf_ref[0, 7 + r0:7 + r0 + rows, :] + wc[2:3] * u)
        if last_sub:
            tail = u[rows - 2:rows, :]
            ubuf_ref[0, 6:8, :] = tail
            conv_out_ref[0] = tail
    else:
        u3 = u.reshape(nseq, chunk, D_CONV)
        ubuf_ref[:, 6:8, :] = conv_in_ref[...]
        ubuf_ref[:, 8:8 + chunk, :] = u3
        z3 = (wc[0:1] * ubuf_ref[:, 6:6 + chunk, :] + wc[1:2] * ubuf_ref[:, 7:7 + chunk, :]
              + wc[2:3] * u3)
        z = z3.reshape(rows, D_CONV)
        conv_out_ref[...] = ubuf_ref[:, 6 + chunk:8 + chunk, :]
    ya = _dot((cb * z).astype(BF16), wco_ref[...])
    v = proj(OFF_V, GLA_HEADS * GLA_DV)
    yield

    b3 = b.reshape(nchunks, chunk, D_CONV)
    bl3 = b3[:, chunk - 1:chunk, :]
    qin = (q * (GLA_DK ** -0.5)) * jnp.exp(b)
    kin = k * jnp.exp(-b)
    kst = (k.reshape(nchunks, chunk, D_CONV) * jnp.exp(bl3 - b3)).reshape(rows, D_CONV)
    pad = (-rows) % LANES
    rows_p = rows + pad

    def transposed(a):
        if pad:
            a = jnp.concatenate([a, jnp.zeros((pad, a.shape[1]), F32)], axis=0)
        return a.T

    b_t = transposed(b)
    kst_t = transposed(kst) if prompt else None
    col_chunk = lax.shift_right_arithmetic(
        lax.broadcasted_iota(jnp.int32, (GLA_DK, rows_p), 1), chunk.bit_length() - 1)
    g = proj(OFF_G, GLA_HEADS * GLA_DV)
    yield
    heads = range(GLA_HEADS)
    vhs = [v[:, hd * GLA_DV:(hd + 1) * GLA_DV].astype(BF16) for hd in heads]
    scs = [lax.dot_general(qin[:, hd * GLA_DK:(hd + 1) * GLA_DK].astype(BF16),
                           kin[:, hd * GLA_DK:(hd + 1) * GLA_DK].astype(BF16),
                           (((1,), (1,)), ((), ())), preferred_element_type=F32) for hd in heads]
    ga = proj(OFF_GA, D_MODEL)
    yield
    upds = []
    for hd in heads if prompt else ():
        kst_h = kst_t[hd * GLA_DK:(hd + 1) * GLA_DK, :]
        stacked = jnp.concatenate(
            [jnp.where(col_chunk == n, kst_h, 0.0).astype(BF16) for n in range(nchunks)], axis=0)
        vh_p = jnp.concatenate([vhs[hd], jnp.zeros((pad, GLA_DV), BF16)], axis=0) if pad else vhs[hd]
        upds.append(_dot(stacked, vh_p))
    o_intras = [_dot(jnp.where(causal, scs[hd], 0.0).astype(BF16), vhs[hd]) for hd in heads]
    gb = proj(OFF_GB, D_MODEL)
    merge_a = jax.nn.sigmoid(ga) * ya
    gate_b = jax.nn.sigmoid(gb)
    yield
    def q_block(n, hd):
        return qin[n * chunk:(n + 1) * chunk, hd * GLA_DK:(hd + 1) * GLA_DK].astype(BF16)

    def decay_col(n, hd):
        last = n * chunk + chunk - 1
        return jnp.exp(b_t[hd * GLA_DK:(hd + 1) * GLA_DK, last:last + 1])

    o_inter = [[None] * nchunks for _ in heads]
    if prompt:
        for hd in heads:
            s_run = s_ref[hd]
            for n in range(nchunks):
                o_inter[hd][n] = _dot(q_block(n, hd), s_run.astype(BF16))
                s_run = s_run * decay_col(n, hd) + upds[hd][n * GLA_DK:(n + 1) * GLA_DK, :]
            s_ref[hd] = s_run
            if last_sub:
                gla_out_ref[0, hd] = s_run
    else:
        for c in range(n_state_chunks):
            in_copy(c, step).wait()
            if c >= STATE_OUT_SLOTS:
                out_copy(c - STATE_OUT_SLOTS, step).wait()
            else:
                @pl.when(step > 0)
                def _():
                    out_copy(c + n_state_chunks - STATE_OUT_SLOTS, step - 1).wait()
            for j in range(STATE_CHUNK):
                n = c * STATE_CHUNK + j
                for hd in heads:
                    s_prev = sin_ref[c, j, hd]
                    o_inter[hd][n] = _dot(q_block(n, hd), s_prev.astype(BF16))
                    ksb = kst[n * chunk:(n + 1) * chunk, hd * GLA_DK:(hd + 1) * GLA_DK].astype(BF16)
                    vb = v[n * chunk:(n + 1) * chunk, hd * GLA_DV:(hd + 1) * GLA_DV].astype(BF16)
                    upd = lax.dot_general(ksb, vb, (((0,), (0,)), ((), ())),
                                          preferred_element_type=F32)
                    sout_ref[c % STATE_OUT_SLOTS, j, hd] = s_prev * decay_col(n, hd) + upd
            out_copy(c, step).start()

        @pl.when(step + 1 < n_steps)
        def _():
            for c in range(n_state_chunks):
                in_copy(c, step + 1).start()

        @pl.when(step + 1 == n_steps)
        def _():
            for c in range(n_state_chunks - STATE_OUT_SLOTS, n_state_chunks):
                out_copy(c, step).wait()
    gated = []
    for hd in heads:
        v0 = hd * GLA_DV
        oh = _rmsnorm(o_intras[hd] + jnp.concatenate(o_inter[hd], axis=0), gnorm_ref[...])
        gh = g[:, v0:v0 + GLA_DV]
        gated.append(oh * (gh * jax.nn.sigmoid(gh)))
    yield
    yb = _dot(jnp.concatenate(gated, axis=1).astype(BF16), wgo_ref[...])
    yield

    m = merge_a + gate_b * yb
    hh = x + _dot(m.astype(BF16), wo_ref[...])
    yield
    xn2 = _rmsnorm(hh, nffn_ref[...])
    yield
    logits_out.append(_router_logits(xn2, wr_ref, br_ref))
    if prompt:
        h_ref[0, r0:r0 + rows, :] = hh
        xn2_ref[0, r0:r0 + rows, :] = _pack_bf16_pairs(xn2)
    else:
        h_ref[...] = hh.reshape(nseq, chunk, D_MODEL)
        xn2_ref[...] = _pack_bf16_pairs(xn2).reshape(nseq, chunk, D_MODEL // 2)


def _mixer_kernel(*refs, prompt):
    n_scratch = 2 if prompt else 6
    outs = refs[len(refs) - n_scratch - 6:len(refs) - n_scratch]
    route_ref, route_t_ref = outs[2], outs[3]
    if prompt:
        s_ref, ubuf_ref = refs[-2:]

        @pl.when(pl.program_id(1) == 0)
        def _():
            s_ref[...] = jnp.zeros_like(s_ref)
            ubuf_ref[...] = jnp.zeros_like(ubuf_ref)
    nsub = PROMPT_TILE // GLA_BLOCK if prompt else 1
    logits = []
    stages = [_mixer_stages(sub, nsub, *refs, prompt=prompt, logits_out=logits)
              for sub in range(nsub)]
    for _ in itertools.zip_longest(*stages):
        pass
    route, route_t = _route(jnp.concatenate(logits, axis=0))
    if prompt:
        route_ref[0] = route
    else:
        route_ref[...] = route.reshape(route_ref.shape)
    route_t_ref[0] = route_t


def _mixer_weights(norm_mix, w_in, w_conv, w_alpha_up, b_alpha, gla_norm, w_conv_out, w_gla_out, w_o,
                   norm_ffn, w_group_router, b_group_router, w_expert_router, b_expert_router):
    wi = w_in[0]
    w_main = wi[:, :OFF_GA].astype(BF16)
    w_gates = wi[:, OFF_GA + GLA_RANK:].astype(BF16)
    w_alr = jnp.concatenate([wi[:, OFF_GA:OFF_GA + GLA_RANK],
                             jnp.zeros((D_MODEL, LANES - GLA_RANK), F32)], axis=1).astype(BF16)
    walpha = jnp.concatenate(
        [w_alpha_up[0], jnp.zeros((LANES - GLA_RANK, D_CONV), F32)], axis=0).astype(BF16)
    wr = jnp.concatenate(
        [w_group_router[0], w_expert_router[0],
         jnp.zeros((D_MODEL, LANES - N_GROUPS - N_EXPERTS), F32)], axis=1)
    wr_hi = wr.astype(BF16)
    wr_split = jnp.concatenate([wr_hi, (wr - wr_hi.astype(F32)).astype(BF16)], axis=1)
    br = jnp.concatenate([b_group_router[0], b_expert_router[0],
                          jnp.zeros((LANES - N_GROUPS - N_EXPERTS,), F32)])[None, :]
    return (norm_mix, w_main, w_gates, w_alr, w_conv[0], walpha, b_alpha, gla_norm,
            w_conv_out[0].astype(BF16), w_gla_out[0].astype(BF16), w_o[0].astype(BF16),
            norm_ffn, wr_split, br)


def _const_spec(shape):
    nd = len(shape)
    return pl.BlockSpec(shape, lambda *_: (0,) * nd, pipeline_mode=pl.Buffered(1))


def _mixer_call(x, conv_state, gla_state, weights, *, prompt):
    nb, seq, _ = x.shape
    if prompt:
        grid = (nb, seq // PROMPT_TILE)
        tok = lambda last: pl.BlockSpec((1, PROMPT_TILE, last), lambda b, i: (b, i, 0))
        conv_spec = pl.BlockSpec((1, CONV_WIDTH - 1, D_CONV), lambda b, i: (b, 0, 0))
        gla_spec = pl.BlockSpec((1, GLA_HEADS, GLA_DK, GLA_DV), lambda b, i: (b, 0, 0, 0))
        gla_in_spec = pl.BlockSpec((1, 1, 8, LANES), lambda b, i: (b, 0, 0, 0))
        scratch = [pltpu.VMEM((GLA_HEADS, GLA_DK, GLA_DV), F32),
                   pltpu.VMEM((1, PROMPT_TILE + 8, D_CONV), F32)]
        route_t_shape = (nb, 8, seq)
        route_t_spec = pl.BlockSpec((1, 8, PROMPT_TILE), lambda b, i: (b, 0, i))
    else:
        grid = (nb // SAMPLE_SEQS, 1)
        tok = lambda last: pl.BlockSpec((SAMPLE_SEQS, seq, last), lambda b, i: (b, 0, 0))
        conv_spec = pl.BlockSpec((SAMPLE_SEQS, CONV_WIDTH - 1, D_CONV), lambda b, i: (b, 0, 0))
        gla_spec = pl.BlockSpec(memory_space=pl.ANY)
        gla_in_spec = gla_spec
        state_chunk = (STATE_CHUNK, GLA_HEADS, GLA_DK, GLA_DV)
        scratch = [pltpu.VMEM((1, 8, LANES), F32),
                   pltpu.VMEM((SAMPLE_SEQS, 8 + seq, D_CONV), F32),
                   pltpu.VMEM((SAMPLE_SEQS // STATE_CHUNK,) + state_chunk, F32),
                   pltpu.VMEM((STATE_OUT_SLOTS,) + state_chunk, F32),
                   pltpu.SemaphoreType.DMA((SAMPLE_SEQS // STATE_CHUNK,)),
                   pltpu.SemaphoreType.DMA((STATE_OUT_SLOTS,))]
        route_t_shape = (nb // SAMPLE_SEQS, 8, SAMPLE_SEQS * seq)
        route_t_spec = pl.BlockSpec((1, 8, SAMPLE_SEQS * seq), lambda b, i: (b, 0, 0))
    out_shape = (jax.ShapeDtypeStruct((nb, seq, D_MODEL), F32),
                 jax.ShapeDtypeStruct((nb, seq, D_MODEL // 2), jnp.uint32),
                 jax.ShapeDtypeStruct((nb, seq, LANES), F32),
                 jax.ShapeDtypeStruct(route_t_shape, F32),
                 jax.ShapeDtypeStruct((nb, CONV_WIDTH - 1, D_CONV), F32),
                 jax.ShapeDtypeStruct((nb, GLA_HEADS, GLA_DK, GLA_DV), F32))
    return pl.pallas_call(
        functools.partial(_mixer_kernel, prompt=prompt),
        grid=grid,
        in_specs=[tok(D_MODEL), conv_spec, gla_in_spec] + [_const_spec(w.shape) for w in weights],
        out_specs=(tok(D_MODEL), tok(D_MODEL // 2), tok(LANES), route_t_spec, conv_spec, gla_spec),
        out_shape=out_shape,
        scratch_shapes=scratch,
        compiler_params=pltpu.CompilerParams(
            dimension_semantics=("arbitrary", "arbitrary"), vmem_limit_bytes=MIXER_VMEM_LIMIT),
        name="mixer_prompt" if prompt else "mixer_sample",
    )(x, conv_state, gla_state, *weights)


def _rank_kernel(eid_ref, rank_ref, cnt_ref, carry_ref):
    @pl.when(pl.program_id(0) == 0)
    def _():
        carry_ref[...] = jnp.zeros_like(carry_ref)

    n = RANK_BLOCK
    e_iota = lax.broadcasted_iota(jnp.int32, (N_EXPERTS, n), 0)
    r = lax.broadcasted_iota(jnp.int32, (n, n), 0)
    c = lax.broadcasted_iota(jnp.int32, (n, n), 1)
    before = jnp.where(r < c, 1.0, 0.0).astype(BF16)
    carry = carry_ref[...]
    for j in range(eid_ref.shape[1] // n):
        cols = slice(j * n, (j + 1) * n)
        oh0 = jnp.where(e_iota == eid_ref[0:1, cols], 1.0, 0.0)
        oh1 = jnp.where(e_iota == eid_ref[1:2, cols], 1.0, 0.0)
        cnt = oh0 + oh1
        base = carry + _dot(cnt.astype(BF16), before)
        rank0 = jnp.sum(oh0 * base, axis=0, keepdims=True)
        rank1 = jnp.sum(oh1 * base, axis=0, keepdims=True)
        rank_ref[:, cols] = jnp.concatenate([rank0, rank1], axis=0).astype(jnp.int32)
        carry = carry + jnp.sum(cnt, axis=1, keepdims=True)
    carry_ref[...] = carry
    cnt_ref[...] = jnp.broadcast_to(carry, cnt_ref.shape)


def _rank_call(eid):
    ntok = eid.shape[1]
    return pl.pallas_call(
        _rank_kernel,
        grid=(ntok // RANK_TILE,),
        in_specs=[pl.BlockSpec((2, RANK_TILE), lambda i: (0, i))],
        out_specs=(pl.BlockSpec((2, RANK_TILE), lambda i: (0, i)),
                   pl.BlockSpec((N_EXPERTS, LANES), lambda i: (0, 0))),
        out_shape=(jax.ShapeDtypeStruct((2, ntok), jnp.int32),
                   jax.ShapeDtypeStruct((N_EXPERTS, LANES), F32)),
        scratch_shapes=[pltpu.VMEM((N_EXPERTS, 1), F32)],
        compiler_params=pltpu.CompilerParams(dimension_semantics=("arbitrary",)),
        name="expert_rank",
    )(eid)


def _sc_mesh():
    return plsc.VectorSubcoreMesh(core_axis_name="c", subcore_axis_name="s")


def _sc_worker_windows(n_windows, body):
    info = plsc.get_sparse_core_info()
    n_workers = info.num_cores * info.num_subcores
    wid = lax.axis_index("s") * info.num_cores + lax.axis_index("c")

    @pl.loop(0, pl.cdiv(n_windows, n_workers))
    def _(j):
        win = wid + n_workers * j

        @pl.when(win < n_windows)
        def _():
            body(win)


def _dispatch_rows(sources, pos, n_rows):
    width, dtype = sources[0].shape[1], sources[0].dtype
    n_win = [src.shape[0] // SC_WINDOW for src in sources]
    first_win = [sum(n_win[:i]) for i in range(len(sources))]
    total_win = sum(n_win)
    pos3 = pos.reshape(2, total_win, SC_WINDOW).transpose(1, 0, 2)

    @functools.partial(pl.kernel, mesh=_sc_mesh(), name="moe_dispatch",
                       out_type=jax.ShapeDtypeStruct((n_rows, width), dtype),
                       scratch_types=[pltpu.VMEM((2, SC_WINDOW), jnp.int32),
                                      pltpu.VMEM((SC_WINDOW, width), dtype)])
    def run(*refs):
        src_hbm, (pos_hbm, out_hbm, idx_v, buf) = refs[:len(sources)], refs[len(sources):]

        def window(win):
            pltpu.sync_copy(pos_hbm.at[win], idx_v)
            for src, w0, nw in zip(src_hbm, first_win, n_win):
                @pl.when((win >= w0) & (win < w0 + nw))
                def _():
                    r0 = pl.multiple_of((win - w0) * SC_WINDOW, SC_WINDOW)
                    pltpu.sync_copy(src.at[pl.ds(r0, SC_WINDOW)], buf)

            pltpu.sync_copy(buf, out_hbm.at[idx_v.at[0]])
            pltpu.sync_copy(buf, out_hbm.at[idx_v.at[1]])

        _sc_worker_windows(total_win, window)

    return run(*sources, pos3)


def _return_rows(ys, idx):
    n_out, width = idx.shape[0], ys.shape[1]
    n_windows = n_out // SC_WINDOW

    @functools.partial(pl.kernel, mesh=_sc_mesh(), name="moe_return",
                       out_type=jax.ShapeDtypeStruct((n_out, width), ys.dtype),
                       scratch_types=[pltpu.VMEM((SC_WINDOW,), jnp.int32),
                                      pltpu.VMEM((SC_WINDOW, width), ys.dtype)])
    def run(ys_hbm, idx_hbm, out_hbm, idx_v, buf):
        def window(win):
            pltpu.sync_copy(idx_hbm.at[win], idx_v)
            pltpu.sync_copy(ys_hbm.at[idx_v], buf)
            r0 = pl.multiple_of(win * SC_WINDOW, SC_WINDOW)
            pltpu.sync_copy(buf, out_hbm.at[pl.ds(r0, SC_WINDOW)])

        _sc_worker_windows(n_windows, window)

    return run(ys, idx.reshape(n_windows, SC_WINDOW))


def _expert_kernel(first_ref, ntile_ref, nrow_ref, total_ref, xs_hbm, wg_ref, wu_ref, wd_ref, out_hbm,
                   xbuf, obuf, wg_s, wu_s, wd_s, sem_in, sem_out):
    e = pl.program_id(0)
    nt = ntile_ref[e]
    g0 = first_ref[e]
    total = total_ref[0]
    ring = EXPERT_RING

    def rows_of(g):
        return pl.ds(pl.multiple_of(g * ROW_TILE, ROW_TILE), ROW_TILE)

    def in_copy(g):
        slot = lax.rem(g, ring)
        return pltpu.make_async_copy(xs_hbm.at[rows_of(g)], xbuf.at[slot], sem_in.at[slot])

    def out_copy(g):
        slot = lax.rem(g, ring)
        return pltpu.make_async_copy(obuf.at[slot], out_hbm.at[rows_of(g)], sem_out.at[slot])

    @pl.when(e == 0)
    def _():
        for g in range(EXPERT_AHEAD):
            @pl.when(g < total)
            def _():
                in_copy(g).start()

    @pl.when(nt > 0)
    def _():
        wg_s[...] = wg_ref[0].astype(BF16)
        wu_s[...] = wu_ref[0].astype(BF16)
        wd_s[...] = wd_ref[0].astype(BF16)

    half = D_MODEL // 2
    sub = ROW_TILE // 2

    def tiles(gs):
        halves = []
        for g in gs:
            slot = lax.rem(g, ring)
            in_copy(g).wait()

            @pl.when(g + EXPERT_AHEAD < total)
            def _():
                in_copy(g + EXPERT_AHEAD).start()

            @pl.when(g >= ring)
            def _():
                out_copy(g - ring).wait()

            n_valid = nrow_ref[e] - (g - g0) * ROW_TILE
            row = lax.broadcasted_iota(jnp.int32, (ROW_TILE, 1), 0)
            lo, hi = _unpack_bf16_pairs(jnp.where(row < n_valid, xbuf[slot], jnp.uint32(0)))
            for r in (0, sub):
                xl, xh = lo[r:r + sub].astype(BF16), hi[r:r + sub].astype(BF16)
                gt = _dot(xl, wg_s[:half, :]) + _dot(xh, wg_s[half:, :])
                up = _dot(xl, wu_s[:half, :]) + _dot(xh, wu_s[half:, :])
                halves.append((slot, r, gt, up))
        for slot, r, gt, up in halves:
            act = (gt * jax.nn.sigmoid(gt)) * up
            obuf[slot, r:r + sub, :] = _pack_bf16_pairs(_dot(act.astype(BF16), wd_s[...]))
        for g in gs:
            out_copy(g).start()

    def pair(i, carry):
        tiles([g0 + 2 * i, g0 + 2 * i + 1])
        return carry

    lax.fori_loop(0, nt // 2, pair, 0)

    @pl.when(lax.rem(nt, 2) == 1)
    def _():
        tiles([g0 + nt - 1])

    @pl.when(e == pl.num_programs(0) - 1)
    def _():
        for k in range(1, ring + 1):
            @pl.when(total >= k)
            def _():
                out_copy(total - k).wait()


def _expert_call(first_tile, n_tiles_per_expert, n_rows_per_expert, total_tiles, xs,
                 w_gate, w_up, w_down):
    nrows = xs.shape[0]
    by_expert = lambda e, *_: (e, 0, 0)
    half_row = D_MODEL // 2
    return pl.pallas_call(
        _expert_kernel,
        grid_spec=pltpu.PrefetchScalarGridSpec(
            num_scalar_prefetch=4,
            grid=(N_EXPERTS,),
            in_specs=[pl.BlockSpec(memory_space=pl.ANY),
                      pl.BlockSpec((1, D_MODEL, D_EXPERT), by_expert),
                      pl.BlockSpec((1, D_MODEL, D_EXPERT), by_expert),
                      pl.BlockSpec((1, D_EXPERT, D_MODEL), by_expert)],
            out_specs=pl.BlockSpec(memory_space=pl.ANY),
            scratch_shapes=[pltpu.VMEM((EXPERT_RING, ROW_TILE, half_row), jnp.uint32),
                            pltpu.VMEM((EXPERT_RING, ROW_TILE, half_row), jnp.uint32),
                            pltpu.VMEM((D_MODEL, D_EXPERT), BF16),
                            pltpu.VMEM((D_MODEL, D_EXPERT), BF16),
                            pltpu.VMEM((D_EXPERT, D_MODEL), BF16),
                            pltpu.SemaphoreType.DMA((EXPERT_RING,)),
                            pltpu.SemaphoreType.DMA((EXPERT_RING,))]),
        out_shape=jax.ShapeDtypeStruct((nrows, half_row), jnp.uint32),
        compiler_params=pltpu.CompilerParams(
            dimension_semantics=("arbitrary",), vmem_limit_bytes=STREAM_VMEM_LIMIT),
        name="expert_mlp",
    )(first_tile, n_tiles_per_expert, n_rows_per_expert, total_tiles, xs, w_gate, w_up, w_down)


def _combine_kernel(h_ref, y2_ref, route_ref, nfin_ref, out_ref):
    route = route_ref[...]
    w1, w2 = route[:, 0:1], route[:, 1:2]
    lo1, hi1 = _unpack_bf16_pairs(y2_ref[0])
    lo2, hi2 = _unpack_bf16_pairs(y2_ref[1])
    y = h_ref[...] + jnp.concatenate([w1 * lo1 + w2 * lo2, w1 * hi1 + w2 * hi2], axis=1)
    out_ref[...] = _rmsnorm(y, nfin_ref[...])


def _combine_call(h, y2, route, nfin, first_token):
    n = h.shape[0]
    first_block = first_token // COMBINE_TILE
    return pl.pallas_call(
        _combine_kernel,
        grid=(n // COMBINE_TILE,),
        in_specs=[pl.BlockSpec((COMBINE_TILE, D_MODEL), lambda i: (i, 0)),
                  pl.BlockSpec((2, COMBINE_TILE, D_MODEL // 2), lambda i: (0, first_block + i, 0)),
                  pl.BlockSpec((COMBINE_TILE, LANES), lambda i: (i, 0)),
                  pl.BlockSpec((1, D_MODEL), lambda i: (0, 0))],
        out_specs=pl.BlockSpec((COMBINE_TILE, D_MODEL), lambda i: (i, 0)),
        out_shape=jax.ShapeDtypeStruct((n, D_MODEL), F32),
        compiler_params=pltpu.CompilerParams(
            dimension_semantics=("arbitrary",), vmem_limit_bytes=STREAM_VMEM_LIMIT),
        name="moe_combine",
    )(h, y2, route, nfin)


def _moe_rows(parts, w_gate, w_up, w_down):
    eid = jnp.concatenate([ids for _, ids in parts], axis=1).astype(jnp.int32)
    ntok = eid.shape[1]
    assert ntok % RANK_TILE == 0 and ntok % SC_WINDOW == 0
    rank, cnt = _rank_call(eid)
    counts = cnt[:, 0].astype(jnp.int32)
    padded = ((counts + ROW_TILE - 1) // ROW_TILE) * ROW_TILE
    ends = jnp.cumsum(padded)
    starts = ends - padded
    expert_iota = jnp.arange(N_EXPERTS, dtype=jnp.int32)[:, None, None]
    pos = rank + jnp.sum(jnp.where(eid[None] == expert_iota, starts[:, None, None], 0), axis=0)
    n_rows = (2 * ntok + N_EXPERTS * (ROW_TILE - 1)) // ROW_TILE * ROW_TILE
    xs = _dispatch_rows([rows for rows, _ in parts], pos, n_rows)
    ys = _expert_call(starts // ROW_TILE, padded // ROW_TILE, counts, ends[-1:] // ROW_TILE, xs,
                      w_gate, w_up, w_down)
    return _return_rows(ys, pos.reshape(-1)).reshape(2, ntok, D_MODEL // 2)


def kernel(x_prompt, x_sample, state_conv, state_gla, norm_mix, w_in, w_conv, w_alpha_up, b_alpha,
           gla_norm, w_conv_out, w_gla_out, w_o, norm_ffn, w_group_router, b_group_router,
           w_expert_router, b_expert_router, w_gate, w_up, w_down, norm_final):
    nbp, seq_p, _ = x_prompt.shape
    nbs, seq_s, _ = x_sample.shape
    assert norm_mix.shape[0] == 1, "single layer"
    assert seq_p % PROMPT_TILE == 0 and nbs % SAMPLE_SEQS == 0 and seq_s == 8

    weights = _mixer_weights(norm_mix, w_in, w_conv, w_alpha_up, b_alpha, gla_norm, w_conv_out,
                             w_gla_out, w_o, norm_ffn, w_group_router, b_group_router,
                             w_expert_router, b_expert_router)

    ntp, nts = nbp * seq_p, nbs * seq_s
    assert ntp % COMBINE_TILE == 0 and nts % COMBINE_TILE == 0
    conv0 = jnp.zeros((nbp, CONV_WIDTH - 1, D_CONV), F32)
    gla0 = jnp.zeros((nbp, 1, 8, LANES), F32)
    flat = lambda a: a.reshape(-1, a.shape[-1])
    ids_of = lambda route_t: route_t[:, 2:4, :].transpose(1, 0, 2).reshape(2, -1)
    h_p, xn_p, rt_p, rtt_p, conv_p, gla_p = _mixer_call(x_prompt, conv0, gla0, weights, prompt=True)
    h_s, xn_s, rt_s, rtt_s, conv_s, gla_s = _mixer_call(x_sample, state_conv[0], state_gla[0],
                                                        weights, prompt=False)
    y2 = _moe_rows([(flat(xn_p), ids_of(rtt_p)), (flat(xn_s), ids_of(rtt_s))],
                   w_gate[0], w_up[0], w_down[0])
    nfin = norm_final[None, :]
    y_p = _combine_call(flat(h_p), y2, flat(rt_p), nfin, 0)
    y_s = _combine_call(flat(h_s), y2, flat(rt_s), nfin, ntp)
    return (y_p.reshape(nbp, seq_p, D_MODEL), y_s.reshape(nbs, seq_s, D_MODEL),
            conv_p[None], gla_p[None], conv_s[None], gla_s[None])
```

```python
import functools
import itertools

import jax
import jax.numpy as jnp
from jax import lax
from jax.experimental import pallas as pl
from jax.experimental.pallas import tpu as pltpu
from jax.experimental.pallas import tpu_sc as plsc

F32 = jnp.float32
BF16 = jnp.bfloat16

D_MODEL = 1024
D_CONV = 512
CONV_WIDTH = 3
GLA_HEADS = 4
GLA_DK = 128
GLA_DV = 256
GLA_RANK = 16
GLA_GATE_NORM = 16.0
GLA_CHUNK = 32
N_GROUPS = 4
EXPERTS_PER_GROUP = 8
N_EXPERTS = 32
D_EXPERT = 256
EPS = 1e-6

LANES = 128
V7X_VMEM_BYTES = 64 * 1024 * 1024
OFF_CB, OFF_CC, OFF_CH = 0, 512, 1024
OFF_Q, OFF_K, OFF_V, OFF_G = 1536, 2048, 2560, 3584
OFF_GA, OFF_GB, OFF_AL = 4608, 5632, 6656

PROMPT_TILE = 512
GLA_BLOCK = 256
SAMPLE_SEQS = 32
STATE_CHUNK = 4
STATE_OUT_SLOTS = 4
ROW_TILE = 256
EXPERT_RING = 6
EXPERT_AHEAD = 4
SC_WINDOW = 128
COMBINE_TILE = 1024
RANK_TILE = 1024
CAST_ROWS = 512
RANK_BLOCK = 512
VMEM_HEADROOM_BYTES = 6 * 1024 * 1024
MIXER_VMEM_LIMIT = V7X_VMEM_BYTES - VMEM_HEADROOM_BYTES
STREAM_VMEM_LIMIT = MIXER_VMEM_LIMIT


def _rmsnorm(x, g):
    ms = jnp.mean(x * x, axis=-1, keepdims=True)
    return (x * lax.rsqrt(ms + EPS)) * g


def _split_bf16(x):
    hi = x.astype(BF16)
    lo = (x - hi.astype(F32)).astype(BF16)
    return hi, lo


def _pack_bf16_pairs(x):
    n = x.shape[1] // 2
    lo = lax.bitcast_convert_type(x[:, :n].astype(BF16).astype(F32), jnp.uint32)
    hi = lax.bitcast_convert_type(x[:, n:].astype(BF16).astype(F32), jnp.uint32)
    return lax.shift_right_logical(lo, jnp.uint32(16)) | hi


def _unpack_bf16_pairs(w):
    lo = lax.bitcast_convert_type(lax.shift_left(w, jnp.uint32(16)), F32)
    hi = lax.bitcast_convert_type(w & jnp.uint32(0xFFFF0000), F32)
    return lo, hi


def _dot(a, b):
    return jnp.dot(a, b, preferred_element_type=F32)


def _chunk_causal(n, chunk):
    r = lax.broadcasted_iota(jnp.int32, (n, n), 0)
    c = lax.broadcasted_iota(jnp.int32, (n, n), 1)
    shift = chunk.bit_length() - 1
    assert chunk == 1 << shift
    same = lax.shift_right_arithmetic(r, shift) == lax.shift_right_arithmetic(c, shift)
    return same & (c <= r)


def _router_logits(xn2, wr_ref, br_ref):
    xh, xl = _split_bf16(xn2)
    part = _dot(xh, wr_ref[...]) + _dot(xl, wr_ref[...])
    return part[:, :LANES] + part[:, LANES:] + br_ref[...]


def _route(lg):
    n = lg.shape[0]
    n_rows = 40
    lt = lg.T[:n_rows, :]
    row_i = lax.broadcasted_iota(jnp.int32, (n_rows, n), 0)
    row = row_i.astype(F32)
    neg = -jnp.inf
    none = float(LANES)
    is_g = row_i < N_GROUPS
    gm = jnp.max(jnp.where(is_g, lt, neg), axis=0, keepdims=True)
    gs = jnp.sum(jnp.where(is_g, jnp.exp(lt - gm), 0.0), axis=0, keepdims=True)
    g_top = 1.0 / gs
    g_idx = jnp.min(jnp.where(is_g & (lt == gm), row, none), axis=0, keepdims=True)
    e_group = lax.shift_right_arithmetic(row_i - N_GROUPS, 3).astype(F32)
    sel = (row_i >= N_GROUPS) & (row_i < N_GROUPS + N_EXPERTS) & (e_group == g_idx)
    el = jnp.where(sel, lt, neg)
    e1 = jnp.max(el, axis=0, keepdims=True)
    i1 = jnp.min(jnp.where(sel & (el == e1), row, none), axis=0, keepdims=True)
    sel2 = sel & (row != i1)
    el2 = jnp.where(sel2, lt, neg)
    e2 = jnp.max(el2, axis=0, keepdims=True)
    i2 = jnp.min(jnp.where(sel2 & (el2 == e2), row, none), axis=0, keepdims=True)
    d = jnp.exp(e2 - e1)
    w1 = g_top / (1.0 + d)
    w2 = (g_top * d) / (1.0 + d)
    id1 = i1 - float(N_GROUPS)
    id2 = i2 - float(N_GROUPS)
    out_i = lax.broadcasted_iota(jnp.int32, (8, n), 0)
    route_t = jnp.where(out_i == 0, w1, jnp.where(out_i == 1, w2, jnp.where(out_i == 2, id1,
                        jnp.where(out_i == 3, id2, 0.0))))
    route = jnp.concatenate([route_t, jnp.zeros((LANES - 8, n), F32)], axis=0).T
    return route, route_t


def _mixer_stages(sub, nsub, x_ref, conv_in_ref, gla_in_ref,
                  nmix_ref, w_ref, wgate_ref, walr_ref, wconv_ref, walpha_ref, balpha_ref, gnorm_ref,
                  wco_ref, wgo_ref, wo_ref, nffn_ref, wr_ref, br_ref,
                  h_ref, xn2_ref, route_ref, route_t_ref, conv_out_ref, gla_out_ref,
                  s_ref, ubuf_ref, *state_ring, prompt, logits_out):
    if prompt:
        rows, chunk, nseq = GLA_BLOCK, GLA_CHUNK, 1
        r0 = sub * rows
        x = x_ref[0, r0:r0 + rows, :]
    else:
        nseq, chunk = SAMPLE_SEQS, x_ref.shape[1]
        rows = nseq * chunk
        r0 = 0
        x = x_ref[...].reshape(rows, D_MODEL)
        sin_ref, sout_ref, sem_in, sem_out = state_ring
        n_state_chunks = nseq // STATE_CHUNK
        step, n_steps = pl.program_id(0), pl.num_programs(0)

        def chunk_seqs(c, of_step):
            first = of_step * nseq + c * STATE_CHUNK
            return pl.ds(pl.multiple_of(first, STATE_CHUNK), STATE_CHUNK)

        def in_copy(c, of_step):
            return pltpu.make_async_copy(gla_in_ref.at[chunk_seqs(c, of_step)], sin_ref.at[c],
                                         sem_in.at[c])

        def out_copy(c, of_step):
            slot = c % STATE_OUT_SLOTS
            return pltpu.make_async_copy(sout_ref.at[slot], gla_out_ref.at[chunk_seqs(c, of_step)],
                                         sem_out.at[slot])

        @pl.when(step == 0)
        def _():
            for c in range(n_state_chunks):
                in_copy(c, step).start()
    nchunks = rows // chunk
    last_sub = sub == nsub - 1

    xn = _rmsnorm(x, nmix_ref[...]).astype(BF16)
    yield

    def proj(off, n):
        if off == OFF_AL:
            return _dot(xn, walr_ref[...])
        if off >= OFF_GA:
            return _dot(xn, wgate_ref[:, off - OFF_GA:off - OFF_GA + n])
        return _dot(xn, w_ref[:, off:off + n])

    alr = proj(OFF_AL, LANES).astype(BF16)
    lap = _dot(alr, walpha_ref[...])
    la = jax.nn.log_sigmoid(lap + balpha_ref[...]) * (1.0 / GLA_GATE_NORM)
    la_hi, la_lo = _split_bf16(la)
    causal = _chunk_causal(rows, chunk)
    tri = jnp.where(causal, 1.0, 0.0).astype(BF16)
    yield
    cc = proj(OFF_CC, D_CONV)
    ch = proj(OFF_CH, D_CONV)
    cb = proj(OFF_CB, D_CONV)
    q = proj(OFF_Q, D_CONV)
    k = proj(OFF_K, D_CONV)
    yield
    b = _dot(tri, la_hi) + _dot(tri, la_lo)

    u = cc * ch
    wc = wconv_ref[...]
    if prompt:
        ubuf_ref[0, 8 + r0:8 + r0 + rows, :] = u
        z = (wc[0:1] * ubuf_ref[0, 6 + r0:6 + r0 + rows, :]
             + wc[1:2] * ubuf_ref[0, 7 + r0:7 + r0 + rows, :] + wc[2:3] * u)
        if last_sub:
            tail = u[rows - 2:rows, :]
            ubuf_ref[0, 6:8, :] = tail
            conv_out_ref[0] = tail
    else:
        u3 = u.reshape(nseq, chunk, D_CONV)
        ubuf_ref[:, 6:8, :] = conv_in_ref[...]
        ubuf_ref[:, 8:8 + chunk, :] = u3
        z3 = (wc[0:1] * ubuf_ref[:, 6:6 + chunk, :] + wc[1:2] * ubuf_ref[:, 7:7 + chunk, :]
              + wc[2:3] * u3)
        z = z3.reshape(rows, D_CONV)
        conv_out_ref[...] = ubuf_ref[:, 6 + chunk:8 + chunk, :]
    ya = _dot((cb * z).astype(BF16), wco_ref[...])
    v = proj(OFF_V, GLA_HEADS * GLA_DV)
    yield

    b3 = b.reshape(nchunks, chunk, D_CONV)
    bl3 = b3[:, chunk - 1:chunk, :]
    qin = (q * (GLA_DK ** -0.5)) * jnp.exp(b)
    kin = k * jnp.exp(-b)
    kst = (k.reshape(nchunks, chunk, D_CONV) * jnp.exp(bl3 - b3)).reshape(rows, D_CONV)
    pad = (-rows) % LANES
    rows_p = rows + pad

    def transposed(a):
        if pad:
            a = jnp.concatenate([a, jnp.zeros((pad, a.shape[1]), F32)], axis=0)
        return a.T

    b_t = transposed(b)
    kst_t = transposed(kst) if prompt else None
    col_chunk = lax.shift_right_arithmetic(
        lax.broadcasted_iota(jnp.int32, (GLA_DK, rows_p), 1), chunk.bit_length() - 1)
    g = proj(OFF_G, GLA_HEADS * GLA_DV)
    yield
    heads = range(GLA_HEADS)
    vhs = [v[:, hd * GLA_DV:(hd + 1) * GLA_DV].astype(BF16) for hd in heads]
    scs = [lax.dot_general(qin[:, hd * GLA_DK:(hd + 1) * GLA_DK].astype(BF16),
                           kin[:, hd * GLA_DK:(hd + 1) * GLA_DK].astype(BF16),
                           (((1,), (1,)), ((), ())), preferred_element_type=F32) for hd in heads]
    ga = proj(OFF_GA, D_MODEL)
    yield
    upds = []
    for hd in heads if prompt else ():
        kst_h = kst_t[hd * GLA_DK:(hd + 1) * GLA_DK, :]
        stacked = jnp.concatenate(
            [jnp.where(col_chunk == n, kst_h, 0.0).astype(BF16) for n in range(nchunks)], axis=0)
        vh_p = jnp.concatenate([vhs[hd], jnp.zeros((pad, GLA_DV), BF16)], axis=0) if pad else vhs[hd]
        upds.append(_dot(stacked, vh_p))
    o_intras = [_dot(jnp.where(causal, scs[hd], 0.0).astype(BF16), vhs[hd]) for hd in heads]
    gb = proj(OFF_GB, D_MODEL)
    merge_a = jax.nn.sigmoid(ga) * ya
    gate_b = jax.nn.sigmoid(gb)
    yield
    def q_block(n, hd):
        return qin[n * chunk:(n + 1) * chunk, hd * GLA_DK:(hd + 1) * GLA_DK].astype(BF16)

    def decay_col(n, hd):
        last = n * chunk + chunk - 1
        return jnp.exp(b_t[hd * GLA_DK:(hd + 1) * GLA_DK, last:last + 1])

    o_inter = [[None] * nchunks for _ in heads]
    if prompt:
        for hd in heads:
            s_run = s_ref[hd]
            for n in range(nchunks):
                o_inter[hd][n] = _dot(q_block(n, hd), s_run.astype(BF16))
                s_run = s_run * decay_col(n, hd) + upds[hd][n * GLA_DK:(n + 1) * GLA_DK, :]
            s_ref[hd] = s_run
            if last_sub:
                gla_out_ref[0, hd] = s_run
    else:
        for c in range(n_state_chunks):
            in_copy(c, step).wait()
            if c >= STATE_OUT_SLOTS:
                out_copy(c - STATE_OUT_SLOTS, step).wait()
            else:
                @pl.when(step > 0)
                def _():
                    out_copy(c + n_state_chunks - STATE_OUT_SLOTS, step - 1).wait()
            for j in range(STATE_CHUNK):
                n = c * STATE_CHUNK + j
                for hd in heads:
                    s_prev = sin_ref[c, j, hd]
                    o_inter[hd][n] = _dot(q_block(n, hd), s_prev.astype(BF16))
                    ksb = kst[n * chunk:(n + 1) * chunk, hd * GLA_DK:(hd + 1) * GLA_DK].astype(BF16)
                    vb = v[n * chunk:(n + 1) * chunk, hd * GLA_DV:(hd + 1) * GLA_DV].astype(BF16)
                    upd = lax.dot_general(ksb, vb, (((0,), (0,)), ((), ())),
                                          preferred_element_type=F32)
                    sout_ref[c % STATE_OUT_SLOTS, j, hd] = s_prev * decay_col(n, hd) + upd
            out_copy(c, step).start()

        @pl.when(step + 1 < n_steps)
        def _():
            for c in range(n_state_chunks):
                in_copy(c, step + 1).start()

        @pl.when(step + 1 == n_steps)
        def _():
            for c in range(n_state_chunks - STATE_OUT_SLOTS, n_state_chunks):
                out_copy(c, step).wait()
    gated = []
    for hd in heads:
        v0 = hd * GLA_DV
        oh = _rmsnorm(o_intras[hd] + jnp.concatenate(o_inter[hd], axis=0), gnorm_ref[...])
        gh = g[:, v0:v0 + GLA_DV]
        gated.append(oh * (gh * jax.nn.sigmoid(gh)))
    yield
    yb = _dot(jnp.concatenate(gated, axis=1).astype(BF16), wgo_ref[...])
    yield

    m = merge_a + gate_b * yb
    hh = x + _dot(m.astype(BF16), wo_ref[...])
    yield
    xn2 = _rmsnorm(hh, nffn_ref[...])
    yield
    logits_out.append(_router_logits(xn2, wr_ref, br_ref))
    if prompt:
        h_ref[0, r0:r0 + rows, :] = hh
        xn2_ref[0, r0:r0 + rows, :] = _pack_bf16_pairs(xn2)
    else:
        h_ref[...] = hh.reshape(nseq, chunk, D_MODEL)
        xn2_ref[...] = _pack_bf16_pairs(xn2).reshape(nseq, chunk, D_MODEL // 2)


def _mixer_kernel(*refs, prompt):
    n_scratch = 2 if prompt else 6
    outs = refs[len(refs) - n_scratch - 6:len(refs) - n_scratch]
    route_ref, route_t_ref = outs[2], outs[3]
    if prompt:
        s_ref, ubuf_ref = refs[-2:]

        @pl.when(pl.program_id(1) == 0)
        def _():
            s_ref[...] = jnp.zeros_like(s_ref)
            ubuf_ref[...] = jnp.zeros_like(ubuf_ref)
    nsub = PROMPT_TILE // GLA_BLOCK if prompt else 1
    logits = []
    stages = [_mixer_stages(sub, nsub, *refs, prompt=prompt, logits_out=logits)
              for sub in range(nsub)]
    for _ in itertools.zip_longest(*stages):
        pass
    route, route_t = _route(jnp.concatenate(logits, axis=0))
    if prompt:
        route_ref[0] = route
    else:
        route_ref[...] = route.reshape(route_ref.shape)
    route_t_ref[0] = route_t


def _in_proj_cast_kernel(wt_hbm, main_ref, gates_ref, buf, sem):
    i, n = pl.program_id(0), pl.num_programs(0)
    n_main = OFF_GA // CAST_ROWS

    def slab_copy(step):
        row = jnp.where(step < n_main, step * CAST_ROWS, step * CAST_ROWS + GLA_RANK)
        slot = step % 2
        return pltpu.make_async_copy(
            wt_hbm.at[pl.ds(pl.multiple_of(row, 8), CAST_ROWS)], buf.at[slot], sem.at[slot])

    @pl.when(i == 0)
    def _():
        slab_copy(i).start()

    @pl.when(i + 1 < n)
    def _():
        slab_copy(i + 1).start()

    slab_copy(i).wait()
    w = buf[i % 2].T.astype(BF16)

    @pl.when(i < n_main)
    def _():
        main_ref[...] = w

    @pl.when(i >= n_main)
    def _():
        gates_ref[...] = w


def _in_proj_cast_call(wt):
    d_in, dm = wt.shape
    n_gate_cols = d_in - OFF_GA - GLA_RANK
    assert dm == D_MODEL and OFF_GA % CAST_ROWS == 0 and n_gate_cols % CAST_ROWS == 0
    assert GLA_RANK % 8 == 0
    n_main = OFF_GA // CAST_ROWS
    return pl.pallas_call(
        _in_proj_cast_kernel,
        grid=(n_main + n_gate_cols // CAST_ROWS,),
        in_specs=[pl.BlockSpec(memory_space=pl.ANY)],
        out_specs=[
            pl.BlockSpec((D_MODEL, CAST_ROWS), lambda i: (0, jnp.minimum(i, n_main - 1))),
            pl.BlockSpec((D_MODEL, CAST_ROWS), lambda i: (0, jnp.maximum(i - n_main, 0))),
        ],
        out_shape=[jax.ShapeDtypeStruct((D_MODEL, OFF_GA), BF16),
                   jax.ShapeDtypeStruct((D_MODEL, n_gate_cols), BF16)],
        scratch_shapes=[pltpu.VMEM((2, CAST_ROWS, D_MODEL), F32), pltpu.SemaphoreType.DMA((2,))],
        compiler_params=pltpu.CompilerParams(
            dimension_semantics=("arbitrary",), vmem_limit_bytes=STREAM_VMEM_LIMIT),
        name="in_proj_cast",
    )(wt)


def _mixer_weights(norm_mix, w_in, w_conv, w_alpha_up, b_alpha, gla_norm, w_conv_out, w_gla_out, w_o,
                   norm_ffn, w_group_router, b_group_router, w_expert_router, b_expert_router):
    wi = w_in[0]
    w_main, w_gates = _in_proj_cast_call(wi.T)
    w_alr = jnp.concatenate([wi[:, OFF_GA:OFF_GA + GLA_RANK],
                             jnp.zeros((D_MODEL, LANES - GLA_RANK), F32)], axis=1).astype(BF16)
    walpha = jnp.concatenate(
        [w_alpha_up[0], jnp.zeros((LANES - GLA_RANK, D_CONV), F32)], axis=0).astype(BF16)
    wr = jnp.concatenate(
        [w_group_router[0], w_expert_router[0],
         jnp.zeros((D_MODEL, LANES - N_GROUPS - N_EXPERTS), F32)], axis=1)
    wr_hi = wr.astype(BF16)
    wr_split = jnp.concatenate([wr_hi, (wr - wr_hi.astype(F32)).astype(BF16)], axis=1)
    br = jnp.concatenate([b_group_router[0], b_expert_router[0],
                          jnp.zeros((LANES - N_GROUPS - N_EXPERTS,), F32)])[None, :]
    return (norm_mix, w_main, w_gates, w_alr, w_conv[0], walpha, b_alpha, gla_norm,
            w_conv_out[0].astype(BF16), w_gla_out[0].astype(BF16), w_o[0].astype(BF16),
            norm_ffn, wr_split, br)


def _const_spec(shape):
    nd = len(shape)
    return pl.BlockSpec(shape, lambda *_: (0,) * nd, pipeline_mode=pl.Buffered(1))


def _mixer_call(x, conv_state, gla_state, weights, *, prompt):
    nb, seq, _ = x.shape
    if prompt:
        grid = (nb, seq // PROMPT_TILE)
        tok = lambda last: pl.BlockSpec((1, PROMPT_TILE, last), lambda b, i: (b, i, 0))
        conv_spec = pl.BlockSpec((1, CONV_WIDTH - 1, D_CONV), lambda b, i: (b, 0, 0))
        gla_spec = pl.BlockSpec((1, GLA_HEADS, GLA_DK, GLA_DV), lambda b, i: (b, 0, 0, 0))
        gla_in_spec = pl.BlockSpec((1, 1, 8, LANES), lambda b, i: (b, 0, 0, 0))
        scratch = [pltpu.VMEM((GLA_HEADS, GLA_DK, GLA_DV), F32),
                   pltpu.VMEM((1, PROMPT_TILE + 8, D_CONV), F32)]
        route_t_shape = (nb, 8, seq)
        route_t_spec = pl.BlockSpec((1, 8, PROMPT_TILE), lambda b, i: (b, 0, i))
    else:
        grid = (nb // SAMPLE_SEQS, 1)
        tok = lambda last: pl.BlockSpec((SAMPLE_SEQS, seq, last), lambda b, i: (b, 0, 0))
        conv_spec = pl.BlockSpec((SAMPLE_SEQS, CONV_WIDTH - 1, D_CONV), lambda b, i: (b, 0, 0))
        gla_spec = pl.BlockSpec(memory_space=pl.ANY)
        gla_in_spec = gla_spec
        state_chunk = (STATE_CHUNK, GLA_HEADS, GLA_DK, GLA_DV)
        scratch = [pltpu.VMEM((1, 8, LANES), F32),
                   pltpu.VMEM((SAMPLE_SEQS, 8 + seq, D_CONV), F32),
                   pltpu.VMEM((SAMPLE_SEQS // STATE_CHUNK,) + state_chunk, F32),
                   pltpu.VMEM((STATE_OUT_SLOTS,) + state_chunk, F32),
                   pltpu.SemaphoreType.DMA((SAMPLE_SEQS // STATE_CHUNK,)),
                   pltpu.SemaphoreType.DMA((STATE_OUT_SLOTS,))]
        route_t_shape = (nb // SAMPLE_SEQS, 8, SAMPLE_SEQS * seq)
        route_t_spec = pl.BlockSpec((1, 8, SAMPLE_SEQS * seq), lambda b, i: (b, 0, 0))
    out_shape = (jax.ShapeDtypeStruct((nb, seq, D_MODEL), F32),
                 jax.ShapeDtypeStruct((nb, seq, D_MODEL // 2), jnp.uint32),
                 jax.ShapeDtypeStruct((nb, seq, LANES), F32),
                 jax.ShapeDtypeStruct(route_t_shape, F32),
                 jax.ShapeDtypeStruct((nb, CONV_WIDTH - 1, D_CONV), F32),
                 jax.ShapeDtypeStruct((nb, GLA_HEADS, GLA_DK, GLA_DV), F32))
    return pl.pallas_call(
        functools.partial(_mixer_kernel, prompt=prompt),
        grid=grid,
        in_specs=[tok(D_MODEL), conv_spec, gla_in_spec] + [_const_spec(w.shape) for w in weights],
        out_specs=(tok(D_MODEL), tok(D_MODEL // 2), tok(LANES), route_t_spec, conv_spec, gla_spec),
        out_shape=out_shape,
        scratch_shapes=scratch,
        compiler_params=pltpu.CompilerParams(
            dimension_semantics=("arbitrary", "arbitrary"), vmem_limit_bytes=MIXER_VMEM_LIMIT),
        name="mixer_prompt" if prompt else "mixer_sample",
    )(x, conv_state, gla_state, *weights)


def _rank_kernel(eid_ref, rank_ref, cnt_ref, carry_ref):
    @pl.when(pl.program_id(0) == 0)
    def _():
        carry_ref[...] = jnp.zeros_like(carry_ref)

    n = RANK_BLOCK
    e_iota = lax.broadcasted_iota(jnp.int32, (N_EXPERTS, n), 0)
    r = lax.broadcasted_iota(jnp.int32, (n, n), 0)
    c = lax.broadcasted_iota(jnp.int32, (n, n), 1)
    before = jnp.where(r < c, 1.0, 0.0).astype(BF16)
    carry = carry_ref[...]
    for j in range(eid_ref.shape[1] // n):
        cols = slice(j * n, (j + 1) * n)
        oh0 = jnp.where(e_iota == eid_ref[0:1, cols], 1.0, 0.0)
        oh1 = jnp.where(e_iota == eid_ref[1:2, cols], 1.0, 0.0)
        cnt = oh0 + oh1
        base = carry + _dot(cnt.astype(BF16), before)
        rank0 = jnp.sum(oh0 * base, axis=0, keepdims=True)
        rank1 = jnp.sum(oh1 * base, axis=0, keepdims=True)
        rank_ref[:, cols] = jnp.concatenate([rank0, rank1], axis=0).astype(jnp.int32)
        carry = carry + jnp.sum(cnt, axis=1, keepdims=True)
    carry_ref[...] = carry
    cnt_ref[...] = jnp.broadcast_to(carry, cnt_ref.shape)


def _rank_call(eid):
    ntok = eid.shape[1]
    return pl.pallas_call(
        _rank_kernel,
        grid=(ntok // RANK_TILE,),
        in_specs=[pl.BlockSpec((2, RANK_TILE), lambda i: (0, i))],
        out_specs=(pl.BlockSpec((2, RANK_TILE), lambda i: (0, i)),
                   pl.BlockSpec((N_EXPERTS, LANES), lambda i: (0, 0))),
        out_shape=(jax.ShapeDtypeStruct((2, ntok), jnp.int32),
                   jax.ShapeDtypeStruct((N_EXPERTS, LANES), F32)),
        scratch_shapes=[pltpu.VMEM((N_EXPERTS, 1), F32)],
        compiler_params=pltpu.CompilerParams(dimension_semantics=("arbitrary",)),
        name="expert_rank",
    )(eid)


def _sc_mesh():
    return plsc.VectorSubcoreMesh(core_axis_name="c", subcore_axis_name="s")


def _sc_worker_windows(n_windows, body):
    info = plsc.get_sparse_core_info()
    n_workers = info.num_cores * info.num_subcores
    wid = lax.axis_index("s") * info.num_cores + lax.axis_index("c")

    @pl.loop(0, pl.cdiv(n_windows, n_workers))
    def _(j):
        win = wid + n_workers * j

        @pl.when(win < n_windows)
        def _():
            body(win)


def _dispatch_rows(sources, pos, n_rows):
    width, dtype = sources[0].shape[1], sources[0].dtype
    n_win = [src.shape[0] // SC_WINDOW for src in sources]
    first_win = [sum(n_win[:i]) for i in range(len(sources))]
    total_win = sum(n_win)
    pos3 = pos.reshape(2, total_win, SC_WINDOW).transpose(1, 0, 2)

    @functools.partial(pl.kernel, mesh=_sc_mesh(), name="moe_dispatch",
                       out_type=jax.ShapeDtypeStruct((n_rows, width), dtype),
                       scratch_types=[pltpu.VMEM((2, SC_WINDOW), jnp.int32),
                                      pltpu.VMEM((SC_WINDOW, width), dtype)])
    def run(*refs):
        src_hbm, (pos_hbm, out_hbm, idx_v, buf) = refs[:len(sources)], refs[len(sources):]

        def window(win):
            pltpu.sync_copy(pos_hbm.at[win], idx_v)
            for src, w0, nw in zip(src_hbm, first_win, n_win):
                @pl.when((win >= w0) & (win < w0 + nw))
                def _():
                    r0 = pl.multiple_of((win - w0) * SC_WINDOW, SC_WINDOW)
                    pltpu.sync_copy(src.at[pl.ds(r0, SC_WINDOW)], buf)

            pltpu.sync_copy(buf, out_hbm.at[idx_v.at[0]])
            pltpu.sync_copy(buf, out_hbm.at[idx_v.at[1]])

        _sc_worker_windows(total_win, window)

    return run(*sources, pos3)


def _return_rows(ys, idx):
    n_out, width = idx.shape[0], ys.shape[1]
    n_windows = n_out // SC_WINDOW

    @functools.partial(pl.kernel, mesh=_sc_mesh(), name="moe_return",
                       out_type=jax.ShapeDtypeStruct((n_out, width), ys.dtype),
                       scratch_types=[pltpu.VMEM((SC_WINDOW,), jnp.int32),
                                      pltpu.VMEM((SC_WINDOW, width), ys.dtype)])
    def run(ys_hbm, idx_hbm, out_hbm, idx_v, buf):
        def window(win):
            pltpu.sync_copy(idx_hbm.at[win], idx_v)
            pltpu.sync_copy(ys_hbm.at[idx_v], buf)
            r0 = pl.multiple_of(win * SC_WINDOW, SC_WINDOW)
            pltpu.sync_copy(buf, out_hbm.at[pl.ds(r0, SC_WINDOW)])

        _sc_worker_windows(n_windows, window)

    return run(ys, idx.reshape(n_windows, SC_WINDOW))


def _expert_kernel(first_ref, ntile_ref, nrow_ref, total_ref, xs_hbm, wg_ref, wu_ref, wd_ref, out_hbm,
                   xbuf, obuf, wg_s, wu_s, wd_s, sem_in, sem_out):
    e = pl.program_id(0)
    nt = ntile_ref[e]
    g0 = first_ref[e]
    total = total_ref[0]
    ring = EXPERT_RING

    def rows_of(g):
        return pl.ds(pl.multiple_of(g * ROW_TILE, ROW_TILE), ROW_TILE)

    def in_copy(g):
        slot = lax.rem(g, ring)
        return pltpu.make_async_copy(xs_hbm.at[rows_of(g)], xbuf.at[slot], sem_in.at[slot])

    def out_copy(g):
        slot = lax.rem(g, ring)
        return pltpu.make_async_copy(obuf.at[slot], out_hbm.at[rows_of(g)], sem_out.at[slot])

    @pl.when(e == 0)
    def _():
        for g in range(EXPERT_AHEAD):
            @pl.when(g < total)
            def _():
                in_copy(g).start()

    @pl.when(nt > 0)
    def _():
        wg_s[...] = wg_ref[0].astype(BF16)
        wu_s[...] = wu_ref[0].astype(BF16)
        wd_s[...] = wd_ref[0].astype(BF16)

    half = D_MODEL // 2
    sub = ROW_TILE // 2

    def tiles(gs):
        halves = []
        for g in gs:
            slot = lax.rem(g, ring)
            in_copy(g).wait()

            @pl.when(g + EXPERT_AHEAD < total)
            def _():
                in_copy(g + EXPERT_AHEAD).start()

            @pl.when(g >= ring)
            def _():
                out_copy(g - ring).wait()

            n_valid = nrow_ref[e] - (g - g0) * ROW_TILE
            row = lax.broadcasted_iota(jnp.int32, (ROW_TILE, 1), 0)
            lo, hi = _unpack_bf16_pairs(jnp.where(row < n_valid, xbuf[slot], jnp.uint32(0)))
            for r in (0, sub):
                xl, xh = lo[r:r + sub].astype(BF16), hi[r:r + sub].astype(BF16)
                gt = _dot(xl, wg_s[:half, :]) + _dot(xh, wg_s[half:, :])
                up = _dot(xl, wu_s[:half, :]) + _dot(xh, wu_s[half:, :])
                halves.append((slot, r, gt, up))
        for slot, r, gt, up in halves:
            act = (gt * jax.nn.sigmoid(gt)) * up
            obuf[slot, r:r + sub, :] = _pack_bf16_pairs(_dot(act.astype(BF16), wd_s[...]))
        for g in gs:
            out_copy(g).start()

    def pair(i, carry):
        tiles([g0 + 2 * i, g0 + 2 * i + 1])
        return carry

    lax.fori_loop(0, nt // 2, pair, 0)

    @pl.when(lax.rem(nt, 2) == 1)
    def _():
        tiles([g0 + nt - 1])

    @pl.when(e == pl.num_programs(0) - 1)
    def _():
        for k in range(1, ring + 1):
            @pl.when(total >= k)
            def _():
                out_copy(total - k).wait()


def _expert_call(first_tile, n_tiles_per_expert, n_rows_per_expert, total_tiles, xs,
                 w_gate, w_up, w_down):
    nrows = xs.shape[0]
    by_expert = lambda e, *_: (e, 0, 0)
    half_row = D_MODEL // 2
    return pl.pallas_call(
        _expert_kernel,
        grid_spec=pltpu.PrefetchScalarGridSpec(
            num_scalar_prefetch=4,
            grid=(N_EXPERTS,),
            in_specs=[pl.BlockSpec(memory_space=pl.ANY),
                      pl.BlockSpec((1, D_MODEL, D_EXPERT), by_expert),
                      pl.BlockSpec((1, D_MODEL, D_EXPERT), by_expert),
                      pl.BlockSpec((1, D_EXPERT, D_MODEL), by_expert)],
            out_specs=pl.BlockSpec(memory_space=pl.ANY),
            scratch_shapes=[pltpu.VMEM((EXPERT_RING, ROW_TILE, half_row), jnp.uint32),
                            pltpu.VMEM((EXPERT_RING, ROW_TILE, half_row), jnp.uint32),
                            pltpu.VMEM((D_MODEL, D_EXPERT), BF16),
                            pltpu.VMEM((D_MODEL, D_EXPERT), BF16),
                            pltpu.VMEM((D_EXPERT, D_MODEL), BF16),
                            pltpu.SemaphoreType.DMA((EXPERT_RING,)),
                            pltpu.SemaphoreType.DMA((EXPERT_RING,))]),
        out_shape=jax.ShapeDtypeStruct((nrows, half_row), jnp.uint32),
        compiler_params=pltpu.CompilerParams(
            dimension_semantics=("arbitrary",), vmem_limit_bytes=STREAM_VMEM_LIMIT),
        name="expert_mlp",
    )(first_tile, n_tiles_per_expert, n_rows_per_expert, total_tiles, xs, w_gate, w_up, w_down)


def _combine_kernel(h_ref, y2_ref, route_ref, nfin_ref, out_ref):
    route = route_ref[...]
    w1, w2 = route[:, 0:1], route[:, 1:2]
    lo1, hi1 = _unpack_bf16_pairs(y2_ref[0])
    lo2, hi2 = _unpack_bf16_pairs(y2_ref[1])
    y = h_ref[...] + jnp.concatenate([w1 * lo1 + w2 * lo2, w1 * hi1 + w2 * hi2], axis=1)
    out_ref[...] = _rmsnorm(y, nfin_ref[...])


def _combine_call(h, y2, route, nfin, first_token):
    n = h.shape[0]
    first_block = first_token // COMBINE_TILE
    return pl.pallas_call(
        _combine_kernel,
        grid=(n // COMBINE_TILE,),
        in_specs=[pl.BlockSpec((COMBINE_TILE, D_MODEL), lambda i: (i, 0)),
                  pl.BlockSpec((2, COMBINE_TILE, D_MODEL // 2), lambda i: (0, first_block + i, 0)),
                  pl.BlockSpec((COMBINE_TILE, LANES), lambda i: (i, 0)),
                  pl.BlockSpec((1, D_MODEL), lambda i: (0, 0))],
        out_specs=pl.BlockSpec((COMBINE_TILE, D_MODEL), lambda i: (i, 0)),
        out_shape=jax.ShapeDtypeStruct((n, D_MODEL), F32),
        compiler_params=pltpu.CompilerParams(
            dimension_semantics=("arbitrary",), vmem_limit_bytes=STREAM_VMEM_LIMIT),
        name="moe_combine",
    )(h, y2, route, nfin)


def _moe_rows(parts, w_gate, w_up, w_down):
    eid = jnp.concatenate([ids for _, ids in parts], axis=1).astype(jnp.int32)
    ntok = eid.shape[1]
    assert ntok % RANK_TILE == 0 and ntok % SC_WINDOW == 0
    rank, cnt = _rank_call(eid)
    counts = cnt[:, 0].astype(jnp.int32)
    padded = ((counts + ROW_TILE - 1) // ROW_TILE) * ROW_TILE
    ends = jnp.cumsum(padded)
    starts = ends - padded
    expert_iota = jnp.arange(N_EXPERTS, dtype=jnp.int32)[:, None, None]
    pos = rank + jnp.sum(jnp.where(eid[None] == expert_iota, starts[:, None, None], 0), axis=0)
    n_rows = (2 * ntok + N_EXPERTS * (ROW_TILE - 1)) // ROW_TILE * ROW_TILE
    xs = _dispatch_rows([rows for rows, _ in parts], pos, n_rows)
    ys = _expert_call(starts // ROW_TILE, padded // ROW_TILE, counts, ends[-1:] // ROW_TILE, xs,
                      w_gate, w_up, w_down)
    return _return_rows(ys, pos.reshape(-1)).reshape(2, ntok, D_MODEL // 2)


def kernel(x_prompt, x_sample, state_conv, state_gla, norm_mix, w_in, w_conv, w_alpha_up, b_alpha,
           gla_norm, w_conv_out, w_gla_out, w_o, norm_ffn, w_group_router, b_group_router,
           w_expert_router, b_expert_router, w_gate, w_up, w_down, norm_final):
    nbp, seq_p, _ = x_prompt.shape
    nbs, seq_s, _ = x_sample.shape
    assert norm_mix.shape[0] == 1, "single layer"
    assert seq_p % PROMPT_TILE == 0 and nbs % SAMPLE_SEQS == 0 and seq_s == 8

    weights = _mixer_weights(norm_mix, w_in, w_conv, w_alpha_up, b_alpha, gla_norm, w_conv_out,
                             w_gla_out, w_o, norm_ffn, w_group_router, b_group_router,
                             w_expert_router, b_expert_router)

    ntp, nts = nbp * seq_p, nbs * seq_s
    assert ntp % COMBINE_TILE == 0 and nts % COMBINE_TILE == 0
    conv0 = jnp.zeros((nbp, CONV_WIDTH - 1, D_CONV), F32)
    gla0 = jnp.zeros((nbp, 1, 8, LANES), F32)
    flat = lambda a: a.reshape(-1, a.shape[-1])
    ids_of = lambda route_t: route_t[:, 2:4, :].transpose(1, 0, 2).reshape(2, -1)
    h_p, xn_p, rt_p, rtt_p, conv_p, gla_p = _mixer_call(x_prompt, conv0, gla0, weights, prompt=True)
    h_s, xn_s, rt_s, rtt_s, conv_s, gla_s = _mixer_call(x_sample, state_conv[0], state_gla[0],
                                                        weights, prompt=False)
    y2 = _moe_rows([(flat(xn_p), ids_of(rtt_p)), (flat(xn_s), ids_of(rtt_s))],
                   w_gate[0], w_up[0], w_down[0])
    nfin = norm_final[None, :]
    y_p = _combine_call(flat(h_p), y2, flat(rt_p), nfin, 0)
    y_s = _combine_call(flat(h_s), y2, flat(rt_s), nfin, ntp)
    return (y_p.reshape(nbp, seq_p, D_MODEL), y_s.reshape(nbs, seq_s, D_MODEL),
            conv_p[None], gla_p[None], conv_s[None], gla_s[None])
```

```python
import functools
import itertools

import jax
import jax.numpy as jnp
from jax import lax
from jax.experimental import pallas as pl
from jax.experimental.pallas import tpu as pltpu
from jax.experimental.pallas import tpu_sc as plsc

F32 = jnp.float32
BF16 = jnp.bfloat16

D_MODEL = 1024
D_CONV = 512
CONV_WIDTH = 3
GLA_HEADS = 4
GLA_DK = 128
GLA_DV = 256
GLA_RANK = 16
GLA_GATE_NORM = 16.0
GLA_CHUNK = 32
N_GROUPS = 4
EXPERTS_PER_GROUP = 8
N_EXPERTS = 32
D_EXPERT = 256
EPS = 1e-6

LANES = 128
V7X_VMEM_BYTES = 64 * 1024 * 1024
OFF_CB, OFF_CC, OFF_CH = 0, 512, 1024
OFF_Q, OFF_K, OFF_V, OFF_G = 1536, 2048, 2560, 3584
OFF_GA, OFF_GB, OFF_AL = 4608, 5632, 6656

PROMPT_TILE = 512
GLA_BLOCK = 256
SAMPLE_SEQS = 32
STATE_CHUNK = 4
STATE_OUT_SLOTS = 4
ROW_TILE = 256
EXPERT_RING = 6
EXPERT_AHEAD = 4
SC_WINDOW = 128
COMBINE_TILE = 1024
RANK_TILE = 1024
CAST_ROWS = 512
RANK_BLOCK = 512
VMEM_HEADROOM_BYTES = 6 * 1024 * 1024
MIXER_VMEM_LIMIT = V7X_VMEM_BYTES - VMEM_HEADROOM_BYTES
STREAM_VMEM_LIMIT = MIXER_VMEM_LIMIT


def _rmsnorm(x, g):
    ms = jnp.mean(x * x, axis=-1, keepdims=True)
    return (x * lax.rsqrt(ms + EPS)) * g


def _split_bf16(x):
    hi = x.astype(BF16)
    lo = (x - hi.astype(F32)).astype(BF16)
    return hi, lo


def _pack_bf16_pairs(x):
    n = x.shape[1] // 2
    lo = lax.bitcast_convert_type(x[:, :n].astype(BF16).astype(F32), jnp.uint32)
    hi = lax.bitcast_convert_type(x[:, n:].astype(BF16).astype(F32), jnp.uint32)
    return lax.shift_right_logical(lo, jnp.uint32(16)) | hi


def _unpack_bf16_pairs(w):
    lo = lax.bitcast_convert_type(lax.shift_left(w, jnp.uint32(16)), F32)
    hi = lax.bitcast_convert_type(w & jnp.uint32(0xFFFF0000), F32)
    return lo, hi


def _dot(a, b):
    return jnp.dot(a, b, preferred_element_type=F32)


def _chunk_causal(n, chunk):
    r = lax.broadcasted_iota(jnp.int32, (n, n), 0)
    c = lax.broadcasted_iota(jnp.int32, (n, n), 1)
    shift = chunk.bit_length() - 1
    assert chunk == 1 << shift
    same = lax.shift_right_arithmetic(r, shift) == lax.shift_right_arithmetic(c, shift)
    return same & (c <= r)


def _router_logits(xn2, wr_ref, br_ref):
    xh, xl = _split_bf16(xn2)
    part = _dot(xh, wr_ref[...]) + _dot(xl, wr_ref[...])
    return part[:, :LANES] + part[:, LANES:] + br_ref[...]


def _route(lg):
    n = lg.shape[0]
    n_rows = 40
    lt = lg.T[:n_rows, :]
    row_i = lax.broadcasted_iota(jnp.int32, (n_rows, n), 0)
    row = row_i.astype(F32)
    neg = -jnp.inf
    none = float(LANES)
    is_g = row_i < N_GROUPS
    gm = jnp.max(jnp.where(is_g, lt, neg), axis=0, keepdims=True)
    gs = jnp.sum(jnp.where(is_g, jnp.exp(lt - gm), 0.0), axis=0, keepdims=True)
    g_top = 1.0 / gs
    g_idx = jnp.min(jnp.where(is_g & (lt == gm), row, none), axis=0, keepdims=True)
    e_group = lax.shift_right_arithmetic(row_i - N_GROUPS, 3).astype(F32)
    sel = (row_i >= N_GROUPS) & (row_i < N_GROUPS + N_EXPERTS) & (e_group == g_idx)
    el = jnp.where(sel, lt, neg)
    e1 = jnp.max(el, axis=0, keepdims=True)
    i1 = jnp.min(jnp.where(sel & (el == e1), row, none), axis=0, keepdims=True)
    sel2 = sel & (row != i1)
    el2 = jnp.where(sel2, lt, neg)
    e2 = jnp.max(el2, axis=0, keepdims=True)
    i2 = jnp.min(jnp.where(sel2 & (el2 == e2), row, none), axis=0, keepdims=True)
    d = jnp.exp(e2 - e1)
    w1 = g_top / (1.0 + d)
    w2 = (g_top * d) / (1.0 + d)
    id1 = i1 - float(N_GROUPS)
    id2 = i2 - float(N_GROUPS)
    out_i = lax.broadcasted_iota(jnp.int32, (8, n), 0)
    route_t = jnp.where(out_i == 0, w1, jnp.where(out_i == 1, w2, jnp.where(out_i == 2, id1,
                        jnp.where(out_i == 3, id2, 0.0))))
    route = jnp.concatenate([route_t, jnp.zeros((LANES - 8, n), F32)], axis=0).T
    return route, route_t


def _mixer_stages(sub, nsub, x_ref, conv_in_ref, gla_in_ref,
                  nmix_ref, w_ref, wgate_ref, walr_ref, wconv_ref, walpha_ref, balpha_ref, gnorm_ref,
                  wco_ref, wgo_ref, wo_ref, nffn_ref, wr_ref, br_ref,
                  h_ref, xn2_ref, route_ref, route_t_ref, conv_out_ref, gla_out_ref,
                  s_ref, ubuf_ref, *state_ring, prompt, logits_out):
    if prompt:
        rows, chunk, nseq = GLA_BLOCK, GLA_CHUNK, 1
        r0 = sub * rows
        x = x_ref[0, r0:r0 + rows, :]
    else:
        nseq, chunk = SAMPLE_SEQS, x_ref.shape[1]
        rows = nseq * chunk
        r0 = 0
        x = x_ref[...].reshape(rows, D_MODEL)
        sin_ref, sout_ref, sem_in, sem_out = state_ring
        n_state_chunks = nseq // STATE_CHUNK
        step, n_steps = pl.program_id(0), pl.num_programs(0)

        def chunk_seqs(c, of_step):
            first = of_step * nseq + c * STATE_CHUNK
            return pl.ds(pl.multiple_of(first, STATE_CHUNK), STATE_CHUNK)

        def in_copy(c, of_step):
            return pltpu.make_async_copy(gla_in_ref.at[chunk_seqs(c, of_step)], sin_ref.at[c],
                                         sem_in.at[c])

        def out_copy(c, of_step):
            slot = c % STATE_OUT_SLOTS
            return pltpu.make_async_copy(sout_ref.at[slot], gla_out_ref.at[chunk_seqs(c, of_step)],
                                         sem_out.at[slot])

        @pl.when(step == 0)
        def _():
            for c in range(n_state_chunks):
                in_copy(c, step).start()
    nchunks = rows // chunk
    last_sub = sub == nsub - 1

    xn = _rmsnorm(x, nmix_ref[...]).astype(BF16)
    yield

    def proj(off, n):
        if off == OFF_AL:
            return _dot(xn, walr_ref[...])
        if off >= OFF_GA:
            return _dot(xn, wgate_ref[:, off - OFF_GA:off - OFF_GA + n])
        return _dot(xn, w_ref[:, off:off + n])

    alr = proj(OFF_AL, LANES).astype(BF16)
    lap = _dot(alr, walpha_ref[...])
    la = jax.nn.log_sigmoid(lap + balpha_ref[...]) * (1.0 / GLA_GATE_NORM)
    la_hi, la_lo = _split_bf16(la)
    causal = _chunk_causal(rows, chunk)
    tri = jnp.where(causal, 1.0, 0.0).astype(BF16)
    yield
    cc = proj(OFF_CC, D_CONV)
    ch = proj(OFF_CH, D_CONV)
    cb = proj(OFF_CB, D_CONV)
    q = proj(OFF_Q, D_CONV)
    k = proj(OFF_K, D_CONV)
    yield
    b = _dot(tri, la_hi) + _dot(tri, la_lo)

    u = cc * ch
    wc = wconv_ref[...]
    if prompt:
        ubuf_ref[0, 8 + r0:8 + r0 + rows, :] = u
        z = (wc[0:1] * ubuf_ref[0, 6 + r0:6 + r0 + rows, :]
             + wc[1:2] * ubuf_ref[0, 7 + r0:7 + r0 + rows, :] + wc[2:3] * u)
        if last_sub:
            tail = u[rows - 2:rows, :]
            ubuf_ref[0, 6:8, :] = tail
            conv_out_ref[0] = tail
    else:
        u3 = u.reshape(nseq, chunk, D_CONV)
        ubuf_ref[:, 6:8, :] = conv_in_ref[...]
        ubuf_ref[:, 8:8 + chunk, :] = u3
        z3 = (wc[0:1] * ubuf_ref[:, 6:6 + chunk, :] + wc[1:2] * ubuf_ref[:, 7:7 + chunk, :]
              + wc[2:3] * u3)
        z = z3.reshape(rows, D_CONV)
        conv_out_ref[...] = ubuf_ref[:, 6 + chunk:8 + chunk, :]
    ya = _dot((cb * z).astype(BF16), wco_ref[...])
    v = proj(OFF_V, GLA_HEADS * GLA_DV)
    yield

    b3 = b.reshape(nchunks, chunk, D_CONV)
    bl3 = b3[:, chunk - 1:chunk, :]
    qin = (q * (GLA_DK ** -0.5)) * jnp.exp(b)
    kin = k * jnp.exp(-b)
    kst = (k.reshape(nchunks, chunk, D_CONV) * jnp.exp(bl3 - b3)).reshape(rows, D_CONV)
    pad = (-rows) % LANES
    rows_p = rows + pad

    def transposed(a):
        if pad:
            a = jnp.concatenate([a, jnp.zeros((pad, a.shape[1]), F32)], axis=0)
        return a.T

    b_t = transposed(b)
    kst_t = transposed(kst) if prompt else None
    col_chunk = lax.shift_right_arithmetic(
        lax.broadcasted_iota(jnp.int32, (GLA_DK, rows_p), 1), chunk.bit_length() - 1)
    g = proj(OFF_G, GLA_HEADS * GLA_DV)
    yield
    heads = range(GLA_HEADS)
    vhs = [v[:, hd * GLA_DV:(hd + 1) * GLA_DV].astype(BF16) for hd in heads]
    scs = [lax.dot_general(qin[:, hd * GLA_DK:(hd + 1) * GLA_DK].astype(BF16),
                           kin[:, hd * GLA_DK:(hd + 1) * GLA_DK].astype(BF16),
                           (((1,), (1,)), ((), ())), preferred_element_type=F32) for hd in heads]
    ga = proj(OFF_GA, D_MODEL)
    yield
    upds = []
    for hd in heads if prompt else ():
        kst_h = kst_t[hd * GLA_DK:(hd + 1) * GLA_DK, :]
        stacked = jnp.concatenate(
            [jnp.where(col_chunk == n, kst_h, 0.0).astype(BF16) for n in range(nchunks)], axis=0)
        vh_p = jnp.concatenate([vhs[hd], jnp.zeros((pad, GLA_DV), BF16)], axis=0) if pad else vhs[hd]
        upds.append(_dot(stacked, vh_p))
    o_intras = [_dot(jnp.where(causal, scs[hd], 0.0).astype(BF16), vhs[hd]) for hd in heads]
    gb = proj(OFF_GB, D_MODEL)
    merge_a = jax.nn.sigmoid(ga) * ya
    gate_b = jax.nn.sigmoid(gb)
    yield
    def q_block(n, hd):
        return qin[n * chunk:(n + 1) * chunk, hd * GLA_DK:(hd + 1) * GLA_DK].astype(BF16)

    def decay_col(n, hd):
        last = n * chunk + chunk - 1
        return jnp.exp(b_t[hd * GLA_DK:(hd + 1) * GLA_DK, last:last + 1])

    o_inter = [[None] * nchunks for _ in heads]
    if prompt:
        for hd in heads:
            s_run = s_ref[hd]
            for n in range(nchunks):
                o_inter[hd][n] = _dot(q_block(n, hd), s_run.astype(BF16))
                s_run = s_run * decay_col(n, hd) + upds[hd][n * GLA_DK:(n + 1) * GLA_DK, :]
            s_ref[hd] = s_run
            if last_sub:
                gla_out_ref[0, hd] = s_run
    else:
        for c in range(n_state_chunks):
            in_copy(c, step).wait()
            if c >= STATE_OUT_SLOTS:
                out_copy(c - STATE_OUT_SLOTS, step).wait()
            else:
                @pl.when(step > 0)
                def _():
                    out_copy(c + n_state_chunks - STATE_OUT_SLOTS, step - 1).wait()
            for j in range(STATE_CHUNK):
                n = c * STATE_CHUNK + j
                for hd in heads:
                    s_prev = sin_ref[c, j, hd]
                    o_inter[hd][n] = _dot(q_block(n, hd), s_prev.astype(BF16))
                    ksb = kst[n * chunk:(n + 1) * chunk, hd * GLA_DK:(hd + 1) * GLA_DK].astype(BF16)
                    vb = v[n * chunk:(n + 1) * chunk, hd * GLA_DV:(hd + 1) * GLA_DV].astype(BF16)
                    upd = lax.dot_general(ksb, vb, (((0,), (0,)), ((), ())),
                                          preferred_element_type=F32)
                    sout_ref[c % STATE_OUT_SLOTS, j, hd] = s_prev * decay_col(n, hd) + upd
            out_copy(c, step).start()

        @pl.when(step + 1 < n_steps)
        def _():
            for c in range(n_state_chunks):
                in_copy(c, step + 1).start()

        @pl.when(step + 1 == n_steps)
        def _():
            for c in range(n_state_chunks - STATE_OUT_SLOTS, n_state_chunks):
                out_copy(c, step).wait()
    gated = []
    for hd in heads:
        v0 = hd * GLA_DV
        oh = _rmsnorm(o_intras[hd] + jnp.concatenate(o_inter[hd], axis=0), gnorm_ref[...])
        gh = g[:, v0:v0 + GLA_DV]
        gated.append(oh * (gh * jax.nn.sigmoid(gh)))
    yield
    yb = _dot(jnp.concatenate(gated, axis=1).astype(BF16), wgo_ref[...])
    yield

    m = merge_a + gate_b * yb
    hh = x + _dot(m.astype(BF16), wo_ref[...])
    yield
    xn2 = _rmsnorm(hh, nffn_ref[...])
    yield
    logits_out.append(_router_logits(xn2, wr_ref, br_ref))
    if prompt:
        h_ref[0, r0:r0 + rows, :] = hh
        xn2_ref[0, r0:r0 + rows, :] = _pack_bf16_pairs(xn2)
    else:
        h_ref[...] = hh.reshape(nseq, chunk, D_MODEL)
        xn2_ref[...] = _pack_bf16_pairs(xn2).reshape(nseq, chunk, D_MODEL // 2)


def _mixer_kernel(*refs, prompt):
    n_scratch = 2 if prompt else 6
    outs = refs[len(refs) - n_scratch - 6:len(refs) - n_scratch]
    route_ref, route_t_ref = outs[2], outs[3]
    if prompt:
        s_ref, ubuf_ref = refs[-2:]

        @pl.when(pl.program_id(1) == 0)
        def _():
            s_ref[...] = jnp.zeros_like(s_ref)
            ubuf_ref[...] = jnp.zeros_like(ubuf_ref)
    nsub = PROMPT_TILE // GLA_BLOCK if prompt else 1
    logits = []
    stages = [_mixer_stages(sub, nsub, *refs, prompt=prompt, logits_out=logits)
              for sub in range(nsub)]
    for _ in itertools.zip_longest(*stages):
        pass
    route, route_t = _route(jnp.concatenate(logits, axis=0))
    if prompt:
        route_ref[0] = route
    else:
        route_ref[...] = route.reshape(route_ref.shape)
    route_t_ref[0] = route_t


def _in_proj_cast_kernel(wt_hbm, main_ref, gates_ref, buf, sem):
    i, n = pl.program_id(0), pl.num_programs(0)
    n_main = OFF_GA // CAST_ROWS

    def slab_copy(step):
        row = jnp.where(step < n_main, step * CAST_ROWS, step * CAST_ROWS + GLA_RANK)
        slot = step % 2
        return pltpu.make_async_copy(
            wt_hbm.at[pl.ds(pl.multiple_of(row, 8), CAST_ROWS)], buf.at[slot], sem.at[slot])

    @pl.when(i == 0)
    def _():
        slab_copy(i).start()

    @pl.when(i + 1 < n)
    def _():
        slab_copy(i + 1).start()

    slab_copy(i).wait()
    w = buf[i % 2].T.astype(BF16)

    @pl.when(i < n_main)
    def _():
        main_ref[...] = w

    @pl.when(i >= n_main)
    def _():
        gates_ref[...] = w


def _in_proj_cast_call(wt):
    d_in, dm = wt.shape
    n_gate_cols = d_in - OFF_GA - GLA_RANK
    assert dm == D_MODEL and OFF_GA % CAST_ROWS == 0 and n_gate_cols % CAST_ROWS == 0
    assert GLA_RANK % 8 == 0
    n_main = OFF_GA // CAST_ROWS
    return pl.pallas_call(
        _in_proj_cast_kernel,
        grid=(n_main + n_gate_cols // CAST_ROWS,),
        in_specs=[pl.BlockSpec(memory_space=pl.ANY)],
        out_specs=[
            pl.BlockSpec((D_MODEL, CAST_ROWS), lambda i: (0, jnp.minimum(i, n_main - 1))),
            pl.BlockSpec((D_MODEL, CAST_ROWS), lambda i: (0, jnp.maximum(i - n_main, 0))),
        ],
        out_shape=[jax.ShapeDtypeStruct((D_MODEL, OFF_GA), BF16),
                   jax.ShapeDtypeStruct((D_MODEL, n_gate_cols), BF16)],
        scratch_shapes=[pltpu.VMEM((2, CAST_ROWS, D_MODEL), F32), pltpu.SemaphoreType.DMA((2,))],
        compiler_params=pltpu.CompilerParams(
            dimension_semantics=("arbitrary",), vmem_limit_bytes=STREAM_VMEM_LIMIT),
        name="in_proj_cast",
    )(wt)


def _mixer_weights(norm_mix, w_in, w_conv, w_alpha_up, b_alpha, gla_norm, w_conv_out, w_gla_out, w_o,
                   norm_ffn, w_group_router, b_group_router, w_expert_router, b_expert_router):
    wt = w_in[0].T
    w_main, w_gates = _in_proj_cast_call(wt)
    w_alr = jnp.concatenate([wt[OFF_GA:OFF_GA + GLA_RANK],
                             jnp.zeros((LANES - GLA_RANK, D_MODEL), F32)], axis=0).T.astype(BF16)
    walpha = jnp.concatenate(
        [w_alpha_up[0], jnp.zeros((LANES - GLA_RANK, D_CONV), F32)], axis=0).astype(BF16)
    wr = jnp.concatenate(
        [w_group_router[0], w_expert_router[0],
         jnp.zeros((D_MODEL, LANES - N_GROUPS - N_EXPERTS), F32)], axis=1)
    wr_hi = wr.astype(BF16)
    wr_split = jnp.concatenate([wr_hi, (wr - wr_hi.astype(F32)).astype(BF16)], axis=1)
    br = jnp.concatenate([b_group_router[0], b_expert_router[0],
                          jnp.zeros((LANES - N_GROUPS - N_EXPERTS,), F32)])[None, :]
    return (norm_mix, w_main, w_gates, w_alr, w_conv[0], walpha, b_alpha, gla_norm,
            w_conv_out[0].astype(BF16), w_gla_out[0].astype(BF16), w_o[0].astype(BF16),
            norm_ffn, wr_split, br)


def _const_spec(shape):
    nd = len(shape)
    return pl.BlockSpec(shape, lambda *_: (0,) * nd, pipeline_mode=pl.Buffered(1))


def _mixer_call(x, conv_state, gla_state, weights, *, prompt):
    nb, seq, _ = x.shape
    if prompt:
        grid = (nb, seq // PROMPT_TILE)
        tok = lambda last: pl.BlockSpec((1, PROMPT_TILE, last), lambda b, i: (b, i, 0))
        conv_spec = pl.BlockSpec((1, CONV_WIDTH - 1, D_CONV), lambda b, i: (b, 0, 0))
        gla_spec = pl.BlockSpec((1, GLA_HEADS, GLA_DK, GLA_DV), lambda b, i: (b, 0, 0, 0))
        gla_in_spec = pl.BlockSpec((1, 1, 8, LANES), lambda b, i: (b, 0, 0, 0))
        scratch = [pltpu.VMEM((GLA_HEADS, GLA_DK, GLA_DV), F32),
                   pltpu.VMEM((1, PROMPT_TILE + 8, D_CONV), F32)]
        route_t_shape = (nb, 8, seq)
        route_t_spec = pl.BlockSpec((1, 8, PROMPT_TILE), lambda b, i: (b, 0, i))
    else:
        grid = (nb // SAMPLE_SEQS, 1)
        tok = lambda last: pl.BlockSpec((SAMPLE_SEQS, seq, last), lambda b, i: (b, 0, 0))
        conv_spec = pl.BlockSpec((SAMPLE_SEQS, CONV_WIDTH - 1, D_CONV), lambda b, i: (b, 0, 0))
        gla_spec = pl.BlockSpec(memory_space=pl.ANY)
        gla_in_spec = gla_spec
        state_chunk = (STATE_CHUNK, GLA_HEADS, GLA_DK, GLA_DV)
        scratch = [pltpu.VMEM((1, 8, LANES), F32),
                   pltpu.VMEM((SAMPLE_SEQS, 8 + seq, D_CONV), F32),
                   pltpu.VMEM((SAMPLE_SEQS // STATE_CHUNK,) + state_chunk, F32),
                   pltpu.VMEM((STATE_OUT_SLOTS,) + state_chunk, F32),
                   pltpu.SemaphoreType.DMA((SAMPLE_SEQS // STATE_CHUNK,)),
                   pltpu.SemaphoreType.DMA((STATE_OUT_SLOTS,))]
        route_t_shape = (nb // SAMPLE_SEQS, 8, SAMPLE_SEQS * seq)
        route_t_spec = pl.BlockSpec((1, 8, SAMPLE_SEQS * seq), lambda b, i: (b, 0, 0))
    out_shape = (jax.ShapeDtypeStruct((nb, seq, D_MODEL), F32),
                 jax.ShapeDtypeStruct((nb, seq, D_MODEL // 2), jnp.uint32),
                 jax.ShapeDtypeStruct((nb, seq, LANES), F32),
                 jax.ShapeDtypeStruct(route_t_shape, F32),
                 jax.ShapeDtypeStruct((nb, CONV_WIDTH - 1, D_CONV), F32),
                 jax.ShapeDtypeStruct((nb, GLA_HEADS, GLA_DK, GLA_DV), F32))
    return pl.pallas_call(
        functools.partial(_mixer_kernel, prompt=prompt),
        grid=grid,
        in_specs=[tok(D_MODEL), conv_spec, gla_in_spec] + [_const_spec(w.shape) for w in weights],
        out_specs=(tok(D_MODEL), tok(D_MODEL // 2), tok(LANES), route_t_spec, conv_spec, gla_spec),
        out_shape=out_shape,
        scratch_shapes=scratch,
        compiler_params=pltpu.CompilerParams(
            dimension_semantics=("arbitrary", "arbitrary"), vmem_limit_bytes=MIXER_VMEM_LIMIT),
        name="mixer_prompt" if prompt else "mixer_sample",
    )(x, conv_state, gla_state, *weights)


def _rank_kernel(eid_ref, rank_ref, cnt_ref, carry_ref):
    @pl.when(pl.program_id(0) == 0)
    def _():
        carry_ref[...] = jnp.zeros_like(carry_ref)

    n = RANK_BLOCK
    e_iota = lax.broadcasted_iota(jnp.int32, (N_EXPERTS, n), 0)
    r = lax.broadcasted_iota(jnp.int32, (n, n), 0)
    c = lax.broadcasted_iota(jnp.int32, (n, n), 1)
    before = jnp.where(r < c, 1.0, 0.0).astype(BF16)
    carry = carry_ref[...]
    for j in range(eid_ref.shape[1] // n):
        cols = slice(j * n, (j + 1) * n)
        oh0 = jnp.where(e_iota == eid_ref[0:1, cols], 1.0, 0.0)
        oh1 = jnp.where(e_iota == eid_ref[1:2, cols], 1.0, 0.0)
        cnt = oh0 + oh1
        base = carry + _dot(cnt.astype(BF16), before)
        rank0 = jnp.sum(oh0 * base, axis=0, keepdims=True)
        rank1 = jnp.sum(oh1 * base, axis=0, keepdims=True)
        rank_ref[:, cols] = jnp.concatenate([rank0, rank1], axis=0).astype(jnp.int32)
        carry = carry + jnp.sum(cnt, axis=1, keepdims=True)
    carry_ref[...] = carry
    cnt_ref[...] = jnp.broadcast_to(carry, cnt_ref.shape)


def _rank_call(eid):
    ntok = eid.shape[1]
    return pl.pallas_call(
        _rank_kernel,
        grid=(ntok // RANK_TILE,),
        in_specs=[pl.BlockSpec((2, RANK_TILE), lambda i: (0, i))],
        out_specs=(pl.BlockSpec((2, RANK_TILE), lambda i: (0, i)),
                   pl.BlockSpec((N_EXPERTS, LANES), lambda i: (0, 0))),
        out_shape=(jax.ShapeDtypeStruct((2, ntok), jnp.int32),
                   jax.ShapeDtypeStruct((N_EXPERTS, LANES), F32)),
        scratch_shapes=[pltpu.VMEM((N_EXPERTS, 1), F32)],
        compiler_params=pltpu.CompilerParams(dimension_semantics=("arbitrary",)),
        name="expert_rank",
    )(eid)


def _sc_mesh():
    return plsc.VectorSubcoreMesh(core_axis_name="c", subcore_axis_name="s")


def _sc_worker_windows(n_windows, body):
    info = plsc.get_sparse_core_info()
    n_workers = info.num_cores * info.num_subcores
    wid = lax.axis_index("s") * info.num_cores + lax.axis_index("c")

    @pl.loop(0, pl.cdiv(n_windows, n_workers))
    def _(j):
        win = wid + n_workers * j

        @pl.when(win < n_windows)
        def _():
            body(win)


def _dispatch_rows(sources, pos, n_rows):
    width, dtype = sources[0].shape[1], sources[0].dtype
    n_win = [src.shape[0] // SC_WINDOW for src in sources]
    first_win = [sum(n_win[:i]) for i in range(len(sources))]
    total_win = sum(n_win)
    pos3 = pos.reshape(2, total_win, SC_WINDOW).transpose(1, 0, 2)

    @functools.partial(pl.kernel, mesh=_sc_mesh(), name="moe_dispatch",
                       out_type=jax.ShapeDtypeStruct((n_rows, width), dtype),
                       scratch_types=[pltpu.VMEM((2, SC_WINDOW), jnp.int32),
                                      pltpu.VMEM((SC_WINDOW, width), dtype)])
    def run(*refs):
        src_hbm, (pos_hbm, out_hbm, idx_v, buf) = refs[:len(sources)], refs[len(sources):]

        def window(win):
            pltpu.sync_copy(pos_hbm.at[win], idx_v)
            for src, w0, nw in zip(src_hbm, first_win, n_win):
                @pl.when((win >= w0) & (win < w0 + nw))
                def _():
                    r0 = pl.multiple_of((win - w0) * SC_WINDOW, SC_WINDOW)
                    pltpu.sync_copy(src.at[pl.ds(r0, SC_WINDOW)], buf)

            pltpu.sync_copy(buf, out_hbm.at[idx_v.at[0]])
            pltpu.sync_copy(buf, out_hbm.at[idx_v.at[1]])

        _sc_worker_windows(total_win, window)

    return run(*sources, pos3)


def _return_rows(ys, idx):
    n_out, width = idx.shape[0], ys.shape[1]
    n_windows = n_out // SC_WINDOW

    @functools.partial(pl.kernel, mesh=_sc_mesh(), name="moe_return",
                       out_type=jax.ShapeDtypeStruct((n_out, width), ys.dtype),
                       scratch_types=[pltpu.VMEM((SC_WINDOW,), jnp.int32),
                                      pltpu.VMEM((SC_WINDOW, width), ys.dtype)])
    def run(ys_hbm, idx_hbm, out_hbm, idx_v, buf):
        def window(win):
            pltpu.sync_copy(idx_hbm.at[win], idx_v)
            pltpu.sync_copy(ys_hbm.at[idx_v], buf)
            r0 = pl.multiple_of(win * SC_WINDOW, SC_WINDOW)
            pltpu.sync_copy(buf, out_hbm.at[pl.ds(r0, SC_WINDOW)])

        _sc_worker_windows(n_windows, window)

    return run(ys, idx.reshape(n_windows, SC_WINDOW))


def _expert_kernel(first_ref, ntile_ref, nrow_ref, total_ref, xs_hbm, wg_ref, wu_ref, wd_ref, out_hbm,
                   xbuf, obuf, wg_s, wu_s, wd_s, sem_in, sem_out):
    e = pl.program_id(0)
    nt = ntile_ref[e]
    g0 = first_ref[e]
    total = total_ref[0]
    ring = EXPERT_RING

    def rows_of(g):
        return pl.ds(pl.multiple_of(g * ROW_TILE, ROW_TILE), ROW_TILE)

    def in_copy(g):
        slot = lax.rem(g, ring)
        return pltpu.make_async_copy(xs_hbm.at[rows_of(g)], xbuf.at[slot], sem_in.at[slot])

    def out_copy(g):
        slot = lax.rem(g, ring)
        return pltpu.make_async_copy(obuf.at[slot], out_hbm.at[rows_of(g)], sem_out.at[slot])

    @pl.when(e == 0)
    def _():
        for g in range(EXPERT_AHEAD):
            @pl.when(g < total)
            def _():
                in_copy(g).start()

    @pl.when(nt > 0)
    def _():
        wg_s[...] = wg_ref[0].astype(BF16)
        wu_s[...] = wu_ref[0].astype(BF16)
        wd_s[...] = wd_ref[0].astype(BF16)

    half = D_MODEL // 2
    sub = ROW_TILE // 2

    def tiles(gs):
        halves = []
        for g in gs:
            slot = lax.rem(g, ring)
            in_copy(g).wait()

            @pl.when(g + EXPERT_AHEAD < total)
            def _():
                in_copy(g + EXPERT_AHEAD).start()

            @pl.when(g >= ring)
            def _():
                out_copy(g - ring).wait()

            n_valid = nrow_ref[e] - (g - g0) * ROW_TILE
            row = lax.broadcasted_iota(jnp.int32, (ROW_TILE, 1), 0)
            lo, hi = _unpack_bf16_pairs(jnp.where(row < n_valid, xbuf[slot], jnp.uint32(0)))
            for r in (0, sub):
                xl, xh = lo[r:r + sub].astype(BF16), hi[r:r + sub].astype(BF16)
                gt = _dot(xl, wg_s[:half, :]) + _dot(xh, wg_s[half:, :])
                up = _dot(xl, wu_s[:half, :]) + _dot(xh, wu_s[half:, :])
                halves.append((slot, r, gt, up))
        for slot, r, gt, up in halves:
            act = (gt * jax.nn.sigmoid(gt)) * up
            obuf[slot, r:r + sub, :] = _pack_bf16_pairs(_dot(act.astype(BF16), wd_s[...]))
        for g in gs:
            out_copy(g).start()

    def pair(i, carry):
        tiles([g0 + 2 * i, g0 + 2 * i + 1])
        return carry

    lax.fori_loop(0, nt // 2, pair, 0)

    @pl.when(lax.rem(nt, 2) == 1)
    def _():
        tiles([g0 + nt - 1])

    @pl.when(e == pl.num_programs(0) - 1)
    def _():
        for k in range(1, ring + 1):
            @pl.when(total >= k)
            def _():
                out_copy(total - k).wait()


def _expert_call(first_tile, n_tiles_per_expert, n_rows_per_expert, total_tiles, xs,
                 w_gate, w_up, w_down):
    nrows = xs.shape[0]
    by_expert = lambda e, *_: (e, 0, 0)
    half_row = D_MODEL // 2
    return pl.pallas_call(
        _expert_kernel,
        grid_spec=pltpu.PrefetchScalarGridSpec(
            num_scalar_prefetch=4,
            grid=(N_EXPERTS,),
            in_specs=[pl.BlockSpec(memory_space=pl.ANY),
                      pl.BlockSpec((1, D_MODEL, D_EXPERT), by_expert),
                      pl.BlockSpec((1, D_MODEL, D_EXPERT), by_expert),
                      pl.BlockSpec((1, D_EXPERT, D_MODEL), by_expert)],
            out_specs=pl.BlockSpec(memory_space=pl.ANY),
            scratch_shapes=[pltpu.VMEM((EXPERT_RING, ROW_TILE, half_row), jnp.uint32),
                            pltpu.VMEM((EXPERT_RING, ROW_TILE, half_row), jnp.uint32),
                            pltpu.VMEM((D_MODEL, D_EXPERT), BF16),
                            pltpu.VMEM((D_MODEL, D_EXPERT), BF16),
                            pltpu.VMEM((D_EXPERT, D_MODEL), BF16),
                            pltpu.SemaphoreType.DMA((EXPERT_RING,)),
                            pltpu.SemaphoreType.DMA((EXPERT_RING,))]),
        out_shape=jax.ShapeDtypeStruct((nrows, half_row), jnp.uint32),
        compiler_params=pltpu.CompilerParams(
            dimension_semantics=("arbitrary",), vmem_limit_bytes=STREAM_VMEM_LIMIT),
        name="expert_mlp",
    )(first_tile, n_tiles_per_expert, n_rows_per_expert, total_tiles, xs, w_gate, w_up, w_down)


def _combine_kernel(h_ref, y2_ref, route_ref, nfin_ref, out_ref):
    route = route_ref[...]
    w1, w2 = route[:, 0:1], route[:, 1:2]
    lo1, hi1 = _unpack_bf16_pairs(y2_ref[0])
    lo2, hi2 = _unpack_bf16_pairs(y2_ref[1])
    y = h_ref[...] + jnp.concatenate([w1 * lo1 + w2 * lo2, w1 * hi1 + w2 * hi2], axis=1)
    out_ref[...] = _rmsnorm(y, nfin_ref[...])


def _combine_call(h, y2, route, nfin, first_token):
    n = h.shape[0]
    first_block = first_token // COMBINE_TILE
    return pl.pallas_call(
        _combine_kernel,
        grid=(n // COMBINE_TILE,),
        in_specs=[pl.BlockSpec((COMBINE_TILE, D_MODEL), lambda i: (i, 0)),
                  pl.BlockSpec((2, COMBINE_TILE, D_MODEL // 2), lambda i: (0, first_block + i, 0)),
                  pl.BlockSpec((COMBINE_TILE, LANES), lambda i: (i, 0)),
                  pl.BlockSpec((1, D_MODEL), lambda i: (0, 0))],
        out_specs=pl.BlockSpec((COMBINE_TILE, D_MODEL), lambda i: (i, 0)),
        out_shape=jax.ShapeDtypeStruct((n, D_MODEL), F32),
        compiler_params=pltpu.CompilerParams(
            dimension_semantics=("arbitrary",), vmem_limit_bytes=STREAM_VMEM_LIMIT),
        name="moe_combine",
    )(h, y2, route, nfin)


def _moe_rows(parts, w_gate, w_up, w_down):
    eid = jnp.concatenate([ids for _, ids in parts], axis=1).astype(jnp.int32)
    ntok = eid.shape[1]
    assert ntok % RANK_TILE == 0 and ntok % SC_WINDOW == 0
    rank, cnt = _rank_call(eid)
    counts = cnt[:, 0].astype(jnp.int32)
    padded = ((counts + ROW_TILE - 1) // ROW_TILE) * ROW_TILE
    ends = jnp.cumsum(padded)
    starts = ends - padded
    expert_iota = jnp.arange(N_EXPERTS, dtype=jnp.int32)[:, None, None]
    pos = rank + jnp.sum(jnp.where(eid[None] == expert_iota, starts[:, None, None], 0), axis=0)
    n_rows = (2 * ntok + N_EXPERTS * (ROW_TILE - 1)) // ROW_TILE * ROW_TILE
    xs = _dispatch_rows([rows for rows, _ in parts], pos, n_rows)
    ys = _expert_call(starts // ROW_TILE, padded // ROW_TILE, counts, ends[-1:] // ROW_TILE, xs,
                      w_gate, w_up, w_down)
    return _return_rows(ys, pos.reshape(-1)).reshape(2, ntok, D_MODEL // 2)


def kernel(x_prompt, x_sample, state_conv, state_gla, norm_mix, w_in, w_conv, w_alpha_up, b_alpha,
           gla_norm, w_conv_out, w_gla_out, w_o, norm_ffn, w_group_router, b_group_router,
           w_expert_router, b_expert_router, w_gate, w_up, w_down, norm_final):
    nbp, seq_p, _ = x_prompt.shape
    nbs, seq_s, _ = x_sample.shape
    assert norm_mix.shape[0] == 1, "single layer"
    assert seq_p % PROMPT_TILE == 0 and nbs % SAMPLE_SEQS == 0 and seq_s == 8

    weights = _mixer_weights(norm_mix, w_in, w_conv, w_alpha_up, b_alpha, gla_norm, w_conv_out,
                             w_gla_out, w_o, norm_ffn, w_group_router, b_group_router,
                             w_expert_router, b_expert_router)

    ntp, nts = nbp * seq_p, nbs * seq_s
    assert ntp % COMBINE_TILE == 0 and nts % COMBINE_TILE == 0
    conv0 = jnp.zeros((nbp, CONV_WIDTH - 1, D_CONV), F32)
    gla0 = jnp.zeros((nbp, 1, 8, LANES), F32)
    flat = lambda a: a.reshape(-1, a.shape[-1])
    ids_of = lambda route_t: route_t[:, 2:4, :].transpose(1, 0, 2).reshape(2, -1)
    h_p, xn_p, rt_p, rtt_p, conv_p, gla_p = _mixer_call(x_prompt, conv0, gla0, weights, prompt=True)
    h_s, xn_s, rt_s, rtt_s, conv_s, gla_s = _mixer_call(x_sample, state_conv[0], state_gla[0],
                                                        weights, prompt=False)
    y2 = _moe_rows([(flat(xn_p), ids_of(rtt_p)), (flat(xn_s), ids_of(rtt_s))],
                   w_gate[0], w_up[0], w_down[0])
    nfin = norm_final[None, :]
    y_p = _combine_call(flat(h_p), y2, flat(rt_p), nfin, 0)
    y_s = _combine_call(flat(h_s), y2, flat(rt_s), nfin, ntp)
    return (y_p.reshape(nbp, seq_p, D_MODEL), y_s.reshape(nbs, seq_s, D_MODEL),
            conv_p[None], gla_p[None], conv_s[None], gla_s[None])
```

```python
import functools
import itertools

import jax
import jax.numpy as jnp
from jax import lax
from jax.experimental import pallas as pl
from jax.experimental.pallas import tpu as pltpu
from jax.experimental.pallas import tpu_sc as plsc

F32 = jnp.float32
BF16 = jnp.bfloat16

D_MODEL = 1024
D_CONV = 512
CONV_WIDTH = 3
GLA_HEADS = 4
GLA_DK = 128
GLA_DV = 256
GLA_RANK = 16
GLA_GATE_NORM = 16.0
GLA_CHUNK = 32
N_GROUPS = 4
EXPERTS_PER_GROUP = 8
N_EXPERTS = 32
D_EXPERT = 256
EPS = 1e-6

LANES = 128
V7X_VMEM_BYTES = 64 * 1024 * 1024
OFF_CB, OFF_CC, OFF_CH = 0, 512, 1024
OFF_Q, OFF_K, OFF_V, OFF_G = 1536, 2048, 2560, 3584
OFF_GA, OFF_GB, OFF_AL = 4608, 5632, 6656

PROMPT_TILE = 512
GLA_BLOCK = 256
SAMPLE_SEQS = 32
STATE_CHUNK = 4
STATE_OUT_SLOTS = 4
ROW_TILE = 256
EXPERT_RING = 6
EXPERT_AHEAD = 4
SC_WINDOW = 128
COMBINE_TILE = 1024
RANK_TILE = 1024
CAST_ROWS = 512
RANK_BLOCK = 512
VMEM_HEADROOM_BYTES = 6 * 1024 * 1024
MIXER_VMEM_LIMIT = V7X_VMEM_BYTES - VMEM_HEADROOM_BYTES
STREAM_VMEM_LIMIT = MIXER_VMEM_LIMIT


def _rmsnorm(x, g):
    ms = jnp.mean(x * x, axis=-1, keepdims=True)
    return (x * lax.rsqrt(ms + EPS)) * g


def _split_bf16(x):
    hi = x.astype(BF16)
    lo = (x - hi.astype(F32)).astype(BF16)
    return hi, lo


def _pack_bf16_pairs(x):
    n = x.shape[1] // 2
    lo = lax.bitcast_convert_type(x[:, :n].astype(BF16).astype(F32), jnp.uint32)
    hi = lax.bitcast_convert_type(x[:, n:].astype(BF16).astype(F32), jnp.uint32)
    return lax.shift_right_logical(lo, jnp.uint32(16)) | hi


def _unpack_bf16_pairs(w):
    lo = lax.bitcast_convert_type(lax.shift_left(w, jnp.uint32(16)), F32)
    hi = lax.bitcast_convert_type(w & jnp.uint32(0xFFFF0000), F32)
    return lo, hi


def _dot(a, b):
    return jnp.dot(a, b, preferred_element_type=F32)


def _chunk_causal(n, chunk):
    r = lax.broadcasted_iota(jnp.int32, (n, n), 0)
    c = lax.broadcasted_iota(jnp.int32, (n, n), 1)
    shift = chunk.bit_length() - 1
    assert chunk == 1 << shift
    same = lax.shift_right_arithmetic(r, shift) == lax.shift_right_arithmetic(c, shift)
    return same & (c <= r)


def _router_logits(xn2, wr_ref, br_ref):
    xh, xl = _split_bf16(xn2)
    part = _dot(xh, wr_ref[...]) + _dot(xl, wr_ref[...])
    return part[:, :LANES] + part[:, LANES:] + br_ref[...]


def _route(lg):
    n = lg.shape[0]
    n_rows = 40
    lt = lg.T[:n_rows, :]
    row_i = lax.broadcasted_iota(jnp.int32, (n_rows, n), 0)
    row = row_i.astype(F32)
    neg = -jnp.inf
    none = float(LANES)
    is_g = row_i < N_GROUPS
    gm = jnp.max(jnp.where(is_g, lt, neg), axis=0, keepdims=True)
    gs = jnp.sum(jnp.where(is_g, jnp.exp(lt - gm), 0.0), axis=0, keepdims=True)
    g_top = 1.0 / gs
    g_idx = jnp.min(jnp.where(is_g & (lt == gm), row, none), axis=0, keepdims=True)
    e_group = lax.shift_right_arithmetic(row_i - N_GROUPS, 3).astype(F32)
    sel = (row_i >= N_GROUPS) & (row_i < N_GROUPS + N_EXPERTS) & (e_group == g_idx)
    el = jnp.where(sel, lt, neg)
    e1 = jnp.max(el, axis=0, keepdims=True)
    i1 = jnp.min(jnp.where(sel & (el == e1), row, none), axis=0, keepdims=True)
    sel2 = sel & (row != i1)
    el2 = jnp.where(sel2, lt, neg)
    e2 = jnp.max(el2, axis=0, keepdims=True)
    i2 = jnp.min(jnp.where(sel2 & (el2 == e2), row, none), axis=0, keepdims=True)
    d = jnp.exp(e2 - e1)
    w1 = g_top / (1.0 + d)
    w2 = (g_top * d) / (1.0 + d)
    id1 = i1 - float(N_GROUPS)
    id2 = i2 - float(N_GROUPS)
    out_i = lax.broadcasted_iota(jnp.int32, (8, n), 0)
    route_t = jnp.where(out_i == 0, w1, jnp.where(out_i == 1, w2, jnp.where(out_i == 2, id1,
                        jnp.where(out_i == 3, id2, 0.0))))
    route = jnp.concatenate([route_t, jnp.zeros((LANES - 8, n), F32)], axis=0).T
    return route, route_t


def _mixer_stages(sub, nsub, x_ref, conv_in_ref, gla_in_ref,
                  nmix_ref, w_ref, wgate_ref, walr_ref, wconv_ref, walpha_ref, balpha_ref, gnorm_ref,
                  wco_ref, wgo_ref, wo_ref, nffn_ref, wr_ref, br_ref,
                  h_ref, xn2_ref, route_ref, route_t_ref, conv_out_ref, gla_out_ref,
                  s_ref, ubuf_ref, *state_ring, prompt, logits_out):
    if prompt:
        rows, chunk, nseq = GLA_BLOCK, GLA_CHUNK, 1
        r0 = sub * rows
        x = x_ref[0, r0:r0 + rows, :]
    else:
        nseq, chunk = SAMPLE_SEQS, x_ref.shape[1]
        rows = nseq * chunk
        r0 = 0
        x = x_ref[...].reshape(rows, D_MODEL)
        sin_ref, sout_ref, sem_in, sem_out = state_ring
        n_state_chunks = nseq // STATE_CHUNK
        step, n_steps = pl.program_id(0), pl.num_programs(0)

        def chunk_seqs(c, of_step):
            first = of_step * nseq + c * STATE_CHUNK
            return pl.ds(pl.multiple_of(first, STATE_CHUNK), STATE_CHUNK)

        def in_copy(c, of_step):
            return pltpu.make_async_copy(gla_in_ref.at[chunk_seqs(c, of_step)], sin_ref.at[c],
                                         sem_in.at[c])

        def out_copy(c, of_step):
            slot = c % STATE_OUT_SLOTS
            return pltpu.make_async_copy(sout_ref.at[slot], gla_out_ref.at[chunk_seqs(c, of_step)],
                                         sem_out.at[slot])

        @pl.when(step == 0)
        def _():
            for c in range(n_state_chunks):
                in_copy(c, step).start()
    nchunks = rows // chunk
    last_sub = sub == nsub - 1

    xn = _rmsnorm(x, nmix_ref[...]).astype(BF16)
    yield

    def proj(off, n):
        if off == OFF_AL:
            return _dot(xn, walr_ref[...])
        if off >= OFF_GA:
            return _dot(xn, wgate_ref[:, off - OFF_GA:off - OFF_GA + n])
        return _dot(xn, w_ref[:, off:off + n])

    alr = proj(OFF_AL, LANES).astype(BF16)
    lap = _dot(alr, walpha_ref[...])
    la = jax.nn.log_sigmoid(lap + balpha_ref[...]) * (1.0 / GLA_GATE_NORM)
    la_hi, la_lo = _split_bf16(la)
    causal = _chunk_causal(rows, chunk)
    tri = jnp.where(causal, 1.0, 0.0).astype(BF16)
    yield
    cc = proj(OFF_CC, D_CONV)
    ch = proj(OFF_CH, D_CONV)
    cb = proj(OFF_CB, D_CONV)
    q = proj(OFF_Q, D_CONV)
    k = proj(OFF_K, D_CONV)
    yield
    b = _dot(tri, la_hi) + _dot(tri, la_lo)

    u = cc * ch
    wc = wconv_ref[...]
    if prompt:
        ubuf_ref[0, 8 + r0:8 + r0 + rows, :] = u
        z = (wc[0:1] * ubuf_ref[0, 6 + r0:6 + r0 + rows, :]
             + wc[1:2] * ubuf_ref[0, 7 + r0:7 + r0 + rows, :] + wc[2:3] * u)
        if last_sub:
            tail = u[rows - 2:rows, :]
            ubuf_ref[0, 6:8, :] = tail
            conv_out_ref[0] = tail
    else:
        u3 = u.reshape(nseq, chunk, D_CONV)
        ubuf_ref[:, 6:8, :] = conv_in_ref[...]
        ubuf_ref[:, 8:8 + chunk, :] = u3
        z3 = (wc[0:1] * ubuf_ref[:, 6:6 + chunk, :] + wc[1:2] * ubuf_ref[:, 7:7 + chunk, :]
              + wc[2:3] * u3)
        z = z3.reshape(rows, D_CONV)
        conv_out_ref[...] = ubuf_ref[:, 6 + chunk:8 + chunk, :]
    ya = _dot((cb * z).astype(BF16), wco_ref[...])
    v = proj(OFF_V, GLA_HEADS * GLA_DV)
    yield

    b3 = b.reshape(nchunks, chunk, D_CONV)
    bl3 = b3[:, chunk - 1:chunk, :]
    qin = (q * (GLA_DK ** -0.5)) * jnp.exp(b)
    kin = k * jnp.exp(-b)
    kst = (k.reshape(nchunks, chunk, D_CONV) * jnp.exp(bl3 - b3)).reshape(rows, D_CONV)
    pad = (-rows) % LANES
    rows_p = rows + pad

    def transposed(a):
        if pad:
            a = jnp.concatenate([a, jnp.zeros((pad, a.shape[1]), F32)], axis=0)
        return a.T

    b_t = transposed(b)
    kst_t = transposed(kst) if prompt else None
    col_chunk = lax.shift_right_arithmetic(
        lax.broadcasted_iota(jnp.int32, (GLA_DK, rows_p), 1), chunk.bit_length() - 1)
    g = proj(OFF_G, GLA_HEADS * GLA_DV)
    yield
    heads = range(GLA_HEADS)
    vhs = [v[:, hd * GLA_DV:(hd + 1) * GLA_DV].astype(BF16) for hd in heads]
    scs = [lax.dot_general(qin[:, hd * GLA_DK:(hd + 1) * GLA_DK].astype(BF16),
                           kin[:, hd * GLA_DK:(hd + 1) * GLA_DK].astype(BF16),
                           (((1,), (1,)), ((), ())), preferred_element_type=F32) for hd in heads]
    ga = proj(OFF_GA, D_MODEL)
    yield
    upds = []
    for hd in heads if prompt else ():
        kst_h = kst_t[hd * GLA_DK:(hd + 1) * GLA_DK, :]
        stacked = jnp.concatenate(
            [jnp.where(col_chunk == n, kst_h, 0.0).astype(BF16) for n in range(nchunks)], axis=0)
        vh_p = jnp.concatenate([vhs[hd], jnp.zeros((pad, GLA_DV), BF16)], axis=0) if pad else vhs[hd]
        upds.append(_dot(stacked, vh_p))
    o_intras = [_dot(jnp.where(causal, scs[hd], 0.0).astype(BF16), vhs[hd]) for hd in heads]
    gb = proj(OFF_GB, D_MODEL)
    merge_a = jax.nn.sigmoid(ga) * ya
    gate_b = jax.nn.sigmoid(gb)
    yield
    def q_block(n, hd):
        return qin[n * chunk:(n + 1) * chunk, hd * GLA_DK:(hd + 1) * GLA_DK].astype(BF16)

    def decay_col(n, hd):
        last = n * chunk + chunk - 1
        return jnp.exp(b_t[hd * GLA_DK:(hd + 1) * GLA_DK, last:last + 1])

    o_inter = [[None] * nchunks for _ in heads]
    if prompt:
        for hd in heads:
            s_run = s_ref[hd]
            for n in range(nchunks):
                o_inter[hd][n] = _dot(q_block(n, hd), s_run.astype(BF16))
                s_run = s_run * decay_col(n, hd) + upds[hd][n * GLA_DK:(n + 1) * GLA_DK, :]
            s_ref[hd] = s_run
            if last_sub:
                gla_out_ref[0, hd] = s_run
    else:
        for c in range(n_state_chunks):
            in_copy(c, step).wait()
            if c >= STATE_OUT_SLOTS:
                out_copy(c - STATE_OUT_SLOTS, step).wait()
            else:
                @pl.when(step > 0)
                def _():
                    out_copy(c + n_state_chunks - STATE_OUT_SLOTS, step - 1).wait()
            for j in range(STATE_CHUNK):
                n = c * STATE_CHUNK + j
                for hd in heads:
                    s_prev = sin_ref[c, j, hd]
                    o_inter[hd][n] = _dot(q_block(n, hd), s_prev.astype(BF16))
                    ksb = kst[n * chunk:(n + 1) * chunk, hd * GLA_DK:(hd + 1) * GLA_DK].astype(BF16)
                    vb = v[n * chunk:(n + 1) * chunk, hd * GLA_DV:(hd + 1) * GLA_DV].astype(BF16)
                    upd = lax.dot_general(ksb, vb, (((0,), (0,)), ((), ())),
                                          preferred_element_type=F32)
                    sout_ref[c % STATE_OUT_SLOTS, j, hd] = s_prev * decay_col(n, hd) + upd
            out_copy(c, step).start()

        @pl.when(step + 1 < n_steps)
        def _():
            for c in range(n_state_chunks):
                in_copy(c, step + 1).start()

        @pl.when(step + 1 == n_steps)
        def _():
            for c in range(n_state_chunks - STATE_OUT_SLOTS, n_state_chunks):
                out_copy(c, step).wait()
    gated = []
    for hd in heads:
        v0 = hd * GLA_DV
        oh = _rmsnorm(o_intras[hd] + jnp.concatenate(o_inter[hd], axis=0), gnorm_ref[...])
        gh = g[:, v0:v0 + GLA_DV]
        gated.append(oh * (gh * jax.nn.sigmoid(gh)))
    yield
    yb = _dot(jnp.concatenate(gated, axis=1).astype(BF16), wgo_ref[...])
    yield

    m = merge_a + gate_b * yb
    hh = x + _dot(m.astype(BF16), wo_ref[...])
    yield
    xn2 = _rmsnorm(hh, nffn_ref[...])
    yield
    logits_out.append(_router_logits(xn2, wr_ref, br_ref))
    if prompt:
        h_ref[0, r0:r0 + rows, :] = hh
        xn2_ref[0, r0:r0 + rows, :] = _pack_bf16_pairs(xn2)
    else:
        h_ref[...] = hh.reshape(nseq, chunk, D_MODEL)
        xn2_ref[...] = _pack_bf16_pairs(xn2).reshape(nseq, chunk, D_MODEL // 2)


def _mixer_kernel(*refs, prompt):
    n_scratch = 2 if prompt else 6
    outs = refs[len(refs) - n_scratch - 6:len(refs) - n_scratch]
    route_ref, route_t_ref = outs[2], outs[3]
    if prompt:
        s_ref, ubuf_ref = refs[-2:]

        @pl.when(pl.program_id(1) == 0)
        def _():
            s_ref[...] = jnp.zeros_like(s_ref)
            ubuf_ref[...] = jnp.zeros_like(ubuf_ref)
    nsub = PROMPT_TILE // GLA_BLOCK if prompt else 1
    logits = []
    stages = [_mixer_stages(sub, nsub, *refs, prompt=prompt, logits_out=logits)
              for sub in range(nsub)]
    for _ in itertools.zip_longest(*stages):
        pass
    route, route_t = _route(jnp.concatenate(logits, axis=0))
    if prompt:
        route_ref[0] = route
    else:
        route_ref[...] = route.reshape(route_ref.shape)
    route_t_ref[0] = route_t


def _in_proj_cast_kernel(wt_hbm, main_ref, gates_ref, alr_ref, buf, lr_buf, sem):
    i, n = pl.program_id(0), pl.num_programs(0)
    n_main = OFF_GA // CAST_ROWS

    def slab_copy(step):
        row = jnp.where(step < n_main, step * CAST_ROWS, step * CAST_ROWS + GLA_RANK)
        slot = step % 2
        return pltpu.make_async_copy(
            wt_hbm.at[pl.ds(pl.multiple_of(row, 8), CAST_ROWS)], buf.at[slot], sem.at[slot])

    @pl.when(i == 0)
    def _():
        slab_copy(i).start()
        low_rank_copy = pltpu.make_async_copy(
            wt_hbm.at[pl.ds(OFF_GA, GLA_RANK)], lr_buf.at[pl.ds(0, GLA_RANK)], sem.at[2])
        low_rank_copy.start()
        lr_buf[GLA_RANK:, :] = jnp.zeros((LANES - GLA_RANK, D_MODEL), F32)
        low_rank_copy.wait()
        alr_ref[...] = lr_buf[...].T.astype(BF16)

    @pl.when(i + 1 < n)
    def _():
        slab_copy(i + 1).start()

    slab_copy(i).wait()
    w = buf[i % 2].T.astype(BF16)

    @pl.when(i < n_main)
    def _():
        main_ref[...] = w

    @pl.when(i >= n_main)
    def _():
        gates_ref[...] = w


def _in_proj_cast_call(wt):
    d_in, dm = wt.shape
    n_gate_cols = d_in - OFF_GA - GLA_RANK
    assert dm == D_MODEL and OFF_GA % CAST_ROWS == 0 and n_gate_cols % CAST_ROWS == 0
    assert GLA_RANK % 8 == 0
    n_main = OFF_GA // CAST_ROWS
    return pl.pallas_call(
        _in_proj_cast_kernel,
        grid=(n_main + n_gate_cols // CAST_ROWS,),
        in_specs=[pl.BlockSpec(memory_space=pl.ANY)],
        out_specs=[
            pl.BlockSpec((D_MODEL, CAST_ROWS), lambda i: (0, jnp.minimum(i, n_main - 1))),
            pl.BlockSpec((D_MODEL, CAST_ROWS), lambda i: (0, jnp.maximum(i - n_main, 0))),
            pl.BlockSpec((D_MODEL, LANES), lambda i: (0, 0)),
        ],
        out_shape=[jax.ShapeDtypeStruct((D_MODEL, OFF_GA), BF16),
                   jax.ShapeDtypeStruct((D_MODEL, n_gate_cols), BF16),
                   jax.ShapeDtypeStruct((D_MODEL, LANES), BF16)],
        scratch_shapes=[pltpu.VMEM((2, CAST_ROWS, D_MODEL), F32), pltpu.VMEM((LANES, D_MODEL), F32),
                        pltpu.SemaphoreType.DMA((3,))],
        compiler_params=pltpu.CompilerParams(
            dimension_semantics=("arbitrary",), vmem_limit_bytes=STREAM_VMEM_LIMIT),
        name="in_proj_cast",
    )(wt)


def _mixer_weights(norm_mix, w_in, w_conv, w_alpha_up, b_alpha, gla_norm, w_conv_out, w_gla_out, w_o,
                   norm_ffn, w_group_router, b_group_router, w_expert_router, b_expert_router):
    w_main, w_gates, w_alr = _in_proj_cast_call(w_in[0].T)
    walpha = jnp.concatenate(
        [w_alpha_up[0], jnp.zeros((LANES - GLA_RANK, D_CONV), F32)], axis=0).astype(BF16)
    wr = jnp.concatenate(
        [w_group_router[0], w_expert_router[0],
         jnp.zeros((D_MODEL, LANES - N_GROUPS - N_EXPERTS), F32)], axis=1)
    wr_hi = wr.astype(BF16)
    wr_split = jnp.concatenate([wr_hi, (wr - wr_hi.astype(F32)).astype(BF16)], axis=1)
    br = jnp.concatenate([b_group_router[0], b_expert_router[0],
                          jnp.zeros((LANES - N_GROUPS - N_EXPERTS,), F32)])[None, :]
    return (norm_mix, w_main, w_gates, w_alr, w_conv[0], walpha, b_alpha, gla_norm,
            w_conv_out[0].astype(BF16), w_gla_out[0].astype(BF16), w_o[0].astype(BF16),
            norm_ffn, wr_split, br)


def _const_spec(shape):
    nd = len(shape)
    return pl.BlockSpec(shape, lambda *_: (0,) * nd, pipeline_mode=pl.Buffered(1))


def _mixer_call(x, conv_state, gla_state, weights, *, prompt):
    nb, seq, _ = x.shape
    if prompt:
        grid = (nb, seq // PROMPT_TILE)
        tok = lambda last: pl.BlockSpec((1, PROMPT_TILE, last), lambda b, i: (b, i, 0))
        conv_spec = pl.BlockSpec((1, CONV_WIDTH - 1, D_CONV), lambda b, i: (b, 0, 0))
        gla_spec = pl.BlockSpec((1, GLA_HEADS, GLA_DK, GLA_DV), lambda b, i: (b, 0, 0, 0))
        gla_in_spec = pl.BlockSpec((1, 1, 8, LANES), lambda b, i: (b, 0, 0, 0))
        scratch = [pltpu.VMEM((GLA_HEADS, GLA_DK, GLA_DV), F32),
                   pltpu.VMEM((1, PROMPT_TILE + 8, D_CONV), F32)]
        route_t_shape = (nb, 8, seq)
        route_t_spec = pl.BlockSpec((1, 8, PROMPT_TILE), lambda b, i: (b, 0, i))
    else:
        grid = (nb // SAMPLE_SEQS, 1)
        tok = lambda last: pl.BlockSpec((SAMPLE_SEQS, seq, last), lambda b, i: (b, 0, 0))
        conv_spec = pl.BlockSpec((SAMPLE_SEQS, CONV_WIDTH - 1, D_CONV), lambda b, i: (b, 0, 0))
        gla_spec = pl.BlockSpec(memory_space=pl.ANY)
        gla_in_spec = gla_spec
        state_chunk = (STATE_CHUNK, GLA_HEADS, GLA_DK, GLA_DV)
        scratch = [pltpu.VMEM((1, 8, LANES), F32),
                   pltpu.VMEM((SAMPLE_SEQS, 8 + seq, D_CONV), F32),
                   pltpu.VMEM((SAMPLE_SEQS // STATE_CHUNK,) + state_chunk, F32),
                   pltpu.VMEM((STATE_OUT_SLOTS,) + state_chunk, F32),
                   pltpu.SemaphoreType.DMA((SAMPLE_SEQS // STATE_CHUNK,)),
                   pltpu.SemaphoreType.DMA((STATE_OUT_SLOTS,))]
        route_t_shape = (nb // SAMPLE_SEQS, 8, SAMPLE_SEQS * seq)
        route_t_spec = pl.BlockSpec((1, 8, SAMPLE_SEQS * seq), lambda b, i: (b, 0, 0))
    out_shape = (jax.ShapeDtypeStruct((nb, seq, D_MODEL), F32),
                 jax.ShapeDtypeStruct((nb, seq, D_MODEL // 2), jnp.uint32),
                 jax.ShapeDtypeStruct((nb, seq, LANES), F32),
                 jax.ShapeDtypeStruct(route_t_shape, F32),
                 jax.ShapeDtypeStruct((nb, CONV_WIDTH - 1, D_CONV), F32),
                 jax.ShapeDtypeStruct((nb, GLA_HEADS, GLA_DK, GLA_DV), F32))
    return pl.pallas_call(
        functools.partial(_mixer_kernel, prompt=prompt),
        grid=grid,
        in_specs=[tok(D_MODEL), conv_spec, gla_in_spec] + [_const_spec(w.shape) for w in weights],
        out_specs=(tok(D_MODEL), tok(D_MODEL // 2), tok(LANES), route_t_spec, conv_spec, gla_spec),
        out_shape=out_shape,
        scratch_shapes=scratch,
        compiler_params=pltpu.CompilerParams(
            dimension_semantics=("arbitrary", "arbitrary"), vmem_limit_bytes=MIXER_VMEM_LIMIT),
        name="mixer_prompt" if prompt else "mixer_sample",
    )(x, conv_state, gla_state, *weights)


def _rank_kernel(eid_ref, rank_ref, cnt_ref, carry_ref):
    @pl.when(pl.program_id(0) == 0)
    def _():
        carry_ref[...] = jnp.zeros_like(carry_ref)

    n = RANK_BLOCK
    e_iota = lax.broadcasted_iota(jnp.int32, (N_EXPERTS, n), 0)
    r = lax.broadcasted_iota(jnp.int32, (n, n), 0)
    c = lax.broadcasted_iota(jnp.int32, (n, n), 1)
    before = jnp.where(r < c, 1.0, 0.0).astype(BF16)
    carry = carry_ref[...]
    for j in range(eid_ref.shape[1] // n):
        cols = slice(j * n, (j + 1) * n)
        oh0 = jnp.where(e_iota == eid_ref[0:1, cols], 1.0, 0.0)
        oh1 = jnp.where(e_iota == eid_ref[1:2, cols], 1.0, 0.0)
        cnt = oh0 + oh1
        base = carry + _dot(cnt.astype(BF16), before)
        rank0 = jnp.sum(oh0 * base, axis=0, keepdims=True)
        rank1 = jnp.sum(oh1 * base, axis=0, keepdims=True)
        rank_ref[:, cols] = jnp.concatenate([rank0, rank1], axis=0).astype(jnp.int32)
        carry = carry + jnp.sum(cnt, axis=1, keepdims=True)
    carry_ref[...] = carry
    cnt_ref[...] = jnp.broadcast_to(carry, cnt_ref.shape)


def _rank_call(eid):
    ntok = eid.shape[1]
    return pl.pallas_call(
        _rank_kernel,
        grid=(ntok // RANK_TILE,),
        in_specs=[pl.BlockSpec((2, RANK_TILE), lambda i: (0, i))],
        out_specs=(pl.BlockSpec((2, RANK_TILE), lambda i: (0, i)),
                   pl.BlockSpec((N_EXPERTS, LANES), lambda i: (0, 0))),
        out_shape=(jax.ShapeDtypeStruct((2, ntok), jnp.int32),
                   jax.ShapeDtypeStruct((N_EXPERTS, LANES), F32)),
        scratch_shapes=[pltpu.VMEM((N_EXPERTS, 1), F32)],
        compiler_params=pltpu.CompilerParams(dimension_semantics=("arbitrary",)),
        name="expert_rank",
    )(eid)


def _sc_mesh():
    return plsc.VectorSubcoreMesh(core_axis_name="c", subcore_axis_name="s")


def _sc_worker_windows(n_windows, body):
    info = plsc.get_sparse_core_info()
    n_workers = info.num_cores * info.num_subcores
    wid = lax.axis_index("s") * info.num_cores + lax.axis_index("c")

    @pl.loop(0, pl.cdiv(n_windows, n_workers))
    def _(j):
        win = wid + n_workers * j

        @pl.when(win < n_windows)
        def _():
            body(win)


def _dispatch_rows(sources, pos, n_rows):
    width, dtype = sources[0].shape[1], sources[0].dtype
    n_win = [src.shape[0] // SC_WINDOW for src in sources]
    first_win = [sum(n_win[:i]) for i in range(len(sources))]
    total_win = sum(n_win)
    pos3 = pos.reshape(2, total_win, SC_WINDOW).transpose(1, 0, 2)

    @functools.partial(pl.kernel, mesh=_sc_mesh(), name="moe_dispatch",
                       out_type=jax.ShapeDtypeStruct((n_rows, width), dtype),
                       scratch_types=[pltpu.VMEM((2, SC_WINDOW), jnp.int32),
                                      pltpu.VMEM((SC_WINDOW, width), dtype)])
    def run(*refs):
        src_hbm, (pos_hbm, out_hbm, idx_v, buf) = refs[:len(sources)], refs[len(sources):]

        def window(win):
            pltpu.sync_copy(pos_hbm.at[win], idx_v)
            for src, w0, nw in zip(src_hbm, first_win, n_win):
                @pl.when((win >= w0) & (win < w0 + nw))
                def _():
                    r0 = pl.multiple_of((win - w0) * SC_WINDOW, SC_WINDOW)
                    pltpu.sync_copy(src.at[pl.ds(r0, SC_WINDOW)], buf)

            pltpu.sync_copy(buf, out_hbm.at[idx_v.at[0]])
            pltpu.sync_copy(buf, out_hbm.at[idx_v.at[1]])

        _sc_worker_windows(total_win, window)

    return run(*sources, pos3)


def _return_rows(ys, idx):
    n_out, width = idx.shape[0], ys.shape[1]
    n_windows = n_out // SC_WINDOW

    @functools.partial(pl.kernel, mesh=_sc_mesh(), name="moe_return",
                       out_type=jax.ShapeDtypeStruct((n_out, width), ys.dtype),
                       scratch_types=[pltpu.VMEM((SC_WINDOW,), jnp.int32),
                                      pltpu.VMEM((SC_WINDOW, width), ys.dtype)])
    def run(ys_hbm, idx_hbm, out_hbm, idx_v, buf):
        def window(win):
            pltpu.sync_copy(idx_hbm.at[win], idx_v)
            pltpu.sync_copy(ys_hbm.at[idx_v], buf)
            r0 = pl.multiple_of(win * SC_WINDOW, SC_WINDOW)
            pltpu.sync_copy(buf, out_hbm.at[pl.ds(r0, SC_WINDOW)])

        _sc_worker_windows(n_windows, window)

    return run(ys, idx.reshape(n_windows, SC_WINDOW))


def _expert_kernel(first_ref, ntile_ref, nrow_ref, total_ref, xs_hbm, wg_ref, wu_ref, wd_ref, out_hbm,
                   xbuf, obuf, wg_s, wu_s, wd_s, sem_in, sem_out):
    e = pl.program_id(0)
    nt = ntile_ref[e]
    g0 = first_ref[e]
    total = total_ref[0]
    ring = EXPERT_RING

    def rows_of(g):
        return pl.ds(pl.multiple_of(g * ROW_TILE, ROW_TILE), ROW_TILE)

    def in_copy(g):
        slot = lax.rem(g, ring)
        return pltpu.make_async_copy(xs_hbm.at[rows_of(g)], xbuf.at[slot], sem_in.at[slot])

    def out_copy(g):
        slot = lax.rem(g, ring)
        return pltpu.make_async_copy(obuf.at[slot], out_hbm.at[rows_of(g)], sem_out.at[slot])

    @pl.when(e == 0)
    def _():
        for g in range(EXPERT_AHEAD):
            @pl.when(g < total)
            def _():
                in_copy(g).start()

    @pl.when(nt > 0)
    def _():
        wg_s[...] = wg_ref[0].astype(BF16)
        wu_s[...] = wu_ref[0].astype(BF16)
        wd_s[...] = wd_ref[0].astype(BF16)

    half = D_MODEL // 2
    sub = ROW_TILE // 2

    def tiles(gs):
        halves = []
        for g in gs:
            slot = lax.rem(g, ring)
            in_copy(g).wait()

            @pl.when(g + EXPERT_AHEAD < total)
            def _():
                in_copy(g + EXPERT_AHEAD).start()

            @pl.when(g >= ring)
            def _():
                out_copy(g - ring).wait()

            n_valid = nrow_ref[e] - (g - g0) * ROW_TILE
            row = lax.broadcasted_iota(jnp.int32, (ROW_TILE, 1), 0)
            lo, hi = _unpack_bf16_pairs(jnp.where(row < n_valid, xbuf[slot], jnp.uint32(0)))
            for r in (0, sub):
                xl, xh = lo[r:r + sub].astype(BF16), hi[r:r + sub].astype(BF16)
                gt = _dot(xl, wg_s[:half, :]) + _dot(xh, wg_s[half:, :])
                up = _dot(xl, wu_s[:half, :]) + _dot(xh, wu_s[half:, :])
                halves.append((slot, r, gt, up))
        for slot, r, gt, up in halves:
            act = (gt * jax.nn.sigmoid(gt)) * up
            obuf[slot, r:r + sub, :] = _pack_bf16_pairs(_dot(act.astype(BF16), wd_s[...]))
        for g in gs:
            out_copy(g).start()

    def pair(i, carry):
        tiles([g0 + 2 * i, g0 + 2 * i + 1])
        return carry

    lax.fori_loop(0, nt // 2, pair, 0)

    @pl.when(lax.rem(nt, 2) == 1)
    def _():
        tiles([g0 + nt - 1])

    @pl.when(e == pl.num_programs(0) - 1)
    def _():
        for k in range(1, ring + 1):
            @pl.when(total >= k)
            def _():
                out_copy(total - k).wait()


def _expert_call(first_tile, n_tiles_per_expert, n_rows_per_expert, total_tiles, xs,
                 w_gate, w_up, w_down):
    nrows = xs.shape[0]
    by_expert = lambda e, *_: (e, 0, 0)
    half_row = D_MODEL // 2
    return pl.pallas_call(
        _expert_kernel,
        grid_spec=pltpu.PrefetchScalarGridSpec(
            num_scalar_prefetch=4,
            grid=(N_EXPERTS,),
            in_specs=[pl.BlockSpec(memory_space=pl.ANY),
                      pl.BlockSpec((1, D_MODEL, D_EXPERT), by_expert),
                      pl.BlockSpec((1, D_MODEL, D_EXPERT), by_expert),
                      pl.BlockSpec((1, D_EXPERT, D_MODEL), by_expert)],
            out_specs=pl.BlockSpec(memory_space=pl.ANY),
            scratch_shapes=[pltpu.VMEM((EXPERT_RING, ROW_TILE, half_row), jnp.uint32),
                            pltpu.VMEM((EXPERT_RING, ROW_TILE, half_row), jnp.uint32),
                            pltpu.VMEM((D_MODEL, D_EXPERT), BF16),
                            pltpu.VMEM((D_MODEL, D_EXPERT), BF16),
                            pltpu.VMEM((D_EXPERT, D_MODEL), BF16),
                            pltpu.SemaphoreType.DMA((EXPERT_RING,)),
                            pltpu.SemaphoreType.DMA((EXPERT_RING,))]),
        out_shape=jax.ShapeDtypeStruct((nrows, half_row), jnp.uint32),
        compiler_params=pltpu.CompilerParams(
            dimension_semantics=("arbitrary",), vmem_limit_bytes=STREAM_VMEM_LIMIT),
        name="expert_mlp",
    )(first_tile, n_tiles_per_expert, n_rows_per_expert, total_tiles, xs, w_gate, w_up, w_down)


def _combine_kernel(h_ref, y2_ref, route_ref, nfin_ref, out_ref):
    route = route_ref[...]
    w1, w2 = route[:, 0:1], route[:, 1:2]
    lo1, hi1 = _unpack_bf16_pairs(y2_ref[0])
    lo2, hi2 = _unpack_bf16_pairs(y2_ref[1])
    y = h_ref[...] + jnp.concatenate([w1 * lo1 + w2 * lo2, w1 * hi1 + w2 * hi2], axis=1)
    out_ref[...] = _rmsnorm(y, nfin_ref[...])


def _combine_call(h, y2, route, nfin, first_token):
    n = h.shape[0]
    first_block = first_token // COMBINE_TILE
    return pl.pallas_call(
        _combine_kernel,
        grid=(n // COMBINE_TILE,),
        in_specs=[pl.BlockSpec((COMBINE_TILE, D_MODEL), lambda i: (i, 0)),
                  pl.BlockSpec((2, COMBINE_TILE, D_MODEL // 2), lambda i: (0, first_block + i, 0)),
                  pl.BlockSpec((COMBINE_TILE, LANES), lambda i: (i, 0)),
                  pl.BlockSpec((1, D_MODEL), lambda i: (0, 0))],
        out_specs=pl.BlockSpec((COMBINE_TILE, D_MODEL), lambda i: (i, 0)),
        out_shape=jax.ShapeDtypeStruct((n, D_MODEL), F32),
        compiler_params=pltpu.CompilerParams(
            dimension_semantics=("arbitrary",), vmem_limit_bytes=STREAM_VMEM_LIMIT),
        name="moe_combine",
    )(h, y2, route, nfin)


def _moe_rows(parts, w_gate, w_up, w_down):
    eid = jnp.concatenate([ids for _, ids in parts], axis=1).astype(jnp.int32)
    ntok = eid.shape[1]
    assert ntok % RANK_TILE == 0 and ntok % SC_WINDOW == 0
    rank, cnt = _rank_call(eid)
    counts = cnt[:, 0].astype(jnp.int32)
    padded = ((counts + ROW_TILE - 1) // ROW_TILE) * ROW_TILE
    ends = jnp.cumsum(padded)
    starts = ends - padded
    expert_iota = jnp.arange(N_EXPERTS, dtype=jnp.int32)[:, None, None]
    pos = rank + jnp.sum(jnp.where(eid[None] == expert_iota, starts[:, None, None], 0), axis=0)
    n_rows = (2 * ntok + N_EXPERTS * (ROW_TILE - 1)) // ROW_TILE * ROW_TILE
    xs = _dispatch_rows([rows for rows, _ in parts], pos, n_rows)
    ys = _expert_call(starts // ROW_TILE, padded // ROW_TILE, counts, ends[-1:] // ROW_TILE, xs,
                      w_gate, w_up, w_down)
    return _return_rows(ys, pos.reshape(-1)).reshape(2, ntok, D_MODEL // 2)


def kernel(x_prompt, x_sample, state_conv, state_gla, norm_mix, w_in, w_conv, w_alpha_up, b_alpha,
           gla_norm, w_conv_out, w_gla_out, w_o, norm_ffn, w_group_router, b_group_router,
           w_expert_router, b_expert_router, w_gate, w_up, w_down, norm_final):
    nbp, seq_p, _ = x_prompt.shape
    nbs, seq_s, _ = x_sample.shape
    assert norm_mix.shape[0] == 1, "single layer"
    assert seq_p % PROMPT_TILE == 0 and nbs % SAMPLE_SEQS == 0 and seq_s == 8

    weights = _mixer_weights(norm_mix, w_in, w_conv, w_alpha_up, b_alpha, gla_norm, w_conv_out,
                             w_gla_out, w_o, norm_ffn, w_group_router, b_group_router,
                             w_expert_router, b_expert_router)

    ntp, nts = nbp * seq_p, nbs * seq_s
    assert ntp % COMBINE_TILE == 0 and nts % COMBINE_TILE == 0
    conv0 = jnp.zeros((nbp, CONV_WIDTH - 1, D_CONV), F32)
    gla0 = jnp.zeros((nbp, 1, 8, LANES), F32)
    flat = lambda a: a.reshape(-1, a.shape[-1])
    ids_of = lambda route_t: route_t[:, 2:4, :].transpose(1, 0, 2).reshape(2, -1)
    h_p, xn_p, rt_p, rtt_p, conv_p, gla_p = _mixer_call(x_prompt, conv0, gla0, weights, prompt=True)
    h_s, xn_s, rt_s, rtt_s, conv_s, gla_s = _mixer_call(x_sample, state_conv[0], state_gla[0],
                                                        weights, prompt=False)
    y2 = _moe_rows([(flat(xn_p), ids_of(rtt_p)), (flat(xn_s), ids_of(rtt_s))],
                   w_gate[0], w_up[0], w_down[0])
    nfin = norm_final[None, :]
    y_p = _combine_call(flat(h_p), y2, flat(rt_p), nfin, 0)
    y_s = _combine_call(flat(h_s), y2, flat(rt_s), nfin, ntp)
    return (y_p.reshape(nbp, seq_p, D_MODEL), y_s.reshape(nbs, seq_s, D_MODEL),
            conv_p[None], gla_p[None], conv_s[None], gla_s[None])
```

```python
import functools
import itertools

import jax
import jax.numpy as jnp
from jax import lax
from jax.experimental import pallas as pl
from jax.experimental.pallas import tpu as pltpu
from jax.experimental.pallas import tpu_sc as plsc

F32 = jnp.float32
BF16 = jnp.bfloat16

D_MODEL = 1024
D_CONV = 512
CONV_WIDTH = 3
GLA_HEADS = 4
GLA_DK = 128
GLA_DV = 256
GLA_RANK = 16
GLA_GATE_NORM = 16.0
GLA_CHUNK = 32
N_GROUPS = 4
EXPERTS_PER_GROUP = 8
N_EXPERTS = 32
D_EXPERT = 256
EPS = 1e-6

LANES = 128
V7X_VMEM_BYTES = 64 * 1024 * 1024
OFF_CB, OFF_CC, OFF_CH = 0, 512, 1024
OFF_Q, OFF_K, OFF_V, OFF_G = 1536, 2048, 2560, 3584
OFF_GA, OFF_GB, OFF_AL = 4608, 5632, 6656

PROMPT_TILE = 512
GLA_BLOCK = 256
SAMPLE_SEQS = 32
STATE_CHUNK = 4
STATE_OUT_SLOTS = 4
ROW_TILE = 256
EXPERT_RING = 6
EXPERT_AHEAD = 4
SC_WINDOW = 128
COMBINE_TILE = 1024
RANK_TILE = 1024
CAST_ROWS = 512
RANK_BLOCK = 512
VMEM_HEADROOM_BYTES = 6 * 1024 * 1024
MIXER_VMEM_LIMIT = V7X_VMEM_BYTES - VMEM_HEADROOM_BYTES
STREAM_VMEM_LIMIT = MIXER_VMEM_LIMIT


def _rmsnorm(x, g):
    ms = jnp.mean(x * x, axis=-1, keepdims=True)
    return (x * lax.rsqrt(ms + EPS)) * g


def _split_bf16(x):
    hi = x.astype(BF16)
    lo = (x - hi.astype(F32)).astype(BF16)
    return hi, lo


def _pack_bf16_pairs(x):
    n = x.shape[1] // 2
    lo = lax.bitcast_convert_type(x[:, :n].astype(BF16).astype(F32), jnp.uint32)
    hi = lax.bitcast_convert_type(x[:, n:].astype(BF16).astype(F32), jnp.uint32)
    return lax.shift_right_logical(lo, jnp.uint32(16)) | hi


def _unpack_bf16_pairs(w):
    lo = lax.bitcast_convert_type(lax.shift_left(w, jnp.uint32(16)), F32)
    hi = lax.bitcast_convert_type(w & jnp.uint32(0xFFFF0000), F32)
    return lo, hi


def _dot(a, b):
    return jnp.dot(a, b, preferred_element_type=F32)


def _chunk_causal(n, chunk):
    r = lax.broadcasted_iota(jnp.int32, (n, n), 0)
    c = lax.broadcasted_iota(jnp.int32, (n, n), 1)
    shift = chunk.bit_length() - 1
    assert chunk == 1 << shift
    same = lax.shift_right_arithmetic(r, shift) == lax.shift_right_arithmetic(c, shift)
    return same & (c <= r)


def _router_logits(xn2, wr_ref, br_ref):
    xh, xl = _split_bf16(xn2)
    part = _dot(xh, wr_ref[...]) + _dot(xl, wr_ref[...])
    return part[:, :LANES] + part[:, LANES:] + br_ref[...]


def _route(lg):
    n = lg.shape[0]
    n_rows = 40
    lt = lg.T[:n_rows, :]
    row_i = lax.broadcasted_iota(jnp.int32, (n_rows, n), 0)
    row = row_i.astype(F32)
    neg = -jnp.inf
    none = float(LANES)
    is_g = row_i < N_GROUPS
    gm = jnp.max(jnp.where(is_g, lt, neg), axis=0, keepdims=True)
    gs = jnp.sum(jnp.where(is_g, jnp.exp(lt - gm), 0.0), axis=0, keepdims=True)
    g_top = 1.0 / gs
    g_idx = jnp.min(jnp.where(is_g & (lt == gm), row, none), axis=0, keepdims=True)
    e_group = lax.shift_right_arithmetic(row_i - N_GROUPS, 3).astype(F32)
    sel = (row_i >= N_GROUPS) & (row_i < N_GROUPS + N_EXPERTS) & (e_group == g_idx)
    el = jnp.where(sel, lt, neg)
    e1 = jnp.max(el, axis=0, keepdims=True)
    i1 = jnp.min(jnp.where(sel & (el == e1), row, none), axis=0, keepdims=True)
    sel2 = sel & (row != i1)
    el2 = jnp.where(sel2, lt, neg)
    e2 = jnp.max(el2, axis=0, keepdims=True)
    i2 = jnp.min(jnp.where(sel2 & (el2 == e2), row, none), axis=0, keepdims=True)
    d = jnp.exp(e2 - e1)
    w1 = g_top / (1.0 + d)
    w2 = (g_top * d) / (1.0 + d)
    id1 = i1 - float(N_GROUPS)
    id2 = i2 - float(N_GROUPS)
    out_i = lax.broadcasted_iota(jnp.int32, (8, n), 0)
    route_t = jnp.where(out_i == 0, w1, jnp.where(out_i == 1, w2, jnp.where(out_i == 2, id1,
                        jnp.where(out_i == 3, id2, 0.0))))
    route = jnp.concatenate([route_t, jnp.zeros((LANES - 8, n), F32)], axis=0).T
    return route, route_t


def _mixer_stages(sub, nsub, x_ref, conv_in_ref, gla_in_ref,
                  nmix_ref, w_ref, wgate_ref, walr_ref, wconv_ref, walpha_ref, balpha_ref, gnorm_ref,
                  wco_ref, wgo_ref, wo_ref, nffn_ref, wr_ref, br_ref,
                  h_ref, xn2_ref, route_ref, route_t_ref, conv_out_ref, gla_out_ref,
                  s_ref, ubuf_ref, *state_ring, prompt, logits_out):
    if prompt:
        rows, chunk, nseq = GLA_BLOCK, GLA_CHUNK, 1
        r0 = sub * rows
        x = x_ref[0, r0:r0 + rows, :]
    else:
        nseq, chunk = SAMPLE_SEQS, x_ref.shape[1]
        rows = nseq * chunk
        r0 = 0
        x = x_ref[...].reshape(rows, D_MODEL)
        sin_ref, sout_ref, sem_in, sem_out = state_ring
        n_state_chunks = nseq // STATE_CHUNK
        step, n_steps = pl.program_id(0), pl.num_programs(0)

        def chunk_seqs(c, of_step):
            first = of_step * nseq + c * STATE_CHUNK
            return pl.ds(pl.multiple_of(first, STATE_CHUNK), STATE_CHUNK)

        def in_copy(c, of_step):
            return pltpu.make_async_copy(gla_in_ref.at[chunk_seqs(c, of_step)], sin_ref.at[c],
                                         sem_in.at[c])

        def out_copy(c, of_step):
            slot = c % STATE_OUT_SLOTS
            return pltpu.make_async_copy(sout_ref.at[slot], gla_out_ref.at[chunk_seqs(c, of_step)],
                                         sem_out.at[slot])

        @pl.when(step == 0)
        def _():
            for c in range(n_state_chunks):
                in_copy(c, step).start()
    nchunks = rows // chunk
    last_sub = sub == nsub - 1

    xn = _rmsnorm(x, nmix_ref[...]).astype(BF16)
    yield

    def proj(off, n):
        if off == OFF_AL:
            return _dot(xn, walr_ref[...])
        if off >= OFF_GA:
            return _dot(xn, wgate_ref[:, off - OFF_GA:off - OFF_GA + n])
        return _dot(xn, w_ref[:, off:off + n])

    alr = proj(OFF_AL, LANES).astype(BF16)
    lap = _dot(alr, walpha_ref[...])
    la = jax.nn.log_sigmoid(lap + balpha_ref[...]) * (1.0 / GLA_GATE_NORM)
    la_hi, la_lo = _split_bf16(la)
    causal = _chunk_causal(rows, chunk)
    tri = jnp.where(causal, 1.0, 0.0).astype(BF16)
    yield
    cc = proj(OFF_CC, D_CONV)
    ch = proj(OFF_CH, D_CONV)
    cb = proj(OFF_CB, D_CONV)
    q = proj(OFF_Q, D_CONV)
    k = proj(OFF_K, D_CONV)
    yield
    b = _dot(tri, la_hi) + _dot(tri, la_lo)

    u = cc * ch
    wc = wconv_ref[...]
    if prompt:
        ubuf_ref[0, 8 + r0:8 + r0 + rows, :] = u
        z = (wc[0:1] * ubuf_ref[0, 6 + r0:6 + r0 + rows, :]
             + wc[1:2] * ubuf_ref[0, 7 + r0:7 + r0 + rows, :] + wc[2:3] * u)
        if last_sub:
            tail = u[rows - 2:rows, :]
            ubuf_ref[0, 6:8, :] = tail
            conv_out_ref[0] = tail
    else:
        u3 = u.reshape(nseq, chunk, D_CONV)
        ubuf_ref[:, 6:8, :] = conv_in_ref[...]
        ubuf_ref[:, 8:8 + chunk, :] = u3
        z3 = (wc[0:1] * ubuf_ref[:, 6:6 + chunk, :] + wc[1:2] * ubuf_ref[:, 7:7 + chunk, :]
              + wc[2:3] * u3)
        z = z3.reshape(rows, D_CONV)
        conv_out_ref[...] = ubuf_ref[:, 6 + chunk:8 + chunk, :]
    ya = _dot((cb * z).astype(BF16), wco_ref[...])
    v = proj(OFF_V, GLA_HEADS * GLA_DV)
    yield

    b3 = b.reshape(nchunks, chunk, D_CONV)
    bl3 = b3[:, chunk - 1:chunk, :]
    qin = (q * (GLA_DK ** -0.5)) * jnp.exp(b)
    kin = k * jnp.exp(-b)
    kst = (k.reshape(nchunks, chunk, D_CONV) * jnp.exp(bl3 - b3)).reshape(rows, D_CONV)
    pad = (-rows) % LANES
    rows_p = rows + pad

    def transposed(a):
        if pad:
            a = jnp.concatenate([a, jnp.zeros((pad, a.shape[1]), F32)], axis=0)
        return a.T

    b_t = transposed(b)
    kst_t = transposed(kst) if prompt else None
    col_chunk = lax.shift_right_arithmetic(
        lax.broadcasted_iota(jnp.int32, (GLA_DK, rows_p), 1), chunk.bit_length() - 1)
    g = proj(OFF_G, GLA_HEADS * GLA_DV)
    yield
    heads = range(GLA_HEADS)
    vhs = [v[:, hd * GLA_DV:(hd + 1) * GLA_DV].astype(BF16) for hd in heads]
    scs = [lax.dot_general(qin[:, hd * GLA_DK:(hd + 1) * GLA_DK].astype(BF16),
                           kin[:, hd * GLA_DK:(hd + 1) * GLA_DK].astype(BF16),
                           (((1,), (1,)), ((), ())), preferred_element_type=F32) for hd in heads]
    ga = proj(OFF_GA, D_MODEL)
    yield
    upds = []
    for hd in heads if prompt else ():
        kst_h = kst_t[hd * GLA_DK:(hd + 1) * GLA_DK, :]
        stacked = jnp.concatenate(
            [jnp.where(col_chunk == n, kst_h, 0.0).astype(BF16) for n in range(nchunks)], axis=0)
        vh_p = jnp.concatenate([vhs[hd], jnp.zeros((pad, GLA_DV), BF16)], axis=0) if pad else vhs[hd]
        upds.append(_dot(stacked, vh_p))
    o_intras = [_dot(jnp.where(causal, scs[hd], 0.0).astype(BF16), vhs[hd]) for hd in heads]
    gb = proj(OFF_GB, D_MODEL)
    merge_a = jax.nn.sigmoid(ga) * ya
    gate_b = jax.nn.sigmoid(gb)
    yield
    def q_block(n, hd):
        return qin[n * chunk:(n + 1) * chunk, hd * GLA_DK:(hd + 1) * GLA_DK].astype(BF16)

    def decay_col(n, hd):
        last = n * chunk + chunk - 1
        return jnp.exp(b_t[hd * GLA_DK:(hd + 1) * GLA_DK, last:last + 1])

    o_inter = [[None] * nchunks for _ in heads]
    if prompt:
        for hd in heads:
            s_run = s_ref[hd]
            for n in range(nchunks):
                o_inter[hd][n] = _dot(q_block(n, hd), s_run.astype(BF16))
                s_run = s_run * decay_col(n, hd) + upds[hd][n * GLA_DK:(n + 1) * GLA_DK, :]
            s_ref[hd] = s_run
            if last_sub:
                gla_out_ref[0, hd] = s_run
    else:
        for c in range(n_state_chunks):
            in_copy(c, step).wait()
            if c >= STATE_OUT_SLOTS:
                out_copy(c - STATE_OUT_SLOTS, step).wait()
            else:
                @pl.when(step > 0)
                def _():
                    out_copy(c + n_state_chunks - STATE_OUT_SLOTS, step - 1).wait()
            for j in range(STATE_CHUNK):
                n = c * STATE_CHUNK + j
                for hd in heads:
                    s_prev = sin_ref[c, j, hd]
                    o_inter[hd][n] = _dot(q_block(n, hd), s_prev.astype(BF16))
                    ksb = kst[n * chunk:(n + 1) * chunk, hd * GLA_DK:(hd + 1) * GLA_DK].astype(BF16)
                    vb = v[n * chunk:(n + 1) * chunk, hd * GLA_DV:(hd + 1) * GLA_DV].astype(BF16)
                    upd = lax.dot_general(ksb, vb, (((0,), (0,)), ((), ())),
                                          preferred_element_type=F32)
                    sout_ref[c % STATE_OUT_SLOTS, j, hd] = s_prev * decay_col(n, hd) + upd
            out_copy(c, step).start()

        @pl.when(step + 1 < n_steps)
        def _():
            for c in range(n_state_chunks):
                in_copy(c, step + 1).start()

        @pl.when(step + 1 == n_steps)
        def _():
            for c in range(n_state_chunks - STATE_OUT_SLOTS, n_state_chunks):
                out_copy(c, step).wait()
    gated = []
    for hd in heads:
        v0 = hd * GLA_DV
        oh = _rmsnorm(o_intras[hd] + jnp.concatenate(o_inter[hd], axis=0), gnorm_ref[...])
        gh = g[:, v0:v0 + GLA_DV]
        gated.append(oh * (gh * jax.nn.sigmoid(gh)))
    yield
    yb = _dot(jnp.concatenate(gated, axis=1).astype(BF16), wgo_ref[...])
    yield

    m = merge_a + gate_b * yb
    hh = x + _dot(m.astype(BF16), wo_ref[...])
    yield
    xn2 = _rmsnorm(hh, nffn_ref[...])
    yield
    logits_out.append(_router_logits(xn2, wr_ref, br_ref))
    if prompt:
        h_ref[0, r0:r0 + rows, :] = hh
        xn2_ref[0, r0:r0 + rows, :] = _pack_bf16_pairs(xn2)
    else:
        h_ref[...] = hh.reshape(nseq, chunk, D_MODEL)
        xn2_ref[...] = _pack_bf16_pairs(xn2).reshape(nseq, chunk, D_MODEL // 2)


def _mixer_kernel(*refs, prompt):
    n_scratch = 2 if prompt else 6
    outs = refs[len(refs) - n_scratch - 6:len(refs) - n_scratch]
    route_ref, route_t_ref = outs[2], outs[3]
    if prompt:
        s_ref, ubuf_ref = refs[-2:]

        @pl.when(pl.program_id(1) == 0)
        def _():
            s_ref[...] = jnp.zeros_like(s_ref)
            ubuf_ref[...] = jnp.zeros_like(ubuf_ref)
    nsub = PROMPT_TILE // GLA_BLOCK if prompt else 1
    logits = []
    stages = [_mixer_stages(sub, nsub, *refs, prompt=prompt, logits_out=logits)
              for sub in range(nsub)]
    for _ in itertools.zip_longest(*stages):
        pass
    route, route_t = _route(jnp.concatenate(logits, axis=0))
    if prompt:
        route_ref[0] = route
    else:
        route_ref[...] = route.reshape(route_ref.shape)
    route_t_ref[0] = route_t


def _in_proj_cast_kernel(wt_hbm, main_ref, gates_ref, alr_ref, buf, lr_buf, sem):
    i, n = pl.program_id(0), pl.num_programs(0)
    n_main = OFF_GA // CAST_ROWS

    def slab_copy(step):
        row = jnp.where(step < n_main, step * CAST_ROWS, step * CAST_ROWS + GLA_RANK)
        slot = step % 2
        return pltpu.make_async_copy(
            wt_hbm.at[pl.ds(pl.multiple_of(row, 8), CAST_ROWS)], buf.at[slot], sem.at[slot])

    @pl.when(i == 0)
    def _():
        slab_copy(i).start()
        low_rank_copy = pltpu.make_async_copy(
            wt_hbm.at[pl.ds(OFF_GA, GLA_RANK)], lr_buf.at[pl.ds(0, GLA_RANK)], sem.at[2])
        low_rank_copy.start()
        lr_buf[GLA_RANK:, :] = jnp.zeros((LANES - GLA_RANK, D_MODEL), F32)
        low_rank_copy.wait()
        alr_ref[...] = lr_buf[...].T.astype(BF16)

    @pl.when(i + 1 < n)
    def _():
        slab_copy(i + 1).start()

    slab_copy(i).wait()
    w = buf[i % 2].astype(BF16).T

    @pl.when(i < n_main)
    def _():
        main_ref[...] = w

    @pl.when(i >= n_main)
    def _():
        gates_ref[...] = w


def _in_proj_cast_call(wt):
    d_in, dm = wt.shape
    n_gate_cols = d_in - OFF_GA - GLA_RANK
    assert dm == D_MODEL and OFF_GA % CAST_ROWS == 0 and n_gate_cols % CAST_ROWS == 0
    assert GLA_RANK % 8 == 0
    n_main = OFF_GA // CAST_ROWS
    return pl.pallas_call(
        _in_proj_cast_kernel,
        grid=(n_main + n_gate_cols // CAST_ROWS,),
        in_specs=[pl.BlockSpec(memory_space=pl.ANY)],
        out_specs=[
            pl.BlockSpec((D_MODEL, CAST_ROWS), lambda i: (0, jnp.minimum(i, n_main - 1))),
            pl.BlockSpec((D_MODEL, CAST_ROWS), lambda i: (0, jnp.maximum(i - n_main, 0))),
            pl.BlockSpec((D_MODEL, LANES), lambda i: (0, 0)),
        ],
        out_shape=[jax.ShapeDtypeStruct((D_MODEL, OFF_GA), BF16),
                   jax.ShapeDtypeStruct((D_MODEL, n_gate_cols), BF16),
                   jax.ShapeDtypeStruct((D_MODEL, LANES), BF16)],
        scratch_shapes=[pltpu.VMEM((2, CAST_ROWS, D_MODEL), F32), pltpu.VMEM((LANES, D_MODEL), F32),
                        pltpu.SemaphoreType.DMA((3,))],
        compiler_params=pltpu.CompilerParams(
            dimension_semantics=("arbitrary",), vmem_limit_bytes=STREAM_VMEM_LIMIT),
        name="in_proj_cast",
    )(wt)


def _mixer_weights(norm_mix, w_in, w_conv, w_alpha_up, b_alpha, gla_norm, w_conv_out, w_gla_out, w_o,
                   norm_ffn, w_group_router, b_group_router, w_expert_router, b_expert_router):
    w_main, w_gates, w_alr = _in_proj_cast_call(w_in[0].T)
    walpha = jnp.concatenate(
        [w_alpha_up[0], jnp.zeros((LANES - GLA_RANK, D_CONV), F32)], axis=0).astype(BF16)
    wr = jnp.concatenate(
        [w_group_router[0], w_expert_router[0],
         jnp.zeros((D_MODEL, LANES - N_GROUPS - N_EXPERTS), F32)], axis=1)
    wr_hi = wr.astype(BF16)
    wr_split = jnp.concatenate([wr_hi, (wr - wr_hi.astype(F32)).astype(BF16)], axis=1)
    br = jnp.concatenate([b_group_router[0], b_expert_router[0],
                          jnp.zeros((LANES - N_GROUPS - N_EXPERTS,), F32)])[None, :]
    return (norm_mix, w_main, w_gates, w_alr, w_conv[0], walpha, b_alpha, gla_norm,
            w_conv_out[0].astype(BF16), w_gla_out[0].astype(BF16), w_o[0].astype(BF16),
            norm_ffn, wr_split, br)


def _const_spec(shape):
    nd = len(shape)
    return pl.BlockSpec(shape, lambda *_: (0,) * nd, pipeline_mode=pl.Buffered(1))


def _mixer_call(x, conv_state, gla_state, weights, *, prompt):
    nb, seq, _ = x.shape
    if prompt:
        grid = (nb, seq // PROMPT_TILE)
        tok = lambda last: pl.BlockSpec((1, PROMPT_TILE, last), lambda b, i: (b, i, 0))
        conv_spec = pl.BlockSpec((1, CONV_WIDTH - 1, D_CONV), lambda b, i: (b, 0, 0))
        gla_spec = pl.BlockSpec((1, GLA_HEADS, GLA_DK, GLA_DV), lambda b, i: (b, 0, 0, 0))
        gla_in_spec = pl.BlockSpec((1, 1, 8, LANES), lambda b, i: (b, 0, 0, 0))
        scratch = [pltpu.VMEM((GLA_HEADS, GLA_DK, GLA_DV), F32),
                   pltpu.VMEM((1, PROMPT_TILE + 8, D_CONV), F32)]
        route_t_shape = (nb, 8, seq)
        route_t_spec = pl.BlockSpec((1, 8, PROMPT_TILE), lambda b, i: (b, 0, i))
    else:
        grid = (nb // SAMPLE_SEQS, 1)
        tok = lambda last: pl.BlockSpec((SAMPLE_SEQS, seq, last), lambda b, i: (b, 0, 0))
        conv_spec = pl.BlockSpec((SAMPLE_SEQS, CONV_WIDTH - 1, D_CONV), lambda b, i: (b, 0, 0))
        gla_spec = pl.BlockSpec(memory_space=pl.ANY)
        gla_in_spec = gla_spec
        state_chunk = (STATE_CHUNK, GLA_HEADS, GLA_DK, GLA_DV)
        scratch = [pltpu.VMEM((1, 8, LANES), F32),
                   pltpu.VMEM((SAMPLE_SEQS, 8 + seq, D_CONV), F32),
                   pltpu.VMEM((SAMPLE_SEQS // STATE_CHUNK,) + state_chunk, F32),
                   pltpu.VMEM((STATE_OUT_SLOTS,) + state_chunk, F32),
                   pltpu.SemaphoreType.DMA((SAMPLE_SEQS // STATE_CHUNK,)),
                   pltpu.SemaphoreType.DMA((STATE_OUT_SLOTS,))]
        route_t_shape = (nb // SAMPLE_SEQS, 8, SAMPLE_SEQS * seq)
        route_t_spec = pl.BlockSpec((1, 8, SAMPLE_SEQS * seq), lambda b, i: (b, 0, 0))
    out_shape = (jax.ShapeDtypeStruct((nb, seq, D_MODEL), F32),
                 jax.ShapeDtypeStruct((nb, seq, D_MODEL // 2), jnp.uint32),
                 jax.ShapeDtypeStruct((nb, seq, LANES), F32),
                 jax.ShapeDtypeStruct(route_t_shape, F32),
                 jax.ShapeDtypeStruct((nb, CONV_WIDTH - 1, D_CONV), F32),
                 jax.ShapeDtypeStruct((nb, GLA_HEADS, GLA_DK, GLA_DV), F32))
    return pl.pallas_call(
        functools.partial(_mixer_kernel, prompt=prompt),
        grid=grid,
        in_specs=[tok(D_MODEL), conv_spec, gla_in_spec] + [_const_spec(w.shape) for w in weights],
        out_specs=(tok(D_MODEL), tok(D_MODEL // 2), tok(LANES), route_t_spec, conv_spec, gla_spec),
        out_shape=out_shape,
        scratch_shapes=scratch,
        compiler_params=pltpu.CompilerParams(
            dimension_semantics=("arbitrary", "arbitrary"), vmem_limit_bytes=MIXER_VMEM_LIMIT),
        name="mixer_prompt" if prompt else "mixer_sample",
    )(x, conv_state, gla_state, *weights)


def _rank_kernel(eid_ref, rank_ref, cnt_ref, carry_ref):
    @pl.when(pl.program_id(0) == 0)
    def _():
        carry_ref[...] = jnp.zeros_like(carry_ref)

    n = RANK_BLOCK
    e_iota = lax.broadcasted_iota(jnp.int32, (N_EXPERTS, n), 0)
    r = lax.broadcasted_iota(jnp.int32, (n, n), 0)
    c = lax.broadcasted_iota(jnp.int32, (n, n), 1)
    before = jnp.where(r < c, 1.0, 0.0).astype(BF16)
    carry = carry_ref[...]
    for j in range(eid_ref.shape[1] // n):
        cols = slice(j * n, (j + 1) * n)
        oh0 = jnp.where(e_iota == eid_ref[0:1, cols], 1.0, 0.0)
        oh1 = jnp.where(e_iota == eid_ref[1:2, cols], 1.0, 0.0)
        cnt = oh0 + oh1
        base = carry + _dot(cnt.astype(BF16), before)
        rank0 = jnp.sum(oh0 * base, axis=0, keepdims=True)
        rank1 = jnp.sum(oh1 * base, axis=0, keepdims=True)
        rank_ref[:, cols] = jnp.concatenate([rank0, rank1], axis=0).astype(jnp.int32)
        carry = carry + jnp.sum(cnt, axis=1, keepdims=True)
    carry_ref[...] = carry
    cnt_ref[...] = jnp.broadcast_to(carry, cnt_ref.shape)


def _rank_call(eid):
    ntok = eid.shape[1]
    return pl.pallas_call(
        _rank_kernel,
        grid=(ntok // RANK_TILE,),
        in_specs=[pl.BlockSpec((2, RANK_TILE), lambda i: (0, i))],
        out_specs=(pl.BlockSpec((2, RANK_TILE), lambda i: (0, i)),
                   pl.BlockSpec((N_EXPERTS, LANES), lambda i: (0, 0))),
        out_shape=(jax.ShapeDtypeStruct((2, ntok), jnp.int32),
                   jax.ShapeDtypeStruct((N_EXPERTS, LANES), F32)),
        scratch_shapes=[pltpu.VMEM((N_EXPERTS, 1), F32)],
        compiler_params=pltpu.CompilerParams(dimension_semantics=("arbitrary",)),
        name="expert_rank",
    )(eid)


def _sc_mesh():
    return plsc.VectorSubcoreMesh(core_axis_name="c", subcore_axis_name="s")


def _sc_worker_windows(n_windows, body):
    info = plsc.get_sparse_core_info()
    n_workers = info.num_cores * info.num_subcores
    wid = lax.axis_index("s") * info.num_cores + lax.axis_index("c")

    @pl.loop(0, pl.cdiv(n_windows, n_workers))
    def _(j):
        win = wid + n_workers * j

        @pl.when(win < n_windows)
        def _():
            body(win)


def _dispatch_rows(sources, pos, n_rows):
    width, dtype = sources[0].shape[1], sources[0].dtype
    n_win = [src.shape[0] // SC_WINDOW for src in sources]
    first_win = [sum(n_win[:i]) for i in range(len(sources))]
    total_win = sum(n_win)
    pos3 = pos.reshape(2, total_win, SC_WINDOW).transpose(1, 0, 2)

    @functools.partial(pl.kernel, mesh=_sc_mesh(), name="moe_dispatch",
                       out_type=jax.ShapeDtypeStruct((n_rows, width), dtype),
                       scratch_types=[pltpu.VMEM((2, SC_WINDOW), jnp.int32),
                                      pltpu.VMEM((SC_WINDOW, width), dtype)])
    def run(*refs):
        src_hbm, (pos_hbm, out_hbm, idx_v, buf) = refs[:len(sources)], refs[len(sources):]

        def window(win):
            pltpu.sync_copy(pos_hbm.at[win], idx_v)
            for src, w0, nw in zip(src_hbm, first_win, n_win):
                @pl.when((win >= w0) & (win < w0 + nw))
                def _():
                    r0 = pl.multiple_of((win - w0) * SC_WINDOW, SC_WINDOW)
                    pltpu.sync_copy(src.at[pl.ds(r0, SC_WINDOW)], buf)

            pltpu.sync_copy(buf, out_hbm.at[idx_v.at[0]])
            pltpu.sync_copy(buf, out_hbm.at[idx_v.at[1]])

        _sc_worker_windows(total_win, window)

    return run(*sources, pos3)


def _return_rows(ys, idx):
    n_out, width = idx.shape[0], ys.shape[1]
    n_windows = n_out // SC_WINDOW

    @functools.partial(pl.kernel, mesh=_sc_mesh(), name="moe_return",
                       out_type=jax.ShapeDtypeStruct((n_out, width), ys.dtype),
                       scratch_types=[pltpu.VMEM((SC_WINDOW,), jnp.int32),
                                      pltpu.VMEM((SC_WINDOW, width), ys.dtype)])
    def run(ys_hbm, idx_hbm, out_hbm, idx_v, buf):
        def window(win):
            pltpu.sync_copy(idx_hbm.at[win], idx_v)
            pltpu.sync_copy(ys_hbm.at[idx_v], buf)
            r0 = pl.multiple_of(win * SC_WINDOW, SC_WINDOW)
            pltpu.sync_copy(buf, out_hbm.at[pl.ds(r0, SC_WINDOW)])

        _sc_worker_windows(n_windows, window)

    return run(ys, idx.reshape(n_windows, SC_WINDOW))


def _expert_kernel(first_ref, ntile_ref, nrow_ref, total_ref, xs_hbm, wg_ref, wu_ref, wd_ref, out_hbm,
                   xbuf, obuf, wg_s, wu_s, wd_s, sem_in, sem_out):
    e = pl.program_id(0)
    nt = ntile_ref[e]
    g0 = first_ref[e]
    total = total_ref[0]
    ring = EXPERT_RING

    def rows_of(g):
        return pl.ds(pl.multiple_of(g * ROW_TILE, ROW_TILE), ROW_TILE)

    def in_copy(g):
        slot = lax.rem(g, ring)
        return pltpu.make_async_copy(xs_hbm.at[rows_of(g)], xbuf.at[slot], sem_in.at[slot])

    def out_copy(g):
        slot = lax.rem(g, ring)
        return pltpu.make_async_copy(obuf.at[slot], out_hbm.at[rows_of(g)], sem_out.at[slot])

    @pl.when(e == 0)
    def _():
        for g in range(EXPERT_AHEAD):
            @pl.when(g < total)
            def _():
                in_copy(g).start()

    @pl.when(nt > 0)
    def _():
        wg_s[...] = wg_ref[0].astype(BF16)
        wu_s[...] = wu_ref[0].astype(BF16)
        wd_s[...] = wd_ref[0].astype(BF16)

    half = D_MODEL // 2
    sub = ROW_TILE // 2

    def tiles(gs):
        halves = []
        for g in gs:
            slot = lax.rem(g, ring)
            in_copy(g).wait()

            @pl.when(g + EXPERT_AHEAD < total)
            def _():
                in_copy(g + EXPERT_AHEAD).start()

            @pl.when(g >= ring)
            def _():
                out_copy(g - ring).wait()

            n_valid = nrow_ref[e] - (g - g0) * ROW_TILE
            row = lax.broadcasted_iota(jnp.int32, (ROW_TILE, 1), 0)
            lo, hi = _unpack_bf16_pairs(jnp.where(row < n_valid, xbuf[slot], jnp.uint32(0)))
            for r in (0, sub):
                xl, xh = lo[r:r + sub].astype(BF16), hi[r:r + sub].astype(BF16)
                gt = _dot(xl, wg_s[:half, :]) + _dot(xh, wg_s[half:, :])
                up = _dot(xl, wu_s[:half, :]) + _dot(xh, wu_s[half:, :])
                halves.append((slot, r, gt, up))
        for slot, r, gt, up in halves:
            act = (gt * jax.nn.sigmoid(gt)) * up
            obuf[slot, r:r + sub, :] = _pack_bf16_pairs(_dot(act.astype(BF16), wd_s[...]))
        for g in gs:
            out_copy(g).start()

    def pair(i, carry):
        tiles([g0 + 2 * i, g0 + 2 * i + 1])
        return carry

    lax.fori_loop(0, nt // 2, pair, 0)

    @pl.when(lax.rem(nt, 2) == 1)
    def _():
        tiles([g0 + nt - 1])

    @pl.when(e == pl.num_programs(0) - 1)
    def _():
        for k in range(1, ring + 1):
            @pl.when(total >= k)
            def _():
                out_copy(total - k).wait()


def _expert_call(first_tile, n_tiles_per_expert, n_rows_per_expert, total_tiles, xs,
                 w_gate, w_up, w_down):
    nrows = xs.shape[0]
    by_expert = lambda e, *_: (e, 0, 0)
    half_row = D_MODEL // 2
    return pl.pallas_call(
        _expert_kernel,
        grid_spec=pltpu.PrefetchScalarGridSpec(
            num_scalar_prefetch=4,
            grid=(N_EXPERTS,),
            in_specs=[pl.BlockSpec(memory_space=pl.ANY),
                      pl.BlockSpec((1, D_MODEL, D_EXPERT), by_expert),
                      pl.BlockSpec((1, D_MODEL, D_EXPERT), by_expert),
                      pl.BlockSpec((1, D_EXPERT, D_MODEL), by_expert)],
            out_specs=pl.BlockSpec(memory_space=pl.ANY),
            scratch_shapes=[pltpu.VMEM((EXPERT_RING, ROW_TILE, half_row), jnp.uint32),
                            pltpu.VMEM((EXPERT_RING, ROW_TILE, half_row), jnp.uint32),
                            pltpu.VMEM((D_MODEL, D_EXPERT), BF16),
                            pltpu.VMEM((D_MODEL, D_EXPERT), BF16),
                            pltpu.VMEM((D_EXPERT, D_MODEL), BF16),
                            pltpu.SemaphoreType.DMA((EXPERT_RING,)),
                            pltpu.SemaphoreType.DMA((EXPERT_RING,))]),
        out_shape=jax.ShapeDtypeStruct((nrows, half_row), jnp.uint32),
        compiler_params=pltpu.CompilerParams(
            dimension_semantics=("arbitrary",), vmem_limit_bytes=STREAM_VMEM_LIMIT),
        name="expert_mlp",
    )(first_tile, n_tiles_per_expert, n_rows_per_expert, total_tiles, xs, w_gate, w_up, w_down)


def _combine_kernel(h_ref, y2_ref, route_ref, nfin_ref, out_ref):
    route = route_ref[...]
    w1, w2 = route[:, 0:1], route[:, 1:2]
    lo1, hi1 = _unpack_bf16_pairs(y2_ref[0])
    lo2, hi2 = _unpack_bf16_pairs(y2_ref[1])
    y = h_ref[...] + jnp.concatenate([w1 * lo1 + w2 * lo2, w1 * hi1 + w2 * hi2], axis=1)
    out_ref[...] = _rmsnorm(y, nfin_ref[...])


def _combine_call(h, y2, route, nfin, first_token):
    n = h.shape[0]
    first_block = first_token // COMBINE_TILE
    return pl.pallas_call(
        _combine_kernel,
        grid=(n // COMBINE_TILE,),
        in_specs=[pl.BlockSpec((COMBINE_TILE, D_MODEL), lambda i: (i, 0)),
                  pl.BlockSpec((2, COMBINE_TILE, D_MODEL // 2), lambda i: (0, first_block + i, 0)),
                  pl.BlockSpec((COMBINE_TILE, LANES), lambda i: (i, 0)),
                  pl.BlockSpec((1, D_MODEL), lambda i: (0, 0))],
        out_specs=pl.BlockSpec((COMBINE_TILE, D_MODEL), lambda i: (i, 0)),
        out_shape=jax.ShapeDtypeStruct((n, D_MODEL), F32),
        compiler_params=pltpu.CompilerParams(
            dimension_semantics=("arbitrary",), vmem_limit_bytes=STREAM_VMEM_LIMIT),
        name="moe_combine",
    )(h, y2, route, nfin)


def _moe_rows(parts, w_gate, w_up, w_down):
    eid = jnp.concatenate([ids for _, ids in parts], axis=1).astype(jnp.int32)
    ntok = eid.shape[1]
    assert ntok % RANK_TILE == 0 and ntok % SC_WINDOW == 0
    rank, cnt = _rank_call(eid)
    counts = cnt[:, 0].astype(jnp.int32)
    padded = ((counts + ROW_TILE - 1) // ROW_TILE) * ROW_TILE
    ends = jnp.cumsum(padded)
    starts = ends - padded
    expert_iota = jnp.arange(N_EXPERTS, dtype=jnp.int32)[:, None, None]
    pos = rank + jnp.sum(jnp.where(eid[None] == expert_iota, starts[:, None, None], 0), axis=0)
    n_rows = (2 * ntok + N_EXPERTS * (ROW_TILE - 1)) // ROW_TILE * ROW_TILE
    xs = _dispatch_rows([rows for rows, _ in parts], pos, n_rows)
    ys = _expert_call(starts // ROW_TILE, padded // ROW_TILE, counts, ends[-1:] // ROW_TILE, xs,
                      w_gate, w_up, w_down)
    return _return_rows(ys, pos.reshape(-1)).reshape(2, ntok, D_MODEL // 2)


def kernel(x_prompt, x_sample, state_conv, state_gla, norm_mix, w_in, w_conv, w_alpha_up, b_alpha,
           gla_norm, w_conv_out, w_gla_out, w_o, norm_ffn, w_group_router, b_group_router,
           w_expert_router, b_expert_router, w_gate, w_up, w_down, norm_final):
    nbp, seq_p, _ = x_prompt.shape
    nbs, seq_s, _ = x_sample.shape
    assert norm_mix.shape[0] == 1, "single layer"
    assert seq_p % PROMPT_TILE == 0 and nbs % SAMPLE_SEQS == 0 and seq_s == 8

    weights = _mixer_weights(norm_mix, w_in, w_conv, w_alpha_up, b_alpha, gla_norm, w_conv_out,
                             w_gla_out, w_o, norm_ffn, w_group_router, b_group_router,
                             w_expert_router, b_expert_router)

    ntp, nts = nbp * seq_p, nbs * seq_s
    assert ntp % COMBINE_TILE == 0 and nts % COMBINE_TILE == 0
    conv0 = jnp.zeros((nbp, CONV_WIDTH - 1, D_CONV), F32)
    gla0 = jnp.zeros((nbp, 1, 8, LANES), F32)
    flat = lambda a: a.reshape(-1, a.shape[-1])
    ids_of = lambda route_t: route_t[:, 2:4, :].transpose(1, 0, 2).reshape(2, -1)
    h_p, xn_p, rt_p, rtt_p, conv_p, gla_p = _mixer_call(x_prompt, conv0, gla0, weights, prompt=True)
    h_s, xn_s, rt_s, rtt_s, conv_s, gla_s = _mixer_call(x_sample, state_conv[0], state_gla[0],
                                                        weights, prompt=False)
    y2 = _moe_rows([(flat(xn_p), ids_of(rtt_p)), (flat(xn_s), ids_of(rtt_s))],
                   w_gate[0], w_up[0], w_down[0])
    nfin = norm_final[None, :]
    y_p = _combine_call(flat(h_p), y2, flat(rt_p), nfin, 0)
    y_s = _combine_call(flat(h_s), y2, flat(rt_s), nfin, ntp)
    return (y_p.reshape(nbp, seq_p, D_MODEL), y_s.reshape(nbs, seq_s, D_MODEL),
            conv_p[None], gla_p[None], conv_s[None], gla_s[None])
```

```python
import functools
import itertools

import jax
import jax.numpy as jnp
from jax import lax
from jax.experimental import pallas as pl
from jax.experimental.pallas import tpu as pltpu
from jax.experimental.pallas import tpu_sc as plsc

F32 = jnp.float32
BF16 = jnp.bfloat16

D_MODEL = 1024
D_CONV = 512
CONV_WIDTH = 3
GLA_HEADS = 4
GLA_DK = 128
GLA_DV = 256
GLA_RANK = 16
GLA_GATE_NORM = 16.0
GLA_CHUNK = 32
N_GROUPS = 4
EXPERTS_PER_GROUP = 8
N_EXPERTS = 32
D_EXPERT = 256
EPS = 1e-6

LANES = 128
V7X_VMEM_BYTES = 64 * 1024 * 1024
OFF_CB, OFF_CC, OFF_CH = 0, 512, 1024
OFF_Q, OFF_K, OFF_V, OFF_G = 1536, 2048, 2560, 3584
OFF_GA, OFF_GB, OFF_AL = 4608, 5632, 6656

PROMPT_TILE = 512
GLA_BLOCK = 256
SAMPLE_SEQS = 32
STATE_CHUNK = 4
STATE_OUT_SLOTS = 4
ROW_TILE = 256
EXPERT_RING = 6
EXPERT_AHEAD = 4
ROW_DMA_PRIORITY = 1
SC_WINDOW = 128
COMBINE_TILE = 1024
RANK_TILE = 1024
CAST_ROWS = 512
RANK_BLOCK = 512
VMEM_HEADROOM_BYTES = 6 * 1024 * 1024
MIXER_VMEM_LIMIT = V7X_VMEM_BYTES - VMEM_HEADROOM_BYTES
STREAM_VMEM_LIMIT = MIXER_VMEM_LIMIT


def _rmsnorm(x, g):
    ms = jnp.mean(x * x, axis=-1, keepdims=True)
    return (x * lax.rsqrt(ms + EPS)) * g


def _split_bf16(x):
    hi = x.astype(BF16)
    lo = (x - hi.astype(F32)).astype(BF16)
    return hi, lo


def _pack_bf16_pairs(x):
    n = x.shape[1] // 2
    lo = lax.bitcast_convert_type(x[:, :n].astype(BF16).astype(F32), jnp.uint32)
    hi = lax.bitcast_convert_type(x[:, n:].astype(BF16).astype(F32), jnp.uint32)
    return lax.shift_right_logical(lo, jnp.uint32(16)) | hi


def _unpack_bf16_pairs(w):
    lo = lax.bitcast_convert_type(lax.shift_left(w, jnp.uint32(16)), F32)
    hi = lax.bitcast_convert_type(w & jnp.uint32(0xFFFF0000), F32)
    return lo, hi


def _dot(a, b):
    return jnp.dot(a, b, preferred_element_type=F32)


def _chunk_causal(n, chunk):
    r = lax.broadcasted_iota(jnp.int32, (n, n), 0)
    c = lax.broadcasted_iota(jnp.int32, (n, n), 1)
    shift = chunk.bit_length() - 1
    assert chunk == 1 << shift
    same = lax.shift_right_arithmetic(r, shift) == lax.shift_right_arithmetic(c, shift)
    return same & (c <= r)


def _router_logits(xn2, wr_ref, br_ref):
    xh, xl = _split_bf16(xn2)
    part = _dot(xh, wr_ref[...]) + _dot(xl, wr_ref[...])
    return part[:, :LANES] + part[:, LANES:] + br_ref[...]


def _route(lg):
    n = lg.shape[0]
    n_rows = 40
    lt = lg.T[:n_rows, :]
    row_i = lax.broadcasted_iota(jnp.int32, (n_rows, n), 0)
    row = row_i.astype(F32)
    neg = -jnp.inf
    none = float(LANES)
    is_g = row_i < N_GROUPS
    gm = jnp.max(jnp.where(is_g, lt, neg), axis=0, keepdims=True)
    gs = jnp.sum(jnp.where(is_g, jnp.exp(lt - gm), 0.0), axis=0, keepdims=True)
    g_top = 1.0 / gs
    g_idx = jnp.min(jnp.where(is_g & (lt == gm), row, none), axis=0, keepdims=True)
    e_group = lax.shift_right_arithmetic(row_i - N_GROUPS, 3).astype(F32)
    sel = (row_i >= N_GROUPS) & (row_i < N_GROUPS + N_EXPERTS) & (e_group == g_idx)
    el = jnp.where(sel, lt, neg)
    e1 = jnp.max(el, axis=0, keepdims=True)
    i1 = jnp.min(jnp.where(sel & (el == e1), row, none), axis=0, keepdims=True)
    sel2 = sel & (row != i1)
    el2 = jnp.where(sel2, lt, neg)
    e2 = jnp.max(el2, axis=0, keepdims=True)
    i2 = jnp.min(jnp.where(sel2 & (el2 == e2), row, none), axis=0, keepdims=True)
    d = jnp.exp(e2 - e1)
    w1 = g_top / (1.0 + d)
    w2 = (g_top * d) / (1.0 + d)
    id1 = i1 - float(N_GROUPS)
    id2 = i2 - float(N_GROUPS)
    out_i = lax.broadcasted_iota(jnp.int32, (8, n), 0)
    route_t = jnp.where(out_i == 0, w1, jnp.where(out_i == 1, w2, jnp.where(out_i == 2, id1,
                        jnp.where(out_i == 3, id2, 0.0))))
    route = jnp.concatenate([route_t, jnp.zeros((LANES - 8, n), F32)], axis=0).T
    return route, route_t


def _mixer_stages(sub, nsub, x_ref, conv_in_ref, gla_in_ref,
                  nmix_ref, w_ref, wgate_ref, walr_ref, wconv_ref, walpha_ref, balpha_ref, gnorm_ref,
                  wco_ref, wgo_ref, wo_ref, nffn_ref, wr_ref, br_ref,
                  h_ref, xn2_ref, route_ref, route_t_ref, conv_out_ref, gla_out_ref,
                  s_ref, ubuf_ref, *state_ring, prompt, logits_out):
    if prompt:
        rows, chunk, nseq = GLA_BLOCK, GLA_CHUNK, 1
        r0 = sub * rows
        x = x_ref[0, r0:r0 + rows, :]
    else:
        nseq, chunk = SAMPLE_SEQS, x_ref.shape[1]
        rows = nseq * chunk
        r0 = 0
        x = x_ref[...].reshape(rows, D_MODEL)
        sin_ref, sout_ref, sem_in, sem_out = state_ring
        n_state_chunks = nseq // STATE_CHUNK
        step, n_steps = pl.program_id(0), pl.num_programs(0)

        def chunk_seqs(c, of_step):
            first = of_step * nseq + c * STATE_CHUNK
            return pl.ds(pl.multiple_of(first, STATE_CHUNK), STATE_CHUNK)

        def in_copy(c, of_step):
            return pltpu.make_async_copy(gla_in_ref.at[chunk_seqs(c, of_step)], sin_ref.at[c],
                                         sem_in.at[c])

        def out_copy(c, of_step):
            slot = c % STATE_OUT_SLOTS
            return pltpu.make_async_copy(sout_ref.at[slot], gla_out_ref.at[chunk_seqs(c, of_step)],
                                         sem_out.at[slot])

        @pl.when(step == 0)
        def _():
            for c in range(n_state_chunks):
                in_copy(c, step).start()
    nchunks = rows // chunk
    last_sub = sub == nsub - 1

    xn = _rmsnorm(x, nmix_ref[...]).astype(BF16)
    yield

    def proj(off, n):
        if off == OFF_AL:
            return _dot(xn, walr_ref[...])
        if off >= OFF_GA:
            return _dot(xn, wgate_ref[:, off - OFF_GA:off - OFF_GA + n])
        return _dot(xn, w_ref[:, off:off + n])

    alr = proj(OFF_AL, LANES).astype(BF16)
    lap = _dot(alr, walpha_ref[...])
    la = jax.nn.log_sigmoid(lap + balpha_ref[...]) * (1.0 / GLA_GATE_NORM)
    la_hi, la_lo = _split_bf16(la)
    causal = _chunk_causal(rows, chunk)
    tri = jnp.where(causal, 1.0, 0.0).astype(BF16)
    yield
    cc = proj(OFF_CC, D_CONV)
    ch = proj(OFF_CH, D_CONV)
    cb = proj(OFF_CB, D_CONV)
    q = proj(OFF_Q, D_CONV)
    k = proj(OFF_K, D_CONV)
    yield
    b = _dot(tri, la_hi) + _dot(tri, la_lo)

    u = cc * ch
    wc = wconv_ref[...]
    if prompt:
        ubuf_ref[0, 8 + r0:8 + r0 + rows, :] = u
        z = (wc[0:1] * ubuf_ref[0, 6 + r0:6 + r0 + rows, :]
             + wc[1:2] * ubuf_ref[0, 7 + r0:7 + r0 + rows, :] + wc[2:3] * u)
        if last_sub:
            tail = u[rows - 2:rows, :]
            ubuf_ref[0, 6:8, :] = tail
            conv_out_ref[0] = tail
    else:
        u3 = u.reshape(nseq, chunk, D_CONV)
        ubuf_ref[:, 6:8, :] = conv_in_ref[...]
        ubuf_ref[:, 8:8 + chunk, :] = u3
        z3 = (wc[0:1] * ubuf_ref[:, 6:6 + chunk, :] + wc[1:2] * ubuf_ref[:, 7:7 + chunk, :]
              + wc[2:3] * u3)
        z = z3.reshape(rows, D_CONV)
        conv_out_ref[...] = ubuf_ref[:, 6 + chunk:8 + chunk, :]
    ya = _dot((cb * z).astype(BF16), wco_ref[...])
    v = proj(OFF_V, GLA_HEADS * GLA_DV)
    yield

    b3 = b.reshape(nchunks, chunk, D_CONV)
    bl3 = b3[:, chunk - 1:chunk, :]
    qin = (q * (GLA_DK ** -0.5)) * jnp.exp(b)
    kin = k * jnp.exp(-b)
    kst = (k.reshape(nchunks, chunk, D_CONV) * jnp.exp(bl3 - b3)).reshape(rows, D_CONV)
    pad = (-rows) % LANES
    rows_p = rows + pad

    def transposed(a):
        if pad:
            a = jnp.concatenate([a, jnp.zeros((pad, a.shape[1]), F32)], axis=0)
        return a.T

    b_t = transposed(b)
    kst_t = transposed(kst) if prompt else None
    col_chunk = lax.shift_right_arithmetic(
        lax.broadcasted_iota(jnp.int32, (GLA_DK, rows_p), 1), chunk.bit_length() - 1)
    g = proj(OFF_G, GLA_HEADS * GLA_DV)
    yield
    heads = range(GLA_HEADS)
    vhs = [v[:, hd * GLA_DV:(hd + 1) * GLA_DV].astype(BF16) for hd in heads]
    scs = [lax.dot_general(qin[:, hd * GLA_DK:(hd + 1) * GLA_DK].astype(BF16),
                           kin[:, hd * GLA_DK:(hd + 1) * GLA_DK].astype(BF16),
                           (((1,), (1,)), ((), ())), preferred_element_type=F32) for hd in heads]
    ga = proj(OFF_GA, D_MODEL)
    yield
    upds = []
    for hd in heads if prompt else ():
        kst_h = kst_t[hd * GLA_DK:(hd + 1) * GLA_DK, :]
        stacked = jnp.concatenate(
            [jnp.where(col_chunk == n, kst_h, 0.0).astype(BF16) for n in range(nchunks)], axis=0)
        vh_p = jnp.concatenate([vhs[hd], jnp.zeros((pad, GLA_DV), BF16)], axis=0) if pad else vhs[hd]
        upds.append(_dot(stacked, vh_p))
    o_intras = [_dot(jnp.where(causal, scs[hd], 0.0).astype(BF16), vhs[hd]) for hd in heads]
    gb = proj(OFF_GB, D_MODEL)
    merge_a = jax.nn.sigmoid(ga) * ya
    gate_b = jax.nn.sigmoid(gb)
    yield
    def q_block(n, hd):
        return qin[n * chunk:(n + 1) * chunk, hd * GLA_DK:(hd + 1) * GLA_DK].astype(BF16)

    def decay_col(n, hd):
        last = n * chunk + chunk - 1
        return jnp.exp(b_t[hd * GLA_DK:(hd + 1) * GLA_DK, last:last + 1])

    o_inter = [[None] * nchunks for _ in heads]
    if prompt:
        for hd in heads:
            s_run = s_ref[hd]
            for n in range(nchunks):
                o_inter[hd][n] = _dot(q_block(n, hd), s_run.astype(BF16))
                s_run = s_run * decay_col(n, hd) + upds[hd][n * GLA_DK:(n + 1) * GLA_DK, :]
            s_ref[hd] = s_run
            if last_sub:
                gla_out_ref[0, hd] = s_run
    else:
        for c in range(n_state_chunks):
            in_copy(c, step).wait()
            if c >= STATE_OUT_SLOTS:
                out_copy(c - STATE_OUT_SLOTS, step).wait()
            else:
                @pl.when(step > 0)
                def _():
                    out_copy(c + n_state_chunks - STATE_OUT_SLOTS, step - 1).wait()
            for j in range(STATE_CHUNK):
                n = c * STATE_CHUNK + j
                for hd in heads:
                    s_prev = sin_ref[c, j, hd]
                    o_inter[hd][n] = _dot(q_block(n, hd), s_prev.astype(BF16))
                    ksb = kst[n * chunk:(n + 1) * chunk, hd * GLA_DK:(hd + 1) * GLA_DK].astype(BF16)
                    vb = v[n * chunk:(n + 1) * chunk, hd * GLA_DV:(hd + 1) * GLA_DV].astype(BF16)
                    upd = lax.dot_general(ksb, vb, (((0,), (0,)), ((), ())),
                                          preferred_element_type=F32)
                    sout_ref[c % STATE_OUT_SLOTS, j, hd] = s_prev * decay_col(n, hd) + upd
            out_copy(c, step).start()

        @pl.when(step + 1 < n_steps)
        def _():
            for c in range(n_state_chunks):
                in_copy(c, step + 1).start()

        @pl.when(step + 1 == n_steps)
        def _():
            for c in range(n_state_chunks - STATE_OUT_SLOTS, n_state_chunks):
                out_copy(c, step).wait()
    gated = []
    for hd in heads:
        v0 = hd * GLA_DV
        oh = _rmsnorm(o_intras[hd] + jnp.concatenate(o_inter[hd], axis=0), gnorm_ref[...])
        gh = g[:, v0:v0 + GLA_DV]
        gated.append(oh * (gh * jax.nn.sigmoid(gh)))
    yield
    yb = _dot(jnp.concatenate(gated, axis=1).astype(BF16), wgo_ref[...])
    yield

    m = merge_a + gate_b * yb
    hh = x + _dot(m.astype(BF16), wo_ref[...])
    yield
    xn2 = _rmsnorm(hh, nffn_ref[...])
    yield
    logits_out.append(_router_logits(xn2, wr_ref, br_ref))
    if prompt:
        h_ref[0, r0:r0 + rows, :] = hh
        xn2_ref[0, r0:r0 + rows, :] = _pack_bf16_pairs(xn2)
    else:
        h_ref[...] = hh.reshape(nseq, chunk, D_MODEL)
        xn2_ref[...] = _pack_bf16_pairs(xn2).reshape(nseq, chunk, D_MODEL // 2)


def _mixer_kernel(*refs, prompt):
    n_scratch = 2 if prompt else 6
    outs = refs[len(refs) - n_scratch - 6:len(refs) - n_scratch]
    route_ref, route_t_ref = outs[2], outs[3]
    if prompt:
        s_ref, ubuf_ref = refs[-2:]

        @pl.when(pl.program_id(1) == 0)
        def _():
            s_ref[...] = jnp.zeros_like(s_ref)
            ubuf_ref[...] = jnp.zeros_like(ubuf_ref)
    nsub = PROMPT_TILE // GLA_BLOCK if prompt else 1
    logits = []
    stages = [_mixer_stages(sub, nsub, *refs, prompt=prompt, logits_out=logits)
              for sub in range(nsub)]
    for _ in itertools.zip_longest(*stages):
        pass
    route, route_t = _route(jnp.concatenate(logits, axis=0))
    if prompt:
        route_ref[0] = route
    else:
        route_ref[...] = route.reshape(route_ref.shape)
    route_t_ref[0] = route_t


def _in_proj_cast_kernel(wt_hbm, main_ref, gates_ref, alr_ref, buf, lr_buf, sem):
    i, n = pl.program_id(0), pl.num_programs(0)
    n_main = OFF_GA // CAST_ROWS

    def slab_copy(step):
        row = jnp.where(step < n_main, step * CAST_ROWS, step * CAST_ROWS + GLA_RANK)
        slot = step % 2
        return pltpu.make_async_copy(
            wt_hbm.at[pl.ds(pl.multiple_of(row, 8), CAST_ROWS)], buf.at[slot], sem.at[slot])

    @pl.when(i == 0)
    def _():
        slab_copy(i).start()
        low_rank_copy = pltpu.make_async_copy(
            wt_hbm.at[pl.ds(OFF_GA, GLA_RANK)], lr_buf.at[pl.ds(0, GLA_RANK)], sem.at[2])
        low_rank_copy.start()
        lr_buf[GLA_RANK:, :] = jnp.zeros((LANES - GLA_RANK, D_MODEL), F32)
        low_rank_copy.wait()
        alr_ref[...] = lr_buf[...].T.astype(BF16)

    @pl.when(i + 1 < n)
    def _():
        slab_copy(i + 1).start()

    slab_copy(i).wait()
    w = buf[i % 2].astype(BF16).T

    @pl.when(i < n_main)
    def _():
        main_ref[...] = w

    @pl.when(i >= n_main)
    def _():
        gates_ref[...] = w


def _in_proj_cast_call(wt):
    d_in, dm = wt.shape
    n_gate_cols = d_in - OFF_GA - GLA_RANK
    assert dm == D_MODEL and OFF_GA % CAST_ROWS == 0 and n_gate_cols % CAST_ROWS == 0
    assert GLA_RANK % 8 == 0
    n_main = OFF_GA // CAST_ROWS
    return pl.pallas_call(
        _in_proj_cast_kernel,
        grid=(n_main + n_gate_cols // CAST_ROWS,),
        in_specs=[pl.BlockSpec(memory_space=pl.ANY)],
        out_specs=[
            pl.BlockSpec((D_MODEL, CAST_ROWS), lambda i: (0, jnp.minimum(i, n_main - 1))),
            pl.BlockSpec((D_MODEL, CAST_ROWS), lambda i: (0, jnp.maximum(i - n_main, 0))),
            pl.BlockSpec((D_MODEL, LANES), lambda i: (0, 0)),
        ],
        out_shape=[jax.ShapeDtypeStruct((D_MODEL, OFF_GA), BF16),
                   jax.ShapeDtypeStruct((D_MODEL, n_gate_cols), BF16),
                   jax.ShapeDtypeStruct((D_MODEL, LANES), BF16)],
        scratch_shapes=[pltpu.VMEM((2, CAST_ROWS, D_MODEL), F32), pltpu.VMEM((LANES, D_MODEL), F32),
                        pltpu.SemaphoreType.DMA((3,))],
        compiler_params=pltpu.CompilerParams(
            dimension_semantics=("arbitrary",), vmem_limit_bytes=STREAM_VMEM_LIMIT),
        name="in_proj_cast",
    )(wt)


def _mixer_weights(norm_mix, w_in, w_conv, w_alpha_up, b_alpha, gla_norm, w_conv_out, w_gla_out, w_o,
                   norm_ffn, w_group_router, b_group_router, w_expert_router, b_expert_router):
    w_main, w_gates, w_alr = _in_proj_cast_call(w_in[0].T)
    walpha = jnp.concatenate(
        [w_alpha_up[0], jnp.zeros((LANES - GLA_RANK, D_CONV), F32)], axis=0).astype(BF16)
    wr = jnp.concatenate(
        [w_group_router[0], w_expert_router[0],
         jnp.zeros((D_MODEL, LANES - N_GROUPS - N_EXPERTS), F32)], axis=1)
    wr_hi = wr.astype(BF16)
    wr_split = jnp.concatenate([wr_hi, (wr - wr_hi.astype(F32)).astype(BF16)], axis=1)
    br = jnp.concatenate([b_group_router[0], b_expert_router[0],
                          jnp.zeros((LANES - N_GROUPS - N_EXPERTS,), F32)])[None, :]
    return (norm_mix, w_main, w_gates, w_alr, w_conv[0], walpha, b_alpha, gla_norm,
            w_conv_out[0].astype(BF16), w_gla_out[0].astype(BF16), w_o[0].astype(BF16),
            norm_ffn, wr_split, br)


def _const_spec(shape):
    nd = len(shape)
    return pl.BlockSpec(shape, lambda *_: (0,) * nd, pipeline_mode=pl.Buffered(1))


def _mixer_call(x, conv_state, gla_state, weights, *, prompt):
    nb, seq, _ = x.shape
    if prompt:
        grid = (nb, seq // PROMPT_TILE)
        tok = lambda last: pl.BlockSpec((1, PROMPT_TILE, last), lambda b, i: (b, i, 0))
        conv_spec = pl.BlockSpec((1, CONV_WIDTH - 1, D_CONV), lambda b, i: (b, 0, 0))
        gla_spec = pl.BlockSpec((1, GLA_HEADS, GLA_DK, GLA_DV), lambda b, i: (b, 0, 0, 0))
        gla_in_spec = pl.BlockSpec((1, 1, 8, LANES), lambda b, i: (b, 0, 0, 0))
        scratch = [pltpu.VMEM((GLA_HEADS, GLA_DK, GLA_DV), F32),
                   pltpu.VMEM((1, PROMPT_TILE + 8, D_CONV), F32)]
        route_t_shape = (nb, 8, seq)
        route_t_spec = pl.BlockSpec((1, 8, PROMPT_TILE), lambda b, i: (b, 0, i))
    else:
        grid = (nb // SAMPLE_SEQS, 1)
        tok = lambda last: pl.BlockSpec((SAMPLE_SEQS, seq, last), lambda b, i: (b, 0, 0))
        conv_spec = pl.BlockSpec((SAMPLE_SEQS, CONV_WIDTH - 1, D_CONV), lambda b, i: (b, 0, 0))
        gla_spec = pl.BlockSpec(memory_space=pl.ANY)
        gla_in_spec = gla_spec
        state_chunk = (STATE_CHUNK, GLA_HEADS, GLA_DK, GLA_DV)
        scratch = [pltpu.VMEM((1, 8, LANES), F32),
                   pltpu.VMEM((SAMPLE_SEQS, 8 + seq, D_CONV), F32),
                   pltpu.VMEM((SAMPLE_SEQS // STATE_CHUNK,) + state_chunk, F32),
                   pltpu.VMEM((STATE_OUT_SLOTS,) + state_chunk, F32),
                   pltpu.SemaphoreType.DMA((SAMPLE_SEQS // STATE_CHUNK,)),
                   pltpu.SemaphoreType.DMA((STATE_OUT_SLOTS,))]
        route_t_shape = (nb // SAMPLE_SEQS, 8, SAMPLE_SEQS * seq)
        route_t_spec = pl.BlockSpec((1, 8, SAMPLE_SEQS * seq), lambda b, i: (b, 0, 0))
    out_shape = (jax.ShapeDtypeStruct((nb, seq, D_MODEL), F32),
                 jax.ShapeDtypeStruct((nb, seq, D_MODEL // 2), jnp.uint32),
                 jax.ShapeDtypeStruct((nb, seq, LANES), F32),
                 jax.ShapeDtypeStruct(route_t_shape, F32),
                 jax.ShapeDtypeStruct((nb, CONV_WIDTH - 1, D_CONV), F32),
                 jax.ShapeDtypeStruct((nb, GLA_HEADS, GLA_DK, GLA_DV), F32))
    return pl.pallas_call(
        functools.partial(_mixer_kernel, prompt=prompt),
        grid=grid,
        in_specs=[tok(D_MODEL), conv_spec, gla_in_spec] + [_const_spec(w.shape) for w in weights],
        out_specs=(tok(D_MODEL), tok(D_MODEL // 2), tok(LANES), route_t_spec, conv_spec, gla_spec),
        out_shape=out_shape,
        scratch_shapes=scratch,
        compiler_params=pltpu.CompilerParams(
            dimension_semantics=("arbitrary", "arbitrary"), vmem_limit_bytes=MIXER_VMEM_LIMIT),
        name="mixer_prompt" if prompt else "mixer_sample",
    )(x, conv_state, gla_state, *weights)


def _rank_kernel(eid_ref, rank_ref, cnt_ref, carry_ref):
    @pl.when(pl.program_id(0) == 0)
    def _():
        carry_ref[...] = jnp.zeros_like(carry_ref)

    n = RANK_BLOCK
    e_iota = lax.broadcasted_iota(jnp.int32, (N_EXPERTS, n), 0)
    r = lax.broadcasted_iota(jnp.int32, (n, n), 0)
    c = lax.broadcasted_iota(jnp.int32, (n, n), 1)
    before = jnp.where(r < c, 1.0, 0.0).astype(BF16)
    carry = carry_ref[...]
    for j in range(eid_ref.shape[1] // n):
        cols = slice(j * n, (j + 1) * n)
        oh0 = jnp.where(e_iota == eid_ref[0:1, cols], 1.0, 0.0)
        oh1 = jnp.where(e_iota == eid_ref[1:2, cols], 1.0, 0.0)
        cnt = oh0 + oh1
        base = carry + _dot(cnt.astype(BF16), before)
        rank0 = jnp.sum(oh0 * base, axis=0, keepdims=True)
        rank1 = jnp.sum(oh1 * base, axis=0, keepdims=True)
        rank_ref[:, cols] = jnp.concatenate([rank0, rank1], axis=0).astype(jnp.int32)
        carry = carry + jnp.sum(cnt, axis=1, keepdims=True)
    carry_ref[...] = carry
    cnt_ref[...] = jnp.broadcast_to(carry, cnt_ref.shape)


def _rank_call(eid):
    ntok = eid.shape[1]
    return pl.pallas_call(
        _rank_kernel,
        grid=(ntok // RANK_TILE,),
        in_specs=[pl.BlockSpec((2, RANK_TILE), lambda i: (0, i))],
        out_specs=(pl.BlockSpec((2, RANK_TILE), lambda i: (0, i)),
                   pl.BlockSpec((N_EXPERTS, LANES), lambda i: (0, 0))),
        out_shape=(jax.ShapeDtypeStruct((2, ntok), jnp.int32),
                   jax.ShapeDtypeStruct((N_EXPERTS, LANES), F32)),
        scratch_shapes=[pltpu.VMEM((N_EXPERTS, 1), F32)],
        compiler_params=pltpu.CompilerParams(dimension_semantics=("arbitrary",)),
        name="expert_rank",
    )(eid)


def _sc_mesh():
    return plsc.VectorSubcoreMesh(core_axis_name="c", subcore_axis_name="s")


def _sc_worker_windows(n_windows, body):
    info = plsc.get_sparse_core_info()
    n_workers = info.num_cores * info.num_subcores
    wid = lax.axis_index("s") * info.num_cores + lax.axis_index("c")

    @pl.loop(0, pl.cdiv(n_windows, n_workers))
    def _(j):
        win = wid + n_workers * j

        @pl.when(win < n_windows)
        def _():
            body(win)


def _dispatch_rows(sources, pos, n_rows):
    width, dtype = sources[0].shape[1], sources[0].dtype
    n_win = [src.shape[0] // SC_WINDOW for src in sources]
    first_win = [sum(n_win[:i]) for i in range(len(sources))]
    total_win = sum(n_win)
    pos3 = pos.reshape(2, total_win, SC_WINDOW).transpose(1, 0, 2)

    @functools.partial(pl.kernel, mesh=_sc_mesh(), name="moe_dispatch",
                       out_type=jax.ShapeDtypeStruct((n_rows, width), dtype),
                       scratch_types=[pltpu.VMEM((2, SC_WINDOW), jnp.int32),
                                      pltpu.VMEM((SC_WINDOW, width), dtype)])
    def run(*refs):
        src_hbm, (pos_hbm, out_hbm, idx_v, buf) = refs[:len(sources)], refs[len(sources):]

        def window(win):
            pltpu.sync_copy(pos_hbm.at[win], idx_v)
            for src, w0, nw in zip(src_hbm, first_win, n_win):
                @pl.when((win >= w0) & (win < w0 + nw))
                def _():
                    r0 = pl.multiple_of((win - w0) * SC_WINDOW, SC_WINDOW)
                    pltpu.sync_copy(src.at[pl.ds(r0, SC_WINDOW)], buf)

            pltpu.sync_copy(buf, out_hbm.at[idx_v.at[0]])
            pltpu.sync_copy(buf, out_hbm.at[idx_v.at[1]])

        _sc_worker_windows(total_win, window)

    return run(*sources, pos3)


def _return_rows(ys, idx):
    n_out, width = idx.shape[0], ys.shape[1]
    n_windows = n_out // SC_WINDOW

    @functools.partial(pl.kernel, mesh=_sc_mesh(), name="moe_return",
                       out_type=jax.ShapeDtypeStruct((n_out, width), ys.dtype),
                       scratch_types=[pltpu.VMEM((SC_WINDOW,), jnp.int32),
                                      pltpu.VMEM((SC_WINDOW, width), ys.dtype)])
    def run(ys_hbm, idx_hbm, out_hbm, idx_v, buf):
        def window(win):
            pltpu.sync_copy(idx_hbm.at[win], idx_v)
            pltpu.sync_copy(ys_hbm.at[idx_v], buf)
            r0 = pl.multiple_of(win * SC_WINDOW, SC_WINDOW)
            pltpu.sync_copy(buf, out_hbm.at[pl.ds(r0, SC_WINDOW)])

        _sc_worker_windows(n_windows, window)

    return run(ys, idx.reshape(n_windows, SC_WINDOW))


def _expert_kernel(first_ref, ntile_ref, nrow_ref, total_ref, xs_hbm, wg_ref, wu_ref, wd_ref, out_hbm,
                   xbuf, obuf, wg_s, wu_s, wd_s, sem_in, sem_out):
    e = pl.program_id(0)
    nt = ntile_ref[e]
    g0 = first_ref[e]
    total = total_ref[0]
    ring = EXPERT_RING

    def rows_of(g):
        return pl.ds(pl.multiple_of(g * ROW_TILE, ROW_TILE), ROW_TILE)

    def in_copy(g):
        slot = lax.rem(g, ring)
        return pltpu.make_async_copy(xs_hbm.at[rows_of(g)], xbuf.at[slot], sem_in.at[slot])

    def out_copy(g):
        slot = lax.rem(g, ring)
        return pltpu.make_async_copy(obuf.at[slot], out_hbm.at[rows_of(g)], sem_out.at[slot])

    @pl.when(e == 0)
    def _():
        for g in range(EXPERT_AHEAD):
            @pl.when(g < total)
            def _():
                in_copy(g).start(priority=ROW_DMA_PRIORITY)

    @pl.when(nt > 0)
    def _():
        wg_s[...] = wg_ref[0].astype(BF16)
        wu_s[...] = wu_ref[0].astype(BF16)
        wd_s[...] = wd_ref[0].astype(BF16)

    half = D_MODEL // 2
    sub = ROW_TILE // 2

    def tiles(gs):
        halves = []
        for g in gs:
            slot = lax.rem(g, ring)
            in_copy(g).wait()

            @pl.when(g + EXPERT_AHEAD < total)
            def _():
                in_copy(g + EXPERT_AHEAD).start(priority=ROW_DMA_PRIORITY)

            @pl.when(g >= ring)
            def _():
                out_copy(g - ring).wait()

            n_valid = nrow_ref[e] - (g - g0) * ROW_TILE
            row = lax.broadcasted_iota(jnp.int32, (ROW_TILE, 1), 0)
            lo, hi = _unpack_bf16_pairs(jnp.where(row < n_valid, xbuf[slot], jnp.uint32(0)))
            for r in (0, sub):
                xl, xh = lo[r:r + sub].astype(BF16), hi[r:r + sub].astype(BF16)
                gt = _dot(xl, wg_s[:half, :]) + _dot(xh, wg_s[half:, :])
                up = _dot(xl, wu_s[:half, :]) + _dot(xh, wu_s[half:, :])
                halves.append((slot, r, gt, up))
        for slot, r, gt, up in halves:
            act = (gt * jax.nn.sigmoid(gt)) * up
            obuf[slot, r:r + sub, :] = _pack_bf16_pairs(_dot(act.astype(BF16), wd_s[...]))
        for g in gs:
            out_copy(g).start(priority=ROW_DMA_PRIORITY)

    def pair(i, carry):
        tiles([g0 + 2 * i, g0 + 2 * i + 1])
        return carry

    lax.fori_loop(0, nt // 2, pair, 0)

    @pl.when(lax.rem(nt, 2) == 1)
    def _():
        tiles([g0 + nt - 1])

    @pl.when(e == pl.num_programs(0) - 1)
    def _():
        for k in range(1, ring + 1):
            @pl.when(total >= k)
            def _():
                out_copy(total - k).wait()


def _expert_call(first_tile, n_tiles_per_expert, n_rows_per_expert, total_tiles, xs,
                 w_gate, w_up, w_down):
    nrows = xs.shape[0]
    by_expert = lambda e, *_: (e, 0, 0)
    half_row = D_MODEL // 2
    return pl.pallas_call(
        _expert_kernel,
        grid_spec=pltpu.PrefetchScalarGridSpec(
            num_scalar_prefetch=4,
            grid=(N_EXPERTS,),
            in_specs=[pl.BlockSpec(memory_space=pl.ANY),
                      pl.BlockSpec((1, D_MODEL, D_EXPERT), by_expert),
                      pl.BlockSpec((1, D_MODEL, D_EXPERT), by_expert),
                      pl.BlockSpec((1, D_EXPERT, D_MODEL), by_expert)],
            out_specs=pl.BlockSpec(memory_space=pl.ANY),
            scratch_shapes=[pltpu.VMEM((EXPERT_RING, ROW_TILE, half_row), jnp.uint32),
                            pltpu.VMEM((EXPERT_RING, ROW_TILE, half_row), jnp.uint32),
                            pltpu.VMEM((D_MODEL, D_EXPERT), BF16),
                            pltpu.VMEM((D_MODEL, D_EXPERT), BF16),
                            pltpu.VMEM((D_EXPERT, D_MODEL), BF16),
                            pltpu.SemaphoreType.DMA((EXPERT_RING,)),
                            pltpu.SemaphoreType.DMA((EXPERT_RING,))]),
        out_shape=jax.ShapeDtypeStruct((nrows, half_row), jnp.uint32),
        compiler_params=pltpu.CompilerParams(
            dimension_semantics=("arbitrary",), vmem_limit_bytes=STREAM_VMEM_LIMIT),
        name="expert_mlp",
    )(first_tile, n_tiles_per_expert, n_rows_per_expert, total_tiles, xs, w_gate, w_up, w_down)


def _combine_kernel(h_ref, y2_ref, route_ref, nfin_ref, out_ref):
    route = route_ref[...]
    w1, w2 = route[:, 0:1], route[:, 1:2]
    lo1, hi1 = _unpack_bf16_pairs(y2_ref[0])
    lo2, hi2 = _unpack_bf16_pairs(y2_ref[1])
    y = h_ref[...] + jnp.concatenate([w1 * lo1 + w2 * lo2, w1 * hi1 + w2 * hi2], axis=1)
    out_ref[...] = _rmsnorm(y, nfin_ref[...])


def _combine_call(h, y2, route, nfin, first_token):
    n = h.shape[0]
    first_block = first_token // COMBINE_TILE
    return pl.pallas_call(
        _combine_kernel,
        grid=(n // COMBINE_TILE,),
        in_specs=[pl.BlockSpec((COMBINE_TILE, D_MODEL), lambda i: (i, 0)),
                  pl.BlockSpec((2, COMBINE_TILE, D_MODEL // 2), lambda i: (0, first_block + i, 0)),
                  pl.BlockSpec((COMBINE_TILE, LANES), lambda i: (i, 0)),
                  pl.BlockSpec((1, D_MODEL), lambda i: (0, 0))],
        out_specs=pl.BlockSpec((COMBINE_TILE, D_MODEL), lambda i: (i, 0)),
        out_shape=jax.ShapeDtypeStruct((n, D_MODEL), F32),
        compiler_params=pltpu.CompilerParams(
            dimension_semantics=("arbitrary",), vmem_limit_bytes=STREAM_VMEM_LIMIT),
        name="moe_combine",
    )(h, y2, route, nfin)


def _moe_rows(parts, w_gate, w_up, w_down):
    eid = jnp.concatenate([ids for _, ids in parts], axis=1).astype(jnp.int32)
    ntok = eid.shape[1]
    assert ntok % RANK_TILE == 0 and ntok % SC_WINDOW == 0
    rank, cnt = _rank_call(eid)
    counts = cnt[:, 0].astype(jnp.int32)
    padded = ((counts + ROW_TILE - 1) // ROW_TILE) * ROW_TILE
    ends = jnp.cumsum(padded)
    starts = ends - padded
    expert_iota = jnp.arange(N_EXPERTS, dtype=jnp.int32)[:, None, None]
    pos = rank + jnp.sum(jnp.where(eid[None] == expert_iota, starts[:, None, None], 0), axis=0)
    n_rows = (2 * ntok + N_EXPERTS * (ROW_TILE - 1)) // ROW_TILE * ROW_TILE
    xs = _dispatch_rows([rows for rows, _ in parts], pos, n_rows)
    ys = _expert_call(starts // ROW_TILE, padded // ROW_TILE, counts, ends[-1:] // ROW_TILE, xs,
                      w_gate, w_up, w_down)
    return _return_rows(ys, pos.reshape(-1)).reshape(2, ntok, D_MODEL // 2)


def kernel(x_prompt, x_sample, state_conv, state_gla, norm_mix, w_in, w_conv, w_alpha_up, b_alpha,
           gla_norm, w_conv_out, w_gla_out, w_o, norm_ffn, w_group_router, b_group_router,
           w_expert_router, b_expert_router, w_gate, w_up, w_down, norm_final):
    nbp, seq_p, _ = x_prompt.shape
    nbs, seq_s, _ = x_sample.shape
    assert norm_mix.shape[0] == 1, "single layer"
    assert seq_p % PROMPT_TILE == 0 and nbs % SAMPLE_SEQS == 0 and seq_s == 8

    weights = _mixer_weights(norm_mix, w_in, w_conv, w_alpha_up, b_alpha, gla_norm, w_conv_out,
                             w_gla_out, w_o, norm_ffn, w_group_router, b_group_router,
                             w_expert_router, b_expert_router)

    ntp, nts = nbp * seq_p, nbs * seq_s
    assert ntp % COMBINE_TILE == 0 and nts % COMBINE_TILE == 0
    conv0 = jnp.zeros((nbp, CONV_WIDTH - 1, D_CONV), F32)
    gla0 = jnp.zeros((nbp, 1, 8, LANES), F32)
    flat = lambda a: a.reshape(-1, a.shape[-1])
    ids_of = lambda route_t: route_t[:, 2:4, :].transpose(1, 0, 2).reshape(2, -1)
    h_p, xn_p, rt_p, rtt_p, conv_p, gla_p = _mixer_call(x_prompt, conv0, gla0, weights, prompt=True)
    h_s, xn_s, rt_s, rtt_s, conv_s, gla_s = _mixer_call(x_sample, state_conv[0], state_gla[0],
                                                        weights, prompt=False)
    y2 = _moe_rows([(flat(xn_p), ids_of(rtt_p)), (flat(xn_s), ids_of(rtt_s))],
                   w_gate[0], w_up[0], w_down[0])
    nfin = norm_final[None, :]
    y_p = _combine_call(flat(h_p), y2, flat(rt_p), nfin, 0)
    y_s = _combine_call(flat(h_s), y2, flat(rt_s), nfin, ntp)
    return (y_p.reshape(nbp, seq_p, D_MODEL), y_s.reshape(nbs, seq_s, D_MODEL),
            conv_p[None], gla_p[None], conv_s[None], gla_s[None])
```

```python
import functools
import itertools

import jax
import jax.numpy as jnp
from jax import lax
from jax.experimental import pallas as pl
from jax.experimental.pallas import tpu as pltpu
from jax.experimental.pallas import tpu_sc as plsc

F32 = jnp.float32
BF16 = jnp.bfloat16

D_MODEL = 1024
D_CONV = 512
CONV_WIDTH = 3
GLA_HEADS = 4
GLA_DK = 128
GLA_DV = 256
GLA_RANK = 16
GLA_GATE_NORM = 16.0
GLA_CHUNK = 32
N_GROUPS = 4
EXPERTS_PER_GROUP = 8
N_EXPERTS = 32
D_EXPERT = 256
EPS = 1e-6

LANES = 128
V7X_VMEM_BYTES = 64 * 1024 * 1024
OFF_CB, OFF_CC, OFF_CH = 0, 512, 1024
OFF_Q, OFF_K, OFF_V, OFF_G = 1536, 2048, 2560, 3584
OFF_GA, OFF_GB, OFF_AL = 4608, 5632, 6656

PROMPT_TILE = 512
GLA_BLOCK = 256
SAMPLE_SEQS = 32
STATE_CHUNK = 4
STATE_OUT_SLOTS = 4
ROW_TILE = 256
EXPERT_RING = 6
EXPERT_AHEAD = 4
SC_WINDOW = 128
COMBINE_TILE = 1024
RANK_TILE = 1024
CAST_ROWS = 512
CAST_SLOTS = 3
RANK_BLOCK = 512
VMEM_HEADROOM_BYTES = 6 * 1024 * 1024
MIXER_VMEM_LIMIT = V7X_VMEM_BYTES - VMEM_HEADROOM_BYTES
STREAM_VMEM_LIMIT = MIXER_VMEM_LIMIT


def _rmsnorm(x, g):
    ms = jnp.mean(x * x, axis=-1, keepdims=True)
    return (x * lax.rsqrt(ms + EPS)) * g


def _split_bf16(x):
    hi = x.astype(BF16)
    lo = (x - hi.astype(F32)).astype(BF16)
    return hi, lo


def _pack_bf16_pairs(x):
    n = x.shape[1] // 2
    lo = lax.bitcast_convert_type(x[:, :n].astype(BF16).astype(F32), jnp.uint32)
    hi = lax.bitcast_convert_type(x[:, n:].astype(BF16).astype(F32), jnp.uint32)
    return lax.shift_right_logical(lo, jnp.uint32(16)) | hi


def _unpack_bf16_pairs(w):
    lo = lax.bitcast_convert_type(lax.shift_left(w, jnp.uint32(16)), F32)
    hi = lax.bitcast_convert_type(w & jnp.uint32(0xFFFF0000), F32)
    return lo, hi


def _dot(a, b):
    return jnp.dot(a, b, preferred_element_type=F32)


def _chunk_causal(n, chunk):
    r = lax.broadcasted_iota(jnp.int32, (n, n), 0)
    c = lax.broadcasted_iota(jnp.int32, (n, n), 1)
    shift = chunk.bit_length() - 1
    assert chunk == 1 << shift
    same = lax.shift_right_arithmetic(r, shift) == lax.shift_right_arithmetic(c, shift)
    return same & (c <= r)


def _router_logits(xn2, wr_ref, br_ref):
    xh, xl = _split_bf16(xn2)
    part = _dot(xh, wr_ref[...]) + _dot(xl, wr_ref[...])
    return part[:, :LANES] + part[:, LANES:] + br_ref[...]


def _route(lg):
    n = lg.shape[0]
    n_rows = 40
    lt = lg.T[:n_rows, :]
    row_i = lax.broadcasted_iota(jnp.int32, (n_rows, n), 0)
    row = row_i.astype(F32)
    neg = -jnp.inf
    none = float(LANES)
    is_g = row_i < N_GROUPS
    gm = jnp.max(jnp.where(is_g, lt, neg), axis=0, keepdims=True)
    gs = jnp.sum(jnp.where(is_g, jnp.exp(lt - gm), 0.0), axis=0, keepdims=True)
    g_top = 1.0 / gs
    g_idx = jnp.min(jnp.where(is_g & (lt == gm), row, none), axis=0, keepdims=True)
    e_group = lax.shift_right_arithmetic(row_i - N_GROUPS, 3).astype(F32)
    sel = (row_i >= N_GROUPS) & (row_i < N_GROUPS + N_EXPERTS) & (e_group == g_idx)
    el = jnp.where(sel, lt, neg)
    e1 = jnp.max(el, axis=0, keepdims=True)
    i1 = jnp.min(jnp.where(sel & (el == e1), row, none), axis=0, keepdims=True)
    sel2 = sel & (row != i1)
    el2 = jnp.where(sel2, lt, neg)
    e2 = jnp.max(el2, axis=0, keepdims=True)
    i2 = jnp.min(jnp.where(sel2 & (el2 == e2), row, none), axis=0, keepdims=True)
    d = jnp.exp(e2 - e1)
    w1 = g_top / (1.0 + d)
    w2 = (g_top * d) / (1.0 + d)
    id1 = i1 - float(N_GROUPS)
    id2 = i2 - float(N_GROUPS)
    out_i = lax.broadcasted_iota(jnp.int32, (8, n), 0)
    route_t = jnp.where(out_i == 0, w1, jnp.where(out_i == 1, w2, jnp.where(out_i == 2, id1,
                        jnp.where(out_i == 3, id2, 0.0))))
    route = jnp.concatenate([route_t, jnp.zeros((LANES - 8, n), F32)], axis=0).T
    return route, route_t


def _mixer_stages(sub, nsub, x_ref, conv_in_ref, gla_in_ref,
                  nmix_ref, w_ref, wgate_ref, walr_ref, wconv_ref, walpha_ref, balpha_ref, gnorm_ref,
                  wco_ref, wgo_ref, wo_ref, nffn_ref, wr_ref, br_ref,
                  h_ref, xn2_ref, route_ref, route_t_ref, conv_out_ref, gla_out_ref,
                  s_ref, ubuf_ref, *state_ring, prompt, logits_out):
    if prompt:
        rows, chunk, nseq = GLA_BLOCK, GLA_CHUNK, 1
        r0 = sub * rows
        x = x_ref[0, r0:r0 + rows, :]
    else:
        nseq, chunk = SAMPLE_SEQS, x_ref.shape[1]
        rows = nseq * chunk
        r0 = 0
        x = x_ref[...].reshape(rows, D_MODEL)
        sin_ref, sout_ref, sem_in, sem_out = state_ring
        n_state_chunks = nseq // STATE_CHUNK
        step, n_steps = pl.program_id(0), pl.num_programs(0)

        def chunk_seqs(c, of_step):
            first = of_step * nseq + c * STATE_CHUNK
            return pl.ds(pl.multiple_of(first, STATE_CHUNK), STATE_CHUNK)

        def in_copy(c, of_step):
            return pltpu.make_async_copy(gla_in_ref.at[chunk_seqs(c, of_step)], sin_ref.at[c],
                                         sem_in.at[c])

        def out_copy(c, of_step):
            slot = c % STATE_OUT_SLOTS
            return pltpu.make_async_copy(sout_ref.at[slot], gla_out_ref.at[chunk_seqs(c, of_step)],
                                         sem_out.at[slot])

        @pl.when(step == 0)
        def _():
            for c in range(n_state_chunks):
                in_copy(c, step).start()
    nchunks = rows // chunk
    last_sub = sub == nsub - 1

    xn = _rmsnorm(x, nmix_ref[...]).astype(BF16)
    yield

    def proj(off, n):
        if off == OFF_AL:
            return _dot(xn, walr_ref[...])
        if off >= OFF_GA:
            return _dot(xn, wgate_ref[:, off - OFF_GA:off - OFF_GA + n])
        return _dot(xn, w_ref[:, off:off + n])

    alr = proj(OFF_AL, LANES).astype(BF16)
    lap = _dot(alr, walpha_ref[...])
    la = jax.nn.log_sigmoid(lap + balpha_ref[...]) * (1.0 / GLA_GATE_NORM)
    la_hi, la_lo = _split_bf16(la)
    causal = _chunk_causal(rows, chunk)
    tri = jnp.where(causal, 1.0, 0.0).astype(BF16)
    yield
    cc = proj(OFF_CC, D_CONV)
    ch = proj(OFF_CH, D_CONV)
    cb = proj(OFF_CB, D_CONV)
    q = proj(OFF_Q, D_CONV)
    k = proj(OFF_K, D_CONV)
    yield
    b = _dot(tri, la_hi) + _dot(tri, la_lo)

    u = cc * ch
    wc = wconv_ref[...]
    if prompt:
        ubuf_ref[0, 8 + r0:8 + r0 + rows, :] = u
        z = (wc[0:1] * ubuf_ref[0, 6 + r0:6 + r0 + rows, :]
             + wc[1:2] * ubuf_ref[0, 7 + r0:7 + r0 + rows, :] + wc[2:3] * u)
        if last_sub:
            tail = u[rows - 2:rows, :]
            ubuf_ref[0, 6:8, :] = tail
            conv_out_ref[0] = tail
    else:
        u3 = u.reshape(nseq, chunk, D_CONV)
        ubuf_ref[:, 6:8, :] = conv_in_ref[...]
        ubuf_ref[:, 8:8 + chunk, :] = u3
        z3 = (wc[0:1] * ubuf_ref[:, 6:6 + chunk, :] + wc[1:2] * ubuf_ref[:, 7:7 + chunk, :]
              + wc[2:3] * u3)
        z = z3.reshape(rows, D_CONV)
        conv_out_ref[...] = ubuf_ref[:, 6 + chunk:8 + chunk, :]
    ya = _dot((cb * z).astype(BF16), wco_ref[...])
    v = proj(OFF_V, GLA_HEADS * GLA_DV)
    yield

    b3 = b.reshape(nchunks, chunk, D_CONV)
    bl3 = b3[:, chunk - 1:chunk, :]
    qin = (q * (GLA_DK ** -0.5)) * jnp.exp(b)
    kin = k * jnp.exp(-b)
    kst = (k.reshape(nchunks, chunk, D_CONV) * jnp.exp(bl3 - b3)).reshape(rows, D_CONV)
    pad = (-rows) % LANES
    rows_p = rows + pad

    def transposed(a):
        if pad:
            a = jnp.concatenate([a, jnp.zeros((pad, a.shape[1]), F32)], axis=0)
        return a.T

    b_t = transposed(b)
    kst_t = transposed(kst) if prompt else None
    col_chunk = lax.shift_right_arithmetic(
        lax.broadcasted_iota(jnp.int32, (GLA_DK, rows_p), 1), chunk.bit_length() - 1)
    g = proj(OFF_G, GLA_HEADS * GLA_DV)
    yield
    heads = range(GLA_HEADS)
    vhs = [v[:, hd * GLA_DV:(hd + 1) * GLA_DV].astype(BF16) for hd in heads]
    scs = [lax.dot_general(qin[:, hd * GLA_DK:(hd + 1) * GLA_DK].astype(BF16),
                           kin[:, hd * GLA_DK:(hd + 1) * GLA_DK].astype(BF16),
                           (((1,), (1,)), ((), ())), preferred_element_type=F32) for hd in heads]
    ga = proj(OFF_GA, D_MODEL)
    yield
    upds = []
    for hd in heads if prompt else ():
        kst_h = kst_t[hd * GLA_DK:(hd + 1) * GLA_DK, :]
        stacked = jnp.concatenate(
            [jnp.where(col_chunk == n, kst_h, 0.0).astype(BF16) for n in range(nchunks)], axis=0)
        vh_p = jnp.concatenate([vhs[hd], jnp.zeros((pad, GLA_DV), BF16)], axis=0) if pad else vhs[hd]
        upds.append(_dot(stacked, vh_p))
    o_intras = [_dot(jnp.where(causal, scs[hd], 0.0).astype(BF16), vhs[hd]) for hd in heads]
    gb = proj(OFF_GB, D_MODEL)
    merge_a = jax.nn.sigmoid(ga) * ya
    gate_b = jax.nn.sigmoid(gb)
    yield
    def q_block(n, hd):
        return qin[n * chunk:(n + 1) * chunk, hd * GLA_DK:(hd + 1) * GLA_DK].astype(BF16)

    def decay_col(n, hd):
        last = n * chunk + chunk - 1
        return jnp.exp(b_t[hd * GLA_DK:(hd + 1) * GLA_DK, last:last + 1])

    o_inter = [[None] * nchunks for _ in heads]
    if prompt:
        for hd in heads:
            s_run = s_ref[hd]
            for n in range(nchunks):
                o_inter[hd][n] = _dot(q_block(n, hd), s_run.astype(BF16))
                s_run = s_run * decay_col(n, hd) + upds[hd][n * GLA_DK:(n + 1) * GLA_DK, :]
            s_ref[hd] = s_run
            if last_sub:
                gla_out_ref[0, hd] = s_run
    else:
        for c in range(n_state_chunks):
            in_copy(c, step).wait()
            if c >= STATE_OUT_SLOTS:
                out_copy(c - STATE_OUT_SLOTS, step).wait()
            else:
                @pl.when(step > 0)
                def _():
                    out_copy(c + n_state_chunks - STATE_OUT_SLOTS, step - 1).wait()
            for j in range(STATE_CHUNK):
                n = c * STATE_CHUNK + j
                for hd in heads:
                    s_prev = sin_ref[c, j, hd]
                    o_inter[hd][n] = _dot(q_block(n, hd), s_prev.astype(BF16))
                    ksb = kst[n * chunk:(n + 1) * chunk, hd * GLA_DK:(hd + 1) * GLA_DK].astype(BF16)
                    vb = v[n * chunk:(n + 1) * chunk, hd * GLA_DV:(hd + 1) * GLA_DV].astype(BF16)
                    upd = lax.dot_general(ksb, vb, (((0,), (0,)), ((), ())),
                                          preferred_element_type=F32)
                    sout_ref[c % STATE_OUT_SLOTS, j, hd] = s_prev * decay_col(n, hd) + upd
            out_copy(c, step).start()

        @pl.when(step + 1 < n_steps)
        def _():
            for c in range(n_state_chunks):
                in_copy(c, step + 1).start()

        @pl.when(step + 1 == n_steps)
        def _():
            for c in range(n_state_chunks - STATE_OUT_SLOTS, n_state_chunks):
                out_copy(c, step).wait()
    gated = []
    for hd in heads:
        v0 = hd * GLA_DV
        oh = _rmsnorm(o_intras[hd] + jnp.concatenate(o_inter[hd], axis=0), gnorm_ref[...])
        gh = g[:, v0:v0 + GLA_DV]
        gated.append(oh * (gh * jax.nn.sigmoid(gh)))
    yield
    yb = _dot(jnp.concatenate(gated, axis=1).astype(BF16), wgo_ref[...])
    yield

    m = merge_a + gate_b * yb
    hh = x + _dot(m.astype(BF16), wo_ref[...])
    yield
    xn2 = _rmsnorm(hh, nffn_ref[...])
    yield
    logits_out.append(_router_logits(xn2, wr_ref, br_ref))
    if prompt:
        h_ref[0, r0:r0 + rows, :] = hh
        xn2_ref[0, r0:r0 + rows, :] = _pack_bf16_pairs(xn2)
    else:
        h_ref[...] = hh.reshape(nseq, chunk, D_MODEL)
        xn2_ref[...] = _pack_bf16_pairs(xn2).reshape(nseq, chunk, D_MODEL // 2)


def _mixer_kernel(*refs, prompt):
    n_scratch = 2 if prompt else 6
    outs = refs[len(refs) - n_scratch - 6:len(refs) - n_scratch]
    route_ref, route_t_ref = outs[2], outs[3]
    if prompt:
        s_ref, ubuf_ref = refs[-2:]

        @pl.when(pl.program_id(1) == 0)
        def _():
            s_ref[...] = jnp.zeros_like(s_ref)
            ubuf_ref[...] = jnp.zeros_like(ubuf_ref)
    nsub = PROMPT_TILE // GLA_BLOCK if prompt else 1
    logits = []
    stages = [_mixer_stages(sub, nsub, *refs, prompt=prompt, logits_out=logits)
              for sub in range(nsub)]
    for _ in itertools.zip_longest(*stages):
        pass
    route, route_t = _route(jnp.concatenate(logits, axis=0))
    if prompt:
        route_ref[0] = route
    else:
        route_ref[...] = route.reshape(route_ref.shape)
    route_t_ref[0] = route_t


def _in_proj_cast_kernel(wt_hbm, main_ref, gates_ref, alr_ref, buf, lr_buf, sem):
    i, n = pl.program_id(0), pl.num_programs(0)
    n_main = OFF_GA // CAST_ROWS

    def slab_copy(step):
        row = jnp.where(step < n_main, step * CAST_ROWS, step * CAST_ROWS + GLA_RANK)
        slot = step % CAST_SLOTS
        return pltpu.make_async_copy(
            wt_hbm.at[pl.ds(pl.multiple_of(row, 8), CAST_ROWS)], buf.at[slot], sem.at[slot])

    @pl.when(i == 0)
    def _():
        for k in range(CAST_SLOTS - 1):
            slab_copy(k).start()
        low_rank_copy = pltpu.make_async_copy(
            wt_hbm.at[pl.ds(OFF_GA, GLA_RANK)], lr_buf.at[pl.ds(0, GLA_RANK)], sem.at[CAST_SLOTS])
        low_rank_copy.start()
        lr_buf[GLA_RANK:, :] = jnp.zeros((LANES - GLA_RANK, D_MODEL), F32)
        low_rank_copy.wait()
        alr_ref[...] = lr_buf[...].T.astype(BF16)

    @pl.when(i + CAST_SLOTS - 1 < n)
    def _():
        slab_copy(i + CAST_SLOTS - 1).start()

    slab_copy(i).wait()
    w = buf[i % CAST_SLOTS].astype(BF16).T

    @pl.when(i < n_main)
    def _():
        main_ref[...] = w

    @pl.when(i >= n_main)
    def _():
        gates_ref[...] = w


def _in_proj_cast_call(wt):
    d_in, dm = wt.shape
    n_gate_cols = d_in - OFF_GA - GLA_RANK
    assert dm == D_MODEL and OFF_GA % CAST_ROWS == 0 and n_gate_cols % CAST_ROWS == 0
    assert GLA_RANK % 8 == 0
    n_main = OFF_GA // CAST_ROWS
    assert n_main + n_gate_cols // CAST_ROWS >= CAST_SLOTS
    return pl.pallas_call(
        _in_proj_cast_kernel,
        grid=(n_main + n_gate_cols // CAST_ROWS,),
        in_specs=[pl.BlockSpec(memory_space=pl.ANY)],
        out_specs=[
            pl.BlockSpec((D_MODEL, CAST_ROWS), lambda i: (0, jnp.minimum(i, n_main - 1))),
            pl.BlockSpec((D_MODEL, CAST_ROWS), lambda i: (0, jnp.maximum(i - n_main, 0))),
            pl.BlockSpec((D_MODEL, LANES), lambda i: (0, 0)),
        ],
        out_shape=[jax.ShapeDtypeStruct((D_MODEL, OFF_GA), BF16),
                   jax.ShapeDtypeStruct((D_MODEL, n_gate_cols), BF16),
                   jax.ShapeDtypeStruct((D_MODEL, LANES), BF16)],
        scratch_shapes=[pltpu.VMEM((CAST_SLOTS, CAST_ROWS, D_MODEL), F32),
                        pltpu.VMEM((LANES, D_MODEL), F32),
                        pltpu.SemaphoreType.DMA((CAST_SLOTS + 1,))],
        compiler_params=pltpu.CompilerParams(
            dimension_semantics=("arbitrary",), vmem_limit_bytes=STREAM_VMEM_LIMIT),
        name="in_proj_cast",
    )(wt)


def _mixer_weights(norm_mix, w_in, w_conv, w_alpha_up, b_alpha, gla_norm, w_conv_out, w_gla_out, w_o,
                   norm_ffn, w_group_router, b_group_router, w_expert_router, b_expert_router):
    w_main, w_gates, w_alr = _in_proj_cast_call(w_in[0].T)
    walpha = jnp.concatenate(
        [w_alpha_up[0], jnp.zeros((LANES - GLA_RANK, D_CONV), F32)], axis=0).astype(BF16)
    wr = jnp.concatenate(
        [w_group_router[0], w_expert_router[0],
         jnp.zeros((D_MODEL, LANES - N_GROUPS - N_EXPERTS), F32)], axis=1)
    wr_hi = wr.astype(BF16)
    wr_split = jnp.concatenate([wr_hi, (wr - wr_hi.astype(F32)).astype(BF16)], axis=1)
    br = jnp.concatenate([b_group_router[0], b_expert_router[0],
                          jnp.zeros((LANES - N_GROUPS - N_EXPERTS,), F32)])[None, :]
    return (norm_mix, w_main, w_gates, w_alr, w_conv[0], walpha, b_alpha, gla_norm,
            w_conv_out[0].astype(BF16), w_gla_out[0].astype(BF16), w_o[0].astype(BF16),
            norm_ffn, wr_split, br)


def _const_spec(shape):
    nd = len(shape)
    return pl.BlockSpec(shape, lambda *_: (0,) * nd, pipeline_mode=pl.Buffered(1))


def _mixer_call(x, conv_state, gla_state, weights, *, prompt):
    nb, seq, _ = x.shape
    if prompt:
        grid = (nb, seq // PROMPT_TILE)
        tok = lambda last: pl.BlockSpec((1, PROMPT_TILE, last), lambda b, i: (b, i, 0))
        conv_spec = pl.BlockSpec((1, CONV_WIDTH - 1, D_CONV), lambda b, i: (b, 0, 0))
        gla_spec = pl.BlockSpec((1, GLA_HEADS, GLA_DK, GLA_DV), lambda b, i: (b, 0, 0, 0))
        gla_in_spec = pl.BlockSpec((1, 1, 8, LANES), lambda b, i: (b, 0, 0, 0))
        scratch = [pltpu.VMEM((GLA_HEADS, GLA_DK, GLA_DV), F32),
                   pltpu.VMEM((1, PROMPT_TILE + 8, D_CONV), F32)]
        route_t_shape = (nb, 8, seq)
        route_t_spec = pl.BlockSpec((1, 8, PROMPT_TILE), lambda b, i: (b, 0, i))
    else:
        grid = (nb // SAMPLE_SEQS, 1)
        tok = lambda last: pl.BlockSpec((SAMPLE_SEQS, seq, last), lambda b, i: (b, 0, 0))
        conv_spec = pl.BlockSpec((SAMPLE_SEQS, CONV_WIDTH - 1, D_CONV), lambda b, i: (b, 0, 0))
        gla_spec = pl.BlockSpec(memory_space=pl.ANY)
        gla_in_spec = gla_spec
        state_chunk = (STATE_CHUNK, GLA_HEADS, GLA_DK, GLA_DV)
        scratch = [pltpu.VMEM((1, 8, LANES), F32),
                   pltpu.VMEM((SAMPLE_SEQS, 8 + seq, D_CONV), F32),
                   pltpu.VMEM((SAMPLE_SEQS // STATE_CHUNK,) + state_chunk, F32),
                   pltpu.VMEM((STATE_OUT_SLOTS,) + state_chunk, F32),
                   pltpu.SemaphoreType.DMA((SAMPLE_SEQS // STATE_CHUNK,)),
                   pltpu.SemaphoreType.DMA((STATE_OUT_SLOTS,))]
        route_t_shape = (nb // SAMPLE_SEQS, 8, SAMPLE_SEQS * seq)
        route_t_spec = pl.BlockSpec((1, 8, SAMPLE_SEQS * seq), lambda b, i: (b, 0, 0))
    out_shape = (jax.ShapeDtypeStruct((nb, seq, D_MODEL), F32),
                 jax.ShapeDtypeStruct((nb, seq, D_MODEL // 2), jnp.uint32),
                 jax.ShapeDtypeStruct((nb, seq, LANES), F32),
                 jax.ShapeDtypeStruct(route_t_shape, F32),
                 jax.ShapeDtypeStruct((nb, CONV_WIDTH - 1, D_CONV), F32),
                 jax.ShapeDtypeStruct((nb, GLA_HEADS, GLA_DK, GLA_DV), F32))
    return pl.pallas_call(
        functools.partial(_mixer_kernel, prompt=prompt),
        grid=grid,
        in_specs=[tok(D_MODEL), conv_spec, gla_in_spec] + [_const_spec(w.shape) for w in weights],
        out_specs=(tok(D_MODEL), tok(D_MODEL // 2), tok(LANES), route_t_spec, conv_spec, gla_spec),
        out_shape=out_shape,
        scratch_shapes=scratch,
        compiler_params=pltpu.CompilerParams(
            dimension_semantics=("arbitrary", "arbitrary"), vmem_limit_bytes=MIXER_VMEM_LIMIT),
        name="mixer_prompt" if prompt else "mixer_sample",
    )(x, conv_state, gla_state, *weights)


def _rank_kernel(eid_ref, rank_ref, cnt_ref, carry_ref):
    @pl.when(pl.program_id(0) == 0)
    def _():
        carry_ref[...] = jnp.zeros_like(carry_ref)

    n = RANK_BLOCK
    e_iota = lax.broadcasted_iota(jnp.int32, (N_EXPERTS, n), 0)
    r = lax.broadcasted_iota(jnp.int32, (n, n), 0)
    c = lax.broadcasted_iota(jnp.int32, (n, n), 1)
    before = jnp.where(r < c, 1.0, 0.0).astype(BF16)
    carry = carry_ref[...]
    for j in range(eid_ref.shape[1] // n):
        cols = slice(j * n, (j + 1) * n)
        oh0 = jnp.where(e_iota == eid_ref[0:1, cols], 1.0, 0.0)
        oh1 = jnp.where(e_iota == eid_ref[1:2, cols], 1.0, 0.0)
        cnt = oh0 + oh1
        base = carry + _dot(cnt.astype(BF16), before)
        rank0 = jnp.sum(oh0 * base, axis=0, keepdims=True)
        rank1 = jnp.sum(oh1 * base, axis=0, keepdims=True)
        rank_ref[:, cols] = jnp.concatenate([rank0, rank1], axis=0).astype(jnp.int32)
        carry = carry + jnp.sum(cnt, axis=1, keepdims=True)
    carry_ref[...] = carry
    cnt_ref[...] = jnp.broadcast_to(carry, cnt_ref.shape)


def _rank_call(eid):
    ntok = eid.shape[1]
    return pl.pallas_call(
        _rank_kernel,
        grid=(ntok // RANK_TILE,),
        in_specs=[pl.BlockSpec((2, RANK_TILE), lambda i: (0, i))],
        out_specs=(pl.BlockSpec((2, RANK_TILE), lambda i: (0, i)),
                   pl.BlockSpec((N_EXPERTS, LANES), lambda i: (0, 0))),
        out_shape=(jax.ShapeDtypeStruct((2, ntok), jnp.int32),
                   jax.ShapeDtypeStruct((N_EXPERTS, LANES), F32)),
        scratch_shapes=[pltpu.VMEM((N_EXPERTS, 1), F32)],
        compiler_params=pltpu.CompilerParams(dimension_semantics=("arbitrary",)),
        name="expert_rank",
    )(eid)


def _sc_mesh():
    return plsc.VectorSubcoreMesh(core_axis_name="c", subcore_axis_name="s")


def _sc_worker_windows(n_windows, body):
    info = plsc.get_sparse_core_info()
    n_workers = info.num_cores * info.num_subcores
    wid = lax.axis_index("s") * info.num_cores + lax.axis_index("c")

    @pl.loop(0, pl.cdiv(n_windows, n_workers))
    def _(j):
        win = wid + n_workers * j

        @pl.when(win < n_windows)
        def _():
            body(win)


def _dispatch_rows(sources, pos, n_rows):
    width, dtype = sources[0].shape[1], sources[0].dtype
    n_win = [src.shape[0] // SC_WINDOW for src in sources]
    first_win = [sum(n_win[:i]) for i in range(len(sources))]
    total_win = sum(n_win)
    pos3 = pos.reshape(2, total_win, SC_WINDOW).transpose(1, 0, 2)

    @functools.partial(pl.kernel, mesh=_sc_mesh(), name="moe_dispatch",
                       out_type=jax.ShapeDtypeStruct((n_rows, width), dtype),
                       scratch_types=[pltpu.VMEM((2, SC_WINDOW), jnp.int32),
                                      pltpu.VMEM((SC_WINDOW, width), dtype)])
    def run(*refs):
        src_hbm, (pos_hbm, out_hbm, idx_v, buf) = refs[:len(sources)], refs[len(sources):]

        def window(win):
            pltpu.sync_copy(pos_hbm.at[win], idx_v)
            for src, w0, nw in zip(src_hbm, first_win, n_win):
                @pl.when((win >= w0) & (win < w0 + nw))
                def _():
                    r0 = pl.multiple_of((win - w0) * SC_WINDOW, SC_WINDOW)
                    pltpu.sync_copy(src.at[pl.ds(r0, SC_WINDOW)], buf)

            pltpu.sync_copy(buf, out_hbm.at[idx_v.at[0]])
            pltpu.sync_copy(buf, out_hbm.at[idx_v.at[1]])

        _sc_worker_windows(total_win, window)

    return run(*sources, pos3)


def _return_rows(ys, idx):
    n_out, width = idx.shape[0], ys.shape[1]
    n_windows = n_out // SC_WINDOW

    @functools.partial(pl.kernel, mesh=_sc_mesh(), name="moe_return",
                       out_type=jax.ShapeDtypeStruct((n_out, width), ys.dtype),
                       scratch_types=[pltpu.VMEM((SC_WINDOW,), jnp.int32),
                                      pltpu.VMEM((SC_WINDOW, width), ys.dtype)])
    def run(ys_hbm, idx_hbm, out_hbm, idx_v, buf):
        def window(win):
            pltpu.sync_copy(idx_hbm.at[win], idx_v)
            pltpu.sync_copy(ys_hbm.at[idx_v], buf)
            r0 = pl.multiple_of(win * SC_WINDOW, SC_WINDOW)
            pltpu.sync_copy(buf, out_hbm.at[pl.ds(r0, SC_WINDOW)])

        _sc_worker_windows(n_windows, window)

    return run(ys, idx.reshape(n_windows, SC_WINDOW))


def _expert_kernel(first_ref, ntile_ref, nrow_ref, total_ref, xs_hbm, wg_ref, wu_ref, wd_ref, out_hbm,
                   xbuf, obuf, wg_s, wu_s, wd_s, sem_in, sem_out):
    e = pl.program_id(0)
    nt = ntile_ref[e]
    g0 = first_ref[e]
    total = total_ref[0]
    ring = EXPERT_RING

    def rows_of(g):
        return pl.ds(pl.multiple_of(g * ROW_TILE, ROW_TILE), ROW_TILE)

    def in_copy(g):
        slot = lax.rem(g, ring)
        return pltpu.make_async_copy(xs_hbm.at[rows_of(g)], xbuf.at[slot], sem_in.at[slot])

    def out_copy(g):
        slot = lax.rem(g, ring)
        return pltpu.make_async_copy(obuf.at[slot], out_hbm.at[rows_of(g)], sem_out.at[slot])

    @pl.when(e == 0)
    def _():
        for g in range(EXPERT_AHEAD):
            @pl.when(g < total)
            def _():
                in_copy(g).start()

    @pl.when(nt > 0)
    def _():
        wg_s[...] = wg_ref[0].astype(BF16)
        wu_s[...] = wu_ref[0].astype(BF16)
        wd_s[...] = wd_ref[0].astype(BF16)

    half = D_MODEL // 2
    sub = ROW_TILE // 2

    def tiles(gs):
        halves = []
        for g in gs:
            slot = lax.rem(g, ring)
            in_copy(g).wait()

            @pl.when(g + EXPERT_AHEAD < total)
            def _():
                in_copy(g + EXPERT_AHEAD).start()

            @pl.when(g >= ring)
            def _():
                out_copy(g - ring).wait()

            n_valid = nrow_ref[e] - (g - g0) * ROW_TILE
            row = lax.broadcasted_iota(jnp.int32, (ROW_TILE, 1), 0)
            lo, hi = _unpack_bf16_pairs(jnp.where(row < n_valid, xbuf[slot], jnp.uint32(0)))
            for r in (0, sub):
                xl, xh = lo[r:r + sub].astype(BF16), hi[r:r + sub].astype(BF16)
                gt = _dot(xl, wg_s[:half, :]) + _dot(xh, wg_s[half:, :])
                up = _dot(xl, wu_s[:half, :]) + _dot(xh, wu_s[half:, :])
                halves.append((slot, r, gt, up))
        for slot, r, gt, up in halves:
            act = (gt * jax.nn.sigmoid(gt)) * up
            obuf[slot, r:r + sub, :] = _pack_bf16_pairs(_dot(act.astype(BF16), wd_s[...]))
        for g in gs:
            out_copy(g).start()

    def pair(i, carry):
        tiles([g0 + 2 * i, g0 + 2 * i + 1])
        return carry

    lax.fori_loop(0, nt // 2, pair, 0)

    @pl.when(lax.rem(nt, 2) == 1)
    def _():
        tiles([g0 + nt - 1])

    @pl.when(e == pl.num_programs(0) - 1)
    def _():
        for k in range(1, ring + 1):
            @pl.when(total >= k)
            def _():
                out_copy(total - k).wait()


def _expert_call(first_tile, n_tiles_per_expert, n_rows_per_expert, total_tiles, xs,
                 w_gate, w_up, w_down):
    nrows = xs.shape[0]
    by_expert = lambda e, *_: (e, 0, 0)
    half_row = D_MODEL // 2
    return pl.pallas_call(
        _expert_kernel,
        grid_spec=pltpu.PrefetchScalarGridSpec(
            num_scalar_prefetch=4,
            grid=(N_EXPERTS,),
            in_specs=[pl.BlockSpec(memory_space=pl.ANY),
                      pl.BlockSpec((1, D_MODEL, D_EXPERT), by_expert),
                      pl.BlockSpec((1, D_MODEL, D_EXPERT), by_expert),
                      pl.BlockSpec((1, D_EXPERT, D_MODEL), by_expert)],
            out_specs=pl.BlockSpec(memory_space=pl.ANY),
            scratch_shapes=[pltpu.VMEM((EXPERT_RING, ROW_TILE, half_row), jnp.uint32),
                            pltpu.VMEM((EXPERT_RING, ROW_TILE, half_row), jnp.uint32),
                            pltpu.VMEM((D_MODEL, D_EXPERT), BF16),
                            pltpu.VMEM((D_MODEL, D_EXPERT), BF16),
                            pltpu.VMEM((D_EXPERT, D_MODEL), BF16),
                            pltpu.SemaphoreType.DMA((EXPERT_RING,)),
                            pltpu.SemaphoreType.DMA((EXPERT_RING,))]),
        out_shape=jax.ShapeDtypeStruct((nrows, half_row), jnp.uint32),
        compiler_params=pltpu.CompilerParams(
            dimension_semantics=("arbitrary",), vmem_limit_bytes=STREAM_VMEM_LIMIT),
        name="expert_mlp",
    )(first_tile, n_tiles_per_expert, n_rows_per_expert, total_tiles, xs, w_gate, w_up, w_down)


def _combine_kernel(h_ref, y2_ref, route_ref, nfin_ref, out_ref):
    route = route_ref[...]
    w1, w2 = route[:, 0:1], route[:, 1:2]
    lo1, hi1 = _unpack_bf16_pairs(y2_ref[0])
    lo2, hi2 = _unpack_bf16_pairs(y2_ref[1])
    y = h_ref[...] + jnp.concatenate([w1 * lo1 + w2 * lo2, w1 * hi1 + w2 * hi2], axis=1)
    out_ref[...] = _rmsnorm(y, nfin_ref[...])


def _combine_call(h, y2, route, nfin, first_token):
    n = h.shape[0]
    first_block = first_token // COMBINE_TILE
    return pl.pallas_call(
        _combine_kernel,
        grid=(n // COMBINE_TILE,),
        in_specs=[pl.BlockSpec((COMBINE_TILE, D_MODEL), lambda i: (i, 0)),
                  pl.BlockSpec((2, COMBINE_TILE, D_MODEL // 2), lambda i: (0, first_block + i, 0)),
                  pl.BlockSpec((COMBINE_TILE, LANES), lambda i: (i, 0)),
                  pl.BlockSpec((1, D_MODEL), lambda i: (0, 0))],
        out_specs=pl.BlockSpec((COMBINE_TILE, D_MODEL), lambda i: (i, 0)),
        out_shape=jax.ShapeDtypeStruct((n, D_MODEL), F32),
        compiler_params=pltpu.CompilerParams(
            dimension_semantics=("arbitrary",), vmem_limit_bytes=STREAM_VMEM_LIMIT),
        name="moe_combine",
    )(h, y2, route, nfin)


def _moe_rows(parts, w_gate, w_up, w_down):
    eid = jnp.concatenate([ids for _, ids in parts], axis=1).astype(jnp.int32)
    ntok = eid.shape[1]
    assert ntok % RANK_TILE == 0 and ntok % SC_WINDOW == 0
    rank, cnt = _rank_call(eid)
    counts = cnt[:, 0].astype(jnp.int32)
    padded = ((counts + ROW_TILE - 1) // ROW_TILE) * ROW_TILE
    ends = jnp.cumsum(padded)
    starts = ends - padded
    expert_iota = jnp.arange(N_EXPERTS, dtype=jnp.int32)[:, None, None]
    pos = rank + jnp.sum(jnp.where(eid[None] == expert_iota, starts[:, None, None], 0), axis=0)
    n_rows = (2 * ntok + N_EXPERTS * (ROW_TILE - 1)) // ROW_TILE * ROW_TILE
    xs = _dispatch_rows([rows for rows, _ in parts], pos, n_rows)
    ys = _expert_call(starts // ROW_TILE, padded // ROW_TILE, counts, ends[-1:] // ROW_TILE, xs,
                      w_gate, w_up, w_down)
    return _return_rows(ys, pos.reshape(-1)).reshape(2, ntok, D_MODEL // 2)


def kernel(x_prompt, x_sample, state_conv, state_gla, norm_mix, w_in, w_conv, w_alpha_up, b_alpha,
           gla_norm, w_conv_out, w_gla_out, w_o, norm_ffn, w_group_router, b_group_router,
           w_expert_router, b_expert_router, w_gate, w_up, w_down, norm_final):
    nbp, seq_p, _ = x_prompt.shape
    nbs, seq_s, _ = x_sample.shape
    assert norm_mix.shape[0] == 1, "single layer"
    assert seq_p % PROMPT_TILE == 0 and nbs % SAMPLE_SEQS == 0 and seq_s == 8

    weights = _mixer_weights(norm_mix, w_in, w_conv, w_alpha_up, b_alpha, gla_norm, w_conv_out,
                             w_gla_out, w_o, norm_ffn, w_group_router, b_group_router,
                             w_expert_router, b_expert_router)

    ntp, nts = nbp * seq_p, nbs * seq_s
    assert ntp % COMBINE_TILE == 0 and nts % COMBINE_TILE == 0
    conv0 = jnp.zeros((nbp, CONV_WIDTH - 1, D_CONV), F32)
    gla0 = jnp.zeros((nbp, 1, 8, LANES), F32)
    flat = lambda a: a.reshape(-1, a.shape[-1])
    ids_of = lambda route_t: route_t[:, 2:4, :].transpose(1, 0, 2).reshape(2, -1)
    h_p, xn_p, rt_p, rtt_p, conv_p, gla_p = _mixer_call(x_prompt, conv0, gla0, weights, prompt=True)
    h_s, xn_s, rt_s, rtt_s, conv_s, gla_s = _mixer_call(x_sample, state_conv[0], state_gla[0],
                                                        weights, prompt=False)
    y2 = _moe_rows([(flat(xn_p), ids_of(rtt_p)), (flat(xn_s), ids_of(rtt_s))],
                   w_gate[0], w_up[0], w_down[0])
    nfin = norm_final[None, :]
    y_p = _combine_call(flat(h_p), y2, flat(rt_p), nfin, 0)
    y_s = _combine_call(flat(h_s), y2, flat(rt_s), nfin, ntp)
    return (y_p.reshape(nbp, seq_p, D_MODEL), y_s.reshape(nbs, seq_s, D_MODEL),
            conv_p[None], gla_p[None], conv_s[None], gla_s[None])
```
